```python
import jax, jax.numpy as jnp
from jax import lax
import numpy as np

D_MODEL = 1024
BATCH = 8
SEQ = 8192
DEPTH = 1

HEAD_DIM = 64
MIX_WIDTH = D_MODEL
FOX_HEADS = (MIX_WIDTH // 2) // HEAD_DIM
SWA_Q_HEADS = (MIX_WIDTH // 2) // HEAD_DIM
SWA_KV_HEADS = 2
SWA_GROUP = SWA_Q_HEADS // SWA_KV_HEADS
WINDOW = 128
Q_BLOCK = 128
ROPE_THETA = 10000.0
RMS_EPS = 1e-6
FOX_W = FOX_HEADS * HEAD_DIM
SWA_W = SWA_Q_HEADS * HEAD_DIM
SWA_KV_W = SWA_KV_HEADS * HEAD_DIM
IN_SIZES = (FOX_W, FOX_W, FOX_W, FOX_HEADS, FOX_W, SWA_W, SWA_KV_W, SWA_KV_W, SWA_W)
IN_WIDTH = FOX_W * 4 + FOX_HEADS + SWA_W * 2 + SWA_KV_W * 2

kernel_name = "hybrid_fox_swa_sink_parallel_heads"


def rms_norm(x, g):
    xf = x.astype(jnp.float32)
    y = xf * lax.rsqrt(jnp.mean(xf * xf, axis=-1, keepdims=True) + RMS_EPS)
    return (y * g.astype(jnp.float32)).astype(x.dtype)


def rope(x, positions):
    half = HEAD_DIM // 2
    inv_freq = ROPE_THETA ** (-jnp.arange(half, dtype=jnp.float32) / half)
    ang = positions.astype(jnp.float32)[..., None] * inv_freq
    cos = jnp.cos(ang)[:, :, None, :]
    sin = jnp.sin(ang)[:, :, None, :]
    xf = x.astype(jnp.float32)
    x1, x2 = xf[..., :half], xf[..., half:]
    return jnp.concatenate([x1 * cos - x2 * sin, x2 * cos + x1 * sin], axis=-1).astype(x.dtype)


def forgetting_attention(q, k, v, f_logit):
    B, S, H, d = q.shape
    nb = S // Q_BLOCK
    cum = jnp.cumsum(jax.nn.log_sigmoid(f_logit.astype(jnp.float32)), axis=1)
    cum = cum.transpose(0, 2, 1)
    kh = k.transpose(0, 2, 1, 3)
    vh = v.transpose(0, 2, 1, 3)
    q_blocks = q.transpose(0, 2, 1, 3).reshape(B, H, nb, Q_BLOCK, d).transpose(2, 0, 1, 3, 4)
    c_blocks = cum.reshape(B, H, nb, Q_BLOCK).transpose(2, 0, 1, 3)
    key_pos = jnp.arange(S)
    scale = d ** -0.5

    def one_block(args):
        qb, cb, n = args
        s = jnp.einsum('bhqd,bhkd->bhqk', qb, kh, preferred_element_type=jnp.float32) * scale
        s = s + cb[..., None] - cum[:, :, None, :]
        q_pos = n * Q_BLOCK + jnp.arange(Q_BLOCK)
        causal = key_pos[None, :] <= q_pos[:, None]
        s = jnp.where(causal, s, -jnp.inf)
        p = jax.nn.softmax(s, axis=-1)
        return jnp.einsum('bhqk,bhkd->bhqd', p.astype(vh.dtype), vh)

    out = lax.map(one_block, (q_blocks, c_blocks, jnp.arange(nb)))
    return out.transpose(1, 0, 3, 2, 4).reshape(B, S, H * d)


def sliding_window_sink_attention(q, k, v, sinks):
    B, S, _, d = q.shape
    nb = S // Q_BLOCK
    qb = q.reshape(B, nb, Q_BLOCK, SWA_KV_HEADS, SWA_GROUP, d)

    def with_prev(t):
        t = t.reshape(B, nb, Q_BLOCK, SWA_KV_HEADS, d)
        prev = jnp.pad(t[:, :-1], ((0, 0), (1, 0), (0, 0), (0, 0), (0, 0)))
        return jnp.concatenate([prev, t], axis=2)

    kk, vv = with_prev(k), with_prev(v)
    s = jnp.einsum('bnqhgd,bnshd->bnhgqs', qb, kk, preferred_element_type=jnp.float32) * (d ** -0.5)
    i = jnp.arange(Q_BLOCK)[:, None]
    j = jnp.arange(2 * Q_BLOCK)[None, :]
    rel = i + Q_BLOCK - j
    band = (rel >= 0) & (rel < WINDOW)
    valid_blk = (jnp.arange(nb)[:, None, None] > 0) | (j[None] >= Q_BLOCK)
    mask = band[None] & valid_blk
    s = jnp.where(mask[None, :, None, None], s, -jnp.inf)
    sink = jnp.broadcast_to(
        sinks.astype(jnp.float32).reshape(SWA_KV_HEADS, SWA_GROUP)[None, None, :, :, None, None],
        s.shape[:-1] + (1,))
    p = jax.nn.softmax(jnp.concatenate([s, sink], axis=-1), axis=-1)[..., :-1]
    out = jnp.einsum('bnhgqs,bnshd->bnqhgd', p.astype(vv.dtype), vv)
    return out.reshape(B, S, SWA_Q_HEADS * d)


def _fwd_setup_inputs(seed: int = 0) -> dict:
    key = jax.random.key(seed)
    ks = jax.random.split(key, 12)
    x = jax.random.normal(ks[0], (BATCH, SEQ, D_MODEL), jnp.float32)
    c = jax.random.normal(ks[1], (BATCH, D_MODEL), jnp.float32)
    positions = jnp.broadcast_to(jnp.arange(SEQ, dtype=jnp.int32)[None, :], (BATCH, SEQ))
    w_ada = jax.random.normal(ks[2], (DEPTH, D_MODEL, 3 * D_MODEL), jnp.float32) * (0.5 * D_MODEL ** -0.5)
    b_ada = jax.random.normal(ks[3], (DEPTH, 3 * D_MODEL), jnp.float32) * 0.02
    g_pre = 1.0 + 0.1 * jax.random.normal(ks[4], (DEPTH, D_MODEL), jnp.float32)
    w_in = jax.random.normal(ks[5], (DEPTH, D_MODEL, IN_WIDTH), jnp.float32) * D_MODEL ** -0.5
    b_fgate = jax.random.uniform(ks[6], (DEPTH, FOX_HEADS), jnp.float32, 1.0, 4.0)
    sinks = jax.random.normal(ks[7], (DEPTH, SWA_Q_HEADS), jnp.float32)
    w_out = jax.random.normal(ks[8], (DEPTH, MIX_WIDTH, D_MODEL), jnp.float32) * MIX_WIDTH ** -0.5
    g_post = 1.0 + 0.1 * jax.random.normal(ks[9], (DEPTH, D_MODEL), jnp.float32)
    return {"x": x, "c": c, "positions": positions, "w_ada": w_ada, "b_ada": b_ada,
            "g_pre": g_pre, "w_in": w_in, "b_fgate": b_fgate, "sinks": sinks,
            "w_out": w_out, "g_post": g_post}


def _fwd_reference(x, c, positions, w_ada, b_ada, g_pre, w_in, b_fgate, sinks, w_out, g_post):
    B, S, _ = x.shape
    split_points = [int(v) for v in np.cumsum(IN_SIZES)[:-1]]
    for l in range(DEPTH):
        mod = jax.nn.silu(c) @ w_ada[l] + b_ada[l]
        shift, scale, gate = jnp.split(mod, 3, axis=-1)
        h = rms_norm(x, g_pre[l]) * (1.0 + scale[:, None, :]) + shift[:, None, :]
        proj = h @ w_in[l]
        qa, ka, va, fa, za, qb, kb, vb, zb = jnp.split(proj, split_points, axis=-1)
        oa = forgetting_attention(
            qa.reshape(B, S, FOX_HEADS, HEAD_DIM),
            ka.reshape(B, S, FOX_HEADS, HEAD_DIM),
            va.reshape(B, S, FOX_HEADS, HEAD_DIM),
            fa + b_fgate[l])
        oa = oa * jax.nn.silu(za)
        qb = rope(qb.reshape(B, S, SWA_Q_HEADS, HEAD_DIM), positions)
        kb = rope(kb.reshape(B, S, SWA_KV_HEADS, HEAD_DIM), positions)
        ob = sliding_window_sink_attention(qb, kb, vb.reshape(B, S, SWA_KV_HEADS, HEAD_DIM), sinks[l])
        ob = ob * jax.nn.silu(zb)
        y = jnp.concatenate([oa, ob], axis=-1) @ w_out[l]
        x = x + gate[:, None, :] * rms_norm(y, g_post[l])
    return x


import jax as _jax
import jax.numpy as _jnp

TWIN_FORMAT = 'train_step'
FWD_PARAMS = ['x', 'c', 'positions', 'w_ada', 'b_ada', 'g_pre', 'w_in', 'b_fgate', 'sinks', 'w_out', 'g_post']
TWIN_WEIGHTS = ['w_ada', 'b_ada', 'g_pre', 'w_in', 'b_fgate', 'sinks', 'w_out', 'g_post']
TWIN_DIFF_INPUT = 'x'
TWIN_INPUTS = ['x', 'c', 'positions', 'w_ada', 'b_ada', 'g_pre', 'w_in', 'b_fgate', 'sinks', 'w_out', 'g_post', 'loss_target', 'm_w_ada', 'm_b_ada', 'm_g_pre', 'm_w_in', 'm_b_fgate', 'm_sinks', 'm_w_out', 'm_g_post', 'v_w_ada', 'v_b_ada', 'v_g_pre', 'v_w_in', 'v_b_fgate', 'v_sinks', 'v_w_out', 'v_g_post']
TWIN_OUTPUTS = ['loss', 'grad_x', 'grad_w_ada', 'grad_b_ada', 'grad_g_pre', 'grad_w_in', 'grad_b_fgate', 'grad_sinks', 'grad_w_out', 'grad_g_post', 'delta_w_ada', 'delta_b_ada', 'delta_g_pre', 'delta_w_in', 'delta_b_fgate', 'delta_sinks', 'delta_w_out', 'delta_g_post', 'new_m_w_ada', 'new_m_b_ada', 'new_m_g_pre', 'new_m_w_in', 'new_m_b_fgate', 'new_m_sinks', 'new_m_w_out', 'new_m_g_post', 'new_v_w_ada', 'new_v_b_ada', 'new_v_g_pre', 'new_v_w_in', 'new_v_b_fgate', 'new_v_sinks', 'new_v_w_out', 'new_v_g_post']
TWIN_LEAF_KINDS = {'loss': 'loss', 'grad_x': 'grad_x', 'grad_w_ada': 'grad_w', 'grad_b_ada': 'grad_w', 'grad_g_pre': 'grad_w', 'grad_w_in': 'grad_w', 'grad_b_fgate': 'grad_w', 'grad_sinks': 'grad_w', 'grad_w_out': 'grad_w', 'grad_g_post': 'grad_w', 'delta_w_ada': 'delta_w', 'delta_b_ada': 'delta_w', 'delta_g_pre': 'delta_w', 'delta_w_in': 'delta_w', 'delta_b_fgate': 'delta_w', 'delta_sinks': 'delta_w', 'delta_w_out': 'delta_w', 'delta_g_post': 'delta_w', 'new_m_w_ada': 'new_m', 'new_m_b_ada': 'new_m', 'new_m_g_pre': 'new_m', 'new_m_w_in': 'new_m', 'new_m_b_fgate': 'new_m', 'new_m_sinks': 'new_m', 'new_m_w_out': 'new_m', 'new_m_g_post': 'new_m', 'new_v_w_ada': 'new_v', 'new_v_b_ada': 'new_v', 'new_v_g_pre': 'new_v', 'new_v_w_in': 'new_v', 'new_v_b_fgate': 'new_v', 'new_v_sinks': 'new_v', 'new_v_w_out': 'new_v', 'new_v_g_post': 'new_v'}


def _forward(args):
    return _fwd_reference(*[args[k] for k in FWD_PARAMS])


def _output_shape():
    def fwd():
        inp = _fwd_setup_inputs(0)
        return _fwd_reference(*[inp[k] for k in FWD_PARAMS])
    out = _jax.eval_shape(fwd)
    return out.shape, out.dtype

N_MICROBATCH = 1
ADAM_LR = 0.001
ADAM_B1 = 0.9
ADAM_B2 = 0.999
ADAM_EPS = 1e-08
ADAM_WD = 0.01
ADAM_STEP = 10
PER_EXAMPLE_BATCH_AXIS = {'x': 0, 'c': 0, 'positions': 0, 'loss_target': 0}
SHARED_INPUTS = []
_WEIGHT_DTYPES = {'w_ada': _jnp.float32, 'b_ada': _jnp.float32, 'g_pre': _jnp.float32, 'w_in': _jnp.float32, 'b_fgate': _jnp.float32, 'sinks': _jnp.float32, 'w_out': _jnp.float32, 'g_post': _jnp.float32}
MOMENT_SCALE = {'w_ada': 2.538109e+00, 'b_ada': 5.260291e+00, 'g_pre': 1.972535e-01, 'w_in': 1.666485e-01, 'b_fgate': 4.723594e-01, 'sinks': 2.525432e-02, 'w_out': 2.544974e-01, 'g_post': 6.789607e+00}


def _to_microbatches(a, axis):
    t = _jnp.moveaxis(a, axis, 0)
    t = t.reshape((N_MICROBATCH, t.shape[0] // N_MICROBATCH) + t.shape[1:])
    return _jnp.moveaxis(t, 1, axis + 1)


def setup_inputs(seed: int = 0) -> dict:
    inp = _fwd_setup_inputs(seed)
    key = _jax.random.fold_in(_jax.random.key(seed), 7919)
    shape, _ = _output_shape()
    out = dict(inp)
    out["loss_target"] = _jax.random.normal(_jax.random.fold_in(key, 0), shape, _jnp.float32)
    for i, name in enumerate(TWIN_WEIGHTS):
        w = inp[name].astype(_jnp.float32)
        if MOMENT_SCALE is None:
            s = _jnp.sqrt(_jnp.mean(_jnp.square(w)) + 1e-30)
        else:
            s = MOMENT_SCALE[name]
        km, kv = _jax.random.split(_jax.random.fold_in(key, i + 1))
        out[name] = w
        out["m_" + name] = s * _jax.random.normal(km, w.shape, _jnp.float32)
        out["v_" + name] = (s * s) * _jax.random.uniform(kv, w.shape, _jnp.float32, 0.5, 1.5)
    if N_MICROBATCH > 1:
        for name, axis in PER_EXAMPLE_BATCH_AXIS.items():
            out[name] = _to_microbatches(out[name], axis)
    return {'x': out['x'], 'c': out['c'], 'positions': out['positions'], 'w_ada': out['w_ada'], 'b_ada': out['b_ada'], 'g_pre': out['g_pre'], 'w_in': out['w_in'], 'b_fgate': out['b_fgate'], 'sinks': out['sinks'], 'w_out': out['w_out'], 'g_post': out['g_post'], 'loss_target': out['loss_target'], 'm_w_ada': out['m_w_ada'], 'm_b_ada': out['m_b_ada'], 'm_g_pre': out['m_g_pre'], 'm_w_in': out['m_w_in'], 'm_b_fgate': out['m_b_fgate'], 'm_sinks': out['m_sinks'], 'm_w_out': out['m_w_out'], 'm_g_post': out['m_g_post'], 'v_w_ada': out['v_w_ada'], 'v_b_ada': out['v_b_ada'], 'v_g_pre': out['v_g_pre'], 'v_w_in': out['v_w_in'], 'v_b_fgate': out['v_b_fgate'], 'v_sinks': out['v_sinks'], 'v_w_out': out['v_w_out'], 'v_g_post': out['v_g_post']}


def _loss(weights, diff, rest, loss_target):
    with _jax.named_scope("forward"):
        args = {**rest, TWIN_DIFF_INPUT: diff, **{k: w.astype(_WEIGHT_DTYPES[k]) for k, w in weights.items()}}
        y = _forward(args)
    with _jax.named_scope("loss_head"):
        err = _jnp.square(y.astype(_jnp.float32) - loss_target)
        return 0.5 * _jnp.sum(_jnp.mean(err, axis=-1)) if err.ndim else 0.5 * err


def _adamw(w, g, m, v):
    m = ADAM_B1 * m + (1.0 - ADAM_B1) * g
    v = ADAM_B2 * v + (1.0 - ADAM_B2) * _jnp.square(g)
    m_hat = m / (1.0 - ADAM_B1 ** ADAM_STEP)
    v_hat = v / (1.0 - ADAM_B2 ** ADAM_STEP)
    delta = -ADAM_LR * (m_hat / (_jnp.sqrt(v_hat) + ADAM_EPS) + ADAM_WD * w)
    return delta, m, v


def reference(x, c, positions, w_ada, b_ada, g_pre, w_in, b_fgate, sinks, w_out, g_post, loss_target, m_w_ada, m_b_ada, m_g_pre, m_w_in, m_b_fgate, m_sinks, m_w_out, m_g_post, v_w_ada, v_b_ada, v_g_pre, v_w_in, v_b_fgate, v_sinks, v_w_out, v_g_post):
    given = dict(x=x, c=c, positions=positions, w_ada=w_ada, b_ada=b_ada, g_pre=g_pre, w_in=w_in, b_fgate=b_fgate, sinks=sinks, w_out=w_out, g_post=g_post, loss_target=loss_target, m_w_ada=m_w_ada, m_b_ada=m_b_ada, m_g_pre=m_g_pre, m_w_in=m_w_in, m_b_fgate=m_b_fgate, m_sinks=m_sinks, m_w_out=m_w_out, m_g_post=m_g_post, v_w_ada=v_w_ada, v_b_ada=v_b_ada, v_g_pre=v_g_pre, v_w_in=v_w_in, v_b_fgate=v_b_fgate, v_sinks=v_sinks, v_w_out=v_w_out, v_g_post=v_g_post)
    weights = {n: given[n] for n in TWIN_WEIGHTS}
    shared = {n: given[n] for n in SHARED_INPUTS}
    per_example = {n: given[n] for n in ['x', 'c', 'positions']}
    grad_fn = _jax.value_and_grad(_loss, argnums=(0, 1))

    def one_microbatch(ex, loss_target):
        ex = dict(ex)
        diff = ex.pop(TWIN_DIFF_INPUT)
        return grad_fn(weights, diff, {**shared, **ex}, loss_target)

    if N_MICROBATCH == 1:
        loss, (grad_w, grad_x) = one_microbatch(per_example, given["loss_target"])
    else:
        def body(carry, xs):
            loss_sum, grad_sum = carry
            l_k, (gw_k, gx_k) = one_microbatch(xs[0], xs[1])
            with _jax.named_scope("update"):
                return (loss_sum + l_k, _jax.tree.map(_jnp.add, grad_sum, gw_k)), gx_k

        init = (_jnp.zeros((), _jnp.float32), _jax.tree.map(_jnp.zeros_like, weights))
        (loss, grad_w), grad_x = _jax.lax.scan(body, init, (per_example, given["loss_target"]))
    with _jax.named_scope("update"):
        delta_w, new_m, new_v = {}, {}, {}
        for n in TWIN_WEIGHTS:
            delta_w[n], new_m[n], new_v[n] = _adamw(weights[n], grad_w[n], given["m_" + n], given["v_" + n])
    return (loss, grad_x, *[grad_w[n] for n in TWIN_WEIGHTS], *[delta_w[n] for n in TWIN_WEIGHTS],
            *[new_m[n] for n in TWIN_WEIGHTS], *[new_v[n] for n in TWIN_WEIGHTS])
```

```python
import functools

import jax
import jax.numpy as jnp
from jax import lax
from jax.experimental import pallas as pl
from jax.experimental.pallas import tpu as pltpu

_INTERPRET = False

D_MODEL = 1024
HEAD_DIM = 64
HALF = HEAD_DIM // 2
AUG_DIM = 128
FOX_HEADS = 8
FOX_W = 512
SWA_W = 512
SWA_KV_HEADS = 2
SWA_GROUP = 4
SWA_KV_W = 128
WINDOW = 128
ROPE_THETA = 10000.0
RMS_EPS = 1e-6
IN_WIDTH = 3336
N_CHIPS = 4
N_DEV = 8
W_IN_SHARD = IN_WIDTH // N_CHIPS
W_IN_SHARD_PAD = 896
W_ADA_SHARD = 3 * D_MODEL // N_CHIPS
W_OUT_SHARD = D_MODEL // N_CHIPS
LANES = 128

_SRC = dict(qa=0, ka=512, va=1024, fa=1536, za=1544, qb=2056, kb=2568, vb=2696, zb=2824)
C_QA, C_KA, C_VA, C_ZA, C_QB, C_ZB, C_KB, C_VB, C_F = 0, 512, 1024, 1536, 2048, 2560, 3072, 3200, 3328
WP = 3456

ADAM_LR = 0.001
ADAM_B1 = 0.9
ADAM_B2 = 0.999
ADAM_EPS = 1e-08
ADAM_WD = 0.01
ADAM_STEP = 10

VMEM_LIMIT = 56 * 1024 * 1024
NEG = -1e30
MESH = pl.DeviceIdType.MESH
BF = jnp.bfloat16
F32 = jnp.float32

P_DMOD, P_GPRE, P_GPOST, P_BF, P_SINK, P_LOSS, P_LEN = 0, 3072, 4096, 5120, 5248, 5376, 5504


def _call(body, **kw):
    return pl.pallas_call(body, interpret=_INTERPRET, **kw)


def _params(sem=None, **kw):
    return pltpu.CompilerParams(dimension_semantics=sem, vmem_limit_bytes=VMEM_LIMIT, **kw)


def _full(shape):
    zeros = (0,) * len(shape)
    return pl.BlockSpec(shape, lambda *_: zeros)


def _dot(a, b):
    return jnp.dot(a, b, preferred_element_type=F32)


def _dot_nt(a, b):
    return lax.dot_general(a, b, (((1,), (1,)), ((), ())), preferred_element_type=F32)


def _dot_tn(a, b):
    return lax.dot_general(a, b, (((0,), (0,)), ((), ())), preferred_element_type=F32)


def _sigmoid(z):
    return 1.0 / (1.0 + jnp.exp(-z))


def _rope_partner(t):
    w = t.shape[-1]
    lane = lax.broadcasted_iota(jnp.int32, t.shape, t.ndim - 1)
    return jnp.where((lane & (HEAD_DIM - 1)) < HALF, pltpu.roll(t, w - HALF, t.ndim - 1), pltpu.roll(t, HALF, t.ndim - 1))


def _allgather_devices(v, name):
    r, cdim = v.shape
    masks = [(dx, dy, dc) for dx in (0, 1) for dy in (0, 1) for dc in (0, 1)][1:]

    def body(v_ref, out_ref, send_sems, recv_sems):
        x, y, c = lax.axis_index("x"), lax.axis_index("y"), lax.axis_index("c")
        me = 4 * x + 2 * y + c
        out_ref[me] = v_ref[...]
        copies = []
        for k, (dx, dy, dc) in enumerate(masks):
            cp = pltpu.make_async_remote_copy(
                src_ref=v_ref, dst_ref=out_ref.at[me], send_sem=send_sems.at[k], recv_sem=recv_sems.at[k],
                device_id=(x ^ dx, y ^ dy, c ^ dc), device_id_type=MESH)
            cp.start()
            copies.append(cp)
        for k, (dx, dy, dc) in enumerate(masks):
            peer = 4 * (x ^ dx) + 2 * (y ^ dy) + (c ^ dc)
            pltpu.make_async_remote_copy(
                src_ref=v_ref, dst_ref=out_ref.at[peer], send_sem=send_sems.at[k], recv_sem=recv_sems.at[k],
                device_id=(x ^ dx, y ^ dy, c ^ dc), device_id_type=MESH).wait_recv()
        for cp in copies:
            cp.wait_send()

    return _call(
        body, name=name, out_shape=jax.ShapeDtypeStruct((N_DEV, r, cdim), v.dtype),
        in_specs=[pl.BlockSpec(memory_space=pltpu.VMEM)], out_specs=pl.BlockSpec(memory_space=pltpu.VMEM),
        scratch_shapes=[pltpu.SemaphoreType.DMA((7,)), pltpu.SemaphoreType.DMA((7,))],
        compiler_params=pltpu.CompilerParams(has_side_effects=True),
    )(v)


def _allgather_chips(v, name):
    r, cdim = v.shape
    masks = [(1, 0), (0, 1), (1, 1)]

    def body(v_ref, out_ref, send_sems, recv_sems, local_sem):
        x, y, c = lax.axis_index("x"), lax.axis_index("y"), lax.axis_index("c")
        me = 2 * x + y
        mine = pltpu.make_async_copy(v_ref, out_ref.at[me], local_sem)
        mine.start()
        copies = []
        for k, (dx, dy) in enumerate(masks):
            cp = pltpu.make_async_remote_copy(
                src_ref=v_ref, dst_ref=out_ref.at[me], send_sem=send_sems.at[k], recv_sem=recv_sems.at[k],
                device_id=(x ^ dx, y ^ dy, c), device_id_type=MESH)
            cp.start()
            copies.append(cp)
        for k, (dx, dy) in enumerate(masks):
            peer = 2 * (x ^ dx) + (y ^ dy)
            pltpu.make_async_remote_copy(
                src_ref=v_ref, dst_ref=out_ref.at[peer], send_sem=send_sems.at[k], recv_sem=recv_sems.at[k],
                device_id=(x ^ dx, y ^ dy, c), device_id_type=MESH).wait_recv()
        for cp in copies:
            cp.wait_send()
        mine.wait()

    return _call(
        body, name=name, out_shape=jax.ShapeDtypeStruct((N_CHIPS, r, cdim), v.dtype),
        in_specs=[pl.BlockSpec(memory_space=pl.ANY)], out_specs=pl.BlockSpec(memory_space=pl.ANY),
        scratch_shapes=[pltpu.SemaphoreType.DMA((3,)), pltpu.SemaphoreType.DMA((3,)), pltpu.SemaphoreType.DMA],
        compiler_params=pltpu.CompilerParams(has_side_effects=True),
    )(v)


def _swap_sibling(v, name):
    def body(v_ref, out_ref, send_sem, recv_sem):
        x, y, c = lax.axis_index("x"), lax.axis_index("y"), lax.axis_index("c")
        cp = pltpu.make_async_remote_copy(
            src_ref=v_ref, dst_ref=out_ref, send_sem=send_sem, recv_sem=recv_sem,
            device_id=(x, y, 1 - c), device_id_type=MESH)
        cp.start()
        cp.wait()

    return _call(
        body, name=name, out_shape=jax.ShapeDtypeStruct(v.shape, v.dtype),
        in_specs=[pl.BlockSpec(memory_space=pl.ANY)], out_specs=pl.BlockSpec(memory_space=pl.ANY),
        scratch_shapes=[pltpu.SemaphoreType.DMA, pltpu.SemaphoreType.DMA],
        compiler_params=pltpu.CompilerParams(has_side_effects=True),
    )(v)


def _exchange_chips(v, name):
    _, r, cdim = v.shape
    masks = [(1, 0), (0, 1), (1, 1)]

    def body(v_ref, out_ref, send_sems, recv_sems, local_sem):
        x, y, c = lax.axis_index("x"), lax.axis_index("y"), lax.axis_index("c")
        me = 2 * x + y
        mine = pltpu.make_async_copy(v_ref.at[me], out_ref.at[me], local_sem)
        mine.start()
        copies = []
        for k, (dx, dy) in enumerate(masks):
            peer = 2 * (x ^ dx) + (y ^ dy)
            cp = pltpu.make_async_remote_copy(
                src_ref=v_ref.at[peer], dst_ref=out_ref.at[me], send_sem=send_sems.at[k], recv_sem=recv_sems.at[k],
                device_id=(x ^ dx, y ^ dy, c), device_id_type=MESH)
            cp.start()
            copies.append(cp)
        for k, (dx, dy) in enumerate(masks):
            peer = 2 * (x ^ dx) + (y ^ dy)
            pltpu.make_async_remote_copy(
                src_ref=v_ref.at[me], dst_ref=out_ref.at[peer], send_sem=send_sems.at[k], recv_sem=recv_sems.at[k],
                device_id=(x ^ dx, y ^ dy, c), device_id_type=MESH).wait_recv()
        for cp in copies:
            cp.wait_send()
        mine.wait()

    return _call(
        body, name=name, out_shape=jax.ShapeDtypeStruct(v.shape, v.dtype),
        in_specs=[pl.BlockSpec(memory_space=pl.ANY)], out_specs=pl.BlockSpec(memory_space=pl.ANY),
        scratch_shapes=[pltpu.SemaphoreType.DMA((3,)), pltpu.SemaphoreType.DMA((3,)), pltpu.SemaphoreType.DMA],
        compiler_params=pltpu.CompilerParams(has_side_effects=True),
    )(v)


def _ada_shard(c_all, w_ada_shard):
    def body(c_ref, w_ref, a_ref, mod_ref):
        cv = c_ref[...]
        a = cv * _sigmoid(cv)
        a_ref[...] = a
        mod_ref[...] = _dot(a.astype(BF), w_ref[...].astype(BF))

    return _call(
        body, name="ada_shard",
        out_shape=(jax.ShapeDtypeStruct((N_DEV, D_MODEL), F32), jax.ShapeDtypeStruct((N_DEV, W_ADA_SHARD), F32)),
        compiler_params=_params(),
    )(c_all, w_ada_shard)


def _grad_w_ada(a_t, dm_shard):
    def body(a_ref, dm_ref, out_ref):
        acc = jnp.zeros((D_MODEL, W_ADA_SHARD), F32)
        for b in range(N_DEV):
            acc = acc + a_ref[:, b:b + 1] * dm_ref[b:b + 1, :]
        out_ref[...] = acc

    return _call(body, name="grad_w_ada", out_shape=jax.ShapeDtypeStruct((D_MODEL, W_ADA_SHARD), F32),
                 compiler_params=_params())(a_t, dm_shard)


def _sum_devices(parts):
    n = parts.shape[-1]

    def body(p_ref, out_ref):
        acc = p_ref[0]
        for b in range(1, N_DEV):
            acc = acc + p_ref[b]
        out_ref[...] = acc

    return _call(body, name="sum_devices", out_shape=jax.ShapeDtypeStruct((1, n), F32), compiler_params=_params())(parts)


def _add(a, b, name):
    r, cdim = a.shape
    tr = min(r, 256)

    def body(a_ref, b_ref, o_ref):
        o_ref[...] = a_ref[...] + b_ref[...]

    spec = pl.BlockSpec((tr, cdim), lambda i: (i, 0))
    return _call(body, name=name, out_shape=jax.ShapeDtypeStruct(a.shape, F32), grid=(r // tr,),
                 in_specs=[spec, spec], out_specs=spec, compiler_params=_params(("parallel",)))(a, b)


def _sum_chips(parts, name):
    _, r, cdim = parts.shape
    tr = min(r, 128)

    def body(p_ref, o_ref):
        o_ref[...] = ((p_ref[0] + p_ref[1]) + p_ref[2]) + p_ref[3]

    return _call(body, name=name, out_shape=jax.ShapeDtypeStruct((r, cdim), F32), grid=(r // tr,),
                 in_specs=[pl.BlockSpec((N_CHIPS, tr, cdim), lambda i: (0, i, 0))],
                 out_specs=pl.BlockSpec((tr, cdim), lambda i: (i, 0)), compiler_params=_params(("parallel",)))(parts)


def _adamw(w, g, m, v, name):
    r, cdim = w.shape
    tr = r if r <= 256 else 256
    c1 = 1.0 / (1.0 - ADAM_B1 ** ADAM_STEP)
    c2 = 1.0 / (1.0 - ADAM_B2 ** ADAM_STEP)

    def body(w_ref, g_ref, m_ref, v_ref, d_ref, nm_ref, nv_ref):
        gv = g_ref[...]
        nm = ADAM_B1 * m_ref[...] + (1.0 - ADAM_B1) * gv
        nv = ADAM_B2 * v_ref[...] + (1.0 - ADAM_B2) * (gv * gv)
        m_hat = nm * c1
        v_hat = nv * c2
        d_ref[...] = -ADAM_LR * (m_hat / (jnp.sqrt(v_hat) + ADAM_EPS) + ADAM_WD * w_ref[...])
        nm_ref[...] = nm
        nv_ref[...] = nv

    spec = pl.BlockSpec((tr, cdim), lambda i: (i, 0))
    shp = jax.ShapeDtypeStruct(w.shape, F32)
    return _call(body, name=name, out_shape=(shp, shp, shp), grid=(r // tr,), in_specs=[spec] * 4,
                 out_specs=(spec, spec, spec), compiler_params=_params(("parallel",)))(w, g, m, v)


def _head_of_row(r, nc):
    assert nc & (nc - 1) == 0
    return lax.shift_right_logical(r, nc.bit_length() - 1)


def _chunk_mats(rows, nc, reverse):
    ri = lax.broadcasted_iota(jnp.int32, (rows, rows), 0)
    ci = lax.broadcasted_iota(jnp.int32, (rows, rows), 1)
    same = _head_of_row(ri, nc) == _head_of_row(ci, nc)
    between = jnp.where(same & ((ci > ri) if reverse else (ci < ri)), 1.0, 0.0).astype(F32)
    li = lax.broadcasted_iota(jnp.int32, (LANES, LANES), 0)
    lj = lax.broadcasted_iota(jnp.int32, (LANES, LANES), 1)
    within = jnp.where((li >= lj) if reverse else (li <= lj), 1.0, 0.0).astype(F32)
    return between, within


def _dot_hi(a, b):
    return jnp.dot(a, b, preferred_element_type=F32, precision=lax.Precision.HIGHEST)


def _scan_rows(t, nc, reverse):
    between, within = _chunk_mats(t.shape[0], nc, reverse)
    inner = _dot_hi(t, within)
    tot = jnp.sum(t, axis=1, keepdims=True)
    return inner + _dot_hi(between, jnp.broadcast_to(tot, t.shape))


def _log_forget_cumsum(f_rows, bias_rows, nc):
    def body(f_ref, b_ref, cum_ref):
        z = f_ref[...] + b_ref[...]
        lf = jnp.minimum(z, 0.0) - jnp.log(1.0 + jnp.exp(-jnp.abs(z)))
        cum_ref[...] = _scan_rows(lf, nc, False)

    return _call(body, name="forget_cumsum", out_shape=jax.ShapeDtypeStruct(f_rows.shape, F32),
                 compiler_params=_params())(f_rows, bias_rows)


def _log_forget_cumsum_bwd(dcum_rows, f_rows, bias_rows, nc):
    rows = f_rows.shape[0]

    def body(d_ref, f_ref, b_ref, df_ref, db_ref):
        dlf = _scan_rows(d_ref[...], nc, True)
        z = f_ref[...] + b_ref[...]
        df = dlf * _sigmoid(-z)
        df_ref[...] = df
        hi = lax.broadcasted_iota(jnp.int32, (FOX_HEADS, rows), 0)
        ri = lax.broadcasted_iota(jnp.int32, (FOX_HEADS, rows), 1)
        sel = jnp.where(_head_of_row(ri, nc) == hi, 1.0, 0.0).astype(F32)
        db_ref[...] = jnp.sum(_dot_hi(sel, df), axis=1, keepdims=True)

    return _call(body, name="forget_cumsum_bwd",
                 out_shape=(jax.ShapeDtypeStruct(f_rows.shape, F32), jax.ShapeDtypeStruct((FOX_HEADS, 1), F32)),
                 compiler_params=_params())(dcum_rows, f_rows, bias_rows)


def _rms_hat(xv):
    rstd = lax.rsqrt(jnp.mean(xv * xv, axis=-1, keepdims=True) + RMS_EPS)
    return xv * rstd, rstd


def _in_proj(x, g_pre, scale1p, shift, w_al, cos_t, sin_t, tm):
    s = x.shape[0]

    def body(x_ref, g_ref, sc_ref, sh_ref, w_ref, cos_ref, sin_ref,
             h_ref, qa_ref, ka_ref, va_ref, za_ref, zb_ref, qb_ref, kb_ref, vb_ref, f_ref):
        xhat, _ = _rms_hat(x_ref[...])
        h = (xhat * g_ref[...]) * sc_ref[...] + sh_ref[...]
        hb = h.astype(BF)
        h_ref[...] = hb

        def sec(c0, width):
            return _dot(hb, w_ref[:, c0:c0 + width])

        qa = sec(C_QA, FOX_W) * (HEAD_DIM ** -0.5)
        ka = sec(C_KA, FOX_W)
        va = sec(C_VA, FOX_W)
        one_hot = jnp.where(lax.broadcasted_iota(jnp.int32, (tm, HEAD_DIM), 1) == 0, 1.0, 0.0).astype(BF)
        for hd in range(FOX_HEADS):
            sl = slice(hd * HEAD_DIM, (hd + 1) * HEAD_DIM)
            qa_ref[hd] = jnp.concatenate([qa[:, sl].astype(BF), one_hot], axis=1)
            ka_ref[hd] = jnp.concatenate([ka[:, sl].astype(BF), one_hot], axis=1)
            va_ref[hd] = va[:, sl].astype(BF)
        za_ref[...] = sec(C_ZA, FOX_W)
        zb_ref[...] = sec(C_ZB, SWA_W)
        f_ref[...] = sec(C_F, LANES)
        cos2, sin2 = cos_ref[...], sin_ref[...]
        cos8 = jnp.concatenate([cos2] * 4, axis=1)
        sin8 = jnp.concatenate([sin2] * 4, axis=1)
        qb = sec(C_QB, SWA_W)
        qb = (qb * cos8 + _rope_partner(qb) * sin8) * (HEAD_DIM ** -0.5)
        qb_ref[...] = qb.astype(BF)
        kb = sec(C_KB, SWA_KV_W)
        kb = kb * cos2 + _rope_partner(kb) * sin2
        vb = sec(C_VB, SWA_KV_W)
        for hd in range(SWA_KV_HEADS):
            sl = slice(hd * HEAD_DIM, (hd + 1) * HEAD_DIM)
            kb_ref[hd] = kb[:, sl].astype(BF)
            vb_ref[hd] = vb[:, sl].astype(BF)

    row = lambda w: pl.BlockSpec((tm, w), lambda i: (i, 0))
    heads = lambda n, w=HEAD_DIM: pl.BlockSpec((n, tm, w), lambda i: (0, i, 0))
    vec = _full((1, D_MODEL))
    out_shape = (
        jax.ShapeDtypeStruct((s, D_MODEL), BF),
        jax.ShapeDtypeStruct((FOX_HEADS, s, AUG_DIM), BF), jax.ShapeDtypeStruct((FOX_HEADS, s, AUG_DIM), BF),
        jax.ShapeDtypeStruct((FOX_HEADS, s, HEAD_DIM), BF),
        jax.ShapeDtypeStruct((s, FOX_W), F32), jax.ShapeDtypeStruct((s, SWA_W), F32),
        jax.ShapeDtypeStruct((s, SWA_W), BF),
        jax.ShapeDtypeStruct((SWA_KV_HEADS, s, HEAD_DIM), BF), jax.ShapeDtypeStruct((SWA_KV_HEADS, s, HEAD_DIM), BF),
        jax.ShapeDtypeStruct((s, LANES), F32),
    )
    return _call(
        body, name="in_proj", out_shape=out_shape, grid=(s // tm,),
        in_specs=[row(D_MODEL), vec, vec, vec, _full((D_MODEL, WP)), row(LANES), row(LANES)],
        out_specs=(row(D_MODEL), heads(FOX_HEADS, AUG_DIM), heads(FOX_HEADS, AUG_DIM), heads(FOX_HEADS), row(FOX_W), row(SWA_W),
                   row(SWA_W), heads(SWA_KV_HEADS), heads(SWA_KV_HEADS), row(LANES)),
        compiler_params=_params(("parallel",)),
    )(x, g_pre, scale1p, shift, w_al, cos_t, sin_t)


def _causal_mask(i, j, blk):
    qpos = i * blk + lax.broadcasted_iota(jnp.int32, (blk, blk), 0)
    kpos = j * blk + lax.broadcasted_iota(jnp.int32, (blk, blk), 1)
    return kpos <= qpos


def _fox_fwd(q, k, v, cum_col, cum_row, blk):
    nh, s, _ = q.shape
    nb = s // blk

    def body(q_ref, k_ref, v_ref, cq_ref, ck_ref, o_ref, lse_ref, m_scr, l_scr, acc_scr):
        i, j = pl.program_id(1), pl.program_id(2)

        @pl.when(j == 0)
        def _():
            m_scr[...] = jnp.full(m_scr.shape, NEG, F32)
            l_scr[...] = jnp.zeros(l_scr.shape, F32)
            acc_scr[...] = jnp.zeros(acc_scr.shape, F32)

        def step(masked):
            sc = _dot_nt(q_ref[0], k_ref[0]) + (cq_ref[0] - ck_ref[0])
            if masked:
                sc = jnp.where(_causal_mask(i, j, blk), sc, NEG)
            m_prev = m_scr[...]
            m_new = jnp.maximum(m_prev, jnp.max(sc, axis=1, keepdims=True))
            alpha = jnp.exp(m_prev - m_new)
            p = jnp.exp(sc - m_new)
            l_scr[...] = alpha * l_scr[...] + jnp.sum(p, axis=1, keepdims=True)
            acc_scr[...] = alpha * acc_scr[...] + _dot(p.astype(BF), v_ref[0])
            m_scr[...] = m_new

        @pl.when(j < i)
        def _():
            step(False)

        @pl.when(j == i)
        def _():
            step(True)
            l = l_scr[...]
            o_ref[0] = acc_scr[...] / l
            lse_ref[0] = m_scr[...] + jnp.log(l)

    ospec = pl.BlockSpec((1, blk, HEAD_DIM), lambda h, i, j: (h, i, 0))
    qspec = pl.BlockSpec((1, blk, AUG_DIM), lambda h, i, j: (h, i, 0))
    kspec = pl.BlockSpec((1, blk, AUG_DIM), lambda h, i, j: (h, jnp.minimum(j, i), 0))
    vspec = pl.BlockSpec((1, blk, HEAD_DIM), lambda h, i, j: (h, jnp.minimum(j, i), 0))
    return _call(
        body, name="fox_fwd",
        out_shape=(jax.ShapeDtypeStruct((nh, s, HEAD_DIM), F32), jax.ShapeDtypeStruct((nh, s, 1), F32)),
        grid=(nh, nb, nb),
        in_specs=[qspec, kspec, vspec, pl.BlockSpec((1, blk, 1), lambda h, i, j: (h, i, 0)),
                  pl.BlockSpec((1, 1, blk), lambda h, i, j: (h, 0, jnp.minimum(j, i)))],
        out_specs=(ospec, pl.BlockSpec((1, blk, 1), lambda h, i, j: (h, i, 0))),
        scratch_shapes=[pltpu.VMEM((blk, 1), F32), pltpu.VMEM((blk, 1), F32), pltpu.VMEM((blk, HEAD_DIM), F32)],
        compiler_params=_params(("parallel", "parallel", "arbitrary")),
    )(q, k, v, cum_col, cum_row)


def _fox_bwd(q, k, v, do, lse, delta, cum_col, cum_row, blk):
    nh, s, _ = q.shape
    nb = s // blk

    def body(q_ref, k_ref, v_ref, do_ref, lse_ref, dl_ref, cq_ref, ck_ref,
             dq_ref, dk_ref, dv_ref, dk_scr, dv_scr):
        j, i = pl.program_id(1), pl.program_id(2)

        @pl.when((j == 0) & (i == 0))
        def _():
            dq_ref[...] = jnp.zeros(dq_ref.shape, F32)

        @pl.when(i == 0)
        def _():
            dk_scr[...] = jnp.zeros(dk_scr.shape, F32)
            dv_scr[...] = jnp.zeros(dv_scr.shape, F32)

        def step(masked):
            qv, kv, dov = q_ref[0], k_ref[0], do_ref[0]
            sc = _dot_nt(qv, kv) + (cq_ref[0] - ck_ref[0])
            p = jnp.exp(sc - lse_ref[0])
            if masked:
                p = jnp.where(_causal_mask(i, j, blk), p, 0.0)
            dv_scr[...] += _dot_tn(p.astype(BF), dov)
            dp = _dot_nt(dov, v_ref[0])
            ds = p * (dp - dl_ref[0])
            dsb = ds.astype(BF)
            dk_scr[...] += _dot_tn(dsb, qv)
            rows = pl.ds(pl.multiple_of(i * blk, blk), blk)
            dq_ref[0, rows, :] += _dot(dsb, kv)

        @pl.when(i > j)
        def _():
            step(False)

        @pl.when(i == j)
        def _():
            step(True)

        @pl.when(i == nb - 1)
        def _():
            dk_ref[0] = dk_scr[...]
            dv_ref[0] = dv_scr[...]

    qspec = pl.BlockSpec((1, blk, AUG_DIM), lambda h, j, i: (h, jnp.maximum(i, j), 0))
    dospec = pl.BlockSpec((1, blk, HEAD_DIM), lambda h, j, i: (h, jnp.maximum(i, j), 0))
    qcol = pl.BlockSpec((1, blk, 1), lambda h, j, i: (h, jnp.maximum(i, j), 0))
    kspec = pl.BlockSpec((1, blk, AUG_DIM), lambda h, j, i: (h, j, 0))
    vspec = pl.BlockSpec((1, blk, HEAD_DIM), lambda h, j, i: (h, j, 0))
    krow = pl.BlockSpec((1, 1, blk), lambda h, j, i: (h, 0, j))
    aug = jax.ShapeDtypeStruct((nh, s, AUG_DIM), F32)
    return _call(
        body, name="fox_bwd", out_shape=(aug, aug, jax.ShapeDtypeStruct((nh, s, HEAD_DIM), F32)),
        grid=(nh, nb, nb),
        in_specs=[qspec, kspec, vspec, dospec, qcol, qcol, qcol, krow],
        out_specs=(pl.BlockSpec((1, s, AUG_DIM), lambda h, j, i: (h, 0, 0)), kspec, vspec),
        scratch_shapes=[pltpu.VMEM((blk, AUG_DIM), F32), pltpu.VMEM((blk, HEAD_DIM), F32)],
        compiler_params=_params(("parallel", "arbitrary", "arbitrary")),
    )(q, k, v, do, lse, delta, cum_col, cum_row)


def _swa_window(i, tq):
    kstart = pl.multiple_of(jnp.maximum(i * tq - WINDOW, 0), WINDOW)
    qpos = i * tq + lax.broadcasted_iota(jnp.int32, (tq, tq + WINDOW), 0)
    kpos = kstart + lax.broadcasted_iota(jnp.int32, (tq, tq + WINDOW), 1)
    rel = qpos - kpos
    return kstart, (rel >= 0) & (rel < WINDOW)


def _swa_probs(qh, kw, mask, sink):
    sc = jnp.where(mask, _dot_nt(qh, kw), NEG)
    m = jnp.maximum(jnp.max(sc, axis=1, keepdims=True), sink)
    p = jnp.exp(sc - m)
    e_sink = jnp.exp(sink - m)
    inv_l = 1.0 / (jnp.sum(p, axis=1, keepdims=True) + e_sink)
    return p * inv_l, e_sink * inv_l


def _swa_fwd(qb, kb, vb, sinks, tq):
    s = qb.shape[0]
    gw = SWA_GROUP * HEAD_DIM

    def body(q_ref, k_ref, v_ref, s_ref, o_ref):
        i = pl.program_id(1)
        kstart, mask = _swa_window(i, tq)
        kw = k_ref[0, pl.ds(kstart, tq + WINDOW), :]
        vw = v_ref[0, pl.ds(kstart, tq + WINDOW), :]
        qv = q_ref[...]
        sk = s_ref[0]
        outs = []
        for hh in range(SWA_GROUP):
            p, _ = _swa_probs(qv[:, hh * HEAD_DIM:(hh + 1) * HEAD_DIM], kw, mask, sk[:, hh:hh + 1])
            outs.append(_dot(p.astype(BF), vw))
        o_ref[...] = jnp.concatenate(outs, axis=1)

    kvspec = pl.BlockSpec((1, s, HEAD_DIM), lambda g, i: (g, 0, 0))
    return _call(
        body, name="swa_fwd", out_shape=jax.ShapeDtypeStruct((s, SWA_W), F32), grid=(SWA_KV_HEADS, s // tq),
        in_specs=[pl.BlockSpec((tq, gw), lambda g, i: (i, g)), kvspec, kvspec,
                  pl.BlockSpec((1, 1, SWA_GROUP), lambda g, i: (g, 0, 0))],
        out_specs=pl.BlockSpec((tq, gw), lambda g, i: (i, g)),
        compiler_params=_params(("parallel", "parallel")),
    )(qb, kb, vb, sinks)


def _swa_bwd(qb, kb, vb, sinks, dob, tq):
    s = qb.shape[0]
    gw = SWA_GROUP * HEAD_DIM

    def body(q_ref, k_ref, v_ref, s_ref, do_ref, dq_ref, dk_ref, dv_ref, ds_ref):
        i = pl.program_id(1)

        @pl.when(i == 0)
        def _():
            dk_ref[...] = jnp.zeros(dk_ref.shape, F32)
            dv_ref[...] = jnp.zeros(dv_ref.shape, F32)
            ds_ref[...] = jnp.zeros(ds_ref.shape, F32)

        kstart, mask = _swa_window(i, tq)
        win = pl.ds(kstart, tq + WINDOW)
        kw = k_ref[0, win, :]
        vw = v_ref[0, win, :]
        qv = q_ref[...]
        dov = do_ref[...]
        sk = s_ref[0]
        dqs, dsinks = [], []
        dk_acc = jnp.zeros((tq + WINDOW, HEAD_DIM), F32)
        dv_acc = jnp.zeros((tq + WINDOW, HEAD_DIM), F32)
        for hh in range(SWA_GROUP):
            sl = slice(hh * HEAD_DIM, (hh + 1) * HEAD_DIM)
            qh, doh = qv[:, sl], dov[:, sl]
            p, p_sink = _swa_probs(qh, kw, mask, sk[:, hh:hh + 1])
            dp = _dot_nt(doh, vw)
            delta = jnp.sum(p * dp, axis=1, keepdims=True)
            dsc = (p * (dp - delta)).astype(BF)
            dqs.append(_dot(dsc, kw))
            dk_acc = dk_acc + _dot_tn(dsc, qh)
            dv_acc = dv_acc + _dot_tn(p.astype(BF), doh)
            dsinks.append(-jnp.sum(p_sink * delta, axis=0, keepdims=True))
        dq_ref[...] = jnp.concatenate(dqs, axis=1)
        dk_ref[0, win, :] += dk_acc
        dv_ref[0, win, :] += dv_acc
        ds_ref[0] += jnp.concatenate(dsinks, axis=1)

    kvspec = pl.BlockSpec((1, s, HEAD_DIM), lambda g, i: (g, 0, 0))
    qspec = pl.BlockSpec((tq, gw), lambda g, i: (i, g))
    kvshape = jax.ShapeDtypeStruct((SWA_KV_HEADS, s, HEAD_DIM), F32)
    return _call(
        body, name="swa_bwd",
        out_shape=(jax.ShapeDtypeStruct((s, SWA_W), F32), kvshape, kvshape,
                   jax.ShapeDtypeStruct((SWA_KV_HEADS, 1, SWA_GROUP), F32)),
        grid=(SWA_KV_HEADS, s // tq),
        in_specs=[qspec, kvspec, kvspec, pl.BlockSpec((1, 1, SWA_GROUP), lambda g, i: (g, 0, 0)), qspec],
        out_specs=(qspec, kvspec, kvspec, pl.BlockSpec((1, 1, SWA_GROUP), lambda g, i: (g, 0, 0))),
        compiler_params=_params(("parallel", "arbitrary")),
    )(qb, kb, vb, sinks, dob)


def _out_proj(oa, za, ob, zb, x, tgt, w_out, w_out_t, gate, g_post, tm):
    s = x.shape[0]

    def body(oa_ref, za_ref, ob_ref, zb_ref, x_ref, t_ref, w_ref, wt_ref, gate_ref, gp_ref,
             dout_ref, doa_ref, dla_ref, dza_ref, dob_ref, dzb_ref, gw_ref, dgate_ref, dgp_ref, loss_ref):
        i = pl.program_id(0)

        @pl.when(i == 0)
        def _():
            gw_ref[...] = jnp.zeros(gw_ref.shape, F32)
            dgate_ref[...] = jnp.zeros(dgate_ref.shape, F32)
            dgp_ref[...] = jnp.zeros(dgp_ref.shape, F32)
            loss_ref[...] = jnp.zeros(loss_ref.shape, F32)

        oa_v = jnp.concatenate([oa_ref[hd] for hd in range(FOX_HEADS)], axis=1)
        ob_v = ob_ref[...]
        za_v, zb_v = za_ref[...], zb_ref[...]
        sga, sgb = _sigmoid(za_v), _sigmoid(zb_v)
        sila, silb = za_v * sga, zb_v * sgb
        u = jnp.concatenate([oa_v * sila, ob_v * silb], axis=1).astype(BF)
        yv = _dot(u, w_ref[...])
        yhat, rstd = _rms_hat(yv)
        gp, gate_v = gp_ref[...], gate_ref[...]
        nrm = yhat * gp
        diff = (x_ref[...] + gate_v * nrm) - t_ref[...]
        loss_ref[...] += 0.5 * jnp.sum(jnp.sum(diff * diff, axis=1, keepdims=True), axis=0, keepdims=True) / D_MODEL
        dout = diff * (1.0 / D_MODEL)
        dout_ref[...] = dout
        dgate_ref[...] += jnp.sum(dout * nrm, axis=0, keepdims=True)
        dn = dout * gate_v
        dgp_ref[...] += jnp.sum(dn * yhat, axis=0, keepdims=True)
        dyhat = dn * gp
        dy = (rstd * (dyhat - yhat * jnp.mean(dyhat * yhat, axis=1, keepdims=True))).astype(BF)
        gw_ref[...] += _dot_tn(u, dy)
        du = _dot(dy, wt_ref[...])
        dua, dub = du[:, :FOX_W], du[:, FOX_W:]
        doa = dua * sila
        for hd in range(FOX_HEADS):
            sl = slice(hd * HEAD_DIM, (hd + 1) * HEAD_DIM)
            doa_ref[hd] = doa[:, sl].astype(BF)
            dla_ref[hd] = jnp.sum(doa[:, sl] * oa_v[:, sl], axis=1, keepdims=True)
        dob_ref[...] = (dub * silb).astype(BF)
        dza_ref[...] = (dua * oa_v * (sga * (1.0 + za_v * (1.0 - sga)))).astype(BF)
        dzb_ref[...] = (dub * ob_v * (sgb * (1.0 + zb_v * (1.0 - sgb)))).astype(BF)

    row = lambda w: pl.BlockSpec((tm, w), lambda i: (i, 0))
    heads = lambda w: pl.BlockSpec((FOX_HEADS, tm, w), lambda i: (0, i, 0))
    vec = _full((1, D_MODEL))
    mat = _full((D_MODEL, D_MODEL))
    out_shape = (
        jax.ShapeDtypeStruct((s, D_MODEL), F32),
        jax.ShapeDtypeStruct((FOX_HEADS, s, HEAD_DIM), BF), jax.ShapeDtypeStruct((FOX_HEADS, s, 1), F32),
        jax.ShapeDtypeStruct((s, FOX_W), BF), jax.ShapeDtypeStruct((s, SWA_W), BF), jax.ShapeDtypeStruct((s, SWA_W), BF),
        jax.ShapeDtypeStruct((D_MODEL, D_MODEL), F32),
        jax.ShapeDtypeStruct((1, D_MODEL), F32), jax.ShapeDtypeStruct((1, D_MODEL), F32),
        jax.ShapeDtypeStruct((1, 1), F32),
    )
    return _call(
        body, name="out_proj", out_shape=out_shape, grid=(s // tm,),
        in_specs=[heads(HEAD_DIM), row(FOX_W), row(SWA_W), row(SWA_W), row(D_MODEL), row(D_MODEL), mat, mat, vec, vec],
        out_specs=(row(D_MODEL), heads(HEAD_DIM), heads(1), row(FOX_W), row(SWA_W), row(SWA_W), mat, vec, vec,
                   _full((1, 1))),
        compiler_params=_params(("arbitrary",)),
    )(oa, za, ob, zb, x, tgt, w_out, w_out_t, gate, g_post)


def _assemble_dproj(dqa, dka, dva, dza, dqb, dzb, dkb, dvb, df, cos_t, sin_t, tm):
    s = dza.shape[0]

    def body(dqa_ref, dka_ref, dva_ref, dza_ref, dqb_ref, dzb_ref, dkb_ref, dvb_ref, df_ref, cos_ref, sin_ref, o_ref):
        def cat(ref, n):
            return jnp.concatenate([ref[hd][:, :HEAD_DIM] for hd in range(n)], axis=1)

        cos2, sin2 = cos_ref[...], sin_ref[...]
        cos8 = jnp.concatenate([cos2] * 4, axis=1)
        sin8 = jnp.concatenate([sin2] * 4, axis=1)
        scale = HEAD_DIM ** -0.5
        o_ref[:, C_QA:C_QA + FOX_W] = (cat(dqa_ref, FOX_HEADS) * scale).astype(BF)
        o_ref[:, C_KA:C_KA + FOX_W] = cat(dka_ref, FOX_HEADS).astype(BF)
        o_ref[:, C_VA:C_VA + FOX_W] = cat(dva_ref, FOX_HEADS).astype(BF)
        o_ref[:, C_ZA:C_ZA + FOX_W] = dza_ref[...]
        dq = dqb_ref[...] * scale
        o_ref[:, C_QB:C_QB + SWA_W] = (dq * cos8 - _rope_partner(dq) * sin8).astype(BF)
        o_ref[:, C_ZB:C_ZB + SWA_W] = dzb_ref[...]
        dk = cat(dkb_ref, SWA_KV_HEADS)
        o_ref[:, C_KB:C_KB + SWA_KV_W] = (dk * cos2 - _rope_partner(dk) * sin2).astype(BF)
        o_ref[:, C_VB:C_VB + SWA_KV_W] = cat(dvb_ref, SWA_KV_HEADS).astype(BF)
        o_ref[:, C_F:C_F + LANES] = df_ref[...].astype(BF)

    row = lambda w: pl.BlockSpec((tm, w), lambda i: (i, 0))
    heads = lambda n, w=HEAD_DIM: pl.BlockSpec((n, tm, w), lambda i: (0, i, 0))
    return _call(
        body, name="assemble_dproj", out_shape=jax.ShapeDtypeStruct((s, WP), BF), grid=(s // tm,),
        in_specs=[heads(FOX_HEADS, AUG_DIM), heads(FOX_HEADS, AUG_DIM), heads(FOX_HEADS), row(FOX_W), row(SWA_W), row(SWA_W),
                  heads(SWA_KV_HEADS), heads(SWA_KV_HEADS), row(LANES), row(LANES), row(LANES)],
        out_specs=row(WP), compiler_params=_params(("parallel",)),
    )(dqa, dka, dva, dza, dqb, dzb, dkb, dvb, df, cos_t, sin_t)


def _in_proj_bwd_x(dproj, w_al_t, x, dout, g_pre, scale1p, tm):
    s = x.shape[0]

    def body(dp_ref, wt_ref, x_ref, dout_ref, g_ref, sc_ref, gx_ref, dsh_ref, dsc_ref, dg_ref):
        i = pl.program_id(0)

        @pl.when(i == 0)
        def _():
            dsh_ref[...] = jnp.zeros(dsh_ref.shape, F32)
            dsc_ref[...] = jnp.zeros(dsc_ref.shape, F32)
            dg_ref[...] = jnp.zeros(dg_ref.shape, F32)

        dh = _dot(dp_ref[...], wt_ref[...])
        xhat, rstd = _rms_hat(x_ref[...])
        g, sc = g_ref[...], sc_ref[...]
        dsh_ref[...] += jnp.sum(dh, axis=0, keepdims=True)
        dhx = dh * xhat
        dsc_ref[...] += jnp.sum(dhx * g, axis=0, keepdims=True)
        dg_ref[...] += jnp.sum(dhx * sc, axis=0, keepdims=True)
        dxhat = dh * (g * sc)
        gx_ref[...] = dout_ref[...] + rstd * (dxhat - xhat * jnp.mean(dxhat * xhat, axis=1, keepdims=True))

    row = lambda w: pl.BlockSpec((tm, w), lambda i: (i, 0))
    vec = _full((1, D_MODEL))
    vshape = jax.ShapeDtypeStruct((1, D_MODEL), F32)
    return _call(
        body, name="in_proj_bwd_x", out_shape=(jax.ShapeDtypeStruct((s, D_MODEL), F32), vshape, vshape, vshape),
        grid=(s // tm,),
        in_specs=[row(WP), _full((WP, D_MODEL)), row(D_MODEL), row(D_MODEL), vec, vec],
        out_specs=(row(D_MODEL), vec, vec, vec), compiler_params=_params(("arbitrary",)),
    )(dproj, w_al_t, x, dout, g_pre, scale1p)


def _in_proj_bwd_w(h, dproj, tk, tn):
    s = h.shape[0]

    def body(h_ref, dp_ref, gw_ref):
        @pl.when(pl.program_id(1) == 0)
        def _():
            gw_ref[...] = jnp.zeros(gw_ref.shape, F32)

        gw_ref[...] += _dot_tn(h_ref[...], dp_ref[...])

    return _call(
        body, name="in_proj_bwd_w", out_shape=jax.ShapeDtypeStruct((D_MODEL, WP), F32), grid=(WP // tn, s // tk),
        in_specs=[pl.BlockSpec((tk, D_MODEL), lambda n, k: (k, 0)), pl.BlockSpec((tk, tn), lambda n, k: (k, n))],
        out_specs=pl.BlockSpec((D_MODEL, tn), lambda n, k: (0, n)),
        compiler_params=_params(("parallel", "arbitrary")),
    )(h, dproj)


def _align_w_in(w_cols):
    def part(name, width):
        return w_cols[:, _SRC[name]:_SRC[name] + width]

    fpad = jnp.pad(part("fa", FOX_HEADS), ((0, 0), (0, LANES - FOX_HEADS)))
    return jnp.concatenate([part("qa", FOX_W), part("ka", FOX_W), part("va", FOX_W), part("za", FOX_W),
                            part("qb", SWA_W), part("zb", SWA_W), part("kb", SWA_KV_W), part("vb", SWA_KV_W), fpad], axis=1)


def _unalign_w_in(g_al):
    def part(c0, width):
        return g_al[:, c0:c0 + width]

    return jnp.concatenate([part(C_QA, FOX_W), part(C_KA, FOX_W), part(C_VA, FOX_W), part(C_F, FOX_HEADS),
                            part(C_ZA, FOX_W), part(C_QB, SWA_W), part(C_KB, SWA_KV_W), part(C_VB, SWA_KV_W),
                            part(C_ZB, SWA_W)], axis=1)


def _rope_tables(positions):
    inv_freq = ROPE_THETA ** (-jnp.arange(HALF, dtype=F32) / HALF)
    ang = positions.astype(F32)[:, None] * inv_freq
    cos, sin = jnp.cos(ang), jnp.sin(ang)
    return jnp.concatenate([cos, cos, cos, cos], axis=1), jnp.concatenate([-sin, sin, -sin, sin], axis=1)


def _tiles(s):
    if s >= 4096:
        return dict(tm=512, blk=512, tq=256, tm_out=256, tk=512, tn=1152)
    return dict(tm=128, blk=128, tq=128, tm_out=128, tk=128, tn=1152)


def kernel(x, c, positions, w_ada, b_ada, g_pre, w_in, b_fgate, sinks, w_out, g_post, loss_target, m_w_ada, m_b_ada, m_g_pre, m_w_in, m_b_fgate, m_sinks, m_w_out, m_g_post, v_w_ada, v_b_ada, v_g_pre, v_w_in, v_b_fgate, v_sinks, v_w_out, v_g_post):
    s = x.shape[1]
    t = _tiles(s)
    nc = s // LANES
    rows = FOX_HEADS * nc
    me = 4 * lax.axis_index("x") + 2 * lax.axis_index("y") + lax.axis_index("c")
    chip = 2 * lax.axis_index("x") + lax.axis_index("y")
    core = lax.axis_index("c")
    x2, tgt = x[0], loss_target[0]

    c_all = _allgather_devices(c, "gather_c")[:, 0, :]
    a_all, mod_shard = _ada_shard(c_all, w_ada[0])
    mod_all = _allgather_devices(mod_shard, "gather_mod")
    mod_rows = lax.dynamic_index_in_dim(mod_all, me, axis=1, keepdims=False)
    mod = mod_rows.reshape(N_CHIPS, 2, W_ADA_SHARD)[:, 0, :].reshape(1, 3 * D_MODEL) + b_ada
    shift, scale1p, gate = mod[:, :D_MODEL], 1.0 + mod[:, D_MODEL:2 * D_MODEL], mod[:, 2 * D_MODEL:]

    w_in_pad = jnp.pad(w_in[0].astype(BF), ((0, 0), (0, W_IN_SHARD_PAD - W_IN_SHARD)))
    w_in_all = _allgather_chips(w_in_pad, "gather_w_in")
    w_cols = jnp.concatenate([w_in_all[k, :, :W_IN_SHARD] for k in range(N_CHIPS)], axis=1)
    w_al = _align_w_in(w_cols)
    w_al_t = w_al.T
    w_out_all = _allgather_chips(w_out[0].astype(BF), "gather_w_out").reshape(D_MODEL, D_MODEL)
    w_out_t = w_out_all.T

    cos_t, sin_t = _rope_tables(positions[0])

    h, qa, ka, va, za, zb, qb, kb, vb, f_pad = _in_proj(x2, g_pre, scale1p, shift, w_al, cos_t, sin_t, t["tm"])
    f_rows = f_pad[:, :FOX_HEADS].T.reshape(rows, LANES)
    bias_rows = jnp.repeat(b_fgate[0], nc)[:, None]
    cum_rows = _log_forget_cumsum(f_rows, bias_rows, nc)
    cum = cum_rows.reshape(FOX_HEADS, s)
    cum_col, cum_row = cum[:, :, None], cum[:, None, :]
    oa, lse = _fox_fwd(qa, ka, va, cum_col, cum_row, t["blk"])
    sinks_g = sinks.reshape(SWA_KV_HEADS, 1, SWA_GROUP)
    ob = _swa_fwd(qb, kb, vb, sinks_g, t["tq"])

    dout, doa, delta_a, dza, dob, dzb, gw_out, dgate, dg_post, loss_part = _out_proj(
        oa, za, ob, zb, x2, tgt, w_out_all, w_out_t, gate, g_post, t["tm_out"])

    dqa, dka, dva = _fox_bwd(qa, ka, va, doa, lse, delta_a, cum_col, cum_row, t["blk"])
    dcum = dqa[:, :, HEAD_DIM] - dka[:, :, HEAD_DIM]
    df_rows, db_heads = _log_forget_cumsum_bwd(dcum.reshape(rows, LANES), f_rows, bias_rows, nc)
    df_pad = jnp.pad(df_rows.reshape(FOX_HEADS, s).T, ((0, 0), (0, LANES - FOX_HEADS)))
    dqb, dkb, dvb, dsinks = _swa_bwd(qb, kb, vb, sinks_g, dob, t["tq"])

    dproj = _assemble_dproj(dqa, dka, dva, dza, dqb, dzb, dkb, dvb, df_pad, cos_t, sin_t, t["tm"])
    grad_x, dshift, dscale, dg_pre = _in_proj_bwd_x(dproj, w_al_t, x2, dout, g_pre, scale1p, t["tm_out"])
    gw_in = _unalign_w_in(_in_proj_bwd_w(h, dproj, t["tk"], t["tn"]))

    pad_lane = lambda vrow: jnp.pad(vrow, ((0, 0), (0, LANES - vrow.shape[1])))
    packed = jnp.concatenate([dshift, dscale, dgate, dg_pre, dg_post,
                              pad_lane(db_heads.reshape(1, FOX_HEADS)), pad_lane(dsinks.reshape(1, FOX_HEADS)),
                              pad_lane(loss_part)], axis=1)
    parts = _allgather_devices(packed, "gather_partials")
    tot = _sum_devices(parts)
    loss = tot[0, P_LOSS]
    g_b_ada = tot[:, P_DMOD:P_DMOD + 3 * D_MODEL]
    g_g_pre = tot[:, P_GPRE:P_GPRE + D_MODEL]
    g_g_post = tot[:, P_GPOST:P_GPOST + D_MODEL]
    g_b_fgate = tot[:, P_BF:P_BF + FOX_HEADS]
    g_sinks = tot[:, P_SINK:P_SINK + FOX_HEADS]
    dm_shard = lax.dynamic_slice_in_dim(parts[:, 0, :3 * D_MODEL], chip * W_ADA_SHARD, W_ADA_SHARD, axis=1)
    g_w_ada = _grad_w_ada(a_all.T, dm_shard)

    gin = jnp.pad(gw_in.reshape(D_MODEL, N_CHIPS, W_IN_SHARD).transpose(1, 0, 2),
                  ((0, 0), (0, 0), (0, W_IN_SHARD_PAD - W_IN_SHARD)))
    gout = gw_out.reshape(N_CHIPS, D_MODEL, W_OUT_SHARD)
    gbig = jnp.concatenate([gin, gout], axis=2)
    half = D_MODEL // 2
    gw = W_IN_SHARD_PAD + W_OUT_SHARD
    keep = lax.dynamic_slice_in_dim(gbig, core * half, half, axis=1)
    give = lax.dynamic_slice_in_dim(gbig, (1 - core) * half, half, axis=1)
    got = _swap_sibling(give.reshape(N_CHIPS * half, gw), "swap_grad_halves")
    pair = _add(keep.reshape(N_CHIPS * half, gw), got, "add_pair").reshape(N_CHIPS, half, gw)
    from_chips = _exchange_chips(pair, "exchange_grad")
    mine = _sum_chips(from_chips, "sum_chips")
    other = _swap_sibling(mine, "swap_grad_result")
    lo = jnp.where(core == 0, mine, other)
    hi = jnp.where(core == 0, other, mine)
    gfull = jnp.concatenate([lo, hi], axis=0)
    g_w_in = gfull[:, :W_IN_SHARD]
    g_w_out = gfull[:, W_IN_SHARD_PAD:].reshape(W_OUT_SHARD, D_MODEL)

    grads = dict(w_ada=g_w_ada, b_ada=g_b_ada, g_pre=g_g_pre, w_in=g_w_in, b_fgate=g_b_fgate, sinks=g_sinks,
                 w_out=g_w_out, g_post=g_g_post)
    weights = dict(w_ada=w_ada, b_ada=b_ada, g_pre=g_pre, w_in=w_in, b_fgate=b_fgate, sinks=sinks, w_out=w_out, g_post=g_post)
    moms = dict(w_ada=m_w_ada, b_ada=m_b_ada, g_pre=m_g_pre, w_in=m_w_in, b_fgate=m_b_fgate, sinks=m_sinks, w_out=m_w_out, g_post=m_g_post)
    vars_ = dict(w_ada=v_w_ada, b_ada=v_b_ada, g_pre=v_g_pre, w_in=v_w_in, b_fgate=v_b_fgate, sinks=v_sinks, w_out=v_w_out, g_post=v_g_post)
    names = ["w_ada", "b_ada", "g_pre", "w_in", "b_fgate", "sinks", "w_out", "g_post"]
    g_out, d_out, m_out, v_out = [], [], [], []
    for n in names:
        shape = weights[n].shape
        w2 = weights[n].reshape(shape[-2], shape[-1])
        g2 = grads[n].reshape(w2.shape)
        d2, nm2, nv2 = _adamw(w2, g2, moms[n].reshape(w2.shape), vars_[n].reshape(w2.shape), "adamw_" + n)
        g_out.append(g2.reshape(shape))
        d_out.append(d2.reshape(shape))
        m_out.append(nm2.reshape(shape))
        v_out.append(nv2.reshape(shape))
    return (loss, grad_x.reshape(x.shape), *g_out, *d_out, *m_out, *v_out)
```

```python
import functools

import jax
import jax.numpy as jnp
from jax import lax
from jax.experimental import pallas as pl
from jax.experimental.pallas import tpu as pltpu

_INTERPRET = False

D_MODEL = 1024
HEAD_DIM = 64
HALF = HEAD_DIM // 2
AUG_DIM = 128
AUG_ROWS = 8
FOX_HEADS = 8
FOX_W = 512
SWA_W = 512
SWA_KV_HEADS = 2
SWA_GROUP = 4
SWA_KV_W = 128
WINDOW = 128
ROPE_THETA = 10000.0
RMS_EPS = 1e-6
IN_WIDTH = 3336
N_CHIPS = 4
N_DEV = 8
W_IN_SHARD = IN_WIDTH // N_CHIPS
W_IN_SHARD_PAD = 896
W_ADA_SHARD = 3 * D_MODEL // N_CHIPS
W_OUT_SHARD = D_MODEL // N_CHIPS
LANES = 128

_SRC = dict(qa=0, ka=512, va=1024, fa=1536, za=1544, qb=2056, kb=2568, vb=2696, zb=2824)
C_QA, C_KA, C_VA, C_ZA, C_QB, C_ZB, C_KB, C_VB, C_F = 0, 512, 1024, 1536, 2048, 2560, 3072, 3200, 3328
WP = 3456

ADAM_LR = 0.001
ADAM_B1 = 0.9
ADAM_B2 = 0.999
ADAM_EPS = 1e-08
ADAM_WD = 0.01
ADAM_STEP = 10

VMEM_LIMIT = 56 * 1024 * 1024
NEG = -1e30
MESH = pl.DeviceIdType.MESH
BF = jnp.bfloat16
F32 = jnp.float32

P_DMOD, P_GPRE, P_GPOST, P_BF, P_SINK, P_LOSS, P_LEN = 0, 3072, 4096, 5120, 5248, 5376, 5504


def _call(body, **kw):
    return pl.pallas_call(body, interpret=_INTERPRET, **kw)


def _params(sem=None, **kw):
    return pltpu.CompilerParams(dimension_semantics=sem, vmem_limit_bytes=VMEM_LIMIT, **kw)


def _full(shape):
    zeros = (0,) * len(shape)
    return pl.BlockSpec(shape, lambda *_: zeros)


def _dot(a, b):
    return jnp.dot(a, b, preferred_element_type=F32)


def _dot_nt(a, b):
    return lax.dot_general(a, b, (((1,), (1,)), ((), ())), preferred_element_type=F32)


def _dot_tn(a, b):
    return lax.dot_general(a, b, (((0,), (0,)), ((), ())), preferred_element_type=F32)


def _sigmoid(z):
    return 1.0 / (1.0 + jnp.exp(-z))


def _rope_partner(t):
    w = t.shape[-1]
    lane = lax.broadcasted_iota(jnp.int32, t.shape, t.ndim - 1)
    return jnp.where((lane & (HEAD_DIM - 1)) < HALF, pltpu.roll(t, w - HALF, t.ndim - 1), pltpu.roll(t, HALF, t.ndim - 1))


def _allgather_devices(v, name):
    r, cdim = v.shape
    masks = [(dx, dy, dc) for dx in (0, 1) for dy in (0, 1) for dc in (0, 1)][1:]

    def body(v_ref, out_ref, send_sems, recv_sems):
        x, y, c = lax.axis_index("x"), lax.axis_index("y"), lax.axis_index("c")
        me = 4 * x + 2 * y + c
        out_ref[me] = v_ref[...]
        copies = []
        for k, (dx, dy, dc) in enumerate(masks):
            cp = pltpu.make_async_remote_copy(
                src_ref=v_ref, dst_ref=out_ref.at[me], send_sem=send_sems.at[k], recv_sem=recv_sems.at[k],
                device_id=(x ^ dx, y ^ dy, c ^ dc), device_id_type=MESH)
            cp.start()
            copies.append(cp)
        for k, (dx, dy, dc) in enumerate(masks):
            peer = 4 * (x ^ dx) + 2 * (y ^ dy) + (c ^ dc)
            pltpu.make_async_remote_copy(
                src_ref=v_ref, dst_ref=out_ref.at[peer], send_sem=send_sems.at[k], recv_sem=recv_sems.at[k],
                device_id=(x ^ dx, y ^ dy, c ^ dc), device_id_type=MESH).wait_recv()
        for cp in copies:
            cp.wait_send()

    return _call(
        body, name=name, out_shape=jax.ShapeDtypeStruct((N_DEV, r, cdim), v.dtype),
        in_specs=[pl.BlockSpec(memory_space=pltpu.VMEM)], out_specs=pl.BlockSpec(memory_space=pltpu.VMEM),
        scratch_shapes=[pltpu.SemaphoreType.DMA((7,)), pltpu.SemaphoreType.DMA((7,))],
        compiler_params=pltpu.CompilerParams(has_side_effects=True),
    )(v)


def _allgather_chips(v, name):
    r, cdim = v.shape
    masks = [(1, 0), (0, 1), (1, 1)]

    def body(v_ref, out_ref, send_sems, recv_sems, local_sem):
        x, y, c = lax.axis_index("x"), lax.axis_index("y"), lax.axis_index("c")
        me = 2 * x + y
        mine = pltpu.make_async_copy(v_ref, out_ref.at[me], local_sem)
        mine.start()
        copies = []
        for k, (dx, dy) in enumerate(masks):
            cp = pltpu.make_async_remote_copy(
                src_ref=v_ref, dst_ref=out_ref.at[me], send_sem=send_sems.at[k], recv_sem=recv_sems.at[k],
                device_id=(x ^ dx, y ^ dy, c), device_id_type=MESH)
            cp.start()
            copies.append(cp)
        for k, (dx, dy) in enumerate(masks):
            peer = 2 * (x ^ dx) + (y ^ dy)
            pltpu.make_async_remote_copy(
                src_ref=v_ref, dst_ref=out_ref.at[peer], send_sem=send_sems.at[k], recv_sem=recv_sems.at[k],
                device_id=(x ^ dx, y ^ dy, c), device_id_type=MESH).wait_recv()
        for cp in copies:
            cp.wait_send()
        mine.wait()

    return _call(
        body, name=name, out_shape=jax.ShapeDtypeStruct((N_CHIPS, r, cdim), v.dtype),
        in_specs=[pl.BlockSpec(memory_space=pl.ANY)], out_specs=pl.BlockSpec(memory_space=pl.ANY),
        scratch_shapes=[pltpu.SemaphoreType.DMA((3,)), pltpu.SemaphoreType.DMA((3,)), pltpu.SemaphoreType.DMA],
        compiler_params=pltpu.CompilerParams(has_side_effects=True),
    )(v)


def _swap_sibling(v, name):
    def body(v_ref, out_ref, send_sem, recv_sem):
        x, y, c = lax.axis_index("x"), lax.axis_index("y"), lax.axis_index("c")
        cp = pltpu.make_async_remote_copy(
            src_ref=v_ref, dst_ref=out_ref, send_sem=send_sem, recv_sem=recv_sem,
            device_id=(x, y, 1 - c), device_id_type=MESH)
        cp.start()
        cp.wait()

    return _call(
        body, name=name, out_shape=jax.ShapeDtypeStruct(v.shape, v.dtype),
        in_specs=[pl.BlockSpec(memory_space=pl.ANY)], out_specs=pl.BlockSpec(memory_space=pl.ANY),
        scratch_shapes=[pltpu.SemaphoreType.DMA, pltpu.SemaphoreType.DMA],
        compiler_params=pltpu.CompilerParams(has_side_effects=True),
    )(v)


def _exchange_chips(v, name):
    _, r, cdim = v.shape
    masks = [(1, 0), (0, 1), (1, 1)]

    def body(v_ref, out_ref, send_sems, recv_sems, local_sem):
        x, y, c = lax.axis_index("x"), lax.axis_index("y"), lax.axis_index("c")
        me = 2 * x + y
        mine = pltpu.make_async_copy(v_ref.at[me], out_ref.at[me], local_sem)
        mine.start()
        copies = []
        for k, (dx, dy) in enumerate(masks):
            peer = 2 * (x ^ dx) + (y ^ dy)
            cp = pltpu.make_async_remote_copy(
                src_ref=v_ref.at[peer], dst_ref=out_ref.at[me], send_sem=send_sems.at[k], recv_sem=recv_sems.at[k],
                device_id=(x ^ dx, y ^ dy, c), device_id_type=MESH)
            cp.start()
            copies.append(cp)
        for k, (dx, dy) in enumerate(masks):
            peer = 2 * (x ^ dx) + (y ^ dy)
            pltpu.make_async_remote_copy(
                src_ref=v_ref.at[me], dst_ref=out_ref.at[peer], send_sem=send_sems.at[k], recv_sem=recv_sems.at[k],
                device_id=(x ^ dx, y ^ dy, c), device_id_type=MESH).wait_recv()
        for cp in copies:
            cp.wait_send()
        mine.wait()

    return _call(
        body, name=name, out_shape=jax.ShapeDtypeStruct(v.shape, v.dtype),
        in_specs=[pl.BlockSpec(memory_space=pl.ANY)], out_specs=pl.BlockSpec(memory_space=pl.ANY),
        scratch_shapes=[pltpu.SemaphoreType.DMA((3,)), pltpu.SemaphoreType.DMA((3,)), pltpu.SemaphoreType.DMA],
        compiler_params=pltpu.CompilerParams(has_side_effects=True),
    )(v)


def _ada_shard(c_all, w_ada_shard):
    def body(c_ref, w_ref, a_ref, mod_ref):
        cv = c_ref[...]
        a = cv * _sigmoid(cv)
        a_ref[...] = a
        mod_ref[...] = _dot(a.astype(BF), w_ref[...].astype(BF))

    return _call(
        body, name="ada_shard",
        out_shape=(jax.ShapeDtypeStruct((N_DEV, D_MODEL), F32), jax.ShapeDtypeStruct((N_DEV, W_ADA_SHARD), F32)),
        compiler_params=_params(),
    )(c_all, w_ada_shard)


def _grad_w_ada(a_t, dm_shard):
    def body(a_ref, dm_ref, out_ref):
        acc = jnp.zeros((D_MODEL, W_ADA_SHARD), F32)
        for b in range(N_DEV):
            acc = acc + a_ref[:, b:b + 1] * dm_ref[b:b + 1, :]
        out_ref[...] = acc

    return _call(body, name="grad_w_ada", out_shape=jax.ShapeDtypeStruct((D_MODEL, W_ADA_SHARD), F32),
                 compiler_params=_params())(a_t, dm_shard)


def _sum_devices(parts):
    n = parts.shape[-1]

    def body(p_ref, out_ref):
        acc = p_ref[0]
        for b in range(1, N_DEV):
            acc = acc + p_ref[b]
        out_ref[...] = acc

    return _call(body, name="sum_devices", out_shape=jax.ShapeDtypeStruct((1, n), F32), compiler_params=_params())(parts)


def _add(a, b, name):
    r, cdim = a.shape
    tr = min(r, 256)

    def body(a_ref, b_ref, o_ref):
        o_ref[...] = a_ref[...] + b_ref[...]

    spec = pl.BlockSpec((tr, cdim), lambda i: (i, 0))
    return _call(body, name=name, out_shape=jax.ShapeDtypeStruct(a.shape, F32), grid=(r // tr,),
                 in_specs=[spec, spec], out_specs=spec, compiler_params=_params(("parallel",)))(a, b)


def _sum_chips(parts, name):
    _, r, cdim = parts.shape
    tr = min(r, 128)

    def body(p_ref, o_ref):
        o_ref[...] = ((p_ref[0] + p_ref[1]) + p_ref[2]) + p_ref[3]

    return _call(body, name=name, out_shape=jax.ShapeDtypeStruct((r, cdim), F32), grid=(r // tr,),
                 in_specs=[pl.BlockSpec((N_CHIPS, tr, cdim), lambda i: (0, i, 0))],
                 out_specs=pl.BlockSpec((tr, cdim), lambda i: (i, 0)), compiler_params=_params(("parallel",)))(parts)


def _adamw(w, g, m, v, name):
    r, cdim = w.shape
    tr = r if r <= 256 else 256
    c1 = 1.0 / (1.0 - ADAM_B1 ** ADAM_STEP)
    c2 = 1.0 / (1.0 - ADAM_B2 ** ADAM_STEP)

    def body(w_ref, g_ref, m_ref, v_ref, d_ref, nm_ref, nv_ref):
        gv = g_ref[...]
        nm = ADAM_B1 * m_ref[...] + (1.0 - ADAM_B1) * gv
        nv = ADAM_B2 * v_ref[...] + (1.0 - ADAM_B2) * (gv * gv)
        m_hat = nm * c1
        v_hat = nv * c2
        d_ref[...] = -ADAM_LR * (m_hat / (jnp.sqrt(v_hat) + ADAM_EPS) + ADAM_WD * w_ref[...])
        nm_ref[...] = nm
        nv_ref[...] = nv

    spec = pl.BlockSpec((tr, cdim), lambda i: (i, 0))
    shp = jax.ShapeDtypeStruct(w.shape, F32)
    return _call(body, name=name, out_shape=(shp, shp, shp), grid=(r // tr,), in_specs=[spec] * 4,
                 out_specs=(spec, spec, spec), compiler_params=_params(("parallel",)))(w, g, m, v)


def _head_of_row(r, nc):
    assert nc & (nc - 1) == 0
    return lax.shift_right_logical(r, nc.bit_length() - 1)


def _chunk_mats(rows, nc, reverse):
    ri = lax.broadcasted_iota(jnp.int32, (rows, rows), 0)
    ci = lax.broadcasted_iota(jnp.int32, (rows, rows), 1)
    same = _head_of_row(ri, nc) == _head_of_row(ci, nc)
    between = jnp.where(same & ((ci > ri) if reverse else (ci < ri)), 1.0, 0.0).astype(F32)
    li = lax.broadcasted_iota(jnp.int32, (LANES, LANES), 0)
    lj = lax.broadcasted_iota(jnp.int32, (LANES, LANES), 1)
    within = jnp.where((li >= lj) if reverse else (li <= lj), 1.0, 0.0).astype(F32)
    return between, within


def _dot_hi(a, b):
    return jnp.dot(a, b, preferred_element_type=F32, precision=lax.Precision.HIGHEST)


def _scan_rows(t, nc, reverse):
    between, within = _chunk_mats(t.shape[0], nc, reverse)
    inner = _dot_hi(t, within)
    tot = jnp.sum(t, axis=1, keepdims=True)
    return inner + _dot_hi(between, jnp.broadcast_to(tot, t.shape))


def _log_forget_cumsum(f_rows, bias_rows, nc):
    def body(f_ref, b_ref, cum_ref):
        z = f_ref[...] + b_ref[...]
        lf = jnp.minimum(z, 0.0) - jnp.log(1.0 + jnp.exp(-jnp.abs(z)))
        cum_ref[...] = _scan_rows(lf, nc, False)

    return _call(body, name="forget_cumsum", out_shape=jax.ShapeDtypeStruct(f_rows.shape, F32),
                 compiler_params=_params())(f_rows, bias_rows)


def _log_forget_cumsum_bwd(dcum_rows, f_rows, bias_rows, nc):
    rows = f_rows.shape[0]

    def body(d_ref, f_ref, b_ref, df_ref, db_ref):
        dlf = _scan_rows(d_ref[...], nc, True)
        z = f_ref[...] + b_ref[...]
        df = dlf * _sigmoid(-z)
        df_ref[...] = df
        hi = lax.broadcasted_iota(jnp.int32, (FOX_HEADS, rows), 0)
        ri = lax.broadcasted_iota(jnp.int32, (FOX_HEADS, rows), 1)
        sel = jnp.where(_head_of_row(ri, nc) == hi, 1.0, 0.0).astype(F32)
        db_ref[...] = jnp.sum(_dot_hi(sel, df), axis=1, keepdims=True)

    return _call(body, name="forget_cumsum_bwd",
                 out_shape=(jax.ShapeDtypeStruct(f_rows.shape, F32), jax.ShapeDtypeStruct((FOX_HEADS, 1), F32)),
                 compiler_params=_params())(dcum_rows, f_rows, bias_rows)


def _rms_hat(xv):
    rstd = lax.rsqrt(jnp.mean(xv * xv, axis=-1, keepdims=True) + RMS_EPS)
    return xv * rstd, rstd


def _modulated(x_ref, g_ref, sc_ref, sh_ref):
    xhat, _ = _rms_hat(x_ref[...])
    return ((xhat * g_ref[...]) * sc_ref[...] + sh_ref[...]).astype(BF)


def _forget_logits(x, g_pre, scale1p, shift, w_f, tm):
    s = x.shape[0]

    def body(x_ref, g_ref, sc_ref, sh_ref, w_ref, f_ref):
        f_ref[...] = _dot(_modulated(x_ref, g_ref, sc_ref, sh_ref), w_ref[...])

    vec = _full((1, D_MODEL))
    return _call(
        body, name="forget_logits", out_shape=jax.ShapeDtypeStruct((s, LANES), F32), grid=(s // tm,),
        in_specs=[pl.BlockSpec((tm, D_MODEL), lambda i: (i, 0)), vec, vec, vec, _full((D_MODEL, LANES))],
        out_specs=pl.BlockSpec((tm, LANES), lambda i: (i, 0)), compiler_params=_params(("parallel",)),
    )(x, g_pre, scale1p, shift, w_f)


def _split3(v):
    hi = v.astype(BF).astype(F32)
    mid = (v - hi).astype(BF).astype(F32)
    lo = ((v - hi) - mid).astype(BF).astype(F32)
    return hi, mid, lo


def _in_proj(x, g_pre, scale1p, shift, w_rows, w_t_fox, cum, cos_t, sin_t, tm):
    s = x.shape[0]
    r_va, r_za, r_qb, r_zb, r_kb, r_vb = 0, 512, 1024, 1536, 2048, 2176

    def body(x_ref, g_ref, sc_ref, sh_ref, w_ref, wt_ref, cum_ref, cos_ref, sin_ref,
             h_ref, qat_ref, ka_ref, kat_ref, v_ref, vt_ref, za_ref, zb_ref, qb_ref, kb_ref, vb_ref):
        hb = _modulated(x_ref, g_ref, sc_ref, sh_ref)
        h_ref[...] = hb

        def sec(c0, width):
            return _dot(hb, w_ref[:, c0:c0 + width])

        def sec_t(r0):
            return _dot_nt(wt_ref[r0:r0 + FOX_W, :], hb)

        q_t = sec_t(0) * (HEAD_DIM ** -0.5)
        k_t = sec_t(FOX_W)
        v_t = sec_t(2 * FOX_W)
        va = sec(r_va, FOX_W)
        zeros = jnp.zeros((AUG_DIM - HEAD_DIM - AUG_ROWS, tm), F32)
        ri = lax.broadcasted_iota(jnp.int32, (AUG_ROWS, tm), 0)
        const = jnp.where(ri == AUG_ROWS - 1, 0.0, 1.0)
        for hd in range(FOX_HEADS):
            rows = slice(hd * HEAD_DIM, (hd + 1) * HEAD_DIM)
            hi, mid, lo = (jnp.broadcast_to(part, (AUG_ROWS, tm)) for part in _split3(cum_ref[hd:hd + 1, :]))
            q_feat = jnp.where(ri == 1, hi, jnp.where(ri == 2, mid, jnp.where(ri == 3, lo, const)))
            k_feat = jnp.where(ri == 4, -hi, jnp.where(ri == 5, -mid, jnp.where(ri == 6, -lo, const)))
            q_aug = jnp.concatenate([q_t[rows], q_feat, zeros], axis=0)
            k_aug = jnp.concatenate([k_t[rows], k_feat, zeros], axis=0)
            qat_ref[hd] = q_aug.astype(BF)
            kat_ref[hd] = k_aug.astype(BF)
            ka_ref[hd] = k_aug.T.astype(BF)
            vt_ref[hd] = v_t[rows].astype(BF)
            v_ref[hd] = va[:, rows].astype(BF)
        za_ref[...] = sec(r_za, FOX_W)
        zb_ref[...] = sec(r_zb, SWA_W)
        cos2, sin2 = cos_ref[...], sin_ref[...]
        cos8 = jnp.concatenate([cos2] * 4, axis=1)
        sin8 = jnp.concatenate([sin2] * 4, axis=1)
        qb = sec(r_qb, SWA_W)
        qb = (qb * cos8 + _rope_partner(qb) * sin8) * (HEAD_DIM ** -0.5)
        qb_ref[...] = qb.astype(BF)
        kb = sec(r_kb, SWA_KV_W)
        kb = kb * cos2 + _rope_partner(kb) * sin2
        vb = sec(r_vb, SWA_KV_W)
        for hd in range(SWA_KV_HEADS):
            sl = slice(hd * HEAD_DIM, (hd + 1) * HEAD_DIM)
            kb_ref[hd] = kb[:, sl].astype(BF)
            vb_ref[hd] = vb[:, sl].astype(BF)

    row = lambda w: pl.BlockSpec((tm, w), lambda i: (i, 0))
    heads = lambda n, w=HEAD_DIM: pl.BlockSpec((n, tm, w), lambda i: (0, i, 0))
    heads_t = lambda w: pl.BlockSpec((FOX_HEADS, w, tm), lambda i: (0, 0, i))
    vec = _full((1, D_MODEL))
    hs = lambda a, b: jax.ShapeDtypeStruct((FOX_HEADS, a, b), BF)
    out_shape = (
        jax.ShapeDtypeStruct((s, D_MODEL), BF),
        hs(AUG_DIM, s), hs(s, AUG_DIM), hs(AUG_DIM, s), hs(s, HEAD_DIM), hs(HEAD_DIM, s),
        jax.ShapeDtypeStruct((s, FOX_W), F32), jax.ShapeDtypeStruct((s, SWA_W), F32),
        jax.ShapeDtypeStruct((s, SWA_W), BF),
        jax.ShapeDtypeStruct((SWA_KV_HEADS, s, HEAD_DIM), BF), jax.ShapeDtypeStruct((SWA_KV_HEADS, s, HEAD_DIM), BF),
    )
    return _call(
        body, name="in_proj", out_shape=out_shape, grid=(s // tm,),
        in_specs=[row(D_MODEL), vec, vec, vec, _full(w_rows.shape), _full(w_t_fox.shape),
                  pl.BlockSpec((FOX_HEADS, tm), lambda i: (0, i)), row(LANES), row(LANES)],
        out_specs=(row(D_MODEL), heads_t(AUG_DIM), heads(FOX_HEADS, AUG_DIM), heads_t(AUG_DIM), heads(FOX_HEADS),
                   heads_t(HEAD_DIM), row(FOX_W), row(SWA_W), row(SWA_W), heads(SWA_KV_HEADS), heads(SWA_KV_HEADS)),
        compiler_params=_params(("parallel",)),
    )(x, g_pre, scale1p, shift, w_rows, w_t_fox, cum, cos_t, sin_t)


def _causal_mask_t(i, j, c0, blk, width):
    kpos = j * blk + lax.broadcasted_iota(jnp.int32, (blk, width), 0)
    qpos = i * blk + c0 + lax.broadcasted_iota(jnp.int32, (blk, width), 1)
    return kpos <= qpos


def _fox_fwd(qat, ka, vt, blk, chunk):
    nh, _, s = qat.shape
    nb = s // blk

    def body(ka_ref, qat_ref, vt_ref, o_ref, lse_ref, m_scr, l_scr, acc_scr):
        i, j = pl.program_id(1), pl.program_id(2)

        @pl.when(j == 0)
        def _():
            m_scr[...] = jnp.full(m_scr.shape, NEG, F32)
            l_scr[...] = jnp.zeros(l_scr.shape, F32)
            acc_scr[...] = jnp.zeros(acc_scr.shape, F32)

        def step(masked):
            kv, vtv = ka_ref[0], vt_ref[0]
            for c0 in range(0, blk, chunk):
                cs = slice(c0, c0 + chunk)
                sc = _dot(kv, qat_ref[0, :, cs])
                if masked:
                    sc = jnp.where(_causal_mask_t(i, j, c0, blk, chunk), sc, NEG)
                m_prev = m_scr[:, cs]
                m_new = jnp.maximum(m_prev, jnp.max(sc, axis=0, keepdims=True))
                alpha = jnp.exp(m_prev - m_new)
                p = jnp.exp(sc - m_new)
                l_scr[:, cs] = alpha * l_scr[:, cs] + jnp.sum(p, axis=0, keepdims=True)
                acc_scr[:, cs] = alpha * acc_scr[:, cs] + _dot(vtv, p.astype(BF))
                m_scr[:, cs] = m_new

        @pl.when(j < i)
        def _():
            step(False)

        @pl.when(j == i)
        def _():
            step(True)
            l = l_scr[...]
            o_ref[0] = acc_scr[...] / l
            lse_ref[0] = m_scr[...] + jnp.log(l)

    return _call(
        body, name="fox_fwd",
        out_shape=(jax.ShapeDtypeStruct((nh, HEAD_DIM, s), F32), jax.ShapeDtypeStruct((nh, 1, s), F32)),
        grid=(nh, nb, nb),
        in_specs=[pl.BlockSpec((1, blk, AUG_DIM), lambda h, i, j: (h, jnp.minimum(j, i), 0)),
                  pl.BlockSpec((1, AUG_DIM, blk), lambda h, i, j: (h, 0, i)),
                  pl.BlockSpec((1, HEAD_DIM, blk), lambda h, i, j: (h, 0, jnp.minimum(j, i)))],
        out_specs=(pl.BlockSpec((1, HEAD_DIM, blk), lambda h, i, j: (h, 0, i)),
                   pl.BlockSpec((1, 1, blk), lambda h, i, j: (h, 0, i))),
        scratch_shapes=[pltpu.VMEM((1, blk), F32), pltpu.VMEM((1, blk), F32), pltpu.VMEM((HEAD_DIM, blk), F32)],
        compiler_params=_params(("parallel", "parallel", "arbitrary")),
    )(ka, qat, vt)


def _fox_bwd(qat, ka, kat, v, dot_, lse, delta, blk, chunk):
    nh, _, s = qat.shape
    nb = s // blk

    def body(ka_ref, kat_ref, v_ref, qat_ref, do_ref, lse_ref, dl_ref, dq_ref, dk_ref, dv_ref, dk_scr, dv_scr):
        j, i = pl.program_id(1), pl.program_id(2)

        @pl.when((j == 0) & (i == 0))
        def _():
            dq_ref[...] = jnp.zeros(dq_ref.shape, F32)

        @pl.when(i == 0)
        def _():
            dk_scr[...] = jnp.zeros(dk_scr.shape, F32)
            dv_scr[...] = jnp.zeros(dv_scr.shape, F32)

        def step(masked):
            kv, ktv, vv = ka_ref[0], kat_ref[0], v_ref[0]
            for c0 in range(0, blk, chunk):
                cs = slice(c0, c0 + chunk)
                qt, dot_v = qat_ref[0, :, cs], do_ref[0, :, cs]
                p = jnp.exp(_dot(kv, qt) - lse_ref[0, :, cs])
                if masked:
                    p = jnp.where(_causal_mask_t(i, j, c0, blk, chunk), p, 0.0)
                dv_scr[...] += _dot_nt(dot_v, p.astype(BF))
                ds = (p * (_dot(vv, dot_v) - dl_ref[0, :, cs])).astype(BF)
                dk_scr[...] += _dot_nt(qt, ds)
                dq_ref[0, i, :, cs] += _dot(ktv, ds)

        @pl.when(i > j)
        def _():
            step(False)

        @pl.when(i == j)
        def _():
            step(True)

        @pl.when(i == nb - 1)
        def _():
            dk_ref[0] = dk_scr[...]
            dv_ref[0] = dv_scr[...]

    qmap = lambda h, j, i: (h, 0, jnp.maximum(i, j))
    kmap_t = lambda h, j, i: (h, 0, j)
    return _call(
        body, name="fox_bwd",
        out_shape=(jax.ShapeDtypeStruct((nh, nb, AUG_DIM, blk), F32), jax.ShapeDtypeStruct((nh, AUG_DIM, s), F32),
                   jax.ShapeDtypeStruct((nh, HEAD_DIM, s), F32)),
        grid=(nh, nb, nb),
        in_specs=[pl.BlockSpec((1, blk, AUG_DIM), lambda h, j, i: (h, j, 0)), pl.BlockSpec((1, AUG_DIM, blk), kmap_t),
                  pl.BlockSpec((1, blk, HEAD_DIM), lambda h, j, i: (h, j, 0)),
                  pl.BlockSpec((1, AUG_DIM, blk), qmap), pl.BlockSpec((1, HEAD_DIM, blk), qmap),
                  pl.BlockSpec((1, 1, blk), qmap), pl.BlockSpec((1, 1, blk), qmap)],
        out_specs=(pl.BlockSpec((1, nb, AUG_DIM, blk), lambda h, j, i: (h, 0, 0, 0)),
                   pl.BlockSpec((1, AUG_DIM, blk), kmap_t), pl.BlockSpec((1, HEAD_DIM, blk), kmap_t)),
        scratch_shapes=[pltpu.VMEM((AUG_DIM, blk), F32), pltpu.VMEM((HEAD_DIM, blk), F32)],
        compiler_params=_params(("parallel", "arbitrary", "arbitrary")),
    )(ka, kat, v, qat, dot_, lse, delta)


def _swa_window(i, tq):
    kstart = pl.multiple_of(jnp.maximum(i * tq - WINDOW, 0), WINDOW)
    qpos = i * tq + lax.broadcasted_iota(jnp.int32, (tq, tq + WINDOW), 0)
    kpos = kstart + lax.broadcasted_iota(jnp.int32, (tq, tq + WINDOW), 1)
    rel = qpos - kpos
    return kstart, (rel >= 0) & (rel < WINDOW)


def _swa_probs(qh, kw, mask, sink):
    sc = jnp.where(mask, _dot_nt(qh, kw), NEG)
    m = jnp.maximum(jnp.max(sc, axis=1, keepdims=True), sink)
    p = jnp.exp(sc - m)
    e_sink = jnp.exp(sink - m)
    inv_l = 1.0 / (jnp.sum(p, axis=1, keepdims=True) + e_sink)
    return p * inv_l, e_sink * inv_l


def _swa_fwd(qb, kb, vb, sinks, tq):
    s = qb.shape[0]
    gw = SWA_GROUP * HEAD_DIM

    def body(q_ref, k_ref, v_ref, s_ref, o_ref):
        i = pl.program_id(1)
        kstart, mask = _swa_window(i, tq)
        kw = k_ref[0, pl.ds(kstart, tq + WINDOW), :]
        vw = v_ref[0, pl.ds(kstart, tq + WINDOW), :]
        qv = q_ref[...]
        sk = s_ref[0]
        outs = []
        for hh in range(SWA_GROUP):
            p, _ = _swa_probs(qv[:, hh * HEAD_DIM:(hh + 1) * HEAD_DIM], kw, mask, sk[:, hh:hh + 1])
            outs.append(_dot(p.astype(BF), vw))
        o_ref[...] = jnp.concatenate(outs, axis=1)

    kvspec = pl.BlockSpec((1, s, HEAD_DIM), lambda g, i: (g, 0, 0))
    return _call(
        body, name="swa_fwd", out_shape=jax.ShapeDtypeStruct((s, SWA_W), F32), grid=(SWA_KV_HEADS, s // tq),
        in_specs=[pl.BlockSpec((tq, gw), lambda g, i: (i, g)), kvspec, kvspec,
                  pl.BlockSpec((1, 1, SWA_GROUP), lambda g, i: (g, 0, 0))],
        out_specs=pl.BlockSpec((tq, gw), lambda g, i: (i, g)),
        compiler_params=_params(("parallel", "parallel")),
    )(qb, kb, vb, sinks)


def _swa_bwd(qb, kb, vb, sinks, dob, tq):
    s = qb.shape[0]
    gw = SWA_GROUP * HEAD_DIM

    def body(q_ref, k_ref, v_ref, s_ref, do_ref, dq_ref, dk_ref, dv_ref, ds_ref):
        i = pl.program_id(1)

        @pl.when(i == 0)
        def _():
            dk_ref[...] = jnp.zeros(dk_ref.shape, F32)
            dv_ref[...] = jnp.zeros(dv_ref.shape, F32)
            ds_ref[...] = jnp.zeros(ds_ref.shape, F32)

        kstart, mask = _swa_window(i, tq)
        win = pl.ds(kstart, tq + WINDOW)
        kw = k_ref[0, win, :]
        vw = v_ref[0, win, :]
        qv = q_ref[...]
        dov = do_ref[...]
        sk = s_ref[0]
        dqs, dsinks = [], []
        dk_acc = jnp.zeros((tq + WINDOW, HEAD_DIM), F32)
        dv_acc = jnp.zeros((tq + WINDOW, HEAD_DIM), F32)
        for hh in range(SWA_GROUP):
            sl = slice(hh * HEAD_DIM, (hh + 1) * HEAD_DIM)
            qh, doh = qv[:, sl], dov[:, sl]
            p, p_sink = _swa_probs(qh, kw, mask, sk[:, hh:hh + 1])
            dp = _dot_nt(doh, vw)
            delta = jnp.sum(p * dp, axis=1, keepdims=True)
            dsc = (p * (dp - delta)).astype(BF)
            dqs.append(_dot(dsc, kw))
            dk_acc = dk_acc + _dot_tn(dsc, qh)
            dv_acc = dv_acc + _dot_tn(p.astype(BF), doh)
            dsinks.append(-jnp.sum(p_sink * delta, axis=0, keepdims=True))
        dq_ref[...] = jnp.concatenate(dqs, axis=1)
        dk_ref[0, win, :] += dk_acc
        dv_ref[0, win, :] += dv_acc
        ds_ref[0] += jnp.concatenate(dsinks, axis=1)

    kvspec = pl.BlockSpec((1, s, HEAD_DIM), lambda g, i: (g, 0, 0))
    qspec = pl.BlockSpec((tq, gw), lambda g, i: (i, g))
    kvshape = jax.ShapeDtypeStruct((SWA_KV_HEADS, s, HEAD_DIM), F32)
    return _call(
        body, name="swa_bwd",
        out_shape=(jax.ShapeDtypeStruct((s, SWA_W), F32), kvshape, kvshape,
                   jax.ShapeDtypeStruct((SWA_KV_HEADS, 1, SWA_GROUP), F32)),
        grid=(SWA_KV_HEADS, s // tq),
        in_specs=[qspec, kvspec, kvspec, pl.BlockSpec((1, 1, SWA_GROUP), lambda g, i: (g, 0, 0)), qspec],
        out_specs=(qspec, kvspec, kvspec, pl.BlockSpec((1, 1, SWA_GROUP), lambda g, i: (g, 0, 0))),
        compiler_params=_params(("parallel", "arbitrary")),
    )(qb, kb, vb, sinks, dob)


def _pairs_to_rows(ref, n_rows=HEAD_DIM):
    parts = []
    for a in range(0, FOX_HEADS, 2):
        parts.append(jnp.concatenate([ref[a][:n_rows], ref[a + 1][:n_rows]], axis=0).T)
    return jnp.concatenate(parts, axis=1)


def _out_proj(oat, za, ob, zb, x, tgt, w_out, w_out_t, gate, g_post, tm):
    s = x.shape[0]

    def body(oat_ref, za_ref, ob_ref, zb_ref, x_ref, t_ref, w_ref, wt_ref, gate_ref, gp_ref,
             dout_ref, doat_ref, dla_ref, dza_ref, dob_ref, dzb_ref, gw_ref, dgate_ref, dgp_ref, loss_ref):
        i = pl.program_id(0)

        @pl.when(i == 0)
        def _():
            gw_ref[...] = jnp.zeros(gw_ref.shape, F32)
            dgate_ref[...] = jnp.zeros(dgate_ref.shape, F32)
            dgp_ref[...] = jnp.zeros(dgp_ref.shape, F32)
            loss_ref[...] = jnp.zeros(loss_ref.shape, F32)

        oa_v = _pairs_to_rows(oat_ref)
        ob_v = ob_ref[...]
        za_v, zb_v = za_ref[...], zb_ref[...]
        sga, sgb = _sigmoid(za_v), _sigmoid(zb_v)
        sila, silb = za_v * sga, zb_v * sgb
        u = jnp.concatenate([oa_v * sila, ob_v * silb], axis=1).astype(BF)
        yv = _dot(u, w_ref[...])
        yhat, rstd = _rms_hat(yv)
        gp, gate_v = gp_ref[...], gate_ref[...]
        nrm = yhat * gp
        diff = (x_ref[...] + gate_v * nrm) - t_ref[...]
        loss_ref[...] += 0.5 * jnp.sum(jnp.sum(diff * diff, axis=1, keepdims=True), axis=0, keepdims=True) / D_MODEL
        dout = diff * (1.0 / D_MODEL)
        dout_ref[...] = dout
        dgate_ref[...] += jnp.sum(dout * nrm, axis=0, keepdims=True)
        dn = dout * gate_v
        dgp_ref[...] += jnp.sum(dn * yhat, axis=0, keepdims=True)
        dyhat = dn * gp
        dy = (rstd * (dyhat - yhat * jnp.mean(dyhat * yhat, axis=1, keepdims=True))).astype(BF)
        gw_ref[...] += _dot_tn(u, dy)
        du = _dot(dy, wt_ref[...])
        dua, dub = du[:, :FOX_W], du[:, FOX_W:]
        doa = dua * sila
        for a in range(0, FOX_HEADS, 2):
            pair_t = doa[:, a * HEAD_DIM:(a + 2) * HEAD_DIM].T
            for hd, rows in ((a, slice(0, HEAD_DIM)), (a + 1, slice(HEAD_DIM, 2 * HEAD_DIM))):
                doat_ref[hd] = pair_t[rows].astype(BF)
                dla_ref[hd] = jnp.sum(pair_t[rows] * oat_ref[hd], axis=0, keepdims=True)
        dob_ref[...] = (dub * silb).astype(BF)
        dza_ref[...] = (dua * oa_v * (sga * (1.0 + za_v * (1.0 - sga)))).astype(BF)
        dzb_ref[...] = (dub * ob_v * (sgb * (1.0 + zb_v * (1.0 - sgb)))).astype(BF)

    row = lambda w: pl.BlockSpec((tm, w), lambda i: (i, 0))
    heads_t = lambda w: pl.BlockSpec((FOX_HEADS, w, tm), lambda i: (0, 0, i))
    vec = _full((1, D_MODEL))
    mat = _full((D_MODEL, D_MODEL))
    out_shape = (
        jax.ShapeDtypeStruct((s, D_MODEL), F32),
        jax.ShapeDtypeStruct((FOX_HEADS, HEAD_DIM, s), BF), jax.ShapeDtypeStruct((FOX_HEADS, 1, s), F32),
        jax.ShapeDtypeStruct((s, FOX_W), BF), jax.ShapeDtypeStruct((s, SWA_W), BF), jax.ShapeDtypeStruct((s, SWA_W), BF),
        jax.ShapeDtypeStruct((D_MODEL, D_MODEL), F32),
        jax.ShapeDtypeStruct((1, D_MODEL), F32), jax.ShapeDtypeStruct((1, D_MODEL), F32),
        jax.ShapeDtypeStruct((1, 1), F32),
    )
    return _call(
        body, name="out_proj", out_shape=out_shape, grid=(s // tm,),
        in_specs=[heads_t(HEAD_DIM), row(FOX_W), row(SWA_W), row(SWA_W), row(D_MODEL), row(D_MODEL), mat, mat, vec, vec],
        out_specs=(row(D_MODEL), heads_t(HEAD_DIM), heads_t(1), row(FOX_W), row(SWA_W), row(SWA_W), mat, vec, vec,
                   _full((1, 1))),
        compiler_params=_params(("arbitrary",)),
    )(oat, za, ob, zb, x, tgt, w_out, w_out_t, gate, g_post)


def _assemble_dproj(dqt, dkt, dvt, dza, dqb, dzb, dkb, dvb, df, cos_t, sin_t, tm):
    s = dza.shape[0]

    def body(dqt_ref, dkt_ref, dvt_ref, dza_ref, dqb_ref, dzb_ref, dkb_ref, dvb_ref, df_ref, cos_ref, sin_ref, o_ref):
        def cat(ref, n):
            return jnp.concatenate([ref[hd] for hd in range(n)], axis=1)

        cos2, sin2 = cos_ref[...], sin_ref[...]
        cos8 = jnp.concatenate([cos2] * 4, axis=1)
        sin8 = jnp.concatenate([sin2] * 4, axis=1)
        scale = HEAD_DIM ** -0.5
        o_ref[:, C_QA:C_QA + FOX_W] = (_pairs_to_rows(dqt_ref.at[:, 0]) * scale).astype(BF)
        o_ref[:, C_KA:C_KA + FOX_W] = _pairs_to_rows(dkt_ref).astype(BF)
        o_ref[:, C_VA:C_VA + FOX_W] = _pairs_to_rows(dvt_ref).astype(BF)
        o_ref[:, C_ZA:C_ZA + FOX_W] = dza_ref[...]
        dq = dqb_ref[...] * scale
        o_ref[:, C_QB:C_QB + SWA_W] = (dq * cos8 - _rope_partner(dq) * sin8).astype(BF)
        o_ref[:, C_ZB:C_ZB + SWA_W] = dzb_ref[...]
        dk = cat(dkb_ref, SWA_KV_HEADS)
        o_ref[:, C_KB:C_KB + SWA_KV_W] = (dk * cos2 - _rope_partner(dk) * sin2).astype(BF)
        o_ref[:, C_VB:C_VB + SWA_KV_W] = cat(dvb_ref, SWA_KV_HEADS).astype(BF)
        o_ref[:, C_F:C_F + LANES] = df_ref[...].astype(BF)

    row = lambda w: pl.BlockSpec((tm, w), lambda i: (i, 0))
    heads = lambda n: pl.BlockSpec((n, tm, HEAD_DIM), lambda i: (0, i, 0))
    heads_t = lambda w: pl.BlockSpec((FOX_HEADS, w, tm), lambda i: (0, 0, i))
    return _call(
        body, name="assemble_dproj", out_shape=jax.ShapeDtypeStruct((s, WP), BF), grid=(s // tm,),
        in_specs=[pl.BlockSpec((FOX_HEADS, 1, AUG_DIM, tm), lambda i: (0, i, 0, 0)), heads_t(AUG_DIM), heads_t(HEAD_DIM),
                  row(FOX_W), row(SWA_W), row(SWA_W), heads(SWA_KV_HEADS), heads(SWA_KV_HEADS),
                  row(LANES), row(LANES), row(LANES)],
        out_specs=row(WP), compiler_params=_params(("parallel",)),
    )(dqt, dkt, dvt, dza, dqb, dzb, dkb, dvb, df, cos_t, sin_t)


def _in_proj_bwd_x(dproj, w_al_t, x, dout, g_pre, scale1p, tm):
    s = x.shape[0]

    def body(dp_ref, wt_ref, x_ref, dout_ref, g_ref, sc_ref, gx_ref, dsh_ref, dsc_ref, dg_ref):
        i = pl.program_id(0)

        @pl.when(i == 0)
        def _():
            dsh_ref[...] = jnp.zeros(dsh_ref.shape, F32)
            dsc_ref[...] = jnp.zeros(dsc_ref.shape, F32)
            dg_ref[...] = jnp.zeros(dg_ref.shape, F32)

        dh = _dot(dp_ref[...], wt_ref[...])
        xhat, rstd = _rms_hat(x_ref[...])
        g, sc = g_ref[...], sc_ref[...]
        dsh_ref[...] += jnp.sum(dh, axis=0, keepdims=True)
        dhx = dh * xhat
        dsc_ref[...] += jnp.sum(dhx * g, axis=0, keepdims=True)
        dg_ref[...] += jnp.sum(dhx * sc, axis=0, keepdims=True)
        dxhat = dh * (g * sc)
        gx_ref[...] = dout_ref[...] + rstd * (dxhat - xhat * jnp.mean(dxhat * xhat, axis=1, keepdims=True))

    row = lambda w: pl.BlockSpec((tm, w), lambda i: (i, 0))
    vec = _full((1, D_MODEL))
    vshape = jax.ShapeDtypeStruct((1, D_MODEL), F32)
    return _call(
        body, name="in_proj_bwd_x", out_shape=(jax.ShapeDtypeStruct((s, D_MODEL), F32), vshape, vshape, vshape),
        grid=(s // tm,),
        in_specs=[row(WP), _full((WP, D_MODEL)), row(D_MODEL), row(D_MODEL), vec, vec],
        out_specs=(row(D_MODEL), vec, vec, vec), compiler_params=_params(("arbitrary",)),
    )(dproj, w_al_t, x, dout, g_pre, scale1p)


def _in_proj_bwd_w(h, dproj, tk, tn):
    s = h.shape[0]

    def body(h_ref, dp_ref, gw_ref):
        @pl.when(pl.program_id(1) == 0)
        def _():
            gw_ref[...] = jnp.zeros(gw_ref.shape, F32)

        gw_ref[...] += _dot_tn(h_ref[...], dp_ref[...])

    return _call(
        body, name="in_proj_bwd_w", out_shape=jax.ShapeDtypeStruct((D_MODEL, WP), F32), grid=(WP // tn, s // tk),
        in_specs=[pl.BlockSpec((tk, D_MODEL), lambda n, k: (k, 0)), pl.BlockSpec((tk, tn), lambda n, k: (k, n))],
        out_specs=pl.BlockSpec((D_MODEL, tn), lambda n, k: (0, n)),
        compiler_params=_params(("parallel", "arbitrary")),
    )(h, dproj)


def _align_w_in(w_cols):
    def part(name, width):
        return w_cols[:, _SRC[name]:_SRC[name] + width]

    fpad = jnp.pad(part("fa", FOX_HEADS), ((0, 0), (0, LANES - FOX_HEADS)))
    return jnp.concatenate([part("qa", FOX_W), part("ka", FOX_W), part("va", FOX_W), part("za", FOX_W),
                            part("qb", SWA_W), part("zb", SWA_W), part("kb", SWA_KV_W), part("vb", SWA_KV_W), fpad], axis=1)


def _unalign_w_in(g_al):
    def part(c0, width):
        return g_al[:, c0:c0 + width]

    return jnp.concatenate([part(C_QA, FOX_W), part(C_KA, FOX_W), part(C_VA, FOX_W), part(C_F, FOX_HEADS),
                            part(C_ZA, FOX_W), part(C_QB, SWA_W), part(C_KB, SWA_KV_W), part(C_VB, SWA_KV_W),
                            part(C_ZB, SWA_W)], axis=1)


def _rope_tables(positions):
    inv_freq = ROPE_THETA ** (-jnp.arange(HALF, dtype=F32) / HALF)
    ang = positions.astype(F32)[:, None] * inv_freq
    cos, sin = jnp.cos(ang), jnp.sin(ang)
    return jnp.concatenate([cos, cos, cos, cos], axis=1), jnp.concatenate([-sin, sin, -sin, sin], axis=1)


def _tiles(s):
    if s >= 4096:
        return dict(tm=512, blk=512, chunk=256, tq=256, tm_out=256, tk=512, tn=1152)
    return dict(tm=128, blk=128, chunk=128, tq=128, tm_out=128, tk=128, tn=1152)


def kernel(x, c, positions, w_ada, b_ada, g_pre, w_in, b_fgate, sinks, w_out, g_post, loss_target, m_w_ada, m_b_ada, m_g_pre, m_w_in, m_b_fgate, m_sinks, m_w_out, m_g_post, v_w_ada, v_b_ada, v_g_pre, v_w_in, v_b_fgate, v_sinks, v_w_out, v_g_post):
    s = x.shape[1]
    t = _tiles(s)
    nc = s // LANES
    rows = FOX_HEADS * nc
    me = 4 * lax.axis_index("x") + 2 * lax.axis_index("y") + lax.axis_index("c")
    chip = 2 * lax.axis_index("x") + lax.axis_index("y")
    core = lax.axis_index("c")
    x2, tgt = x[0], loss_target[0]

    c_all = _allgather_devices(c, "gather_c")[:, 0, :]
    a_all, mod_shard = _ada_shard(c_all, w_ada[0])
    mod_all = _allgather_devices(mod_shard, "gather_mod")
    mod_rows = lax.dynamic_index_in_dim(mod_all, me, axis=1, keepdims=False)
    mod = mod_rows.reshape(N_CHIPS, 2, W_ADA_SHARD)[:, 0, :].reshape(1, 3 * D_MODEL) + b_ada
    shift, scale1p, gate = mod[:, :D_MODEL], 1.0 + mod[:, D_MODEL:2 * D_MODEL], mod[:, 2 * D_MODEL:]

    w_in_pad = jnp.pad(w_in[0].astype(BF), ((0, 0), (0, W_IN_SHARD_PAD - W_IN_SHARD)))
    w_in_all = _allgather_chips(w_in_pad, "gather_w_in")
    w_cols = jnp.concatenate([w_in_all[k, :, :W_IN_SHARD] for k in range(N_CHIPS)], axis=1)
    w_al = _align_w_in(w_cols)
    w_al_t = w_al.T
    w_out_all = _allgather_chips(w_out[0].astype(BF), "gather_w_out").reshape(D_MODEL, D_MODEL)
    w_out_t = w_out_all.T

    cos_t, sin_t = _rope_tables(positions[0])

    f_pad = _forget_logits(x2, g_pre, scale1p, shift, w_al[:, C_F:], t["tm"])
    f_rows = f_pad[:, :FOX_HEADS].T.reshape(rows, LANES)
    bias_rows = jnp.repeat(b_fgate[0], nc)[:, None]
    cum = _log_forget_cumsum(f_rows, bias_rows, nc).reshape(FOX_HEADS, s)
    h, qat, ka, kat, va, vat, za, zb, qb, kb, vb = _in_proj(
        x2, g_pre, scale1p, shift, w_al[:, C_VA:C_F], w_al_t[:C_ZA], cum, cos_t, sin_t, t["tm"])
    oat, lse = _fox_fwd(qat, ka, vat, t["blk"], t["chunk"])
    sinks_g = sinks.reshape(SWA_KV_HEADS, 1, SWA_GROUP)
    ob = _swa_fwd(qb, kb, vb, sinks_g, t["tq"])

    dout, doat, delta_a, dza, dob, dzb, gw_out, dgate, dg_post, loss_part = _out_proj(
        oat, za, ob, zb, x2, tgt, w_out_all, w_out_t, gate, g_post, t["tm_out"])

    dqt, dkt, dvt = _fox_bwd(qat, ka, kat, va, doat, lse, delta_a, t["blk"], t["chunk"])
    dcum = dqt[:, :, HEAD_DIM, :].reshape(FOX_HEADS, s) - dkt[:, HEAD_DIM, :]
    df_rows, db_heads = _log_forget_cumsum_bwd(dcum.reshape(rows, LANES), f_rows, bias_rows, nc)
    df_pad = jnp.pad(df_rows.reshape(FOX_HEADS, s).T, ((0, 0), (0, LANES - FOX_HEADS)))
    dqb, dkb, dvb, dsinks = _swa_bwd(qb, kb, vb, sinks_g, dob, t["tq"])

    dproj = _assemble_dproj(dqt, dkt, dvt, dza, dqb, dzb, dkb, dvb, df_pad, cos_t, sin_t, t["blk"])
    grad_x, dshift, dscale, dg_pre = _in_proj_bwd_x(dproj, w_al_t, x2, dout, g_pre, scale1p, t["tm_out"])
    gw_in = _unalign_w_in(_in_proj_bwd_w(h, dproj, t["tk"], t["tn"]))

    pad_lane = lambda vrow: jnp.pad(vrow, ((0, 0), (0, LANES - vrow.shape[1])))
    packed = jnp.concatenate([dshift, dscale, dgate, dg_pre, dg_post,
                              pad_lane(db_heads.reshape(1, FOX_HEADS)), pad_lane(dsinks.reshape(1, FOX_HEADS)),
                              pad_lane(loss_part)], axis=1)
    parts = _allgather_devices(packed, "gather_partials")
    tot = _sum_devices(parts)
    loss = tot[0, P_LOSS]
    g_b_ada = tot[:, P_DMOD:P_DMOD + 3 * D_MODEL]
    g_g_pre = tot[:, P_GPRE:P_GPRE + D_MODEL]
    g_g_post = tot[:, P_GPOST:P_GPOST + D_MODEL]
    g_b_fgate = tot[:, P_BF:P_BF + FOX_HEADS]
    g_sinks = tot[:, P_SINK:P_SINK + FOX_HEADS]
    dm_shard = lax.dynamic_slice_in_dim(parts[:, 0, :3 * D_MODEL], chip * W_ADA_SHARD, W_ADA_SHARD, axis=1)
    g_w_ada = _grad_w_ada(a_all.T, dm_shard)

    gin = jnp.pad(gw_in.reshape(D_MODEL, N_CHIPS, W_IN_SHARD).transpose(1, 0, 2),
                  ((0, 0), (0, 0), (0, W_IN_SHARD_PAD - W_IN_SHARD)))
    gout = gw_out.reshape(N_CHIPS, D_MODEL, W_OUT_SHARD)
    gbig = jnp.concatenate([gin, gout], axis=2)
    half = D_MODEL // 2
    gw = W_IN_SHARD_PAD + W_OUT_SHARD
    keep = lax.dynamic_slice_in_dim(gbig, core * half, half, axis=1)
    give = lax.dynamic_slice_in_dim(gbig, (1 - core) * half, half, axis=1)
    got = _swap_sibling(give.reshape(N_CHIPS * half, gw), "swap_grad_halves")
    pair = _add(keep.reshape(N_CHIPS * half, gw), got, "add_pair").reshape(N_CHIPS, half, gw)
    from_chips = _exchange_chips(pair, "exchange_grad")
    mine = _sum_chips(from_chips, "sum_chips")
    other = _swap_sibling(mine, "swap_grad_result")
    lo = jnp.where(core == 0, mine, other)
    hi = jnp.where(core == 0, other, mine)
    gfull = jnp.concatenate([lo, hi], axis=0)
    g_w_in = gfull[:, :W_IN_SHARD]
    g_w_out = gfull[:, W_IN_SHARD_PAD:].reshape(W_OUT_SHARD, D_MODEL)

    grads = dict(w_ada=g_w_ada, b_ada=g_b_ada, g_pre=g_g_pre, w_in=g_w_in, b_fgate=g_b_fgate, sinks=g_sinks,
                 w_out=g_w_out, g_post=g_g_post)
    weights = dict(w_ada=w_ada, b_ada=b_ada, g_pre=g_pre, w_in=w_in, b_fgate=b_fgate, sinks=sinks, w_out=w_out, g_post=g_post)
    moms = dict(w_ada=m_w_ada, b_ada=m_b_ada, g_pre=m_g_pre, w_in=m_w_in, b_fgate=m_b_fgate, sinks=m_sinks, w_out=m_w_out, g_post=m_g_post)
    vars_ = dict(w_ada=v_w_ada, b_ada=v_b_ada, g_pre=v_g_pre, w_in=v_w_in, b_fgate=v_b_fgate, sinks=v_sinks, w_out=v_w_out, g_post=v_g_post)
    names = ["w_ada", "b_ada", "g_pre", "w_in", "b_fgate", "sinks", "w_out", "g_post"]
    g_out, d_out, m_out, v_out = [], [], [], []
    for n in names:
        shape = weights[n].shape
        w2 = weights[n].reshape(shape[-2], shape[-1])
        g2 = grads[n].reshape(w2.shape)
        d2, nm2, nv2 = _adamw(w2, g2, moms[n].reshape(w2.shape), vars_[n].reshape(w2.shape), "adamw_" + n)
        g_out.append(g2.reshape(shape))
        d_out.append(d2.reshape(shape))
        m_out.append(nm2.reshape(shape))
        v_out.append(nv2.reshape(shape))
    return (loss, grad_x.reshape(x.shape), *g_out, *d_out, *m_out, *v_out)
```

```python
import functools

import jax
import jax.numpy as jnp
from jax import lax
from jax.experimental import pallas as pl
from jax.experimental.pallas import tpu as pltpu

_INTERPRET = False

D_MODEL = 1024
HEAD_DIM = 64
HALF = HEAD_DIM // 2
AUG_DIM = 128
AUG_ROWS = 8
VT_ROWS = 80
LOG2E = 1.4426950408889634
LN2 = 0.6931471805599453
Q_SCALE = LOG2E * 64 ** -0.5
FOX_HEADS = 8
FOX_W = 512
SWA_W = 512
SWA_KV_HEADS = 2
SWA_GROUP = 4
SWA_KV_W = 128
WINDOW = 128
ROPE_THETA = 10000.0
RMS_EPS = 1e-6
IN_WIDTH = 3336
N_CHIPS = 4
N_DEV = 8
W_IN_SHARD = IN_WIDTH // N_CHIPS
W_IN_SHARD_PAD = 896
W_ADA_SHARD = 3 * D_MODEL // N_CHIPS
W_OUT_SHARD = D_MODEL // N_CHIPS
LANES = 128

_SRC = dict(qa=0, ka=512, va=1024, fa=1536, za=1544, qb=2056, kb=2568, vb=2696, zb=2824)
C_QA, C_KA, C_VA, C_ZA, C_QB, C_ZB, C_KB, C_VB, C_F = 0, 512, 1024, 1536, 2048, 2560, 3072, 3200, 3328
WP = 3456

ADAM_LR = 0.001
ADAM_B1 = 0.9
ADAM_B2 = 0.999
ADAM_EPS = 1e-08
ADAM_WD = 0.01
ADAM_STEP = 10

VMEM_LIMIT = 56 * 1024 * 1024
NEG = -1e30
MESH = pl.DeviceIdType.MESH
BF = jnp.bfloat16
F32 = jnp.float32

P_DMOD, P_GPRE, P_GPOST, P_BF, P_SINK, P_LOSS, P_LEN = 0, 3072, 4096, 5120, 5248, 5376, 5504


def _call(body, **kw):
    return pl.pallas_call(body, interpret=_INTERPRET, **kw)


def _params(sem=None, **kw):
    return pltpu.CompilerParams(dimension_semantics=sem, vmem_limit_bytes=VMEM_LIMIT, **kw)


def _full(shape):
    zeros = (0,) * len(shape)
    return pl.BlockSpec(shape, lambda *_: zeros)


def _dot(a, b):
    return jnp.dot(a, b, preferred_element_type=F32)


def _dot_nt(a, b):
    return lax.dot_general(a, b, (((1,), (1,)), ((), ())), preferred_element_type=F32)


def _dot_tn(a, b):
    return lax.dot_general(a, b, (((0,), (0,)), ((), ())), preferred_element_type=F32)


def _sigmoid(z):
    return 1.0 / (1.0 + jnp.exp(-z))


def _rope_partner(t):
    w = t.shape[-1]
    lane = lax.broadcasted_iota(jnp.int32, t.shape, t.ndim - 1)
    return jnp.where((lane & (HEAD_DIM - 1)) < HALF, pltpu.roll(t, w - HALF, t.ndim - 1), pltpu.roll(t, HALF, t.ndim - 1))


def _allgather_devices(v, name):
    r, cdim = v.shape
    masks = [(dx, dy, dc) for dx in (0, 1) for dy in (0, 1) for dc in (0, 1)][1:]

    def body(v_ref, out_ref, send_sems, recv_sems):
        x, y, c = lax.axis_index("x"), lax.axis_index("y"), lax.axis_index("c")
        me = 4 * x + 2 * y + c
        out_ref[me] = v_ref[...]
        copies = []
        for k, (dx, dy, dc) in enumerate(masks):
            cp = pltpu.make_async_remote_copy(
                src_ref=v_ref, dst_ref=out_ref.at[me], send_sem=send_sems.at[k], recv_sem=recv_sems.at[k],
                device_id=(x ^ dx, y ^ dy, c ^ dc), device_id_type=MESH)
            cp.start()
            copies.append(cp)
        for k, (dx, dy, dc) in enumerate(masks):
            peer = 4 * (x ^ dx) + 2 * (y ^ dy) + (c ^ dc)
            pltpu.make_async_remote_copy(
                src_ref=v_ref, dst_ref=out_ref.at[peer], send_sem=send_sems.at[k], recv_sem=recv_sems.at[k],
                device_id=(x ^ dx, y ^ dy, c ^ dc), device_id_type=MESH).wait_recv()
        for cp in copies:
            cp.wait_send()

    return _call(
        body, name=name, out_shape=jax.ShapeDtypeStruct((N_DEV, r, cdim), v.dtype),
        in_specs=[pl.BlockSpec(memory_space=pltpu.VMEM)], out_specs=pl.BlockSpec(memory_space=pltpu.VMEM),
        scratch_shapes=[pltpu.SemaphoreType.DMA((7,)), pltpu.SemaphoreType.DMA((7,))],
        compiler_params=pltpu.CompilerParams(has_side_effects=True),
    )(v)


def _allgather_chips(v, name):
    r, cdim = v.shape
    masks = [(1, 0), (0, 1), (1, 1)]

    def body(v_ref, out_ref, send_sems, recv_sems, local_sem):
        x, y, c = lax.axis_index("x"), lax.axis_index("y"), lax.axis_index("c")
        me = 2 * x + y
        mine = pltpu.make_async_copy(v_ref, out_ref.at[me], local_sem)
        mine.start()
        copies = []
        for k, (dx, dy) in enumerate(masks):
            cp = pltpu.make_async_remote_copy(
                src_ref=v_ref, dst_ref=out_ref.at[me], send_sem=send_sems.at[k], recv_sem=recv_sems.at[k],
                device_id=(x ^ dx, y ^ dy, c), device_id_type=MESH)
            cp.start()
            copies.append(cp)
        for k, (dx, dy) in enumerate(masks):
            peer = 2 * (x ^ dx) + (y ^ dy)
            pltpu.make_async_remote_copy(
                src_ref=v_ref, dst_ref=out_ref.at[peer], send_sem=send_sems.at[k], recv_sem=recv_sems.at[k],
                device_id=(x ^ dx, y ^ dy, c), device_id_type=MESH).wait_recv()
        for cp in copies:
            cp.wait_send()
        mine.wait()

    return _call(
        body, name=name, out_shape=jax.ShapeDtypeStruct((N_CHIPS, r, cdim), v.dtype),
        in_specs=[pl.BlockSpec(memory_space=pl.ANY)], out_specs=pl.BlockSpec(memory_space=pl.ANY),
        scratch_shapes=[pltpu.SemaphoreType.DMA((3,)), pltpu.SemaphoreType.DMA((3,)), pltpu.SemaphoreType.DMA],
        compiler_params=pltpu.CompilerParams(has_side_effects=True),
    )(v)


def _swap_sibling(v, name):
    def body(v_ref, out_ref, send_sem, recv_sem):
        x, y, c = lax.axis_index("x"), lax.axis_index("y"), lax.axis_index("c")
        cp = pltpu.make_async_remote_copy(
            src_ref=v_ref, dst_ref=out_ref, send_sem=send_sem, recv_sem=recv_sem,
            device_id=(x, y, 1 - c), device_id_type=MESH)
        cp.start()
        cp.wait()

    return _call(
        body, name=name, out_shape=jax.ShapeDtypeStruct(v.shape, v.dtype),
        in_specs=[pl.BlockSpec(memory_space=pl.ANY)], out_specs=pl.BlockSpec(memory_space=pl.ANY),
        scratch_shapes=[pltpu.SemaphoreType.DMA, pltpu.SemaphoreType.DMA],
        compiler_params=pltpu.CompilerParams(has_side_effects=True),
    )(v)


def _exchange_chips(v, name):
    _, r, cdim = v.shape
    masks = [(1, 0), (0, 1), (1, 1)]

    def body(v_ref, out_ref, send_sems, recv_sems, local_sem):
        x, y, c = lax.axis_index("x"), lax.axis_index("y"), lax.axis_index("c")
        me = 2 * x + y
        mine = pltpu.make_async_copy(v_ref.at[me], out_ref.at[me], local_sem)
        mine.start()
        copies = []
        for k, (dx, dy) in enumerate(masks):
            peer = 2 * (x ^ dx) + (y ^ dy)
            cp = pltpu.make_async_remote_copy(
                src_ref=v_ref.at[peer], dst_ref=out_ref.at[me], send_sem=send_sems.at[k], recv_sem=recv_sems.at[k],
                device_id=(x ^ dx, y ^ dy, c), device_id_type=MESH)
            cp.start()
            copies.append(cp)
        for k, (dx, dy) in enumerate(masks):
            peer = 2 * (x ^ dx) + (y ^ dy)
            pltpu.make_async_remote_copy(
                src_ref=v_ref.at[me], dst_ref=out_ref.at[peer], send_sem=send_sems.at[k], recv_sem=recv_sems.at[k],
                device_id=(x ^ dx, y ^ dy, c), device_id_type=MESH).wait_recv()
        for cp in copies:
            cp.wait_send()
        mine.wait()

    return _call(
        body, name=name, out_shape=jax.ShapeDtypeStruct(v.shape, v.dtype),
        in_specs=[pl.BlockSpec(memory_space=pl.ANY)], out_specs=pl.BlockSpec(memory_space=pl.ANY),
        scratch_shapes=[pltpu.SemaphoreType.DMA((3,)), pltpu.SemaphoreType.DMA((3,)), pltpu.SemaphoreType.DMA],
        compiler_params=pltpu.CompilerParams(has_side_effects=True),
    )(v)


def _ada_shard(c_all, w_ada_shard):
    def body(c_ref, w_ref, a_ref, mod_ref):
        cv = c_ref[...]
        a = cv * _sigmoid(cv)
        a_ref[...] = a
        mod_ref[...] = _dot(a.astype(BF), w_ref[...].astype(BF))

    return _call(
        body, name="ada_shard",
        out_shape=(jax.ShapeDtypeStruct((N_DEV, D_MODEL), F32), jax.ShapeDtypeStruct((N_DEV, W_ADA_SHARD), F32)),
        compiler_params=_params(),
    )(c_all, w_ada_shard)


def _grad_w_ada(a_t, dm_shard):
    def body(a_ref, dm_ref, out_ref):
        acc = jnp.zeros((D_MODEL, W_ADA_SHARD), F32)
        for b in range(N_DEV):
            acc = acc + a_ref[:, b:b + 1] * dm_ref[b:b + 1, :]
        out_ref[...] = acc

    return _call(body, name="grad_w_ada", out_shape=jax.ShapeDtypeStruct((D_MODEL, W_ADA_SHARD), F32),
                 compiler_params=_params())(a_t, dm_shard)


def _sum_devices(parts):
    n = parts.shape[-1]

    def body(p_ref, out_ref):
        acc = p_ref[0]
        for b in range(1, N_DEV):
            acc = acc + p_ref[b]
        out_ref[...] = acc

    return _call(body, name="sum_devices", out_shape=jax.ShapeDtypeStruct((1, n), F32), compiler_params=_params())(parts)


def _add(a, b, name):
    r, cdim = a.shape
    tr = min(r, 256)

    def body(a_ref, b_ref, o_ref):
        o_ref[...] = a_ref[...] + b_ref[...]

    spec = pl.BlockSpec((tr, cdim), lambda i: (i, 0))
    return _call(body, name=name, out_shape=jax.ShapeDtypeStruct(a.shape, F32), grid=(r // tr,),
                 in_specs=[spec, spec], out_specs=spec, compiler_params=_params(("parallel",)))(a, b)


def _sum_chips(parts, name):
    _, r, cdim = parts.shape
    tr = min(r, 128)

    def body(p_ref, o_ref):
        o_ref[...] = ((p_ref[0] + p_ref[1]) + p_ref[2]) + p_ref[3]

    return _call(body, name=name, out_shape=jax.ShapeDtypeStruct((r, cdim), F32), grid=(r // tr,),
                 in_specs=[pl.BlockSpec((N_CHIPS, tr, cdim), lambda i: (0, i, 0))],
                 out_specs=pl.BlockSpec((tr, cdim), lambda i: (i, 0)), compiler_params=_params(("parallel",)))(parts)


def _adamw(w, g, m, v, name):
    r, cdim = w.shape
    tr = r if r <= 256 else 256
    c1 = 1.0 / (1.0 - ADAM_B1 ** ADAM_STEP)
    c2 = 1.0 / (1.0 - ADAM_B2 ** ADAM_STEP)

    def body(w_ref, g_ref, m_ref, v_ref, d_ref, nm_ref, nv_ref):
        gv = g_ref[...]
        nm = ADAM_B1 * m_ref[...] + (1.0 - ADAM_B1) * gv
        nv = ADAM_B2 * v_ref[...] + (1.0 - ADAM_B2) * (gv * gv)
        m_hat = nm * c1
        v_hat = nv * c2
        d_ref[...] = -ADAM_LR * (m_hat / (jnp.sqrt(v_hat) + ADAM_EPS) + ADAM_WD * w_ref[...])
        nm_ref[...] = nm
        nv_ref[...] = nv

    spec = pl.BlockSpec((tr, cdim), lambda i: (i, 0))
    shp = jax.ShapeDtypeStruct(w.shape, F32)
    return _call(body, name=name, out_shape=(shp, shp, shp), grid=(r // tr,), in_specs=[spec] * 4,
                 out_specs=(spec, spec, spec), compiler_params=_params(("parallel",)))(w, g, m, v)


def _head_of_row(r, nc):
    assert nc & (nc - 1) == 0
    return lax.shift_right_logical(r, nc.bit_length() - 1)


def _chunk_mats(rows, nc, reverse):
    ri = lax.broadcasted_iota(jnp.int32, (rows, rows), 0)
    ci = lax.broadcasted_iota(jnp.int32, (rows, rows), 1)
    same = _head_of_row(ri, nc) == _head_of_row(ci, nc)
    between = jnp.where(same & ((ci > ri) if reverse else (ci < ri)), 1.0, 0.0).astype(F32)
    li = lax.broadcasted_iota(jnp.int32, (LANES, LANES), 0)
    lj = lax.broadcasted_iota(jnp.int32, (LANES, LANES), 1)
    within = jnp.where((li >= lj) if reverse else (li <= lj), 1.0, 0.0).astype(F32)
    return between, within


def _dot_hi(a, b):
    return jnp.dot(a, b, preferred_element_type=F32, precision=lax.Precision.HIGHEST)


def _scan_rows(t, nc, reverse):
    between, within = _chunk_mats(t.shape[0], nc, reverse)
    inner = _dot_hi(t, within)
    tot = jnp.sum(t, axis=1, keepdims=True)
    return inner + _dot_hi(between, jnp.broadcast_to(tot, t.shape))


def _log_forget_cumsum(f_rows, bias_rows, nc):
    def body(f_ref, b_ref, cum_ref):
        z = f_ref[...] + b_ref[...]
        lf = jnp.minimum(z, 0.0) - jnp.log(1.0 + jnp.exp(-jnp.abs(z)))
        cum_ref[...] = _scan_rows(lf, nc, False)

    return _call(body, name="forget_cumsum", out_shape=jax.ShapeDtypeStruct(f_rows.shape, F32),
                 compiler_params=_params())(f_rows, bias_rows)


def _log_forget_cumsum_bwd(dcum_rows, f_rows, bias_rows, nc):
    rows = f_rows.shape[0]

    def body(d_ref, f_ref, b_ref, df_ref, db_ref):
        dlf = _scan_rows(d_ref[...], nc, True)
        z = f_ref[...] + b_ref[...]
        df = dlf * _sigmoid(-z)
        df_ref[...] = df
        hi = lax.broadcasted_iota(jnp.int32, (FOX_HEADS, rows), 0)
        ri = lax.broadcasted_iota(jnp.int32, (FOX_HEADS, rows), 1)
        sel = jnp.where(_head_of_row(ri, nc) == hi, 1.0, 0.0).astype(F32)
        db_ref[...] = jnp.sum(_dot_hi(sel, df), axis=1, keepdims=True)

    return _call(body, name="forget_cumsum_bwd",
                 out_shape=(jax.ShapeDtypeStruct(f_rows.shape, F32), jax.ShapeDtypeStruct((FOX_HEADS, 1), F32)),
                 compiler_params=_params())(dcum_rows, f_rows, bias_rows)


def _rms_hat(xv):
    rstd = lax.rsqrt(jnp.mean(xv * xv, axis=-1, keepdims=True) + RMS_EPS)
    return xv * rstd, rstd


def _modulated(x_ref, g_ref, sc_ref, sh_ref):
    xhat, _ = _rms_hat(x_ref[...])
    return ((xhat * g_ref[...]) * sc_ref[...] + sh_ref[...]).astype(BF)


def _forget_logits(x, g_pre, scale1p, shift, w_f, tm):
    s = x.shape[0]

    def body(x_ref, g_ref, sc_ref, sh_ref, w_ref, f_ref):
        f_ref[...] = _dot(_modulated(x_ref, g_ref, sc_ref, sh_ref), w_ref[...])

    vec = _full((1, D_MODEL))
    return _call(
        body, name="forget_logits", out_shape=jax.ShapeDtypeStruct((s, LANES), F32), grid=(s // tm,),
        in_specs=[pl.BlockSpec((tm, D_MODEL), lambda i: (i, 0)), vec, vec, vec, _full((D_MODEL, LANES))],
        out_specs=pl.BlockSpec((tm, LANES), lambda i: (i, 0)), compiler_params=_params(("parallel",)),
    )(x, g_pre, scale1p, shift, w_f)


def _split3(v):
    hi = v.astype(BF).astype(F32)
    mid = (v - hi).astype(BF).astype(F32)
    lo = ((v - hi) - mid).astype(BF).astype(F32)
    return hi, mid, lo


def _in_proj(x, g_pre, scale1p, shift, w_rows, w_t_fox, cum, cos_t, sin_t, tm):
    s = x.shape[0]
    r_va, r_za, r_qb, r_zb, r_kb, r_vb = 0, 512, 1024, 1536, 2048, 2176

    def body(x_ref, g_ref, sc_ref, sh_ref, w_ref, wt_ref, cum_ref, cos_ref, sin_ref,
             h_ref, qat_ref, ka_ref, kat_ref, v_ref, vt_ref, za_ref, zb_ref, qb_ref, kb_ref, vb_ref):
        hb = _modulated(x_ref, g_ref, sc_ref, sh_ref)
        h_ref[...] = hb

        def sec(c0, width):
            return _dot(hb, w_ref[:, c0:c0 + width])

        def sec_t(r0):
            return _dot_nt(wt_ref[r0:r0 + FOX_W, :], hb)

        q_t = sec_t(0) * Q_SCALE
        k_t = sec_t(FOX_W)
        v_t = sec_t(2 * FOX_W)
        va = sec(r_va, FOX_W)
        zeros = jnp.zeros((AUG_DIM - HEAD_DIM - AUG_ROWS, tm), F32)
        ri = lax.broadcasted_iota(jnp.int32, (AUG_ROWS, tm), 0)
        const = jnp.where(ri == AUG_ROWS - 1, 0.0, 1.0)
        ri_v = lax.broadcasted_iota(jnp.int32, (VT_ROWS - HEAD_DIM, tm), 0)
        v_feat = jnp.where(ri_v == 0, 1.0, 0.0).astype(BF)
        for hd in range(FOX_HEADS):
            rows = slice(hd * HEAD_DIM, (hd + 1) * HEAD_DIM)
            cum2 = cum_ref[hd:hd + 1, :] * LOG2E
            hi, mid, lo = (jnp.broadcast_to(part, (AUG_ROWS, tm)) for part in _split3(cum2))
            q_feat = jnp.where(ri == 1, hi, jnp.where(ri == 2, mid, jnp.where(ri == 3, lo, const)))
            k_feat = jnp.where(ri == 4, -hi, jnp.where(ri == 5, -mid, jnp.where(ri == 6, -lo, const)))
            q_aug = jnp.concatenate([q_t[rows], q_feat, zeros], axis=0)
            k_aug = jnp.concatenate([k_t[rows], k_feat, zeros], axis=0)
            qat_ref[hd] = q_aug.astype(BF)
            kat_ref[hd] = k_aug.astype(BF)
            ka_ref[hd] = k_aug.T.astype(BF)
            vt_ref[hd] = jnp.concatenate([v_t[rows].astype(BF), v_feat], axis=0)
            v_ref[hd] = va[:, rows].astype(BF)
        za_ref[...] = sec(r_za, FOX_W)
        zb_ref[...] = sec(r_zb, SWA_W)
        cos2, sin2 = cos_ref[...], sin_ref[...]
        cos8 = jnp.concatenate([cos2] * 4, axis=1)
        sin8 = jnp.concatenate([sin2] * 4, axis=1)
        qb = sec(r_qb, SWA_W)
        qb = (qb * cos8 + _rope_partner(qb) * sin8) * (HEAD_DIM ** -0.5)
        qb_ref[...] = qb.astype(BF)
        kb = sec(r_kb, SWA_KV_W)
        kb = kb * cos2 + _rope_partner(kb) * sin2
        vb = sec(r_vb, SWA_KV_W)
        for hd in range(SWA_KV_HEADS):
            sl = slice(hd * HEAD_DIM, (hd + 1) * HEAD_DIM)
            kb_ref[hd] = kb[:, sl].astype(BF)
            vb_ref[hd] = vb[:, sl].astype(BF)

    row = lambda w: pl.BlockSpec((tm, w), lambda i: (i, 0))
    heads = lambda n, w=HEAD_DIM: pl.BlockSpec((n, tm, w), lambda i: (0, i, 0))
    heads_t = lambda w: pl.BlockSpec((FOX_HEADS, w, tm), lambda i: (0, 0, i))
    vec = _full((1, D_MODEL))
    hs = lambda a, b: jax.ShapeDtypeStruct((FOX_HEADS, a, b), BF)
    out_shape = (
        jax.ShapeDtypeStruct((s, D_MODEL), BF),
        hs(AUG_DIM, s), hs(s, AUG_DIM), hs(AUG_DIM, s), hs(s, HEAD_DIM), hs(VT_ROWS, s),
        jax.ShapeDtypeStruct((s, FOX_W), F32), jax.ShapeDtypeStruct((s, SWA_W), F32),
        jax.ShapeDtypeStruct((s, SWA_W), BF),
        jax.ShapeDtypeStruct((SWA_KV_HEADS, s, HEAD_DIM), BF), jax.ShapeDtypeStruct((SWA_KV_HEADS, s, HEAD_DIM), BF),
    )
    return _call(
        body, name="in_proj", out_shape=out_shape, grid=(s // tm,),
        in_specs=[row(D_MODEL), vec, vec, vec, _full(w_rows.shape), _full(w_t_fox.shape),
                  pl.BlockSpec((FOX_HEADS, tm), lambda i: (0, i)), row(LANES), row(LANES)],
        out_specs=(row(D_MODEL), heads_t(AUG_DIM), heads(FOX_HEADS, AUG_DIM), heads_t(AUG_DIM), heads(FOX_HEADS),
                   heads_t(VT_ROWS), row(FOX_W), row(SWA_W), row(SWA_W), heads(SWA_KV_HEADS), heads(SWA_KV_HEADS)),
        compiler_params=_params(("parallel",)),
    )(x, g_pre, scale1p, shift, w_rows, w_t_fox, cum, cos_t, sin_t)


def _diag_chunks(d, bq, bk, chunk):
    out = []
    for c0 in range(0, bq, chunk):
        if d is None or d * bk + bk - 1 <= c0:
            out.append((c0, None))
        elif d * bk <= c0 + chunk - 1:
            kpos = d * bk + lax.broadcasted_iota(jnp.int32, (bk, chunk), 0)
            qpos = c0 + lax.broadcasted_iota(jnp.int32, (bk, chunk), 1)
            out.append((c0, kpos <= qpos))
    return out


def _fox_fwd(qat, ka, vt, bq, bk, chunk):
    nh, _, s = qat.shape
    r = bq // bk

    def body(ka_ref, qat_ref, vt_ref, o_ref, lse_ref, m_scr, acc_scr):
        i, j = pl.program_id(1), pl.program_id(2)

        @pl.when(j == 0)
        def _():
            m_scr[...] = jnp.full(m_scr.shape, NEG, F32)
            acc_scr[...] = jnp.zeros(acc_scr.shape, F32)

        def step(d):
            kv, vtv = ka_ref[0], vt_ref[0]
            todo = _diag_chunks(d, bq, bk, chunk)
            scores = lambda c0: _dot(kv, qat_ref[0, :, c0:c0 + chunk])
            sc_next = scores(todo[0][0])
            for n, (c0, mask) in enumerate(todo):
                cs = slice(c0, c0 + chunk)
                sc = sc_next
                if n + 1 < len(todo):
                    sc_next = scores(todo[n + 1][0])
                if mask is not None:
                    sc = jnp.where(mask, sc, NEG)
                m_prev = m_scr[:, cs]
                m_new = jnp.maximum(m_prev, jnp.max(sc, axis=0, keepdims=True))
                p = jnp.exp2(sc - m_new).astype(BF)
                acc_scr[:, cs] = jnp.exp2(m_prev - m_new) * acc_scr[:, cs] + _dot(vtv, p)
                m_scr[:, cs] = m_new

        @pl.when(j < i * r)
        def _():
            step(None)

        for d in range(r):
            @pl.when(j == i * r + d)
            def _(d=d):
                step(d)

        @pl.when(j == i * r + r - 1)
        def _():
            l = acc_scr[HEAD_DIM:HEAD_DIM + 1, :]
            o_ref[0] = acc_scr[:HEAD_DIM, :] / l
            lse_ref[0] = m_scr[...] + jnp.log2(l)

    kmap = lambda h, i, j: (h, jnp.minimum(j, i * r + r - 1), 0)
    kmap_t = lambda h, i, j: (h, 0, jnp.minimum(j, i * r + r - 1))
    return _call(
        body, name="fox_fwd",
        out_shape=(jax.ShapeDtypeStruct((nh, HEAD_DIM, s), F32), jax.ShapeDtypeStruct((nh, 1, s), F32)),
        grid=(nh, s // bq, s // bk),
        in_specs=[pl.BlockSpec((1, bk, AUG_DIM), kmap), pl.BlockSpec((1, AUG_DIM, bq), lambda h, i, j: (h, 0, i)),
                  pl.BlockSpec((1, VT_ROWS, bk), kmap_t)],
        out_specs=(pl.BlockSpec((1, HEAD_DIM, bq), lambda h, i, j: (h, 0, i)),
                   pl.BlockSpec((1, 1, bq), lambda h, i, j: (h, 0, i))),
        scratch_shapes=[pltpu.VMEM((1, bq), F32), pltpu.VMEM((VT_ROWS, bq), F32)],
        compiler_params=_params(("parallel", "parallel", "arbitrary")),
    )(ka, qat, vt)


def _fox_bwd(qat, ka, kat, v, dot_, lse, delta, bq, bk, chunk, dq_blk):
    nh, _, s = qat.shape
    r = bq // bk
    nq = s // bq

    def body(ka_ref, kat_ref, v_ref, qat_ref, do_ref, lse_ref, dl_ref, dq_ref, dk_ref, dv_ref, dk_scr, dv_scr):
        j, i = pl.program_id(1), pl.program_id(2)

        @pl.when((j == 0) & (i == 0))
        def _():
            dq_ref[...] = jnp.zeros(dq_ref.shape, F32)

        @pl.when(i == 0)
        def _():
            dk_scr[...] = jnp.zeros(dk_scr.shape, F32)
            dv_scr[...] = jnp.zeros(dv_scr.shape, F32)

        def step(d):
            kv, ktv, vv = ka_ref[0], kat_ref[0], v_ref[0]
            todo = _diag_chunks(d, bq, bk, chunk)

            def products(c0):
                cs = slice(c0, c0 + chunk)
                return _dot(kv, qat_ref[0, :, cs]), _dot(vv, do_ref[0, :, cs])

            nxt = products(todo[0][0])
            for n, (c0, mask) in enumerate(todo):
                cs = slice(c0, c0 + chunk)
                sc, dp = nxt
                if n + 1 < len(todo):
                    nxt = products(todo[n + 1][0])
                p = jnp.exp2(sc - lse_ref[0, :, cs])
                if mask is not None:
                    p = jnp.where(mask, p, 0.0)
                ds = (p * (dp - dl_ref[0, :, cs])).astype(BF)
                dv_scr[...] += _dot_nt(do_ref[0, :, cs], p.astype(BF))
                dk_scr[...] += _dot_nt(qat_ref[0, :, cs], ds)
                c1 = c0 % dq_blk
                dq_ref[0, i * (bq // dq_blk) + c0 // dq_blk, :, c1:c1 + chunk] += _dot(ktv, ds)

        @pl.when(i * r > j)
        def _():
            step(None)

        for d in range(r):
            @pl.when(j == i * r + d)
            def _(d=d):
                step(d)

        @pl.when(i == nq - 1)
        def _():
            dk_ref[0] = dk_scr[...]
            dv_ref[0] = dv_scr[...]

    qmap = lambda h, j, i: (h, 0, jnp.maximum(i, j // r))
    kmap_t = lambda h, j, i: (h, 0, j)
    return _call(
        body, name="fox_bwd",
        out_shape=(jax.ShapeDtypeStruct((nh, s // dq_blk, AUG_DIM, dq_blk), F32),
                   jax.ShapeDtypeStruct((nh, AUG_DIM, s), F32), jax.ShapeDtypeStruct((nh, HEAD_DIM, s), F32)),
        grid=(nh, s // bk, nq),
        in_specs=[pl.BlockSpec((1, bk, AUG_DIM), lambda h, j, i: (h, j, 0)), pl.BlockSpec((1, AUG_DIM, bk), kmap_t),
                  pl.BlockSpec((1, bk, HEAD_DIM), lambda h, j, i: (h, j, 0)),
                  pl.BlockSpec((1, AUG_DIM, bq), qmap), pl.BlockSpec((1, HEAD_DIM, bq), qmap),
                  pl.BlockSpec((1, 1, bq), qmap), pl.BlockSpec((1, 1, bq), qmap)],
        out_specs=(pl.BlockSpec((1, s // dq_blk, AUG_DIM, dq_blk), lambda h, j, i: (h, 0, 0, 0)),
                   pl.BlockSpec((1, AUG_DIM, bk), kmap_t), pl.BlockSpec((1, HEAD_DIM, bk), kmap_t)),
        scratch_shapes=[pltpu.VMEM((AUG_DIM, bk), F32), pltpu.VMEM((HEAD_DIM, bk), F32)],
        compiler_params=_params(("parallel", "arbitrary", "arbitrary")),
    )(ka, kat, v, qat, dot_, lse, delta)


def _swa_window(i, tq):
    kstart = pl.multiple_of(jnp.maximum(i * tq - WINDOW, 0), WINDOW)
    qpos = i * tq + lax.broadcasted_iota(jnp.int32, (tq, tq + WINDOW), 0)
    kpos = kstart + lax.broadcasted_iota(jnp.int32, (tq, tq + WINDOW), 1)
    rel = qpos - kpos
    return kstart, (rel >= 0) & (rel < WINDOW)


def _swa_probs(qh, kw, mask, sink):
    sc = jnp.where(mask, _dot_nt(qh, kw), NEG)
    m = jnp.maximum(jnp.max(sc, axis=1, keepdims=True), sink)
    p = jnp.exp(sc - m)
    e_sink = jnp.exp(sink - m)
    inv_l = 1.0 / (jnp.sum(p, axis=1, keepdims=True) + e_sink)
    return p * inv_l, e_sink * inv_l


def _swa_fwd(qb, kb, vb, sinks, tq):
    s = qb.shape[0]
    gw = SWA_GROUP * HEAD_DIM

    def body(q_ref, k_ref, v_ref, s_ref, o_ref):
        i = pl.program_id(1)
        kstart, mask = _swa_window(i, tq)
        kw = k_ref[0, pl.ds(kstart, tq + WINDOW), :]
        vw = v_ref[0, pl.ds(kstart, tq + WINDOW), :]
        qv = q_ref[...]
        sk = s_ref[0]
        outs = []
        for hh in range(SWA_GROUP):
            p, _ = _swa_probs(qv[:, hh * HEAD_DIM:(hh + 1) * HEAD_DIM], kw, mask, sk[:, hh:hh + 1])
            outs.append(_dot(p.astype(BF), vw))
        o_ref[...] = jnp.concatenate(outs, axis=1)

    kvspec = pl.BlockSpec((1, s, HEAD_DIM), lambda g, i: (g, 0, 0))
    return _call(
        body, name="swa_fwd", out_shape=jax.ShapeDtypeStruct((s, SWA_W), F32), grid=(SWA_KV_HEADS, s // tq),
        in_specs=[pl.BlockSpec((tq, gw), lambda g, i: (i, g)), kvspec, kvspec,
                  pl.BlockSpec((1, 1, SWA_GROUP), lambda g, i: (g, 0, 0))],
        out_specs=pl.BlockSpec((tq, gw), lambda g, i: (i, g)),
        compiler_params=_params(("parallel", "parallel")),
    )(qb, kb, vb, sinks)


def _swa_bwd(qb, kb, vb, sinks, dob, tq):
    s = qb.shape[0]
    gw = SWA_GROUP * HEAD_DIM

    def body(q_ref, k_ref, v_ref, s_ref, do_ref, dq_ref, dk_ref, dv_ref, ds_ref):
        i = pl.program_id(1)

        @pl.when(i == 0)
        def _():
            dk_ref[...] = jnp.zeros(dk_ref.shape, F32)
            dv_ref[...] = jnp.zeros(dv_ref.shape, F32)
            ds_ref[...] = jnp.zeros(ds_ref.shape, F32)

        kstart, mask = _swa_window(i, tq)
        win = pl.ds(kstart, tq + WINDOW)
        kw = k_ref[0, win, :]
        vw = v_ref[0, win, :]
        qv = q_ref[...]
        dov = do_ref[...]
        sk = s_ref[0]
        dqs, dsinks = [], []
        dk_acc = jnp.zeros((tq + WINDOW, HEAD_DIM), F32)
        dv_acc = jnp.zeros((tq + WINDOW, HEAD_DIM), F32)
        for hh in range(SWA_GROUP):
            sl = slice(hh * HEAD_DIM, (hh + 1) * HEAD_DIM)
            qh, doh = qv[:, sl], dov[:, sl]
            p, p_sink = _swa_probs(qh, kw, mask, sk[:, hh:hh + 1])
            dp = _dot_nt(doh, vw)
            delta = jnp.sum(p * dp, axis=1, keepdims=True)
            dsc = (p * (dp - delta)).astype(BF)
            dqs.append(_dot(dsc, kw))
            dk_acc = dk_acc + _dot_tn(dsc, qh)
            dv_acc = dv_acc + _dot_tn(p.astype(BF), doh)
            dsinks.append(-jnp.sum(p_sink * delta, axis=0, keepdims=True))
        dq_ref[...] = jnp.concatenate(dqs, axis=1)
        dk_ref[0, win, :] += dk_acc
        dv_ref[0, win, :] += dv_acc
        ds_ref[0] += jnp.concatenate(dsinks, axis=1)

    kvspec = pl.BlockSpec((1, s, HEAD_DIM), lambda g, i: (g, 0, 0))
    qspec = pl.BlockSpec((tq, gw), lambda g, i: (i, g))
    kvshape = jax.ShapeDtypeStruct((SWA_KV_HEADS, s, HEAD_DIM), F32)
    return _call(
        body, name="swa_bwd",
        out_shape=(jax.ShapeDtypeStruct((s, SWA_W), F32), kvshape, kvshape,
                   jax.ShapeDtypeStruct((SWA_KV_HEADS, 1, SWA_GROUP), F32)),
        grid=(SWA_KV_HEADS, s // tq),
        in_specs=[qspec, kvspec, kvspec, pl.BlockSpec((1, 1, SWA_GROUP), lambda g, i: (g, 0, 0)), qspec],
        out_specs=(qspec, kvspec, kvspec, pl.BlockSpec((1, 1, SWA_GROUP), lambda g, i: (g, 0, 0))),
        compiler_params=_params(("parallel", "arbitrary")),
    )(qb, kb, vb, sinks, dob)


def _pairs_to_rows(ref, n_rows=HEAD_DIM):
    parts = []
    for a in range(0, FOX_HEADS, 2):
        parts.append(jnp.concatenate([ref[a][:n_rows], ref[a + 1][:n_rows]], axis=0).T)
    return jnp.concatenate(parts, axis=1)


def _out_proj(oat, za, ob, zb, x, tgt, w_out, w_out_t, gate, g_post, tm):
    s = x.shape[0]

    def body(oat_ref, za_ref, ob_ref, zb_ref, x_ref, t_ref, w_ref, wt_ref, gate_ref, gp_ref,
             dout_ref, doat_ref, dla_ref, dza_ref, dob_ref, dzb_ref, gw_ref, dgate_ref, dgp_ref, loss_ref):
        i = pl.program_id(0)

        @pl.when(i == 0)
        def _():
            gw_ref[...] = jnp.zeros(gw_ref.shape, F32)
            dgate_ref[...] = jnp.zeros(dgate_ref.shape, F32)
            dgp_ref[...] = jnp.zeros(dgp_ref.shape, F32)
            loss_ref[...] = jnp.zeros(loss_ref.shape, F32)

        oa_v = _pairs_to_rows(oat_ref)
        ob_v = ob_ref[...]
        za_v, zb_v = za_ref[...], zb_ref[...]
        sga, sgb = _sigmoid(za_v), _sigmoid(zb_v)
        sila, silb = za_v * sga, zb_v * sgb
        u = jnp.concatenate([oa_v * sila, ob_v * silb], axis=1).astype(BF)
        yv = _dot(u, w_ref[...])
        yhat, rstd = _rms_hat(yv)
        gp, gate_v = gp_ref[...], gate_ref[...]
        nrm = yhat * gp
        diff = (x_ref[...] + gate_v * nrm) - t_ref[...]
        loss_ref[...] += 0.5 * jnp.sum(jnp.sum(diff * diff, axis=1, keepdims=True), axis=0, keepdims=True) / D_MODEL
        dout = diff * (1.0 / D_MODEL)
        dout_ref[...] = dout
        dgate_ref[...] += jnp.sum(dout * nrm, axis=0, keepdims=True)
        dn = dout * gate_v
        dgp_ref[...] += jnp.sum(dn * yhat, axis=0, keepdims=True)
        dyhat = dn * gp
        dy = (rstd * (dyhat - yhat * jnp.mean(dyhat * yhat, axis=1, keepdims=True))).astype(BF)
        gw_ref[...] += _dot_tn(u, dy)
        du = _dot(dy, wt_ref[...])
        dua, dub = du[:, :FOX_W], du[:, FOX_W:]
        doa = dua * sila
        for a in range(0, FOX_HEADS, 2):
            pair_t = doa[:, a * HEAD_DIM:(a + 2) * HEAD_DIM].T
            for hd, rows in ((a, slice(0, HEAD_DIM)), (a + 1, slice(HEAD_DIM, 2 * HEAD_DIM))):
                doat_ref[hd] = pair_t[rows].astype(BF)
                dla_ref[hd] = jnp.sum(pair_t[rows] * oat_ref[hd], axis=0, keepdims=True)
        dob_ref[...] = (dub * silb).astype(BF)
        dza_ref[...] = (dua * oa_v * (sga * (1.0 + za_v * (1.0 - sga)))).astype(BF)
        dzb_ref[...] = (dub * ob_v * (sgb * (1.0 + zb_v * (1.0 - sgb)))).astype(BF)

    row = lambda w: pl.BlockSpec((tm, w), lambda i: (i, 0))
    heads_t = lambda w: pl.BlockSpec((FOX_HEADS, w, tm), lambda i: (0, 0, i))
    vec = _full((1, D_MODEL))
    mat = _full((D_MODEL, D_MODEL))
    out_shape = (
        jax.ShapeDtypeStruct((s, D_MODEL), F32),
        jax.ShapeDtypeStruct((FOX_HEADS, HEAD_DIM, s), BF), jax.ShapeDtypeStruct((FOX_HEADS, 1, s), F32),
        jax.ShapeDtypeStruct((s, FOX_W), BF), jax.ShapeDtypeStruct((s, SWA_W), BF), jax.ShapeDtypeStruct((s, SWA_W), BF),
        jax.ShapeDtypeStruct((D_MODEL, D_MODEL), F32),
        jax.ShapeDtypeStruct((1, D_MODEL), F32), jax.ShapeDtypeStruct((1, D_MODEL), F32),
        jax.ShapeDtypeStruct((1, 1), F32),
    )
    return _call(
        body, name="out_proj", out_shape=out_shape, grid=(s // tm,),
        in_specs=[heads_t(HEAD_DIM), row(FOX_W), row(SWA_W), row(SWA_W), row(D_MODEL), row(D_MODEL), mat, mat, vec, vec],
        out_specs=(row(D_MODEL), heads_t(HEAD_DIM), heads_t(1), row(FOX_W), row(SWA_W), row(SWA_W), mat, vec, vec,
                   _full((1, 1))),
        compiler_params=_params(("arbitrary",)),
    )(oat, za, ob, zb, x, tgt, w_out, w_out_t, gate, g_post)


def _assemble_dproj(dqt, dkt, dvt, dza, dqb, dzb, dkb, dvb, df, cos_t, sin_t, tm):
    s = dza.shape[0]

    def body(dqt_ref, dkt_ref, dvt_ref, dza_ref, dqb_ref, dzb_ref, dkb_ref, dvb_ref, df_ref, cos_ref, sin_ref, o_ref):
        def cat(ref, n):
            return jnp.concatenate([ref[hd] for hd in range(n)], axis=1)

        cos2, sin2 = cos_ref[...], sin_ref[...]
        cos8 = jnp.concatenate([cos2] * 4, axis=1)
        sin8 = jnp.concatenate([sin2] * 4, axis=1)
        scale = HEAD_DIM ** -0.5
        o_ref[:, C_QA:C_QA + FOX_W] = (_pairs_to_rows(dqt_ref.at[:, 0]) * scale).astype(BF)
        o_ref[:, C_KA:C_KA + FOX_W] = (_pairs_to_rows(dkt_ref) * LN2).astype(BF)
        o_ref[:, C_VA:C_VA + FOX_W] = _pairs_to_rows(dvt_ref).astype(BF)
        o_ref[:, C_ZA:C_ZA + FOX_W] = dza_ref[...]
        dq = dqb_ref[...] * scale
        o_ref[:, C_QB:C_QB + SWA_W] = (dq * cos8 - _rope_partner(dq) * sin8).astype(BF)
        o_ref[:, C_ZB:C_ZB + SWA_W] = dzb_ref[...]
        dk = cat(dkb_ref, SWA_KV_HEADS)
        o_ref[:, C_KB:C_KB + SWA_KV_W] = (dk * cos2 - _rope_partner(dk) * sin2).astype(BF)
        o_ref[:, C_VB:C_VB + SWA_KV_W] = cat(dvb_ref, SWA_KV_HEADS).astype(BF)
        o_ref[:, C_F:C_F + LANES] = df_ref[...].astype(BF)

    row = lambda w: pl.BlockSpec((tm, w), lambda i: (i, 0))
    heads = lambda n: pl.BlockSpec((n, tm, HEAD_DIM), lambda i: (0, i, 0))
    heads_t = lambda w: pl.BlockSpec((FOX_HEADS, w, tm), lambda i: (0, 0, i))
    return _call(
        body, name="assemble_dproj", out_shape=jax.ShapeDtypeStruct((s, WP), BF), grid=(s // tm,),
        in_specs=[pl.BlockSpec((FOX_HEADS, 1, AUG_DIM, tm), lambda i: (0, i, 0, 0)), heads_t(AUG_DIM), heads_t(HEAD_DIM),
                  row(FOX_W), row(SWA_W), row(SWA_W), heads(SWA_KV_HEADS), heads(SWA_KV_HEADS),
                  row(LANES), row(LANES), row(LANES)],
        out_specs=row(WP), compiler_params=_params(("parallel",)),
    )(dqt, dkt, dvt, dza, dqb, dzb, dkb, dvb, df, cos_t, sin_t)


def _in_proj_bwd_x(dproj, w_al_t, x, dout, g_pre, scale1p, tm):
    s = x.shape[0]

    def body(dp_ref, wt_ref, x_ref, dout_ref, g_ref, sc_ref, gx_ref, dsh_ref, dsc_ref, dg_ref):
        i = pl.program_id(0)

        @pl.when(i == 0)
        def _():
            dsh_ref[...] = jnp.zeros(dsh_ref.shape, F32)
            dsc_ref[...] = jnp.zeros(dsc_ref.shape, F32)
            dg_ref[...] = jnp.zeros(dg_ref.shape, F32)

        dh = _dot(dp_ref[...], wt_ref[...])
        xhat, rstd = _rms_hat(x_ref[...])
        g, sc = g_ref[...], sc_ref[...]
        dsh_ref[...] += jnp.sum(dh, axis=0, keepdims=True)
        dhx = dh * xhat
        dsc_ref[...] += jnp.sum(dhx * g, axis=0, keepdims=True)
        dg_ref[...] += jnp.sum(dhx * sc, axis=0, keepdims=True)
        dxhat = dh * (g * sc)
        gx_ref[...] = dout_ref[...] + rstd * (dxhat - xhat * jnp.mean(dxhat * xhat, axis=1, keepdims=True))

    row = lambda w: pl.BlockSpec((tm, w), lambda i: (i, 0))
    vec = _full((1, D_MODEL))
    vshape = jax.ShapeDtypeStruct((1, D_MODEL), F32)
    return _call(
        body, name="in_proj_bwd_x", out_shape=(jax.ShapeDtypeStruct((s, D_MODEL), F32), vshape, vshape, vshape),
        grid=(s // tm,),
        in_specs=[row(WP), _full((WP, D_MODEL)), row(D_MODEL), row(D_MODEL), vec, vec],
        out_specs=(row(D_MODEL), vec, vec, vec), compiler_params=_params(("arbitrary",)),
    )(dproj, w_al_t, x, dout, g_pre, scale1p)


def _in_proj_bwd_w(h, dproj, tk, tn):
    s = h.shape[0]

    def body(h_ref, dp_ref, gw_ref):
        @pl.when(pl.program_id(1) == 0)
        def _():
            gw_ref[...] = jnp.zeros(gw_ref.shape, F32)

        gw_ref[...] += _dot_tn(h_ref[...], dp_ref[...])

    return _call(
        body, name="in_proj_bwd_w", out_shape=jax.ShapeDtypeStruct((D_MODEL, WP), F32), grid=(WP // tn, s // tk),
        in_specs=[pl.BlockSpec((tk, D_MODEL), lambda n, k: (k, 0)), pl.BlockSpec((tk, tn), lambda n, k: (k, n))],
        out_specs=pl.BlockSpec((D_MODEL, tn), lambda n, k: (0, n)),
        compiler_params=_params(("parallel", "arbitrary")),
    )(h, dproj)


def _align_w_in(w_cols):
    def part(name, width):
        return w_cols[:, _SRC[name]:_SRC[name] + width]

    fpad = jnp.pad(part("fa", FOX_HEADS), ((0, 0), (0, LANES - FOX_HEADS)))
    return jnp.concatenate([part("qa", FOX_W), part("ka", FOX_W), part("va", FOX_W), part("za", FOX_W),
                            part("qb", SWA_W), part("zb", SWA_W), part("kb", SWA_KV_W), part("vb", SWA_KV_W), fpad], axis=1)


def _unalign_w_in(g_al):
    def part(c0, width):
        return g_al[:, c0:c0 + width]

    return jnp.concatenate([part(C_QA, FOX_W), part(C_KA, FOX_W), part(C_VA, FOX_W), part(C_F, FOX_HEADS),
                            part(C_ZA, FOX_W), part(C_QB, SWA_W), part(C_KB, SWA_KV_W), part(C_VB, SWA_KV_W),
                            part(C_ZB, SWA_W)], axis=1)


def _rope_tables(positions):
    inv_freq = ROPE_THETA ** (-jnp.arange(HALF, dtype=F32) / HALF)
    ang = positions.astype(F32)[:, None] * inv_freq
    cos, sin = jnp.cos(ang), jnp.sin(ang)
    return jnp.concatenate([cos, cos, cos, cos], axis=1), jnp.concatenate([-sin, sin, -sin, sin], axis=1)


def _tiles(s):
    if s >= 4096:
        return dict(tm=512, blk=512, bq=1024, bk=512, chunk=256, tq=256, tm_out=256, tk=512, tn=1152)
    return dict(tm=128, blk=128, bq=256, bk=128, chunk=128, tq=128, tm_out=128, tk=128, tn=1152)


def kernel(x, c, positions, w_ada, b_ada, g_pre, w_in, b_fgate, sinks, w_out, g_post, loss_target, m_w_ada, m_b_ada, m_g_pre, m_w_in, m_b_fgate, m_sinks, m_w_out, m_g_post, v_w_ada, v_b_ada, v_g_pre, v_w_in, v_b_fgate, v_sinks, v_w_out, v_g_post):
    s = x.shape[1]
    t = _tiles(s)
    nc = s // LANES
    rows = FOX_HEADS * nc
    me = 4 * lax.axis_index("x") + 2 * lax.axis_index("y") + lax.axis_index("c")
    chip = 2 * lax.axis_index("x") + lax.axis_index("y")
    core = lax.axis_index("c")
    x2, tgt = x[0], loss_target[0]

    c_all = _allgather_devices(c, "gather_c")[:, 0, :]
    a_all, mod_shard = _ada_shard(c_all, w_ada[0])
    mod_all = _allgather_devices(mod_shard, "gather_mod")
    mod_rows = lax.dynamic_index_in_dim(mod_all, me, axis=1, keepdims=False)
    mod = mod_rows.reshape(N_CHIPS, 2, W_ADA_SHARD)[:, 0, :].reshape(1, 3 * D_MODEL) + b_ada
    shift, scale1p, gate = mod[:, :D_MODEL], 1.0 + mod[:, D_MODEL:2 * D_MODEL], mod[:, 2 * D_MODEL:]

    w_in_pad = jnp.pad(w_in[0].astype(BF), ((0, 0), (0, W_IN_SHARD_PAD - W_IN_SHARD)))
    w_in_all = _allgather_chips(w_in_pad, "gather_w_in")
    w_cols = jnp.concatenate([w_in_all[k, :, :W_IN_SHARD] for k in range(N_CHIPS)], axis=1)
    w_al = _align_w_in(w_cols)
    w_al_t = w_al.T
    w_out_all = _allgather_chips(w_out[0].astype(BF), "gather_w_out").reshape(D_MODEL, D_MODEL)
    w_out_t = w_out_all.T

    cos_t, sin_t = _rope_tables(positions[0])

    f_pad = _forget_logits(x2, g_pre, scale1p, shift, w_al[:, C_F:], t["tm"])
    f_rows = f_pad[:, :FOX_HEADS].T.reshape(rows, LANES)
    bias_rows = jnp.repeat(b_fgate[0], nc)[:, None]
    cum = _log_forget_cumsum(f_rows, bias_rows, nc).reshape(FOX_HEADS, s)
    h, qat, ka, kat, va, vat, za, zb, qb, kb, vb = _in_proj(
        x2, g_pre, scale1p, shift, w_al[:, C_VA:C_F], w_al_t[:C_ZA], cum, cos_t, sin_t, t["tm"])
    oat, lse = _fox_fwd(qat, ka, vat, t["bq"], t["bk"], t["chunk"])
    sinks_g = sinks.reshape(SWA_KV_HEADS, 1, SWA_GROUP)
    ob = _swa_fwd(qb, kb, vb, sinks_g, t["tq"])

    dout, doat, delta_a, dza, dob, dzb, gw_out, dgate, dg_post, loss_part = _out_proj(
        oat, za, ob, zb, x2, tgt, w_out_all, w_out_t, gate, g_post, t["tm_out"])

    dqt, dkt, dvt = _fox_bwd(qat, ka, kat, va, doat, lse, delta_a, t["bq"], t["bk"], t["chunk"], t["blk"])
    dcum = dqt[:, :, HEAD_DIM, :].reshape(FOX_HEADS, s) - dkt[:, HEAD_DIM, :]
    df_rows, db_heads = _log_forget_cumsum_bwd(dcum.reshape(rows, LANES), f_rows, bias_rows, nc)
    df_pad = jnp.pad(df_rows.reshape(FOX_HEADS, s).T, ((0, 0), (0, LANES - FOX_HEADS)))
    dqb, dkb, dvb, dsinks = _swa_bwd(qb, kb, vb, sinks_g, dob, t["tq"])

    dproj = _assemble_dproj(dqt, dkt, dvt, dza, dqb, dzb, dkb, dvb, df_pad, cos_t, sin_t, t["blk"])
    grad_x, dshift, dscale, dg_pre = _in_proj_bwd_x(dproj, w_al_t, x2, dout, g_pre, scale1p, t["tm_out"])
    gw_in = _unalign_w_in(_in_proj_bwd_w(h, dproj, t["tk"], t["tn"]))

    pad_lane = lambda vrow: jnp.pad(vrow, ((0, 0), (0, LANES - vrow.shape[1])))
    packed = jnp.concatenate([dshift, dscale, dgate, dg_pre, dg_post,
                              pad_lane(db_heads.reshape(1, FOX_HEADS)), pad_lane(dsinks.reshape(1, FOX_HEADS)),
                              pad_lane(loss_part)], axis=1)
    parts = _allgather_devices(packed, "gather_partials")
    tot = _sum_devices(parts)
    loss = tot[0, P_LOSS]
    g_b_ada = tot[:, P_DMOD:P_DMOD + 3 * D_MODEL]
    g_g_pre = tot[:, P_GPRE:P_GPRE + D_MODEL]
    g_g_post = tot[:, P_GPOST:P_GPOST + D_MODEL]
    g_b_fgate = tot[:, P_BF:P_BF + FOX_HEADS]
    g_sinks = tot[:, P_SINK:P_SINK + FOX_HEADS]
    dm_shard = lax.dynamic_slice_in_dim(parts[:, 0, :3 * D_MODEL], chip * W_ADA_SHARD, W_ADA_SHARD, axis=1)
    g_w_ada = _grad_w_ada(a_all.T, dm_shard)

    gin = jnp.pad(gw_in.reshape(D_MODEL, N_CHIPS, W_IN_SHARD).transpose(1, 0, 2),
                  ((0, 0), (0, 0), (0, W_IN_SHARD_PAD - W_IN_SHARD)))
    gout = gw_out.reshape(N_CHIPS, D_MODEL, W_OUT_SHARD)
    gbig = jnp.concatenate([gin, gout], axis=2)
    half = D_MODEL // 2
    gw = W_IN_SHARD_PAD + W_OUT_SHARD
    keep = lax.dynamic_slice_in_dim(gbig, core * half, half, axis=1)
    give = lax.dynamic_slice_in_dim(gbig, (1 - core) * half, half, axis=1)
    got = _swap_sibling(give.reshape(N_CHIPS * half, gw), "swap_grad_halves")
    pair = _add(keep.reshape(N_CHIPS * half, gw), got, "add_pair").reshape(N_CHIPS, half, gw)
    from_chips = _exchange_chips(pair, "exchange_grad")
    mine = _sum_chips(from_chips, "sum_chips")
    other = _swap_sibling(mine, "swap_grad_result")
    lo = jnp.where(core == 0, mine, other)
    hi = jnp.where(core == 0, other, mine)
    gfull = jnp.concatenate([lo, hi], axis=0)
    g_w_in = gfull[:, :W_IN_SHARD]
    g_w_out = gfull[:, W_IN_SHARD_PAD:].reshape(W_OUT_SHARD, D_MODEL)

    grads = dict(w_ada=g_w_ada, b_ada=g_b_ada, g_pre=g_g_pre, w_in=g_w_in, b_fgate=g_b_fgate, sinks=g_sinks,
                 w_out=g_w_out, g_post=g_g_post)
    weights = dict(w_ada=w_ada, b_ada=b_ada, g_pre=g_pre, w_in=w_in, b_fgate=b_fgate, sinks=sinks, w_out=w_out, g_post=g_post)
    moms = dict(w_ada=m_w_ada, b_ada=m_b_ada, g_pre=m_g_pre, w_in=m_w_in, b_fgate=m_b_fgate, sinks=m_sinks, w_out=m_w_out, g_post=m_g_post)
    vars_ = dict(w_ada=v_w_ada, b_ada=v_b_ada, g_pre=v_g_pre, w_in=v_w_in, b_fgate=v_b_fgate, sinks=v_sinks, w_out=v_w_out, g_post=v_g_post)
    names = ["w_ada", "b_ada", "g_pre", "w_in", "b_fgate", "sinks", "w_out", "g_post"]
    g_out, d_out, m_out, v_out = [], [], [], []
    for n in names:
        shape = weights[n].shape
        w2 = weights[n].reshape(shape[-2], shape[-1])
        g2 = grads[n].reshape(w2.shape)
        d2, nm2, nv2 = _adamw(w2, g2, moms[n].reshape(w2.shape), vars_[n].reshape(w2.shape), "adamw_" + n)
        g_out.append(g2.reshape(shape))
        d_out.append(d2.reshape(shape))
        m_out.append(nm2.reshape(shape))
        v_out.append(nv2.reshape(shape))
    return (loss, grad_x.reshape(x.shape), *g_out, *d_out, *m_out, *v_out)
```

```python
import functools

import jax
import jax.numpy as jnp
from jax import lax
from jax.experimental import pallas as pl
from jax.experimental.pallas import tpu as pltpu

_INTERPRET = False

D_MODEL = 1024
HEAD_DIM = 64
HALF = HEAD_DIM // 2
AUG_DIM = 128
AUG_ROWS = 8
VT_ROWS = 80
LOG2E = 1.4426950408889634
LN2 = 0.6931471805599453
Q_SCALE = LOG2E * 64 ** -0.5
FOX_HEADS = 8
FOX_W = 512
SWA_W = 512
SWA_KV_HEADS = 2
SWA_GROUP = 4
SWA_KV_W = 128
WINDOW = 128
ROPE_THETA = 10000.0
RMS_EPS = 1e-6
IN_WIDTH = 3336
N_CHIPS = 4
N_DEV = 8
W_IN_SHARD = IN_WIDTH // N_CHIPS
W_IN_SHARD_PAD = 896
W_ADA_SHARD = 3 * D_MODEL // N_CHIPS
W_OUT_SHARD = D_MODEL // N_CHIPS
LANES = 128

_SRC = dict(qa=0, ka=512, va=1024, fa=1536, za=1544, qb=2056, kb=2568, vb=2696, zb=2824)
C_QA, C_KA, C_VA, C_ZA, C_QB, C_ZB, C_KB, C_VB, C_F = 0, 512, 1024, 1536, 2048, 2560, 3072, 3200, 3328
WP = 3456

ADAM_LR = 0.001
ADAM_B1 = 0.9
ADAM_B2 = 0.999
ADAM_EPS = 1e-08
ADAM_WD = 0.01
ADAM_STEP = 10

VMEM_LIMIT = 56 * 1024 * 1024
NEG = -1e30
MESH = pl.DeviceIdType.MESH
BF = jnp.bfloat16
F32 = jnp.float32

P_DMOD, P_GPRE, P_GPOST, P_BF, P_SINK, P_LOSS, P_LEN = 0, 3072, 4096, 5120, 5248, 5376, 5504


def _call(body, **kw):
    return pl.pallas_call(body, interpret=_INTERPRET, **kw)


def _params(sem=None, **kw):
    return pltpu.CompilerParams(dimension_semantics=sem, vmem_limit_bytes=VMEM_LIMIT, **kw)


def _full(shape):
    zeros = (0,) * len(shape)
    return pl.BlockSpec(shape, lambda *_: zeros)


def _dot(a, b):
    return jnp.dot(a, b, preferred_element_type=F32)


def _dot_nt(a, b):
    return lax.dot_general(a, b, (((1,), (1,)), ((), ())), preferred_element_type=F32)


def _dot_tn(a, b):
    return lax.dot_general(a, b, (((0,), (0,)), ((), ())), preferred_element_type=F32)


def _sigmoid(z):
    return 1.0 / (1.0 + jnp.exp(-z))


def _rope_partner(t):
    w = t.shape[-1]
    lane = lax.broadcasted_iota(jnp.int32, t.shape, t.ndim - 1)
    return jnp.where((lane & (HEAD_DIM - 1)) < HALF, pltpu.roll(t, w - HALF, t.ndim - 1), pltpu.roll(t, HALF, t.ndim - 1))


def _allgather_devices(v, name):
    r, cdim = v.shape
    masks = [(dx, dy, dc) for dx in (0, 1) for dy in (0, 1) for dc in (0, 1)][1:]

    def body(v_ref, out_ref, send_sems, recv_sems):
        x, y, c = lax.axis_index("x"), lax.axis_index("y"), lax.axis_index("c")
        me = 4 * x + 2 * y + c
        out_ref[me] = v_ref[...]
        copies = []
        for k, (dx, dy, dc) in enumerate(masks):
            cp = pltpu.make_async_remote_copy(
                src_ref=v_ref, dst_ref=out_ref.at[me], send_sem=send_sems.at[k], recv_sem=recv_sems.at[k],
                device_id=(x ^ dx, y ^ dy, c ^ dc), device_id_type=MESH)
            cp.start()
            copies.append(cp)
        for k, (dx, dy, dc) in enumerate(masks):
            peer = 4 * (x ^ dx) + 2 * (y ^ dy) + (c ^ dc)
            pltpu.make_async_remote_copy(
                src_ref=v_ref, dst_ref=out_ref.at[peer], send_sem=send_sems.at[k], recv_sem=recv_sems.at[k],
                device_id=(x ^ dx, y ^ dy, c ^ dc), device_id_type=MESH).wait_recv()
        for cp in copies:
            cp.wait_send()

    return _call(
        body, name=name, out_shape=jax.ShapeDtypeStruct((N_DEV, r, cdim), v.dtype),
        in_specs=[pl.BlockSpec(memory_space=pltpu.VMEM)], out_specs=pl.BlockSpec(memory_space=pltpu.VMEM),
        scratch_shapes=[pltpu.SemaphoreType.DMA((7,)), pltpu.SemaphoreType.DMA((7,))],
        compiler_params=pltpu.CompilerParams(has_side_effects=True),
    )(v)


def _allgather_chips(v, name):
    r, cdim = v.shape
    masks = [(1, 0), (0, 1), (1, 1)]

    def body(v_ref, out_ref, send_sems, recv_sems, local_sem):
        x, y, c = lax.axis_index("x"), lax.axis_index("y"), lax.axis_index("c")
        me = 2 * x + y
        mine = pltpu.make_async_copy(v_ref, out_ref.at[me], local_sem)
        mine.start()
        copies = []
        for k, (dx, dy) in enumerate(masks):
            cp = pltpu.make_async_remote_copy(
                src_ref=v_ref, dst_ref=out_ref.at[me], send_sem=send_sems.at[k], recv_sem=recv_sems.at[k],
                device_id=(x ^ dx, y ^ dy, c), device_id_type=MESH)
            cp.start()
            copies.append(cp)
        for k, (dx, dy) in enumerate(masks):
            peer = 2 * (x ^ dx) + (y ^ dy)
            pltpu.make_async_remote_copy(
                src_ref=v_ref, dst_ref=out_ref.at[peer], send_sem=send_sems.at[k], recv_sem=recv_sems.at[k],
                device_id=(x ^ dx, y ^ dy, c), device_id_type=MESH).wait_recv()
        for cp in copies:
            cp.wait_send()
        mine.wait()

    return _call(
        body, name=name, out_shape=jax.ShapeDtypeStruct((N_CHIPS, r, cdim), v.dtype),
        in_specs=[pl.BlockSpec(memory_space=pl.ANY)], out_specs=pl.BlockSpec(memory_space=pl.ANY),
        scratch_shapes=[pltpu.SemaphoreType.DMA((3,)), pltpu.SemaphoreType.DMA((3,)), pltpu.SemaphoreType.DMA],
        compiler_params=pltpu.CompilerParams(has_side_effects=True),
    )(v)


def _swap_sibling(v, name):
    def body(v_ref, out_ref, send_sem, recv_sem):
        x, y, c = lax.axis_index("x"), lax.axis_index("y"), lax.axis_index("c")
        cp = pltpu.make_async_remote_copy(
            src_ref=v_ref, dst_ref=out_ref, send_sem=send_sem, recv_sem=recv_sem,
            device_id=(x, y, 1 - c), device_id_type=MESH)
        cp.start()
        cp.wait()

    return _call(
        body, name=name, out_shape=jax.ShapeDtypeStruct(v.shape, v.dtype),
        in_specs=[pl.BlockSpec(memory_space=pl.ANY)], out_specs=pl.BlockSpec(memory_space=pl.ANY),
        scratch_shapes=[pltpu.SemaphoreType.DMA, pltpu.SemaphoreType.DMA],
        compiler_params=pltpu.CompilerParams(has_side_effects=True),
    )(v)


def _exchange_chips(v, name):
    _, r, cdim = v.shape
    masks = [(1, 0), (0, 1), (1, 1)]

    def body(v_ref, out_ref, send_sems, recv_sems, local_sem):
        x, y, c = lax.axis_index("x"), lax.axis_index("y"), lax.axis_index("c")
        me = 2 * x + y
        mine = pltpu.make_async_copy(v_ref.at[me], out_ref.at[me], local_sem)
        mine.start()
        copies = []
        for k, (dx, dy) in enumerate(masks):
            peer = 2 * (x ^ dx) + (y ^ dy)
            cp = pltpu.make_async_remote_copy(
                src_ref=v_ref.at[peer], dst_ref=out_ref.at[me], send_sem=send_sems.at[k], recv_sem=recv_sems.at[k],
                device_id=(x ^ dx, y ^ dy, c), device_id_type=MESH)
            cp.start()
            copies.append(cp)
        for k, (dx, dy) in enumerate(masks):
            peer = 2 * (x ^ dx) + (y ^ dy)
            pltpu.make_async_remote_copy(
                src_ref=v_ref.at[me], dst_ref=out_ref.at[peer], send_sem=send_sems.at[k], recv_sem=recv_sems.at[k],
                device_id=(x ^ dx, y ^ dy, c), device_id_type=MESH).wait_recv()
        for cp in copies:
            cp.wait_send()
        mine.wait()

    return _call(
        body, name=name, out_shape=jax.ShapeDtypeStruct(v.shape, v.dtype),
        in_specs=[pl.BlockSpec(memory_space=pl.ANY)], out_specs=pl.BlockSpec(memory_space=pl.ANY),
        scratch_shapes=[pltpu.SemaphoreType.DMA((3,)), pltpu.SemaphoreType.DMA((3,)), pltpu.SemaphoreType.DMA],
        compiler_params=pltpu.CompilerParams(has_side_effects=True),
    )(v)


def _ada_shard(c_all, w_ada_shard):
    def body(c_ref, w_ref, a_ref, mod_ref):
        cv = c_ref[...]
        a = cv * _sigmoid(cv)
        a_ref[...] = a
        mod_ref[...] = _dot(a.astype(BF), w_ref[...].astype(BF))

    return _call(
        body, name="ada_shard",
        out_shape=(jax.ShapeDtypeStruct((N_DEV, D_MODEL), F32), jax.ShapeDtypeStruct((N_DEV, W_ADA_SHARD), F32)),
        compiler_params=_params(),
    )(c_all, w_ada_shard)


def _grad_w_ada(a_t, dm_shard):
    def body(a_ref, dm_ref, out_ref):
        acc = jnp.zeros((D_MODEL, W_ADA_SHARD), F32)
        for b in range(N_DEV):
            acc = acc + a_ref[:, b:b + 1] * dm_ref[b:b + 1, :]
        out_ref[...] = acc

    return _call(body, name="grad_w_ada", out_shape=jax.ShapeDtypeStruct((D_MODEL, W_ADA_SHARD), F32),
                 compiler_params=_params())(a_t, dm_shard)


def _sum_devices(parts):
    n = parts.shape[-1]

    def body(p_ref, out_ref):
        acc = p_ref[0]
        for b in range(1, N_DEV):
            acc = acc + p_ref[b]
        out_ref[...] = acc

    return _call(body, name="sum_devices", out_shape=jax.ShapeDtypeStruct((1, n), F32), compiler_params=_params())(parts)


def _add(a, b, name, out_dtype):
    r, cdim = a.shape
    tr = min(r, 256)

    def body(a_ref, b_ref, o_ref):
        o_ref[...] = (a_ref[...] + b_ref[...]).astype(out_dtype)

    spec = pl.BlockSpec((tr, cdim), lambda i: (i, 0))
    return _call(body, name=name, out_shape=jax.ShapeDtypeStruct(a.shape, out_dtype), grid=(r // tr,),
                 in_specs=[spec, spec], out_specs=spec, compiler_params=_params(("parallel",)))(a, b)


def _sum_chips(parts, name):
    _, r, cdim = parts.shape
    tr = min(r, 128)

    def body(p_ref, o_ref):
        o_ref[...] = ((p_ref[0].astype(F32) + p_ref[1].astype(F32)) + p_ref[2].astype(F32)) + p_ref[3].astype(F32)

    return _call(body, name=name, out_shape=jax.ShapeDtypeStruct((r, cdim), F32), grid=(r // tr,),
                 in_specs=[pl.BlockSpec((N_CHIPS, tr, cdim), lambda i: (0, i, 0))],
                 out_specs=pl.BlockSpec((tr, cdim), lambda i: (i, 0)), compiler_params=_params(("parallel",)))(parts)


def _adamw(w, g, m, v, name):
    r, cdim = w.shape
    tr = r if r <= 256 else 256
    c1 = 1.0 / (1.0 - ADAM_B1 ** ADAM_STEP)
    c2 = 1.0 / (1.0 - ADAM_B2 ** ADAM_STEP)

    def body(w_ref, g_ref, m_ref, v_ref, d_ref, nm_ref, nv_ref):
        gv = g_ref[...]
        nm = ADAM_B1 * m_ref[...] + (1.0 - ADAM_B1) * gv
        nv = ADAM_B2 * v_ref[...] + (1.0 - ADAM_B2) * (gv * gv)
        m_hat = nm * c1
        v_hat = nv * c2
        d_ref[...] = -ADAM_LR * (m_hat / (jnp.sqrt(v_hat) + ADAM_EPS) + ADAM_WD * w_ref[...])
        nm_ref[...] = nm
        nv_ref[...] = nv

    spec = pl.BlockSpec((tr, cdim), lambda i: (i, 0))
    shp = jax.ShapeDtypeStruct(w.shape, F32)
    return _call(body, name=name, out_shape=(shp, shp, shp), grid=(r // tr,), in_specs=[spec] * 4,
                 out_specs=(spec, spec, spec), compiler_params=_params(("parallel",)))(w, g, m, v)


def _head_of_row(r, nc):
    assert nc & (nc - 1) == 0
    return lax.shift_right_logical(r, nc.bit_length() - 1)


def _chunk_mats(rows, nc, reverse):
    ri = lax.broadcasted_iota(jnp.int32, (rows, rows), 0)
    ci = lax.broadcasted_iota(jnp.int32, (rows, rows), 1)
    same = _head_of_row(ri, nc) == _head_of_row(ci, nc)
    between = jnp.where(same & ((ci > ri) if reverse else (ci < ri)), 1.0, 0.0).astype(F32)
    li = lax.broadcasted_iota(jnp.int32, (LANES, LANES), 0)
    lj = lax.broadcasted_iota(jnp.int32, (LANES, LANES), 1)
    within = jnp.where((li >= lj) if reverse else (li <= lj), 1.0, 0.0).astype(F32)
    return between, within


def _dot_hi(a, b):
    return jnp.dot(a, b, preferred_element_type=F32, precision=lax.Precision.HIGHEST)


def _scan_rows(t, nc, reverse):
    between, within = _chunk_mats(t.shape[0], nc, reverse)
    inner = _dot_hi(t, within)
    tot = jnp.sum(t, axis=1, keepdims=True)
    return inner + _dot_hi(between, jnp.broadcast_to(tot, t.shape))


def _log_forget_cumsum(f_rows, bias_rows, nc):
    def body(f_ref, b_ref, cum_ref):
        z = f_ref[...] + b_ref[...]
        lf = jnp.minimum(z, 0.0) - jnp.log(1.0 + jnp.exp(-jnp.abs(z)))
        cum_ref[...] = _scan_rows(lf, nc, False)

    return _call(body, name="forget_cumsum", out_shape=jax.ShapeDtypeStruct(f_rows.shape, F32),
                 compiler_params=_params())(f_rows, bias_rows)


def _log_forget_cumsum_bwd(dcum_rows, f_rows, bias_rows, nc):
    rows = f_rows.shape[0]

    def body(d_ref, f_ref, b_ref, df_ref, db_ref):
        dlf = _scan_rows(d_ref[...], nc, True)
        z = f_ref[...] + b_ref[...]
        df = dlf * _sigmoid(-z)
        df_ref[...] = df
        hi = lax.broadcasted_iota(jnp.int32, (FOX_HEADS, rows), 0)
        ri = lax.broadcasted_iota(jnp.int32, (FOX_HEADS, rows), 1)
        sel = jnp.where(_head_of_row(ri, nc) == hi, 1.0, 0.0).astype(F32)
        db_ref[...] = jnp.sum(_dot_hi(sel, df), axis=1, keepdims=True)

    return _call(body, name="forget_cumsum_bwd",
                 out_shape=(jax.ShapeDtypeStruct(f_rows.shape, F32), jax.ShapeDtypeStruct((FOX_HEADS, 1), F32)),
                 compiler_params=_params())(dcum_rows, f_rows, bias_rows)


def _rms_hat(xv):
    rstd = lax.rsqrt(jnp.mean(xv * xv, axis=-1, keepdims=True) + RMS_EPS)
    return xv * rstd, rstd


def _modulated(x_ref, g_ref, sc_ref, sh_ref):
    xhat, _ = _rms_hat(x_ref[...])
    return ((xhat * g_ref[...]) * sc_ref[...] + sh_ref[...]).astype(BF)


def _forget_logits(x, g_pre, scale1p, shift, w_f, tm):
    s = x.shape[0]

    def body(x_ref, g_ref, sc_ref, sh_ref, w_ref, f_ref):
        f_ref[...] = _dot(_modulated(x_ref, g_ref, sc_ref, sh_ref), w_ref[...])

    vec = _full((1, D_MODEL))
    return _call(
        body, name="forget_logits", out_shape=jax.ShapeDtypeStruct((s, LANES), F32), grid=(s // tm,),
        in_specs=[pl.BlockSpec((tm, D_MODEL), lambda i: (i, 0)), vec, vec, vec, _full((D_MODEL, LANES))],
        out_specs=pl.BlockSpec((tm, LANES), lambda i: (i, 0)), compiler_params=_params(("parallel",)),
    )(x, g_pre, scale1p, shift, w_f)


def _split3(v):
    hi = v.astype(BF).astype(F32)
    mid = (v - hi).astype(BF).astype(F32)
    lo = ((v - hi) - mid).astype(BF).astype(F32)
    return hi, mid, lo


def _in_proj(x, g_pre, scale1p, shift, w_rows, w_t_fox, cum, cos_t, sin_t, tm):
    s = x.shape[0]
    r_va, r_za, r_qb, r_zb, r_kb, r_vb = 0, 512, 1024, 1536, 2048, 2176

    def body(x_ref, g_ref, sc_ref, sh_ref, w_ref, wt_ref, cum_ref, cos_ref, sin_ref,
             h_ref, qat_ref, ka_ref, kat_ref, v_ref, vt_ref, za_ref, zb_ref, qb_ref, kb_ref, vb_ref,
             qbt_ref, kbt_ref, vbt_ref):
        hb = _modulated(x_ref, g_ref, sc_ref, sh_ref)
        h_ref[...] = hb

        def sec(c0, width):
            return _dot(hb, w_ref[:, c0:c0 + width])

        def sec_t(r0):
            return _dot_nt(wt_ref[r0:r0 + FOX_W, :], hb)

        q_t = sec_t(0) * Q_SCALE
        k_t = sec_t(FOX_W)
        v_t = sec_t(2 * FOX_W)
        va = sec(r_va, FOX_W)
        zeros = jnp.zeros((AUG_DIM - HEAD_DIM - AUG_ROWS, tm), F32)
        ri = lax.broadcasted_iota(jnp.int32, (AUG_ROWS, tm), 0)
        const = jnp.where(ri == AUG_ROWS - 1, 0.0, 1.0)
        ri_v = lax.broadcasted_iota(jnp.int32, (VT_ROWS - HEAD_DIM, tm), 0)
        v_feat = jnp.where(ri_v == 0, 1.0, 0.0).astype(BF)
        for hd in range(FOX_HEADS):
            rows = slice(hd * HEAD_DIM, (hd + 1) * HEAD_DIM)
            cum2 = cum_ref[hd:hd + 1, :] * LOG2E
            hi, mid, lo = (jnp.broadcast_to(part, (AUG_ROWS, tm)) for part in _split3(cum2))
            q_feat = jnp.where(ri == 1, hi, jnp.where(ri == 2, mid, jnp.where(ri == 3, lo, const)))
            k_feat = jnp.where(ri == 4, -hi, jnp.where(ri == 5, -mid, jnp.where(ri == 6, -lo, const)))
            q_aug = jnp.concatenate([q_t[rows], q_feat, zeros], axis=0)
            k_aug = jnp.concatenate([k_t[rows], k_feat, zeros], axis=0)
            qat_ref[hd] = q_aug.astype(BF)
            kat_ref[hd] = k_aug.astype(BF)
            ka_ref[hd] = k_aug.T.astype(BF)
            vt_ref[hd] = jnp.concatenate([v_t[rows].astype(BF), v_feat], axis=0)
            v_ref[hd] = va[:, rows].astype(BF)
        za_ref[...] = sec(r_za, FOX_W)
        zb_ref[...] = sec(r_zb, SWA_W)
        cos2, sin2 = cos_ref[...], sin_ref[...]
        cos8 = jnp.concatenate([cos2] * 4, axis=1)
        sin8 = jnp.concatenate([sin2] * 4, axis=1)
        qb = sec(r_qb, SWA_W)
        qb = (qb * cos8 + _rope_partner(qb) * sin8) * (HEAD_DIM ** -0.5)
        qb_ref[...] = qb.astype(BF)
        for a in range(SWA_W // LANES):
            qbt_ref[a * LANES:(a + 1) * LANES, :] = qb[:, a * LANES:(a + 1) * LANES].T.astype(BF)
        kb = sec(r_kb, SWA_KV_W)
        kb = kb * cos2 + _rope_partner(kb) * sin2
        vb = sec(r_vb, SWA_KV_W)
        kb_t, vb_t = kb.T, vb.T
        for hd in range(SWA_KV_HEADS):
            sl = slice(hd * HEAD_DIM, (hd + 1) * HEAD_DIM)
            kb_ref[hd] = kb[:, sl].astype(BF)
            vb_ref[hd] = vb[:, sl].astype(BF)
            kbt_ref[hd] = kb_t[sl].astype(BF)
            vbt_ref[hd] = jnp.concatenate([vb_t[sl].astype(BF), v_feat], axis=0)

    row = lambda w: pl.BlockSpec((tm, w), lambda i: (i, 0))
    heads = lambda n, w=HEAD_DIM: pl.BlockSpec((n, tm, w), lambda i: (0, i, 0))
    heads_t = lambda w: pl.BlockSpec((FOX_HEADS, w, tm), lambda i: (0, 0, i))
    vec = _full((1, D_MODEL))
    hs = lambda a, b: jax.ShapeDtypeStruct((FOX_HEADS, a, b), BF)
    out_shape = (
        jax.ShapeDtypeStruct((s, D_MODEL), BF),
        hs(AUG_DIM, s), hs(s, AUG_DIM), hs(AUG_DIM, s), hs(s, HEAD_DIM), hs(VT_ROWS, s),
        jax.ShapeDtypeStruct((s, FOX_W), F32), jax.ShapeDtypeStruct((s, SWA_W), F32),
        jax.ShapeDtypeStruct((s, SWA_W), BF),
        jax.ShapeDtypeStruct((SWA_KV_HEADS, s, HEAD_DIM), BF), jax.ShapeDtypeStruct((SWA_KV_HEADS, s, HEAD_DIM), BF),
        jax.ShapeDtypeStruct((SWA_W, s), BF),
        jax.ShapeDtypeStruct((SWA_KV_HEADS, HEAD_DIM, s), BF), jax.ShapeDtypeStruct((SWA_KV_HEADS, VT_ROWS, s), BF),
    )
    kv_t = lambda w: pl.BlockSpec((SWA_KV_HEADS, w, tm), lambda i: (0, 0, i))
    return _call(
        body, name="in_proj", out_shape=out_shape, grid=(s // tm,),
        in_specs=[row(D_MODEL), vec, vec, vec, _full(w_rows.shape), _full(w_t_fox.shape),
                  pl.BlockSpec((FOX_HEADS, tm), lambda i: (0, i)), row(LANES), row(LANES)],
        out_specs=(row(D_MODEL), heads_t(AUG_DIM), heads(FOX_HEADS, AUG_DIM), heads_t(AUG_DIM), heads(FOX_HEADS),
                   heads_t(VT_ROWS), row(FOX_W), row(SWA_W), row(SWA_W), heads(SWA_KV_HEADS), heads(SWA_KV_HEADS),
                   pl.BlockSpec((SWA_W, tm), lambda i: (0, i)), kv_t(HEAD_DIM), kv_t(VT_ROWS)),
        compiler_params=_params(("parallel",)),
    )(x, g_pre, scale1p, shift, w_rows, w_t_fox, cum, cos_t, sin_t)


def _diag_chunks(d, bq, bk, chunk):
    out = []
    for c0 in range(0, bq, chunk):
        if d is None or d * bk + bk - 1 <= c0:
            out.append((c0, None))
        elif d * bk <= c0 + chunk - 1:
            kpos = d * bk + lax.broadcasted_iota(jnp.int32, (bk, chunk), 0)
            qpos = c0 + lax.broadcasted_iota(jnp.int32, (bk, chunk), 1)
            out.append((c0, kpos <= qpos))
    return out


def _fox_fwd(qat, ka, vt, bq, bk, chunk):
    nh, _, s = qat.shape
    r = bq // bk

    def body(ka_ref, qat_ref, vt_ref, o_ref, lse_ref, m_scr, acc_scr):
        i, j = pl.program_id(1), pl.program_id(2)

        @pl.when(j == 0)
        def _():
            m_scr[...] = jnp.full(m_scr.shape, NEG, F32)
            acc_scr[...] = jnp.zeros(acc_scr.shape, F32)

        def step(d):
            kv, vtv = ka_ref[0], vt_ref[0]
            todo = _diag_chunks(d, bq, bk, chunk)
            scores = lambda c0: _dot(kv, qat_ref[0, :, c0:c0 + chunk])
            sc_next = scores(todo[0][0])
            for n, (c0, mask) in enumerate(todo):
                cs = slice(c0, c0 + chunk)
                sc = sc_next
                if n + 1 < len(todo):
                    sc_next = scores(todo[n + 1][0])
                if mask is not None:
                    sc = jnp.where(mask, sc, NEG)
                m_prev = m_scr[:, cs]
                m_new = jnp.maximum(m_prev, jnp.max(sc, axis=0, keepdims=True))
                p = jnp.exp2(sc - m_new).astype(BF)
                acc_scr[:, cs] = jnp.exp2(m_prev - m_new) * acc_scr[:, cs] + _dot(vtv, p)
                m_scr[:, cs] = m_new

        @pl.when(j < i * r)
        def _():
            step(None)

        for d in range(r):
            @pl.when(j == i * r + d)
            def _(d=d):
                step(d)

        @pl.when(j == i * r + r - 1)
        def _():
            l = acc_scr[HEAD_DIM:HEAD_DIM + 1, :]
            o_ref[0] = acc_scr[:HEAD_DIM, :] / l
            lse_ref[0] = m_scr[...] + jnp.log2(l)

    kmap = lambda h, i, j: (h, jnp.minimum(j, i * r + r - 1), 0)
    kmap_t = lambda h, i, j: (h, 0, jnp.minimum(j, i * r + r - 1))
    return _call(
        body, name="fox_fwd",
        out_shape=(jax.ShapeDtypeStruct((nh, HEAD_DIM, s), F32), jax.ShapeDtypeStruct((nh, 1, s), F32)),
        grid=(nh, s // bq, s // bk),
        in_specs=[pl.BlockSpec((1, bk, AUG_DIM), kmap), pl.BlockSpec((1, AUG_DIM, bq), lambda h, i, j: (h, 0, i)),
                  pl.BlockSpec((1, VT_ROWS, bk), kmap_t)],
        out_specs=(pl.BlockSpec((1, HEAD_DIM, bq), lambda h, i, j: (h, 0, i)),
                   pl.BlockSpec((1, 1, bq), lambda h, i, j: (h, 0, i))),
        scratch_shapes=[pltpu.VMEM((1, bq), F32), pltpu.VMEM((VT_ROWS, bq), F32)],
        compiler_params=_params(("parallel", "parallel", "arbitrary")),
    )(ka, qat, vt)


def _fox_bwd(qat, ka, kat, v, dot_, lse, delta, bq, bk, chunk, dq_blk):
    nh, _, s = qat.shape
    r = bq // bk
    nq = s // bq

    def body(ka_ref, kat_ref, v_ref, qat_ref, do_ref, lse_ref, dl_ref, dq_ref, dk_ref, dv_ref, dk_scr, dv_scr):
        j, i = pl.program_id(1), pl.program_id(2)

        @pl.when((j == 0) & (i == 0))
        def _():
            dq_ref[...] = jnp.zeros(dq_ref.shape, F32)

        @pl.when(i == 0)
        def _():
            dk_scr[...] = jnp.zeros(dk_scr.shape, F32)
            dv_scr[...] = jnp.zeros(dv_scr.shape, F32)

        def step(d):
            kv, ktv, vv = ka_ref[0], kat_ref[0], v_ref[0]
            todo = _diag_chunks(d, bq, bk, chunk)

            def products(c0):
                cs = slice(c0, c0 + chunk)
                return _dot(kv, qat_ref[0, :, cs]), _dot(vv, do_ref[0, :, cs])

            nxt = products(todo[0][0])
            for n, (c0, mask) in enumerate(todo):
                cs = slice(c0, c0 + chunk)
                sc, dp = nxt
                if n + 1 < len(todo):
                    nxt = products(todo[n + 1][0])
                p = jnp.exp2(sc - lse_ref[0, :, cs])
                if mask is not None:
                    p = jnp.where(mask, p, 0.0)
                ds = (p * (dp - dl_ref[0, :, cs])).astype(BF)
                dv_scr[...] += _dot_nt(do_ref[0, :, cs], p.astype(BF))
                dk_scr[...] += _dot_nt(qat_ref[0, :, cs], ds)
                c1 = c0 % dq_blk
                dq_ref[0, i * (bq // dq_blk) + c0 // dq_blk, :, c1:c1 + chunk] += _dot(ktv, ds)

        @pl.when(i * r > j)
        def _():
            step(None)

        for d in range(r):
            @pl.when(j == i * r + d)
            def _(d=d):
                step(d)

        @pl.when(i == nq - 1)
        def _():
            dk_ref[0] = dk_scr[...]
            dv_ref[0] = dv_scr[...]

    qmap = lambda h, j, i: (h, 0, jnp.maximum(i, j // r))
    kmap_t = lambda h, j, i: (h, 0, j)
    return _call(
        body, name="fox_bwd",
        out_shape=(jax.ShapeDtypeStruct((nh, s // dq_blk, AUG_DIM, dq_blk), F32),
                   jax.ShapeDtypeStruct((nh, AUG_DIM, s), F32), jax.ShapeDtypeStruct((nh, HEAD_DIM, s), F32)),
        grid=(nh, s // bk, nq),
        in_specs=[pl.BlockSpec((1, bk, AUG_DIM), lambda h, j, i: (h, j, 0)), pl.BlockSpec((1, AUG_DIM, bk), kmap_t),
                  pl.BlockSpec((1, bk, HEAD_DIM), lambda h, j, i: (h, j, 0)),
                  pl.BlockSpec((1, AUG_DIM, bq), qmap), pl.BlockSpec((1, HEAD_DIM, bq), qmap),
                  pl.BlockSpec((1, 1, bq), qmap), pl.BlockSpec((1, 1, bq), qmap)],
        out_specs=(pl.BlockSpec((1, s // dq_blk, AUG_DIM, dq_blk), lambda h, j, i: (h, 0, 0, 0)),
                   pl.BlockSpec((1, AUG_DIM, bk), kmap_t), pl.BlockSpec((1, HEAD_DIM, bk), kmap_t)),
        scratch_shapes=[pltpu.VMEM((AUG_DIM, bk), F32), pltpu.VMEM((HEAD_DIM, bk), F32)],
        compiler_params=_params(("parallel", "arbitrary", "arbitrary")),
    )(ka, kat, v, qat, dot_, lse, delta)


def _swa_mask(i, tq):
    kpos = i * tq - WINDOW + lax.broadcasted_iota(jnp.int32, (tq + WINDOW, tq), 0)
    qpos = i * tq + lax.broadcasted_iota(jnp.int32, (tq + WINDOW, tq), 1)
    rel = qpos - kpos
    return (rel >= 0) & (rel < WINDOW) & (kpos >= 0)


def _swa_rows(ref, i, tq):
    before = pl.multiple_of(jnp.maximum(i * tq - WINDOW, 0), WINDOW)
    return jnp.concatenate([ref[0, pl.ds(before, WINDOW), :], ref[0, pl.ds(pl.multiple_of(i * tq, tq), tq), :]], axis=0)


def _swa_before(n_rows, tq):
    return pl.BlockSpec((1, n_rows, WINDOW), lambda g, i: (g, 0, jnp.maximum(i * (tq // WINDOW) - 1, 0)))


def _swa_probs_t(kw, q_t, mask, sink):
    sc = jnp.where(mask, _dot(kw, q_t), NEG)
    m = jnp.maximum(jnp.max(sc, axis=0, keepdims=True), sink)
    p = jnp.exp(sc - m)
    e_sink = jnp.exp(sink - m)
    inv_l = 1.0 / (jnp.sum(p, axis=0, keepdims=True) + e_sink)
    return p * inv_l, e_sink * inv_l


def _swa_fwd(qbt, kb, vbt, sinks, tq):
    s = qbt.shape[1]
    gw = SWA_GROUP * HEAD_DIM

    def body(q_ref, k_ref, vb_ref, vc_ref, s_ref, o_ref):
        i = pl.program_id(1)
        mask = _swa_mask(i, tq)
        kw = _swa_rows(k_ref, i, tq)
        vtw = jnp.concatenate([vb_ref[0], vc_ref[0]], axis=1)
        sk = s_ref[0]
        for hh in range(SWA_GROUP):
            rows = slice(hh * HEAD_DIM, (hh + 1) * HEAD_DIM)
            sink = sk[:, hh:hh + 1]
            sc = jnp.where(mask, _dot(kw, q_ref[rows, :]), NEG)
            m = jnp.maximum(jnp.max(sc, axis=0, keepdims=True), sink)
            acc = _dot(vtw, jnp.exp(sc - m).astype(BF))
            o_ref[rows, :] = acc[:HEAD_DIM] / (acc[HEAD_DIM:HEAD_DIM + 1] + jnp.exp(sink - m))

    kvspec = pl.BlockSpec((1, s, HEAD_DIM), lambda g, i: (g, 0, 0))
    qspec = pl.BlockSpec((gw, tq), lambda g, i: (g, i))
    return _call(
        body, name="swa_fwd", out_shape=jax.ShapeDtypeStruct((SWA_W, s), F32), grid=(SWA_KV_HEADS, s // tq),
        in_specs=[qspec, kvspec, _swa_before(VT_ROWS, tq), pl.BlockSpec((1, VT_ROWS, tq), lambda g, i: (g, 0, i)),
                  pl.BlockSpec((1, 1, SWA_GROUP), lambda g, i: (g, 0, 0))],
        out_specs=qspec, compiler_params=_params(("parallel", "parallel")),
    )(qbt, kb, vbt, vbt, sinks)


def _swa_bwd(qb, qbt, kb, kbt, vb, sinks, dob, dobt, tq):
    s = qb.shape[0]
    gw = SWA_GROUP * HEAD_DIM

    def body(q_ref, qt_ref, k_ref, ktb_ref, ktc_ref, v_ref, s_ref, do_ref, dot_ref, dq_ref, dk_ref, dv_ref, ds_ref):
        i = pl.program_id(1)

        @pl.when(i == 0)
        def _():
            dk_ref[...] = jnp.zeros(dk_ref.shape, F32)
            dv_ref[...] = jnp.zeros(dv_ref.shape, F32)
            ds_ref[...] = jnp.zeros(ds_ref.shape, F32)

        mask = _swa_mask(i, tq)
        kw = _swa_rows(k_ref, i, tq)
        vw = _swa_rows(v_ref, i, tq)
        ktw = jnp.concatenate([ktb_ref[0], ktc_ref[0]], axis=1)
        qv, dov = q_ref[...], do_ref[...]
        sk = s_ref[0]
        dsinks = []
        dk_acc = jnp.zeros((tq + WINDOW, HEAD_DIM), F32)
        dv_acc = jnp.zeros((tq + WINDOW, HEAD_DIM), F32)
        for hh in range(SWA_GROUP):
            rows = slice(hh * HEAD_DIM, (hh + 1) * HEAD_DIM)
            p, p_sink = _swa_probs_t(kw, qt_ref[rows, :], mask, sk[:, hh:hh + 1])
            dp = _dot(vw, dot_ref[rows, :])
            delta = jnp.sum(p * dp, axis=0, keepdims=True)
            dsc = (p * (dp - delta)).astype(BF)
            dq_ref[rows, :] = _dot(ktw, dsc)
            dk_acc = dk_acc + _dot(dsc, qv[:, rows])
            dv_acc = dv_acc + _dot(p.astype(BF), dov[:, rows])
            dsinks.append(-jnp.sum(p_sink * delta, axis=1, keepdims=True))
        before = pl.ds(pl.multiple_of(jnp.maximum(i * tq - WINDOW, 0), WINDOW), WINDOW)
        own = pl.ds(pl.multiple_of(i * tq, tq), tq)
        dk_ref[0, before, :] += dk_acc[:WINDOW]
        dk_ref[0, own, :] += dk_acc[WINDOW:]
        dv_ref[0, before, :] += dv_acc[:WINDOW]
        dv_ref[0, own, :] += dv_acc[WINDOW:]
        ds_ref[0] += jnp.concatenate(dsinks, axis=1)

    kvspec = pl.BlockSpec((1, s, HEAD_DIM), lambda g, i: (g, 0, 0))
    qspec = pl.BlockSpec((tq, gw), lambda g, i: (i, g))
    qspec_t = pl.BlockSpec((gw, tq), lambda g, i: (g, i))
    sspec = pl.BlockSpec((1, 1, SWA_GROUP), lambda g, i: (g, 0, 0))
    kvshape = jax.ShapeDtypeStruct((SWA_KV_HEADS, s, HEAD_DIM), F32)
    return _call(
        body, name="swa_bwd",
        out_shape=(jax.ShapeDtypeStruct((SWA_W, s), F32), kvshape, kvshape,
                   jax.ShapeDtypeStruct((SWA_KV_HEADS, 1, SWA_GROUP), F32)),
        grid=(SWA_KV_HEADS, s // tq),
        in_specs=[qspec, qspec_t, kvspec, _swa_before(HEAD_DIM, tq),
                  pl.BlockSpec((1, HEAD_DIM, tq), lambda g, i: (g, 0, i)), kvspec, sspec, qspec, qspec_t],
        out_specs=(qspec_t, kvspec, kvspec, sspec),
        compiler_params=_params(("parallel", "arbitrary")),
    )(qb, qbt, kb, kbt, kbt, vb, sinks, dob, dobt)


def _pairs_to_rows(ref, n_rows=HEAD_DIM):
    parts = []
    for a in range(0, FOX_HEADS, 2):
        parts.append(jnp.concatenate([ref[a][:n_rows], ref[a + 1][:n_rows]], axis=0).T)
    return jnp.concatenate(parts, axis=1)


def _blocks_to_rows(ref):
    return jnp.concatenate([ref[a:a + LANES, :].T for a in range(0, ref.shape[0], LANES)], axis=1)


def _out_proj(oat, za, obt, zb, x, tgt, w_out, w_out_t, gate, g_post, tm):
    s = x.shape[0]

    def body(oat_ref, za_ref, obt_ref, zb_ref, x_ref, t_ref, w_ref, wt_ref, gate_ref, gp_ref,
             dout_ref, doat_ref, dla_ref, dza_ref, dob_ref, dobt_ref, dzb_ref, gw_ref, dgate_ref, dgp_ref, loss_ref):
        i = pl.program_id(0)

        @pl.when(i == 0)
        def _():
            gw_ref[...] = jnp.zeros(gw_ref.shape, F32)
            dgate_ref[...] = jnp.zeros(dgate_ref.shape, F32)
            dgp_ref[...] = jnp.zeros(dgp_ref.shape, F32)
            loss_ref[...] = jnp.zeros(loss_ref.shape, F32)

        oa_v = _pairs_to_rows(oat_ref)
        ob_v = _blocks_to_rows(obt_ref)
        za_v, zb_v = za_ref[...], zb_ref[...]
        sga, sgb = _sigmoid(za_v), _sigmoid(zb_v)
        sila, silb = za_v * sga, zb_v * sgb
        u = jnp.concatenate([oa_v * sila, ob_v * silb], axis=1).astype(BF)
        yv = _dot(u, w_ref[...])
        yhat, rstd = _rms_hat(yv)
        gp, gate_v = gp_ref[...], gate_ref[...]
        nrm = yhat * gp
        diff = (x_ref[...] + gate_v * nrm) - t_ref[...]
        loss_ref[...] += 0.5 * jnp.sum(jnp.sum(diff * diff, axis=1, keepdims=True), axis=0, keepdims=True) / D_MODEL
        dout = diff * (1.0 / D_MODEL)
        dout_ref[...] = dout
        dgate_ref[...] += jnp.sum(dout * nrm, axis=0, keepdims=True)
        dn = dout * gate_v
        dgp_ref[...] += jnp.sum(dn * yhat, axis=0, keepdims=True)
        dyhat = dn * gp
        dy = (rstd * (dyhat - yhat * jnp.mean(dyhat * yhat, axis=1, keepdims=True))).astype(BF)
        gw_ref[...] += _dot_tn(u, dy)
        du = _dot(dy, wt_ref[...])
        dua, dub = du[:, :FOX_W], du[:, FOX_W:]
        doa = dua * sila
        for a in range(0, FOX_HEADS, 2):
            pair_t = doa[:, a * HEAD_DIM:(a + 2) * HEAD_DIM].T
            for hd, rows in ((a, slice(0, HEAD_DIM)), (a + 1, slice(HEAD_DIM, 2 * HEAD_DIM))):
                doat_ref[hd] = pair_t[rows].astype(BF)
                dla_ref[hd] = jnp.sum(pair_t[rows] * oat_ref[hd], axis=0, keepdims=True)
        dob = dub * silb
        dob_ref[...] = dob.astype(BF)
        for a in range(0, SWA_W, LANES):
            dobt_ref[a:a + LANES, :] = dob[:, a:a + LANES].T.astype(BF)
        dza_ref[...] = (dua * oa_v * (sga * (1.0 + za_v * (1.0 - sga)))).astype(BF)
        dzb_ref[...] = (dub * ob_v * (sgb * (1.0 + zb_v * (1.0 - sgb)))).astype(BF)

    row = lambda w: pl.BlockSpec((tm, w), lambda i: (i, 0))
    heads_t = lambda w: pl.BlockSpec((FOX_HEADS, w, tm), lambda i: (0, 0, i))
    vec = _full((1, D_MODEL))
    mat = _full((D_MODEL, D_MODEL))
    out_shape = (
        jax.ShapeDtypeStruct((s, D_MODEL), F32),
        jax.ShapeDtypeStruct((FOX_HEADS, HEAD_DIM, s), BF), jax.ShapeDtypeStruct((FOX_HEADS, 1, s), F32),
        jax.ShapeDtypeStruct((s, FOX_W), BF), jax.ShapeDtypeStruct((s, SWA_W), BF), jax.ShapeDtypeStruct((SWA_W, s), BF),
        jax.ShapeDtypeStruct((s, SWA_W), BF),
        jax.ShapeDtypeStruct((D_MODEL, D_MODEL), F32),
        jax.ShapeDtypeStruct((1, D_MODEL), F32), jax.ShapeDtypeStruct((1, D_MODEL), F32),
        jax.ShapeDtypeStruct((1, 1), F32),
    )
    col = pl.BlockSpec((SWA_W, tm), lambda i: (0, i))
    return _call(
        body, name="out_proj", out_shape=out_shape, grid=(s // tm,),
        in_specs=[heads_t(HEAD_DIM), row(FOX_W), col, row(SWA_W), row(D_MODEL), row(D_MODEL), mat, mat, vec, vec],
        out_specs=(row(D_MODEL), heads_t(HEAD_DIM), heads_t(1), row(FOX_W), row(SWA_W), col, row(SWA_W), mat, vec, vec,
                   _full((1, 1))),
        compiler_params=_params(("arbitrary",)),
    )(oat, za, obt, zb, x, tgt, w_out, w_out_t, gate, g_post)


def _assemble_dproj(dqt, dkt, dvt, dza, dqb, dzb, dkb, dvb, df, cos_t, sin_t, tm):
    s = dza.shape[0]

    def body(dqt_ref, dkt_ref, dvt_ref, dza_ref, dqb_ref, dzb_ref, dkb_ref, dvb_ref, df_ref, cos_ref, sin_ref, o_ref):
        def cat(ref, n):
            return jnp.concatenate([ref[hd] for hd in range(n)], axis=1)

        cos2, sin2 = cos_ref[...], sin_ref[...]
        cos8 = jnp.concatenate([cos2] * 4, axis=1)
        sin8 = jnp.concatenate([sin2] * 4, axis=1)
        scale = HEAD_DIM ** -0.5
        o_ref[:, C_QA:C_QA + FOX_W] = (_pairs_to_rows(dqt_ref.at[:, 0]) * scale).astype(BF)
        o_ref[:, C_KA:C_KA + FOX_W] = (_pairs_to_rows(dkt_ref) * LN2).astype(BF)
        o_ref[:, C_VA:C_VA + FOX_W] = _pairs_to_rows(dvt_ref).astype(BF)
        o_ref[:, C_ZA:C_ZA + FOX_W] = dza_ref[...]
        dq = _blocks_to_rows(dqb_ref) * scale
        o_ref[:, C_QB:C_QB + SWA_W] = (dq * cos8 - _rope_partner(dq) * sin8).astype(BF)
        o_ref[:, C_ZB:C_ZB + SWA_W] = dzb_ref[...]
        dk = cat(dkb_ref, SWA_KV_HEADS)
        o_ref[:, C_KB:C_KB + SWA_KV_W] = (dk * cos2 - _rope_partner(dk) * sin2).astype(BF)
        o_ref[:, C_VB:C_VB + SWA_KV_W] = cat(dvb_ref, SWA_KV_HEADS).astype(BF)
        o_ref[:, C_F:C_F + LANES] = df_ref[...].astype(BF)

    row = lambda w: pl.BlockSpec((tm, w), lambda i: (i, 0))
    heads = lambda n: pl.BlockSpec((n, tm, HEAD_DIM), lambda i: (0, i, 0))
    heads_t = lambda w: pl.BlockSpec((FOX_HEADS, w, tm), lambda i: (0, 0, i))
    return _call(
        body, name="assemble_dproj", out_shape=jax.ShapeDtypeStruct((s, WP), BF), grid=(s // tm,),
        in_specs=[pl.BlockSpec((FOX_HEADS, 1, AUG_DIM, tm), lambda i: (0, i, 0, 0)), heads_t(AUG_DIM), heads_t(HEAD_DIM),
                  row(FOX_W), pl.BlockSpec((SWA_W, tm), lambda i: (0, i)), row(SWA_W), heads(SWA_KV_HEADS),
                  heads(SWA_KV_HEADS), row(LANES), row(LANES), row(LANES)],
        out_specs=row(WP), compiler_params=_params(("parallel",)),
    )(dqt, dkt, dvt, dza, dqb, dzb, dkb, dvb, df, cos_t, sin_t)


def _in_proj_bwd_x(dproj, w_al_t, x, dout, g_pre, scale1p, tm):
    s = x.shape[0]

    def body(dp_ref, wt_ref, x_ref, dout_ref, g_ref, sc_ref, gx_ref, dsh_ref, dsc_ref, dg_ref):
        i = pl.program_id(0)

        @pl.when(i == 0)
        def _():
            dsh_ref[...] = jnp.zeros(dsh_ref.shape, F32)
            dsc_ref[...] = jnp.zeros(dsc_ref.shape, F32)
            dg_ref[...] = jnp.zeros(dg_ref.shape, F32)

        dh = _dot(dp_ref[...], wt_ref[...])
        xhat, rstd = _rms_hat(x_ref[...])
        g, sc = g_ref[...], sc_ref[...]
        dsh_ref[...] += jnp.sum(dh, axis=0, keepdims=True)
        dhx = dh * xhat
        dsc_ref[...] += jnp.sum(dhx * g, axis=0, keepdims=True)
        dg_ref[...] += jnp.sum(dhx * sc, axis=0, keepdims=True)
        dxhat = dh * (g * sc)
        gx_ref[...] = dout_ref[...] + rstd * (dxhat - xhat * jnp.mean(dxhat * xhat, axis=1, keepdims=True))

    row = lambda w: pl.BlockSpec((tm, w), lambda i: (i, 0))
    vec = _full((1, D_MODEL))
    vshape = jax.ShapeDtypeStruct((1, D_MODEL), F32)
    return _call(
        body, name="in_proj_bwd_x", out_shape=(jax.ShapeDtypeStruct((s, D_MODEL), F32), vshape, vshape, vshape),
        grid=(s // tm,),
        in_specs=[row(WP), _full((WP, D_MODEL)), row(D_MODEL), row(D_MODEL), vec, vec],
        out_specs=(row(D_MODEL), vec, vec, vec), compiler_params=_params(("arbitrary",)),
    )(dproj, w_al_t, x, dout, g_pre, scale1p)


def _in_proj_bwd_w(h, dproj, tk, tn):
    s = h.shape[0]

    def body(h_ref, dp_ref, gw_ref):
        @pl.when(pl.program_id(1) == 0)
        def _():
            gw_ref[...] = jnp.zeros(gw_ref.shape, F32)

        gw_ref[...] += _dot_tn(h_ref[...], dp_ref[...])

    return _call(
        body, name="in_proj_bwd_w", out_shape=jax.ShapeDtypeStruct((D_MODEL, WP), F32), grid=(WP // tn, s // tk),
        in_specs=[pl.BlockSpec((tk, D_MODEL), lambda n, k: (k, 0)), pl.BlockSpec((tk, tn), lambda n, k: (k, n))],
        out_specs=pl.BlockSpec((D_MODEL, tn), lambda n, k: (0, n)),
        compiler_params=_params(("parallel", "arbitrary")),
    )(h, dproj)


def _align_w_in(w_cols):
    def part(name, width):
        return w_cols[:, _SRC[name]:_SRC[name] + width]

    fpad = jnp.pad(part("fa", FOX_HEADS), ((0, 0), (0, LANES - FOX_HEADS)))
    return jnp.concatenate([part("qa", FOX_W), part("ka", FOX_W), part("va", FOX_W), part("za", FOX_W),
                            part("qb", SWA_W), part("zb", SWA_W), part("kb", SWA_KV_W), part("vb", SWA_KV_W), fpad], axis=1)


def _unalign_w_in(g_al):
    def part(c0, width):
        return g_al[:, c0:c0 + width]

    return jnp.concatenate([part(C_QA, FOX_W), part(C_KA, FOX_W), part(C_VA, FOX_W), part(C_F, FOX_HEADS),
                            part(C_ZA, FOX_W), part(C_QB, SWA_W), part(C_KB, SWA_KV_W), part(C_VB, SWA_KV_W),
                            part(C_ZB, SWA_W)], axis=1)


def _rope_tables(positions):
    inv_freq = ROPE_THETA ** (-jnp.arange(HALF, dtype=F32) / HALF)
    ang = positions.astype(F32)[:, None] * inv_freq
    cos, sin = jnp.cos(ang), jnp.sin(ang)
    return jnp.concatenate([cos, cos, cos, cos], axis=1), jnp.concatenate([-sin, sin, -sin, sin], axis=1)


def _tiles(s):
    if s >= 4096:
        return dict(tm=512, blk=512, bq=2048, bk=512, chunk=256, tq=256, tm_out=256, tk=512, tn=1152)
    return dict(tm=128, blk=128, bq=256, bk=128, chunk=128, tq=128, tm_out=128, tk=128, tn=1152)


def kernel(x, c, positions, w_ada, b_ada, g_pre, w_in, b_fgate, sinks, w_out, g_post, loss_target, m_w_ada, m_b_ada, m_g_pre, m_w_in, m_b_fgate, m_sinks, m_w_out, m_g_post, v_w_ada, v_b_ada, v_g_pre, v_w_in, v_b_fgate, v_sinks, v_w_out, v_g_post):
    s = x.shape[1]
    t = _tiles(s)
    nc = s // LANES
    rows = FOX_HEADS * nc
    me = 4 * lax.axis_index("x") + 2 * lax.axis_index("y") + lax.axis_index("c")
    chip = 2 * lax.axis_index("x") + lax.axis_index("y")
    core = lax.axis_index("c")
    x2, tgt = x[0], loss_target[0]

    c_all = _allgather_devices(c, "gather_c")[:, 0, :]
    a_all, mod_shard = _ada_shard(c_all, w_ada[0])
    mod_all = _allgather_devices(mod_shard, "gather_mod")
    mod_rows = lax.dynamic_index_in_dim(mod_all, me, axis=1, keepdims=False)
    mod = mod_rows.reshape(N_CHIPS, 2, W_ADA_SHARD)[:, 0, :].reshape(1, 3 * D_MODEL) + b_ada
    shift, scale1p, gate = mod[:, :D_MODEL], 1.0 + mod[:, D_MODEL:2 * D_MODEL], mod[:, 2 * D_MODEL:]

    w_in_pad = jnp.pad(w_in[0].astype(BF), ((0, 0), (0, W_IN_SHARD_PAD - W_IN_SHARD)))
    w_in_all = _allgather_chips(w_in_pad, "gather_w_in")
    w_cols = jnp.concatenate([w_in_all[k, :, :W_IN_SHARD] for k in range(N_CHIPS)], axis=1)
    w_al = _align_w_in(w_cols)
    w_al_t = w_al.T
    w_out_all = _allgather_chips(w_out[0].astype(BF), "gather_w_out").reshape(D_MODEL, D_MODEL)
    w_out_t = w_out_all.T

    cos_t, sin_t = _rope_tables(positions[0])

    f_pad = _forget_logits(x2, g_pre, scale1p, shift, w_al[:, C_F:], t["tm"])
    f_rows = f_pad[:, :FOX_HEADS].T.reshape(rows, LANES)
    bias_rows = jnp.repeat(b_fgate[0], nc)[:, None]
    cum = _log_forget_cumsum(f_rows, bias_rows, nc).reshape(FOX_HEADS, s)
    h, qat, ka, kat, va, vat, za, zb, qb, kb, vb, qbt, kbt, vbt = _in_proj(
        x2, g_pre, scale1p, shift, w_al[:, C_VA:C_F], w_al_t[:C_ZA], cum, cos_t, sin_t, t["tm"])
    oat, lse = _fox_fwd(qat, ka, vat, t["bq"], t["bk"], t["chunk"])
    sinks_g = sinks.reshape(SWA_KV_HEADS, 1, SWA_GROUP)
    obt = _swa_fwd(qbt, kb, vbt, sinks_g, t["tq"])

    dout, doat, delta_a, dza, dob, dobt, dzb, gw_out, dgate, dg_post, loss_part = _out_proj(
        oat, za, obt, zb, x2, tgt, w_out_all, w_out_t, gate, g_post, t["tm_out"])

    dqt, dkt, dvt = _fox_bwd(qat, ka, kat, va, doat, lse, delta_a, t["bq"], t["bk"], t["chunk"], t["blk"])
    dcum = dqt[:, :, HEAD_DIM, :].reshape(FOX_HEADS, s) - dkt[:, HEAD_DIM, :]
    df_rows, db_heads = _log_forget_cumsum_bwd(dcum.reshape(rows, LANES), f_rows, bias_rows, nc)
    df_pad = jnp.pad(df_rows.reshape(FOX_HEADS, s).T, ((0, 0), (0, LANES - FOX_HEADS)))
    dqb, dkb, dvb, dsinks = _swa_bwd(qb, qbt, kb, kbt, vb, sinks_g, dob, dobt, t["tq"])

    dproj = _assemble_dproj(dqt, dkt, dvt, dza, dqb, dzb, dkb, dvb, df_pad, cos_t, sin_t, t["blk"])
    grad_x, dshift, dscale, dg_pre = _in_proj_bwd_x(dproj, w_al_t, x2, dout, g_pre, scale1p, t["tm_out"])
    gw_in = _unalign_w_in(_in_proj_bwd_w(h, dproj, t["tk"], t["tn"]))

    pad_lane = lambda vrow: jnp.pad(vrow, ((0, 0), (0, LANES - vrow.shape[1])))
    packed = jnp.concatenate([dshift, dscale, dgate, dg_pre, dg_post,
                              pad_lane(db_heads.reshape(1, FOX_HEADS)), pad_lane(dsinks.reshape(1, FOX_HEADS)),
                              pad_lane(loss_part)], axis=1)
    parts = _allgather_devices(packed, "gather_partials")
    tot = _sum_devices(parts)
    loss = tot[0, P_LOSS]
    g_b_ada = tot[:, P_DMOD:P_DMOD + 3 * D_MODEL]
    g_g_pre = tot[:, P_GPRE:P_GPRE + D_MODEL]
    g_g_post = tot[:, P_GPOST:P_GPOST + D_MODEL]
    g_b_fgate = tot[:, P_BF:P_BF + FOX_HEADS]
    g_sinks = tot[:, P_SINK:P_SINK + FOX_HEADS]
    dm_shard = lax.dynamic_slice_in_dim(parts[:, 0, :3 * D_MODEL], chip * W_ADA_SHARD, W_ADA_SHARD, axis=1)
    g_w_ada = _grad_w_ada(a_all.T, dm_shard)

    gin = jnp.pad(gw_in.reshape(D_MODEL, N_CHIPS, W_IN_SHARD).transpose(1, 0, 2),
                  ((0, 0), (0, 0), (0, W_IN_SHARD_PAD - W_IN_SHARD)))
    gout = gw_out.reshape(N_CHIPS, D_MODEL, W_OUT_SHARD)
    gbig = jnp.concatenate([gin, gout], axis=2)
    half = D_MODEL // 2
    gw = W_IN_SHARD_PAD + W_OUT_SHARD
    keep = lax.dynamic_slice_in_dim(gbig, core * half, half, axis=1)
    give = lax.dynamic_slice_in_dim(gbig, (1 - core) * half, half, axis=1)
    got = _swap_sibling(give.reshape(N_CHIPS * half, gw), "swap_grad_halves")
    pair = _add(keep.reshape(N_CHIPS * half, gw), got, "add_pair", BF).reshape(N_CHIPS, half, gw)
    from_chips = _exchange_chips(pair, "exchange_grad")
    mine = _sum_chips(from_chips, "sum_chips")
    other = _swap_sibling(mine, "swap_grad_result")
    lo = jnp.where(core == 0, mine, other)
    hi = jnp.where(core == 0, other, mine)
    gfull = jnp.concatenate([lo, hi], axis=0)
    g_w_in = gfull[:, :W_IN_SHARD]
    g_w_out = gfull[:, W_IN_SHARD_PAD:].reshape(W_OUT_SHARD, D_MODEL)

    grads = dict(w_ada=g_w_ada, b_ada=g_b_ada, g_pre=g_g_pre, w_in=g_w_in, b_fgate=g_b_fgate, sinks=g_sinks,
                 w_out=g_w_out, g_post=g_g_post)
    weights = dict(w_ada=w_ada, b_ada=b_ada, g_pre=g_pre, w_in=w_in, b_fgate=b_fgate, sinks=sinks, w_out=w_out, g_post=g_post)
    moms = dict(w_ada=m_w_ada, b_ada=m_b_ada, g_pre=m_g_pre, w_in=m_w_in, b_fgate=m_b_fgate, sinks=m_sinks, w_out=m_w_out, g_post=m_g_post)
    vars_ = dict(w_ada=v_w_ada, b_ada=v_b_ada, g_pre=v_g_pre, w_in=v_w_in, b_fgate=v_b_fgate, sinks=v_sinks, w_out=v_w_out, g_post=v_g_post)
    names = ["w_ada", "b_ada", "g_pre", "w_in", "b_fgate", "sinks", "w_out", "g_post"]
    g_out, d_out, m_out, v_out = [], [], [], []
    for n in names:
        shape = weights[n].shape
        w2 = weights[n].reshape(shape[-2], shape[-1])
        g2 = grads[n].reshape(w2.shape)
        d2, nm2, nv2 = _adamw(w2, g2, moms[n].reshape(w2.shape), vars_[n].reshape(w2.shape), "adamw_" + n)
        g_out.append(g2.reshape(shape))
        d_out.append(d2.reshape(shape))
        m_out.append(nm2.reshape(shape))
        v_out.append(nv2.reshape(shape))
    return (loss, grad_x.reshape(x.shape), *g_out, *d_out, *m_out, *v_out)
```

```python
import functools

import jax
import jax.numpy as jnp
from jax import lax
from jax.experimental import pallas as pl
from jax.experimental.pallas import tpu as pltpu

_INTERPRET = False

D_MODEL = 1024
HEAD_DIM = 64
HALF = HEAD_DIM // 2
AUG_DIM = 128
AUG_ROWS = 8
VT_ROWS = 80
LOG2E = 1.4426950408889634
LN2 = 0.6931471805599453
Q_SCALE = LOG2E * 64 ** -0.5
FOX_HEADS = 8
FOX_W = 512
SWA_W = 512
SWA_KV_HEADS = 2
SWA_GROUP = 4
SWA_KV_W = 128
WINDOW = 128
ROPE_THETA = 10000.0
RMS_EPS = 1e-6
IN_WIDTH = 3336
N_CHIPS = 4
N_DEV = 8
W_IN_SHARD = IN_WIDTH // N_CHIPS
W_IN_SHARD_PAD = 896
W_ADA_SHARD = 3 * D_MODEL // N_CHIPS
W_OUT_SHARD = D_MODEL // N_CHIPS
LANES = 128

_SRC = dict(qa=0, ka=512, va=1024, fa=1536, za=1544, qb=2056, kb=2568, vb=2696, zb=2824)
C_QA, C_KA, C_VA, C_ZA, C_QB, C_ZB, C_KB, C_VB, C_F = 0, 512, 1024, 1536, 2048, 2560, 3072, 3200, 3328
WP = 3456

ADAM_LR = 0.001
ADAM_B1 = 0.9
ADAM_B2 = 0.999
ADAM_EPS = 1e-08
ADAM_WD = 0.01
ADAM_STEP = 10

VMEM_LIMIT = 56 * 1024 * 1024
NEG = -1e30
OVERFLOW_GUARD = 1e30
MESH = pl.DeviceIdType.MESH
BF = jnp.bfloat16
F32 = jnp.float32

P_DMOD, P_GPRE, P_GPOST, P_BF, P_SINK, P_LOSS, P_LEN = 0, 3072, 4096, 5120, 5248, 5376, 5504


def _call(body, **kw):
    return pl.pallas_call(body, interpret=_INTERPRET, **kw)


def _params(sem=None, **kw):
    return pltpu.CompilerParams(dimension_semantics=sem, vmem_limit_bytes=VMEM_LIMIT, **kw)


def _full(shape):
    zeros = (0,) * len(shape)
    return pl.BlockSpec(shape, lambda *_: zeros)


def _dot(a, b):
    return jnp.dot(a, b, preferred_element_type=F32)


def _dot_nt(a, b):
    return lax.dot_general(a, b, (((1,), (1,)), ((), ())), preferred_element_type=F32)


def _dot_tn(a, b):
    return lax.dot_general(a, b, (((0,), (0,)), ((), ())), preferred_element_type=F32)


def _sigmoid(z):
    return 1.0 / (1.0 + jnp.exp(-z))


def _rope_partner(t):
    w = t.shape[-1]
    lane = lax.broadcasted_iota(jnp.int32, t.shape, t.ndim - 1)
    return jnp.where((lane & (HEAD_DIM - 1)) < HALF, pltpu.roll(t, w - HALF, t.ndim - 1), pltpu.roll(t, HALF, t.ndim - 1))


def _allgather_devices(v, name):
    r, cdim = v.shape
    masks = [(dx, dy, dc) for dx in (0, 1) for dy in (0, 1) for dc in (0, 1)][1:]

    def body(v_ref, out_ref, send_sems, recv_sems):
        x, y, c = lax.axis_index("x"), lax.axis_index("y"), lax.axis_index("c")
        me = 4 * x + 2 * y + c
        out_ref[me] = v_ref[...]
        copies = []
        for k, (dx, dy, dc) in enumerate(masks):
            cp = pltpu.make_async_remote_copy(
                src_ref=v_ref, dst_ref=out_ref.at[me], send_sem=send_sems.at[k], recv_sem=recv_sems.at[k],
                device_id=(x ^ dx, y ^ dy, c ^ dc), device_id_type=MESH)
            cp.start()
            copies.append(cp)
        for k, (dx, dy, dc) in enumerate(masks):
            peer = 4 * (x ^ dx) + 2 * (y ^ dy) + (c ^ dc)
            pltpu.make_async_remote_copy(
                src_ref=v_ref, dst_ref=out_ref.at[peer], send_sem=send_sems.at[k], recv_sem=recv_sems.at[k],
                device_id=(x ^ dx, y ^ dy, c ^ dc), device_id_type=MESH).wait_recv()
        for cp in copies:
            cp.wait_send()

    return _call(
        body, name=name, out_shape=jax.ShapeDtypeStruct((N_DEV, r, cdim), v.dtype),
        in_specs=[pl.BlockSpec(memory_space=pltpu.VMEM)], out_specs=pl.BlockSpec(memory_space=pltpu.VMEM),
        scratch_shapes=[pltpu.SemaphoreType.DMA((7,)), pltpu.SemaphoreType.DMA((7,))],
        compiler_params=pltpu.CompilerParams(has_side_effects=True),
    )(v)


def _allgather_chips(v, name):
    r, cdim = v.shape
    masks = [(1, 0), (0, 1), (1, 1)]

    def body(v_ref, out_ref, send_sems, recv_sems, local_sem):
        x, y, c = lax.axis_index("x"), lax.axis_index("y"), lax.axis_index("c")
        me = 2 * x + y
        mine = pltpu.make_async_copy(v_ref, out_ref.at[me], local_sem)
        mine.start()
        copies = []
        for k, (dx, dy) in enumerate(masks):
            cp = pltpu.make_async_remote_copy(
                src_ref=v_ref, dst_ref=out_ref.at[me], send_sem=send_sems.at[k], recv_sem=recv_sems.at[k],
                device_id=(x ^ dx, y ^ dy, c), device_id_type=MESH)
            cp.start()
            copies.append(cp)
        for k, (dx, dy) in enumerate(masks):
            peer = 2 * (x ^ dx) + (y ^ dy)
            pltpu.make_async_remote_copy(
                src_ref=v_ref, dst_ref=out_ref.at[peer], send_sem=send_sems.at[k], recv_sem=recv_sems.at[k],
                device_id=(x ^ dx, y ^ dy, c), device_id_type=MESH).wait_recv()
        for cp in copies:
            cp.wait_send()
        mine.wait()

    return _call(
        body, name=name, out_shape=jax.ShapeDtypeStruct((N_CHIPS, r, cdim), v.dtype),
        in_specs=[pl.BlockSpec(memory_space=pl.ANY)], out_specs=pl.BlockSpec(memory_space=pl.ANY),
        scratch_shapes=[pltpu.SemaphoreType.DMA((3,)), pltpu.SemaphoreType.DMA((3,)), pltpu.SemaphoreType.DMA],
        compiler_params=pltpu.CompilerParams(has_side_effects=True),
    )(v)


def _swap_sibling(v, name):
    def body(v_ref, out_ref, send_sem, recv_sem):
        x, y, c = lax.axis_index("x"), lax.axis_index("y"), lax.axis_index("c")
        cp = pltpu.make_async_remote_copy(
            src_ref=v_ref, dst_ref=out_ref, send_sem=send_sem, recv_sem=recv_sem,
            device_id=(x, y, 1 - c), device_id_type=MESH)
        cp.start()
        cp.wait()

    return _call(
        body, name=name, out_shape=jax.ShapeDtypeStruct(v.shape, v.dtype),
        in_specs=[pl.BlockSpec(memory_space=pl.ANY)], out_specs=pl.BlockSpec(memory_space=pl.ANY),
        scratch_shapes=[pltpu.SemaphoreType.DMA, pltpu.SemaphoreType.DMA],
        compiler_params=pltpu.CompilerParams(has_side_effects=True),
    )(v)


def _exchange_chips(v, name):
    _, r, cdim = v.shape
    masks = [(1, 0), (0, 1), (1, 1)]

    def body(v_ref, out_ref, send_sems, recv_sems, local_sem):
        x, y, c = lax.axis_index("x"), lax.axis_index("y"), lax.axis_index("c")
        me = 2 * x + y
        mine = pltpu.make_async_copy(v_ref.at[me], out_ref.at[me], local_sem)
        mine.start()
        copies = []
        for k, (dx, dy) in enumerate(masks):
            peer = 2 * (x ^ dx) + (y ^ dy)
            cp = pltpu.make_async_remote_copy(
                src_ref=v_ref.at[peer], dst_ref=out_ref.at[me], send_sem=send_sems.at[k], recv_sem=recv_sems.at[k],
                device_id=(x ^ dx, y ^ dy, c), device_id_type=MESH)
            cp.start()
            copies.append(cp)
        for k, (dx, dy) in enumerate(masks):
            peer = 2 * (x ^ dx) + (y ^ dy)
            pltpu.make_async_remote_copy(
                src_ref=v_ref.at[me], dst_ref=out_ref.at[peer], send_sem=send_sems.at[k], recv_sem=recv_sems.at[k],
                device_id=(x ^ dx, y ^ dy, c), device_id_type=MESH).wait_recv()
        for cp in copies:
            cp.wait_send()
        mine.wait()

    return _call(
        body, name=name, out_shape=jax.ShapeDtypeStruct(v.shape, v.dtype),
        in_specs=[pl.BlockSpec(memory_space=pl.ANY)], out_specs=pl.BlockSpec(memory_space=pl.ANY),
        scratch_shapes=[pltpu.SemaphoreType.DMA((3,)), pltpu.SemaphoreType.DMA((3,)), pltpu.SemaphoreType.DMA],
        compiler_params=pltpu.CompilerParams(has_side_effects=True),
    )(v)


def _ada_shard(c_all, w_ada_shard):
    def body(c_ref, w_ref, a_ref, mod_ref):
        cv = c_ref[...]
        a = cv * _sigmoid(cv)
        a_ref[...] = a
        mod_ref[...] = _dot(a.astype(BF), w_ref[...].astype(BF))

    return _call(
        body, name="ada_shard",
        out_shape=(jax.ShapeDtypeStruct((N_DEV, D_MODEL), F32), jax.ShapeDtypeStruct((N_DEV, W_ADA_SHARD), F32)),
        compiler_params=_params(),
    )(c_all, w_ada_shard)


def _grad_w_ada(a_t, dm_shard):
    def body(a_ref, dm_ref, out_ref):
        acc = jnp.zeros((D_MODEL, W_ADA_SHARD), F32)
        for b in range(N_DEV):
            acc = acc + a_ref[:, b:b + 1] * dm_ref[b:b + 1, :]
        out_ref[...] = acc

    return _call(body, name="grad_w_ada", out_shape=jax.ShapeDtypeStruct((D_MODEL, W_ADA_SHARD), F32),
                 compiler_params=_params())(a_t, dm_shard)


def _sum_devices(parts):
    n = parts.shape[-1]

    def body(p_ref, out_ref):
        acc = p_ref[0]
        for b in range(1, N_DEV):
            acc = acc + p_ref[b]
        out_ref[...] = acc

    return _call(body, name="sum_devices", out_shape=jax.ShapeDtypeStruct((1, n), F32), compiler_params=_params())(parts)


def _add(a, b, name, out_dtype):
    r, cdim = a.shape
    tr = min(r, 256)

    def body(a_ref, b_ref, o_ref):
        o_ref[...] = (a_ref[...] + b_ref[...]).astype(out_dtype)

    spec = pl.BlockSpec((tr, cdim), lambda i: (i, 0))
    return _call(body, name=name, out_shape=jax.ShapeDtypeStruct(a.shape, out_dtype), grid=(r // tr,),
                 in_specs=[spec, spec], out_specs=spec, compiler_params=_params(("parallel",)))(a, b)


def _sum_chips(parts, name):
    _, r, cdim = parts.shape
    tr = min(r, 128)

    def body(p_ref, o_ref):
        o_ref[...] = ((p_ref[0].astype(F32) + p_ref[1].astype(F32)) + p_ref[2].astype(F32)) + p_ref[3].astype(F32)

    return _call(body, name=name, out_shape=jax.ShapeDtypeStruct((r, cdim), F32), grid=(r // tr,),
                 in_specs=[pl.BlockSpec((N_CHIPS, tr, cdim), lambda i: (0, i, 0))],
                 out_specs=pl.BlockSpec((tr, cdim), lambda i: (i, 0)), compiler_params=_params(("parallel",)))(parts)


def _adamw(w, g, m, v, name):
    r, cdim = w.shape[-2:]
    lead = w.ndim - 2
    tr = r if r <= 256 else 256
    c1 = 1.0 / (1.0 - ADAM_B1 ** ADAM_STEP)
    c2 = 1.0 / (1.0 - ADAM_B2 ** ADAM_STEP)

    def body(w_ref, g_ref, m_ref, v_ref, go_ref, d_ref, nm_ref, nv_ref):
        gv = g_ref[...].reshape(go_ref.shape)
        nm = ADAM_B1 * m_ref[...] + (1.0 - ADAM_B1) * gv
        nv = ADAM_B2 * v_ref[...] + (1.0 - ADAM_B2) * (gv * gv)
        m_hat = nm * c1
        v_hat = nv * c2
        go_ref[...] = gv
        d_ref[...] = -ADAM_LR * (m_hat / (jnp.sqrt(v_hat) + ADAM_EPS) + ADAM_WD * w_ref[...])
        nm_ref[...] = nm
        nv_ref[...] = nv

    spec = pl.BlockSpec((1,) * lead + (tr, cdim), lambda i: (0,) * lead + (i, 0))
    shp = jax.ShapeDtypeStruct(w.shape, F32)
    return _call(body, name=name, out_shape=(shp,) * 4, grid=(r // tr,),
                 in_specs=[spec, pl.BlockSpec((tr, cdim), lambda i: (i, 0)), spec, spec],
                 out_specs=(spec,) * 4, compiler_params=_params(("parallel",)))(w, g, m, v)


def _head_of_row(r, nc):
    assert nc & (nc - 1) == 0
    return lax.shift_right_logical(r, nc.bit_length() - 1)


def _chunk_mats(rows, nc, reverse):
    ri = lax.broadcasted_iota(jnp.int32, (rows, rows), 0)
    ci = lax.broadcasted_iota(jnp.int32, (rows, rows), 1)
    same = _head_of_row(ri, nc) == _head_of_row(ci, nc)
    between = jnp.where(same & ((ci > ri) if reverse else (ci < ri)), 1.0, 0.0).astype(F32)
    li = lax.broadcasted_iota(jnp.int32, (LANES, LANES), 0)
    lj = lax.broadcasted_iota(jnp.int32, (LANES, LANES), 1)
    within = jnp.where((li >= lj) if reverse else (li <= lj), 1.0, 0.0).astype(F32)
    return between, within


def _dot_hi(a, b):
    return jnp.dot(a, b, preferred_element_type=F32, precision=lax.Precision.HIGHEST)


def _scan_rows(t, nc, reverse):
    between, within = _chunk_mats(t.shape[0], nc, reverse)
    inner = _dot_hi(t, within)
    tot = jnp.sum(t, axis=1, keepdims=True)
    return inner + _dot_hi(between, jnp.broadcast_to(tot, t.shape))


def _log_forget_cumsum(f_rows, bias_rows, nc):
    def body(f_ref, b_ref, cum_ref):
        z = f_ref[...] + b_ref[...]
        lf = jnp.minimum(z, 0.0) - jnp.log(1.0 + jnp.exp(-jnp.abs(z)))
        cum_ref[...] = _scan_rows(lf, nc, False)

    return _call(body, name="forget_cumsum", out_shape=jax.ShapeDtypeStruct(f_rows.shape, F32),
                 compiler_params=_params())(f_rows, bias_rows)


def _log_forget_cumsum_bwd(dcum_rows, f_rows, bias_rows, nc):
    rows = f_rows.shape[0]

    def body(d_ref, f_ref, b_ref, df_ref, db_ref):
        dlf = _scan_rows(d_ref[...], nc, True)
        z = f_ref[...] + b_ref[...]
        df = dlf * _sigmoid(-z)
        df_ref[...] = df
        hi = lax.broadcasted_iota(jnp.int32, (FOX_HEADS, rows), 0)
        ri = lax.broadcasted_iota(jnp.int32, (FOX_HEADS, rows), 1)
        sel = jnp.where(_head_of_row(ri, nc) == hi, 1.0, 0.0).astype(F32)
        db_ref[...] = jnp.sum(_dot_hi(sel, df), axis=1, keepdims=True)

    return _call(body, name="forget_cumsum_bwd",
                 out_shape=(jax.ShapeDtypeStruct(f_rows.shape, F32), jax.ShapeDtypeStruct((FOX_HEADS, 1), F32)),
                 compiler_params=_params())(dcum_rows, f_rows, bias_rows)


def _rms_hat(xv):
    rstd = lax.rsqrt(jnp.mean(xv * xv, axis=-1, keepdims=True) + RMS_EPS)
    return xv * rstd, rstd


def _modulated(x_ref, g_ref, sc_ref, sh_ref):
    xhat, _ = _rms_hat(x_ref[...])
    return ((xhat * g_ref[...]) * sc_ref[...] + sh_ref[...]).astype(BF)


def _forget_logits(x, g_pre, scale1p, shift, w_f, tm):
    s = x.shape[0]

    def body(x_ref, g_ref, sc_ref, sh_ref, w_ref, f_ref):
        f_ref[...] = _dot(_modulated(x_ref, g_ref, sc_ref, sh_ref), w_ref[...])

    vec = _full((1, D_MODEL))
    return _call(
        body, name="forget_logits", out_shape=jax.ShapeDtypeStruct((s, LANES), F32), grid=(s // tm,),
        in_specs=[pl.BlockSpec((tm, D_MODEL), lambda i: (i, 0)), vec, vec, vec, _full((D_MODEL, LANES))],
        out_specs=pl.BlockSpec((tm, LANES), lambda i: (i, 0)), compiler_params=_params(("parallel",)),
    )(x, g_pre, scale1p, shift, w_f)


def _split3(v):
    hi = v.astype(BF).astype(F32)
    mid = (v - hi).astype(BF).astype(F32)
    lo = ((v - hi) - mid).astype(BF).astype(F32)
    return hi, mid, lo


def _in_proj(x, g_pre, scale1p, shift, w_rows, w_t_fox, cum, cos_t, sin_t, tm):
    s = x.shape[0]
    r_va, r_za, r_qb, r_zb, r_kb, r_vb = 0, 512, 1024, 1536, 2048, 2176

    def body(x_ref, g_ref, sc_ref, sh_ref, w_ref, wt_ref, cum_ref, cos_ref, sin_ref,
             h_ref, qat_ref, ka_ref, kat_ref, v_ref, vt_ref, za_ref, zb_ref, qb_ref, kb_ref, vb_ref,
             qbt_ref, kbt_ref, vbt_ref, mo_ref):
        hb = _modulated(x_ref, g_ref, sc_ref, sh_ref)
        h_ref[...] = hb

        def sec(c0, width):
            return _dot(hb, w_ref[:, c0:c0 + width])

        def sec_t(r0):
            return _dot_nt(wt_ref[r0:r0 + FOX_W, :], hb)

        q_t = sec_t(0) * Q_SCALE
        k_t = sec_t(FOX_W)
        v_t = sec_t(2 * FOX_W)
        va = sec(r_va, FOX_W)
        zeros = jnp.zeros((AUG_DIM - HEAD_DIM - AUG_ROWS, tm), F32)
        ri = lax.broadcasted_iota(jnp.int32, (AUG_ROWS, tm), 0)
        const = jnp.where(ri == AUG_ROWS - 1, 0.0, 1.0)
        ri_v = lax.broadcasted_iota(jnp.int32, (VT_ROWS - HEAD_DIM, tm), 0)
        v_feat = jnp.where(ri_v == 0, 1.0, 0.0).astype(BF)
        for hd in range(FOX_HEADS):
            rows = slice(hd * HEAD_DIM, (hd + 1) * HEAD_DIM)
            cum2 = cum_ref[hd:hd + 1, :] * LOG2E
            hi, mid, lo = (jnp.broadcast_to(part, (AUG_ROWS, tm)) for part in _split3(cum2))
            q_feat = jnp.where(ri == 1, hi, jnp.where(ri == 2, mid, jnp.where(ri == 3, lo, const)))
            k_feat = jnp.where(ri == 4, -hi, jnp.where(ri == 5, -mid, jnp.where(ri == 6, -lo, const)))
            q_aug = jnp.concatenate([q_t[rows], q_feat, zeros], axis=0)
            k_aug = jnp.concatenate([k_t[rows], k_feat, zeros], axis=0)
            mo_ref[hd:hd + 1, :] = jnp.sum(q_t[rows] * k_t[rows], axis=0, keepdims=True) + 1.0
            qat_ref[hd] = q_aug.astype(BF)
            kat_ref[hd] = k_aug.astype(BF)
            ka_ref[hd] = k_aug.T.astype(BF)
            vt_ref[hd] = jnp.concatenate([v_t[rows].astype(BF), v_feat], axis=0)
            v_ref[hd] = va[:, rows].astype(BF)
        za_ref[...] = sec(r_za, FOX_W)
        zb_ref[...] = sec(r_zb, SWA_W)
        cos2, sin2 = cos_ref[...], sin_ref[...]
        cos8 = jnp.concatenate([cos2] * 4, axis=1)
        sin8 = jnp.concatenate([sin2] * 4, axis=1)
        qb = sec(r_qb, SWA_W)
        qb = (qb * cos8 + _rope_partner(qb) * sin8) * (HEAD_DIM ** -0.5)
        qb_ref[...] = qb.astype(BF)
        for a in range(SWA_W // LANES):
            qbt_ref[a * LANES:(a + 1) * LANES, :] = qb[:, a * LANES:(a + 1) * LANES].T.astype(BF)
        kb = sec(r_kb, SWA_KV_W)
        kb = kb * cos2 + _rope_partner(kb) * sin2
        vb = sec(r_vb, SWA_KV_W)
        kb_t, vb_t = kb.T, vb.T
        for hd in range(SWA_KV_HEADS):
            sl = slice(hd * HEAD_DIM, (hd + 1) * HEAD_DIM)
            kb_ref[hd] = kb[:, sl].astype(BF)
            vb_ref[hd] = vb[:, sl].astype(BF)
            kbt_ref[hd] = kb_t[sl].astype(BF)
            vbt_ref[hd] = jnp.concatenate([vb_t[sl].astype(BF), v_feat], axis=0)

    row = lambda w: pl.BlockSpec((tm, w), lambda i: (i, 0))
    heads = lambda n, w=HEAD_DIM: pl.BlockSpec((n, tm, w), lambda i: (0, i, 0))
    heads_t = lambda w: pl.BlockSpec((FOX_HEADS, w, tm), lambda i: (0, 0, i))
    vec = _full((1, D_MODEL))
    hs = lambda a, b: jax.ShapeDtypeStruct((FOX_HEADS, a, b), BF)
    out_shape = (
        jax.ShapeDtypeStruct((s, D_MODEL), BF),
        hs(AUG_DIM, s), hs(s, AUG_DIM), hs(AUG_DIM, s), hs(s, HEAD_DIM), hs(VT_ROWS, s),
        jax.ShapeDtypeStruct((s, FOX_W), F32), jax.ShapeDtypeStruct((s, SWA_W), F32),
        jax.ShapeDtypeStruct((s, SWA_W), BF),
        jax.ShapeDtypeStruct((SWA_KV_HEADS, s, HEAD_DIM), BF), jax.ShapeDtypeStruct((SWA_KV_HEADS, s, HEAD_DIM), BF),
        jax.ShapeDtypeStruct((SWA_W, s), BF),
        jax.ShapeDtypeStruct((SWA_KV_HEADS, HEAD_DIM, s), BF), jax.ShapeDtypeStruct((SWA_KV_HEADS, VT_ROWS, s), BF),
        jax.ShapeDtypeStruct((FOX_HEADS, s), F32),
    )
    kv_t = lambda w: pl.BlockSpec((SWA_KV_HEADS, w, tm), lambda i: (0, 0, i))
    return _call(
        body, name="in_proj", out_shape=out_shape, grid=(s // tm,),
        in_specs=[row(D_MODEL), vec, vec, vec, _full(w_rows.shape), _full(w_t_fox.shape),
                  pl.BlockSpec((FOX_HEADS, tm), lambda i: (0, i)), row(LANES), row(LANES)],
        out_specs=(row(D_MODEL), heads_t(AUG_DIM), heads(FOX_HEADS, AUG_DIM), heads_t(AUG_DIM), heads(FOX_HEADS),
                   heads_t(VT_ROWS), row(FOX_W), row(SWA_W), row(SWA_W), heads(SWA_KV_HEADS), heads(SWA_KV_HEADS),
                   pl.BlockSpec((SWA_W, tm), lambda i: (0, i)), kv_t(HEAD_DIM), kv_t(VT_ROWS),
                   pl.BlockSpec((FOX_HEADS, tm), lambda i: (0, i))),
        compiler_params=_params(("parallel",)),
    )(x, g_pre, scale1p, shift, w_rows, w_t_fox, cum, cos_t, sin_t)


def _diag_chunks(d, bq, bk, chunk):
    out = []
    for c0 in range(0, bq, chunk):
        if d is None or d * bk + bk - 1 <= c0:
            out.append((c0, None))
        elif d * bk <= c0 + chunk - 1:
            kpos = d * bk + lax.broadcasted_iota(jnp.int32, (bk, chunk), 0)
            qpos = c0 + lax.broadcasted_iota(jnp.int32, (bk, chunk), 1)
            out.append((c0, kpos <= qpos))
    return out


def _fox_fwd(qat, ka, vt, m_own, bq, bk, chunk, running_max):
    nh, _, s = qat.shape
    r = bq // bk

    def body(ka_ref, qat_ref, vt_ref, mo_ref, o_ref, lse_ref, bad_ref, m_scr, acc_scr):
        i, j = pl.program_id(1), pl.program_id(2)

        @pl.when(j == 0)
        def _():
            m_scr[...] = jnp.full(m_scr.shape, NEG, F32) if running_max else mo_ref[0]
            acc_scr[...] = jnp.zeros(acc_scr.shape, F32)

        def careful(d):
            kv, vtv = ka_ref[0], vt_ref[0]

            def one_chunk(n, carry):
                c0 = pl.multiple_of(n * chunk, chunk)
                cs = pl.ds(c0, chunk)
                sc = _dot(kv, qat_ref[0, :, cs])
                if d is not None:
                    kpos = d * bk + lax.broadcasted_iota(jnp.int32, (bk, chunk), 0)
                    qpos = c0 + lax.broadcasted_iota(jnp.int32, (bk, chunk), 1)
                    sc = jnp.where(kpos <= qpos, sc, NEG)
                m_prev = m_scr[:, cs]
                m_new = jnp.maximum(m_prev, jnp.max(sc, axis=0, keepdims=True))
                p = jnp.exp2(sc - m_new).astype(BF)
                acc_scr[:, cs] = jnp.exp2(m_prev - m_new) * acc_scr[:, cs] + _dot(vtv, p)
                m_scr[:, cs] = m_new
                return carry

            lax.fori_loop(0, bq // chunk, one_chunk, 0)

        def fast(d):
            kv, vtv = ka_ref[0], vt_ref[0]
            todo = _diag_chunks(d, bq, bk, chunk)
            scores = lambda c0: _dot(kv, qat_ref[0, :, c0:c0 + chunk])
            sc_next = scores(todo[0][0])
            for n, (c0, mask) in enumerate(todo):
                cs = slice(c0, c0 + chunk)
                sc = sc_next
                if n + 1 < len(todo):
                    sc_next = scores(todo[n + 1][0])
                if mask is not None:
                    sc = jnp.where(mask, sc, NEG)
                p = jnp.exp2(sc - m_scr[:, cs]).astype(BF)
                acc_scr[:, cs] += _dot(vtv, p)

        step = careful if running_max else fast

        @pl.when(j < i * r)
        def _():
            step(None)

        for d in range(r):
            @pl.when(j == i * r + d)
            def _(d=d):
                step(d)

        @pl.when(j == i * r + r - 1)
        def _():
            l = acc_scr[HEAD_DIM:HEAD_DIM + 1, :]
            o_ref[0] = acc_scr[:HEAD_DIM, :] / l
            lse_ref[0] = m_scr[...] + jnp.log2(l)
            bad_ref[0] = jnp.where(l < OVERFLOW_GUARD, 0.0, 1.0)

    kmap = lambda h, i, j: (h, jnp.minimum(j, i * r + r - 1), 0)
    kmap_t = lambda h, i, j: (h, 0, jnp.minimum(j, i * r + r - 1))
    qrow = pl.BlockSpec((1, 1, bq), lambda h, i, j: (h, 0, i))
    row_shape = jax.ShapeDtypeStruct((nh, 1, s), F32)
    return _call(
        body, name="fox_fwd_running_max" if running_max else "fox_fwd",
        out_shape=(jax.ShapeDtypeStruct((nh, HEAD_DIM, s), F32), row_shape, row_shape),
        grid=(nh, s // bq, s // bk),
        in_specs=[pl.BlockSpec((1, bk, AUG_DIM), kmap), pl.BlockSpec((1, AUG_DIM, bq), lambda h, i, j: (h, 0, i)),
                  pl.BlockSpec((1, VT_ROWS, bk), kmap_t), qrow],
        out_specs=(pl.BlockSpec((1, HEAD_DIM, bq), lambda h, i, j: (h, 0, i)), qrow, qrow),
        scratch_shapes=[pltpu.VMEM((1, bq), F32), pltpu.VMEM((VT_ROWS, bq), F32)],
        compiler_params=_params(("parallel", "parallel", "arbitrary")),
    )(ka, qat, vt, m_own)


def _fox_bwd(qat, ka, kat, v, dot_, lse, delta, bq, bk, chunk, dq_blk):
    nh, _, s = qat.shape
    r = bq // bk
    nq = s // bq

    def body(ka_ref, kat_ref, v_ref, qat_ref, do_ref, lse_ref, dl_ref, dq_ref, dk_ref, dv_ref, dk_scr, dv_scr):
        j, i = pl.program_id(1), pl.program_id(2)

        @pl.when((j == 0) & (i == 0))
        def _():
            dq_ref[...] = jnp.zeros(dq_ref.shape, F32)

        @pl.when(i == 0)
        def _():
            dk_scr[...] = jnp.zeros(dk_scr.shape, F32)
            dv_scr[...] = jnp.zeros(dv_scr.shape, F32)

        def step(d):
            kv, ktv, vv = ka_ref[0], kat_ref[0], v_ref[0]
            todo = _diag_chunks(d, bq, bk, chunk)

            def products(c0):
                cs = slice(c0, c0 + chunk)
                return _dot(kv, qat_ref[0, :, cs]), _dot(vv, do_ref[0, :, cs])

            nxt = products(todo[0][0])
            for n, (c0, mask) in enumerate(todo):
                cs = slice(c0, c0 + chunk)
                sc, dp = nxt
                if n + 1 < len(todo):
                    nxt = products(todo[n + 1][0])
                p = jnp.exp2(sc - lse_ref[0, :, cs])
                if mask is not None:
                    p = jnp.where(mask, p, 0.0)
                ds = (p * (dp - dl_ref[0, :, cs])).astype(BF)
                dv_scr[...] += _dot_nt(do_ref[0, :, cs], p.astype(BF))
                dk_scr[:VT_ROWS, :] += _dot_nt(qat_ref[0, :VT_ROWS, cs], ds)
                c1 = c0 % dq_blk
                dq_ref[0, i * (bq // dq_blk) + c0 // dq_blk, :VT_ROWS, c1:c1 + chunk] += _dot(ktv[:VT_ROWS], ds)

        @pl.when(i * r > j)
        def _():
            step(None)

        for d in range(r):
            @pl.when(j == i * r + d)
            def _(d=d):
                step(d)

        @pl.when(i == nq - 1)
        def _():
            dk_ref[0] = dk_scr[...]
            dv_ref[0] = dv_scr[...]

    qmap = lambda h, j, i: (h, 0, jnp.maximum(i, j // r))
    kmap_t = lambda h, j, i: (h, 0, j)
    return _call(
        body, name="fox_bwd",
        out_shape=(jax.ShapeDtypeStruct((nh, s // dq_blk, AUG_DIM, dq_blk), F32),
                   jax.ShapeDtypeStruct((nh, AUG_DIM, s), F32), jax.ShapeDtypeStruct((nh, HEAD_DIM, s), F32)),
        grid=(nh, s // bk, nq),
        in_specs=[pl.BlockSpec((1, bk, AUG_DIM), lambda h, j, i: (h, j, 0)), pl.BlockSpec((1, AUG_DIM, bk), kmap_t),
                  pl.BlockSpec((1, bk, HEAD_DIM), lambda h, j, i: (h, j, 0)),
                  pl.BlockSpec((1, AUG_DIM, bq), qmap), pl.BlockSpec((1, HEAD_DIM, bq), qmap),
                  pl.BlockSpec((1, 1, bq), qmap), pl.BlockSpec((1, 1, bq), qmap)],
        out_specs=(pl.BlockSpec((1, s // dq_blk, AUG_DIM, dq_blk), lambda h, j, i: (h, 0, 0, 0)),
                   pl.BlockSpec((1, AUG_DIM, bk), kmap_t), pl.BlockSpec((1, HEAD_DIM, bk), kmap_t)),
        scratch_shapes=[pltpu.VMEM((AUG_DIM, bk), F32), pltpu.VMEM((HEAD_DIM, bk), F32)],
        compiler_params=_params(("parallel", "arbitrary", "arbitrary")),
    )(ka, kat, v, qat, dot_, lse, delta)


def _swa_mask(i, tq):
    kpos = i * tq - WINDOW + lax.broadcasted_iota(jnp.int32, (tq + WINDOW, tq), 0)
    qpos = i * tq + lax.broadcasted_iota(jnp.int32, (tq + WINDOW, tq), 1)
    rel = qpos - kpos
    return (rel >= 0) & (rel < WINDOW) & (kpos >= 0)


def _swa_rows(ref, i, tq):
    before = pl.multiple_of(jnp.maximum(i * tq - WINDOW, 0), WINDOW)
    return jnp.concatenate([ref[0, pl.ds(before, WINDOW), :], ref[0, pl.ds(pl.multiple_of(i * tq, tq), tq), :]], axis=0)


def _swa_before(n_rows, tq):
    return pl.BlockSpec((1, n_rows, WINDOW), lambda g, i: (g, 0, jnp.maximum(i * (tq // WINDOW) - 1, 0)))


def _swa_probs_t(kw, q_t, mask, sink):
    sc = jnp.where(mask, _dot(kw, q_t), NEG)
    m = jnp.maximum(jnp.max(sc, axis=0, keepdims=True), sink)
    p = jnp.exp(sc - m)
    e_sink = jnp.exp(sink - m)
    inv_l = 1.0 / (jnp.sum(p, axis=0, keepdims=True) + e_sink)
    return p * inv_l, e_sink * inv_l


def _swa_fwd(qbt, kb, vbt, sinks, tq):
    s = qbt.shape[1]
    gw = SWA_GROUP * HEAD_DIM

    def body(q_ref, k_ref, vb_ref, vc_ref, s_ref, o_ref):
        i = pl.program_id(1)
        mask = _swa_mask(i, tq)
        kw = _swa_rows(k_ref, i, tq)
        vtw = jnp.concatenate([vb_ref[0], vc_ref[0]], axis=1)
        sk = s_ref[0]
        for hh in range(SWA_GROUP):
            rows = slice(hh * HEAD_DIM, (hh + 1) * HEAD_DIM)
            sink = sk[:, hh:hh + 1]
            sc = jnp.where(mask, _dot(kw, q_ref[rows, :]), NEG)
            m = jnp.maximum(jnp.max(sc, axis=0, keepdims=True), sink)
            acc = _dot(vtw, jnp.exp(sc - m).astype(BF))
            o_ref[rows, :] = acc[:HEAD_DIM] / (acc[HEAD_DIM:HEAD_DIM + 1] + jnp.exp(sink - m))

    kvspec = pl.BlockSpec((1, s, HEAD_DIM), lambda g, i: (g, 0, 0))
    qspec = pl.BlockSpec((gw, tq), lambda g, i: (g, i))
    return _call(
        body, name="swa_fwd", out_shape=jax.ShapeDtypeStruct((SWA_W, s), F32), grid=(SWA_KV_HEADS, s // tq),
        in_specs=[qspec, kvspec, _swa_before(VT_ROWS, tq), pl.BlockSpec((1, VT_ROWS, tq), lambda g, i: (g, 0, i)),
                  pl.BlockSpec((1, 1, SWA_GROUP), lambda g, i: (g, 0, 0))],
        out_specs=qspec, compiler_params=_params(("parallel", "parallel")),
    )(qbt, kb, vbt, vbt, sinks)


def _swa_bwd(qb, qbt, kb, kbt, vb, sinks, dob, dobt, tq):
    s = qb.shape[0]
    gw = SWA_GROUP * HEAD_DIM

    def body(q_ref, qt_ref, k_ref, ktb_ref, ktc_ref, v_ref, s_ref, do_ref, dot_ref, dq_ref, dk_ref, dv_ref, ds_ref):
        i = pl.program_id(1)

        @pl.when(i == 0)
        def _():
            dk_ref[...] = jnp.zeros(dk_ref.shape, F32)
            dv_ref[...] = jnp.zeros(dv_ref.shape, F32)
            ds_ref[...] = jnp.zeros(ds_ref.shape, F32)

        mask = _swa_mask(i, tq)
        kw = _swa_rows(k_ref, i, tq)
        vw = _swa_rows(v_ref, i, tq)
        ktw = jnp.concatenate([ktb_ref[0], ktc_ref[0]], axis=1)
        qv, dov = q_ref[...], do_ref[...]
        sk = s_ref[0]
        dsinks = []
        dk_acc = jnp.zeros((tq + WINDOW, HEAD_DIM), F32)
        dv_acc = jnp.zeros((tq + WINDOW, HEAD_DIM), F32)
        for hh in range(SWA_GROUP):
            rows = slice(hh * HEAD_DIM, (hh + 1) * HEAD_DIM)
            p, p_sink = _swa_probs_t(kw, qt_ref[rows, :], mask, sk[:, hh:hh + 1])
            dp = _dot(vw, dot_ref[rows, :])
            delta = jnp.sum(p * dp, axis=0, keepdims=True)
            dsc = (p * (dp - delta)).astype(BF)
            dq_ref[rows, :] = _dot(ktw, dsc)
            dk_acc = dk_acc + _dot(dsc, qv[:, rows])
            dv_acc = dv_acc + _dot(p.astype(BF), dov[:, rows])
            dsinks.append(-jnp.sum(p_sink * delta, axis=1, keepdims=True))
        before = pl.ds(pl.multiple_of(jnp.maximum(i * tq - WINDOW, 0), WINDOW), WINDOW)
        own = pl.ds(pl.multiple_of(i * tq, tq), tq)
        dk_ref[0, before, :] += dk_acc[:WINDOW]
        dk_ref[0, own, :] += dk_acc[WINDOW:]
        dv_ref[0, before, :] += dv_acc[:WINDOW]
        dv_ref[0, own, :] += dv_acc[WINDOW:]
        ds_ref[0] += jnp.concatenate(dsinks, axis=1)

    kvspec = pl.BlockSpec((1, s, HEAD_DIM), lambda g, i: (g, 0, 0))
    qspec = pl.BlockSpec((tq, gw), lambda g, i: (i, g))
    qspec_t = pl.BlockSpec((gw, tq), lambda g, i: (g, i))
    sspec = pl.BlockSpec((1, 1, SWA_GROUP), lambda g, i: (g, 0, 0))
    kvshape = jax.ShapeDtypeStruct((SWA_KV_HEADS, s, HEAD_DIM), F32)
    return _call(
        body, name="swa_bwd",
        out_shape=(jax.ShapeDtypeStruct((SWA_W, s), F32), kvshape, kvshape,
                   jax.ShapeDtypeStruct((SWA_KV_HEADS, 1, SWA_GROUP), F32)),
        grid=(SWA_KV_HEADS, s // tq),
        in_specs=[qspec, qspec_t, kvspec, _swa_before(HEAD_DIM, tq),
                  pl.BlockSpec((1, HEAD_DIM, tq), lambda g, i: (g, 0, i)), kvspec, sspec, qspec, qspec_t],
        out_specs=(qspec_t, kvspec, kvspec, sspec),
        compiler_params=_params(("parallel", "arbitrary")),
    )(qb, qbt, kb, kbt, kbt, vb, sinks, dob, dobt)


def _pairs_to_rows(ref, n_rows=HEAD_DIM):
    parts = []
    for a in range(0, FOX_HEADS, 2):
        parts.append(jnp.concatenate([ref[a][:n_rows], ref[a + 1][:n_rows]], axis=0).T)
    return jnp.concatenate(parts, axis=1)


def _blocks_to_rows(ref):
    return jnp.concatenate([ref[a:a + LANES, :].T for a in range(0, ref.shape[0], LANES)], axis=1)


def _out_proj(oat, za, obt, zb, x, tgt, w_out, w_out_t, gate, g_post, tm):
    s = x.shape[0]

    def body(oat_ref, za_ref, obt_ref, zb_ref, x_ref, t_ref, w_ref, wt_ref, gate_ref, gp_ref,
             dout_ref, doat_ref, dla_ref, dza_ref, dob_ref, dobt_ref, dzb_ref, gw_ref, dgate_ref, dgp_ref, loss_ref):
        i = pl.program_id(0)

        @pl.when(i == 0)
        def _():
            gw_ref[...] = jnp.zeros(gw_ref.shape, F32)
            dgate_ref[...] = jnp.zeros(dgate_ref.shape, F32)
            dgp_ref[...] = jnp.zeros(dgp_ref.shape, F32)
            loss_ref[...] = jnp.zeros(loss_ref.shape, F32)

        oa_v = _pairs_to_rows(oat_ref)
        ob_v = _blocks_to_rows(obt_ref)
        za_v, zb_v = za_ref[...], zb_ref[...]
        sga, sgb = _sigmoid(za_v), _sigmoid(zb_v)
        sila, silb = za_v * sga, zb_v * sgb
        u = jnp.concatenate([oa_v * sila, ob_v * silb], axis=1).astype(BF)
        yv = _dot(u, w_ref[...])
        yhat, rstd = _rms_hat(yv)
        gp, gate_v = gp_ref[...], gate_ref[...]
        nrm = yhat * gp
        diff = (x_ref[...] + gate_v * nrm) - t_ref[...]
        loss_ref[...] += 0.5 * jnp.sum(jnp.sum(diff * diff, axis=1, keepdims=True), axis=0, keepdims=True) / D_MODEL
        dout = diff * (1.0 / D_MODEL)
        dout_ref[...] = dout
        dgate_ref[...] += jnp.sum(dout * nrm, axis=0, keepdims=True)
        dn = dout * gate_v
        dgp_ref[...] += jnp.sum(dn * yhat, axis=0, keepdims=True)
        dyhat = dn * gp
        dy = (rstd * (dyhat - yhat * jnp.mean(dyhat * yhat, axis=1, keepdims=True))).astype(BF)
        gw_ref[...] += _dot_tn(u, dy)
        du = _dot(dy, wt_ref[...])
        dua, dub = du[:, :FOX_W], du[:, FOX_W:]
        doa = dua * sila
        for a in range(0, FOX_HEADS, 2):
            pair_t = doa[:, a * HEAD_DIM:(a + 2) * HEAD_DIM].T
            for hd, rows in ((a, slice(0, HEAD_DIM)), (a + 1, slice(HEAD_DIM, 2 * HEAD_DIM))):
                doat_ref[hd] = pair_t[rows].astype(BF)
                dla_ref[hd] = jnp.sum(pair_t[rows] * oat_ref[hd], axis=0, keepdims=True)
        dob = dub * silb
        dob_ref[...] = dob.astype(BF)
        for a in range(0, SWA_W, LANES):
            dobt_ref[a:a + LANES, :] = dob[:, a:a + LANES].T.astype(BF)
        dza_ref[...] = (dua * oa_v * (sga * (1.0 + za_v * (1.0 - sga)))).astype(BF)
        dzb_ref[...] = (dub * ob_v * (sgb * (1.0 + zb_v * (1.0 - sgb)))).astype(BF)

    row = lambda w: pl.BlockSpec((tm, w), lambda i: (i, 0))
    heads_t = lambda w: pl.BlockSpec((FOX_HEADS, w, tm), lambda i: (0, 0, i))
    vec = _full((1, D_MODEL))
    mat = _full((D_MODEL, D_MODEL))
    out_shape = (
        jax.ShapeDtypeStruct((s, D_MODEL), F32),
        jax.ShapeDtypeStruct((FOX_HEADS, HEAD_DIM, s), BF), jax.ShapeDtypeStruct((FOX_HEADS, 1, s), F32),
        jax.ShapeDtypeStruct((s, FOX_W), BF), jax.ShapeDtypeStruct((s, SWA_W), BF), jax.ShapeDtypeStruct((SWA_W, s), BF),
        jax.ShapeDtypeStruct((s, SWA_W), BF),
        jax.ShapeDtypeStruct((D_MODEL, D_MODEL), F32),
        jax.ShapeDtypeStruct((1, D_MODEL), F32), jax.ShapeDtypeStruct((1, D_MODEL), F32),
        jax.ShapeDtypeStruct((1, 1), F32),
    )
    col = pl.BlockSpec((SWA_W, tm), lambda i: (0, i))
    return _call(
        body, name="out_proj", out_shape=out_shape, grid=(s // tm,),
        in_specs=[heads_t(HEAD_DIM), row(FOX_W), col, row(SWA_W), row(D_MODEL), row(D_MODEL), mat, mat, vec, vec],
        out_specs=(row(D_MODEL), heads_t(HEAD_DIM), heads_t(1), row(FOX_W), row(SWA_W), col, row(SWA_W), mat, vec, vec,
                   _full((1, 1))),
        compiler_params=_params(("arbitrary",)),
    )(oat, za, obt, zb, x, tgt, w_out, w_out_t, gate, g_post)


def _assemble_dproj(dqt, dkt, dvt, dza, dqb, dzb, dkb, dvb, df, cos_t, sin_t, tm):
    s = dza.shape[0]

    def body(dqt_ref, dkt_ref, dvt_ref, dza_ref, dqb_ref, dzb_ref, dkb_ref, dvb_ref, df_ref, cos_ref, sin_ref, o_ref):
        def cat(ref, n):
            return jnp.concatenate([ref[hd] for hd in range(n)], axis=1)

        cos2, sin2 = cos_ref[...], sin_ref[...]
        cos8 = jnp.concatenate([cos2] * 4, axis=1)
        sin8 = jnp.concatenate([sin2] * 4, axis=1)
        scale = HEAD_DIM ** -0.5
        o_ref[:, C_QA:C_QA + FOX_W] = (_pairs_to_rows(dqt_ref.at[:, 0]) * scale).astype(BF)
        o_ref[:, C_KA:C_KA + FOX_W] = (_pairs_to_rows(dkt_ref) * LN2).astype(BF)
        o_ref[:, C_VA:C_VA + FOX_W] = _pairs_to_rows(dvt_ref).astype(BF)
        o_ref[:, C_ZA:C_ZA + FOX_W] = dza_ref[...]
        dq = _blocks_to_rows(dqb_ref) * scale
        o_ref[:, C_QB:C_QB + SWA_W] = (dq * cos8 - _rope_partner(dq) * sin8).astype(BF)
        o_ref[:, C_ZB:C_ZB + SWA_W] = dzb_ref[...]
        dk = cat(dkb_ref, SWA_KV_HEADS)
        o_ref[:, C_KB:C_KB + SWA_KV_W] = (dk * cos2 - _rope_partner(dk) * sin2).astype(BF)
        o_ref[:, C_VB:C_VB + SWA_KV_W] = cat(dvb_ref, SWA_KV_HEADS).astype(BF)
        o_ref[:, C_F:C_F + LANES] = df_ref[...].astype(BF)

    row = lambda w: pl.BlockSpec((tm, w), lambda i: (i, 0))
    heads = lambda n: pl.BlockSpec((n, tm, HEAD_DIM), lambda i: (0, i, 0))
    heads_t = lambda w: pl.BlockSpec((FOX_HEADS, w, tm), lambda i: (0, 0, i))
    return _call(
        body, name="assemble_dproj", out_shape=jax.ShapeDtypeStruct((s, WP), BF), grid=(s // tm,),
        in_specs=[pl.BlockSpec((FOX_HEADS, 1, AUG_DIM, tm), lambda i: (0, i, 0, 0)), heads_t(AUG_DIM), heads_t(HEAD_DIM),
                  row(FOX_W), pl.BlockSpec((SWA_W, tm), lambda i: (0, i)), row(SWA_W), heads(SWA_KV_HEADS),
                  heads(SWA_KV_HEADS), row(LANES), row(LANES), row(LANES)],
        out_specs=row(WP), compiler_params=_params(("parallel",)),
    )(dqt, dkt, dvt, dza, dqb, dzb, dkb, dvb, df, cos_t, sin_t)


def _in_proj_bwd_x(dproj, w_al_t, x, dout, g_pre, scale1p, tm):
    s = x.shape[0]

    def body(dp_ref, wt_ref, x_ref, dout_ref, g_ref, sc_ref, gx_ref, dsh_ref, dsc_ref, dg_ref):
        i = pl.program_id(0)

        @pl.when(i == 0)
        def _():
            dsh_ref[...] = jnp.zeros(dsh_ref.shape, F32)
            dsc_ref[...] = jnp.zeros(dsc_ref.shape, F32)
            dg_ref[...] = jnp.zeros(dg_ref.shape, F32)

        dh = _dot(dp_ref[...], wt_ref[...])
        xhat, rstd = _rms_hat(x_ref[...])
        g, sc = g_ref[...], sc_ref[...]
        dsh_ref[...] += jnp.sum(dh, axis=0, keepdims=True)
        dhx = dh * xhat
        dsc_ref[...] += jnp.sum(dhx * g, axis=0, keepdims=True)
        dg_ref[...] += jnp.sum(dhx * sc, axis=0, keepdims=True)
        dxhat = dh * (g * sc)
        gx_ref[...] = dout_ref[...] + rstd * (dxhat - xhat * jnp.mean(dxhat * xhat, axis=1, keepdims=True))

    row = lambda w: pl.BlockSpec((tm, w), lambda i: (i, 0))
    vec = _full((1, D_MODEL))
    vshape = jax.ShapeDtypeStruct((1, D_MODEL), F32)
    return _call(
        body, name="in_proj_bwd_x", out_shape=(jax.ShapeDtypeStruct((s, D_MODEL), F32), vshape, vshape, vshape),
        grid=(s // tm,),
        in_specs=[row(WP), _full((WP, D_MODEL)), row(D_MODEL), row(D_MODEL), vec, vec],
        out_specs=(row(D_MODEL), vec, vec, vec), compiler_params=_params(("arbitrary",)),
    )(dproj, w_al_t, x, dout, g_pre, scale1p)


def _in_proj_bwd_w(h, dproj, tk, tn):
    s = h.shape[0]

    def body(h_ref, dp_ref, gw_ref):
        @pl.when(pl.program_id(1) == 0)
        def _():
            gw_ref[...] = jnp.zeros(gw_ref.shape, F32)

        gw_ref[...] += _dot_tn(h_ref[...], dp_ref[...])

    return _call(
        body, name="in_proj_bwd_w", out_shape=jax.ShapeDtypeStruct((D_MODEL, WP), F32), grid=(WP // tn, s // tk),
        in_specs=[pl.BlockSpec((tk, D_MODEL), lambda n, k: (k, 0)), pl.BlockSpec((tk, tn), lambda n, k: (k, n))],
        out_specs=pl.BlockSpec((D_MODEL, tn), lambda n, k: (0, n)),
        compiler_params=_params(("parallel", "arbitrary")),
    )(h, dproj)


def _align_w_in(w_cols):
    def part(name, width):
        return w_cols[:, _SRC[name]:_SRC[name] + width]

    fpad = jnp.pad(part("fa", FOX_HEADS), ((0, 0), (0, LANES - FOX_HEADS)))
    return jnp.concatenate([part("qa", FOX_W), part("ka", FOX_W), part("va", FOX_W), part("za", FOX_W),
                            part("qb", SWA_W), part("zb", SWA_W), part("kb", SWA_KV_W), part("vb", SWA_KV_W), fpad], axis=1)


def _unalign_w_in(g_al):
    def part(c0, width):
        return g_al[:, c0:c0 + width]

    return jnp.concatenate([part(C_QA, FOX_W), part(C_KA, FOX_W), part(C_VA, FOX_W), part(C_F, FOX_HEADS),
                            part(C_ZA, FOX_W), part(C_QB, SWA_W), part(C_KB, SWA_KV_W), part(C_VB, SWA_KV_W),
                            part(C_ZB, SWA_W)], axis=1)


def _rope_tables(positions):
    inv_freq = ROPE_THETA ** (-jnp.arange(HALF, dtype=F32) / HALF)
    ang = positions.astype(F32)[:, None] * inv_freq
    cos, sin = jnp.cos(ang), jnp.sin(ang)
    return jnp.concatenate([cos, cos, cos, cos], axis=1), jnp.concatenate([-sin, sin, -sin, sin], axis=1)


def _tiles(s):
    if s >= 4096:
        return dict(tm=512, blk=512, bq=2048, bk=512, chunk=256, tq=256, tm_out=256, tk=512, tn=1152)
    return dict(tm=128, blk=128, bq=256, bk=128, chunk=128, tq=128, tm_out=128, tk=128, tn=1152)


def kernel(x, c, positions, w_ada, b_ada, g_pre, w_in, b_fgate, sinks, w_out, g_post, loss_target, m_w_ada, m_b_ada, m_g_pre, m_w_in, m_b_fgate, m_sinks, m_w_out, m_g_post, v_w_ada, v_b_ada, v_g_pre, v_w_in, v_b_fgate, v_sinks, v_w_out, v_g_post):
    s = x.shape[1]
    t = _tiles(s)
    nc = s // LANES
    rows = FOX_HEADS * nc
    me = 4 * lax.axis_index("x") + 2 * lax.axis_index("y") + lax.axis_index("c")
    chip = 2 * lax.axis_index("x") + lax.axis_index("y")
    core = lax.axis_index("c")
    x2, tgt = x[0], loss_target[0]

    c_all = _allgather_devices(c, "gather_c")[:, 0, :]
    a_all, mod_shard = _ada_shard(c_all, w_ada[0])
    mod_all = _allgather_devices(mod_shard, "gather_mod")
    mod_rows = lax.dynamic_index_in_dim(mod_all, me, axis=1, keepdims=False)
    mod = mod_rows.reshape(N_CHIPS, 2, W_ADA_SHARD)[:, 0, :].reshape(1, 3 * D_MODEL) + b_ada
    shift, scale1p, gate = mod[:, :D_MODEL], 1.0 + mod[:, D_MODEL:2 * D_MODEL], mod[:, 2 * D_MODEL:]

    w_in_pad = jnp.pad(w_in[0].astype(BF), ((0, 0), (0, W_IN_SHARD_PAD - W_IN_SHARD)))
    w_in_all = _allgather_chips(w_in_pad, "gather_w_in")
    w_cols = jnp.concatenate([w_in_all[k, :, :W_IN_SHARD] for k in range(N_CHIPS)], axis=1)
    w_al = _align_w_in(w_cols)
    w_al_t = w_al.T
    w_out_all = _allgather_chips(w_out[0].astype(BF), "gather_w_out").reshape(D_MODEL, D_MODEL)
    w_out_t = w_out_all.T

    cos_t, sin_t = _rope_tables(positions[0])

    f_pad = _forget_logits(x2, g_pre, scale1p, shift, w_al[:, C_F:], t["tm"])
    f_rows = f_pad[:, :FOX_HEADS].T.reshape(rows, LANES)
    bias_rows = jnp.repeat(b_fgate[0], nc)[:, None]
    cum = _log_forget_cumsum(f_rows, bias_rows, nc).reshape(FOX_HEADS, s)
    h, qat, ka, kat, va, vat, za, zb, qb, kb, vb, qbt, kbt, vbt, m_own = _in_proj(
        x2, g_pre, scale1p, shift, w_al[:, C_VA:C_F], w_al_t[:C_ZA], cum, cos_t, sin_t, t["tm"])
    m_own = m_own[:, None, :]
    fox_args = (qat, ka, vat, m_own, t["bq"], t["bk"], t["chunk"])
    oat, lse, bad = _fox_fwd(*fox_args, running_max=False)
    oat, lse = lax.cond(jnp.max(bad) > 0.0, lambda: _fox_fwd(*fox_args, running_max=True)[:2], lambda: (oat, lse))
    sinks_g = sinks.reshape(SWA_KV_HEADS, 1, SWA_GROUP)
    obt = _swa_fwd(qbt, kb, vbt, sinks_g, t["tq"])

    dout, doat, delta_a, dza, dob, dobt, dzb, gw_out, dgate, dg_post, loss_part = _out_proj(
        oat, za, obt, zb, x2, tgt, w_out_all, w_out_t, gate, g_post, t["tm_out"])

    dqt, dkt, dvt = _fox_bwd(qat, ka, kat, va, doat, lse, delta_a, t["bq"], t["bk"], t["chunk"], t["blk"])
    dcum = dqt[:, :, HEAD_DIM, :].reshape(FOX_HEADS, s) - dkt[:, HEAD_DIM, :]
    df_rows, db_heads = _log_forget_cumsum_bwd(dcum.reshape(rows, LANES), f_rows, bias_rows, nc)
    df_pad = jnp.pad(df_rows.reshape(FOX_HEADS, s).T, ((0, 0), (0, LANES - FOX_HEADS)))
    dqb, dkb, dvb, dsinks = _swa_bwd(qb, qbt, kb, kbt, vb, sinks_g, dob, dobt, t["tq"])

    dproj = _assemble_dproj(dqt, dkt, dvt, dza, dqb, dzb, dkb, dvb, df_pad, cos_t, sin_t, t["blk"])
    grad_x, dshift, dscale, dg_pre = _in_proj_bwd_x(dproj, w_al_t, x2, dout, g_pre, scale1p, t["tm_out"])
    gw_in = _unalign_w_in(_in_proj_bwd_w(h, dproj, t["tk"], t["tn"]))

    pad_lane = lambda vrow: jnp.pad(vrow, ((0, 0), (0, LANES - vrow.shape[1])))
    packed = jnp.concatenate([dshift, dscale, dgate, dg_pre, dg_post,
                              pad_lane(db_heads.reshape(1, FOX_HEADS)), pad_lane(dsinks.reshape(1, FOX_HEADS)),
                              pad_lane(loss_part)], axis=1)
    parts = _allgather_devices(packed, "gather_partials")
    tot = _sum_devices(parts)
    loss = tot[0, P_LOSS]
    g_b_ada = tot[:, P_DMOD:P_DMOD + 3 * D_MODEL]
    g_g_pre = tot[:, P_GPRE:P_GPRE + D_MODEL]
    g_g_post = tot[:, P_GPOST:P_GPOST + D_MODEL]
    g_b_fgate = tot[:, P_BF:P_BF + FOX_HEADS]
    g_sinks = tot[:, P_SINK:P_SINK + FOX_HEADS]
    dm_shard = lax.dynamic_slice_in_dim(parts[:, 0, :3 * D_MODEL], chip * W_ADA_SHARD, W_ADA_SHARD, axis=1)
    g_w_ada = _grad_w_ada(a_all.T, dm_shard)

    gin = jnp.pad(gw_in.reshape(D_MODEL, N_CHIPS, W_IN_SHARD).transpose(1, 0, 2),
                  ((0, 0), (0, 0), (0, W_IN_SHARD_PAD - W_IN_SHARD)))
    gout = gw_out.reshape(N_CHIPS, D_MODEL, W_OUT_SHARD)
    gbig = jnp.concatenate([gin, gout], axis=2)
    half = D_MODEL // 2
    gw = W_IN_SHARD_PAD + W_OUT_SHARD
    keep = lax.dynamic_slice_in_dim(gbig, core * half, half, axis=1)
    give = lax.dynamic_slice_in_dim(gbig, (1 - core) * half, half, axis=1)
    got = _swap_sibling(give.reshape(N_CHIPS * half, gw), "swap_grad_halves")
    pair = _add(keep.reshape(N_CHIPS * half, gw), got, "add_pair", BF).reshape(N_CHIPS, half, gw)
    from_chips = _exchange_chips(pair, "exchange_grad")
    mine = _sum_chips(from_chips, "sum_chips")
    other = _swap_sibling(mine, "swap_grad_result")
    lo = jnp.where(core == 0, mine, other)
    hi = jnp.where(core == 0, other, mine)
    gfull = jnp.concatenate([lo, hi], axis=0)
    g_w_in = gfull[:, :W_IN_SHARD]
    g_w_out = gfull[:, W_IN_SHARD_PAD:].reshape(W_OUT_SHARD, D_MODEL)

    grads = dict(w_ada=g_w_ada, b_ada=g_b_ada, g_pre=g_g_pre, w_in=g_w_in, b_fgate=g_b_fgate, sinks=g_sinks,
                 w_out=g_w_out, g_post=g_g_post)
    weights = dict(w_ada=w_ada, b_ada=b_ada, g_pre=g_pre, w_in=w_in, b_fgate=b_fgate, sinks=sinks, w_out=w_out, g_post=g_post)
    moms = dict(w_ada=m_w_ada, b_ada=m_b_ada, g_pre=m_g_pre, w_in=m_w_in, b_fgate=m_b_fgate, sinks=m_sinks, w_out=m_w_out, g_post=m_g_post)
    vars_ = dict(w_ada=v_w_ada, b_ada=v_b_ada, g_pre=v_g_pre, w_in=v_w_in, b_fgate=v_b_fgate, sinks=v_sinks, w_out=v_w_out, g_post=v_g_post)
    names = ["w_ada", "b_ada", "g_pre", "w_in", "b_fgate", "sinks", "w_out", "g_post"]
    g_out, d_out, m_out, v_out = [], [], [], []
    for n in names:
        g2 = grads[n].reshape(weights[n].shape[-2:])
        go, d, nm, nv = _adamw(weights[n], g2, moms[n], vars_[n], "adamw_" + n)
        g_out.append(go)
        d_out.append(d)
        m_out.append(nm)
        v_out.append(nv)
    return (loss, grad_x.reshape(x.shape), *g_out, *d_out, *m_out, *v_out)
```

```python
import functools

import jax
import jax.numpy as jnp
from jax import lax
from jax.experimental import pallas as pl
from jax.experimental.pallas import tpu as pltpu

_INTERPRET = False

D_MODEL = 1024
HEAD_DIM = 64
HALF = HEAD_DIM // 2
AUG_DIM = 128
AUG_ROWS = 8
VT_ROWS = 80
LOG2E = 1.4426950408889634
LN2 = 0.6931471805599453
Q_SCALE = LOG2E * 64 ** -0.5
FOX_HEADS = 8
FOX_W = 512
SWA_W = 512
SWA_KV_HEADS = 2
SWA_GROUP = 4
SWA_KV_W = 128
WINDOW = 128
ROPE_THETA = 10000.0
RMS_EPS = 1e-6
IN_WIDTH = 3336
N_CHIPS = 4
N_DEV = 8
W_IN_SHARD = IN_WIDTH // N_CHIPS
W_IN_SHARD_PAD = 896
W_ADA_SHARD = 3 * D_MODEL // N_CHIPS
W_OUT_SHARD = D_MODEL // N_CHIPS
LANES = 128

_SRC = dict(qa=0, ka=512, va=1024, fa=1536, za=1544, qb=2056, kb=2568, vb=2696, zb=2824)
C_QA, C_KA, C_VA, C_ZA, C_QB, C_ZB, C_KB, C_VB, C_F = 0, 512, 1024, 1536, 2048, 2560, 3072, 3200, 3328
WP = 3456

ADAM_LR = 0.001
ADAM_B1 = 0.9
ADAM_B2 = 0.999
ADAM_EPS = 1e-08
ADAM_WD = 0.01
ADAM_STEP = 10

VMEM_LIMIT = 56 * 1024 * 1024
NEG = -1e30
OVERFLOW_GUARD = 1e30
MESH = pl.DeviceIdType.MESH
BF = jnp.bfloat16
F32 = jnp.float32

P_DMOD, P_GPRE, P_GPOST, P_BF, P_SINK, P_LOSS, P_LEN = 0, 3072, 4096, 5120, 5248, 5376, 5504


def _call(body, **kw):
    return pl.pallas_call(body, interpret=_INTERPRET, **kw)


def _params(sem=None, **kw):
    return pltpu.CompilerParams(dimension_semantics=sem, vmem_limit_bytes=VMEM_LIMIT, **kw)


def _full(shape):
    zeros = (0,) * len(shape)
    return pl.BlockSpec(shape, lambda *_: zeros)


def _dot(a, b):
    return jnp.dot(a, b, preferred_element_type=F32)


def _dot_nt(a, b):
    return lax.dot_general(a, b, (((1,), (1,)), ((), ())), preferred_element_type=F32)


def _dot_tn(a, b):
    return lax.dot_general(a, b, (((0,), (0,)), ((), ())), preferred_element_type=F32)


def _sigmoid(z):
    return 1.0 / (1.0 + jnp.exp(-z))


def _rope_partner(t):
    w = t.shape[-1]
    lane = lax.broadcasted_iota(jnp.int32, t.shape, t.ndim - 1)
    return jnp.where((lane & (HEAD_DIM - 1)) < HALF, pltpu.roll(t, w - HALF, t.ndim - 1), pltpu.roll(t, HALF, t.ndim - 1))


def _allgather_devices(v, name):
    r, cdim = v.shape
    masks = [(dx, dy, dc) for dx in (0, 1) for dy in (0, 1) for dc in (0, 1)][1:]

    def body(v_ref, out_ref, send_sems, recv_sems):
        x, y, c = lax.axis_index("x"), lax.axis_index("y"), lax.axis_index("c")
        me = 4 * x + 2 * y + c
        out_ref[me] = v_ref[...]
        copies = []
        for k, (dx, dy, dc) in enumerate(masks):
            cp = pltpu.make_async_remote_copy(
                src_ref=v_ref, dst_ref=out_ref.at[me], send_sem=send_sems.at[k], recv_sem=recv_sems.at[k],
                device_id=(x ^ dx, y ^ dy, c ^ dc), device_id_type=MESH)
            cp.start()
            copies.append(cp)
        for k, (dx, dy, dc) in enumerate(masks):
            peer = 4 * (x ^ dx) + 2 * (y ^ dy) + (c ^ dc)
            pltpu.make_async_remote_copy(
                src_ref=v_ref, dst_ref=out_ref.at[peer], send_sem=send_sems.at[k], recv_sem=recv_sems.at[k],
                device_id=(x ^ dx, y ^ dy, c ^ dc), device_id_type=MESH).wait_recv()
        for cp in copies:
            cp.wait_send()

    return _call(
        body, name=name, out_shape=jax.ShapeDtypeStruct((N_DEV, r, cdim), v.dtype),
        in_specs=[pl.BlockSpec(memory_space=pltpu.VMEM)], out_specs=pl.BlockSpec(memory_space=pltpu.VMEM),
        scratch_shapes=[pltpu.SemaphoreType.DMA((7,)), pltpu.SemaphoreType.DMA((7,))],
        compiler_params=pltpu.CompilerParams(has_side_effects=True),
    )(v)


def _allgather_chips(v, name):
    r, cdim = v.shape
    masks = [(1, 0), (0, 1), (1, 1)]

    def body(v_ref, out_ref, send_sems, recv_sems, local_sem):
        x, y, c = lax.axis_index("x"), lax.axis_index("y"), lax.axis_index("c")
        me = 2 * x + y
        mine = pltpu.make_async_copy(v_ref, out_ref.at[me], local_sem)
        mine.start()
        copies = []
        for k, (dx, dy) in enumerate(masks):
            cp = pltpu.make_async_remote_copy(
                src_ref=v_ref, dst_ref=out_ref.at[me], send_sem=send_sems.at[k], recv_sem=recv_sems.at[k],
                device_id=(x ^ dx, y ^ dy, c), device_id_type=MESH)
            cp.start()
            copies.append(cp)
        for k, (dx, dy) in enumerate(masks):
            peer = 2 * (x ^ dx) + (y ^ dy)
            pltpu.make_async_remote_copy(
                src_ref=v_ref, dst_ref=out_ref.at[peer], send_sem=send_sems.at[k], recv_sem=recv_sems.at[k],
                device_id=(x ^ dx, y ^ dy, c), device_id_type=MESH).wait_recv()
        for cp in copies:
            cp.wait_send()
        mine.wait()

    return _call(
        body, name=name, out_shape=jax.ShapeDtypeStruct((N_CHIPS, r, cdim), v.dtype),
        in_specs=[pl.BlockSpec(memory_space=pl.ANY)], out_specs=pl.BlockSpec(memory_space=pl.ANY),
        scratch_shapes=[pltpu.SemaphoreType.DMA((3,)), pltpu.SemaphoreType.DMA((3,)), pltpu.SemaphoreType.DMA],
        compiler_params=pltpu.CompilerParams(has_side_effects=True),
    )(v)


def _swap_sibling(v, name):
    def body(v_ref, out_ref, send_sem, recv_sem):
        x, y, c = lax.axis_index("x"), lax.axis_index("y"), lax.axis_index("c")
        cp = pltpu.make_async_remote_copy(
            src_ref=v_ref, dst_ref=out_ref, send_sem=send_sem, recv_sem=recv_sem,
            device_id=(x, y, 1 - c), device_id_type=MESH)
        cp.start()
        cp.wait()

    return _call(
        body, name=name, out_shape=jax.ShapeDtypeStruct(v.shape, v.dtype),
        in_specs=[pl.BlockSpec(memory_space=pl.ANY)], out_specs=pl.BlockSpec(memory_space=pl.ANY),
        scratch_shapes=[pltpu.SemaphoreType.DMA, pltpu.SemaphoreType.DMA],
        compiler_params=pltpu.CompilerParams(has_side_effects=True),
    )(v)


def _exchange_chips(v, name):
    _, r, cdim = v.shape
    masks = [(1, 0), (0, 1), (1, 1)]

    def body(v_ref, out_ref, send_sems, recv_sems, local_sem):
        x, y, c = lax.axis_index("x"), lax.axis_index("y"), lax.axis_index("c")
        me = 2 * x + y
        mine = pltpu.make_async_copy(v_ref.at[me], out_ref.at[me], local_sem)
        mine.start()
        copies = []
        for k, (dx, dy) in enumerate(masks):
            peer = 2 * (x ^ dx) + (y ^ dy)
            cp = pltpu.make_async_remote_copy(
                src_ref=v_ref.at[peer], dst_ref=out_ref.at[me], send_sem=send_sems.at[k], recv_sem=recv_sems.at[k],
                device_id=(x ^ dx, y ^ dy, c), device_id_type=MESH)
            cp.start()
            copies.append(cp)
        for k, (dx, dy) in enumerate(masks):
            peer = 2 * (x ^ dx) + (y ^ dy)
            pltpu.make_async_remote_copy(
                src_ref=v_ref.at[me], dst_ref=out_ref.at[peer], send_sem=send_sems.at[k], recv_sem=recv_sems.at[k],
                device_id=(x ^ dx, y ^ dy, c), device_id_type=MESH).wait_recv()
        for cp in copies:
            cp.wait_send()
        mine.wait()

    return _call(
        body, name=name, out_shape=jax.ShapeDtypeStruct(v.shape, v.dtype),
        in_specs=[pl.BlockSpec(memory_space=pl.ANY)], out_specs=pl.BlockSpec(memory_space=pl.ANY),
        scratch_shapes=[pltpu.SemaphoreType.DMA((3,)), pltpu.SemaphoreType.DMA((3,)), pltpu.SemaphoreType.DMA],
        compiler_params=pltpu.CompilerParams(has_side_effects=True),
    )(v)


def _ada_shard(c_all, w_ada_shard):
    def body(c_ref, w_ref, a_ref, mod_ref):
        cv = c_ref[...]
        a = cv * _sigmoid(cv)
        a_ref[...] = a
        mod_ref[...] = _dot(a.astype(BF), w_ref[...].astype(BF))

    return _call(
        body, name="ada_shard",
        out_shape=(jax.ShapeDtypeStruct((N_DEV, D_MODEL), F32), jax.ShapeDtypeStruct((N_DEV, W_ADA_SHARD), F32)),
        compiler_params=_params(),
    )(c_all, w_ada_shard)


def _grad_w_ada(a_t, dm_shard):
    def body(a_ref, dm_ref, out_ref):
        acc = jnp.zeros((D_MODEL, W_ADA_SHARD), F32)
        for b in range(N_DEV):
            acc = acc + a_ref[:, b:b + 1] * dm_ref[b:b + 1, :]
        out_ref[...] = acc

    return _call(body, name="grad_w_ada", out_shape=jax.ShapeDtypeStruct((D_MODEL, W_ADA_SHARD), F32),
                 compiler_params=_params())(a_t, dm_shard)


def _sum_devices(parts):
    n = parts.shape[-1]

    def body(p_ref, out_ref):
        acc = p_ref[0]
        for b in range(1, N_DEV):
            acc = acc + p_ref[b]
        out_ref[...] = acc

    return _call(body, name="sum_devices", out_shape=jax.ShapeDtypeStruct((1, n), F32), compiler_params=_params())(parts)


def _add(a, b, name, out_dtype):
    r, cdim = a.shape
    tr = min(r, 256)

    def body(a_ref, b_ref, o_ref):
        o_ref[...] = (a_ref[...] + b_ref[...]).astype(out_dtype)

    spec = pl.BlockSpec((tr, cdim), lambda i: (i, 0))
    return _call(body, name=name, out_shape=jax.ShapeDtypeStruct(a.shape, out_dtype), grid=(r // tr,),
                 in_specs=[spec, spec], out_specs=spec, compiler_params=_params(("parallel",)))(a, b)


def _sum_chips(parts, name):
    _, r, cdim = parts.shape
    tr = min(r, 128)

    def body(p_ref, o_ref):
        o_ref[...] = ((p_ref[0].astype(F32) + p_ref[1].astype(F32)) + p_ref[2].astype(F32)) + p_ref[3].astype(F32)

    return _call(body, name=name, out_shape=jax.ShapeDtypeStruct((r, cdim), F32), grid=(r // tr,),
                 in_specs=[pl.BlockSpec((N_CHIPS, tr, cdim), lambda i: (0, i, 0))],
                 out_specs=pl.BlockSpec((tr, cdim), lambda i: (i, 0)), compiler_params=_params(("parallel",)))(parts)


def _adamw(w, g, m, v, name):
    r, cdim = w.shape[-2:]
    lead = w.ndim - 2
    tr = r if r <= 256 else 256
    c1 = 1.0 / (1.0 - ADAM_B1 ** ADAM_STEP)
    c2 = 1.0 / (1.0 - ADAM_B2 ** ADAM_STEP)

    def body(w_ref, g_ref, m_ref, v_ref, go_ref, d_ref, nm_ref, nv_ref):
        gv = g_ref[...].reshape(go_ref.shape)
        nm = ADAM_B1 * m_ref[...] + (1.0 - ADAM_B1) * gv
        nv = ADAM_B2 * v_ref[...] + (1.0 - ADAM_B2) * (gv * gv)
        m_hat = nm * c1
        v_hat = nv * c2
        go_ref[...] = gv
        d_ref[...] = -ADAM_LR * (m_hat / (jnp.sqrt(v_hat) + ADAM_EPS) + ADAM_WD * w_ref[...])
        nm_ref[...] = nm
        nv_ref[...] = nv

    spec = pl.BlockSpec((1,) * lead + (tr, cdim), lambda i: (0,) * lead + (i, 0))
    shp = jax.ShapeDtypeStruct(w.shape, F32)
    return _call(body, name=name, out_shape=(shp,) * 4, grid=(r // tr,),
                 in_specs=[spec, pl.BlockSpec((tr, cdim), lambda i: (i, 0)), spec, spec],
                 out_specs=(spec,) * 4, compiler_params=_params(("parallel",)))(w, g, m, v)


def _head_of_row(r, nc):
    assert nc & (nc - 1) == 0
    return lax.shift_right_logical(r, nc.bit_length() - 1)


def _chunk_mats(rows, nc, reverse):
    ri = lax.broadcasted_iota(jnp.int32, (rows, rows), 0)
    ci = lax.broadcasted_iota(jnp.int32, (rows, rows), 1)
    same = _head_of_row(ri, nc) == _head_of_row(ci, nc)
    between = jnp.where(same & ((ci > ri) if reverse else (ci < ri)), 1.0, 0.0).astype(F32)
    li = lax.broadcasted_iota(jnp.int32, (LANES, LANES), 0)
    lj = lax.broadcasted_iota(jnp.int32, (LANES, LANES), 1)
    within = jnp.where((li >= lj) if reverse else (li <= lj), 1.0, 0.0).astype(F32)
    return between, within


def _dot_hi(a, b):
    return jnp.dot(a, b, preferred_element_type=F32, precision=lax.Precision.HIGHEST)


def _scan_rows(t, nc, reverse):
    between, within = _chunk_mats(t.shape[0], nc, reverse)
    inner = _dot_hi(t, within)
    tot = jnp.sum(t, axis=1, keepdims=True)
    return inner + _dot_hi(between, jnp.broadcast_to(tot, t.shape))


def _log_forget_cumsum(f_rows, bias_rows, nc):
    def body(f_ref, b_ref, cum_ref):
        z = f_ref[...] + b_ref[...]
        lf = jnp.minimum(z, 0.0) - jnp.log(1.0 + jnp.exp(-jnp.abs(z)))
        cum_ref[...] = _scan_rows(lf, nc, False)

    return _call(body, name="forget_cumsum", out_shape=jax.ShapeDtypeStruct(f_rows.shape, F32),
                 compiler_params=_params())(f_rows, bias_rows)


def _log_forget_cumsum_bwd(dcum_rows, f_rows, bias_rows, nc):
    rows = f_rows.shape[0]

    def body(d_ref, f_ref, b_ref, df_ref, db_ref):
        dlf = _scan_rows(d_ref[...], nc, True)
        z = f_ref[...] + b_ref[...]
        df = dlf * _sigmoid(-z)
        df_ref[...] = df
        hi = lax.broadcasted_iota(jnp.int32, (FOX_HEADS, rows), 0)
        ri = lax.broadcasted_iota(jnp.int32, (FOX_HEADS, rows), 1)
        sel = jnp.where(_head_of_row(ri, nc) == hi, 1.0, 0.0).astype(F32)
        db_ref[...] = jnp.sum(_dot_hi(sel, df), axis=1, keepdims=True)

    return _call(body, name="forget_cumsum_bwd",
                 out_shape=(jax.ShapeDtypeStruct(f_rows.shape, F32), jax.ShapeDtypeStruct((FOX_HEADS, 1), F32)),
                 compiler_params=_params())(dcum_rows, f_rows, bias_rows)


def _rms_hat(xv):
    rstd = lax.rsqrt(jnp.mean(xv * xv, axis=-1, keepdims=True) + RMS_EPS)
    return xv * rstd, rstd


def _modulated(x_ref, g_ref, sc_ref, sh_ref):
    xhat, _ = _rms_hat(x_ref[...])
    return ((xhat * g_ref[...]) * sc_ref[...] + sh_ref[...]).astype(BF)


def _forget_logits(x, g_pre, scale1p, shift, w_f, tm):
    s = x.shape[0]

    def body(x_ref, g_ref, sc_ref, sh_ref, w_ref, f_ref):
        f_ref[...] = _dot(_modulated(x_ref, g_ref, sc_ref, sh_ref), w_ref[...])

    vec = _full((1, D_MODEL))
    return _call(
        body, name="forget_logits", out_shape=jax.ShapeDtypeStruct((s, LANES), F32), grid=(s // tm,),
        in_specs=[pl.BlockSpec((tm, D_MODEL), lambda i: (i, 0)), vec, vec, vec, _full((D_MODEL, LANES))],
        out_specs=pl.BlockSpec((tm, LANES), lambda i: (i, 0)), compiler_params=_params(("parallel",)),
    )(x, g_pre, scale1p, shift, w_f)


def _split3(v):
    hi = v.astype(BF).astype(F32)
    mid = (v - hi).astype(BF).astype(F32)
    lo = ((v - hi) - mid).astype(BF).astype(F32)
    return hi, mid, lo


def _in_proj(x, g_pre, scale1p, shift, w_rows, w_t_fox, cum, cos_t, sin_t, tm):
    s = x.shape[0]
    r_va, r_za, r_qb, r_zb, r_kb, r_vb = 0, 512, 1024, 1536, 2048, 2176

    def body(x_ref, g_ref, sc_ref, sh_ref, w_ref, wt_ref, cum_ref, cos_ref, sin_ref,
             h_ref, qat_ref, ka_ref, kat_ref, v_ref, vt_ref, za_ref, zb_ref, qb_ref, kb_ref, vb_ref,
             qbt_ref, kbt_ref, vbt_ref, mo_ref):
        hb = _modulated(x_ref, g_ref, sc_ref, sh_ref)
        h_ref[...] = hb

        def sec(c0, width):
            return _dot(hb, w_ref[:, c0:c0 + width])

        def sec_t(r0):
            return _dot_nt(wt_ref[r0:r0 + FOX_W, :], hb)

        q_t = sec_t(0) * Q_SCALE
        k_t = sec_t(FOX_W)
        v_t = sec_t(2 * FOX_W)
        va = sec(r_va, FOX_W)
        zeros = jnp.zeros((AUG_DIM - HEAD_DIM - AUG_ROWS, tm), F32)
        ri = lax.broadcasted_iota(jnp.int32, (AUG_ROWS, tm), 0)
        const = jnp.where(ri == AUG_ROWS - 1, 0.0, 1.0)
        ri_v = lax.broadcasted_iota(jnp.int32, (VT_ROWS - HEAD_DIM, tm), 0)
        v_feat = jnp.where(ri_v == 0, 1.0, 0.0).astype(BF)
        for hd in range(FOX_HEADS):
            rows = slice(hd * HEAD_DIM, (hd + 1) * HEAD_DIM)
            cum2 = cum_ref[hd:hd + 1, :] * LOG2E
            hi, mid, lo = (jnp.broadcast_to(part, (AUG_ROWS, tm)) for part in _split3(cum2))
            q_feat = jnp.where(ri == 1, hi, jnp.where(ri == 2, mid, jnp.where(ri == 3, lo, const)))
            k_feat = jnp.where(ri == 4, -hi, jnp.where(ri == 5, -mid, jnp.where(ri == 6, -lo, const)))
            q_aug = jnp.concatenate([q_t[rows], q_feat, zeros], axis=0)
            k_aug = jnp.concatenate([k_t[rows], k_feat, zeros], axis=0)
            mo_ref[hd:hd + 1, :] = jnp.sum(q_t[rows] * k_t[rows], axis=0, keepdims=True) + 1.0
            qat_ref[hd] = q_aug.astype(BF)
            kat_ref[hd] = k_aug.astype(BF)
            ka_ref[hd] = k_aug.T.astype(BF)
            vt_ref[hd] = jnp.concatenate([v_t[rows].astype(BF), v_feat], axis=0)
            v_ref[hd] = va[:, rows].astype(BF)
        za_ref[...] = sec(r_za, FOX_W)
        zb_ref[...] = sec(r_zb, SWA_W)
        cos2, sin2 = cos_ref[...], sin_ref[...]
        cos8 = jnp.concatenate([cos2] * 4, axis=1)
        sin8 = jnp.concatenate([sin2] * 4, axis=1)
        qb = sec(r_qb, SWA_W)
        qb = (qb * cos8 + _rope_partner(qb) * sin8) * (HEAD_DIM ** -0.5)
        qb_ref[...] = qb.astype(BF)
        for a in range(SWA_W // LANES):
            qbt_ref[a * LANES:(a + 1) * LANES, :] = qb[:, a * LANES:(a + 1) * LANES].T.astype(BF)
        kb = sec(r_kb, SWA_KV_W)
        kb = kb * cos2 + _rope_partner(kb) * sin2
        vb = sec(r_vb, SWA_KV_W)
        kb_t, vb_t = kb.T, vb.T
        for hd in range(SWA_KV_HEADS):
            sl = slice(hd * HEAD_DIM, (hd + 1) * HEAD_DIM)
            kb_ref[hd] = kb[:, sl].astype(BF)
            vb_ref[hd] = vb[:, sl].astype(BF)
            kbt_ref[hd] = kb_t[sl].astype(BF)
            vbt_ref[hd] = jnp.concatenate([vb_t[sl].astype(BF), v_feat], axis=0)

    row = lambda w: pl.BlockSpec((tm, w), lambda i: (i, 0))
    heads = lambda n, w=HEAD_DIM: pl.BlockSpec((n, tm, w), lambda i: (0, i, 0))
    heads_t = lambda w: pl.BlockSpec((FOX_HEADS, w, tm), lambda i: (0, 0, i))
    vec = _full((1, D_MODEL))
    hs = lambda a, b: jax.ShapeDtypeStruct((FOX_HEADS, a, b), BF)
    out_shape = (
        jax.ShapeDtypeStruct((s, D_MODEL), BF),
        hs(AUG_DIM, s), hs(s, AUG_DIM), hs(AUG_DIM, s), hs(s, HEAD_DIM), hs(VT_ROWS, s),
        jax.ShapeDtypeStruct((s, FOX_W), F32), jax.ShapeDtypeStruct((s, SWA_W), F32),
        jax.ShapeDtypeStruct((s, SWA_W), BF),
        jax.ShapeDtypeStruct((SWA_KV_HEADS, s, HEAD_DIM), BF), jax.ShapeDtypeStruct((SWA_KV_HEADS, s, HEAD_DIM), BF),
        jax.ShapeDtypeStruct((SWA_W, s), BF),
        jax.ShapeDtypeStruct((SWA_KV_HEADS, HEAD_DIM, s), BF), jax.ShapeDtypeStruct((SWA_KV_HEADS, VT_ROWS, s), BF),
        jax.ShapeDtypeStruct((FOX_HEADS, s), F32),
    )
    kv_t = lambda w: pl.BlockSpec((SWA_KV_HEADS, w, tm), lambda i: (0, 0, i))
    return _call(
        body, name="in_proj", out_shape=out_shape, grid=(s // tm,),
        in_specs=[row(D_MODEL), vec, vec, vec, _full(w_rows.shape), _full(w_t_fox.shape),
                  pl.BlockSpec((FOX_HEADS, tm), lambda i: (0, i)), row(LANES), row(LANES)],
        out_specs=(row(D_MODEL), heads_t(AUG_DIM), heads(FOX_HEADS, AUG_DIM), heads_t(AUG_DIM), heads(FOX_HEADS),
                   heads_t(VT_ROWS), row(FOX_W), row(SWA_W), row(SWA_W), heads(SWA_KV_HEADS), heads(SWA_KV_HEADS),
                   pl.BlockSpec((SWA_W, tm), lambda i: (0, i)), kv_t(HEAD_DIM), kv_t(VT_ROWS),
                   pl.BlockSpec((FOX_HEADS, tm), lambda i: (0, i))),
        compiler_params=_params(("parallel",)),
    )(x, g_pre, scale1p, shift, w_rows, w_t_fox, cum, cos_t, sin_t)


def _diag_chunks(d, bq, bk, chunk):
    out = []
    for c0 in range(0, bq, chunk):
        if d is None or d * bk + bk - 1 <= c0:
            out.append((c0, None))
        elif d * bk <= c0 + chunk - 1:
            kpos = d * bk + lax.broadcasted_iota(jnp.int32, (bk, chunk), 0)
            qpos = c0 + lax.broadcasted_iota(jnp.int32, (bk, chunk), 1)
            out.append((c0, kpos <= qpos))
    return out


def _fox_fwd(qat, ka, vt, m_own, bq, bk, chunk, running_max):
    nh, _, s = qat.shape
    r = bq // bk

    def body(ka_ref, qat_ref, vt_ref, mo_ref, o_ref, lse_ref, bad_ref, *rest):
        pt_ref, m_scr, acc_scr = (None,) * running_max + rest
        i, j = pl.program_id(1), pl.program_id(2)

        @pl.when(j == 0)
        def _():
            m_scr[...] = jnp.full(m_scr.shape, NEG, F32) if running_max else mo_ref[0]
            acc_scr[...] = jnp.zeros(acc_scr.shape, F32)

        def careful(d):
            kv, vtv = ka_ref[0], vt_ref[0]

            def one_chunk(n, carry):
                c0 = pl.multiple_of(n * chunk, chunk)
                cs = pl.ds(c0, chunk)
                sc = _dot(kv, qat_ref[0, :, cs])
                if d is not None:
                    kpos = d * bk + lax.broadcasted_iota(jnp.int32, (bk, chunk), 0)
                    qpos = c0 + lax.broadcasted_iota(jnp.int32, (bk, chunk), 1)
                    sc = jnp.where(kpos <= qpos, sc, NEG)
                m_prev = m_scr[:, cs]
                m_new = jnp.maximum(m_prev, jnp.max(sc, axis=0, keepdims=True))
                p = jnp.exp2(sc - m_new).astype(BF)
                acc_scr[:, cs] = jnp.exp2(m_prev - m_new) * acc_scr[:, cs] + _dot(vtv, p)
                m_scr[:, cs] = m_new
                return carry

            lax.fori_loop(0, bq // chunk, one_chunk, 0)

        def fast(d):
            kv, vtv = ka_ref[0], vt_ref[0]
            todo = _diag_chunks(d, bq, bk, chunk)
            scores = lambda c0: _dot(kv, qat_ref[0, :, c0:c0 + chunk])
            sc_next = scores(todo[0][0])
            for n, (c0, mask) in enumerate(todo):
                cs = slice(c0, c0 + chunk)
                sc = sc_next
                if n + 1 < len(todo):
                    sc_next = scores(todo[n + 1][0])
                if mask is not None:
                    sc = jnp.where(mask, sc, NEG)
                p = jnp.exp2(sc - m_scr[:, cs]).astype(BF)
                pt_ref[0, :, cs] = p
                acc_scr[:, cs] += _dot(vtv, p)

        step = careful if running_max else fast

        @pl.when(j < i * r)
        def _():
            step(None)

        for d in range(r):
            @pl.when(j == i * r + d)
            def _(d=d):
                step(d)

        @pl.when(j == i * r + r - 1)
        def _():
            l = acc_scr[HEAD_DIM:HEAD_DIM + 1, :]
            o_ref[0] = acc_scr[:HEAD_DIM, :] / l
            lse_ref[0] = m_scr[...] + jnp.log2(l)
            bad_ref[0] = jnp.where(l < OVERFLOW_GUARD, 0.0, 1.0)

    kmap = lambda h, i, j: (h, jnp.minimum(j, i * r + r - 1), 0)
    kmap_t = lambda h, i, j: (h, 0, jnp.minimum(j, i * r + r - 1))
    qrow = pl.BlockSpec((1, 1, bq), lambda h, i, j: (h, 0, i))
    row_shape = jax.ShapeDtypeStruct((nh, 1, s), F32)
    out_shape = (jax.ShapeDtypeStruct((nh, HEAD_DIM, s), F32), row_shape, row_shape)
    out_specs = (pl.BlockSpec((1, HEAD_DIM, bq), lambda h, i, j: (h, 0, i)), qrow, qrow)
    if not running_max:
        out_shape += (jax.ShapeDtypeStruct((nh, s, s), BF),)
        out_specs += (pl.BlockSpec((1, bk, bq), lambda h, i, j: (h, jnp.minimum(j, i * r + r - 1), i)),)
    return _call(
        body, name="fox_fwd_running_max" if running_max else "fox_fwd",
        out_shape=out_shape,
        grid=(nh, s // bq, s // bk),
        in_specs=[pl.BlockSpec((1, bk, AUG_DIM), kmap), pl.BlockSpec((1, AUG_DIM, bq), lambda h, i, j: (h, 0, i)),
                  pl.BlockSpec((1, VT_ROWS, bk), kmap_t), qrow],
        out_specs=out_specs,
        scratch_shapes=[pltpu.VMEM((1, bq), F32), pltpu.VMEM((VT_ROWS, bq), F32)],
        compiler_params=_params(("parallel", "parallel", "arbitrary")),
    )(ka, qat, vt, m_own)


def _fox_bwd(qat, ka, kat, v, dot_, lse, delta, pt, bq, bk, chunk, dq_blk):
    nh, _, s = qat.shape
    r = bq // bk
    nq = s // bq
    stored = pt is not None

    def body(a_ref, b_ref, kat_ref, v_ref, qat_ref, do_ref, dl_ref, dq_ref, dk_ref, dv_ref, dk_scr, dv_scr):
        ka_ref, lse_ref, pt_ref = (None, None, a_ref) if stored else (a_ref, b_ref, None)
        j, i = pl.program_id(1), pl.program_id(2)

        @pl.when((j == 0) & (i == 0))
        def _():
            dq_ref[...] = jnp.zeros(dq_ref.shape, F32)

        @pl.when(i == 0)
        def _():
            dk_scr[...] = jnp.zeros(dk_scr.shape, F32)
            dv_scr[...] = jnp.zeros(dv_scr.shape, F32)

        def step(d):
            ktv, vv = kat_ref[0], v_ref[0]
            kv = None if stored else ka_ref[0]
            todo = _diag_chunks(d, bq, bk, chunk)

            def products(c0):
                cs = slice(c0, c0 + chunk)
                return None if stored else _dot(kv, qat_ref[0, :, cs]), _dot(vv, do_ref[0, :, cs])

            nxt = products(todo[0][0])
            for n, (c0, mask) in enumerate(todo):
                cs = slice(c0, c0 + chunk)
                sc, dp = nxt
                if n + 1 < len(todo):
                    nxt = products(todo[n + 1][0])
                if stored:
                    p_bf = pt_ref[0, :, cs]
                    p = p_bf.astype(F32)
                else:
                    p = jnp.exp2(sc - lse_ref[0, :, cs])
                    if mask is not None:
                        p = jnp.where(mask, p, 0.0)
                    p_bf = p.astype(BF)
                ds = (p * (dp - dl_ref[0, :, cs])).astype(BF)
                dv_scr[...] += _dot_nt(do_ref[0, :, cs], p_bf)
                dk_scr[:VT_ROWS, :] += _dot_nt(qat_ref[0, :VT_ROWS, cs], ds)
                c1 = c0 % dq_blk
                dq_ref[0, i * (bq // dq_blk) + c0 // dq_blk, :VT_ROWS, c1:c1 + chunk] += _dot(ktv[:VT_ROWS], ds)

        @pl.when(i * r > j)
        def _():
            step(None)

        for d in range(r):
            @pl.when(j == i * r + d)
            def _(d=d):
                step(d)

        @pl.when(i == nq - 1)
        def _():
            dk_ref[0] = dk_scr[...]
            dv_ref[0] = dv_scr[...]

    qmap = lambda h, j, i: (h, 0, jnp.maximum(i, j // r))
    kmap_t = lambda h, j, i: (h, 0, j)
    if stored:
        first = [(pt, pl.BlockSpec((1, bk, bq), lambda h, j, i: (h, j, jnp.maximum(i, j // r)))),
                 (delta, pl.BlockSpec((1, 1, bq), qmap))]
    else:
        first = [(ka, pl.BlockSpec((1, bk, AUG_DIM), lambda h, j, i: (h, j, 0))), (lse, pl.BlockSpec((1, 1, bq), qmap))]
    return _call(
        body, name="fox_bwd" if stored else "fox_bwd_recompute",
        out_shape=(jax.ShapeDtypeStruct((nh, s // dq_blk, AUG_DIM, dq_blk), F32),
                   jax.ShapeDtypeStruct((nh, AUG_DIM, s), F32), jax.ShapeDtypeStruct((nh, HEAD_DIM, s), F32)),
        grid=(nh, s // bk, nq),
        in_specs=[first[0][1], first[1][1], pl.BlockSpec((1, AUG_DIM, bk), kmap_t),
                  pl.BlockSpec((1, bk, HEAD_DIM), lambda h, j, i: (h, j, 0)),
                  pl.BlockSpec((1, AUG_DIM, bq), qmap), pl.BlockSpec((1, HEAD_DIM, bq), qmap),
                  pl.BlockSpec((1, 1, bq), qmap)],
        out_specs=(pl.BlockSpec((1, s // dq_blk, AUG_DIM, dq_blk), lambda h, j, i: (h, 0, 0, 0)),
                   pl.BlockSpec((1, AUG_DIM, bk), kmap_t), pl.BlockSpec((1, HEAD_DIM, bk), kmap_t)),
        scratch_shapes=[pltpu.VMEM((AUG_DIM, bk), F32), pltpu.VMEM((HEAD_DIM, bk), F32)],
        compiler_params=_params(("parallel", "arbitrary", "arbitrary")),
    )(first[0][0], first[1][0], kat, v, qat, dot_, delta)


def _swa_mask(i, tq):
    kpos = i * tq - WINDOW + lax.broadcasted_iota(jnp.int32, (tq + WINDOW, tq), 0)
    qpos = i * tq + lax.broadcasted_iota(jnp.int32, (tq + WINDOW, tq), 1)
    rel = qpos - kpos
    return (rel >= 0) & (rel < WINDOW) & (kpos >= 0)


def _swa_rows(ref, i, tq):
    before = pl.multiple_of(jnp.maximum(i * tq - WINDOW, 0), WINDOW)
    return jnp.concatenate([ref[0, pl.ds(before, WINDOW), :], ref[0, pl.ds(pl.multiple_of(i * tq, tq), tq), :]], axis=0)


def _swa_before(n_rows, tq):
    return pl.BlockSpec((1, n_rows, WINDOW), lambda g, i: (g, 0, jnp.maximum(i * (tq // WINDOW) - 1, 0)))


def _swa_probs_t(kw, q_t, mask, sink):
    sc = jnp.where(mask, _dot(kw, q_t), NEG)
    m = jnp.maximum(jnp.max(sc, axis=0, keepdims=True), sink)
    p = jnp.exp(sc - m)
    e_sink = jnp.exp(sink - m)
    inv_l = 1.0 / (jnp.sum(p, axis=0, keepdims=True) + e_sink)
    return p * inv_l, e_sink * inv_l


def _swa_fwd(qbt, kb, vbt, sinks, tq):
    s = qbt.shape[1]
    gw = SWA_GROUP * HEAD_DIM

    def body(q_ref, k_ref, vb_ref, vc_ref, s_ref, o_ref):
        i = pl.program_id(1)
        mask = _swa_mask(i, tq)
        kw = _swa_rows(k_ref, i, tq)
        vtw = jnp.concatenate([vb_ref[0], vc_ref[0]], axis=1)
        sk = s_ref[0]
        for hh in range(SWA_GROUP):
            rows = slice(hh * HEAD_DIM, (hh + 1) * HEAD_DIM)
            sink = sk[:, hh:hh + 1]
            sc = jnp.where(mask, _dot(kw, q_ref[rows, :]), NEG)
            m = jnp.maximum(jnp.max(sc, axis=0, keepdims=True), sink)
            acc = _dot(vtw, jnp.exp(sc - m).astype(BF))
            o_ref[rows, :] = acc[:HEAD_DIM] / (acc[HEAD_DIM:HEAD_DIM + 1] + jnp.exp(sink - m))

    kvspec = pl.BlockSpec((1, s, HEAD_DIM), lambda g, i: (g, 0, 0))
    qspec = pl.BlockSpec((gw, tq), lambda g, i: (g, i))
    return _call(
        body, name="swa_fwd", out_shape=jax.ShapeDtypeStruct((SWA_W, s), F32), grid=(SWA_KV_HEADS, s // tq),
        in_specs=[qspec, kvspec, _swa_before(VT_ROWS, tq), pl.BlockSpec((1, VT_ROWS, tq), lambda g, i: (g, 0, i)),
                  pl.BlockSpec((1, 1, SWA_GROUP), lambda g, i: (g, 0, 0))],
        out_specs=qspec, compiler_params=_params(("parallel", "parallel")),
    )(qbt, kb, vbt, vbt, sinks)


def _swa_bwd(qb, qbt, kb, kbt, vb, sinks, dob, dobt, tq):
    s = qb.shape[0]
    gw = SWA_GROUP * HEAD_DIM

    def body(q_ref, qt_ref, k_ref, ktb_ref, ktc_ref, v_ref, s_ref, do_ref, dot_ref, dq_ref, dk_ref, dv_ref, ds_ref):
        i = pl.program_id(1)

        @pl.when(i == 0)
        def _():
            dk_ref[...] = jnp.zeros(dk_ref.shape, F32)
            dv_ref[...] = jnp.zeros(dv_ref.shape, F32)
            ds_ref[...] = jnp.zeros(ds_ref.shape, F32)

        mask = _swa_mask(i, tq)
        kw = _swa_rows(k_ref, i, tq)
        vw = _swa_rows(v_ref, i, tq)
        ktw = jnp.concatenate([ktb_ref[0], ktc_ref[0]], axis=1)
        qv, dov = q_ref[...], do_ref[...]
        sk = s_ref[0]
        dsinks = []
        dk_acc = jnp.zeros((tq + WINDOW, HEAD_DIM), F32)
        dv_acc = jnp.zeros((tq + WINDOW, HEAD_DIM), F32)
        for hh in range(SWA_GROUP):
            rows = slice(hh * HEAD_DIM, (hh + 1) * HEAD_DIM)
            p, p_sink = _swa_probs_t(kw, qt_ref[rows, :], mask, sk[:, hh:hh + 1])
            dp = _dot(vw, dot_ref[rows, :])
            delta = jnp.sum(p * dp, axis=0, keepdims=True)
            dsc = (p * (dp - delta)).astype(BF)
            dq_ref[rows, :] = _dot(ktw, dsc)
            dk_acc = dk_acc + _dot(dsc, qv[:, rows])
            dv_acc = dv_acc + _dot(p.astype(BF), dov[:, rows])
            dsinks.append(-jnp.sum(p_sink * delta, axis=1, keepdims=True))
        before = pl.ds(pl.multiple_of(jnp.maximum(i * tq - WINDOW, 0), WINDOW), WINDOW)
        own = pl.ds(pl.multiple_of(i * tq, tq), tq)
        dk_ref[0, before, :] += dk_acc[:WINDOW]
        dk_ref[0, own, :] += dk_acc[WINDOW:]
        dv_ref[0, before, :] += dv_acc[:WINDOW]
        dv_ref[0, own, :] += dv_acc[WINDOW:]
        ds_ref[0] += jnp.concatenate(dsinks, axis=1)

    kvspec = pl.BlockSpec((1, s, HEAD_DIM), lambda g, i: (g, 0, 0))
    qspec = pl.BlockSpec((tq, gw), lambda g, i: (i, g))
    qspec_t = pl.BlockSpec((gw, tq), lambda g, i: (g, i))
    sspec = pl.BlockSpec((1, 1, SWA_GROUP), lambda g, i: (g, 0, 0))
    kvshape = jax.ShapeDtypeStruct((SWA_KV_HEADS, s, HEAD_DIM), F32)
    return _call(
        body, name="swa_bwd",
        out_shape=(jax.ShapeDtypeStruct((SWA_W, s), F32), kvshape, kvshape,
                   jax.ShapeDtypeStruct((SWA_KV_HEADS, 1, SWA_GROUP), F32)),
        grid=(SWA_KV_HEADS, s // tq),
        in_specs=[qspec, qspec_t, kvspec, _swa_before(HEAD_DIM, tq),
                  pl.BlockSpec((1, HEAD_DIM, tq), lambda g, i: (g, 0, i)), kvspec, sspec, qspec, qspec_t],
        out_specs=(qspec_t, kvspec, kvspec, sspec),
        compiler_params=_params(("parallel", "arbitrary")),
    )(qb, qbt, kb, kbt, kbt, vb, sinks, dob, dobt)


def _pairs_to_rows(ref, n_rows=HEAD_DIM):
    parts = []
    for a in range(0, FOX_HEADS, 2):
        parts.append(jnp.concatenate([ref[a][:n_rows], ref[a + 1][:n_rows]], axis=0).T)
    return jnp.concatenate(parts, axis=1)


def _blocks_to_rows(ref):
    return jnp.concatenate([ref[a:a + LANES, :].T for a in range(0, ref.shape[0], LANES)], axis=1)


def _out_proj(oat, za, obt, zb, x, tgt, w_out, w_out_t, gate, g_post, inv_l, tm):
    s = x.shape[0]

    def body(oat_ref, za_ref, obt_ref, zb_ref, x_ref, t_ref, w_ref, wt_ref, gate_ref, gp_ref, il_ref,
             dout_ref, doat_ref, dla_ref, dza_ref, dob_ref, dobt_ref, dzb_ref, gw_ref, dgate_ref, dgp_ref, loss_ref):
        i = pl.program_id(0)

        @pl.when(i == 0)
        def _():
            gw_ref[...] = jnp.zeros(gw_ref.shape, F32)
            dgate_ref[...] = jnp.zeros(dgate_ref.shape, F32)
            dgp_ref[...] = jnp.zeros(dgp_ref.shape, F32)
            loss_ref[...] = jnp.zeros(loss_ref.shape, F32)

        oa_v = _pairs_to_rows(oat_ref)
        ob_v = _blocks_to_rows(obt_ref)
        za_v, zb_v = za_ref[...], zb_ref[...]
        sga, sgb = _sigmoid(za_v), _sigmoid(zb_v)
        sila, silb = za_v * sga, zb_v * sgb
        u = jnp.concatenate([oa_v * sila, ob_v * silb], axis=1).astype(BF)
        yv = _dot(u, w_ref[...])
        yhat, rstd = _rms_hat(yv)
        gp, gate_v = gp_ref[...], gate_ref[...]
        nrm = yhat * gp
        diff = (x_ref[...] + gate_v * nrm) - t_ref[...]
        loss_ref[...] += 0.5 * jnp.sum(jnp.sum(diff * diff, axis=1, keepdims=True), axis=0, keepdims=True) / D_MODEL
        dout = diff * (1.0 / D_MODEL)
        dout_ref[...] = dout
        dgate_ref[...] += jnp.sum(dout * nrm, axis=0, keepdims=True)
        dn = dout * gate_v
        dgp_ref[...] += jnp.sum(dn * yhat, axis=0, keepdims=True)
        dyhat = dn * gp
        dy = (rstd * (dyhat - yhat * jnp.mean(dyhat * yhat, axis=1, keepdims=True))).astype(BF)
        gw_ref[...] += _dot_tn(u, dy)
        du = _dot(dy, wt_ref[...])
        dua, dub = du[:, :FOX_W], du[:, FOX_W:]
        doa = dua * sila
        for a in range(0, FOX_HEADS, 2):
            pair_t = doa[:, a * HEAD_DIM:(a + 2) * HEAD_DIM].T
            for hd, rows in ((a, slice(0, HEAD_DIM)), (a + 1, slice(HEAD_DIM, 2 * HEAD_DIM))):
                inv_l = il_ref[hd]
                doat_ref[hd] = (pair_t[rows] * inv_l).astype(BF)
                dla_ref[hd] = jnp.sum(pair_t[rows] * oat_ref[hd], axis=0, keepdims=True) * inv_l
        dob = dub * silb
        dob_ref[...] = dob.astype(BF)
        for a in range(0, SWA_W, LANES):
            dobt_ref[a:a + LANES, :] = dob[:, a:a + LANES].T.astype(BF)
        dza_ref[...] = (dua * oa_v * (sga * (1.0 + za_v * (1.0 - sga)))).astype(BF)
        dzb_ref[...] = (dub * ob_v * (sgb * (1.0 + zb_v * (1.0 - sgb)))).astype(BF)

    row = lambda w: pl.BlockSpec((tm, w), lambda i: (i, 0))
    heads_t = lambda w: pl.BlockSpec((FOX_HEADS, w, tm), lambda i: (0, 0, i))
    vec = _full((1, D_MODEL))
    mat = _full((D_MODEL, D_MODEL))
    out_shape = (
        jax.ShapeDtypeStruct((s, D_MODEL), F32),
        jax.ShapeDtypeStruct((FOX_HEADS, HEAD_DIM, s), BF), jax.ShapeDtypeStruct((FOX_HEADS, 1, s), F32),
        jax.ShapeDtypeStruct((s, FOX_W), BF), jax.ShapeDtypeStruct((s, SWA_W), BF), jax.ShapeDtypeStruct((SWA_W, s), BF),
        jax.ShapeDtypeStruct((s, SWA_W), BF),
        jax.ShapeDtypeStruct((D_MODEL, D_MODEL), F32),
        jax.ShapeDtypeStruct((1, D_MODEL), F32), jax.ShapeDtypeStruct((1, D_MODEL), F32),
        jax.ShapeDtypeStruct((1, 1), F32),
    )
    col = pl.BlockSpec((SWA_W, tm), lambda i: (0, i))
    return _call(
        body, name="out_proj", out_shape=out_shape, grid=(s // tm,),
        in_specs=[heads_t(HEAD_DIM), row(FOX_W), col, row(SWA_W), row(D_MODEL), row(D_MODEL), mat, mat, vec, vec,
                  heads_t(1)],
        out_specs=(row(D_MODEL), heads_t(HEAD_DIM), heads_t(1), row(FOX_W), row(SWA_W), col, row(SWA_W), mat, vec, vec,
                   _full((1, 1))),
        compiler_params=_params(("arbitrary",)),
    )(oat, za, obt, zb, x, tgt, w_out, w_out_t, gate, g_post, inv_l)


def _assemble_dproj(dqt, dkt, dvt, dza, dqb, dzb, dkb, dvb, df, cos_t, sin_t, tm):
    s = dza.shape[0]

    def body(dqt_ref, dkt_ref, dvt_ref, dza_ref, dqb_ref, dzb_ref, dkb_ref, dvb_ref, df_ref, cos_ref, sin_ref, o_ref):
        def cat(ref, n):
            return jnp.concatenate([ref[hd] for hd in range(n)], axis=1)

        cos2, sin2 = cos_ref[...], sin_ref[...]
        cos8 = jnp.concatenate([cos2] * 4, axis=1)
        sin8 = jnp.concatenate([sin2] * 4, axis=1)
        scale = HEAD_DIM ** -0.5
        o_ref[:, C_QA:C_QA + FOX_W] = (_pairs_to_rows(dqt_ref.at[:, 0]) * scale).astype(BF)
        o_ref[:, C_KA:C_KA + FOX_W] = (_pairs_to_rows(dkt_ref) * LN2).astype(BF)
        o_ref[:, C_VA:C_VA + FOX_W] = _pairs_to_rows(dvt_ref).astype(BF)
        o_ref[:, C_ZA:C_ZA + FOX_W] = dza_ref[...]
        dq = _blocks_to_rows(dqb_ref) * scale
        o_ref[:, C_QB:C_QB + SWA_W] = (dq * cos8 - _rope_partner(dq) * sin8).astype(BF)
        o_ref[:, C_ZB:C_ZB + SWA_W] = dzb_ref[...]
        dk = cat(dkb_ref, SWA_KV_HEADS)
        o_ref[:, C_KB:C_KB + SWA_KV_W] = (dk * cos2 - _rope_partner(dk) * sin2).astype(BF)
        o_ref[:, C_VB:C_VB + SWA_KV_W] = cat(dvb_ref, SWA_KV_HEADS).astype(BF)
        o_ref[:, C_F:C_F + LANES] = df_ref[...].astype(BF)

    row = lambda w: pl.BlockSpec((tm, w), lambda i: (i, 0))
    heads = lambda n: pl.BlockSpec((n, tm, HEAD_DIM), lambda i: (0, i, 0))
    heads_t = lambda w: pl.BlockSpec((FOX_HEADS, w, tm), lambda i: (0, 0, i))
    return _call(
        body, name="assemble_dproj", out_shape=jax.ShapeDtypeStruct((s, WP), BF), grid=(s // tm,),
        in_specs=[pl.BlockSpec((FOX_HEADS, 1, AUG_DIM, tm), lambda i: (0, i, 0, 0)), heads_t(AUG_DIM), heads_t(HEAD_DIM),
                  row(FOX_W), pl.BlockSpec((SWA_W, tm), lambda i: (0, i)), row(SWA_W), heads(SWA_KV_HEADS),
                  heads(SWA_KV_HEADS), row(LANES), row(LANES), row(LANES)],
        out_specs=row(WP), compiler_params=_params(("parallel",)),
    )(dqt, dkt, dvt, dza, dqb, dzb, dkb, dvb, df, cos_t, sin_t)


def _in_proj_bwd_x(dproj, w_al_t, x, dout, g_pre, scale1p, tm):
    s = x.shape[0]

    def body(dp_ref, wt_ref, x_ref, dout_ref, g_ref, sc_ref, gx_ref, dsh_ref, dsc_ref, dg_ref):
        i = pl.program_id(0)

        @pl.when(i == 0)
        def _():
            dsh_ref[...] = jnp.zeros(dsh_ref.shape, F32)
            dsc_ref[...] = jnp.zeros(dsc_ref.shape, F32)
            dg_ref[...] = jnp.zeros(dg_ref.shape, F32)

        dh = _dot(dp_ref[...], wt_ref[...])
        xhat, rstd = _rms_hat(x_ref[...])
        g, sc = g_ref[...], sc_ref[...]
        dsh_ref[...] += jnp.sum(dh, axis=0, keepdims=True)
        dhx = dh * xhat
        dsc_ref[...] += jnp.sum(dhx * g, axis=0, keepdims=True)
        dg_ref[...] += jnp.sum(dhx * sc, axis=0, keepdims=True)
        dxhat = dh * (g * sc)
        gx_ref[...] = dout_ref[...] + rstd * (dxhat - xhat * jnp.mean(dxhat * xhat, axis=1, keepdims=True))

    row = lambda w: pl.BlockSpec((tm, w), lambda i: (i, 0))
    vec = _full((1, D_MODEL))
    vshape = jax.ShapeDtypeStruct((1, D_MODEL), F32)
    return _call(
        body, name="in_proj_bwd_x", out_shape=(jax.ShapeDtypeStruct((s, D_MODEL), F32), vshape, vshape, vshape),
        grid=(s // tm,),
        in_specs=[row(WP), _full((WP, D_MODEL)), row(D_MODEL), row(D_MODEL), vec, vec],
        out_specs=(row(D_MODEL), vec, vec, vec), compiler_params=_params(("arbitrary",)),
    )(dproj, w_al_t, x, dout, g_pre, scale1p)


def _in_proj_bwd_w(h, dproj, tk, tn):
    s = h.shape[0]

    def body(h_ref, dp_ref, gw_ref):
        @pl.when(pl.program_id(1) == 0)
        def _():
            gw_ref[...] = jnp.zeros(gw_ref.shape, F32)

        gw_ref[...] += _dot_tn(h_ref[...], dp_ref[...])

    return _call(
        body, name="in_proj_bwd_w", out_shape=jax.ShapeDtypeStruct((D_MODEL, WP), F32), grid=(WP // tn, s // tk),
        in_specs=[pl.BlockSpec((tk, D_MODEL), lambda n, k: (k, 0)), pl.BlockSpec((tk, tn), lambda n, k: (k, n))],
        out_specs=pl.BlockSpec((D_MODEL, tn), lambda n, k: (0, n)),
        compiler_params=_params(("parallel", "arbitrary")),
    )(h, dproj)


def _align_w_in(w_cols):
    def part(name, width):
        return w_cols[:, _SRC[name]:_SRC[name] + width]

    fpad = jnp.pad(part("fa", FOX_HEADS), ((0, 0), (0, LANES - FOX_HEADS)))
    return jnp.concatenate([part("qa", FOX_W), part("ka", FOX_W), part("va", FOX_W), part("za", FOX_W),
                            part("qb", SWA_W), part("zb", SWA_W), part("kb", SWA_KV_W), part("vb", SWA_KV_W), fpad], axis=1)


def _unalign_w_in(g_al):
    def part(c0, width):
        return g_al[:, c0:c0 + width]

    return jnp.concatenate([part(C_QA, FOX_W), part(C_KA, FOX_W), part(C_VA, FOX_W), part(C_F, FOX_HEADS),
                            part(C_ZA, FOX_W), part(C_QB, SWA_W), part(C_KB, SWA_KV_W), part(C_VB, SWA_KV_W),
                            part(C_ZB, SWA_W)], axis=1)


def _rope_tables(positions):
    inv_freq = ROPE_THETA ** (-jnp.arange(HALF, dtype=F32) / HALF)
    ang = positions.astype(F32)[:, None] * inv_freq
    cos, sin = jnp.cos(ang), jnp.sin(ang)
    return jnp.concatenate([cos, cos, cos, cos], axis=1), jnp.concatenate([-sin, sin, -sin, sin], axis=1)


def _tiles(s):
    if s >= 4096:
        return dict(tm=512, blk=512, bq=2048, bk=1024, bk_bwd=512, chunk=256, tq=256, tm_out=256, tk=512, tn=1152)
    return dict(tm=128, blk=128, bq=256, bk=128, bk_bwd=128, chunk=128, tq=128, tm_out=128, tk=128, tn=1152)


def kernel(x, c, positions, w_ada, b_ada, g_pre, w_in, b_fgate, sinks, w_out, g_post, loss_target, m_w_ada, m_b_ada, m_g_pre, m_w_in, m_b_fgate, m_sinks, m_w_out, m_g_post, v_w_ada, v_b_ada, v_g_pre, v_w_in, v_b_fgate, v_sinks, v_w_out, v_g_post):
    s = x.shape[1]
    t = _tiles(s)
    nc = s // LANES
    rows = FOX_HEADS * nc
    me = 4 * lax.axis_index("x") + 2 * lax.axis_index("y") + lax.axis_index("c")
    chip = 2 * lax.axis_index("x") + lax.axis_index("y")
    core = lax.axis_index("c")
    x2, tgt = x[0], loss_target[0]

    c_all = _allgather_devices(c, "gather_c")[:, 0, :]
    a_all, mod_shard = _ada_shard(c_all, w_ada[0])
    mod_all = _allgather_devices(mod_shard, "gather_mod")
    mod_rows = lax.dynamic_index_in_dim(mod_all, me, axis=1, keepdims=False)
    mod = mod_rows.reshape(N_CHIPS, 2, W_ADA_SHARD)[:, 0, :].reshape(1, 3 * D_MODEL) + b_ada
    shift, scale1p, gate = mod[:, :D_MODEL], 1.0 + mod[:, D_MODEL:2 * D_MODEL], mod[:, 2 * D_MODEL:]

    w_in_pad = jnp.pad(w_in[0].astype(BF), ((0, 0), (0, W_IN_SHARD_PAD - W_IN_SHARD)))
    w_in_all = _allgather_chips(w_in_pad, "gather_w_in")
    w_cols = jnp.concatenate([w_in_all[k, :, :W_IN_SHARD] for k in range(N_CHIPS)], axis=1)
    w_al = _align_w_in(w_cols)
    w_al_t = w_al.T
    w_out_all = _allgather_chips(w_out[0].astype(BF), "gather_w_out").reshape(D_MODEL, D_MODEL)
    w_out_t = w_out_all.T

    cos_t, sin_t = _rope_tables(positions[0])

    f_pad = _forget_logits(x2, g_pre, scale1p, shift, w_al[:, C_F:], t["tm"])
    f_rows = f_pad[:, :FOX_HEADS].T.reshape(rows, LANES)
    bias_rows = jnp.repeat(b_fgate[0], nc)[:, None]
    cum = _log_forget_cumsum(f_rows, bias_rows, nc).reshape(FOX_HEADS, s)
    h, qat, ka, kat, va, vat, za, zb, qb, kb, vb, qbt, kbt, vbt, m_own = _in_proj(
        x2, g_pre, scale1p, shift, w_al[:, C_VA:C_F], w_al_t[:C_ZA], cum, cos_t, sin_t, t["tm"])
    m_own = m_own[:, None, :]
    fox_args = (qat, ka, vat, m_own, t["bq"], t["bk"], t["chunk"])
    oat, lse, bad, pt = _fox_fwd(*fox_args, running_max=False)
    overflowed = jnp.max(bad) > 0.0
    oat, lse = lax.cond(overflowed, lambda: _fox_fwd(*fox_args, running_max=True)[:2], lambda: (oat, lse))
    inv_l = jnp.where(overflowed, 1.0, jnp.exp2(m_own - lse))
    sinks_g = sinks.reshape(SWA_KV_HEADS, 1, SWA_GROUP)
    obt = _swa_fwd(qbt, kb, vbt, sinks_g, t["tq"])

    dout, doat, delta_a, dza, dob, dobt, dzb, gw_out, dgate, dg_post, loss_part = _out_proj(
        oat, za, obt, zb, x2, tgt, w_out_all, w_out_t, gate, g_post, inv_l, t["tm_out"])

    bwd_args = (qat, ka, kat, va, doat, lse, delta_a)
    bwd_tiles = (t["bq"], t["bk_bwd"], t["chunk"], t["blk"])
    dqt, dkt, dvt = lax.cond(overflowed, lambda: _fox_bwd(*bwd_args, None, *bwd_tiles),
                             lambda: _fox_bwd(*bwd_args, pt, *bwd_tiles))
    dcum = dqt[:, :, HEAD_DIM, :].reshape(FOX_HEADS, s) - dkt[:, HEAD_DIM, :]
    df_rows, db_heads = _log_forget_cumsum_bwd(dcum.reshape(rows, LANES), f_rows, bias_rows, nc)
    df_pad = jnp.pad(df_rows.reshape(FOX_HEADS, s).T, ((0, 0), (0, LANES - FOX_HEADS)))
    dqb, dkb, dvb, dsinks = _swa_bwd(qb, qbt, kb, kbt, vb, sinks_g, dob, dobt, t["tq"])

    dproj = _assemble_dproj(dqt, dkt, dvt, dza, dqb, dzb, dkb, dvb, df_pad, cos_t, sin_t, t["blk"])
    grad_x, dshift, dscale, dg_pre = _in_proj_bwd_x(dproj, w_al_t, x2, dout, g_pre, scale1p, t["tm_out"])
    gw_in = _unalign_w_in(_in_proj_bwd_w(h, dproj, t["tk"], t["tn"]))

    pad_lane = lambda vrow: jnp.pad(vrow, ((0, 0), (0, LANES - vrow.shape[1])))
    packed = jnp.concatenate([dshift, dscale, dgate, dg_pre, dg_post,
                              pad_lane(db_heads.reshape(1, FOX_HEADS)), pad_lane(dsinks.reshape(1, FOX_HEADS)),
                              pad_lane(loss_part)], axis=1)
    parts = _allgather_devices(packed, "gather_partials")
    tot = _sum_devices(parts)
    loss = tot[0, P_LOSS]
    g_b_ada = tot[:, P_DMOD:P_DMOD + 3 * D_MODEL]
    g_g_pre = tot[:, P_GPRE:P_GPRE + D_MODEL]
    g_g_post = tot[:, P_GPOST:P_GPOST + D_MODEL]
    g_b_fgate = tot[:, P_BF:P_BF + FOX_HEADS]
    g_sinks = tot[:, P_SINK:P_SINK + FOX_HEADS]
    dm_shard = lax.dynamic_slice_in_dim(parts[:, 0, :3 * D_MODEL], chip * W_ADA_SHARD, W_ADA_SHARD, axis=1)
    g_w_ada = _grad_w_ada(a_all.T, dm_shard)

    gin = jnp.pad(gw_in.reshape(D_MODEL, N_CHIPS, W_IN_SHARD).transpose(1, 0, 2),
                  ((0, 0), (0, 0), (0, W_IN_SHARD_PAD - W_IN_SHARD)))
    gout = gw_out.reshape(N_CHIPS, D_MODEL, W_OUT_SHARD)
    gbig = jnp.concatenate([gin, gout], axis=2)
    half = D_MODEL // 2
    gw = W_IN_SHARD_PAD + W_OUT_SHARD
    keep = lax.dynamic_slice_in_dim(gbig, core * half, half, axis=1)
    give = lax.dynamic_slice_in_dim(gbig, (1 - core) * half, half, axis=1)
    got = _swap_sibling(give.reshape(N_CHIPS * half, gw), "swap_grad_halves")
    pair = _add(keep.reshape(N_CHIPS * half, gw), got, "add_pair", BF).reshape(N_CHIPS, half, gw)
    from_chips = _exchange_chips(pair, "exchange_grad")
    mine = _sum_chips(from_chips, "sum_chips")
    other = _swap_sibling(mine, "swap_grad_result")
    lo = jnp.where(core == 0, mine, other)
    hi = jnp.where(core == 0, other, mine)
    gfull = jnp.concatenate([lo, hi], axis=0)
    g_w_in = gfull[:, :W_IN_SHARD]
    g_w_out = gfull[:, W_IN_SHARD_PAD:].reshape(W_OUT_SHARD, D_MODEL)

    grads = dict(w_ada=g_w_ada, b_ada=g_b_ada, g_pre=g_g_pre, w_in=g_w_in, b_fgate=g_b_fgate, sinks=g_sinks,
                 w_out=g_w_out, g_post=g_g_post)
    weights = dict(w_ada=w_ada, b_ada=b_ada, g_pre=g_pre, w_in=w_in, b_fgate=b_fgate, sinks=sinks, w_out=w_out, g_post=g_post)
    moms = dict(w_ada=m_w_ada, b_ada=m_b_ada, g_pre=m_g_pre, w_in=m_w_in, b_fgate=m_b_fgate, sinks=m_sinks, w_out=m_w_out, g_post=m_g_post)
    vars_ = dict(w_ada=v_w_ada, b_ada=v_b_ada, g_pre=v_g_pre, w_in=v_w_in, b_fgate=v_b_fgate, sinks=v_sinks, w_out=v_w_out, g_post=v_g_post)
    names = ["w_ada", "b_ada", "g_pre", "w_in", "b_fgate", "sinks", "w_out", "g_post"]
    g_out, d_out, m_out, v_out = [], [], [], []
    for n in names:
        g2 = grads[n].reshape(weights[n].shape[-2:])
        go, d, nm, nv = _adamw(weights[n], g2, moms[n], vars_[n], "adamw_" + n)
        g_out.append(go)
        d_out.append(d)
        m_out.append(nm)
        v_out.append(nv)
    return (loss, grad_x.reshape(x.shape), *g_out, *d_out, *m_out, *v_out)
```

```python
import functools

import jax
import jax.numpy as jnp
from jax import lax
from jax.experimental import pallas as pl
from jax.experimental.pallas import tpu as pltpu

_INTERPRET = False

D_MODEL = 1024
HEAD_DIM = 64
HALF = HEAD_DIM // 2
AUG_DIM = 128
AUG_ROWS = 8
VT_ROWS = 80
LOG2E = 1.4426950408889634
LN2 = 0.6931471805599453
Q_SCALE = LOG2E * 64 ** -0.5
FOX_HEADS = 8
FOX_W = 512
SWA_W = 512
SWA_KV_HEADS = 2
SWA_GROUP = 4
SWA_KV_W = 128
WINDOW = 128
ROPE_THETA = 10000.0
RMS_EPS = 1e-6
IN_WIDTH = 3336
N_CHIPS = 4
N_DEV = 8
W_IN_SHARD = IN_WIDTH // N_CHIPS
W_IN_SHARD_PAD = 896
W_ADA_SHARD = 3 * D_MODEL // N_CHIPS
W_OUT_SHARD = D_MODEL // N_CHIPS
LANES = 128

_SRC = dict(qa=0, ka=512, va=1024, fa=1536, za=1544, qb=2056, kb=2568, vb=2696, zb=2824)
C_QA, C_KA, C_VA, C_ZA, C_QB, C_ZB, C_KB, C_VB, C_F = 0, 512, 1024, 1536, 2048, 2560, 3072, 3200, 3328
WP = 3456

ADAM_LR = 0.001
ADAM_B1 = 0.9
ADAM_B2 = 0.999
ADAM_EPS = 1e-08
ADAM_WD = 0.01
ADAM_STEP = 10

VMEM_LIMIT = 56 * 1024 * 1024
NEG = -1e30
OVERFLOW_GUARD = 1e30
MESH = pl.DeviceIdType.MESH
BF = jnp.bfloat16
F32 = jnp.float32

P_DMOD, P_GPRE, P_GPOST, P_BF, P_SINK, P_LOSS, P_LEN = 0, 3072, 4096, 5120, 5248, 5376, 5504


def _call(body, **kw):
    return pl.pallas_call(body, interpret=_INTERPRET, **kw)


def _params(sem=None, **kw):
    return pltpu.CompilerParams(dimension_semantics=sem, vmem_limit_bytes=VMEM_LIMIT, **kw)


def _full(shape):
    zeros = (0,) * len(shape)
    return pl.BlockSpec(shape, lambda *_: zeros)


def _dot(a, b):
    return jnp.dot(a, b, preferred_element_type=F32)


def _dot_nt(a, b):
    return lax.dot_general(a, b, (((1,), (1,)), ((), ())), preferred_element_type=F32)


def _dot_tn(a, b):
    return lax.dot_general(a, b, (((0,), (0,)), ((), ())), preferred_element_type=F32)


def _sigmoid(z):
    return 1.0 / (1.0 + jnp.exp(-z))


def _rope_partner(t):
    w = t.shape[-1]
    lane = lax.broadcasted_iota(jnp.int32, t.shape, t.ndim - 1)
    return jnp.where((lane & (HEAD_DIM - 1)) < HALF, pltpu.roll(t, w - HALF, t.ndim - 1), pltpu.roll(t, HALF, t.ndim - 1))


def _allgather_devices(v, name):
    r, cdim = v.shape
    masks = [(dx, dy, dc) for dx in (0, 1) for dy in (0, 1) for dc in (0, 1)][1:]

    def body(v_ref, out_ref, send_sems, recv_sems):
        x, y, c = lax.axis_index("x"), lax.axis_index("y"), lax.axis_index("c")
        me = 4 * x + 2 * y + c
        out_ref[me] = v_ref[...]
        copies = []
        for k, (dx, dy, dc) in enumerate(masks):
            cp = pltpu.make_async_remote_copy(
                src_ref=v_ref, dst_ref=out_ref.at[me], send_sem=send_sems.at[k], recv_sem=recv_sems.at[k],
                device_id=(x ^ dx, y ^ dy, c ^ dc), device_id_type=MESH)
            cp.start()
            copies.append(cp)
        for k, (dx, dy, dc) in enumerate(masks):
            peer = 4 * (x ^ dx) + 2 * (y ^ dy) + (c ^ dc)
            pltpu.make_async_remote_copy(
                src_ref=v_ref, dst_ref=out_ref.at[peer], send_sem=send_sems.at[k], recv_sem=recv_sems.at[k],
                device_id=(x ^ dx, y ^ dy, c ^ dc), device_id_type=MESH).wait_recv()
        for cp in copies:
            cp.wait_send()

    return _call(
        body, name=name, out_shape=jax.ShapeDtypeStruct((N_DEV, r, cdim), v.dtype),
        in_specs=[pl.BlockSpec(memory_space=pltpu.VMEM)], out_specs=pl.BlockSpec(memory_space=pltpu.VMEM),
        scratch_shapes=[pltpu.SemaphoreType.DMA((7,)), pltpu.SemaphoreType.DMA((7,))],
        compiler_params=pltpu.CompilerParams(has_side_effects=True),
    )(v)


def _allgather_chips(v, name):
    _, r, cdim = v.shape
    masks = [(1, 0), (0, 1), (1, 1)]
    n = len(masks)

    def body(v_ref, out_ref, send_sems, recv_sems, local_sem):
        x, y, c = lax.axis_index("x"), lax.axis_index("y"), lax.axis_index("c")
        me = 2 * x + y
        mine = pltpu.make_async_copy(v_ref, out_ref.at[me], local_sem)
        mine.start()

        def copy(k, chip, half, to):
            return pltpu.make_async_remote_copy(
                src_ref=v_ref.at[half] if k < n else out_ref.at[chip, half], dst_ref=out_ref.at[chip, half],
                send_sem=send_sems.at[k], recv_sem=recv_sems.at[k], device_id=to, device_id_type=MESH)

        first = [copy(k, me, c, (x ^ dx, y ^ dy, c)) for k, (dx, dy) in enumerate(masks)]
        for cp in first:
            cp.start()
        passed = []
        for k, (dx, dy) in enumerate(masks):
            peer = 2 * (x ^ dx) + (y ^ dy)
            copy(k, peer, c, (x, y, c)).wait_recv()
            cp = copy(n + k, peer, c, (x, y, 1 - c))
            cp.start()
            passed.append(cp)
        for k, (dx, dy) in enumerate(masks):
            copy(n + k, 2 * (x ^ dx) + (y ^ dy), 1 - c, (x, y, c)).wait_recv()
        for cp in first + passed:
            cp.wait_send()
        mine.wait()

    return _call(
        body, name=name, out_shape=jax.ShapeDtypeStruct((N_CHIPS, 2, r, cdim), v.dtype),
        in_specs=[pl.BlockSpec(memory_space=pl.ANY)], out_specs=pl.BlockSpec(memory_space=pl.ANY),
        scratch_shapes=[pltpu.SemaphoreType.DMA((2 * n,)), pltpu.SemaphoreType.DMA((2 * n,)), pltpu.SemaphoreType.DMA],
        compiler_params=pltpu.CompilerParams(has_side_effects=True),
    )(v)


def _swap_sibling(v, name):
    def body(v_ref, out_ref, send_sem, recv_sem):
        x, y, c = lax.axis_index("x"), lax.axis_index("y"), lax.axis_index("c")
        cp = pltpu.make_async_remote_copy(
            src_ref=v_ref, dst_ref=out_ref, send_sem=send_sem, recv_sem=recv_sem,
            device_id=(x, y, 1 - c), device_id_type=MESH)
        cp.start()
        cp.wait()

    return _call(
        body, name=name, out_shape=jax.ShapeDtypeStruct(v.shape, v.dtype),
        in_specs=[pl.BlockSpec(memory_space=pl.ANY)], out_specs=pl.BlockSpec(memory_space=pl.ANY),
        scratch_shapes=[pltpu.SemaphoreType.DMA, pltpu.SemaphoreType.DMA],
        compiler_params=pltpu.CompilerParams(has_side_effects=True),
    )(v)


def _ada_shard(c_all, w_ada_shard):
    def body(c_ref, w_ref, a_ref, mod_ref):
        cv = c_ref[...]
        a = cv * _sigmoid(cv)
        a_ref[...] = a
        mod_ref[...] = _dot(a.astype(BF), w_ref[...].astype(BF))

    return _call(
        body, name="ada_shard",
        out_shape=(jax.ShapeDtypeStruct((N_DEV, D_MODEL), F32), jax.ShapeDtypeStruct((N_DEV, W_ADA_SHARD), F32)),
        compiler_params=_params(),
    )(c_all, w_ada_shard)


def _grad_w_ada(a_t, dm_shard):
    def body(a_ref, dm_ref, out_ref):
        acc = jnp.zeros((D_MODEL, W_ADA_SHARD), F32)
        for b in range(N_DEV):
            acc = acc + a_ref[:, b:b + 1] * dm_ref[b:b + 1, :]
        out_ref[...] = acc

    return _call(body, name="grad_w_ada", out_shape=jax.ShapeDtypeStruct((D_MODEL, W_ADA_SHARD), F32),
                 compiler_params=_params())(a_t, dm_shard)


def _sum_devices(parts):
    n = parts.shape[-1]

    def body(p_ref, out_ref):
        acc = p_ref[0]
        for b in range(1, N_DEV):
            acc = acc + p_ref[b]
        out_ref[...] = acc

    return _call(body, name="sum_devices", out_shape=jax.ShapeDtypeStruct((1, n), F32), compiler_params=_params())(parts)


def _add(a, b, name, out_dtype):
    r, cdim = a.shape
    tr = min(r, 256)

    def body(a_ref, b_ref, o_ref):
        o_ref[...] = (a_ref[...] + b_ref[...]).astype(out_dtype)

    spec = pl.BlockSpec((tr, cdim), lambda i: (i, 0))
    return _call(body, name=name, out_shape=jax.ShapeDtypeStruct(a.shape, out_dtype), grid=(r // tr,),
                 in_specs=[spec, spec], out_specs=spec, compiler_params=_params(("parallel",)))(a, b)


def _sum_chips(parts, name):
    _, r, cdim = parts.shape
    tr = min(r, 128)

    def body(p_ref, o_ref):
        o_ref[...] = ((p_ref[0].astype(F32) + p_ref[1].astype(F32)) + p_ref[2].astype(F32)) + p_ref[3].astype(F32)

    return _call(body, name=name, out_shape=jax.ShapeDtypeStruct((r, cdim), F32), grid=(r // tr,),
                 in_specs=[pl.BlockSpec((N_CHIPS, tr, cdim), lambda i: (0, i, 0))],
                 out_specs=pl.BlockSpec((tr, cdim), lambda i: (i, 0)), compiler_params=_params(("parallel",)))(parts)


def _adamw(w, g, m, v, name):
    r, cdim = w.shape[-2:]
    lead = w.ndim - 2
    tr = r if r <= 256 else 256
    c1 = 1.0 / (1.0 - ADAM_B1 ** ADAM_STEP)
    c2 = 1.0 / (1.0 - ADAM_B2 ** ADAM_STEP)

    def body(w_ref, g_ref, m_ref, v_ref, go_ref, d_ref, nm_ref, nv_ref):
        gv = g_ref[...].reshape(go_ref.shape)
        nm = ADAM_B1 * m_ref[...] + (1.0 - ADAM_B1) * gv
        nv = ADAM_B2 * v_ref[...] + (1.0 - ADAM_B2) * (gv * gv)
        m_hat = nm * c1
        v_hat = nv * c2
        go_ref[...] = gv
        d_ref[...] = -ADAM_LR * (m_hat / (jnp.sqrt(v_hat) + ADAM_EPS) + ADAM_WD * w_ref[...])
        nm_ref[...] = nm
        nv_ref[...] = nv

    spec = pl.BlockSpec((1,) * lead + (tr, cdim), lambda i: (0,) * lead + (i, 0))
    shp = jax.ShapeDtypeStruct(w.shape, F32)
    return _call(body, name=name, out_shape=(shp,) * 4, grid=(r // tr,),
                 in_specs=[spec, pl.BlockSpec((tr, cdim), lambda i: (i, 0)), spec, spec],
                 out_specs=(spec,) * 4, compiler_params=_params(("parallel",)))(w, g, m, v)


def _head_of_row(r, nc):
    assert nc & (nc - 1) == 0
    return lax.shift_right_logical(r, nc.bit_length() - 1)


def _chunk_mats(rows, nc, reverse):
    ri = lax.broadcasted_iota(jnp.int32, (rows, rows), 0)
    ci = lax.broadcasted_iota(jnp.int32, (rows, rows), 1)
    same = _head_of_row(ri, nc) == _head_of_row(ci, nc)
    between = jnp.where(same & ((ci > ri) if reverse else (ci < ri)), 1.0, 0.0).astype(F32)
    li = lax.broadcasted_iota(jnp.int32, (LANES, LANES), 0)
    lj = lax.broadcasted_iota(jnp.int32, (LANES, LANES), 1)
    within = jnp.where((li >= lj) if reverse else (li <= lj), 1.0, 0.0).astype(F32)
    return between, within


def _dot_hi(a, b):
    return jnp.dot(a, b, preferred_element_type=F32, precision=lax.Precision.HIGHEST)


def _scan_rows(t, nc, reverse):
    between, within = _chunk_mats(t.shape[0], nc, reverse)
    inner = _dot_hi(t, within)
    tot = jnp.sum(t, axis=1, keepdims=True)
    return inner + _dot_hi(between, jnp.broadcast_to(tot, t.shape))


def _log_forget_cumsum(f_rows, bias_rows, nc):
    def body(f_ref, b_ref, cum_ref):
        z = f_ref[...] + b_ref[...]
        lf = jnp.minimum(z, 0.0) - jnp.log(1.0 + jnp.exp(-jnp.abs(z)))
        cum_ref[...] = _scan_rows(lf, nc, False)

    return _call(body, name="forget_cumsum", out_shape=jax.ShapeDtypeStruct(f_rows.shape, F32),
                 compiler_params=_params())(f_rows, bias_rows)


def _log_forget_cumsum_bwd(dcum_rows, f_rows, bias_rows, nc):
    rows = f_rows.shape[0]

    def body(d_ref, f_ref, b_ref, df_ref, db_ref):
        dlf = _scan_rows(d_ref[...], nc, True)
        z = f_ref[...] + b_ref[...]
        df = dlf * _sigmoid(-z)
        df_ref[...] = df
        hi = lax.broadcasted_iota(jnp.int32, (FOX_HEADS, rows), 0)
        ri = lax.broadcasted_iota(jnp.int32, (FOX_HEADS, rows), 1)
        sel = jnp.where(_head_of_row(ri, nc) == hi, 1.0, 0.0).astype(F32)
        db_ref[...] = jnp.sum(_dot_hi(sel, df), axis=1, keepdims=True)

    return _call(body, name="forget_cumsum_bwd",
                 out_shape=(jax.ShapeDtypeStruct(f_rows.shape, F32), jax.ShapeDtypeStruct((FOX_HEADS, 1), F32)),
                 compiler_params=_params())(dcum_rows, f_rows, bias_rows)


def _rms_hat(xv):
    rstd = lax.rsqrt(jnp.mean(xv * xv, axis=-1, keepdims=True) + RMS_EPS)
    return xv * rstd, rstd


def _modulated(x_ref, g_ref, sc_ref, sh_ref):
    xhat, _ = _rms_hat(x_ref[...])
    return ((xhat * g_ref[...]) * sc_ref[...] + sh_ref[...]).astype(BF)


def _forget_logits(x, g_pre, scale1p, shift, w_f, tm):
    s = x.shape[0]

    def body(x_ref, g_ref, sc_ref, sh_ref, w_ref, f_ref):
        f_ref[...] = _dot(_modulated(x_ref, g_ref, sc_ref, sh_ref), w_ref[...])

    vec = _full((1, D_MODEL))
    return _call(
        body, name="forget_logits", out_shape=jax.ShapeDtypeStruct((s, LANES), F32), grid=(s // tm,),
        in_specs=[pl.BlockSpec((tm, D_MODEL), lambda i: (i, 0)), vec, vec, vec, _full((D_MODEL, LANES))],
        out_specs=pl.BlockSpec((tm, LANES), lambda i: (i, 0)), compiler_params=_params(("parallel",)),
    )(x, g_pre, scale1p, shift, w_f)


def _split3(v):
    hi = v.astype(BF).astype(F32)
    mid = (v - hi).astype(BF).astype(F32)
    lo = ((v - hi) - mid).astype(BF).astype(F32)
    return hi, mid, lo


def _in_proj(x, g_pre, scale1p, shift, w_rows, w_t_fox, cum, cos_t, sin_t, tm):
    s = x.shape[0]
    r_va, r_za, r_qb, r_zb, r_kb, r_vb = 0, 512, 1024, 1536, 2048, 2176

    def body(x_ref, g_ref, sc_ref, sh_ref, w_ref, wt_ref, cum_ref, cos_ref, sin_ref,
             h_ref, qat_ref, ka_ref, kat_ref, v_ref, vt_ref, za_ref, zb_ref, qb_ref, kb_ref, vb_ref,
             qbt_ref, kbt_ref, vbt_ref, mo_ref):
        hb = _modulated(x_ref, g_ref, sc_ref, sh_ref)
        h_ref[...] = hb

        def sec(c0, width):
            return _dot(hb, w_ref[:, c0:c0 + width])

        def sec_t(r0):
            return _dot_nt(wt_ref[r0:r0 + FOX_W, :], hb)

        q_t = sec_t(0) * Q_SCALE
        k_t = sec_t(FOX_W)
        v_t = sec_t(2 * FOX_W)
        va = sec(r_va, FOX_W)
        zeros = jnp.zeros((AUG_DIM - HEAD_DIM - AUG_ROWS, tm), F32)
        ri = lax.broadcasted_iota(jnp.int32, (AUG_ROWS, tm), 0)
        const = jnp.where(ri == AUG_ROWS - 1, 0.0, 1.0)
        ri_v = lax.broadcasted_iota(jnp.int32, (VT_ROWS - HEAD_DIM, tm), 0)
        v_feat = jnp.where(ri_v == 0, 1.0, 0.0).astype(BF)
        for hd in range(FOX_HEADS):
            rows = slice(hd * HEAD_DIM, (hd + 1) * HEAD_DIM)
            cum2 = cum_ref[hd:hd + 1, :] * LOG2E
            hi, mid, lo = (jnp.broadcast_to(part, (AUG_ROWS, tm)) for part in _split3(cum2))
            q_feat = jnp.where(ri == 1, hi, jnp.where(ri == 2, mid, jnp.where(ri == 3, lo, const)))
            k_feat = jnp.where(ri == 4, -hi, jnp.where(ri == 5, -mid, jnp.where(ri == 6, -lo, const)))
            q_aug = jnp.concatenate([q_t[rows], q_feat, zeros], axis=0)
            k_aug = jnp.concatenate([k_t[rows], k_feat, zeros], axis=0)
            mo_ref[hd:hd + 1, :] = jnp.sum(q_t[rows] * k_t[rows], axis=0, keepdims=True) + 1.0
            qat_ref[hd] = q_aug.astype(BF)
            kat_ref[hd] = k_aug.astype(BF)
            ka_ref[hd] = k_aug.T.astype(BF)
            vt_ref[hd] = jnp.concatenate([v_t[rows].astype(BF), v_feat], axis=0)
            v_ref[hd] = va[:, rows].astype(BF)
        za_ref[...] = sec(r_za, FOX_W)
        zb_ref[...] = sec(r_zb, SWA_W)
        cos2, sin2 = cos_ref[...], sin_ref[...]
        cos8 = jnp.concatenate([cos2] * 4, axis=1)
        sin8 = jnp.concatenate([sin2] * 4, axis=1)
        qb = sec(r_qb, SWA_W)
        qb = (qb * cos8 + _rope_partner(qb) * sin8) * (HEAD_DIM ** -0.5)
        qb_ref[...] = qb.astype(BF)
        for a in range(SWA_W // LANES):
            qbt_ref[a * LANES:(a + 1) * LANES, :] = qb[:, a * LANES:(a + 1) * LANES].T.astype(BF)
        kb = sec(r_kb, SWA_KV_W)
        kb = kb * cos2 + _rope_partner(kb) * sin2
        vb = sec(r_vb, SWA_KV_W)
        kb_t, vb_t = kb.T, vb.T
        for hd in range(SWA_KV_HEADS):
            sl = slice(hd * HEAD_DIM, (hd + 1) * HEAD_DIM)
            kb_ref[hd] = kb[:, sl].astype(BF)
            vb_ref[hd] = vb[:, sl].astype(BF)
            kbt_ref[hd] = kb_t[sl].astype(BF)
            vbt_ref[hd] = jnp.concatenate([vb_t[sl].astype(BF), v_feat], axis=0)

    row = lambda w: pl.BlockSpec((tm, w), lambda i: (i, 0))
    heads = lambda n, w=HEAD_DIM: pl.BlockSpec((n, tm, w), lambda i: (0, i, 0))
    heads_t = lambda w: pl.BlockSpec((FOX_HEADS, w, tm), lambda i: (0, 0, i))
    vec = _full((1, D_MODEL))
    hs = lambda a, b: jax.ShapeDtypeStruct((FOX_HEADS, a, b), BF)
    out_shape = (
        jax.ShapeDtypeStruct((s, D_MODEL), BF),
        hs(AUG_DIM, s), hs(s, AUG_DIM), hs(AUG_DIM, s), hs(s, HEAD_DIM), hs(VT_ROWS, s),
        jax.ShapeDtypeStruct((s, FOX_W), F32), jax.ShapeDtypeStruct((s, SWA_W), F32),
        jax.ShapeDtypeStruct((s, SWA_W), BF),
        jax.ShapeDtypeStruct((SWA_KV_HEADS, s, HEAD_DIM), BF), jax.ShapeDtypeStruct((SWA_KV_HEADS, s, HEAD_DIM), BF),
        jax.ShapeDtypeStruct((SWA_W, s), BF),
        jax.ShapeDtypeStruct((SWA_KV_HEADS, HEAD_DIM, s), BF), jax.ShapeDtypeStruct((SWA_KV_HEADS, VT_ROWS, s), BF),
        jax.ShapeDtypeStruct((FOX_HEADS, s), F32),
    )
    kv_t = lambda w: pl.BlockSpec((SWA_KV_HEADS, w, tm), lambda i: (0, 0, i))
    return _call(
        body, name="in_proj", out_shape=out_shape, grid=(s // tm,),
        in_specs=[row(D_MODEL), vec, vec, vec, _full(w_rows.shape), _full(w_t_fox.shape),
                  pl.BlockSpec((FOX_HEADS, tm), lambda i: (0, i)), row(LANES), row(LANES)],
        out_specs=(row(D_MODEL), heads_t(AUG_DIM), heads(FOX_HEADS, AUG_DIM), heads_t(AUG_DIM), heads(FOX_HEADS),
                   heads_t(VT_ROWS), row(FOX_W), row(SWA_W), row(SWA_W), heads(SWA_KV_HEADS), heads(SWA_KV_HEADS),
                   pl.BlockSpec((SWA_W, tm), lambda i: (0, i)), kv_t(HEAD_DIM), kv_t(VT_ROWS),
                   pl.BlockSpec((FOX_HEADS, tm), lambda i: (0, i))),
        compiler_params=_params(("parallel",)),
    )(x, g_pre, scale1p, shift, w_rows, w_t_fox, cum, cos_t, sin_t)


def _diag_chunks(d, bq, bk, chunk):
    out = []
    for c0 in range(0, bq, chunk):
        if d is None or d * bk + bk - 1 <= c0:
            out.append((c0, None))
        elif d * bk <= c0 + chunk - 1:
            kpos = d * bk + lax.broadcasted_iota(jnp.int32, (bk, chunk), 0)
            qpos = c0 + lax.broadcasted_iota(jnp.int32, (bk, chunk), 1)
            out.append((c0, kpos <= qpos))
    return out


def _fox_fwd(qat, ka, vt, m_own, bq, bk, chunk, running_max):
    nh, _, s = qat.shape
    r = bq // bk

    def body(ka_ref, qat_ref, vt_ref, mo_ref, o_ref, lse_ref, bad_ref, *rest):
        pt_ref, m_scr, acc_scr = (None,) * running_max + rest
        i, j = pl.program_id(1), pl.program_id(2)

        @pl.when(j == 0)
        def _():
            m_scr[...] = jnp.full(m_scr.shape, NEG, F32) if running_max else mo_ref[0]
            acc_scr[...] = jnp.zeros(acc_scr.shape, F32)

        def careful(d):
            kv, vtv = ka_ref[0], vt_ref[0]

            def one_chunk(n, carry):
                c0 = pl.multiple_of(n * chunk, chunk)
                cs = pl.ds(c0, chunk)
                sc = _dot(kv, qat_ref[0, :, cs])
                if d is not None:
                    kpos = d * bk + lax.broadcasted_iota(jnp.int32, (bk, chunk), 0)
                    qpos = c0 + lax.broadcasted_iota(jnp.int32, (bk, chunk), 1)
                    sc = jnp.where(kpos <= qpos, sc, NEG)
                m_prev = m_scr[:, cs]
                m_new = jnp.maximum(m_prev, jnp.max(sc, axis=0, keepdims=True))
                p = jnp.exp2(sc - m_new).astype(BF)
                acc_scr[:, cs] = jnp.exp2(m_prev - m_new) * acc_scr[:, cs] + _dot(vtv, p)
                m_scr[:, cs] = m_new
                return carry

            lax.fori_loop(0, bq // chunk, one_chunk, 0)

        def fast(d):
            kv, vtv = ka_ref[0], vt_ref[0]
            todo = _diag_chunks(d, bq, bk, chunk)
            scores = lambda c0: _dot(kv, qat_ref[0, :, c0:c0 + chunk])
            sc_next = scores(todo[0][0])
            for n, (c0, mask) in enumerate(todo):
                cs = slice(c0, c0 + chunk)
                sc = sc_next
                if n + 1 < len(todo):
                    sc_next = scores(todo[n + 1][0])
                if mask is not None:
                    sc = jnp.where(mask, sc, NEG)
                p = jnp.exp2(sc - m_scr[:, cs]).astype(BF)
                pt_ref[0, :, cs] = p
                acc_scr[:, cs] += _dot(vtv, p)

        step = careful if running_max else fast

        @pl.when(j < i * r)
        def _():
            step(None)

        for d in range(r):
            @pl.when(j == i * r + d)
            def _(d=d):
                step(d)

        @pl.when(j == i * r + r - 1)
        def _():
            l = acc_scr[HEAD_DIM:HEAD_DIM + 1, :]
            o_ref[0] = acc_scr[:HEAD_DIM, :] / l
            lse_ref[0] = m_scr[...] + jnp.log2(l)
            bad_ref[0] = jnp.where(l < OVERFLOW_GUARD, 0.0, 1.0)

    kmap = lambda h, i, j: (h, jnp.minimum(j, i * r + r - 1), 0)
    kmap_t = lambda h, i, j: (h, 0, jnp.minimum(j, i * r + r - 1))
    qrow = pl.BlockSpec((1, 1, bq), lambda h, i, j: (h, 0, i))
    row_shape = jax.ShapeDtypeStruct((nh, 1, s), F32)
    out_shape = (jax.ShapeDtypeStruct((nh, HEAD_DIM, s), F32), row_shape, row_shape)
    out_specs = (pl.BlockSpec((1, HEAD_DIM, bq), lambda h, i, j: (h, 0, i)), qrow, qrow)
    if not running_max:
        out_shape += (jax.ShapeDtypeStruct((nh, s, s), BF),)
        out_specs += (pl.BlockSpec((1, bk, bq), lambda h, i, j: (h, jnp.minimum(j, i * r + r - 1), i)),)
    return _call(
        body, name="fox_fwd_running_max" if running_max else "fox_fwd",
        out_shape=out_shape,
        grid=(nh, s // bq, s // bk),
        in_specs=[pl.BlockSpec((1, bk, AUG_DIM), kmap), pl.BlockSpec((1, AUG_DIM, bq), lambda h, i, j: (h, 0, i)),
                  pl.BlockSpec((1, VT_ROWS, bk), kmap_t), qrow],
        out_specs=out_specs,
        scratch_shapes=[pltpu.VMEM((1, bq), F32), pltpu.VMEM((VT_ROWS, bq), F32)],
        compiler_params=_params(("parallel", "parallel", "arbitrary")),
    )(ka, qat, vt, m_own)


def _fox_bwd(qat, ka, kat, v, dot_, lse, delta, pt, bq, bk, chunk, dq_blk):
    nh, _, s = qat.shape
    r = bq // bk
    nq = s // bq
    stored = pt is not None

    def body(a_ref, b_ref, kat_ref, v_ref, qat_ref, do_ref, dl_ref, dq_ref, dk_ref, dv_ref, dk_scr, dv_scr):
        ka_ref, lse_ref, pt_ref = (None, None, a_ref) if stored else (a_ref, b_ref, None)
        j, i = pl.program_id(1), pl.program_id(2)

        @pl.when((j == 0) & (i == 0))
        def _():
            dq_ref[...] = jnp.zeros(dq_ref.shape, F32)

        @pl.when(i == 0)
        def _():
            dk_scr[...] = jnp.zeros(dk_scr.shape, F32)
            dv_scr[...] = jnp.zeros(dv_scr.shape, F32)

        def step(d):
            ktv, vv = kat_ref[0], v_ref[0]
            kv = None if stored else ka_ref[0]
            todo = _diag_chunks(d, bq, bk, chunk)

            def products(c0):
                cs = slice(c0, c0 + chunk)
                return None if stored else _dot(kv, qat_ref[0, :, cs]), _dot(vv, do_ref[0, :, cs])

            nxt = products(todo[0][0])
            for n, (c0, mask) in enumerate(todo):
                cs = slice(c0, c0 + chunk)
                sc, dp = nxt
                if n + 1 < len(todo):
                    nxt = products(todo[n + 1][0])
                if stored:
                    p_bf = pt_ref[0, :, cs]
                    p = p_bf.astype(F32)
                else:
                    p = jnp.exp2(sc - lse_ref[0, :, cs])
                    if mask is not None:
                        p = jnp.where(mask, p, 0.0)
                    p_bf = p.astype(BF)
                ds = (p * (dp - dl_ref[0, :, cs])).astype(BF)
                dv_scr[...] += _dot_nt(do_ref[0, :, cs], p_bf)
                dk_scr[:VT_ROWS, :] += _dot_nt(qat_ref[0, :VT_ROWS, cs], ds)
                c1 = c0 % dq_blk
                dq_ref[0, i * (bq // dq_blk) + c0 // dq_blk, :VT_ROWS, c1:c1 + chunk] += _dot(ktv[:VT_ROWS], ds)

        @pl.when(i * r > j)
        def _():
            step(None)

        for d in range(r):
            @pl.when(j == i * r + d)
            def _(d=d):
                step(d)

        @pl.when(i == nq - 1)
        def _():
            dk_ref[0] = dk_scr[...]
            dv_ref[0] = dv_scr[...]

    qmap = lambda h, j, i: (h, 0, jnp.maximum(i, j // r))
    kmap_t = lambda h, j, i: (h, 0, j)
    if stored:
        first = [(pt, pl.BlockSpec((1, bk, bq), lambda h, j, i: (h, j, jnp.maximum(i, j // r)))),
                 (delta, pl.BlockSpec((1, 1, bq), qmap))]
    else:
        first = [(ka, pl.BlockSpec((1, bk, AUG_DIM), lambda h, j, i: (h, j, 0))), (lse, pl.BlockSpec((1, 1, bq), qmap))]
    return _call(
        body, name="fox_bwd" if stored else "fox_bwd_recompute",
        out_shape=(jax.ShapeDtypeStruct((nh, s // dq_blk, AUG_DIM, dq_blk), F32),
                   jax.ShapeDtypeStruct((nh, AUG_DIM, s), F32), jax.ShapeDtypeStruct((nh, HEAD_DIM, s), F32)),
        grid=(nh, s // bk, nq),
        in_specs=[first[0][1], first[1][1], pl.BlockSpec((1, AUG_DIM, bk), kmap_t),
                  pl.BlockSpec((1, bk, HEAD_DIM), lambda h, j, i: (h, j, 0)),
                  pl.BlockSpec((1, AUG_DIM, bq), qmap), pl.BlockSpec((1, HEAD_DIM, bq), qmap),
                  pl.BlockSpec((1, 1, bq), qmap)],
        out_specs=(pl.BlockSpec((1, s // dq_blk, AUG_DIM, dq_blk), lambda h, j, i: (h, 0, 0, 0)),
                   pl.BlockSpec((1, AUG_DIM, bk), kmap_t), pl.BlockSpec((1, HEAD_DIM, bk), kmap_t)),
        scratch_shapes=[pltpu.VMEM((AUG_DIM, bk), F32), pltpu.VMEM((HEAD_DIM, bk), F32)],
        compiler_params=_params(("parallel", "arbitrary", "arbitrary")),
    )(first[0][0], first[1][0], kat, v, qat, dot_, delta)


def _swa_mask(i, tq):
    kpos = i * tq - WINDOW + lax.broadcasted_iota(jnp.int32, (tq + WINDOW, tq), 0)
    qpos = i * tq + lax.broadcasted_iota(jnp.int32, (tq + WINDOW, tq), 1)
    rel = qpos - kpos
    return (rel >= 0) & (rel < WINDOW) & (kpos >= 0)


def _swa_rows(ref, i, tq):
    before = pl.multiple_of(jnp.maximum(i * tq - WINDOW, 0), WINDOW)
    return jnp.concatenate([ref[0, pl.ds(before, WINDOW), :], ref[0, pl.ds(pl.multiple_of(i * tq, tq), tq), :]], axis=0)


def _swa_before(n_rows, tq):
    return pl.BlockSpec((1, n_rows, WINDOW), lambda g, i: (g, 0, jnp.maximum(i * (tq // WINDOW) - 1, 0)))


def _swa_probs_t(kw, q_t, mask, sink):
    sc = jnp.where(mask, _dot(kw, q_t), NEG)
    m = jnp.maximum(jnp.max(sc, axis=0, keepdims=True), sink)
    p = jnp.exp(sc - m)
    e_sink = jnp.exp(sink - m)
    inv_l = 1.0 / (jnp.sum(p, axis=0, keepdims=True) + e_sink)
    return p * inv_l, e_sink * inv_l


def _swa_fwd(qbt, kb, vbt, sinks, tq):
    s = qbt.shape[1]
    gw = SWA_GROUP * HEAD_DIM

    def body(q_ref, k_ref, vb_ref, vc_ref, s_ref, o_ref):
        i = pl.program_id(1)
        mask = _swa_mask(i, tq)
        kw = _swa_rows(k_ref, i, tq)
        vtw = jnp.concatenate([vb_ref[0], vc_ref[0]], axis=1)
        sk = s_ref[0]
        for hh in range(SWA_GROUP):
            rows = slice(hh * HEAD_DIM, (hh + 1) * HEAD_DIM)
            sink = sk[:, hh:hh + 1]
            sc = jnp.where(mask, _dot(kw, q_ref[rows, :]), NEG)
            m = jnp.maximum(jnp.max(sc, axis=0, keepdims=True), sink)
            acc = _dot(vtw, jnp.exp(sc - m).astype(BF))
            o_ref[rows, :] = acc[:HEAD_DIM] / (acc[HEAD_DIM:HEAD_DIM + 1] + jnp.exp(sink - m))

    kvspec = pl.BlockSpec((1, s, HEAD_DIM), lambda g, i: (g, 0, 0))
    qspec = pl.BlockSpec((gw, tq), lambda g, i: (g, i))
    return _call(
        body, name="swa_fwd", out_shape=jax.ShapeDtypeStruct((SWA_W, s), F32), grid=(SWA_KV_HEADS, s // tq),
        in_specs=[qspec, kvspec, _swa_before(VT_ROWS, tq), pl.BlockSpec((1, VT_ROWS, tq), lambda g, i: (g, 0, i)),
                  pl.BlockSpec((1, 1, SWA_GROUP), lambda g, i: (g, 0, 0))],
        out_specs=qspec, compiler_params=_params(("parallel", "parallel")),
    )(qbt, kb, vbt, vbt, sinks)


def _swa_bwd(qb, qbt, kb, kbt, vb, sinks, dob, dobt, tq):
    s = qb.shape[0]
    gw = SWA_GROUP * HEAD_DIM

    def body(q_ref, qt_ref, k_ref, ktb_ref, ktc_ref, v_ref, s_ref, do_ref, dot_ref, dq_ref, dk_ref, dv_ref, ds_ref):
        i = pl.program_id(1)

        @pl.when(i == 0)
        def _():
            dk_ref[...] = jnp.zeros(dk_ref.shape, F32)
            dv_ref[...] = jnp.zeros(dv_ref.shape, F32)
            ds_ref[...] = jnp.zeros(ds_ref.shape, F32)

        mask = _swa_mask(i, tq)
        kw = _swa_rows(k_ref, i, tq)
        vw = _swa_rows(v_ref, i, tq)
        ktw = jnp.concatenate([ktb_ref[0], ktc_ref[0]], axis=1)
        qv, dov = q_ref[...], do_ref[...]
        sk = s_ref[0]
        dsinks = []
        dk_acc = jnp.zeros((tq + WINDOW, HEAD_DIM), F32)
        dv_acc = jnp.zeros((tq + WINDOW, HEAD_DIM), F32)
        for hh in range(SWA_GROUP):
            rows = slice(hh * HEAD_DIM, (hh + 1) * HEAD_DIM)
            p, p_sink = _swa_probs_t(kw, qt_ref[rows, :], mask, sk[:, hh:hh + 1])
            dp = _dot(vw, dot_ref[rows, :])
            delta = jnp.sum(p * dp, axis=0, keepdims=True)
            dsc = (p * (dp - delta)).astype(BF)
            dq_ref[rows, :] = _dot(ktw, dsc)
            dk_acc = dk_acc + _dot(dsc, qv[:, rows])
            dv_acc = dv_acc + _dot(p.astype(BF), dov[:, rows])
            dsinks.append(-jnp.sum(p_sink * delta, axis=1, keepdims=True))
        before = pl.ds(pl.multiple_of(jnp.maximum(i * tq - WINDOW, 0), WINDOW), WINDOW)
        own = pl.ds(pl.multiple_of(i * tq, tq), tq)
        dk_ref[0, before, :] += dk_acc[:WINDOW]
        dk_ref[0, own, :] += dk_acc[WINDOW:]
        dv_ref[0, before, :] += dv_acc[:WINDOW]
        dv_ref[0, own, :] += dv_acc[WINDOW:]
        ds_ref[0] += jnp.concatenate(dsinks, axis=1)

    kvspec = pl.BlockSpec((1, s, HEAD_DIM), lambda g, i: (g, 0, 0))
    qspec = pl.BlockSpec((tq, gw), lambda g, i: (i, g))
    qspec_t = pl.BlockSpec((gw, tq), lambda g, i: (g, i))
    sspec = pl.BlockSpec((1, 1, SWA_GROUP), lambda g, i: (g, 0, 0))
    kvshape = jax.ShapeDtypeStruct((SWA_KV_HEADS, s, HEAD_DIM), F32)
    return _call(
        body, name="swa_bwd",
        out_shape=(jax.ShapeDtypeStruct((SWA_W, s), F32), kvshape, kvshape,
                   jax.ShapeDtypeStruct((SWA_KV_HEADS, 1, SWA_GROUP), F32)),
        grid=(SWA_KV_HEADS, s // tq),
        in_specs=[qspec, qspec_t, kvspec, _swa_before(HEAD_DIM, tq),
                  pl.BlockSpec((1, HEAD_DIM, tq), lambda g, i: (g, 0, i)), kvspec, sspec, qspec, qspec_t],
        out_specs=(qspec_t, kvspec, kvspec, sspec),
        compiler_params=_params(("parallel", "arbitrary")),
    )(qb, qbt, kb, kbt, kbt, vb, sinks, dob, dobt)


def _pairs_to_rows(ref, n_rows=HEAD_DIM):
    parts = []
    for a in range(0, FOX_HEADS, 2):
        parts.append(jnp.concatenate([ref[a][:n_rows], ref[a + 1][:n_rows]], axis=0).T)
    return jnp.concatenate(parts, axis=1)


def _blocks_to_rows(ref):
    return jnp.concatenate([ref[a:a + LANES, :].T for a in range(0, ref.shape[0], LANES)], axis=1)


def _out_proj(oat, za, obt, zb, x, tgt, w_out, w_out_t, gate, g_post, inv_l, tm):
    s = x.shape[0]

    def body(oat_ref, za_ref, obt_ref, zb_ref, x_ref, t_ref, w_ref, wt_ref, gate_ref, gp_ref, il_ref,
             dout_ref, doat_ref, dla_ref, dza_ref, dob_ref, dobt_ref, dzb_ref, gw_ref, dgate_ref, dgp_ref, loss_ref):
        i = pl.program_id(0)

        @pl.when(i == 0)
        def _():
            gw_ref[...] = jnp.zeros(gw_ref.shape, F32)
            dgate_ref[...] = jnp.zeros(dgate_ref.shape, F32)
            dgp_ref[...] = jnp.zeros(dgp_ref.shape, F32)
            loss_ref[...] = jnp.zeros(loss_ref.shape, F32)

        oa_v = _pairs_to_rows(oat_ref)
        ob_v = _blocks_to_rows(obt_ref)
        za_v, zb_v = za_ref[...], zb_ref[...]
        sga, sgb = _sigmoid(za_v), _sigmoid(zb_v)
        sila, silb = za_v * sga, zb_v * sgb
        u = jnp.concatenate([oa_v * sila, ob_v * silb], axis=1).astype(BF)
        yv = _dot(u, w_ref[...])
        yhat, rstd = _rms_hat(yv)
        gp, gate_v = gp_ref[...], gate_ref[...]
        nrm = yhat * gp
        diff = (x_ref[...] + gate_v * nrm) - t_ref[...]
        loss_ref[...] += 0.5 * jnp.sum(jnp.sum(diff * diff, axis=1, keepdims=True), axis=0, keepdims=True) / D_MODEL
        dout = diff * (1.0 / D_MODEL)
        dout_ref[...] = dout
        dgate_ref[...] += jnp.sum(dout * nrm, axis=0, keepdims=True)
        dn = dout * gate_v
        dgp_ref[...] += jnp.sum(dn * yhat, axis=0, keepdims=True)
        dyhat = dn * gp
        dy = (rstd * (dyhat - yhat * jnp.mean(dyhat * yhat, axis=1, keepdims=True))).astype(BF)
        gw_ref[...] += _dot_tn(u, dy)
        du = _dot(dy, wt_ref[...])
        dua, dub = du[:, :FOX_W], du[:, FOX_W:]
        doa = dua * sila
        for a in range(0, FOX_HEADS, 2):
            pair_t = doa[:, a * HEAD_DIM:(a + 2) * HEAD_DIM].T
            for hd, rows in ((a, slice(0, HEAD_DIM)), (a + 1, slice(HEAD_DIM, 2 * HEAD_DIM))):
                inv_l = il_ref[hd]
                doat_ref[hd] = (pair_t[rows] * inv_l).astype(BF)
                dla_ref[hd] = jnp.sum(pair_t[rows] * oat_ref[hd], axis=0, keepdims=True) * inv_l
        dob = dub * silb
        dob_ref[...] = dob.astype(BF)
        for a in range(0, SWA_W, LANES):
            dobt_ref[a:a + LANES, :] = dob[:, a:a + LANES].T.astype(BF)
        dza_ref[...] = (dua * oa_v * (sga * (1.0 + za_v * (1.0 - sga)))).astype(BF)
        dzb_ref[...] = (dub * ob_v * (sgb * (1.0 + zb_v * (1.0 - sgb)))).astype(BF)

    row = lambda w: pl.BlockSpec((tm, w), lambda i: (i, 0))
    heads_t = lambda w: pl.BlockSpec((FOX_HEADS, w, tm), lambda i: (0, 0, i))
    vec = _full((1, D_MODEL))
    mat = _full((D_MODEL, D_MODEL))
    out_shape = (
        jax.ShapeDtypeStruct((s, D_MODEL), F32),
        jax.ShapeDtypeStruct((FOX_HEADS, HEAD_DIM, s), BF), jax.ShapeDtypeStruct((FOX_HEADS, 1, s), F32),
        jax.ShapeDtypeStruct((s, FOX_W), BF), jax.ShapeDtypeStruct((s, SWA_W), BF), jax.ShapeDtypeStruct((SWA_W, s), BF),
        jax.ShapeDtypeStruct((s, SWA_W), BF),
        jax.ShapeDtypeStruct((D_MODEL, D_MODEL), F32),
        jax.ShapeDtypeStruct((1, D_MODEL), F32), jax.ShapeDtypeStruct((1, D_MODEL), F32),
        jax.ShapeDtypeStruct((1, 1), F32),
    )
    col = pl.BlockSpec((SWA_W, tm), lambda i: (0, i))
    return _call(
        body, name="out_proj", out_shape=out_shape, grid=(s // tm,),
        in_specs=[heads_t(HEAD_DIM), row(FOX_W), col, row(SWA_W), row(D_MODEL), row(D_MODEL), mat, mat, vec, vec,
                  heads_t(1)],
        out_specs=(row(D_MODEL), heads_t(HEAD_DIM), heads_t(1), row(FOX_W), row(SWA_W), col, row(SWA_W), mat, vec, vec,
                   _full((1, 1))),
        compiler_params=_params(("arbitrary",)),
    )(oat, za, obt, zb, x, tgt, w_out, w_out_t, gate, g_post, inv_l)


def _assemble_dproj(dqt, dkt, dvt, dza, dqb, dzb, dkb, dvb, df, cos_t, sin_t, tm):
    s = dza.shape[0]

    def body(dqt_ref, dkt_ref, dvt_ref, dza_ref, dqb_ref, dzb_ref, dkb_ref, dvb_ref, df_ref, cos_ref, sin_ref, o_ref):
        def cat(ref, n):
            return jnp.concatenate([ref[hd] for hd in range(n)], axis=1)

        cos2, sin2 = cos_ref[...], sin_ref[...]
        cos8 = jnp.concatenate([cos2] * 4, axis=1)
        sin8 = jnp.concatenate([sin2] * 4, axis=1)
        scale = HEAD_DIM ** -0.5
        o_ref[:, C_QA:C_QA + FOX_W] = (_pairs_to_rows(dqt_ref.at[:, 0]) * scale).astype(BF)
        o_ref[:, C_KA:C_KA + FOX_W] = (_pairs_to_rows(dkt_ref) * LN2).astype(BF)
        o_ref[:, C_VA:C_VA + FOX_W] = _pairs_to_rows(dvt_ref).astype(BF)
        o_ref[:, C_ZA:C_ZA + FOX_W] = dza_ref[...]
        dq = _blocks_to_rows(dqb_ref) * scale
        o_ref[:, C_QB:C_QB + SWA_W] = (dq * cos8 - _rope_partner(dq) * sin8).astype(BF)
        o_ref[:, C_ZB:C_ZB + SWA_W] = dzb_ref[...]
        dk = cat(dkb_ref, SWA_KV_HEADS)
        o_ref[:, C_KB:C_KB + SWA_KV_W] = (dk * cos2 - _rope_partner(dk) * sin2).astype(BF)
        o_ref[:, C_VB:C_VB + SWA_KV_W] = cat(dvb_ref, SWA_KV_HEADS).astype(BF)
        o_ref[:, C_F:C_F + LANES] = df_ref[...].astype(BF)

    row = lambda w: pl.BlockSpec((tm, w), lambda i: (i, 0))
    heads = lambda n: pl.BlockSpec((n, tm, HEAD_DIM), lambda i: (0, i, 0))
    heads_t = lambda w: pl.BlockSpec((FOX_HEADS, w, tm), lambda i: (0, 0, i))
    return _call(
        body, name="assemble_dproj", out_shape=jax.ShapeDtypeStruct((s, WP), BF), grid=(s // tm,),
        in_specs=[pl.BlockSpec((FOX_HEADS, 1, AUG_DIM, tm), lambda i: (0, i, 0, 0)), heads_t(AUG_DIM), heads_t(HEAD_DIM),
                  row(FOX_W), pl.BlockSpec((SWA_W, tm), lambda i: (0, i)), row(SWA_W), heads(SWA_KV_HEADS),
                  heads(SWA_KV_HEADS), row(LANES), row(LANES), row(LANES)],
        out_specs=row(WP), compiler_params=_params(("parallel",)),
    )(dqt, dkt, dvt, dza, dqb, dzb, dkb, dvb, df, cos_t, sin_t)


def _in_proj_bwd_x(dproj, w_al_t, x, dout, g_pre, scale1p, tm, parts):
    s = x.shape[0]
    n_steps = s // tm
    masks = [(1, 0), (0, 1), (1, 1)]

    def body(dp_ref, wt_ref, x_ref, dout_ref, g_ref, sc_ref, parts_ref, gx_ref, dsh_ref, dsc_ref, dg_ref, got_ref,
             send_sems, recv_sems, local_sem):
        i = pl.program_id(0)
        cx, cy, cc = lax.axis_index("x"), lax.axis_index("y"), lax.axis_index("c")
        me = 2 * cx + cy
        own = pltpu.make_async_copy(parts_ref.at[me], got_ref.at[me], local_sem)

        def copy(k, send):
            dx, dy = masks[k]
            peer = 2 * (cx ^ dx) + (cy ^ dy)
            return pltpu.make_async_remote_copy(
                src_ref=parts_ref.at[peer if send else me], dst_ref=got_ref.at[me if send else peer],
                send_sem=send_sems.at[k], recv_sem=recv_sems.at[k], device_id=(cx ^ dx, cy ^ dy, cc), device_id_type=MESH)

        @pl.when(i == 0)
        def _():
            dsh_ref[...] = jnp.zeros(dsh_ref.shape, F32)
            dsc_ref[...] = jnp.zeros(dsc_ref.shape, F32)
            dg_ref[...] = jnp.zeros(dg_ref.shape, F32)
            own.start()
            for k in range(len(masks)):
                copy(k, True).start()

        @pl.when(i == n_steps - 1)
        def _():
            for k in range(len(masks)):
                copy(k, False).wait_recv()
            for k in range(len(masks)):
                copy(k, True).wait_send()
            own.wait()

        dh = _dot(dp_ref[...], wt_ref[...])
        xhat, rstd = _rms_hat(x_ref[...])
        g, sc = g_ref[...], sc_ref[...]
        dsh_ref[...] += jnp.sum(dh, axis=0, keepdims=True)
        dhx = dh * xhat
        dsc_ref[...] += jnp.sum(dhx * g, axis=0, keepdims=True)
        dg_ref[...] += jnp.sum(dhx * sc, axis=0, keepdims=True)
        dxhat = dh * (g * sc)
        gx_ref[...] = dout_ref[...] + rstd * (dxhat - xhat * jnp.mean(dxhat * xhat, axis=1, keepdims=True))

    row = lambda w: pl.BlockSpec((tm, w), lambda i: (i, 0))
    vec = _full((1, D_MODEL))
    vshape = jax.ShapeDtypeStruct((1, D_MODEL), F32)
    hbm = pl.BlockSpec(memory_space=pl.ANY)
    return _call(
        body, name="in_proj_bwd_x",
        out_shape=(jax.ShapeDtypeStruct((s, D_MODEL), F32), vshape, vshape, vshape,
                   jax.ShapeDtypeStruct(parts.shape, parts.dtype)),
        grid=(n_steps,),
        in_specs=[row(WP), _full((WP, D_MODEL)), row(D_MODEL), row(D_MODEL), vec, vec, hbm],
        out_specs=(row(D_MODEL), vec, vec, vec, hbm),
        scratch_shapes=[pltpu.SemaphoreType.DMA((3,)), pltpu.SemaphoreType.DMA((3,)), pltpu.SemaphoreType.DMA],
        compiler_params=_params(("arbitrary",), has_side_effects=True),
    )(dproj, w_al_t, x, dout, g_pre, scale1p, parts)


def _in_proj_bwd_w(h, dproj, tk, tn):
    s = h.shape[0]

    def body(h_ref, dp_ref, gw_ref):
        @pl.when(pl.program_id(1) == 0)
        def _():
            gw_ref[...] = jnp.zeros(gw_ref.shape, F32)

        gw_ref[...] += _dot_tn(h_ref[...], dp_ref[...])

    return _call(
        body, name="in_proj_bwd_w", out_shape=jax.ShapeDtypeStruct((D_MODEL, WP), F32), grid=(WP // tn, s // tk),
        in_specs=[pl.BlockSpec((tk, D_MODEL), lambda n, k: (k, 0)), pl.BlockSpec((tk, tn), lambda n, k: (k, n))],
        out_specs=pl.BlockSpec((D_MODEL, tn), lambda n, k: (0, n)),
        compiler_params=_params(("parallel", "arbitrary")),
    )(h, dproj)


def _align_w_in(w_cols):
    def part(name, width):
        return w_cols[:, _SRC[name]:_SRC[name] + width]

    fpad = jnp.pad(part("fa", FOX_HEADS), ((0, 0), (0, LANES - FOX_HEADS)))
    return jnp.concatenate([part("qa", FOX_W), part("ka", FOX_W), part("va", FOX_W), part("za", FOX_W),
                            part("qb", SWA_W), part("zb", SWA_W), part("kb", SWA_KV_W), part("vb", SWA_KV_W), fpad], axis=1)


def _unalign_w_in(g_al):
    def part(c0, width):
        return g_al[:, c0:c0 + width]

    return jnp.concatenate([part(C_QA, FOX_W), part(C_KA, FOX_W), part(C_VA, FOX_W), part(C_F, FOX_HEADS),
                            part(C_ZA, FOX_W), part(C_QB, SWA_W), part(C_KB, SWA_KV_W), part(C_VB, SWA_KV_W),
                            part(C_ZB, SWA_W)], axis=1)


def _rope_tables(positions):
    inv_freq = ROPE_THETA ** (-jnp.arange(HALF, dtype=F32) / HALF)
    ang = positions.astype(F32)[:, None] * inv_freq
    cos, sin = jnp.cos(ang), jnp.sin(ang)
    return jnp.concatenate([cos, cos, cos, cos], axis=1), jnp.concatenate([-sin, sin, -sin, sin], axis=1)


def _tiles(s):
    if s >= 4096:
        return dict(tm=512, blk=512, bq=2048, bk=1024, bk_bwd=512, chunk=256, tq=256, tm_out=512, tk=1024, tn=1152)
    return dict(tm=128, blk=128, bq=256, bk=128, bk_bwd=128, chunk=128, tq=128, tm_out=128, tk=128, tn=1152)


def kernel(x, c, positions, w_ada, b_ada, g_pre, w_in, b_fgate, sinks, w_out, g_post, loss_target, m_w_ada, m_b_ada, m_g_pre, m_w_in, m_b_fgate, m_sinks, m_w_out, m_g_post, v_w_ada, v_b_ada, v_g_pre, v_w_in, v_b_fgate, v_sinks, v_w_out, v_g_post):
    s = x.shape[1]
    t = _tiles(s)
    nc = s // LANES
    rows = FOX_HEADS * nc
    me = 4 * lax.axis_index("x") + 2 * lax.axis_index("y") + lax.axis_index("c")
    chip = 2 * lax.axis_index("x") + lax.axis_index("y")
    core = lax.axis_index("c")
    x2, tgt = x[0], loss_target[0]

    c_all = _allgather_devices(c, "gather_c")[:, 0, :]
    a_all, mod_shard = _ada_shard(c_all, w_ada[0])
    mod_all = _allgather_devices(mod_shard, "gather_mod")
    mod_rows = lax.dynamic_index_in_dim(mod_all, me, axis=1, keepdims=False)
    mod = mod_rows.reshape(N_CHIPS, 2, W_ADA_SHARD)[:, 0, :].reshape(1, 3 * D_MODEL) + b_ada
    shift, scale1p, gate = mod[:, :D_MODEL], 1.0 + mod[:, D_MODEL:2 * D_MODEL], mod[:, 2 * D_MODEL:]

    w_in_pad = jnp.pad(w_in[0].astype(BF), ((0, 0), (0, W_IN_SHARD_PAD - W_IN_SHARD)))
    w_pack = jnp.concatenate([w_in_pad, w_out[0].astype(BF).reshape(D_MODEL, W_OUT_SHARD)], axis=1)
    w_all = _allgather_chips(w_pack.reshape(2, D_MODEL // 2, -1), "gather_weights").reshape(N_CHIPS, D_MODEL, -1)
    w_cols = jnp.concatenate([w_all[k, :, :W_IN_SHARD] for k in range(N_CHIPS)], axis=1)
    w_al = _align_w_in(w_cols)
    w_al_t = w_al.T
    w_out_all = w_all[:, :, W_IN_SHARD_PAD:].reshape(D_MODEL, D_MODEL)
    w_out_t = w_out_all.T

    cos_t, sin_t = _rope_tables(positions[0])

    f_pad = _forget_logits(x2, g_pre, scale1p, shift, w_al[:, C_F:], t["tm"])
    f_rows = f_pad[:, :FOX_HEADS].T.reshape(rows, LANES)
    bias_rows = jnp.repeat(b_fgate[0], nc)[:, None]
    cum = _log_forget_cumsum(f_rows, bias_rows, nc).reshape(FOX_HEADS, s)
    h, qat, ka, kat, va, vat, za, zb, qb, kb, vb, qbt, kbt, vbt, m_own = _in_proj(
        x2, g_pre, scale1p, shift, w_al[:, C_VA:C_F], w_al_t[:C_ZA], cum, cos_t, sin_t, t["tm"])
    m_own = m_own[:, None, :]
    fox_args = (qat, ka, vat, m_own, t["bq"], t["bk"], t["chunk"])
    oat, lse, bad, pt = _fox_fwd(*fox_args, running_max=False)
    overflowed = jnp.max(bad) > 0.0
    oat, lse = lax.cond(overflowed, lambda: _fox_fwd(*fox_args, running_max=True)[:2], lambda: (oat, lse))
    inv_l = jnp.where(overflowed, 1.0, jnp.exp2(m_own - lse))
    sinks_g = sinks.reshape(SWA_KV_HEADS, 1, SWA_GROUP)
    obt = _swa_fwd(qbt, kb, vbt, sinks_g, t["tq"])

    dout, doat, delta_a, dza, dob, dobt, dzb, gw_out, dgate, dg_post, loss_part = _out_proj(
        oat, za, obt, zb, x2, tgt, w_out_all, w_out_t, gate, g_post, inv_l, t["tm_out"])

    bwd_args = (qat, ka, kat, va, doat, lse, delta_a)
    bwd_tiles = (t["bq"], t["bk_bwd"], t["chunk"], t["blk"])
    dqt, dkt, dvt = lax.cond(overflowed, lambda: _fox_bwd(*bwd_args, None, *bwd_tiles),
                             lambda: _fox_bwd(*bwd_args, pt, *bwd_tiles))
    dcum = dqt[:, :, HEAD_DIM, :].reshape(FOX_HEADS, s) - dkt[:, HEAD_DIM, :]
    df_rows, db_heads = _log_forget_cumsum_bwd(dcum.reshape(rows, LANES), f_rows, bias_rows, nc)
    df_pad = jnp.pad(df_rows.reshape(FOX_HEADS, s).T, ((0, 0), (0, LANES - FOX_HEADS)))
    dqb, dkb, dvb, dsinks = _swa_bwd(qb, qbt, kb, kbt, vb, sinks_g, dob, dobt, t["tq"])

    dproj = _assemble_dproj(dqt, dkt, dvt, dza, dqb, dzb, dkb, dvb, df_pad, cos_t, sin_t, t["blk"])
    gw_in = _unalign_w_in(_in_proj_bwd_w(h, dproj, t["tk"], t["tn"]))

    gin = jnp.pad(gw_in.reshape(D_MODEL, N_CHIPS, W_IN_SHARD).transpose(1, 0, 2),
                  ((0, 0), (0, 0), (0, W_IN_SHARD_PAD - W_IN_SHARD)))
    gout = gw_out.reshape(N_CHIPS, D_MODEL, W_OUT_SHARD)
    gbig = jnp.concatenate([gin, gout], axis=2)
    half = D_MODEL // 2
    gw = W_IN_SHARD_PAD + W_OUT_SHARD
    keep = lax.dynamic_slice_in_dim(gbig, core * half, half, axis=1)
    give = lax.dynamic_slice_in_dim(gbig, (1 - core) * half, half, axis=1)
    got = _swap_sibling(give.reshape(N_CHIPS * half, gw), "swap_grad_halves")
    pair = _add(keep.reshape(N_CHIPS * half, gw), got, "add_pair", BF).reshape(N_CHIPS, half, gw)
    grad_x, dshift, dscale, dg_pre, from_chips = _in_proj_bwd_x(
        dproj, w_al_t, x2, dout, g_pre, scale1p, t["tm_out"], pair)

    pad_lane = lambda vrow: jnp.pad(vrow, ((0, 0), (0, LANES - vrow.shape[1])))
    packed = jnp.concatenate([dshift, dscale, dgate, dg_pre, dg_post,
                              pad_lane(db_heads.reshape(1, FOX_HEADS)), pad_lane(dsinks.reshape(1, FOX_HEADS)),
                              pad_lane(loss_part)], axis=1)
    parts = _allgather_devices(packed, "gather_partials")
    tot = _sum_devices(parts)
    loss = tot[0, P_LOSS]
    g_b_ada = tot[:, P_DMOD:P_DMOD + 3 * D_MODEL]
    g_g_pre = tot[:, P_GPRE:P_GPRE + D_MODEL]
    g_g_post = tot[:, P_GPOST:P_GPOST + D_MODEL]
    g_b_fgate = tot[:, P_BF:P_BF + FOX_HEADS]
    g_sinks = tot[:, P_SINK:P_SINK + FOX_HEADS]
    dm_shard = lax.dynamic_slice_in_dim(parts[:, 0, :3 * D_MODEL], chip * W_ADA_SHARD, W_ADA_SHARD, axis=1)
    g_w_ada = _grad_w_ada(a_all.T, dm_shard)

    mine = _sum_chips(from_chips, "sum_chips")
    other = _swap_sibling(mine, "swap_grad_result")
    lo = jnp.where(core == 0, mine, other)
    hi = jnp.where(core == 0, other, mine)
    gfull = jnp.concatenate([lo, hi], axis=0)
    g_w_in = gfull[:, :W_IN_SHARD]
    g_w_out = gfull[:, W_IN_SHARD_PAD:].reshape(W_OUT_SHARD, D_MODEL)

    grads = dict(w_ada=g_w_ada, b_ada=g_b_ada, g_pre=g_g_pre, w_in=g_w_in, b_fgate=g_b_fgate, sinks=g_sinks,
                 w_out=g_w_out, g_post=g_g_post)
    weights = dict(w_ada=w_ada, b_ada=b_ada, g_pre=g_pre, w_in=w_in, b_fgate=b_fgate, sinks=sinks, w_out=w_out, g_post=g_post)
    moms = dict(w_ada=m_w_ada, b_ada=m_b_ada, g_pre=m_g_pre, w_in=m_w_in, b_fgate=m_b_fgate, sinks=m_sinks, w_out=m_w_out, g_post=m_g_post)
    vars_ = dict(w_ada=v_w_ada, b_ada=v_b_ada, g_pre=v_g_pre, w_in=v_w_in, b_fgate=v_b_fgate, sinks=v_sinks, w_out=v_w_out, g_post=v_g_post)
    names = ["w_ada", "b_ada", "g_pre", "w_in", "b_fgate", "sinks", "w_out", "g_post"]
    g_out, d_out, m_out, v_out = [], [], [], []
    for n in names:
        g2 = grads[n].reshape(weights[n].shape[-2:])
        go, d, nm, nv = _adamw(weights[n], g2, moms[n], vars_[n], "adamw_" + n)
        g_out.append(go)
        d_out.append(d)
        m_out.append(nm)
        v_out.append(nv)
    return (loss, grad_x.reshape(x.shape), *g_out, *d_out, *m_out, *v_out)
```

```python
import functools

import jax
import jax.numpy as jnp
from jax import lax
from jax.experimental import pallas as pl
from jax.experimental.pallas import tpu as pltpu

_INTERPRET = False

D_MODEL = 1024
HEAD_DIM = 64
HALF = HEAD_DIM // 2
AUG_DIM = 128
AUG_ROWS = 8
VT_ROWS = 80
LOG2E = 1.4426950408889634
LN2 = 0.6931471805599453
Q_SCALE = LOG2E * 64 ** -0.5
FOX_HEADS = 8
FOX_W = 512
SWA_W = 512
SWA_KV_HEADS = 2
SWA_GROUP = 4
SWA_KV_W = 128
WINDOW = 128
ROPE_THETA = 10000.0
RMS_EPS = 1e-6
IN_WIDTH = 3336
N_CHIPS = 4
N_DEV = 8
W_IN_SHARD = IN_WIDTH // N_CHIPS
W_IN_SHARD_PAD = 896
W_ADA_SHARD = 3 * D_MODEL // N_CHIPS
W_OUT_SHARD = D_MODEL // N_CHIPS
LANES = 128

_SRC = dict(qa=0, ka=512, va=1024, fa=1536, za=1544, qb=2056, kb=2568, vb=2696, zb=2824)
C_QA, C_KA, C_VA, C_ZA, C_QB, C_ZB, C_KB, C_VB, C_F = 0, 512, 1024, 1536, 2048, 2560, 3072, 3200, 3328
WP = 3456

ADAM_LR = 0.001
ADAM_B1 = 0.9
ADAM_B2 = 0.999
ADAM_EPS = 1e-08
ADAM_WD = 0.01
ADAM_STEP = 10

VMEM_LIMIT = 56 * 1024 * 1024
NEG = -1e30
OVERFLOW_GUARD = 1e30
MESH = pl.DeviceIdType.MESH
BF = jnp.bfloat16
F32 = jnp.float32

P_DMOD, P_GPRE, P_GPOST, P_BF, P_SINK, P_LOSS, P_LEN = 0, 3072, 4096, 5120, 5248, 5376, 5504


def _call(body, **kw):
    return pl.pallas_call(body, interpret=_INTERPRET, **kw)


def _params(sem=None, **kw):
    return pltpu.CompilerParams(dimension_semantics=sem, vmem_limit_bytes=VMEM_LIMIT, **kw)


def _full(shape):
    zeros = (0,) * len(shape)
    return pl.BlockSpec(shape, lambda *_: zeros)


def _dot(a, b):
    return jnp.dot(a, b, preferred_element_type=F32)


def _dot_nt(a, b):
    return lax.dot_general(a, b, (((1,), (1,)), ((), ())), preferred_element_type=F32)


def _dot_tn(a, b):
    return lax.dot_general(a, b, (((0,), (0,)), ((), ())), preferred_element_type=F32)


def _sigmoid(z):
    return 1.0 / (1.0 + jnp.exp(-z))


def _rope_partner(t):
    w = t.shape[-1]
    lane = lax.broadcasted_iota(jnp.int32, t.shape, t.ndim - 1)
    return jnp.where((lane & (HEAD_DIM - 1)) < HALF, pltpu.roll(t, w - HALF, t.ndim - 1), pltpu.roll(t, HALF, t.ndim - 1))


def _allgather_devices(v, name):
    r, cdim = v.shape
    masks = [(dx, dy, dc) for dx in (0, 1) for dy in (0, 1) for dc in (0, 1)][1:]

    def body(v_ref, out_ref, send_sems, recv_sems):
        x, y, c = lax.axis_index("x"), lax.axis_index("y"), lax.axis_index("c")
        me = 4 * x + 2 * y + c
        out_ref[me] = v_ref[...]
        copies = []
        for k, (dx, dy, dc) in enumerate(masks):
            cp = pltpu.make_async_remote_copy(
                src_ref=v_ref, dst_ref=out_ref.at[me], send_sem=send_sems.at[k], recv_sem=recv_sems.at[k],
                device_id=(x ^ dx, y ^ dy, c ^ dc), device_id_type=MESH)
            cp.start()
            copies.append(cp)
        for k, (dx, dy, dc) in enumerate(masks):
            peer = 4 * (x ^ dx) + 2 * (y ^ dy) + (c ^ dc)
            pltpu.make_async_remote_copy(
                src_ref=v_ref, dst_ref=out_ref.at[peer], send_sem=send_sems.at[k], recv_sem=recv_sems.at[k],
                device_id=(x ^ dx, y ^ dy, c ^ dc), device_id_type=MESH).wait_recv()
        for cp in copies:
            cp.wait_send()

    return _call(
        body, name=name, out_shape=jax.ShapeDtypeStruct((N_DEV, r, cdim), v.dtype),
        in_specs=[pl.BlockSpec(memory_space=pltpu.VMEM)], out_specs=pl.BlockSpec(memory_space=pltpu.VMEM),
        scratch_shapes=[pltpu.SemaphoreType.DMA((7,)), pltpu.SemaphoreType.DMA((7,))],
        compiler_params=pltpu.CompilerParams(has_side_effects=True),
    )(v)


def _allgather_chips(v, name):
    _, r, cdim = v.shape
    masks = [(1, 0), (0, 1), (1, 1)]
    n = len(masks)

    def body(v_ref, out_ref, send_sems, recv_sems, local_sem):
        x, y, c = lax.axis_index("x"), lax.axis_index("y"), lax.axis_index("c")
        me = 2 * x + y
        mine = pltpu.make_async_copy(v_ref, out_ref.at[me], local_sem)
        mine.start()

        def copy(k, chip, half, to):
            return pltpu.make_async_remote_copy(
                src_ref=v_ref.at[half] if k < n else out_ref.at[chip, half], dst_ref=out_ref.at[chip, half],
                send_sem=send_sems.at[k], recv_sem=recv_sems.at[k], device_id=to, device_id_type=MESH)

        first = [copy(k, me, c, (x ^ dx, y ^ dy, c)) for k, (dx, dy) in enumerate(masks)]
        for cp in first:
            cp.start()
        passed = []
        for k, (dx, dy) in enumerate(masks):
            peer = 2 * (x ^ dx) + (y ^ dy)
            copy(k, peer, c, (x, y, c)).wait_recv()
            cp = copy(n + k, peer, c, (x, y, 1 - c))
            cp.start()
            passed.append(cp)
        for k, (dx, dy) in enumerate(masks):
            copy(n + k, 2 * (x ^ dx) + (y ^ dy), 1 - c, (x, y, c)).wait_recv()
        for cp in first + passed:
            cp.wait_send()
        mine.wait()

    return _call(
        body, name=name, out_shape=jax.ShapeDtypeStruct((N_CHIPS, 2, r, cdim), v.dtype),
        in_specs=[pl.BlockSpec(memory_space=pl.ANY)], out_specs=pl.BlockSpec(memory_space=pl.ANY),
        scratch_shapes=[pltpu.SemaphoreType.DMA((2 * n,)), pltpu.SemaphoreType.DMA((2 * n,)), pltpu.SemaphoreType.DMA],
        compiler_params=pltpu.CompilerParams(has_side_effects=True),
    )(v)


def _swap_sibling(v, name):
    def body(v_ref, out_ref, send_sem, recv_sem):
        x, y, c = lax.axis_index("x"), lax.axis_index("y"), lax.axis_index("c")
        cp = pltpu.make_async_remote_copy(
            src_ref=v_ref, dst_ref=out_ref, send_sem=send_sem, recv_sem=recv_sem,
            device_id=(x, y, 1 - c), device_id_type=MESH)
        cp.start()
        cp.wait()

    return _call(
        body, name=name, out_shape=jax.ShapeDtypeStruct(v.shape, v.dtype),
        in_specs=[pl.BlockSpec(memory_space=pl.ANY)], out_specs=pl.BlockSpec(memory_space=pl.ANY),
        scratch_shapes=[pltpu.SemaphoreType.DMA, pltpu.SemaphoreType.DMA],
        compiler_params=pltpu.CompilerParams(has_side_effects=True),
    )(v)


def _ada_shard(c_all, w_ada_shard):
    def body(c_ref, w_ref, a_ref, mod_ref):
        cv = c_ref[...]
        a = cv * _sigmoid(cv)
        a_ref[...] = a
        mod_ref[...] = _dot(a.astype(BF), w_ref[...].astype(BF))

    return _call(
        body, name="ada_shard",
        out_shape=(jax.ShapeDtypeStruct((N_DEV, D_MODEL), F32), jax.ShapeDtypeStruct((N_DEV, W_ADA_SHARD), F32)),
        compiler_params=_params(),
    )(c_all, w_ada_shard)


def _grad_w_ada(a_t, dm_shard):
    def body(a_ref, dm_ref, out_ref):
        acc = jnp.zeros((D_MODEL, W_ADA_SHARD), F32)
        for b in range(N_DEV):
            acc = acc + a_ref[:, b:b + 1] * dm_ref[b:b + 1, :]
        out_ref[...] = acc

    return _call(body, name="grad_w_ada", out_shape=jax.ShapeDtypeStruct((D_MODEL, W_ADA_SHARD), F32),
                 compiler_params=_params())(a_t, dm_shard)


def _sum_devices(parts):
    n = parts.shape[-1]

    def body(p_ref, out_ref):
        acc = p_ref[0]
        for b in range(1, N_DEV):
            acc = acc + p_ref[b]
        out_ref[...] = acc

    return _call(body, name="sum_devices", out_shape=jax.ShapeDtypeStruct((1, n), F32), compiler_params=_params())(parts)


def _add(a, b, name, out_dtype):
    r, cdim = a.shape
    tr = min(r, 256)

    def body(a_ref, b_ref, o_ref):
        o_ref[...] = (a_ref[...] + b_ref[...]).astype(out_dtype)

    spec = pl.BlockSpec((tr, cdim), lambda i: (i, 0))
    return _call(body, name=name, out_shape=jax.ShapeDtypeStruct(a.shape, out_dtype), grid=(r // tr,),
                 in_specs=[spec, spec], out_specs=spec, compiler_params=_params(("parallel",)))(a, b)


def _sum_chips(parts, name):
    _, r, cdim = parts.shape
    tr = min(r, 128)

    def body(p_ref, o_ref):
        o_ref[...] = ((p_ref[0].astype(F32) + p_ref[1].astype(F32)) + p_ref[2].astype(F32)) + p_ref[3].astype(F32)

    return _call(body, name=name, out_shape=jax.ShapeDtypeStruct((r, cdim), F32), grid=(r // tr,),
                 in_specs=[pl.BlockSpec((N_CHIPS, tr, cdim), lambda i: (0, i, 0))],
                 out_specs=pl.BlockSpec((tr, cdim), lambda i: (i, 0)), compiler_params=_params(("parallel",)))(parts)


def _adamw(w, g, m, v, name):
    r, cdim = w.shape[-2:]
    lead = w.ndim - 2
    tr = r if r <= 256 else 256
    c1 = 1.0 / (1.0 - ADAM_B1 ** ADAM_STEP)
    c2 = 1.0 / (1.0 - ADAM_B2 ** ADAM_STEP)

    def body(w_ref, g_ref, m_ref, v_ref, go_ref, d_ref, nm_ref, nv_ref):
        gv = g_ref[...].reshape(go_ref.shape)
        nm = ADAM_B1 * m_ref[...] + (1.0 - ADAM_B1) * gv
        nv = ADAM_B2 * v_ref[...] + (1.0 - ADAM_B2) * (gv * gv)
        m_hat = nm * c1
        v_hat = nv * c2
        go_ref[...] = gv
        d_ref[...] = -ADAM_LR * (m_hat / (jnp.sqrt(v_hat) + ADAM_EPS) + ADAM_WD * w_ref[...])
        nm_ref[...] = nm
        nv_ref[...] = nv

    spec = pl.BlockSpec((1,) * lead + (tr, cdim), lambda i: (0,) * lead + (i, 0))
    shp = jax.ShapeDtypeStruct(w.shape, F32)
    return _call(body, name=name, out_shape=(shp,) * 4, grid=(r // tr,),
                 in_specs=[spec, pl.BlockSpec((tr, cdim), lambda i: (i, 0)), spec, spec],
                 out_specs=(spec,) * 4, compiler_params=_params(("parallel",)))(w, g, m, v)


def _head_of_row(r, nc):
    assert nc & (nc - 1) == 0
    return lax.shift_right_logical(r, nc.bit_length() - 1)


def _chunk_mats(rows, nc, reverse):
    ri = lax.broadcasted_iota(jnp.int32, (rows, rows), 0)
    ci = lax.broadcasted_iota(jnp.int32, (rows, rows), 1)
    same = _head_of_row(ri, nc) == _head_of_row(ci, nc)
    between = jnp.where(same & ((ci > ri) if reverse else (ci < ri)), 1.0, 0.0).astype(F32)
    li = lax.broadcasted_iota(jnp.int32, (LANES, LANES), 0)
    lj = lax.broadcasted_iota(jnp.int32, (LANES, LANES), 1)
    within = jnp.where((li >= lj) if reverse else (li <= lj), 1.0, 0.0).astype(F32)
    return between, within


def _dot_hi(a, b):
    return jnp.dot(a, b, preferred_element_type=F32, precision=lax.Precision.HIGHEST)


def _scan_rows(t, nc, reverse):
    between, within = _chunk_mats(t.shape[0], nc, reverse)
    inner = _dot_hi(t, within)
    tot = jnp.sum(t, axis=1, keepdims=True)
    return inner + _dot_hi(between, jnp.broadcast_to(tot, t.shape))


def _log_forget_cumsum(f_rows, bias_rows, nc):
    def body(f_ref, b_ref, cum_ref):
        z = f_ref[...] + b_ref[...]
        lf = jnp.minimum(z, 0.0) - jnp.log(1.0 + jnp.exp(-jnp.abs(z)))
        cum_ref[...] = _scan_rows(lf, nc, False)

    return _call(body, name="forget_cumsum", out_shape=jax.ShapeDtypeStruct(f_rows.shape, F32),
                 compiler_params=_params())(f_rows, bias_rows)


def _log_forget_cumsum_bwd(dcum_rows, f_rows, bias_rows, nc):
    rows = f_rows.shape[0]

    def body(d_ref, f_ref, b_ref, df_ref, db_ref):
        dlf = _scan_rows(d_ref[...], nc, True)
        z = f_ref[...] + b_ref[...]
        df = dlf * _sigmoid(-z)
        df_ref[...] = df
        hi = lax.broadcasted_iota(jnp.int32, (FOX_HEADS, rows), 0)
        ri = lax.broadcasted_iota(jnp.int32, (FOX_HEADS, rows), 1)
        sel = jnp.where(_head_of_row(ri, nc) == hi, 1.0, 0.0).astype(F32)
        db_ref[...] = jnp.sum(_dot_hi(sel, df), axis=1, keepdims=True)

    return _call(body, name="forget_cumsum_bwd",
                 out_shape=(jax.ShapeDtypeStruct(f_rows.shape, F32), jax.ShapeDtypeStruct((FOX_HEADS, 1), F32)),
                 compiler_params=_params())(dcum_rows, f_rows, bias_rows)


def _rms_hat(xv):
    rstd = lax.rsqrt(jnp.mean(xv * xv, axis=-1, keepdims=True) + RMS_EPS)
    return xv * rstd, rstd


def _modulated(x_ref, g_ref, sc_ref, sh_ref):
    xhat, _ = _rms_hat(x_ref[...])
    return ((xhat * g_ref[...]) * sc_ref[...] + sh_ref[...]).astype(BF)


def _forget_logits(x, g_pre, scale1p, shift, w_f, tm):
    s = x.shape[0]

    def body(x_ref, g_ref, sc_ref, sh_ref, w_ref, f_ref):
        f_ref[...] = _dot(_modulated(x_ref, g_ref, sc_ref, sh_ref), w_ref[...])

    vec = _full((1, D_MODEL))
    return _call(
        body, name="forget_logits", out_shape=jax.ShapeDtypeStruct((s, LANES), F32), grid=(s // tm,),
        in_specs=[pl.BlockSpec((tm, D_MODEL), lambda i: (i, 0)), vec, vec, vec, _full((D_MODEL, LANES))],
        out_specs=pl.BlockSpec((tm, LANES), lambda i: (i, 0)), compiler_params=_params(("parallel",)),
    )(x, g_pre, scale1p, shift, w_f)


def _split3(v):
    hi = v.astype(BF).astype(F32)
    mid = (v - hi).astype(BF).astype(F32)
    lo = ((v - hi) - mid).astype(BF).astype(F32)
    return hi, mid, lo


def _in_proj(x, g_pre, scale1p, shift, w_rows, w_t_fox, cum, cos_t, sin_t, tm):
    s = x.shape[0]
    r_va, r_za, r_qb, r_zb, r_kb, r_vb = 0, 512, 1024, 1536, 2048, 2176

    def body(x_ref, g_ref, sc_ref, sh_ref, w_ref, wt_ref, cum_ref, cos_ref, sin_ref,
             h_ref, qat_ref, ka_ref, kat_ref, v_ref, vt_ref, za_ref, zb_ref, qb_ref, kb_ref, vb_ref,
             qbt_ref, kbt_ref, vbt_ref, mo_ref):
        hb = _modulated(x_ref, g_ref, sc_ref, sh_ref)
        h_ref[...] = hb

        def sec(c0, width):
            return _dot(hb, w_ref[:, c0:c0 + width])

        def sec_t(r0):
            return _dot_nt(wt_ref[r0:r0 + FOX_W, :], hb)

        q_t = sec_t(0) * Q_SCALE
        k_t = sec_t(FOX_W)
        v_t = sec_t(2 * FOX_W)
        va = sec(r_va, FOX_W)
        zeros = jnp.zeros((AUG_DIM - HEAD_DIM - AUG_ROWS, tm), F32)
        ri = lax.broadcasted_iota(jnp.int32, (AUG_ROWS, tm), 0)
        const = jnp.where(ri == AUG_ROWS - 1, 0.0, 1.0)
        ri_v = lax.broadcasted_iota(jnp.int32, (VT_ROWS - HEAD_DIM, tm), 0)
        v_feat = jnp.where(ri_v == 0, 1.0, 0.0).astype(BF)
        for hd in range(FOX_HEADS):
            rows = slice(hd * HEAD_DIM, (hd + 1) * HEAD_DIM)
            cum2 = cum_ref[hd:hd + 1, :] * LOG2E
            hi, mid, lo = (jnp.broadcast_to(part, (AUG_ROWS, tm)) for part in _split3(cum2))
            q_feat = jnp.where(ri == 1, hi, jnp.where(ri == 2, mid, jnp.where(ri == 3, lo, const)))
            k_feat = jnp.where(ri == 4, -hi, jnp.where(ri == 5, -mid, jnp.where(ri == 6, -lo, const)))
            q_aug = jnp.concatenate([q_t[rows], q_feat, zeros], axis=0)
            k_aug = jnp.concatenate([k_t[rows], k_feat, zeros], axis=0)
            mo_ref[hd:hd + 1, :] = jnp.sum(q_t[rows] * k_t[rows], axis=0, keepdims=True) + 1.0
            qat_ref[hd] = q_aug.astype(BF)
            kat_ref[hd] = k_aug.astype(BF)
            ka_ref[hd] = k_aug.T.astype(BF)
            vt_ref[hd] = jnp.concatenate([v_t[rows].astype(BF), v_feat], axis=0)
            v_ref[hd] = va[:, rows].astype(BF)
        za_ref[...] = sec(r_za, FOX_W)
        zb_ref[...] = sec(r_zb, SWA_W)
        cos2, sin2 = cos_ref[...], sin_ref[...]
        cos8 = jnp.concatenate([cos2] * 4, axis=1)
        sin8 = jnp.concatenate([sin2] * 4, axis=1)
        qb = sec(r_qb, SWA_W)
        qb = (qb * cos8 + _rope_partner(qb) * sin8) * (HEAD_DIM ** -0.5)
        qb_ref[...] = qb.astype(BF)
        for a in range(SWA_W // LANES):
            qbt_ref[a * LANES:(a + 1) * LANES, :] = qb[:, a * LANES:(a + 1) * LANES].T.astype(BF)
        kb = sec(r_kb, SWA_KV_W)
        kb = kb * cos2 + _rope_partner(kb) * sin2
        vb = sec(r_vb, SWA_KV_W)
        kb_t, vb_t = kb.T, vb.T
        for hd in range(SWA_KV_HEADS):
            sl = slice(hd * HEAD_DIM, (hd + 1) * HEAD_DIM)
            kb_ref[hd] = kb[:, sl].astype(BF)
            vb_ref[hd] = vb[:, sl].astype(BF)
            kbt_ref[hd] = kb_t[sl].astype(BF)
            vbt_ref[hd] = jnp.concatenate([vb_t[sl].astype(BF), v_feat], axis=0)

    row = lambda w: pl.BlockSpec((tm, w), lambda i: (i, 0))
    heads = lambda n, w=HEAD_DIM: pl.BlockSpec((n, tm, w), lambda i: (0, i, 0))
    heads_t = lambda w: pl.BlockSpec((FOX_HEADS, w, tm), lambda i: (0, 0, i))
    vec = _full((1, D_MODEL))
    hs = lambda a, b: jax.ShapeDtypeStruct((FOX_HEADS, a, b), BF)
    out_shape = (
        jax.ShapeDtypeStruct((s, D_MODEL), BF),
        hs(AUG_DIM, s), hs(s, AUG_DIM), hs(AUG_DIM, s), hs(s, HEAD_DIM), hs(VT_ROWS, s),
        jax.ShapeDtypeStruct((s, FOX_W), F32), jax.ShapeDtypeStruct((s, SWA_W), F32),
        jax.ShapeDtypeStruct((s, SWA_W), BF),
        jax.ShapeDtypeStruct((SWA_KV_HEADS, s, HEAD_DIM), BF), jax.ShapeDtypeStruct((SWA_KV_HEADS, s, HEAD_DIM), BF),
        jax.ShapeDtypeStruct((SWA_W, s), BF),
        jax.ShapeDtypeStruct((SWA_KV_HEADS, HEAD_DIM, s), BF), jax.ShapeDtypeStruct((SWA_KV_HEADS, VT_ROWS, s), BF),
        jax.ShapeDtypeStruct((FOX_HEADS, s), F32),
    )
    kv_t = lambda w: pl.BlockSpec((SWA_KV_HEADS, w, tm), lambda i: (0, 0, i))
    return _call(
        body, name="in_proj", out_shape=out_shape, grid=(s // tm,),
        in_specs=[row(D_MODEL), vec, vec, vec, _full(w_rows.shape), _full(w_t_fox.shape),
                  pl.BlockSpec((FOX_HEADS, tm), lambda i: (0, i)), row(LANES), row(LANES)],
        out_specs=(row(D_MODEL), heads_t(AUG_DIM), heads(FOX_HEADS, AUG_DIM), heads_t(AUG_DIM), heads(FOX_HEADS),
                   heads_t(VT_ROWS), row(FOX_W), row(SWA_W), row(SWA_W), heads(SWA_KV_HEADS), heads(SWA_KV_HEADS),
                   pl.BlockSpec((SWA_W, tm), lambda i: (0, i)), kv_t(HEAD_DIM), kv_t(VT_ROWS),
                   pl.BlockSpec((FOX_HEADS, tm), lambda i: (0, i))),
        compiler_params=_params(("parallel",)),
    )(x, g_pre, scale1p, shift, w_rows, w_t_fox, cum, cos_t, sin_t)


def _diag_chunks(d, bq, bk, chunk):
    out = []
    for c0 in range(0, bq, chunk):
        if d is None or d * bk + bk - 1 <= c0:
            out.append((c0, None))
        elif d * bk <= c0 + chunk - 1:
            kpos = d * bk + lax.broadcasted_iota(jnp.int32, (bk, chunk), 0)
            qpos = c0 + lax.broadcasted_iota(jnp.int32, (bk, chunk), 1)
            out.append((c0, kpos <= qpos))
    return out


def _fox_fwd(qat, ka, vt, m_own, bq, bk, chunk, running_max):
    nh, _, s = qat.shape
    r = bq // bk

    def body(ka_ref, qat_ref, vt_ref, mo_ref, o_ref, lse_ref, bad_ref, *rest):
        pt_ref, m_scr, acc_scr = (None,) * running_max + rest
        i, j = pl.program_id(1), pl.program_id(2)

        @pl.when(j == 0)
        def _():
            m_scr[...] = jnp.full(m_scr.shape, NEG, F32) if running_max else mo_ref[0]
            acc_scr[...] = jnp.zeros(acc_scr.shape, F32)

        def careful(d):
            kv, vtv = ka_ref[0], vt_ref[0]

            def one_chunk(n, carry):
                c0 = pl.multiple_of(n * chunk, chunk)
                cs = pl.ds(c0, chunk)
                sc = _dot(kv, qat_ref[0, :, cs])
                if d is not None:
                    kpos = d * bk + lax.broadcasted_iota(jnp.int32, (bk, chunk), 0)
                    qpos = c0 + lax.broadcasted_iota(jnp.int32, (bk, chunk), 1)
                    sc = jnp.where(kpos <= qpos, sc, NEG)
                m_prev = m_scr[:, cs]
                m_new = jnp.maximum(m_prev, jnp.max(sc, axis=0, keepdims=True))
                p = jnp.exp2(sc - m_new).astype(BF)
                acc_scr[:, cs] = jnp.exp2(m_prev - m_new) * acc_scr[:, cs] + _dot(vtv, p)
                m_scr[:, cs] = m_new
                return carry

            lax.fori_loop(0, bq // chunk, one_chunk, 0)

        def fast(d):
            kv, vtv = ka_ref[0], vt_ref[0]
            todo = _diag_chunks(d, bq, bk, chunk)
            scores = lambda c0: _dot(kv, qat_ref[0, :, c0:c0 + chunk])
            sc_next = scores(todo[0][0])
            for n, (c0, mask) in enumerate(todo):
                cs = slice(c0, c0 + chunk)
                sc = sc_next
                if n + 1 < len(todo):
                    sc_next = scores(todo[n + 1][0])
                if mask is not None:
                    sc = jnp.where(mask, sc, NEG)
                p = jnp.exp2(sc - m_scr[:, cs]).astype(BF)
                pt_ref[0, :, cs] = p
                acc_scr[:, cs] += _dot(vtv, p)

        step = careful if running_max else fast

        @pl.when(j < i * r)
        def _():
            step(None)

        for d in range(r):
            @pl.when(j == i * r + d)
            def _(d=d):
                step(d)

        @pl.when(j == i * r + r - 1)
        def _():
            l = acc_scr[HEAD_DIM:HEAD_DIM + 1, :]
            o_ref[0] = acc_scr[:HEAD_DIM, :] / l
            lse_ref[0] = m_scr[...] + jnp.log2(l)
            bad_ref[0] = jnp.where(l < OVERFLOW_GUARD, 0.0, 1.0)

    kmap = lambda h, i, j: (h, jnp.minimum(j, i * r + r - 1), 0)
    kmap_t = lambda h, i, j: (h, 0, jnp.minimum(j, i * r + r - 1))
    qrow = pl.BlockSpec((1, 1, bq), lambda h, i, j: (h, 0, i))
    row_shape = jax.ShapeDtypeStruct((nh, 1, s), F32)
    out_shape = (jax.ShapeDtypeStruct((nh, HEAD_DIM, s), F32), row_shape, row_shape)
    out_specs = (pl.BlockSpec((1, HEAD_DIM, bq), lambda h, i, j: (h, 0, i)), qrow, qrow)
    if not running_max:
        out_shape += (jax.ShapeDtypeStruct((nh, s, s), BF),)
        out_specs += (pl.BlockSpec((1, bk, bq), lambda h, i, j: (h, jnp.minimum(j, i * r + r - 1), i)),)
    return _call(
        body, name="fox_fwd_running_max" if running_max else "fox_fwd",
        out_shape=out_shape,
        grid=(nh, s // bq, s // bk),
        in_specs=[pl.BlockSpec((1, bk, AUG_DIM), kmap), pl.BlockSpec((1, AUG_DIM, bq), lambda h, i, j: (h, 0, i)),
                  pl.BlockSpec((1, VT_ROWS, bk), kmap_t), qrow],
        out_specs=out_specs,
        scratch_shapes=[pltpu.VMEM((1, bq), F32), pltpu.VMEM((VT_ROWS, bq), F32)],
        compiler_params=_params(("parallel", "parallel", "arbitrary")),
    )(ka, qat, vt, m_own)


def _fox_bwd(qat, ka, kat, v, dot_, lse, delta, pt, bq, bk, chunk, dq_blk):
    nh, _, s = qat.shape
    r = bq // bk
    nq = s // bq
    stored = pt is not None

    def body(a_ref, b_ref, kat_ref, v_ref, qat_ref, do_ref, dl_ref, dq_ref, dk_ref, dv_ref, dk_scr, dv_scr):
        ka_ref, lse_ref, pt_ref = (None, None, a_ref) if stored else (a_ref, b_ref, None)
        j, i = pl.program_id(1), pl.program_id(2)

        @pl.when((j == 0) & (i == 0))
        def _():
            dq_ref[...] = jnp.zeros(dq_ref.shape, F32)

        @pl.when(i == 0)
        def _():
            dk_scr[...] = jnp.zeros(dk_scr.shape, F32)
            dv_scr[...] = jnp.zeros(dv_scr.shape, F32)

        def step(d):
            ktv, vv = kat_ref[0], v_ref[0]
            kv = None if stored else ka_ref[0]
            todo = _diag_chunks(d, bq, bk, chunk)

            def products(c0):
                cs = slice(c0, c0 + chunk)
                return None if stored else _dot(kv, qat_ref[0, :, cs]), _dot(vv, do_ref[0, :, cs])

            nxt = products(todo[0][0])
            for n, (c0, mask) in enumerate(todo):
                cs = slice(c0, c0 + chunk)
                sc, dp = nxt
                if n + 1 < len(todo):
                    nxt = products(todo[n + 1][0])
                if stored:
                    p_bf = pt_ref[0, :, cs]
                    p = p_bf.astype(F32)
                else:
                    p = jnp.exp2(sc - lse_ref[0, :, cs])
                    if mask is not None:
                        p = jnp.where(mask, p, 0.0)
                    p_bf = p.astype(BF)
                ds = (p * (dp - dl_ref[0, :, cs])).astype(BF)
                dv_scr[...] += _dot_nt(do_ref[0, :, cs], p_bf)
                dk_scr[...] += _dot_nt(qat_ref[0, :VT_ROWS, cs], ds)
                c1 = c0 % dq_blk
                dq_ref[0, i * (bq // dq_blk) + c0 // dq_blk, :, c1:c1 + chunk] += _dot(ktv[:VT_ROWS], ds)

        @pl.when(i * r > j)
        def _():
            step(None)

        for d in range(r):
            @pl.when(j == i * r + d)
            def _(d=d):
                step(d)

        @pl.when(i == nq - 1)
        def _():
            dk_ref[0] = dk_scr[...]
            dv_ref[0] = dv_scr[...]

    qmap = lambda h, j, i: (h, 0, jnp.maximum(i, j // r))
    kmap_t = lambda h, j, i: (h, 0, j)
    if stored:
        first = [(pt, pl.BlockSpec((1, bk, bq), lambda h, j, i: (h, j, jnp.maximum(i, j // r)))),
                 (delta, pl.BlockSpec((1, 1, bq), qmap))]
    else:
        first = [(ka, pl.BlockSpec((1, bk, AUG_DIM), lambda h, j, i: (h, j, 0))), (lse, pl.BlockSpec((1, 1, bq), qmap))]
    return _call(
        body, name="fox_bwd" if stored else "fox_bwd_recompute",
        out_shape=(jax.ShapeDtypeStruct((nh, s // dq_blk, VT_ROWS, dq_blk), F32),
                   jax.ShapeDtypeStruct((nh, VT_ROWS, s), F32), jax.ShapeDtypeStruct((nh, HEAD_DIM, s), F32)),
        grid=(nh, s // bk, nq),
        in_specs=[first[0][1], first[1][1], pl.BlockSpec((1, AUG_DIM, bk), kmap_t),
                  pl.BlockSpec((1, bk, HEAD_DIM), lambda h, j, i: (h, j, 0)),
                  pl.BlockSpec((1, AUG_DIM, bq), qmap), pl.BlockSpec((1, HEAD_DIM, bq), qmap),
                  pl.BlockSpec((1, 1, bq), qmap)],
        out_specs=(pl.BlockSpec((1, s // dq_blk, VT_ROWS, dq_blk), lambda h, j, i: (h, 0, 0, 0)),
                   pl.BlockSpec((1, VT_ROWS, bk), kmap_t), pl.BlockSpec((1, HEAD_DIM, bk), kmap_t)),
        scratch_shapes=[pltpu.VMEM((VT_ROWS, bk), F32), pltpu.VMEM((HEAD_DIM, bk), F32)],
        compiler_params=_params(("parallel", "arbitrary", "arbitrary")),
    )(first[0][0], first[1][0], kat, v, qat, dot_, delta)


def _swa_mask(i, tq):
    kpos = i * tq - WINDOW + lax.broadcasted_iota(jnp.int32, (tq + WINDOW, tq), 0)
    qpos = i * tq + lax.broadcasted_iota(jnp.int32, (tq + WINDOW, tq), 1)
    rel = qpos - kpos
    return (rel >= 0) & (rel < WINDOW) & (kpos >= 0)


def _swa_rows(ref, i, tq):
    before = pl.multiple_of(jnp.maximum(i * tq - WINDOW, 0), WINDOW)
    return jnp.concatenate([ref[0, pl.ds(before, WINDOW), :], ref[0, pl.ds(pl.multiple_of(i * tq, tq), tq), :]], axis=0)


def _swa_before(n_rows, tq):
    return pl.BlockSpec((1, n_rows, WINDOW), lambda g, i: (g, 0, jnp.maximum(i * (tq // WINDOW) - 1, 0)))


def _swa_probs_t(sc, mask, sink):
    sc = jnp.where(mask, sc, NEG)
    m = jnp.maximum(jnp.max(sc, axis=0, keepdims=True), sink)
    p = jnp.exp(sc - m)
    e_sink = jnp.exp(sink - m)
    inv_l = 1.0 / (jnp.sum(p, axis=0, keepdims=True) + e_sink)
    return p * inv_l, e_sink * inv_l


def _swa_fwd(qbt, kb, vbt, sinks, tq):
    s = qbt.shape[1]
    gw = SWA_GROUP * HEAD_DIM

    def body(q_ref, k_ref, vb_ref, vc_ref, s_ref, o_ref):
        i = pl.program_id(1)
        mask = _swa_mask(i, tq)
        kw = _swa_rows(k_ref, i, tq)
        vtw = jnp.concatenate([vb_ref[0], vc_ref[0]], axis=1)
        sk = s_ref[0]
        scores = lambda hh: _dot(kw, q_ref[hh * HEAD_DIM:(hh + 1) * HEAD_DIM, :])
        sc_next = scores(0)
        for hh in range(SWA_GROUP):
            rows = slice(hh * HEAD_DIM, (hh + 1) * HEAD_DIM)
            sink = sk[:, hh:hh + 1]
            sc = jnp.where(mask, sc_next, NEG)
            if hh + 1 < SWA_GROUP:
                sc_next = scores(hh + 1)
            m = jnp.maximum(jnp.max(sc, axis=0, keepdims=True), sink)
            acc = _dot(vtw, jnp.exp(sc - m).astype(BF))
            o_ref[rows, :] = acc[:HEAD_DIM] / (acc[HEAD_DIM:HEAD_DIM + 1] + jnp.exp(sink - m))

    kvspec = pl.BlockSpec((1, s, HEAD_DIM), lambda g, i: (g, 0, 0))
    qspec = pl.BlockSpec((gw, tq), lambda g, i: (g, i))
    return _call(
        body, name="swa_fwd", out_shape=jax.ShapeDtypeStruct((SWA_W, s), F32), grid=(SWA_KV_HEADS, s // tq),
        in_specs=[qspec, kvspec, _swa_before(VT_ROWS, tq), pl.BlockSpec((1, VT_ROWS, tq), lambda g, i: (g, 0, i)),
                  pl.BlockSpec((1, 1, SWA_GROUP), lambda g, i: (g, 0, 0))],
        out_specs=qspec, compiler_params=_params(("parallel", "parallel")),
    )(qbt, kb, vbt, vbt, sinks)


def _swa_bwd(qb, qbt, kb, kbt, vb, sinks, dob, dobt, tq):
    s = qb.shape[0]
    gw = SWA_GROUP * HEAD_DIM

    def body(q_ref, qt_ref, k_ref, ktb_ref, ktc_ref, v_ref, s_ref, do_ref, dot_ref, dq_ref, dk_ref, dv_ref, ds_ref):
        i = pl.program_id(1)

        @pl.when(i == 0)
        def _():
            dk_ref[...] = jnp.zeros(dk_ref.shape, F32)
            dv_ref[...] = jnp.zeros(dv_ref.shape, F32)
            ds_ref[...] = jnp.zeros(ds_ref.shape, F32)

        mask = _swa_mask(i, tq)
        kw = _swa_rows(k_ref, i, tq)
        vw = _swa_rows(v_ref, i, tq)
        ktw = jnp.concatenate([ktb_ref[0], ktc_ref[0]], axis=1)
        qv, dov = q_ref[...], do_ref[...]
        sk = s_ref[0]
        dsinks = []
        dk_acc = jnp.zeros((tq + WINDOW, HEAD_DIM), F32)
        dv_acc = jnp.zeros((tq + WINDOW, HEAD_DIM), F32)
        def products(hh):
            rows = slice(hh * HEAD_DIM, (hh + 1) * HEAD_DIM)
            return _dot(kw, qt_ref[rows, :]), _dot(vw, dot_ref[rows, :])

        nxt = products(0)
        for hh in range(SWA_GROUP):
            rows = slice(hh * HEAD_DIM, (hh + 1) * HEAD_DIM)
            sc, dp = nxt
            if hh + 1 < SWA_GROUP:
                nxt = products(hh + 1)
            p, p_sink = _swa_probs_t(sc, mask, sk[:, hh:hh + 1])
            delta = jnp.sum(p * dp, axis=0, keepdims=True)
            dsc = (p * (dp - delta)).astype(BF)
            dq_ref[rows, :] = _dot(ktw, dsc)
            dk_acc = dk_acc + _dot(dsc, qv[:, rows])
            dv_acc = dv_acc + _dot(p.astype(BF), dov[:, rows])
            dsinks.append(-jnp.sum(p_sink * delta, axis=1, keepdims=True))
        before = pl.ds(pl.multiple_of(jnp.maximum(i * tq - WINDOW, 0), WINDOW), WINDOW)
        own = pl.ds(pl.multiple_of(i * tq, tq), tq)
        dk_ref[0, before, :] += dk_acc[:WINDOW]
        dk_ref[0, own, :] += dk_acc[WINDOW:]
        dv_ref[0, before, :] += dv_acc[:WINDOW]
        dv_ref[0, own, :] += dv_acc[WINDOW:]
        ds_ref[0] += jnp.concatenate(dsinks, axis=1)

    kvspec = pl.BlockSpec((1, s, HEAD_DIM), lambda g, i: (g, 0, 0))
    qspec = pl.BlockSpec((tq, gw), lambda g, i: (i, g))
    qspec_t = pl.BlockSpec((gw, tq), lambda g, i: (g, i))
    sspec = pl.BlockSpec((1, 1, SWA_GROUP), lambda g, i: (g, 0, 0))
    kvshape = jax.ShapeDtypeStruct((SWA_KV_HEADS, s, HEAD_DIM), F32)
    return _call(
        body, name="swa_bwd",
        out_shape=(jax.ShapeDtypeStruct((SWA_W, s), F32), kvshape, kvshape,
                   jax.ShapeDtypeStruct((SWA_KV_HEADS, 1, SWA_GROUP), F32)),
        grid=(SWA_KV_HEADS, s // tq),
        in_specs=[qspec, qspec_t, kvspec, _swa_before(HEAD_DIM, tq),
                  pl.BlockSpec((1, HEAD_DIM, tq), lambda g, i: (g, 0, i)), kvspec, sspec, qspec, qspec_t],
        out_specs=(qspec_t, kvspec, kvspec, sspec),
        compiler_params=_params(("parallel", "arbitrary")),
    )(qb, qbt, kb, kbt, kbt, vb, sinks, dob, dobt)


def _pairs_to_rows(ref, n_rows=HEAD_DIM):
    parts = []
    for a in range(0, FOX_HEADS, 2):
        parts.append(jnp.concatenate([ref[a][:n_rows], ref[a + 1][:n_rows]], axis=0).T)
    return jnp.concatenate(parts, axis=1)


def _blocks_to_rows(ref):
    return jnp.concatenate([ref[a:a + LANES, :].T for a in range(0, ref.shape[0], LANES)], axis=1)


def _out_proj(oat, za, obt, zb, x, tgt, w_out, w_out_t, gate, g_post, inv_l, tm):
    s = x.shape[0]

    def body(oat_ref, za_ref, obt_ref, zb_ref, x_ref, t_ref, w_ref, wt_ref, gate_ref, gp_ref, il_ref,
             dout_ref, doat_ref, dla_ref, dza_ref, dob_ref, dobt_ref, dzb_ref, gw_ref, dgate_ref, dgp_ref, loss_ref):
        i = pl.program_id(0)

        @pl.when(i == 0)
        def _():
            gw_ref[...] = jnp.zeros(gw_ref.shape, F32)
            dgate_ref[...] = jnp.zeros(dgate_ref.shape, F32)
            dgp_ref[...] = jnp.zeros(dgp_ref.shape, F32)
            loss_ref[...] = jnp.zeros(loss_ref.shape, F32)

        oa_v = _pairs_to_rows(oat_ref)
        ob_v = _blocks_to_rows(obt_ref)
        za_v, zb_v = za_ref[...], zb_ref[...]
        sga, sgb = _sigmoid(za_v), _sigmoid(zb_v)
        sila, silb = za_v * sga, zb_v * sgb
        u = jnp.concatenate([oa_v * sila, ob_v * silb], axis=1).astype(BF)
        yv = _dot(u, w_ref[...])
        yhat, rstd = _rms_hat(yv)
        gp, gate_v = gp_ref[...], gate_ref[...]
        nrm = yhat * gp
        diff = (x_ref[...] + gate_v * nrm) - t_ref[...]
        loss_ref[...] += 0.5 * jnp.sum(jnp.sum(diff * diff, axis=1, keepdims=True), axis=0, keepdims=True) / D_MODEL
        dout = diff * (1.0 / D_MODEL)
        dout_ref[...] = dout
        dgate_ref[...] += jnp.sum(dout * nrm, axis=0, keepdims=True)
        dn = dout * gate_v
        dgp_ref[...] += jnp.sum(dn * yhat, axis=0, keepdims=True)
        dyhat = dn * gp
        dy = (rstd * (dyhat - yhat * jnp.mean(dyhat * yhat, axis=1, keepdims=True))).astype(BF)
        gw_ref[...] += _dot_tn(u, dy)
        du = _dot(dy, wt_ref[...])
        dua, dub = du[:, :FOX_W], du[:, FOX_W:]
        doa = dua * sila
        for a in range(0, FOX_HEADS, 2):
            pair_t = doa[:, a * HEAD_DIM:(a + 2) * HEAD_DIM].T
            for hd, rows in ((a, slice(0, HEAD_DIM)), (a + 1, slice(HEAD_DIM, 2 * HEAD_DIM))):
                inv_l = il_ref[hd]
                doat_ref[hd] = (pair_t[rows] * inv_l).astype(BF)
                dla_ref[hd] = jnp.sum(pair_t[rows] * oat_ref[hd], axis=0, keepdims=True) * inv_l
        dob = dub * silb
        dob_ref[...] = dob.astype(BF)
        for a in range(0, SWA_W, LANES):
            dobt_ref[a:a + LANES, :] = dob[:, a:a + LANES].T.astype(BF)
        dza_ref[...] = (dua * oa_v * (sga * (1.0 + za_v * (1.0 - sga)))).astype(BF)
        dzb_ref[...] = (dub * ob_v * (sgb * (1.0 + zb_v * (1.0 - sgb)))).astype(BF)

    row = lambda w: pl.BlockSpec((tm, w), lambda i: (i, 0))
    heads_t = lambda w: pl.BlockSpec((FOX_HEADS, w, tm), lambda i: (0, 0, i))
    vec = _full((1, D_MODEL))
    mat = _full((D_MODEL, D_MODEL))
    out_shape = (
        jax.ShapeDtypeStruct((s, D_MODEL), F32),
        jax.ShapeDtypeStruct((FOX_HEADS, HEAD_DIM, s), BF), jax.ShapeDtypeStruct((FOX_HEADS, 1, s), F32),
        jax.ShapeDtypeStruct((s, FOX_W), BF), jax.ShapeDtypeStruct((s, SWA_W), BF), jax.ShapeDtypeStruct((SWA_W, s), BF),
        jax.ShapeDtypeStruct((s, SWA_W), BF),
        jax.ShapeDtypeStruct((D_MODEL, D_MODEL), F32),
        jax.ShapeDtypeStruct((1, D_MODEL), F32), jax.ShapeDtypeStruct((1, D_MODEL), F32),
        jax.ShapeDtypeStruct((1, 1), F32),
    )
    col = pl.BlockSpec((SWA_W, tm), lambda i: (0, i))
    return _call(
        body, name="out_proj", out_shape=out_shape, grid=(s // tm,),
        in_specs=[heads_t(HEAD_DIM), row(FOX_W), col, row(SWA_W), row(D_MODEL), row(D_MODEL), mat, mat, vec, vec,
                  heads_t(1)],
        out_specs=(row(D_MODEL), heads_t(HEAD_DIM), heads_t(1), row(FOX_W), row(SWA_W), col, row(SWA_W), mat, vec, vec,
                   _full((1, 1))),
        compiler_params=_params(("arbitrary",)),
    )(oat, za, obt, zb, x, tgt, w_out, w_out_t, gate, g_post, inv_l)


def _assemble_dproj(dqt, dkt, dvt, dza, dqb, dzb, dkb, dvb, df, cos_t, sin_t, tm):
    s = dza.shape[0]

    def body(dqt_ref, dkt_ref, dvt_ref, dza_ref, dqb_ref, dzb_ref, dkb_ref, dvb_ref, df_ref, cos_ref, sin_ref, o_ref):
        def cat(ref, n):
            return jnp.concatenate([ref[hd] for hd in range(n)], axis=1)

        cos2, sin2 = cos_ref[...], sin_ref[...]
        cos8 = jnp.concatenate([cos2] * 4, axis=1)
        sin8 = jnp.concatenate([sin2] * 4, axis=1)
        scale = HEAD_DIM ** -0.5
        o_ref[:, C_QA:C_QA + FOX_W] = (_pairs_to_rows(dqt_ref.at[:, 0]) * scale).astype(BF)
        o_ref[:, C_KA:C_KA + FOX_W] = (_pairs_to_rows(dkt_ref) * LN2).astype(BF)
        o_ref[:, C_VA:C_VA + FOX_W] = _pairs_to_rows(dvt_ref).astype(BF)
        o_ref[:, C_ZA:C_ZA + FOX_W] = dza_ref[...]
        dq = _blocks_to_rows(dqb_ref) * scale
        o_ref[:, C_QB:C_QB + SWA_W] = (dq * cos8 - _rope_partner(dq) * sin8).astype(BF)
        o_ref[:, C_ZB:C_ZB + SWA_W] = dzb_ref[...]
        dk = cat(dkb_ref, SWA_KV_HEADS)
        o_ref[:, C_KB:C_KB + SWA_KV_W] = (dk * cos2 - _rope_partner(dk) * sin2).astype(BF)
        o_ref[:, C_VB:C_VB + SWA_KV_W] = cat(dvb_ref, SWA_KV_HEADS).astype(BF)
        o_ref[:, C_F:C_F + LANES] = df_ref[...].astype(BF)

    row = lambda w: pl.BlockSpec((tm, w), lambda i: (i, 0))
    heads = lambda n: pl.BlockSpec((n, tm, HEAD_DIM), lambda i: (0, i, 0))
    heads_t = lambda w: pl.BlockSpec((FOX_HEADS, w, tm), lambda i: (0, 0, i))
    return _call(
        body, name="assemble_dproj", out_shape=jax.ShapeDtypeStruct((s, WP), BF), grid=(s // tm,),
        in_specs=[pl.BlockSpec((FOX_HEADS, 1, VT_ROWS, tm), lambda i: (0, i, 0, 0)), heads_t(VT_ROWS), heads_t(HEAD_DIM),
                  row(FOX_W), pl.BlockSpec((SWA_W, tm), lambda i: (0, i)), row(SWA_W), heads(SWA_KV_HEADS),
                  heads(SWA_KV_HEADS), row(LANES), row(LANES), row(LANES)],
        out_specs=row(WP), compiler_params=_params(("parallel",)),
    )(dqt, dkt, dvt, dza, dqb, dzb, dkb, dvb, df, cos_t, sin_t)


def _in_proj_bwd_x(dproj, w_al_t, x, dout, g_pre, scale1p, tm, parts):
    s = x.shape[0]
    n_steps = s // tm
    masks = [(1, 0), (0, 1), (1, 1)]

    def body(dp_ref, wt_ref, x_ref, dout_ref, g_ref, sc_ref, parts_ref, gx_ref, dsh_ref, dsc_ref, dg_ref, got_ref,
             send_sems, recv_sems, local_sem):
        i = pl.program_id(0)
        cx, cy, cc = lax.axis_index("x"), lax.axis_index("y"), lax.axis_index("c")
        me = 2 * cx + cy
        own = pltpu.make_async_copy(parts_ref.at[me], got_ref.at[me], local_sem)

        def copy(k, send):
            dx, dy = masks[k]
            peer = 2 * (cx ^ dx) + (cy ^ dy)
            return pltpu.make_async_remote_copy(
                src_ref=parts_ref.at[peer if send else me], dst_ref=got_ref.at[me if send else peer],
                send_sem=send_sems.at[k], recv_sem=recv_sems.at[k], device_id=(cx ^ dx, cy ^ dy, cc), device_id_type=MESH)

        @pl.when(i == 0)
        def _():
            dsh_ref[...] = jnp.zeros(dsh_ref.shape, F32)
            dsc_ref[...] = jnp.zeros(dsc_ref.shape, F32)
            dg_ref[...] = jnp.zeros(dg_ref.shape, F32)
            own.start()
            for k in range(len(masks)):
                copy(k, True).start()

        @pl.when(i == n_steps - 1)
        def _():
            for k in range(len(masks)):
                copy(k, False).wait_recv()
            for k in range(len(masks)):
                copy(k, True).wait_send()
            own.wait()

        dh = _dot(dp_ref[...], wt_ref[...])
        xhat, rstd = _rms_hat(x_ref[...])
        g, sc = g_ref[...], sc_ref[...]
        dsh_ref[...] += jnp.sum(dh, axis=0, keepdims=True)
        dhx = dh * xhat
        dsc_ref[...] += jnp.sum(dhx * g, axis=0, keepdims=True)
        dg_ref[...] += jnp.sum(dhx * sc, axis=0, keepdims=True)
        dxhat = dh * (g * sc)
        gx_ref[...] = dout_ref[...] + rstd * (dxhat - xhat * jnp.mean(dxhat * xhat, axis=1, keepdims=True))

    row = lambda w: pl.BlockSpec((tm, w), lambda i: (i, 0))
    vec = _full((1, D_MODEL))
    vshape = jax.ShapeDtypeStruct((1, D_MODEL), F32)
    hbm = pl.BlockSpec(memory_space=pl.ANY)
    return _call(
        body, name="in_proj_bwd_x",
        out_shape=(jax.ShapeDtypeStruct((s, D_MODEL), F32), vshape, vshape, vshape,
                   jax.ShapeDtypeStruct(parts.shape, parts.dtype)),
        grid=(n_steps,),
        in_specs=[row(WP), _full((WP, D_MODEL)), row(D_MODEL), row(D_MODEL), vec, vec, hbm],
        out_specs=(row(D_MODEL), vec, vec, vec, hbm),
        scratch_shapes=[pltpu.SemaphoreType.DMA((3,)), pltpu.SemaphoreType.DMA((3,)), pltpu.SemaphoreType.DMA],
        compiler_params=_params(("arbitrary",), has_side_effects=True),
    )(dproj, w_al_t, x, dout, g_pre, scale1p, parts)


def _in_proj_bwd_w(h, dproj, tk, tn):
    s = h.shape[0]

    def body(h_ref, dp_ref, gw_ref):
        @pl.when(pl.program_id(1) == 0)
        def _():
            gw_ref[...] = jnp.zeros(gw_ref.shape, F32)

        gw_ref[...] += _dot_tn(h_ref[...], dp_ref[...])

    return _call(
        body, name="in_proj_bwd_w", out_shape=jax.ShapeDtypeStruct((D_MODEL, WP), F32), grid=(WP // tn, s // tk),
        in_specs=[pl.BlockSpec((tk, D_MODEL), lambda n, k: (k, 0)), pl.BlockSpec((tk, tn), lambda n, k: (k, n))],
        out_specs=pl.BlockSpec((D_MODEL, tn), lambda n, k: (0, n)),
        compiler_params=_params(("parallel", "arbitrary")),
    )(h, dproj)


def _align_w_in(w_cols):
    def part(name, width):
        return w_cols[:, _SRC[name]:_SRC[name] + width]

    fpad = jnp.pad(part("fa", FOX_HEADS), ((0, 0), (0, LANES - FOX_HEADS)))
    return jnp.concatenate([part("qa", FOX_W), part("ka", FOX_W), part("va", FOX_W), part("za", FOX_W),
                            part("qb", SWA_W), part("zb", SWA_W), part("kb", SWA_KV_W), part("vb", SWA_KV_W), fpad], axis=1)


def _unalign_w_in(g_al):
    def part(c0, width):
        return g_al[:, c0:c0 + width]

    return jnp.concatenate([part(C_QA, FOX_W), part(C_KA, FOX_W), part(C_VA, FOX_W), part(C_F, FOX_HEADS),
                            part(C_ZA, FOX_W), part(C_QB, SWA_W), part(C_KB, SWA_KV_W), part(C_VB, SWA_KV_W),
                            part(C_ZB, SWA_W)], axis=1)


def _rope_tables(positions):
    inv_freq = ROPE_THETA ** (-jnp.arange(HALF, dtype=F32) / HALF)
    ang = positions.astype(F32)[:, None] * inv_freq
    cos, sin = jnp.cos(ang), jnp.sin(ang)
    return jnp.concatenate([cos, cos, cos, cos], axis=1), jnp.concatenate([-sin, sin, -sin, sin], axis=1)


def _tiles(s):
    if s >= 4096:
        return dict(tm=512, blk=512, bq=2048, bk=1024, bk_bwd=512, chunk=256, tq=256, tm_out=512, tk=1024, tn=1152)
    return dict(tm=128, blk=128, bq=256, bk=128, bk_bwd=128, chunk=128, tq=128, tm_out=128, tk=128, tn=1152)


def kernel(x, c, positions, w_ada, b_ada, g_pre, w_in, b_fgate, sinks, w_out, g_post, loss_target, m_w_ada, m_b_ada, m_g_pre, m_w_in, m_b_fgate, m_sinks, m_w_out, m_g_post, v_w_ada, v_b_ada, v_g_pre, v_w_in, v_b_fgate, v_sinks, v_w_out, v_g_post):
    s = x.shape[1]
    t = _tiles(s)
    nc = s // LANES
    rows = FOX_HEADS * nc
    me = 4 * lax.axis_index("x") + 2 * lax.axis_index("y") + lax.axis_index("c")
    chip = 2 * lax.axis_index("x") + lax.axis_index("y")
    core = lax.axis_index("c")
    x2, tgt = x[0], loss_target[0]

    c_all = _allgather_devices(c, "gather_c")[:, 0, :]
    a_all, mod_shard = _ada_shard(c_all, w_ada[0])
    mod_all = _allgather_devices(mod_shard, "gather_mod")
    mod_rows = lax.dynamic_index_in_dim(mod_all, me, axis=1, keepdims=False)
    mod = mod_rows.reshape(N_CHIPS, 2, W_ADA_SHARD)[:, 0, :].reshape(1, 3 * D_MODEL) + b_ada
    shift, scale1p, gate = mod[:, :D_MODEL], 1.0 + mod[:, D_MODEL:2 * D_MODEL], mod[:, 2 * D_MODEL:]

    w_in_pad = jnp.pad(w_in[0].astype(BF), ((0, 0), (0, W_IN_SHARD_PAD - W_IN_SHARD)))
    w_pack = jnp.concatenate([w_in_pad, w_out[0].astype(BF).reshape(D_MODEL, W_OUT_SHARD)], axis=1)
    w_all = _allgather_chips(w_pack.reshape(2, D_MODEL // 2, -1), "gather_weights").reshape(N_CHIPS, D_MODEL, -1)
    w_cols = jnp.concatenate([w_all[k, :, :W_IN_SHARD] for k in range(N_CHIPS)], axis=1)
    w_al = _align_w_in(w_cols)
    w_al_t = w_al.T
    w_out_all = w_all[:, :, W_IN_SHARD_PAD:].reshape(D_MODEL, D_MODEL)
    w_out_t = w_out_all.T

    cos_t, sin_t = _rope_tables(positions[0])

    f_pad = _forget_logits(x2, g_pre, scale1p, shift, w_al[:, C_F:], t["tm"])
    f_rows = f_pad[:, :FOX_HEADS].T.reshape(rows, LANES)
    bias_rows = jnp.repeat(b_fgate[0], nc)[:, None]
    cum = _log_forget_cumsum(f_rows, bias_rows, nc).reshape(FOX_HEADS, s)
    h, qat, ka, kat, va, vat, za, zb, qb, kb, vb, qbt, kbt, vbt, m_own = _in_proj(
        x2, g_pre, scale1p, shift, w_al[:, C_VA:C_F], w_al_t[:C_ZA], cum, cos_t, sin_t, t["tm"])
    m_own = m_own[:, None, :]
    fox_args = (qat, ka, vat, m_own, t["bq"], t["bk"], t["chunk"])
    oat, lse, bad, pt = _fox_fwd(*fox_args, running_max=False)
    overflowed = jnp.max(bad) > 0.0
    oat, lse = lax.cond(overflowed, lambda: _fox_fwd(*fox_args, running_max=True)[:2], lambda: (oat, lse))
    inv_l = jnp.where(overflowed, 1.0, jnp.exp2(m_own - lse))
    sinks_g = sinks.reshape(SWA_KV_HEADS, 1, SWA_GROUP)
    obt = _swa_fwd(qbt, kb, vbt, sinks_g, t["tq"])

    dout, doat, delta_a, dza, dob, dobt, dzb, gw_out, dgate, dg_post, loss_part = _out_proj(
        oat, za, obt, zb, x2, tgt, w_out_all, w_out_t, gate, g_post, inv_l, t["tm_out"])

    bwd_args = (qat, ka, kat, va, doat, lse, delta_a)
    bwd_tiles = (t["bq"], t["bk_bwd"], t["chunk"], t["blk"])
    dqt, dkt, dvt = lax.cond(overflowed, lambda: _fox_bwd(*bwd_args, None, *bwd_tiles),
                             lambda: _fox_bwd(*bwd_args, pt, *bwd_tiles))
    dcum = dqt[:, :, HEAD_DIM, :].reshape(FOX_HEADS, s) - dkt[:, HEAD_DIM, :]
    df_rows, db_heads = _log_forget_cumsum_bwd(dcum.reshape(rows, LANES), f_rows, bias_rows, nc)
    df_pad = jnp.pad(df_rows.reshape(FOX_HEADS, s).T, ((0, 0), (0, LANES - FOX_HEADS)))
    dqb, dkb, dvb, dsinks = _swa_bwd(qb, qbt, kb, kbt, vb, sinks_g, dob, dobt, t["tq"])

    dproj = _assemble_dproj(dqt, dkt, dvt, dza, dqb, dzb, dkb, dvb, df_pad, cos_t, sin_t, t["blk"])
    gw_in = _unalign_w_in(_in_proj_bwd_w(h, dproj, t["tk"], t["tn"]))

    gin = jnp.pad(gw_in.reshape(D_MODEL, N_CHIPS, W_IN_SHARD).transpose(1, 0, 2),
                  ((0, 0), (0, 0), (0, W_IN_SHARD_PAD - W_IN_SHARD)))
    gout = gw_out.reshape(N_CHIPS, D_MODEL, W_OUT_SHARD)
    gbig = jnp.concatenate([gin, gout], axis=2)
    half = D_MODEL // 2
    gw = W_IN_SHARD_PAD + W_OUT_SHARD
    keep = lax.dynamic_slice_in_dim(gbig, core * half, half, axis=1)
    give = lax.dynamic_slice_in_dim(gbig, (1 - core) * half, half, axis=1)
    got = _swap_sibling(give.reshape(N_CHIPS * half, gw), "swap_grad_halves")
    pair = _add(keep.reshape(N_CHIPS * half, gw), got, "add_pair", BF).reshape(N_CHIPS, half, gw)
    grad_x, dshift, dscale, dg_pre, from_chips = _in_proj_bwd_x(
        dproj, w_al_t, x2, dout, g_pre, scale1p, t["tm_out"], pair)

    pad_lane = lambda vrow: jnp.pad(vrow, ((0, 0), (0, LANES - vrow.shape[1])))
    packed = jnp.concatenate([dshift, dscale, dgate, dg_pre, dg_post,
                              pad_lane(db_heads.reshape(1, FOX_HEADS)), pad_lane(dsinks.reshape(1, FOX_HEADS)),
                              pad_lane(loss_part)], axis=1)
    parts = _allgather_devices(packed, "gather_partials")
    tot = _sum_devices(parts)
    loss = tot[0, P_LOSS]
    g_b_ada = tot[:, P_DMOD:P_DMOD + 3 * D_MODEL]
    g_g_pre = tot[:, P_GPRE:P_GPRE + D_MODEL]
    g_g_post = tot[:, P_GPOST:P_GPOST + D_MODEL]
    g_b_fgate = tot[:, P_BF:P_BF + FOX_HEADS]
    g_sinks = tot[:, P_SINK:P_SINK + FOX_HEADS]
    dm_shard = lax.dynamic_slice_in_dim(parts[:, 0, :3 * D_MODEL], chip * W_ADA_SHARD, W_ADA_SHARD, axis=1)
    g_w_ada = _grad_w_ada(a_all.T, dm_shard)

    mine = _sum_chips(from_chips, "sum_chips")
    other = _swap_sibling(mine, "swap_grad_result")
    lo = jnp.where(core == 0, mine, other)
    hi = jnp.where(core == 0, other, mine)
    gfull = jnp.concatenate([lo, hi], axis=0)
    g_w_in = gfull[:, :W_IN_SHARD]
    g_w_out = gfull[:, W_IN_SHARD_PAD:].reshape(W_OUT_SHARD, D_MODEL)

    grads = dict(w_ada=g_w_ada, b_ada=g_b_ada, g_pre=g_g_pre, w_in=g_w_in, b_fgate=g_b_fgate, sinks=g_sinks,
                 w_out=g_w_out, g_post=g_g_post)
    weights = dict(w_ada=w_ada, b_ada=b_ada, g_pre=g_pre, w_in=w_in, b_fgate=b_fgate, sinks=sinks, w_out=w_out, g_post=g_post)
    moms = dict(w_ada=m_w_ada, b_ada=m_b_ada, g_pre=m_g_pre, w_in=m_w_in, b_fgate=m_b_fgate, sinks=m_sinks, w_out=m_w_out, g_post=m_g_post)
    vars_ = dict(w_ada=v_w_ada, b_ada=v_b_ada, g_pre=v_g_pre, w_in=v_w_in, b_fgate=v_b_fgate, sinks=v_sinks, w_out=v_w_out, g_post=v_g_post)
    names = ["w_ada", "b_ada", "g_pre", "w_in", "b_fgate", "sinks", "w_out", "g_post"]
    g_out, d_out, m_out, v_out = [], [], [], []
    for n in names:
        g2 = grads[n].reshape(weights[n].shape[-2:])
        go, d, nm, nv = _adamw(weights[n], g2, moms[n], vars_[n], "adamw_" + n)
        g_out.append(go)
        d_out.append(d)
        m_out.append(nm)
        v_out.append(nv)
    return (loss, grad_x.reshape(x.shape), *g_out, *d_out, *m_out, *v_out)
```

```python
import functools

import jax
import jax.numpy as jnp
from jax import lax
from jax.experimental import pallas as pl
from jax.experimental.pallas import tpu as pltpu

_INTERPRET = False

D_MODEL = 1024
HEAD_DIM = 64
HALF = HEAD_DIM // 2
AUG_DIM = 128
AUG_ROWS = 8
VT_ROWS = 80
LOG2E = 1.4426950408889634
LN2 = 0.6931471805599453
Q_SCALE = LOG2E * 64 ** -0.5
FOX_HEADS = 8
FOX_W = 512
SWA_W = 512
SWA_KV_HEADS = 2
SWA_GROUP = 4
SWA_KV_W = 128
WINDOW = 128
ROPE_THETA = 10000.0
RMS_EPS = 1e-6
IN_WIDTH = 3336
N_CHIPS = 4
N_DEV = 8
W_IN_SHARD = IN_WIDTH // N_CHIPS
W_IN_SHARD_PAD = 896
W_ADA_SHARD = 3 * D_MODEL // N_CHIPS
W_OUT_SHARD = D_MODEL // N_CHIPS
LANES = 128

_SRC = dict(qa=0, ka=512, va=1024, fa=1536, za=1544, qb=2056, kb=2568, vb=2696, zb=2824)
C_QA, C_KA, C_VA, C_ZA, C_QB, C_ZB, C_KB, C_VB, C_F = 0, 512, 1024, 1536, 2048, 2560, 3072, 3200, 3328
WP = 3456

ADAM_LR = 0.001
ADAM_B1 = 0.9
ADAM_B2 = 0.999
ADAM_EPS = 1e-08
ADAM_WD = 0.01
ADAM_STEP = 10

VMEM_LIMIT = 56 * 1024 * 1024
NEG = -1e30
OVERFLOW_GUARD = 1e30
MESH = pl.DeviceIdType.MESH
BF = jnp.bfloat16
F32 = jnp.float32

P_DMOD, P_GPRE, P_GPOST, P_BF, P_SINK, P_LOSS, P_LEN = 0, 3072, 4096, 5120, 5248, 5376, 5504


def _call(body, **kw):
    return pl.pallas_call(body, interpret=_INTERPRET, **kw)


def _params(sem=None, **kw):
    return pltpu.CompilerParams(dimension_semantics=sem, vmem_limit_bytes=VMEM_LIMIT, **kw)


def _full(shape):
    zeros = (0,) * len(shape)
    return pl.BlockSpec(shape, lambda *_: zeros)


def _dot(a, b):
    return jnp.dot(a, b, preferred_element_type=F32)


def _dot_nt(a, b):
    return lax.dot_general(a, b, (((1,), (1,)), ((), ())), preferred_element_type=F32)


def _dot_tn(a, b):
    return lax.dot_general(a, b, (((0,), (0,)), ((), ())), preferred_element_type=F32)


def _sigmoid(z):
    return 1.0 / (1.0 + jnp.exp(-z))


def _rope_partner(t):
    w = t.shape[-1]
    lane = lax.broadcasted_iota(jnp.int32, t.shape, t.ndim - 1)
    return jnp.where((lane & (HEAD_DIM - 1)) < HALF, pltpu.roll(t, w - HALF, t.ndim - 1), pltpu.roll(t, HALF, t.ndim - 1))


def _allgather_devices(v, name):
    r, cdim = v.shape
    masks = [(dx, dy, dc) for dx in (0, 1) for dy in (0, 1) for dc in (0, 1)][1:]

    def body(v_ref, out_ref, send_sems, recv_sems):
        x, y, c = lax.axis_index("x"), lax.axis_index("y"), lax.axis_index("c")
        me = 4 * x + 2 * y + c
        out_ref[me] = v_ref[...]
        copies = []
        for k, (dx, dy, dc) in enumerate(masks):
            cp = pltpu.make_async_remote_copy(
                src_ref=v_ref, dst_ref=out_ref.at[me], send_sem=send_sems.at[k], recv_sem=recv_sems.at[k],
                device_id=(x ^ dx, y ^ dy, c ^ dc), device_id_type=MESH)
            cp.start()
            copies.append(cp)
        for k, (dx, dy, dc) in enumerate(masks):
            peer = 4 * (x ^ dx) + 2 * (y ^ dy) + (c ^ dc)
            pltpu.make_async_remote_copy(
                src_ref=v_ref, dst_ref=out_ref.at[peer], send_sem=send_sems.at[k], recv_sem=recv_sems.at[k],
                device_id=(x ^ dx, y ^ dy, c ^ dc), device_id_type=MESH).wait_recv()
        for cp in copies:
            cp.wait_send()

    return _call(
        body, name=name, out_shape=jax.ShapeDtypeStruct((N_DEV, r, cdim), v.dtype),
        in_specs=[pl.BlockSpec(memory_space=pltpu.VMEM)], out_specs=pl.BlockSpec(memory_space=pltpu.VMEM),
        scratch_shapes=[pltpu.SemaphoreType.DMA((7,)), pltpu.SemaphoreType.DMA((7,))],
        compiler_params=pltpu.CompilerParams(has_side_effects=True),
    )(v)


def _allgather_chips(v, name):
    _, r, cdim = v.shape
    masks = [(1, 0), (0, 1), (1, 1)]
    n = len(masks)

    def body(v_ref, out_ref, send_sems, recv_sems, local_sem):
        x, y, c = lax.axis_index("x"), lax.axis_index("y"), lax.axis_index("c")
        me = 2 * x + y
        mine = pltpu.make_async_copy(v_ref, out_ref.at[me], local_sem)
        mine.start()

        def copy(k, chip, half, to):
            return pltpu.make_async_remote_copy(
                src_ref=v_ref.at[half] if k < n else out_ref.at[chip, half], dst_ref=out_ref.at[chip, half],
                send_sem=send_sems.at[k], recv_sem=recv_sems.at[k], device_id=to, device_id_type=MESH)

        first = [copy(k, me, c, (x ^ dx, y ^ dy, c)) for k, (dx, dy) in enumerate(masks)]
        for cp in first:
            cp.start()
        passed = []
        for k, (dx, dy) in enumerate(masks):
            peer = 2 * (x ^ dx) + (y ^ dy)
            copy(k, peer, c, (x, y, c)).wait_recv()
            cp = copy(n + k, peer, c, (x, y, 1 - c))
            cp.start()
            passed.append(cp)
        for k, (dx, dy) in enumerate(masks):
            copy(n + k, 2 * (x ^ dx) + (y ^ dy), 1 - c, (x, y, c)).wait_recv()
        for cp in first + passed:
            cp.wait_send()
        mine.wait()

    return _call(
        body, name=name, out_shape=jax.ShapeDtypeStruct((N_CHIPS, 2, r, cdim), v.dtype),
        in_specs=[pl.BlockSpec(memory_space=pl.ANY)], out_specs=pl.BlockSpec(memory_space=pl.ANY),
        scratch_shapes=[pltpu.SemaphoreType.DMA((2 * n,)), pltpu.SemaphoreType.DMA((2 * n,)), pltpu.SemaphoreType.DMA],
        compiler_params=pltpu.CompilerParams(has_side_effects=True),
    )(v)


def _swap_sibling(v, name):
    def body(v_ref, out_ref, send_sem, recv_sem):
        x, y, c = lax.axis_index("x"), lax.axis_index("y"), lax.axis_index("c")
        cp = pltpu.make_async_remote_copy(
            src_ref=v_ref, dst_ref=out_ref, send_sem=send_sem, recv_sem=recv_sem,
            device_id=(x, y, 1 - c), device_id_type=MESH)
        cp.start()
        cp.wait()

    return _call(
        body, name=name, out_shape=jax.ShapeDtypeStruct(v.shape, v.dtype),
        in_specs=[pl.BlockSpec(memory_space=pl.ANY)], out_specs=pl.BlockSpec(memory_space=pl.ANY),
        scratch_shapes=[pltpu.SemaphoreType.DMA, pltpu.SemaphoreType.DMA],
        compiler_params=pltpu.CompilerParams(has_side_effects=True),
    )(v)


def _ada_shard(c_all, w_ada_shard):
    def body(c_ref, w_ref, a_ref, mod_ref):
        cv = c_ref[...]
        a = cv * _sigmoid(cv)
        a_ref[...] = a
        mod_ref[...] = _dot(a.astype(BF), w_ref[...].astype(BF))

    return _call(
        body, name="ada_shard",
        out_shape=(jax.ShapeDtypeStruct((N_DEV, D_MODEL), F32), jax.ShapeDtypeStruct((N_DEV, W_ADA_SHARD), F32)),
        compiler_params=_params(),
    )(c_all, w_ada_shard)


def _grad_w_ada(a_t, dm_shard):
    def body(a_ref, dm_ref, out_ref):
        acc = jnp.zeros((D_MODEL, W_ADA_SHARD), F32)
        for b in range(N_DEV):
            acc = acc + a_ref[:, b:b + 1] * dm_ref[b:b + 1, :]
        out_ref[...] = acc

    return _call(body, name="grad_w_ada", out_shape=jax.ShapeDtypeStruct((D_MODEL, W_ADA_SHARD), F32),
                 compiler_params=_params())(a_t, dm_shard)


def _sum_devices(parts):
    n = parts.shape[-1]

    def body(p_ref, out_ref):
        acc = p_ref[0]
        for b in range(1, N_DEV):
            acc = acc + p_ref[b]
        out_ref[...] = acc

    return _call(body, name="sum_devices", out_shape=jax.ShapeDtypeStruct((1, n), F32), compiler_params=_params())(parts)


def _add(a, b, name, out_dtype):
    r, cdim = a.shape
    tr = min(r, 256)

    def body(a_ref, b_ref, o_ref):
        o_ref[...] = (a_ref[...] + b_ref[...]).astype(out_dtype)

    spec = pl.BlockSpec((tr, cdim), lambda i: (i, 0))
    return _call(body, name=name, out_shape=jax.ShapeDtypeStruct(a.shape, out_dtype), grid=(r // tr,),
                 in_specs=[spec, spec], out_specs=spec, compiler_params=_params(("parallel",)))(a, b)


def _sum_chips(parts, name):
    _, r, cdim = parts.shape
    tr = min(r, 128)

    def body(p_ref, o_ref):
        o_ref[...] = ((p_ref[0].astype(F32) + p_ref[1].astype(F32)) + p_ref[2].astype(F32)) + p_ref[3].astype(F32)

    return _call(body, name=name, out_shape=jax.ShapeDtypeStruct((r, cdim), F32), grid=(r // tr,),
                 in_specs=[pl.BlockSpec((N_CHIPS, tr, cdim), lambda i: (0, i, 0))],
                 out_specs=pl.BlockSpec((tr, cdim), lambda i: (i, 0)), compiler_params=_params(("parallel",)))(parts)


def _adamw(w, g, m, v, name):
    r, cdim = w.shape[-2:]
    lead = w.ndim - 2
    tr = r if r <= 256 else 256
    c1 = 1.0 / (1.0 - ADAM_B1 ** ADAM_STEP)
    c2 = 1.0 / (1.0 - ADAM_B2 ** ADAM_STEP)

    def body(w_ref, g_ref, m_ref, v_ref, go_ref, d_ref, nm_ref, nv_ref):
        gv = g_ref[...].reshape(go_ref.shape)
        nm = ADAM_B1 * m_ref[...] + (1.0 - ADAM_B1) * gv
        nv = ADAM_B2 * v_ref[...] + (1.0 - ADAM_B2) * (gv * gv)
        m_hat = nm * c1
        v_hat = nv * c2
        go_ref[...] = gv
        d_ref[...] = -ADAM_LR * (m_hat / (jnp.sqrt(v_hat) + ADAM_EPS) + ADAM_WD * w_ref[...])
        nm_ref[...] = nm
        nv_ref[...] = nv

    spec = pl.BlockSpec((1,) * lead + (tr, cdim), lambda i: (0,) * lead + (i, 0))
    shp = jax.ShapeDtypeStruct(w.shape, F32)
    return _call(body, name=name, out_shape=(shp,) * 4, grid=(r // tr,),
                 in_specs=[spec, pl.BlockSpec((tr, cdim), lambda i: (i, 0)), spec, spec],
                 out_specs=(spec,) * 4, compiler_params=_params(("parallel",)))(w, g, m, v)


def _head_of_row(r, nc):
    assert nc & (nc - 1) == 0
    return lax.shift_right_logical(r, nc.bit_length() - 1)


def _chunk_mats(rows, nc, reverse):
    ri = lax.broadcasted_iota(jnp.int32, (rows, rows), 0)
    ci = lax.broadcasted_iota(jnp.int32, (rows, rows), 1)
    same = _head_of_row(ri, nc) == _head_of_row(ci, nc)
    between = jnp.where(same & ((ci > ri) if reverse else (ci < ri)), 1.0, 0.0).astype(F32)
    li = lax.broadcasted_iota(jnp.int32, (LANES, LANES), 0)
    lj = lax.broadcasted_iota(jnp.int32, (LANES, LANES), 1)
    within = jnp.where((li >= lj) if reverse else (li <= lj), 1.0, 0.0).astype(F32)
    return between, within


def _dot_hi(a, b):
    return jnp.dot(a, b, preferred_element_type=F32, precision=lax.Precision.HIGHEST)


def _scan_rows(t, nc, reverse):
    between, within = _chunk_mats(t.shape[0], nc, reverse)
    inner = _dot_hi(t, within)
    tot = jnp.sum(t, axis=1, keepdims=True)
    return inner + _dot_hi(between, jnp.broadcast_to(tot, t.shape))


def _log_forget_cumsum(f_rows, bias_rows, nc):
    def body(f_ref, b_ref, cum_ref):
        z = f_ref[...] + b_ref[...]
        lf = jnp.minimum(z, 0.0) - jnp.log(1.0 + jnp.exp(-jnp.abs(z)))
        cum_ref[...] = _scan_rows(lf, nc, False)

    return _call(body, name="forget_cumsum", out_shape=jax.ShapeDtypeStruct(f_rows.shape, F32),
                 compiler_params=_params())(f_rows, bias_rows)


def _log_forget_cumsum_bwd(dcum_rows, f_rows, bias_rows, nc):
    rows = f_rows.shape[0]

    def body(d_ref, f_ref, b_ref, df_ref, db_ref):
        dlf = _scan_rows(d_ref[...], nc, True)
        z = f_ref[...] + b_ref[...]
        df = dlf * _sigmoid(-z)
        df_ref[...] = df
        hi = lax.broadcasted_iota(jnp.int32, (FOX_HEADS, rows), 0)
        ri = lax.broadcasted_iota(jnp.int32, (FOX_HEADS, rows), 1)
        sel = jnp.where(_head_of_row(ri, nc) == hi, 1.0, 0.0).astype(F32)
        db_ref[...] = jnp.sum(_dot_hi(sel, df), axis=1, keepdims=True)

    return _call(body, name="forget_cumsum_bwd",
                 out_shape=(jax.ShapeDtypeStruct(f_rows.shape, F32), jax.ShapeDtypeStruct((FOX_HEADS, 1), F32)),
                 compiler_params=_params())(dcum_rows, f_rows, bias_rows)


def _rms_hat(xv):
    rstd = lax.rsqrt(jnp.mean(xv * xv, axis=-1, keepdims=True) + RMS_EPS)
    return xv * rstd, rstd


def _modulated(x_ref, g_ref, sc_ref, sh_ref):
    xhat, _ = _rms_hat(x_ref[...])
    return ((xhat * g_ref[...]) * sc_ref[...] + sh_ref[...]).astype(BF)


def _forget_logits(x, g_pre, scale1p, shift, w_f, tm):
    s = x.shape[0]

    def body(x_ref, g_ref, sc_ref, sh_ref, w_ref, f_ref):
        f_ref[...] = _dot(_modulated(x_ref, g_ref, sc_ref, sh_ref), w_ref[...])

    vec = _full((1, D_MODEL))
    return _call(
        body, name="forget_logits", out_shape=jax.ShapeDtypeStruct((s, LANES), F32), grid=(s // tm,),
        in_specs=[pl.BlockSpec((tm, D_MODEL), lambda i: (i, 0)), vec, vec, vec, _full((D_MODEL, LANES))],
        out_specs=pl.BlockSpec((tm, LANES), lambda i: (i, 0)), compiler_params=_params(("parallel",)),
    )(x, g_pre, scale1p, shift, w_f)


def _split3(v):
    hi = v.astype(BF).astype(F32)
    mid = (v - hi).astype(BF).astype(F32)
    lo = ((v - hi) - mid).astype(BF).astype(F32)
    return hi, mid, lo


def _in_proj(x, g_pre, scale1p, shift, w_rows, w_t_fox, cum, cos_t, sin_t, tm):
    s = x.shape[0]
    r_va, r_za, r_qb, r_zb, r_kb, r_vb = 0, 512, 1024, 1536, 2048, 2176

    def body(x_ref, g_ref, sc_ref, sh_ref, w_ref, wt_ref, cum_ref, cos_ref, sin_ref,
             h_ref, qat_ref, ka_ref, kat_ref, v_ref, vt_ref, za_ref, zb_ref, qb_ref, kb_ref, vb_ref,
             qbt_ref, kbt_ref, vbt_ref, mo_ref):
        hb = _modulated(x_ref, g_ref, sc_ref, sh_ref)
        h_ref[...] = hb

        def sec(c0, width):
            return _dot(hb, w_ref[:, c0:c0 + width])

        def sec_t(r0):
            return _dot_nt(wt_ref[r0:r0 + FOX_W, :], hb)

        q_t = sec_t(0) * Q_SCALE
        k_t = sec_t(FOX_W)
        v_t = sec_t(2 * FOX_W)
        va = sec(r_va, FOX_W)
        zeros = jnp.zeros((AUG_DIM - HEAD_DIM - AUG_ROWS, tm), F32)
        ri = lax.broadcasted_iota(jnp.int32, (AUG_ROWS, tm), 0)
        const = jnp.where(ri == AUG_ROWS - 1, 0.0, 1.0)
        ri_v = lax.broadcasted_iota(jnp.int32, (VT_ROWS - HEAD_DIM, tm), 0)
        v_feat = jnp.where(ri_v == 0, 1.0, 0.0).astype(BF)
        for hd in range(FOX_HEADS):
            rows = slice(hd * HEAD_DIM, (hd + 1) * HEAD_DIM)
            cum2 = cum_ref[hd:hd + 1, :] * LOG2E
            hi, mid, lo = (jnp.broadcast_to(part, (AUG_ROWS, tm)) for part in _split3(cum2))
            q_feat = jnp.where(ri == 1, hi, jnp.where(ri == 2, mid, jnp.where(ri == 3, lo, const)))
            k_feat = jnp.where(ri == 4, -hi, jnp.where(ri == 5, -mid, jnp.where(ri == 6, -lo, const)))
            q_aug = jnp.concatenate([q_t[rows], q_feat, zeros], axis=0)
            k_aug = jnp.concatenate([k_t[rows], k_feat, zeros], axis=0)
            mo_ref[hd:hd + 1, :] = jnp.sum(q_t[rows] * k_t[rows], axis=0, keepdims=True) + 1.0
            qat_ref[hd] = q_aug.astype(BF)
            kat_ref[hd] = k_aug.astype(BF)
            ka_ref[hd] = k_aug.T.astype(BF)
            vt_ref[hd] = jnp.concatenate([v_t[rows].astype(BF), v_feat], axis=0)
            v_ref[hd] = va[:, rows].astype(BF)
        za_ref[...] = sec(r_za, FOX_W)
        zb_ref[...] = sec(r_zb, SWA_W)
        cos2, sin2 = cos_ref[...], sin_ref[...]
        cos8 = jnp.concatenate([cos2] * 4, axis=1)
        sin8 = jnp.concatenate([sin2] * 4, axis=1)
        qb = sec(r_qb, SWA_W)
        qb = (qb * cos8 + _rope_partner(qb) * sin8) * (HEAD_DIM ** -0.5)
        qb_ref[...] = qb.astype(BF)
        for a in range(SWA_W // LANES):
            qbt_ref[a * LANES:(a + 1) * LANES, :] = qb[:, a * LANES:(a + 1) * LANES].T.astype(BF)
        kb = sec(r_kb, SWA_KV_W)
        kb = kb * cos2 + _rope_partner(kb) * sin2
        vb = sec(r_vb, SWA_KV_W)
        kb_t, vb_t = kb.T, vb.T
        for hd in range(SWA_KV_HEADS):
            sl = slice(hd * HEAD_DIM, (hd + 1) * HEAD_DIM)
            kb_ref[hd] = kb[:, sl].astype(BF)
            vb_ref[hd] = vb[:, sl].astype(BF)
            kbt_ref[hd] = kb_t[sl].astype(BF)
            vbt_ref[hd] = jnp.concatenate([vb_t[sl].astype(BF), v_feat], axis=0)

    row = lambda w: pl.BlockSpec((tm, w), lambda i: (i, 0))
    heads = lambda n, w=HEAD_DIM: pl.BlockSpec((n, tm, w), lambda i: (0, i, 0))
    heads_t = lambda w: pl.BlockSpec((FOX_HEADS, w, tm), lambda i: (0, 0, i))
    vec = _full((1, D_MODEL))
    hs = lambda a, b: jax.ShapeDtypeStruct((FOX_HEADS, a, b), BF)
    out_shape = (
        jax.ShapeDtypeStruct((s, D_MODEL), BF),
        hs(AUG_DIM, s), hs(s, AUG_DIM), hs(AUG_DIM, s), hs(s, HEAD_DIM), hs(VT_ROWS, s),
        jax.ShapeDtypeStruct((s, FOX_W), F32), jax.ShapeDtypeStruct((s, SWA_W), F32),
        jax.ShapeDtypeStruct((s, SWA_W), BF),
        jax.ShapeDtypeStruct((SWA_KV_HEADS, s, HEAD_DIM), BF), jax.ShapeDtypeStruct((SWA_KV_HEADS, s, HEAD_DIM), BF),
        jax.ShapeDtypeStruct((SWA_W, s), BF),
        jax.ShapeDtypeStruct((SWA_KV_HEADS, HEAD_DIM, s), BF), jax.ShapeDtypeStruct((SWA_KV_HEADS, VT_ROWS, s), BF),
        jax.ShapeDtypeStruct((FOX_HEADS, s), F32),
    )
    kv_t = lambda w: pl.BlockSpec((SWA_KV_HEADS, w, tm), lambda i: (0, 0, i))
    return _call(
        body, name="in_proj", out_shape=out_shape, grid=(s // tm,),
        in_specs=[row(D_MODEL), vec, vec, vec, _full(w_rows.shape), _full(w_t_fox.shape),
                  pl.BlockSpec((FOX_HEADS, tm), lambda i: (0, i)), row(LANES), row(LANES)],
        out_specs=(row(D_MODEL), heads_t(AUG_DIM), heads(FOX_HEADS, AUG_DIM), heads_t(AUG_DIM), heads(FOX_HEADS),
                   heads_t(VT_ROWS), row(FOX_W), row(SWA_W), row(SWA_W), heads(SWA_KV_HEADS), heads(SWA_KV_HEADS),
                   pl.BlockSpec((SWA_W, tm), lambda i: (0, i)), kv_t(HEAD_DIM), kv_t(VT_ROWS),
                   pl.BlockSpec((FOX_HEADS, tm), lambda i: (0, i))),
        compiler_params=_params(("parallel",)),
    )(x, g_pre, scale1p, shift, w_rows, w_t_fox, cum, cos_t, sin_t)


def _diag_chunks(d, bq, bk, chunk):
    out = []
    for c0 in range(0, bq, chunk):
        if d is None or d * bk + bk - 1 <= c0:
            out.append((c0, None))
        elif d * bk <= c0 + chunk - 1:
            kpos = d * bk + lax.broadcasted_iota(jnp.int32, (bk, chunk), 0)
            qpos = c0 + lax.broadcasted_iota(jnp.int32, (bk, chunk), 1)
            out.append((c0, kpos <= qpos))
    return out


def _fox_fwd(qat, ka, vt, m_own, bq, bk, chunk, running_max):
    nh, _, s = qat.shape
    r = bq // bk

    pairs = [(i, j) for i in range(s // bq) for j in range(i * r + r)]

    def body(i_tab, j_tab, ka_ref, qat_ref, vt_ref, mo_ref, o_ref, lse_ref, bad_ref, *rest):
        pt_ref, m_scr, acc_scr = (None,) * running_max + rest
        i, j = i_tab[pl.program_id(1)], j_tab[pl.program_id(1)]

        @pl.when(j == 0)
        def _():
            m_scr[...] = jnp.full(m_scr.shape, NEG, F32) if running_max else mo_ref[0]
            acc_scr[...] = jnp.zeros(acc_scr.shape, F32)

        def careful(d):
            kv, vtv = ka_ref[0], vt_ref[0]

            def one_chunk(n, carry):
                c0 = pl.multiple_of(n * chunk, chunk)
                cs = pl.ds(c0, chunk)
                sc = _dot(kv, qat_ref[0, :, cs])
                if d is not None:
                    kpos = d * bk + lax.broadcasted_iota(jnp.int32, (bk, chunk), 0)
                    qpos = c0 + lax.broadcasted_iota(jnp.int32, (bk, chunk), 1)
                    sc = jnp.where(kpos <= qpos, sc, NEG)
                m_prev = m_scr[:, cs]
                m_new = jnp.maximum(m_prev, jnp.max(sc, axis=0, keepdims=True))
                p = jnp.exp2(sc - m_new).astype(BF)
                acc_scr[:, cs] = jnp.exp2(m_prev - m_new) * acc_scr[:, cs] + _dot(vtv, p)
                m_scr[:, cs] = m_new
                return carry

            lax.fori_loop(0, bq // chunk, one_chunk, 0)

        def fast(d):
            kv, vtv = ka_ref[0], vt_ref[0]
            todo = _diag_chunks(d, bq, bk, chunk)
            scores = lambda c0: _dot(kv, qat_ref[0, :, c0:c0 + chunk])
            sc_next = scores(todo[0][0])
            for n, (c0, mask) in enumerate(todo):
                cs = slice(c0, c0 + chunk)
                sc = sc_next
                if n + 1 < len(todo):
                    sc_next = scores(todo[n + 1][0])
                if mask is not None:
                    sc = jnp.where(mask, sc, NEG)
                p = jnp.exp2(sc - m_scr[:, cs]).astype(BF)
                pt_ref[0, :, cs] = p
                acc_scr[:, cs] += _dot(vtv, p)

        step = careful if running_max else fast

        @pl.when(j < i * r)
        def _():
            step(None)

        for d in range(r):
            @pl.when(j == i * r + d)
            def _(d=d):
                step(d)

        @pl.when(j == i * r + r - 1)
        def _():
            l = acc_scr[HEAD_DIM:HEAD_DIM + 1, :]
            o_ref[0] = acc_scr[:HEAD_DIM, :] / l
            lse_ref[0] = m_scr[...] + jnp.log2(l)
            bad_ref[0] = jnp.where(l < OVERFLOW_GUARD, 0.0, 1.0)

    qmap_t = lambda h, t, it, jt: (h, 0, it[t])
    qrow = pl.BlockSpec((1, 1, bq), qmap_t)
    row_shape = jax.ShapeDtypeStruct((nh, 1, s), F32)
    out_shape = (jax.ShapeDtypeStruct((nh, HEAD_DIM, s), F32), row_shape, row_shape)
    out_specs = (pl.BlockSpec((1, HEAD_DIM, bq), qmap_t), qrow, qrow)
    if not running_max:
        out_shape += (jax.ShapeDtypeStruct((nh, s, s), BF),)
        out_specs += (pl.BlockSpec((1, bk, bq), lambda h, t, it, jt: (h, jt[t], it[t])),)
    grid_spec = pltpu.PrefetchScalarGridSpec(
        num_scalar_prefetch=2, grid=(nh, len(pairs)),
        in_specs=[pl.BlockSpec((1, bk, AUG_DIM), lambda h, t, it, jt: (h, jt[t], 0)), pl.BlockSpec((1, AUG_DIM, bq), qmap_t),
                  pl.BlockSpec((1, VT_ROWS, bk), lambda h, t, it, jt: (h, 0, jt[t])), qrow],
        out_specs=out_specs,
        scratch_shapes=[pltpu.VMEM((1, bq), F32), pltpu.VMEM((VT_ROWS, bq), F32)])
    return _call(
        body, name="fox_fwd_running_max" if running_max else "fox_fwd", out_shape=out_shape, grid_spec=grid_spec,
        compiler_params=_params(("parallel", "arbitrary")),
    )(jnp.asarray([p[0] for p in pairs], jnp.int32), jnp.asarray([p[1] for p in pairs], jnp.int32), ka, qat, vt, m_own)


def _fox_bwd(qat, ka, kat, v, dot_, lse, delta, pt, bq, bk, chunk, dq_blk):
    nh, _, s = qat.shape
    r = bq // bk
    nq = s // bq
    stored = pt is not None

    pairs = [(j, i) for j in range(s // bk) for i in range(j // r, nq)]

    def body(j_tab, i_tab, a_ref, b_ref, kat_ref, v_ref, qat_ref, do_ref, dl_ref, dq_ref, dk_ref, dv_ref, dk_scr, dv_scr):
        ka_ref, lse_ref, pt_ref = (None, None, a_ref) if stored else (a_ref, b_ref, None)
        j, i = j_tab[pl.program_id(1)], i_tab[pl.program_id(1)]

        @pl.when(pl.program_id(1) == 0)
        def _():
            dq_ref[...] = jnp.zeros(dq_ref.shape, F32)

        @pl.when(i * r <= j)
        def _():
            dk_scr[...] = jnp.zeros(dk_scr.shape, F32)
            dv_scr[...] = jnp.zeros(dv_scr.shape, F32)

        def step(d):
            ktv, vv = kat_ref[0], v_ref[0]
            kv = None if stored else ka_ref[0]
            todo = _diag_chunks(d, bq, bk, chunk)

            def products(c0):
                cs = slice(c0, c0 + chunk)
                return None if stored else _dot(kv, qat_ref[0, :, cs]), _dot(vv, do_ref[0, :, cs])

            nxt = products(todo[0][0])
            for n, (c0, mask) in enumerate(todo):
                cs = slice(c0, c0 + chunk)
                sc, dp = nxt
                if n + 1 < len(todo):
                    nxt = products(todo[n + 1][0])
                if stored:
                    p_bf = pt_ref[0, :, cs]
                    p = p_bf.astype(F32)
                else:
                    p = jnp.exp2(sc - lse_ref[0, :, cs])
                    if mask is not None:
                        p = jnp.where(mask, p, 0.0)
                    p_bf = p.astype(BF)
                ds = (p * (dp - dl_ref[0, :, cs])).astype(BF)
                dv_scr[...] += _dot_nt(do_ref[0, :, cs], p_bf)
                dk_scr[...] += _dot_nt(qat_ref[0, :VT_ROWS, cs], ds)
                c1 = c0 % dq_blk
                dq_ref[0, i * (bq // dq_blk) + c0 // dq_blk, :, c1:c1 + chunk] += _dot(ktv[:VT_ROWS], ds)

        @pl.when(i * r > j)
        def _():
            step(None)

        for d in range(r):
            @pl.when(j == i * r + d)
            def _(d=d):
                step(d)

        @pl.when(i == nq - 1)
        def _():
            dk_ref[0] = dk_scr[...]
            dv_ref[0] = dv_scr[...]

    qmap = lambda h, t, jt, it: (h, 0, it[t])
    kmap = lambda h, t, jt, it: (h, jt[t], 0)
    kmap_t = lambda h, t, jt, it: (h, 0, jt[t])
    if stored:
        first = [(pt, pl.BlockSpec((1, bk, bq), lambda h, t, jt, it: (h, jt[t], it[t]))),
                 (delta, pl.BlockSpec((1, 1, bq), qmap))]
    else:
        first = [(ka, pl.BlockSpec((1, bk, AUG_DIM), kmap)), (lse, pl.BlockSpec((1, 1, bq), qmap))]
    grid_spec = pltpu.PrefetchScalarGridSpec(
        num_scalar_prefetch=2, grid=(nh, len(pairs)),
        in_specs=[first[0][1], first[1][1], pl.BlockSpec((1, AUG_DIM, bk), kmap_t), pl.BlockSpec((1, bk, HEAD_DIM), kmap),
                  pl.BlockSpec((1, AUG_DIM, bq), qmap), pl.BlockSpec((1, HEAD_DIM, bq), qmap),
                  pl.BlockSpec((1, 1, bq), qmap)],
        out_specs=(pl.BlockSpec((1, s // dq_blk, VT_ROWS, dq_blk), lambda h, t, jt, it: (h, 0, 0, 0)),
                   pl.BlockSpec((1, VT_ROWS, bk), kmap_t), pl.BlockSpec((1, HEAD_DIM, bk), kmap_t)),
        scratch_shapes=[pltpu.VMEM((VT_ROWS, bk), F32), pltpu.VMEM((HEAD_DIM, bk), F32)])
    return _call(
        body, name="fox_bwd" if stored else "fox_bwd_recompute",
        out_shape=(jax.ShapeDtypeStruct((nh, s // dq_blk, VT_ROWS, dq_blk), F32),
                   jax.ShapeDtypeStruct((nh, VT_ROWS, s), F32), jax.ShapeDtypeStruct((nh, HEAD_DIM, s), F32)),
        grid_spec=grid_spec, compiler_params=_params(("parallel", "arbitrary")),
    )(jnp.asarray([p[0] for p in pairs], jnp.int32), jnp.asarray([p[1] for p in pairs], jnp.int32),
      first[0][0], first[1][0], kat, v, qat, dot_, delta)


def _swa_mask(i, tq):
    kpos = i * tq - WINDOW + lax.broadcasted_iota(jnp.int32, (tq + WINDOW, tq), 0)
    qpos = i * tq + lax.broadcasted_iota(jnp.int32, (tq + WINDOW, tq), 1)
    rel = qpos - kpos
    return (rel >= 0) & (rel < WINDOW) & (kpos >= 0)


def _swa_rows(ref, i, tq):
    before = pl.multiple_of(jnp.maximum(i * tq - WINDOW, 0), WINDOW)
    return jnp.concatenate([ref[0, pl.ds(before, WINDOW), :], ref[0, pl.ds(pl.multiple_of(i * tq, tq), tq), :]], axis=0)


def _swa_before(n_rows, tq):
    return pl.BlockSpec((1, n_rows, WINDOW), lambda g, i: (g, 0, jnp.maximum(i * (tq // WINDOW) - 1, 0)))


def _swa_probs_t(sc, mask, sink):
    sc = jnp.where(mask, sc, NEG)
    m = jnp.maximum(jnp.max(sc, axis=0, keepdims=True), sink)
    p = jnp.exp(sc - m)
    e_sink = jnp.exp(sink - m)
    inv_l = 1.0 / (jnp.sum(p, axis=0, keepdims=True) + e_sink)
    return p * inv_l, e_sink * inv_l


def _swa_fwd(qbt, kb, vbt, sinks, tq):
    s = qbt.shape[1]
    gw = SWA_GROUP * HEAD_DIM

    def body(q_ref, k_ref, vb_ref, vc_ref, s_ref, o_ref):
        i = pl.program_id(1)
        mask = _swa_mask(i, tq)
        kw = _swa_rows(k_ref, i, tq)
        vtw = jnp.concatenate([vb_ref[0], vc_ref[0]], axis=1)
        sk = s_ref[0]
        scores = lambda hh: _dot(kw, q_ref[hh * HEAD_DIM:(hh + 1) * HEAD_DIM, :])
        sc_next = scores(0)
        for hh in range(SWA_GROUP):
            rows = slice(hh * HEAD_DIM, (hh + 1) * HEAD_DIM)
            sink = sk[:, hh:hh + 1]
            sc = jnp.where(mask, sc_next, NEG)
            if hh + 1 < SWA_GROUP:
                sc_next = scores(hh + 1)
            m = jnp.maximum(jnp.max(sc, axis=0, keepdims=True), sink)
            acc = _dot(vtw, jnp.exp(sc - m).astype(BF))
            o_ref[rows, :] = acc[:HEAD_DIM] / (acc[HEAD_DIM:HEAD_DIM + 1] + jnp.exp(sink - m))

    kvspec = pl.BlockSpec((1, s, HEAD_DIM), lambda g, i: (g, 0, 0))
    qspec = pl.BlockSpec((gw, tq), lambda g, i: (g, i))
    return _call(
        body, name="swa_fwd", out_shape=jax.ShapeDtypeStruct((SWA_W, s), F32), grid=(SWA_KV_HEADS, s // tq),
        in_specs=[qspec, kvspec, _swa_before(VT_ROWS, tq), pl.BlockSpec((1, VT_ROWS, tq), lambda g, i: (g, 0, i)),
                  pl.BlockSpec((1, 1, SWA_GROUP), lambda g, i: (g, 0, 0))],
        out_specs=qspec, compiler_params=_params(("parallel", "parallel")),
    )(qbt, kb, vbt, vbt, sinks)


def _swa_bwd(qb, qbt, kb, kbt, vb, sinks, dob, dobt, tq):
    s = qb.shape[0]
    gw = SWA_GROUP * HEAD_DIM

    def body(q_ref, qt_ref, k_ref, ktb_ref, ktc_ref, v_ref, s_ref, do_ref, dot_ref, dq_ref, dk_ref, dv_ref, ds_ref):
        i = pl.program_id(1)

        @pl.when(i == 0)
        def _():
            dk_ref[...] = jnp.zeros(dk_ref.shape, F32)
            dv_ref[...] = jnp.zeros(dv_ref.shape, F32)
            ds_ref[...] = jnp.zeros(ds_ref.shape, F32)

        mask = _swa_mask(i, tq)
        kw = _swa_rows(k_ref, i, tq)
        vw = _swa_rows(v_ref, i, tq)
        ktw = jnp.concatenate([ktb_ref[0], ktc_ref[0]], axis=1)
        qv, dov = q_ref[...], do_ref[...]
        sk = s_ref[0]
        dsinks = []
        dk_acc = jnp.zeros((tq + WINDOW, HEAD_DIM), F32)
        dv_acc = jnp.zeros((tq + WINDOW, HEAD_DIM), F32)
        def products(hh):
            rows = slice(hh * HEAD_DIM, (hh + 1) * HEAD_DIM)
            return _dot(kw, qt_ref[rows, :]), _dot(vw, dot_ref[rows, :])

        nxt = products(0)
        for hh in range(SWA_GROUP):
            rows = slice(hh * HEAD_DIM, (hh + 1) * HEAD_DIM)
            sc, dp = nxt
            if hh + 1 < SWA_GROUP:
                nxt = products(hh + 1)
            p, p_sink = _swa_probs_t(sc, mask, sk[:, hh:hh + 1])
            delta = jnp.sum(p * dp, axis=0, keepdims=True)
            dsc = (p * (dp - delta)).astype(BF)
            dq_ref[rows, :] = _dot(ktw, dsc)
            dk_acc = dk_acc + _dot(dsc, qv[:, rows])
            dv_acc = dv_acc + _dot(p.astype(BF), dov[:, rows])
            dsinks.append(-jnp.sum(p_sink * delta, axis=1, keepdims=True))
        before = pl.ds(pl.multiple_of(jnp.maximum(i * tq - WINDOW, 0), WINDOW), WINDOW)
        own = pl.ds(pl.multiple_of(i * tq, tq), tq)
        dk_ref[0, before, :] += dk_acc[:WINDOW]
        dk_ref[0, own, :] += dk_acc[WINDOW:]
        dv_ref[0, before, :] += dv_acc[:WINDOW]
        dv_ref[0, own, :] += dv_acc[WINDOW:]
        ds_ref[0] += jnp.concatenate(dsinks, axis=1)

    kvspec = pl.BlockSpec((1, s, HEAD_DIM), lambda g, i: (g, 0, 0))
    qspec = pl.BlockSpec((tq, gw), lambda g, i: (i, g))
    qspec_t = pl.BlockSpec((gw, tq), lambda g, i: (g, i))
    sspec = pl.BlockSpec((1, 1, SWA_GROUP), lambda g, i: (g, 0, 0))
    kvshape = jax.ShapeDtypeStruct((SWA_KV_HEADS, s, HEAD_DIM), F32)
    return _call(
        body, name="swa_bwd",
        out_shape=(jax.ShapeDtypeStruct((SWA_W, s), F32), kvshape, kvshape,
                   jax.ShapeDtypeStruct((SWA_KV_HEADS, 1, SWA_GROUP), F32)),
        grid=(SWA_KV_HEADS, s // tq),
        in_specs=[qspec, qspec_t, kvspec, _swa_before(HEAD_DIM, tq),
                  pl.BlockSpec((1, HEAD_DIM, tq), lambda g, i: (g, 0, i)), kvspec, sspec, qspec, qspec_t],
        out_specs=(qspec_t, kvspec, kvspec, sspec),
        compiler_params=_params(("parallel", "arbitrary")),
    )(qb, qbt, kb, kbt, kbt, vb, sinks, dob, dobt)


def _pairs_to_rows(ref, n_rows=HEAD_DIM):
    parts = []
    for a in range(0, FOX_HEADS, 2):
        parts.append(jnp.concatenate([ref[a][:n_rows], ref[a + 1][:n_rows]], axis=0).T)
    return jnp.concatenate(parts, axis=1)


def _blocks_to_rows(ref):
    return jnp.concatenate([ref[a:a + LANES, :].T for a in range(0, ref.shape[0], LANES)], axis=1)


def _out_proj(oat, za, obt, zb, x, tgt, w_out, w_out_t, gate, g_post, inv_l, tm):
    s = x.shape[0]

    def body(oat_ref, za_ref, obt_ref, zb_ref, x_ref, t_ref, w_ref, wt_ref, gate_ref, gp_ref, il_ref,
             dout_ref, doat_ref, dla_ref, dza_ref, dob_ref, dobt_ref, dzb_ref, gw_ref, dgate_ref, dgp_ref, loss_ref):
        i = pl.program_id(0)

        @pl.when(i == 0)
        def _():
            gw_ref[...] = jnp.zeros(gw_ref.shape, F32)
            dgate_ref[...] = jnp.zeros(dgate_ref.shape, F32)
            dgp_ref[...] = jnp.zeros(dgp_ref.shape, F32)
            loss_ref[...] = jnp.zeros(loss_ref.shape, F32)

        oa_v = _pairs_to_rows(oat_ref)
        ob_v = _blocks_to_rows(obt_ref)
        za_v, zb_v = za_ref[...], zb_ref[...]
        sga, sgb = _sigmoid(za_v), _sigmoid(zb_v)
        sila, silb = za_v * sga, zb_v * sgb
        u = jnp.concatenate([oa_v * sila, ob_v * silb], axis=1).astype(BF)
        yv = _dot(u, w_ref[...])
        yhat, rstd = _rms_hat(yv)
        gp, gate_v = gp_ref[...], gate_ref[...]
        nrm = yhat * gp
        diff = (x_ref[...] + gate_v * nrm) - t_ref[...]
        loss_ref[...] += 0.5 * jnp.sum(jnp.sum(diff * diff, axis=1, keepdims=True), axis=0, keepdims=True) / D_MODEL
        dout = diff * (1.0 / D_MODEL)
        dout_ref[...] = dout
        dgate_ref[...] += jnp.sum(dout * nrm, axis=0, keepdims=True)
        dn = dout * gate_v
        dgp_ref[...] += jnp.sum(dn * yhat, axis=0, keepdims=True)
        dyhat = dn * gp
        dy = (rstd * (dyhat - yhat * jnp.mean(dyhat * yhat, axis=1, keepdims=True))).astype(BF)
        gw_ref[...] += _dot_tn(u, dy)
        du = _dot(dy, wt_ref[...])
        dua, dub = du[:, :FOX_W], du[:, FOX_W:]
        doa = dua * sila
        for a in range(0, FOX_HEADS, 2):
            pair_t = doa[:, a * HEAD_DIM:(a + 2) * HEAD_DIM].T
            for hd, rows in ((a, slice(0, HEAD_DIM)), (a + 1, slice(HEAD_DIM, 2 * HEAD_DIM))):
                inv_l = il_ref[hd]
                doat_ref[hd] = (pair_t[rows] * inv_l).astype(BF)
                dla_ref[hd] = jnp.sum(pair_t[rows] * oat_ref[hd], axis=0, keepdims=True) * inv_l
        dob = dub * silb
        dob_ref[...] = dob.astype(BF)
        for a in range(0, SWA_W, LANES):
            dobt_ref[a:a + LANES, :] = dob[:, a:a + LANES].T.astype(BF)
        dza_ref[...] = (dua * oa_v * (sga * (1.0 + za_v * (1.0 - sga)))).astype(BF)
        dzb_ref[...] = (dub * ob_v * (sgb * (1.0 + zb_v * (1.0 - sgb)))).astype(BF)

    row = lambda w: pl.BlockSpec((tm, w), lambda i: (i, 0))
    heads_t = lambda w: pl.BlockSpec((FOX_HEADS, w, tm), lambda i: (0, 0, i))
    vec = _full((1, D_MODEL))
    mat = _full((D_MODEL, D_MODEL))
    out_shape = (
        jax.ShapeDtypeStruct((s, D_MODEL), F32),
        jax.ShapeDtypeStruct((FOX_HEADS, HEAD_DIM, s), BF), jax.ShapeDtypeStruct((FOX_HEADS, 1, s), F32),
        jax.ShapeDtypeStruct((s, FOX_W), BF), jax.ShapeDtypeStruct((s, SWA_W), BF), jax.ShapeDtypeStruct((SWA_W, s), BF),
        jax.ShapeDtypeStruct((s, SWA_W), BF),
        jax.ShapeDtypeStruct((D_MODEL, D_MODEL), F32),
        jax.ShapeDtypeStruct((1, D_MODEL), F32), jax.ShapeDtypeStruct((1, D_MODEL), F32),
        jax.ShapeDtypeStruct((1, 1), F32),
    )
    col = pl.BlockSpec((SWA_W, tm), lambda i: (0, i))
    return _call(
        body, name="out_proj", out_shape=out_shape, grid=(s // tm,),
        in_specs=[heads_t(HEAD_DIM), row(FOX_W), col, row(SWA_W), row(D_MODEL), row(D_MODEL), mat, mat, vec, vec,
                  heads_t(1)],
        out_specs=(row(D_MODEL), heads_t(HEAD_DIM), heads_t(1), row(FOX_W), row(SWA_W), col, row(SWA_W), mat, vec, vec,
                   _full((1, 1))),
        compiler_params=_params(("arbitrary",)),
    )(oat, za, obt, zb, x, tgt, w_out, w_out_t, gate, g_post, inv_l)


def _assemble_dproj(dqt, dkt, dvt, dza, dqb, dzb, dkb, dvb, df, cos_t, sin_t, tm):
    s = dza.shape[0]

    def body(dqt_ref, dkt_ref, dvt_ref, dza_ref, dqb_ref, dzb_ref, dkb_ref, dvb_ref, df_ref, cos_ref, sin_ref, o_ref):
        def cat(ref, n):
            return jnp.concatenate([ref[hd] for hd in range(n)], axis=1)

        cos2, sin2 = cos_ref[...], sin_ref[...]
        cos8 = jnp.concatenate([cos2] * 4, axis=1)
        sin8 = jnp.concatenate([sin2] * 4, axis=1)
        scale = HEAD_DIM ** -0.5
        o_ref[:, C_QA:C_QA + FOX_W] = (_pairs_to_rows(dqt_ref.at[:, 0]) * scale).astype(BF)
        o_ref[:, C_KA:C_KA + FOX_W] = (_pairs_to_rows(dkt_ref) * LN2).astype(BF)
        o_ref[:, C_VA:C_VA + FOX_W] = _pairs_to_rows(dvt_ref).astype(BF)
        o_ref[:, C_ZA:C_ZA + FOX_W] = dza_ref[...]
        dq = _blocks_to_rows(dqb_ref) * scale
        o_ref[:, C_QB:C_QB + SWA_W] = (dq * cos8 - _rope_partner(dq) * sin8).astype(BF)
        o_ref[:, C_ZB:C_ZB + SWA_W] = dzb_ref[...]
        dk = cat(dkb_ref, SWA_KV_HEADS)
        o_ref[:, C_KB:C_KB + SWA_KV_W] = (dk * cos2 - _rope_partner(dk) * sin2).astype(BF)
        o_ref[:, C_VB:C_VB + SWA_KV_W] = cat(dvb_ref, SWA_KV_HEADS).astype(BF)
        o_ref[:, C_F:C_F + LANES] = df_ref[...].astype(BF)

    row = lambda w: pl.BlockSpec((tm, w), lambda i: (i, 0))
    heads = lambda n: pl.BlockSpec((n, tm, HEAD_DIM), lambda i: (0, i, 0))
    heads_t = lambda w: pl.BlockSpec((FOX_HEADS, w, tm), lambda i: (0, 0, i))
    return _call(
        body, name="assemble_dproj", out_shape=jax.ShapeDtypeStruct((s, WP), BF), grid=(s // tm,),
        in_specs=[pl.BlockSpec((FOX_HEADS, 1, VT_ROWS, tm), lambda i: (0, i, 0, 0)), heads_t(VT_ROWS), heads_t(HEAD_DIM),
                  row(FOX_W), pl.BlockSpec((SWA_W, tm), lambda i: (0, i)), row(SWA_W), heads(SWA_KV_HEADS),
                  heads(SWA_KV_HEADS), row(LANES), row(LANES), row(LANES)],
        out_specs=row(WP), compiler_params=_params(("parallel",)),
    )(dqt, dkt, dvt, dza, dqb, dzb, dkb, dvb, df, cos_t, sin_t)


def _in_proj_bwd_x(dproj, w_al_t, x, dout, g_pre, scale1p, tm, parts):
    s = x.shape[0]
    n_steps = s // tm
    masks = [(1, 0), (0, 1), (1, 1)]

    def body(dp_ref, wt_ref, x_ref, dout_ref, g_ref, sc_ref, parts_ref, gx_ref, dsh_ref, dsc_ref, dg_ref, got_ref,
             send_sems, recv_sems, local_sem):
        i = pl.program_id(0)
        cx, cy, cc = lax.axis_index("x"), lax.axis_index("y"), lax.axis_index("c")
        me = 2 * cx + cy
        own = pltpu.make_async_copy(parts_ref.at[me], got_ref.at[me], local_sem)

        def copy(k, send):
            dx, dy = masks[k]
            peer = 2 * (cx ^ dx) + (cy ^ dy)
            return pltpu.make_async_remote_copy(
                src_ref=parts_ref.at[peer if send else me], dst_ref=got_ref.at[me if send else peer],
                send_sem=send_sems.at[k], recv_sem=recv_sems.at[k], device_id=(cx ^ dx, cy ^ dy, cc), device_id_type=MESH)

        @pl.when(i == 0)
        def _():
            dsh_ref[...] = jnp.zeros(dsh_ref.shape, F32)
            dsc_ref[...] = jnp.zeros(dsc_ref.shape, F32)
            dg_ref[...] = jnp.zeros(dg_ref.shape, F32)
            own.start()
            for k in range(len(masks)):
                copy(k, True).start()

        @pl.when(i == n_steps - 1)
        def _():
            for k in range(len(masks)):
                copy(k, False).wait_recv()
            for k in range(len(masks)):
                copy(k, True).wait_send()
            own.wait()

        dh = _dot(dp_ref[...], wt_ref[...])
        xhat, rstd = _rms_hat(x_ref[...])
        g, sc = g_ref[...], sc_ref[...]
        dsh_ref[...] += jnp.sum(dh, axis=0, keepdims=True)
        dhx = dh * xhat
        dsc_ref[...] += jnp.sum(dhx * g, axis=0, keepdims=True)
        dg_ref[...] += jnp.sum(dhx * sc, axis=0, keepdims=True)
        dxhat = dh * (g * sc)
        gx_ref[...] = dout_ref[...] + rstd * (dxhat - xhat * jnp.mean(dxhat * xhat, axis=1, keepdims=True))

    row = lambda w: pl.BlockSpec((tm, w), lambda i: (i, 0))
    vec = _full((1, D_MODEL))
    vshape = jax.ShapeDtypeStruct((1, D_MODEL), F32)
    hbm = pl.BlockSpec(memory_space=pl.ANY)
    return _call(
        body, name="in_proj_bwd_x",
        out_shape=(jax.ShapeDtypeStruct((s, D_MODEL), F32), vshape, vshape, vshape,
                   jax.ShapeDtypeStruct(parts.shape, parts.dtype)),
        grid=(n_steps,),
        in_specs=[row(WP), _full((WP, D_MODEL)), row(D_MODEL), row(D_MODEL), vec, vec, hbm],
        out_specs=(row(D_MODEL), vec, vec, vec, hbm),
        scratch_shapes=[pltpu.SemaphoreType.DMA((3,)), pltpu.SemaphoreType.DMA((3,)), pltpu.SemaphoreType.DMA],
        compiler_params=_params(("arbitrary",), has_side_effects=True),
    )(dproj, w_al_t, x, dout, g_pre, scale1p, parts)


def _in_proj_bwd_w(h, dproj, tk, tn):
    s = h.shape[0]

    def body(h_ref, dp_ref, gw_ref):
        @pl.when(pl.program_id(1) == 0)
        def _():
            gw_ref[...] = jnp.zeros(gw_ref.shape, F32)

        gw_ref[...] += _dot_tn(h_ref[...], dp_ref[...])

    return _call(
        body, name="in_proj_bwd_w", out_shape=jax.ShapeDtypeStruct((D_MODEL, WP), F32), grid=(WP // tn, s // tk),
        in_specs=[pl.BlockSpec((tk, D_MODEL), lambda n, k: (k, 0)), pl.BlockSpec((tk, tn), lambda n, k: (k, n))],
        out_specs=pl.BlockSpec((D_MODEL, tn), lambda n, k: (0, n)),
        compiler_params=_params(("parallel", "arbitrary")),
    )(h, dproj)


def _align_w_in(w_cols):
    def part(name, width):
        return w_cols[:, _SRC[name]:_SRC[name] + width]

    fpad = jnp.pad(part("fa", FOX_HEADS), ((0, 0), (0, LANES - FOX_HEADS)))
    return jnp.concatenate([part("qa", FOX_W), part("ka", FOX_W), part("va", FOX_W), part("za", FOX_W),
                            part("qb", SWA_W), part("zb", SWA_W), part("kb", SWA_KV_W), part("vb", SWA_KV_W), fpad], axis=1)


def _unalign_w_in(g_al):
    def part(c0, width):
        return g_al[:, c0:c0 + width]

    return jnp.concatenate([part(C_QA, FOX_W), part(C_KA, FOX_W), part(C_VA, FOX_W), part(C_F, FOX_HEADS),
                            part(C_ZA, FOX_W), part(C_QB, SWA_W), part(C_KB, SWA_KV_W), part(C_VB, SWA_KV_W),
                            part(C_ZB, SWA_W)], axis=1)


def _rope_tables(positions):
    inv_freq = ROPE_THETA ** (-jnp.arange(HALF, dtype=F32) / HALF)
    ang = positions.astype(F32)[:, None] * inv_freq
    cos, sin = jnp.cos(ang), jnp.sin(ang)
    return jnp.concatenate([cos, cos, cos, cos], axis=1), jnp.concatenate([-sin, sin, -sin, sin], axis=1)


def _tiles(s):
    if s >= 4096:
        return dict(tm=512, blk=512, bq=2048, bk=1024, bk_bwd=512, chunk=256, tq=256, tm_out=512, tk=1024, tn=1152)
    return dict(tm=128, blk=128, bq=256, bk=128, bk_bwd=128, chunk=128, tq=128, tm_out=128, tk=128, tn=1152)


def kernel(x, c, positions, w_ada, b_ada, g_pre, w_in, b_fgate, sinks, w_out, g_post, loss_target, m_w_ada, m_b_ada, m_g_pre, m_w_in, m_b_fgate, m_sinks, m_w_out, m_g_post, v_w_ada, v_b_ada, v_g_pre, v_w_in, v_b_fgate, v_sinks, v_w_out, v_g_post):
    s = x.shape[1]
    t = _tiles(s)
    nc = s // LANES
    rows = FOX_HEADS * nc
    me = 4 * lax.axis_index("x") + 2 * lax.axis_index("y") + lax.axis_index("c")
    chip = 2 * lax.axis_index("x") + lax.axis_index("y")
    core = lax.axis_index("c")
    x2, tgt = x[0], loss_target[0]

    c_all = _allgather_devices(c, "gather_c")[:, 0, :]
    a_all, mod_shard = _ada_shard(c_all, w_ada[0])
    mod_all = _allgather_devices(mod_shard, "gather_mod")
    mod_rows = lax.dynamic_index_in_dim(mod_all, me, axis=1, keepdims=False)
    mod = mod_rows.reshape(N_CHIPS, 2, W_ADA_SHARD)[:, 0, :].reshape(1, 3 * D_MODEL) + b_ada
    shift, scale1p, gate = mod[:, :D_MODEL], 1.0 + mod[:, D_MODEL:2 * D_MODEL], mod[:, 2 * D_MODEL:]

    w_in_pad = jnp.pad(w_in[0].astype(BF), ((0, 0), (0, W_IN_SHARD_PAD - W_IN_SHARD)))
    w_pack = jnp.concatenate([w_in_pad, w_out[0].astype(BF).reshape(D_MODEL, W_OUT_SHARD)], axis=1)
    w_all = _allgather_chips(w_pack.reshape(2, D_MODEL // 2, -1), "gather_weights").reshape(N_CHIPS, D_MODEL, -1)
    w_cols = jnp.concatenate([w_all[k, :, :W_IN_SHARD] for k in range(N_CHIPS)], axis=1)
    w_al = _align_w_in(w_cols)
    w_al_t = w_al.T
    w_out_all = w_all[:, :, W_IN_SHARD_PAD:].reshape(D_MODEL, D_MODEL)
    w_out_t = w_out_all.T

    cos_t, sin_t = _rope_tables(positions[0])

    f_pad = _forget_logits(x2, g_pre, scale1p, shift, w_al[:, C_F:], t["tm"])
    f_rows = f_pad[:, :FOX_HEADS].T.reshape(rows, LANES)
    bias_rows = jnp.repeat(b_fgate[0], nc)[:, None]
    cum = _log_forget_cumsum(f_rows, bias_rows, nc).reshape(FOX_HEADS, s)
    h, qat, ka, kat, va, vat, za, zb, qb, kb, vb, qbt, kbt, vbt, m_own = _in_proj(
        x2, g_pre, scale1p, shift, w_al[:, C_VA:C_F], w_al_t[:C_ZA], cum, cos_t, sin_t, t["tm"])
    m_own = m_own[:, None, :]
    fox_args = (qat, ka, vat, m_own, t["bq"], t["bk"], t["chunk"])
    oat, lse, bad, pt = _fox_fwd(*fox_args, running_max=False)
    overflowed = jnp.max(bad) > 0.0
    oat, lse = lax.cond(overflowed, lambda: _fox_fwd(*fox_args, running_max=True)[:2], lambda: (oat, lse))
    inv_l = jnp.where(overflowed, 1.0, jnp.exp2(m_own - lse))
    sinks_g = sinks.reshape(SWA_KV_HEADS, 1, SWA_GROUP)
    obt = _swa_fwd(qbt, kb, vbt, sinks_g, t["tq"])

    dout, doat, delta_a, dza, dob, dobt, dzb, gw_out, dgate, dg_post, loss_part = _out_proj(
        oat, za, obt, zb, x2, tgt, w_out_all, w_out_t, gate, g_post, inv_l, t["tm_out"])

    bwd_args = (qat, ka, kat, va, doat, lse, delta_a)
    bwd_tiles = (t["bq"], t["bk_bwd"], t["chunk"], t["blk"])
    dqt, dkt, dvt = lax.cond(overflowed, lambda: _fox_bwd(*bwd_args, None, *bwd_tiles),
                             lambda: _fox_bwd(*bwd_args, pt, *bwd_tiles))
    dcum = dqt[:, :, HEAD_DIM, :].reshape(FOX_HEADS, s) - dkt[:, HEAD_DIM, :]
    df_rows, db_heads = _log_forget_cumsum_bwd(dcum.reshape(rows, LANES), f_rows, bias_rows, nc)
    df_pad = jnp.pad(df_rows.reshape(FOX_HEADS, s).T, ((0, 0), (0, LANES - FOX_HEADS)))
    dqb, dkb, dvb, dsinks = _swa_bwd(qb, qbt, kb, kbt, vb, sinks_g, dob, dobt, t["tq"])

    dproj = _assemble_dproj(dqt, dkt, dvt, dza, dqb, dzb, dkb, dvb, df_pad, cos_t, sin_t, t["blk"])
    gw_in = _unalign_w_in(_in_proj_bwd_w(h, dproj, t["tk"], t["tn"]))

    gin = jnp.pad(gw_in.reshape(D_MODEL, N_CHIPS, W_IN_SHARD).transpose(1, 0, 2),
                  ((0, 0), (0, 0), (0, W_IN_SHARD_PAD - W_IN_SHARD)))
    gout = gw_out.reshape(N_CHIPS, D_MODEL, W_OUT_SHARD)
    gbig = jnp.concatenate([gin, gout], axis=2)
    half = D_MODEL // 2
    gw = W_IN_SHARD_PAD + W_OUT_SHARD
    keep = lax.dynamic_slice_in_dim(gbig, core * half, half, axis=1)
    give = lax.dynamic_slice_in_dim(gbig, (1 - core) * half, half, axis=1)
    got = _swap_sibling(give.reshape(N_CHIPS * half, gw), "swap_grad_halves")
    pair = _add(keep.reshape(N_CHIPS * half, gw), got, "add_pair", BF).reshape(N_CHIPS, half, gw)
    grad_x, dshift, dscale, dg_pre, from_chips = _in_proj_bwd_x(
        dproj, w_al_t, x2, dout, g_pre, scale1p, t["tm_out"], pair)

    pad_lane = lambda vrow: jnp.pad(vrow, ((0, 0), (0, LANES - vrow.shape[1])))
    packed = jnp.concatenate([dshift, dscale, dgate, dg_pre, dg_post,
                              pad_lane(db_heads.reshape(1, FOX_HEADS)), pad_lane(dsinks.reshape(1, FOX_HEADS)),
                              pad_lane(loss_part)], axis=1)
    parts = _allgather_devices(packed, "gather_partials")
    tot = _sum_devices(parts)
    loss = tot[0, P_LOSS]
    g_b_ada = tot[:, P_DMOD:P_DMOD + 3 * D_MODEL]
    g_g_pre = tot[:, P_GPRE:P_GPRE + D_MODEL]
    g_g_post = tot[:, P_GPOST:P_GPOST + D_MODEL]
    g_b_fgate = tot[:, P_BF:P_BF + FOX_HEADS]
    g_sinks = tot[:, P_SINK:P_SINK + FOX_HEADS]
    dm_shard = lax.dynamic_slice_in_dim(parts[:, 0, :3 * D_MODEL], chip * W_ADA_SHARD, W_ADA_SHARD, axis=1)
    g_w_ada = _grad_w_ada(a_all.T, dm_shard)

    mine = _sum_chips(from_chips, "sum_chips")
    other = _swap_sibling(mine, "swap_grad_result")
    lo = jnp.where(core == 0, mine, other)
    hi = jnp.where(core == 0, other, mine)
    gfull = jnp.concatenate([lo, hi], axis=0)
    g_w_in = gfull[:, :W_IN_SHARD]
    g_w_out = gfull[:, W_IN_SHARD_PAD:].reshape(W_OUT_SHARD, D_MODEL)

    grads = dict(w_ada=g_w_ada, b_ada=g_b_ada, g_pre=g_g_pre, w_in=g_w_in, b_fgate=g_b_fgate, sinks=g_sinks,
                 w_out=g_w_out, g_post=g_g_post)
    weights = dict(w_ada=w_ada, b_ada=b_ada, g_pre=g_pre, w_in=w_in, b_fgate=b_fgate, sinks=sinks, w_out=w_out, g_post=g_post)
    moms = dict(w_ada=m_w_ada, b_ada=m_b_ada, g_pre=m_g_pre, w_in=m_w_in, b_fgate=m_b_fgate, sinks=m_sinks, w_out=m_w_out, g_post=m_g_post)
    vars_ = dict(w_ada=v_w_ada, b_ada=v_b_ada, g_pre=v_g_pre, w_in=v_w_in, b_fgate=v_b_fgate, sinks=v_sinks, w_out=v_w_out, g_post=v_g_post)
    names = ["w_ada", "b_ada", "g_pre", "w_in", "b_fgate", "sinks", "w_out", "g_post"]
    g_out, d_out, m_out, v_out = [], [], [], []
    for n in names:
        g2 = grads[n].reshape(weights[n].shape[-2:])
        go, d, nm, nv = _adamw(weights[n], g2, moms[n], vars_[n], "adamw_" + n)
        g_out.append(go)
        d_out.append(d)
        m_out.append(nm)
        v_out.append(nv)
    return (loss, grad_x.reshape(x.shape), *g_out, *d_out, *m_out, *v_out)
```

```python
import functools

import jax
import jax.numpy as jnp
from jax import lax
from jax.experimental import pallas as pl
from jax.experimental.pallas import tpu as pltpu

_INTERPRET = False

D_MODEL = 1024
HEAD_DIM = 64
HALF = HEAD_DIM // 2
AUG_DIM = 128
AUG_ROWS = 8
VT_ROWS = 80
LOG2E = 1.4426950408889634
LN2 = 0.6931471805599453
Q_SCALE = LOG2E * 64 ** -0.5
FOX_HEADS = 8
FOX_W = 512
SWA_W = 512
SWA_KV_HEADS = 2
SWA_GROUP = 4
SWA_KV_W = 128
WINDOW = 128
ROPE_THETA = 10000.0
RMS_EPS = 1e-6
IN_WIDTH = 3336
N_CHIPS = 4
N_DEV = 8
W_IN_SHARD = IN_WIDTH // N_CHIPS
W_IN_SHARD_PAD = 896
W_ADA_SHARD = 3 * D_MODEL // N_CHIPS
W_OUT_SHARD = D_MODEL // N_CHIPS
LANES = 128

_SRC = dict(qa=0, ka=512, va=1024, fa=1536, za=1544, qb=2056, kb=2568, vb=2696, zb=2824)
C_QA, C_KA, C_VA, C_ZA, C_QB, C_ZB, C_KB, C_VB, C_F = 0, 512, 1024, 1536, 2048, 2560, 3072, 3200, 3328
WP = 3456

ADAM_LR = 0.001
ADAM_B1 = 0.9
ADAM_B2 = 0.999
ADAM_EPS = 1e-08
ADAM_WD = 0.01
ADAM_STEP = 10

VMEM_LIMIT = 56 * 1024 * 1024
NEG = -1e30
OVERFLOW_GUARD = 1e30
MESH = pl.DeviceIdType.MESH
BF = jnp.bfloat16
F32 = jnp.float32

P_DMOD, P_GPRE, P_GPOST, P_BF, P_SINK, P_LOSS, P_LEN = 0, 3072, 4096, 5120, 5248, 5376, 5504


def _call(body, **kw):
    return pl.pallas_call(body, interpret=_INTERPRET, **kw)


def _params(sem=None, **kw):
    return pltpu.CompilerParams(dimension_semantics=sem, vmem_limit_bytes=VMEM_LIMIT, **kw)


def _full(shape):
    zeros = (0,) * len(shape)
    return pl.BlockSpec(shape, lambda *_: zeros)


def _dot(a, b):
    return jnp.dot(a, b, preferred_element_type=F32)


def _dot_nt(a, b):
    return lax.dot_general(a, b, (((1,), (1,)), ((), ())), preferred_element_type=F32)


def _dot_tn(a, b):
    return lax.dot_general(a, b, (((0,), (0,)), ((), ())), preferred_element_type=F32)


def _sigmoid(z):
    return 1.0 / (1.0 + jnp.exp(-z))


def _rope_partner(t):
    w = t.shape[-1]
    lane = lax.broadcasted_iota(jnp.int32, t.shape, t.ndim - 1)
    return jnp.where((lane & (HEAD_DIM - 1)) < HALF, pltpu.roll(t, w - HALF, t.ndim - 1), pltpu.roll(t, HALF, t.ndim - 1))


def _allgather_devices(v, name):
    r, cdim = v.shape
    masks = [(dx, dy, dc) for dx in (0, 1) for dy in (0, 1) for dc in (0, 1)][1:]

    def body(v_ref, out_ref, send_sems, recv_sems):
        x, y, c = lax.axis_index("x"), lax.axis_index("y"), lax.axis_index("c")
        me = 4 * x + 2 * y + c
        out_ref[me] = v_ref[...]
        copies = []
        for k, (dx, dy, dc) in enumerate(masks):
            cp = pltpu.make_async_remote_copy(
                src_ref=v_ref, dst_ref=out_ref.at[me], send_sem=send_sems.at[k], recv_sem=recv_sems.at[k],
                device_id=(x ^ dx, y ^ dy, c ^ dc), device_id_type=MESH)
            cp.start()
            copies.append(cp)
        for k, (dx, dy, dc) in enumerate(masks):
            peer = 4 * (x ^ dx) + 2 * (y ^ dy) + (c ^ dc)
            pltpu.make_async_remote_copy(
                src_ref=v_ref, dst_ref=out_ref.at[peer], send_sem=send_sems.at[k], recv_sem=recv_sems.at[k],
                device_id=(x ^ dx, y ^ dy, c ^ dc), device_id_type=MESH).wait_recv()
        for cp in copies:
            cp.wait_send()

    return _call(
        body, name=name, out_shape=jax.ShapeDtypeStruct((N_DEV, r, cdim), v.dtype),
        in_specs=[pl.BlockSpec(memory_space=pltpu.VMEM)], out_specs=pl.BlockSpec(memory_space=pltpu.VMEM),
        scratch_shapes=[pltpu.SemaphoreType.DMA((7,)), pltpu.SemaphoreType.DMA((7,))],
        compiler_params=pltpu.CompilerParams(has_side_effects=True),
    )(v)


def _allgather_chips(v, name):
    _, r, cdim = v.shape
    masks = [(1, 0), (0, 1), (1, 1)]
    n = len(masks)

    def body(v_ref, out_ref, send_sems, recv_sems, local_sem):
        x, y, c = lax.axis_index("x"), lax.axis_index("y"), lax.axis_index("c")
        me = 2 * x + y
        mine = pltpu.make_async_copy(v_ref, out_ref.at[me], local_sem)
        mine.start()

        def copy(k, chip, half, to):
            return pltpu.make_async_remote_copy(
                src_ref=v_ref.at[half] if k < n else out_ref.at[chip, half], dst_ref=out_ref.at[chip, half],
                send_sem=send_sems.at[k], recv_sem=recv_sems.at[k], device_id=to, device_id_type=MESH)

        first = [copy(k, me, c, (x ^ dx, y ^ dy, c)) for k, (dx, dy) in enumerate(masks)]
        for cp in first:
            cp.start()
        passed = []
        for k, (dx, dy) in enumerate(masks):
            peer = 2 * (x ^ dx) + (y ^ dy)
            copy(k, peer, c, (x, y, c)).wait_recv()
            cp = copy(n + k, peer, c, (x, y, 1 - c))
            cp.start()
            passed.append(cp)
        for k, (dx, dy) in enumerate(masks):
            copy(n + k, 2 * (x ^ dx) + (y ^ dy), 1 - c, (x, y, c)).wait_recv()
        for cp in first + passed:
            cp.wait_send()
        mine.wait()

    return _call(
        body, name=name, out_shape=jax.ShapeDtypeStruct((N_CHIPS, 2, r, cdim), v.dtype),
        in_specs=[pl.BlockSpec(memory_space=pl.ANY)], out_specs=pl.BlockSpec(memory_space=pl.ANY),
        scratch_shapes=[pltpu.SemaphoreType.DMA((2 * n,)), pltpu.SemaphoreType.DMA((2 * n,)), pltpu.SemaphoreType.DMA],
        compiler_params=pltpu.CompilerParams(has_side_effects=True),
    )(v)


def _swap_sibling(v, name):
    def body(v_ref, out_ref, send_sem, recv_sem):
        x, y, c = lax.axis_index("x"), lax.axis_index("y"), lax.axis_index("c")
        cp = pltpu.make_async_remote_copy(
            src_ref=v_ref, dst_ref=out_ref, send_sem=send_sem, recv_sem=recv_sem,
            device_id=(x, y, 1 - c), device_id_type=MESH)
        cp.start()
        cp.wait()

    return _call(
        body, name=name, out_shape=jax.ShapeDtypeStruct(v.shape, v.dtype),
        in_specs=[pl.BlockSpec(memory_space=pl.ANY)], out_specs=pl.BlockSpec(memory_space=pl.ANY),
        scratch_shapes=[pltpu.SemaphoreType.DMA, pltpu.SemaphoreType.DMA],
        compiler_params=pltpu.CompilerParams(has_side_effects=True),
    )(v)


def _ada_shard(c_all, w_ada_shard):
    def body(c_ref, w_ref, a_ref, mod_ref):
        cv = c_ref[...]
        a = cv * _sigmoid(cv)
        a_ref[...] = a
        mod_ref[...] = _dot(a.astype(BF), w_ref[...].astype(BF))

    return _call(
        body, name="ada_shard",
        out_shape=(jax.ShapeDtypeStruct((N_DEV, D_MODEL), F32), jax.ShapeDtypeStruct((N_DEV, W_ADA_SHARD), F32)),
        compiler_params=_params(),
    )(c_all, w_ada_shard)


def _grad_w_ada(a_t, dm_shard):
    def body(a_ref, dm_ref, out_ref):
        acc = jnp.zeros((D_MODEL, W_ADA_SHARD), F32)
        for b in range(N_DEV):
            acc = acc + a_ref[:, b:b + 1] * dm_ref[b:b + 1, :]
        out_ref[...] = acc

    return _call(body, name="grad_w_ada", out_shape=jax.ShapeDtypeStruct((D_MODEL, W_ADA_SHARD), F32),
                 compiler_params=_params())(a_t, dm_shard)


def _sum_devices(parts):
    n = parts.shape[-1]

    def body(p_ref, out_ref):
        acc = p_ref[0]
        for b in range(1, N_DEV):
            acc = acc + p_ref[b]
        out_ref[...] = acc

    return _call(body, name="sum_devices", out_shape=jax.ShapeDtypeStruct((1, n), F32), compiler_params=_params())(parts)


def _add(a, b, name, out_dtype):
    r, cdim = a.shape
    tr = min(r, 256)

    def body(a_ref, b_ref, o_ref):
        o_ref[...] = (a_ref[...] + b_ref[...]).astype(out_dtype)

    spec = pl.BlockSpec((tr, cdim), lambda i: (i, 0))
    return _call(body, name=name, out_shape=jax.ShapeDtypeStruct(a.shape, out_dtype), grid=(r // tr,),
                 in_specs=[spec, spec], out_specs=spec, compiler_params=_params(("parallel",)))(a, b)


def _sum_chips(parts, name):
    _, r, cdim = parts.shape
    tr = min(r, 128)

    def body(p_ref, o_ref):
        o_ref[...] = ((p_ref[0].astype(F32) + p_ref[1].astype(F32)) + p_ref[2].astype(F32)) + p_ref[3].astype(F32)

    return _call(body, name=name, out_shape=jax.ShapeDtypeStruct((r, cdim), F32), grid=(r // tr,),
                 in_specs=[pl.BlockSpec((N_CHIPS, tr, cdim), lambda i: (0, i, 0))],
                 out_specs=pl.BlockSpec((tr, cdim), lambda i: (i, 0)), compiler_params=_params(("parallel",)))(parts)


def _adamw(w, g, m, v, name):
    r, cdim = w.shape[-2:]
    lead = w.ndim - 2
    tr = r if r <= 256 else 256
    c1 = 1.0 / (1.0 - ADAM_B1 ** ADAM_STEP)
    c2 = 1.0 / (1.0 - ADAM_B2 ** ADAM_STEP)

    def body(w_ref, g_ref, m_ref, v_ref, go_ref, d_ref, nm_ref, nv_ref):
        gv = g_ref[...].reshape(go_ref.shape)
        nm = ADAM_B1 * m_ref[...] + (1.0 - ADAM_B1) * gv
        nv = ADAM_B2 * v_ref[...] + (1.0 - ADAM_B2) * (gv * gv)
        m_hat = nm * c1
        v_hat = nv * c2
        go_ref[...] = gv
        d_ref[...] = -ADAM_LR * (m_hat / (jnp.sqrt(v_hat) + ADAM_EPS) + ADAM_WD * w_ref[...])
        nm_ref[...] = nm
        nv_ref[...] = nv

    spec = pl.BlockSpec((1,) * lead + (tr, cdim), lambda i: (0,) * lead + (i, 0))
    shp = jax.ShapeDtypeStruct(w.shape, F32)
    return _call(body, name=name, out_shape=(shp,) * 4, grid=(r // tr,),
                 in_specs=[spec, pl.BlockSpec((tr, cdim), lambda i: (i, 0)), spec, spec],
                 out_specs=(spec,) * 4, compiler_params=_params(("parallel",)))(w, g, m, v)


def _head_of_row(r, nc):
    assert nc & (nc - 1) == 0
    return lax.shift_right_logical(r, nc.bit_length() - 1)


def _chunk_mats(rows, nc, reverse):
    ri = lax.broadcasted_iota(jnp.int32, (rows, rows), 0)
    ci = lax.broadcasted_iota(jnp.int32, (rows, rows), 1)
    same = _head_of_row(ri, nc) == _head_of_row(ci, nc)
    between = jnp.where(same & ((ci > ri) if reverse else (ci < ri)), 1.0, 0.0).astype(F32)
    li = lax.broadcasted_iota(jnp.int32, (LANES, LANES), 0)
    lj = lax.broadcasted_iota(jnp.int32, (LANES, LANES), 1)
    within = jnp.where((li >= lj) if reverse else (li <= lj), 1.0, 0.0).astype(F32)
    return between, within


def _dot_hi(a, b):
    return jnp.dot(a, b, preferred_element_type=F32, precision=lax.Precision.HIGHEST)


def _scan_rows(t, nc, reverse):
    between, within = _chunk_mats(t.shape[0], nc, reverse)
    inner = _dot_hi(t, within)
    tot = jnp.sum(t, axis=1, keepdims=True)
    return inner + _dot_hi(between, jnp.broadcast_to(tot, t.shape))


def _log_forget_cumsum(f_rows, bias_rows, nc):
    def body(f_ref, b_ref, cum_ref):
        z = f_ref[...] + b_ref[...]
        lf = jnp.minimum(z, 0.0) - jnp.log(1.0 + jnp.exp(-jnp.abs(z)))
        cum_ref[...] = _scan_rows(lf, nc, False)

    return _call(body, name="forget_cumsum", out_shape=jax.ShapeDtypeStruct(f_rows.shape, F32),
                 compiler_params=_params())(f_rows, bias_rows)


def _log_forget_cumsum_bwd(dcum_rows, f_rows, bias_rows, nc):
    rows = f_rows.shape[0]

    def body(d_ref, f_ref, b_ref, df_ref, db_ref):
        dlf = _scan_rows(d_ref[...], nc, True)
        z = f_ref[...] + b_ref[...]
        df = dlf * _sigmoid(-z)
        df_ref[...] = df
        hi = lax.broadcasted_iota(jnp.int32, (FOX_HEADS, rows), 0)
        ri = lax.broadcasted_iota(jnp.int32, (FOX_HEADS, rows), 1)
        sel = jnp.where(_head_of_row(ri, nc) == hi, 1.0, 0.0).astype(F32)
        db_ref[...] = jnp.sum(_dot_hi(sel, df), axis=1, keepdims=True)

    return _call(body, name="forget_cumsum_bwd",
                 out_shape=(jax.ShapeDtypeStruct(f_rows.shape, F32), jax.ShapeDtypeStruct((FOX_HEADS, 1), F32)),
                 compiler_params=_params())(dcum_rows, f_rows, bias_rows)


def _rms_hat(xv):
    rstd = lax.rsqrt(jnp.mean(xv * xv, axis=-1, keepdims=True) + RMS_EPS)
    return xv * rstd, rstd


def _modulated(x_ref, g_ref, sc_ref, sh_ref):
    xhat, _ = _rms_hat(x_ref[...])
    return ((xhat * g_ref[...]) * sc_ref[...] + sh_ref[...]).astype(BF)


def _forget_logits(x, g_pre, scale1p, shift, w_f, tm):
    s = x.shape[0]

    def body(x_ref, g_ref, sc_ref, sh_ref, w_ref, f_ref):
        f_ref[...] = _dot(_modulated(x_ref, g_ref, sc_ref, sh_ref), w_ref[...])

    vec = _full((1, D_MODEL))
    return _call(
        body, name="forget_logits", out_shape=jax.ShapeDtypeStruct((s, LANES), F32), grid=(s // tm,),
        in_specs=[pl.BlockSpec((tm, D_MODEL), lambda i: (i, 0)), vec, vec, vec, _full((D_MODEL, LANES))],
        out_specs=pl.BlockSpec((tm, LANES), lambda i: (i, 0)), compiler_params=_params(("parallel",)),
    )(x, g_pre, scale1p, shift, w_f)


def _split3(v):
    hi = v.astype(BF).astype(F32)
    mid = (v - hi).astype(BF).astype(F32)
    lo = ((v - hi) - mid).astype(BF).astype(F32)
    return hi, mid, lo


def _in_proj(x, g_pre, scale1p, shift, w_rows, w_t_fox, cum, cos_t, sin_t, tm):
    s = x.shape[0]
    r_va, r_za, r_qb, r_zb, r_kb, r_vb = 0, 512, 1024, 1536, 2048, 2176

    def body(x_ref, g_ref, sc_ref, sh_ref, w_ref, wt_ref, cum_ref, cos_ref, sin_ref,
             h_ref, qat_ref, ka_ref, kat_ref, v_ref, vt_ref, za_ref, zb_ref, qb_ref, kb_ref, vb_ref,
             qbt_ref, kbt_ref, vbt_ref, mo_ref):
        hb = _modulated(x_ref, g_ref, sc_ref, sh_ref)
        h_ref[...] = hb

        def sec(c0, width):
            return _dot(hb, w_ref[:, c0:c0 + width])

        def sec_t(r0):
            return _dot_nt(wt_ref[r0:r0 + FOX_W, :], hb)

        q_t = sec_t(0) * Q_SCALE
        k_t = sec_t(FOX_W)
        v_t = sec_t(2 * FOX_W)
        va = sec(r_va, FOX_W)
        zeros = jnp.zeros((AUG_DIM - HEAD_DIM - AUG_ROWS, tm), F32)
        ri = lax.broadcasted_iota(jnp.int32, (AUG_ROWS, tm), 0)
        const = jnp.where(ri == AUG_ROWS - 1, 0.0, 1.0)
        ri_v = lax.broadcasted_iota(jnp.int32, (VT_ROWS - HEAD_DIM, tm), 0)
        v_feat = jnp.where(ri_v == 0, 1.0, 0.0).astype(BF)
        for hd in range(FOX_HEADS):
            rows = slice(hd * HEAD_DIM, (hd + 1) * HEAD_DIM)
            cum2 = cum_ref[hd:hd + 1, :] * LOG2E
            hi, mid, lo = (jnp.broadcast_to(part, (AUG_ROWS, tm)) for part in _split3(cum2))
            q_feat = jnp.where(ri == 1, hi, jnp.where(ri == 2, mid, jnp.where(ri == 3, lo, const)))
            k_feat = jnp.where(ri == 4, -hi, jnp.where(ri == 5, -mid, jnp.where(ri == 6, -lo, const)))
            q_aug = jnp.concatenate([q_t[rows], q_feat, zeros], axis=0)
            k_aug = jnp.concatenate([k_t[rows], k_feat, zeros], axis=0)
            mo_ref[hd:hd + 1, :] = jnp.sum(q_t[rows] * k_t[rows], axis=0, keepdims=True) + 1.0
            qat_ref[hd] = q_aug.astype(BF)
            kat_ref[hd] = k_aug.astype(BF)
            ka_ref[hd] = k_aug.T.astype(BF)
            vt_ref[hd] = jnp.concatenate([v_t[rows].astype(BF), v_feat], axis=0)
            v_ref[hd] = va[:, rows].astype(BF)
        za_ref[...] = sec(r_za, FOX_W)
        zb_ref[...] = sec(r_zb, SWA_W)
        cos2, sin2 = cos_ref[...], sin_ref[...]
        cos8 = jnp.concatenate([cos2] * 4, axis=1)
        sin8 = jnp.concatenate([sin2] * 4, axis=1)
        qb = sec(r_qb, SWA_W)
        qb = (qb * cos8 + _rope_partner(qb) * sin8) * (HEAD_DIM ** -0.5)
        qb_ref[...] = qb.astype(BF)
        for a in range(SWA_W // LANES):
            qbt_ref[a * LANES:(a + 1) * LANES, :] = qb[:, a * LANES:(a + 1) * LANES].T.astype(BF)
        kb = sec(r_kb, SWA_KV_W)
        kb = kb * cos2 + _rope_partner(kb) * sin2
        vb = sec(r_vb, SWA_KV_W)
        kb_t, vb_t = kb.T, vb.T
        for hd in range(SWA_KV_HEADS):
            sl = slice(hd * HEAD_DIM, (hd + 1) * HEAD_DIM)
            kb_ref[hd] = kb[:, sl].astype(BF)
            vb_ref[hd] = vb[:, sl].astype(BF)
            kbt_ref[hd] = kb_t[sl].astype(BF)
            vbt_ref[hd] = jnp.concatenate([vb_t[sl].astype(BF), v_feat], axis=0)

    row = lambda w: pl.BlockSpec((tm, w), lambda i: (i, 0))
    heads = lambda n, w=HEAD_DIM: pl.BlockSpec((n, tm, w), lambda i: (0, i, 0))
    heads_t = lambda w: pl.BlockSpec((FOX_HEADS, w, tm), lambda i: (0, 0, i))
    vec = _full((1, D_MODEL))
    hs = lambda a, b: jax.ShapeDtypeStruct((FOX_HEADS, a, b), BF)
    out_shape = (
        jax.ShapeDtypeStruct((s, D_MODEL), BF),
        hs(AUG_DIM, s), hs(s, AUG_DIM), hs(AUG_DIM, s), hs(s, HEAD_DIM), hs(VT_ROWS, s),
        jax.ShapeDtypeStruct((s, FOX_W), F32), jax.ShapeDtypeStruct((s, SWA_W), F32),
        jax.ShapeDtypeStruct((s, SWA_W), BF),
        jax.ShapeDtypeStruct((SWA_KV_HEADS, s, HEAD_DIM), BF), jax.ShapeDtypeStruct((SWA_KV_HEADS, s, HEAD_DIM), BF),
        jax.ShapeDtypeStruct((SWA_W, s), BF),
        jax.ShapeDtypeStruct((SWA_KV_HEADS, HEAD_DIM, s), BF), jax.ShapeDtypeStruct((SWA_KV_HEADS, VT_ROWS, s), BF),
        jax.ShapeDtypeStruct((FOX_HEADS, s), F32),
    )
    kv_t = lambda w: pl.BlockSpec((SWA_KV_HEADS, w, tm), lambda i: (0, 0, i))
    return _call(
        body, name="in_proj", out_shape=out_shape, grid=(s // tm,),
        in_specs=[row(D_MODEL), vec, vec, vec, _full(w_rows.shape), _full(w_t_fox.shape),
                  pl.BlockSpec((FOX_HEADS, tm), lambda i: (0, i)), row(LANES), row(LANES)],
        out_specs=(row(D_MODEL), heads_t(AUG_DIM), heads(FOX_HEADS, AUG_DIM), heads_t(AUG_DIM), heads(FOX_HEADS),
                   heads_t(VT_ROWS), row(FOX_W), row(SWA_W), row(SWA_W), heads(SWA_KV_HEADS), heads(SWA_KV_HEADS),
                   pl.BlockSpec((SWA_W, tm), lambda i: (0, i)), kv_t(HEAD_DIM), kv_t(VT_ROWS),
                   pl.BlockSpec((FOX_HEADS, tm), lambda i: (0, i))),
        compiler_params=_params(("parallel",)),
    )(x, g_pre, scale1p, shift, w_rows, w_t_fox, cum, cos_t, sin_t)


def _diag_chunks(d, bq, bk, chunk):
    out = []
    for c0 in range(0, bq, chunk):
        if d is None or d * bk + bk - 1 <= c0:
            out.append((c0, None))
        elif d * bk <= c0 + chunk - 1:
            kpos = d * bk + lax.broadcasted_iota(jnp.int32, (bk, chunk), 0)
            qpos = c0 + lax.broadcasted_iota(jnp.int32, (bk, chunk), 1)
            out.append((c0, kpos <= qpos))
    return out


def _fox_fwd(qat, ka, vt, m_own, bq, bk, chunk, running_max):
    nh, _, s = qat.shape
    r = bq // bk

    pairs = [(i, j) for i in range(s // bq) for j in range(i * r + r)]

    def body(i_tab, j_tab, ka_ref, qat_ref, vt_ref, mo_ref, o_ref, lse_ref, bad_ref, *rest):
        pt_ref, m_scr, acc_scr = (None,) * running_max + rest
        i, j = i_tab[pl.program_id(1)], j_tab[pl.program_id(1)]

        @pl.when(j == 0)
        def _():
            m_scr[...] = jnp.full(m_scr.shape, NEG, F32) if running_max else mo_ref[0]
            acc_scr[...] = jnp.zeros(acc_scr.shape, F32)

        def careful(d):
            kv, vtv = ka_ref[0], vt_ref[0]

            def one_chunk(n, carry):
                c0 = pl.multiple_of(n * chunk, chunk)
                cs = pl.ds(c0, chunk)
                sc = _dot(kv, qat_ref[0, :, cs])
                if d is not None:
                    kpos = d * bk + lax.broadcasted_iota(jnp.int32, (bk, chunk), 0)
                    qpos = c0 + lax.broadcasted_iota(jnp.int32, (bk, chunk), 1)
                    sc = jnp.where(kpos <= qpos, sc, NEG)
                m_prev = m_scr[:, cs]
                m_new = jnp.maximum(m_prev, jnp.max(sc, axis=0, keepdims=True))
                p = jnp.exp2(sc - m_new).astype(BF)
                acc_scr[:, cs] = jnp.exp2(m_prev - m_new) * acc_scr[:, cs] + _dot(vtv, p)
                m_scr[:, cs] = m_new
                return carry

            lax.fori_loop(0, bq // chunk, one_chunk, 0)

        def fast(d):
            kv, vtv = ka_ref[0], vt_ref[0]
            todo = _diag_chunks(d, bq, bk, chunk)
            scores = lambda c0: _dot(kv, qat_ref[0, :, c0:c0 + chunk])
            sc_next = scores(todo[0][0])
            for n, (c0, mask) in enumerate(todo):
                cs = slice(c0, c0 + chunk)
                sc = sc_next
                if n + 1 < len(todo):
                    sc_next = scores(todo[n + 1][0])
                if mask is not None:
                    sc = jnp.where(mask, sc, NEG)
                p = jnp.exp2(sc - m_scr[:, cs]).astype(BF)
                pt_ref[0, :, cs] = p
                acc_scr[:, cs] += _dot(vtv, p)

        step = careful if running_max else fast

        @pl.when(j < i * r)
        def _():
            step(None)

        for d in range(r):
            @pl.when(j == i * r + d)
            def _(d=d):
                step(d)

        @pl.when(j == i * r + r - 1)
        def _():
            l = acc_scr[HEAD_DIM:HEAD_DIM + 1, :]
            o_ref[0] = acc_scr[:HEAD_DIM, :] / l
            lse_ref[0] = m_scr[...] + jnp.log2(l)
            bad_ref[0] = jnp.where(l < OVERFLOW_GUARD, 0.0, 1.0)

    qmap_t = lambda h, t, it, jt: (h, 0, it[t])
    qrow = pl.BlockSpec((1, 1, bq), qmap_t)
    row_shape = jax.ShapeDtypeStruct((nh, 1, s), F32)
    out_shape = (jax.ShapeDtypeStruct((nh, HEAD_DIM, s), F32), row_shape, row_shape)
    out_specs = (pl.BlockSpec((1, HEAD_DIM, bq), qmap_t), qrow, qrow)
    if not running_max:
        out_shape += (jax.ShapeDtypeStruct((nh, s, s), BF),)
        out_specs += (pl.BlockSpec((1, bk, bq), lambda h, t, it, jt: (h, jt[t], it[t])),)
    grid_spec = pltpu.PrefetchScalarGridSpec(
        num_scalar_prefetch=2, grid=(nh, len(pairs)),
        in_specs=[pl.BlockSpec((1, bk, AUG_DIM), lambda h, t, it, jt: (h, jt[t], 0)), pl.BlockSpec((1, AUG_DIM, bq), qmap_t),
                  pl.BlockSpec((1, VT_ROWS, bk), lambda h, t, it, jt: (h, 0, jt[t])), qrow],
        out_specs=out_specs,
        scratch_shapes=[pltpu.VMEM((1, bq), F32), pltpu.VMEM((VT_ROWS, bq), F32)])
    return _call(
        body, name="fox_fwd_running_max" if running_max else "fox_fwd", out_shape=out_shape, grid_spec=grid_spec,
        compiler_params=_params(("parallel", "arbitrary")),
    )(jnp.asarray([p[0] for p in pairs], jnp.int32), jnp.asarray([p[1] for p in pairs], jnp.int32), ka, qat, vt, m_own)


def _fox_bwd(qat, ka, kat, v, dot_, lse, delta, pt, bq, bk, chunk, dq_blk):
    nh, _, s = qat.shape
    r = bq // bk
    nq = s // bq
    stored = pt is not None

    pairs = [(j, i) for j in range(s // bk) for i in range(j // r, nq)]

    def body(j_tab, i_tab, a_ref, b_ref, kat_ref, v_ref, qat_ref, do_ref, dl_ref, dq_ref, dk_ref, dv_ref, dk_scr, dv_scr):
        ka_ref, lse_ref, pt_ref = (None, None, a_ref) if stored else (a_ref, b_ref, None)
        j, i = j_tab[pl.program_id(1)], i_tab[pl.program_id(1)]

        @pl.when(pl.program_id(1) == 0)
        def _():
            dq_ref[...] = jnp.zeros(dq_ref.shape, F32)

        @pl.when(i * r <= j)
        def _():
            dk_scr[...] = jnp.zeros(dk_scr.shape, F32)
            dv_scr[...] = jnp.zeros(dv_scr.shape, F32)

        def step(d):
            ktv, vv = kat_ref[0], v_ref[0]
            kv = None if stored else ka_ref[0]
            todo = _diag_chunks(d, bq, bk, chunk)

            def products(c0):
                cs = slice(c0, c0 + chunk)
                return None if stored else _dot(kv, qat_ref[0, :, cs]), _dot(vv, do_ref[0, :, cs])

            nxt = products(todo[0][0])
            for n, (c0, mask) in enumerate(todo):
                cs = slice(c0, c0 + chunk)
                sc, dp = nxt
                if n + 1 < len(todo):
                    nxt = products(todo[n + 1][0])
                if stored:
                    p_bf = pt_ref[0, :, cs]
                    p = p_bf.astype(F32)
                else:
                    p = jnp.exp2(sc - lse_ref[0, :, cs])
                    if mask is not None:
                        p = jnp.where(mask, p, 0.0)
                    p_bf = p.astype(BF)
                ds = (p * (dp - dl_ref[0, :, cs])).astype(BF)
                dv_scr[...] += _dot_nt(do_ref[0, :, cs], p_bf)
                dk_scr[...] += _dot_nt(qat_ref[0, :VT_ROWS, cs], ds)
                c1 = c0 % dq_blk
                dq_ref[0, i * (bq // dq_blk) + c0 // dq_blk, :, c1:c1 + chunk] += _dot(ktv[:VT_ROWS], ds)

        @pl.when(i * r > j)
        def _():
            step(None)

        for d in range(r):
            @pl.when(j == i * r + d)
            def _(d=d):
                step(d)

        @pl.when(i == nq - 1)
        def _():
            dk_ref[0] = dk_scr[...]
            dv_ref[0] = dv_scr[...]

    qmap = lambda h, t, jt, it: (h, 0, it[t])
    kmap = lambda h, t, jt, it: (h, jt[t], 0)
    kmap_t = lambda h, t, jt, it: (h, 0, jt[t])
    if stored:
        first = [(pt, pl.BlockSpec((1, bk, bq), lambda h, t, jt, it: (h, jt[t], it[t]))),
                 (delta, pl.BlockSpec((1, 1, bq), qmap))]
    else:
        first = [(ka, pl.BlockSpec((1, bk, AUG_DIM), kmap)), (lse, pl.BlockSpec((1, 1, bq), qmap))]
    grid_spec = pltpu.PrefetchScalarGridSpec(
        num_scalar_prefetch=2, grid=(nh, len(pairs)),
        in_specs=[first[0][1], first[1][1], pl.BlockSpec((1, AUG_DIM, bk), kmap_t), pl.BlockSpec((1, bk, HEAD_DIM), kmap),
                  pl.BlockSpec((1, AUG_DIM, bq), qmap), pl.BlockSpec((1, HEAD_DIM, bq), qmap),
                  pl.BlockSpec((1, 1, bq), qmap)],
        out_specs=(pl.BlockSpec((1, s // dq_blk, VT_ROWS, dq_blk), lambda h, t, jt, it: (h, 0, 0, 0)),
                   pl.BlockSpec((1, VT_ROWS, bk), kmap_t), pl.BlockSpec((1, HEAD_DIM, bk), kmap_t)),
        scratch_shapes=[pltpu.VMEM((VT_ROWS, bk), F32), pltpu.VMEM((HEAD_DIM, bk), F32)])
    return _call(
        body, name="fox_bwd" if stored else "fox_bwd_recompute",
        out_shape=(jax.ShapeDtypeStruct((nh, s // dq_blk, VT_ROWS, dq_blk), F32),
                   jax.ShapeDtypeStruct((nh, VT_ROWS, s), F32), jax.ShapeDtypeStruct((nh, HEAD_DIM, s), F32)),
        grid_spec=grid_spec, compiler_params=_params(("parallel", "arbitrary")),
    )(jnp.asarray([p[0] for p in pairs], jnp.int32), jnp.asarray([p[1] for p in pairs], jnp.int32),
      first[0][0], first[1][0], kat, v, qat, dot_, delta)


def _swa_mask(i, tq):
    kpos = i * tq - WINDOW + lax.broadcasted_iota(jnp.int32, (tq + WINDOW, tq), 0)
    qpos = i * tq + lax.broadcasted_iota(jnp.int32, (tq + WINDOW, tq), 1)
    rel = qpos - kpos
    return (rel >= 0) & (rel < WINDOW) & (kpos >= 0)


def _swa_rows(ref, i, tq):
    before = pl.multiple_of(jnp.maximum(i * tq - WINDOW, 0), WINDOW)
    return jnp.concatenate([ref[0, pl.ds(before, WINDOW), :], ref[0, pl.ds(pl.multiple_of(i * tq, tq), tq), :]], axis=0)


def _swa_before(n_rows, tq):
    return pl.BlockSpec((1, n_rows, WINDOW), lambda g, i: (g, 0, jnp.maximum(i * (tq // WINDOW) - 1, 0)))


def _swa_probs_t(sc, mask, sink):
    sc = jnp.where(mask, sc, NEG)
    m = jnp.maximum(jnp.max(sc, axis=0, keepdims=True), sink)
    p = jnp.exp(sc - m)
    e_sink = jnp.exp(sink - m)
    inv_l = 1.0 / (jnp.sum(p, axis=0, keepdims=True) + e_sink)
    return p * inv_l, e_sink * inv_l


def _swa_fwd(qbt, kb, vbt, sinks, tq):
    s = qbt.shape[1]
    gw = SWA_GROUP * HEAD_DIM

    def body(q_ref, k_ref, vb_ref, vc_ref, s_ref, o_ref):
        i = pl.program_id(1)
        mask = _swa_mask(i, tq)
        kw = _swa_rows(k_ref, i, tq)
        vtw = jnp.concatenate([vb_ref[0], vc_ref[0]], axis=1)
        sk = s_ref[0]
        scores = lambda hh: _dot(kw, q_ref[hh * HEAD_DIM:(hh + 1) * HEAD_DIM, :])
        sc_next = scores(0)
        for hh in range(SWA_GROUP):
            rows = slice(hh * HEAD_DIM, (hh + 1) * HEAD_DIM)
            sink = sk[:, hh:hh + 1]
            sc = jnp.where(mask, sc_next, NEG)
            if hh + 1 < SWA_GROUP:
                sc_next = scores(hh + 1)
            m = jnp.maximum(jnp.max(sc, axis=0, keepdims=True), sink)
            acc = _dot(vtw, jnp.exp(sc - m).astype(BF))
            o_ref[rows, :] = acc[:HEAD_DIM] / (acc[HEAD_DIM:HEAD_DIM + 1] + jnp.exp(sink - m))

    kvspec = pl.BlockSpec((1, s, HEAD_DIM), lambda g, i: (g, 0, 0))
    qspec = pl.BlockSpec((gw, tq), lambda g, i: (g, i))
    return _call(
        body, name="swa_fwd", out_shape=jax.ShapeDtypeStruct((SWA_W, s), F32), grid=(SWA_KV_HEADS, s // tq),
        in_specs=[qspec, kvspec, _swa_before(VT_ROWS, tq), pl.BlockSpec((1, VT_ROWS, tq), lambda g, i: (g, 0, i)),
                  pl.BlockSpec((1, 1, SWA_GROUP), lambda g, i: (g, 0, 0))],
        out_specs=qspec, compiler_params=_params(("parallel", "parallel")),
    )(qbt, kb, vbt, vbt, sinks)


def _swa_bwd(qb, qbt, kb, kbt, vb, sinks, dob, dobt, tq):
    s = qb.shape[0]
    gw = SWA_GROUP * HEAD_DIM

    def body(q_ref, qt_ref, k_ref, ktb_ref, ktc_ref, v_ref, s_ref, do_ref, dot_ref, dq_ref, dk_ref, dv_ref, ds_ref):
        i = pl.program_id(1)

        @pl.when(i == 0)
        def _():
            dk_ref[...] = jnp.zeros(dk_ref.shape, F32)
            dv_ref[...] = jnp.zeros(dv_ref.shape, F32)
            ds_ref[...] = jnp.zeros(ds_ref.shape, F32)

        mask = _swa_mask(i, tq)
        kw = _swa_rows(k_ref, i, tq)
        vw = _swa_rows(v_ref, i, tq)
        ktw = jnp.concatenate([ktb_ref[0], ktc_ref[0]], axis=1)
        qv, dov = q_ref[...], do_ref[...]
        sk = s_ref[0]
        dsinks = []
        dk_acc = jnp.zeros((tq + WINDOW, HEAD_DIM), F32)
        dv_acc = jnp.zeros((tq + WINDOW, HEAD_DIM), F32)
        def products(hh):
            rows = slice(hh * HEAD_DIM, (hh + 1) * HEAD_DIM)
            return _dot(kw, qt_ref[rows, :]), _dot(vw, dot_ref[rows, :])

        nxt = products(0)
        for hh in range(SWA_GROUP):
            rows = slice(hh * HEAD_DIM, (hh + 1) * HEAD_DIM)
            sc, dp = nxt
            if hh + 1 < SWA_GROUP:
                nxt = products(hh + 1)
            p, p_sink = _swa_probs_t(sc, mask, sk[:, hh:hh + 1])
            delta = jnp.sum(p * dp, axis=0, keepdims=True)
            dsc = (p * (dp - delta)).astype(BF)
            dq_ref[rows, :] = _dot(ktw, dsc)
            dk_acc = dk_acc + _dot(dsc, qv[:, rows])
            dv_acc = dv_acc + _dot(p.astype(BF), dov[:, rows])
            dsinks.append(-jnp.sum(p_sink * delta, axis=1, keepdims=True))
        before = pl.ds(pl.multiple_of(jnp.maximum(i * tq - WINDOW, 0), WINDOW), WINDOW)
        own = pl.ds(pl.multiple_of(i * tq, tq), tq)
        dk_ref[0, before, :] += dk_acc[:WINDOW]
        dk_ref[0, own, :] += dk_acc[WINDOW:]
        dv_ref[0, before, :] += dv_acc[:WINDOW]
        dv_ref[0, own, :] += dv_acc[WINDOW:]
        ds_ref[0] += jnp.concatenate(dsinks, axis=1)

    kvspec = pl.BlockSpec((1, s, HEAD_DIM), lambda g, i: (g, 0, 0))
    qspec = pl.BlockSpec((tq, gw), lambda g, i: (i, g))
    qspec_t = pl.BlockSpec((gw, tq), lambda g, i: (g, i))
    sspec = pl.BlockSpec((1, 1, SWA_GROUP), lambda g, i: (g, 0, 0))
    kvshape = jax.ShapeDtypeStruct((SWA_KV_HEADS, s, HEAD_DIM), F32)
    return _call(
        body, name="swa_bwd",
        out_shape=(jax.ShapeDtypeStruct((SWA_W, s), F32), kvshape, kvshape,
                   jax.ShapeDtypeStruct((SWA_KV_HEADS, 1, SWA_GROUP), F32)),
        grid=(SWA_KV_HEADS, s // tq),
        in_specs=[qspec, qspec_t, kvspec, _swa_before(HEAD_DIM, tq),
                  pl.BlockSpec((1, HEAD_DIM, tq), lambda g, i: (g, 0, i)), kvspec, sspec, qspec, qspec_t],
        out_specs=(qspec_t, kvspec, kvspec, sspec),
        compiler_params=_params(("parallel", "arbitrary")),
    )(qb, qbt, kb, kbt, kbt, vb, sinks, dob, dobt)


def _pairs_to_rows(ref, n_rows=HEAD_DIM):
    parts = []
    for a in range(0, FOX_HEADS, 2):
        parts.append(jnp.concatenate([ref[a][:n_rows], ref[a + 1][:n_rows]], axis=0).T)
    return jnp.concatenate(parts, axis=1)


def _blocks_to_rows(ref):
    return jnp.concatenate([ref[a:a + LANES, :].T for a in range(0, ref.shape[0], LANES)], axis=1)


def _out_proj(oat, za, obt, zb, x, tgt, w_out, w_out_t, gate, g_post, inv_l, tm):
    s = x.shape[0]

    def body(oat_ref, za_ref, obt_ref, zb_ref, x_ref, t_ref, w_ref, wt_ref, gate_ref, gp_ref, il_ref,
             dout_ref, doat_ref, dla_ref, dza_ref, dob_ref, dobt_ref, dzb_ref, gw_ref, dgate_ref, dgp_ref, loss_ref):
        i = pl.program_id(0)

        @pl.when(i == 0)
        def _():
            gw_ref[...] = jnp.zeros(gw_ref.shape, F32)
            dgate_ref[...] = jnp.zeros(dgate_ref.shape, F32)
            dgp_ref[...] = jnp.zeros(dgp_ref.shape, F32)
            loss_ref[...] = jnp.zeros(loss_ref.shape, F32)

        oa_v = _pairs_to_rows(oat_ref)
        ob_v = _blocks_to_rows(obt_ref)
        za_v, zb_v = za_ref[...], zb_ref[...]
        sga, sgb = _sigmoid(za_v), _sigmoid(zb_v)
        sila, silb = za_v * sga, zb_v * sgb
        u = jnp.concatenate([oa_v * sila, ob_v * silb], axis=1).astype(BF)
        yv = _dot(u, w_ref[...])
        yhat, rstd = _rms_hat(yv)
        gp, gate_v = gp_ref[...], gate_ref[...]
        nrm = yhat * gp
        diff = (x_ref[...] + gate_v * nrm) - t_ref[...]
        loss_ref[...] += 0.5 * jnp.sum(jnp.sum(diff * diff, axis=1, keepdims=True), axis=0, keepdims=True) / D_MODEL
        dout = diff * (1.0 / D_MODEL)
        dout_ref[...] = dout
        dgate_ref[...] += jnp.sum(dout * nrm, axis=0, keepdims=True)
        dn = dout * gate_v
        dgp_ref[...] += jnp.sum(dn * yhat, axis=0, keepdims=True)
        dyhat = dn * gp
        dy = (rstd * (dyhat - yhat * jnp.mean(dyhat * yhat, axis=1, keepdims=True))).astype(BF)
        gw_ref[...] += _dot_tn(u, dy)
        du = _dot(dy, wt_ref[...])
        dua, dub = du[:, :FOX_W], du[:, FOX_W:]
        doa = dua * sila
        for a in range(0, FOX_HEADS, 2):
            pair_t = doa[:, a * HEAD_DIM:(a + 2) * HEAD_DIM].T
            for hd, rows in ((a, slice(0, HEAD_DIM)), (a + 1, slice(HEAD_DIM, 2 * HEAD_DIM))):
                inv_l = il_ref[hd]
                doat_ref[hd] = (pair_t[rows] * inv_l).astype(BF)
                dla_ref[hd] = jnp.sum(pair_t[rows] * oat_ref[hd], axis=0, keepdims=True) * inv_l
        dob = dub * silb
        dob_ref[...] = dob.astype(BF)
        for a in range(0, SWA_W, LANES):
            dobt_ref[a:a + LANES, :] = dob[:, a:a + LANES].T.astype(BF)
        dza_ref[...] = (dua * oa_v * (sga * (1.0 + za_v * (1.0 - sga)))).astype(BF)
        dzb_ref[...] = (dub * ob_v * (sgb * (1.0 + zb_v * (1.0 - sgb)))).astype(BF)

    row = lambda w: pl.BlockSpec((tm, w), lambda i: (i, 0))
    heads_t = lambda w: pl.BlockSpec((FOX_HEADS, w, tm), lambda i: (0, 0, i))
    vec = _full((1, D_MODEL))
    mat = _full((D_MODEL, D_MODEL))
    out_shape = (
        jax.ShapeDtypeStruct((s, D_MODEL), F32),
        jax.ShapeDtypeStruct((FOX_HEADS, HEAD_DIM, s), BF), jax.ShapeDtypeStruct((FOX_HEADS, 1, s), F32),
        jax.ShapeDtypeStruct((s, FOX_W), BF), jax.ShapeDtypeStruct((s, SWA_W), BF), jax.ShapeDtypeStruct((SWA_W, s), BF),
        jax.ShapeDtypeStruct((s, SWA_W), BF),
        jax.ShapeDtypeStruct((D_MODEL, D_MODEL), F32),
        jax.ShapeDtypeStruct((1, D_MODEL), F32), jax.ShapeDtypeStruct((1, D_MODEL), F32),
        jax.ShapeDtypeStruct((1, 1), F32),
    )
    col = pl.BlockSpec((SWA_W, tm), lambda i: (0, i))
    return _call(
        body, name="out_proj", out_shape=out_shape, grid=(s // tm,),
        in_specs=[heads_t(HEAD_DIM), row(FOX_W), col, row(SWA_W), row(D_MODEL), row(D_MODEL), mat, mat, vec, vec,
                  heads_t(1)],
        out_specs=(row(D_MODEL), heads_t(HEAD_DIM), heads_t(1), row(FOX_W), row(SWA_W), col, row(SWA_W), mat, vec, vec,
                   _full((1, 1))),
        compiler_params=_params(("arbitrary",)),
    )(oat, za, obt, zb, x, tgt, w_out, w_out_t, gate, g_post, inv_l)


def _assemble_dproj(dqt, dkt, dvt, dza, dqb, dzb, dkb, dvb, df, cos_t, sin_t, tm):
    s = dza.shape[0]

    def body(dqt_ref, dkt_ref, dvt_ref, dza_ref, dqb_ref, dzb_ref, dkb_ref, dvb_ref, df_ref, cos_ref, sin_ref, o_ref):
        def cat(ref, n):
            return jnp.concatenate([ref[hd] for hd in range(n)], axis=1)

        cos2, sin2 = cos_ref[...], sin_ref[...]
        cos8 = jnp.concatenate([cos2] * 4, axis=1)
        sin8 = jnp.concatenate([sin2] * 4, axis=1)
        scale = HEAD_DIM ** -0.5
        o_ref[:, C_QA:C_QA + FOX_W] = (_pairs_to_rows(dqt_ref.at[:, 0]) * scale).astype(BF)
        o_ref[:, C_KA:C_KA + FOX_W] = (_pairs_to_rows(dkt_ref) * LN2).astype(BF)
        o_ref[:, C_VA:C_VA + FOX_W] = _pairs_to_rows(dvt_ref).astype(BF)
        o_ref[:, C_ZA:C_ZA + FOX_W] = dza_ref[...]
        dq = _blocks_to_rows(dqb_ref) * scale
        o_ref[:, C_QB:C_QB + SWA_W] = (dq * cos8 - _rope_partner(dq) * sin8).astype(BF)
        o_ref[:, C_ZB:C_ZB + SWA_W] = dzb_ref[...]
        dk = cat(dkb_ref, SWA_KV_HEADS)
        o_ref[:, C_KB:C_KB + SWA_KV_W] = (dk * cos2 - _rope_partner(dk) * sin2).astype(BF)
        o_ref[:, C_VB:C_VB + SWA_KV_W] = cat(dvb_ref, SWA_KV_HEADS).astype(BF)
        o_ref[:, C_F:C_F + LANES] = df_ref[...].astype(BF)

    row = lambda w: pl.BlockSpec((tm, w), lambda i: (i, 0))
    heads = lambda n: pl.BlockSpec((n, tm, HEAD_DIM), lambda i: (0, i, 0))
    heads_t = lambda w: pl.BlockSpec((FOX_HEADS, w, tm), lambda i: (0, 0, i))
    return _call(
        body, name="assemble_dproj", out_shape=jax.ShapeDtypeStruct((s, WP), BF), grid=(s // tm,),
        in_specs=[pl.BlockSpec((FOX_HEADS, 1, VT_ROWS, tm), lambda i: (0, i, 0, 0)), heads_t(VT_ROWS), heads_t(HEAD_DIM),
                  row(FOX_W), pl.BlockSpec((SWA_W, tm), lambda i: (0, i)), row(SWA_W), heads(SWA_KV_HEADS),
                  heads(SWA_KV_HEADS), row(LANES), row(LANES), row(LANES)],
        out_specs=row(WP), compiler_params=_params(("parallel",)),
    )(dqt, dkt, dvt, dza, dqb, dzb, dkb, dvb, df, cos_t, sin_t)


def _in_proj_bwd_x(dproj, w_al_t, x, dout, g_pre, scale1p, tm, parts):
    s = x.shape[0]
    n_steps = s // tm
    masks = [(1, 0), (0, 1), (1, 1)]

    def body(dp_ref, wt_ref, x_ref, dout_ref, g_ref, sc_ref, parts_ref, gx_ref, dsh_ref, dsc_ref, dg_ref, got_ref,
             send_sems, recv_sems, local_sem):
        i = pl.program_id(0)
        cx, cy, cc = lax.axis_index("x"), lax.axis_index("y"), lax.axis_index("c")
        me = 2 * cx + cy
        own = pltpu.make_async_copy(parts_ref.at[me], got_ref.at[me], local_sem)

        def copy(k, send):
            dx, dy = masks[k]
            peer = 2 * (cx ^ dx) + (cy ^ dy)
            return pltpu.make_async_remote_copy(
                src_ref=parts_ref.at[peer if send else me], dst_ref=got_ref.at[me if send else peer],
                send_sem=send_sems.at[k], recv_sem=recv_sems.at[k], device_id=(cx ^ dx, cy ^ dy, cc), device_id_type=MESH)

        @pl.when(i == 0)
        def _():
            dsh_ref[...] = jnp.zeros(dsh_ref.shape, F32)
            dsc_ref[...] = jnp.zeros(dsc_ref.shape, F32)
            dg_ref[...] = jnp.zeros(dg_ref.shape, F32)
            own.start()
            for k in range(len(masks)):
                copy(k, True).start()

        @pl.when(i == n_steps - 1)
        def _():
            for k in range(len(masks)):
                copy(k, False).wait_recv()
            for k in range(len(masks)):
                copy(k, True).wait_send()
            own.wait()

        dh = _dot(dp_ref[...], wt_ref[...])
        xhat, rstd = _rms_hat(x_ref[...])
        g, sc = g_ref[...], sc_ref[...]
        dsh_ref[...] += jnp.sum(dh, axis=0, keepdims=True)
        dhx = dh * xhat
        dsc_ref[...] += jnp.sum(dhx * g, axis=0, keepdims=True)
        dg_ref[...] += jnp.sum(dhx * sc, axis=0, keepdims=True)
        dxhat = dh * (g * sc)
        gx_ref[...] = dout_ref[...] + rstd * (dxhat - xhat * jnp.mean(dxhat * xhat, axis=1, keepdims=True))

    row = lambda w: pl.BlockSpec((tm, w), lambda i: (i, 0))
    vec = _full((1, D_MODEL))
    vshape = jax.ShapeDtypeStruct((1, D_MODEL), F32)
    hbm = pl.BlockSpec(memory_space=pl.ANY)
    return _call(
        body, name="in_proj_bwd_x",
        out_shape=(jax.ShapeDtypeStruct((s, D_MODEL), F32), vshape, vshape, vshape,
                   jax.ShapeDtypeStruct(parts.shape, parts.dtype)),
        grid=(n_steps,),
        in_specs=[row(WP), _full((WP, D_MODEL)), row(D_MODEL), row(D_MODEL), vec, vec, hbm],
        out_specs=(row(D_MODEL), vec, vec, vec, hbm),
        scratch_shapes=[pltpu.SemaphoreType.DMA((3,)), pltpu.SemaphoreType.DMA((3,)), pltpu.SemaphoreType.DMA],
        compiler_params=_params(("arbitrary",), has_side_effects=True),
    )(dproj, w_al_t, x, dout, g_pre, scale1p, parts)


def _in_proj_bwd_w(h, dproj, tk, tn):
    s = h.shape[0]

    def body(h_ref, dp_ref, gw_ref):
        @pl.when(pl.program_id(1) == 0)
        def _():
            gw_ref[...] = jnp.zeros(gw_ref.shape, F32)

        gw_ref[...] += _dot_tn(h_ref[...], dp_ref[...])

    return _call(
        body, name="in_proj_bwd_w", out_shape=jax.ShapeDtypeStruct((D_MODEL, WP), F32), grid=(WP // tn, s // tk),
        in_specs=[pl.BlockSpec((tk, D_MODEL), lambda n, k: (k, 0)), pl.BlockSpec((tk, tn), lambda n, k: (k, n))],
        out_specs=pl.BlockSpec((D_MODEL, tn), lambda n, k: (0, n)),
        compiler_params=_params(("parallel", "arbitrary")),
    )(h, dproj)


def _align_w_in(w_cols):
    def part(name, width):
        return w_cols[:, _SRC[name]:_SRC[name] + width]

    fpad = jnp.pad(part("fa", FOX_HEADS), ((0, 0), (0, LANES - FOX_HEADS)))
    return jnp.concatenate([part("qa", FOX_W), part("ka", FOX_W), part("va", FOX_W), part("za", FOX_W),
                            part("qb", SWA_W), part("zb", SWA_W), part("kb", SWA_KV_W), part("vb", SWA_KV_W), fpad], axis=1)


def _unalign_w_in(g_al):
    def part(c0, width):
        return g_al[:, c0:c0 + width]

    return jnp.concatenate([part(C_QA, FOX_W), part(C_KA, FOX_W), part(C_VA, FOX_W), part(C_F, FOX_HEADS),
                            part(C_ZA, FOX_W), part(C_QB, SWA_W), part(C_KB, SWA_KV_W), part(C_VB, SWA_KV_W),
                            part(C_ZB, SWA_W)], axis=1)


def _rope_tables(positions):
    inv_freq = ROPE_THETA ** (-jnp.arange(HALF, dtype=F32) / HALF)
    ang = positions.astype(F32)[:, None] * inv_freq
    cos, sin = jnp.cos(ang), jnp.sin(ang)
    return jnp.concatenate([cos, cos, cos, cos], axis=1), jnp.concatenate([-sin, sin, -sin, sin], axis=1)


def _tiles(s):
    if s >= 4096:
        return dict(tm=512, blk=512, bq=2048, bk=1024, bk_bwd=1024, chunk=256, tq=256, tm_out=512, tk=1024, tn=1152)
    return dict(tm=128, blk=128, bq=256, bk=128, bk_bwd=128, chunk=128, tq=128, tm_out=128, tk=128, tn=1152)


def kernel(x, c, positions, w_ada, b_ada, g_pre, w_in, b_fgate, sinks, w_out, g_post, loss_target, m_w_ada, m_b_ada, m_g_pre, m_w_in, m_b_fgate, m_sinks, m_w_out, m_g_post, v_w_ada, v_b_ada, v_g_pre, v_w_in, v_b_fgate, v_sinks, v_w_out, v_g_post):
    s = x.shape[1]
    t = _tiles(s)
    nc = s // LANES
    rows = FOX_HEADS * nc
    me = 4 * lax.axis_index("x") + 2 * lax.axis_index("y") + lax.axis_index("c")
    chip = 2 * lax.axis_index("x") + lax.axis_index("y")
    core = lax.axis_index("c")
    x2, tgt = x[0], loss_target[0]

    c_all = _allgather_devices(c, "gather_c")[:, 0, :]
    a_all, mod_shard = _ada_shard(c_all, w_ada[0])
    mod_all = _allgather_devices(mod_shard, "gather_mod")
    mod_rows = lax.dynamic_index_in_dim(mod_all, me, axis=1, keepdims=False)
    mod = mod_rows.reshape(N_CHIPS, 2, W_ADA_SHARD)[:, 0, :].reshape(1, 3 * D_MODEL) + b_ada
    shift, scale1p, gate = mod[:, :D_MODEL], 1.0 + mod[:, D_MODEL:2 * D_MODEL], mod[:, 2 * D_MODEL:]

    w_in_pad = jnp.pad(w_in[0].astype(BF), ((0, 0), (0, W_IN_SHARD_PAD - W_IN_SHARD)))
    w_pack = jnp.concatenate([w_in_pad, w_out[0].astype(BF).reshape(D_MODEL, W_OUT_SHARD)], axis=1)
    w_all = _allgather_chips(w_pack.reshape(2, D_MODEL // 2, -1), "gather_weights").reshape(N_CHIPS, D_MODEL, -1)
    w_cols = jnp.concatenate([w_all[k, :, :W_IN_SHARD] for k in range(N_CHIPS)], axis=1)
    w_al = _align_w_in(w_cols)
    w_al_t = w_al.T
    w_out_all = w_all[:, :, W_IN_SHARD_PAD:].reshape(D_MODEL, D_MODEL)
    w_out_t = w_out_all.T

    cos_t, sin_t = _rope_tables(positions[0])

    f_pad = _forget_logits(x2, g_pre, scale1p, shift, w_al[:, C_F:], t["tm"])
    f_rows = f_pad[:, :FOX_HEADS].T.reshape(rows, LANES)
    bias_rows = jnp.repeat(b_fgate[0], nc)[:, None]
    cum = _log_forget_cumsum(f_rows, bias_rows, nc).reshape(FOX_HEADS, s)
    h, qat, ka, kat, va, vat, za, zb, qb, kb, vb, qbt, kbt, vbt, m_own = _in_proj(
        x2, g_pre, scale1p, shift, w_al[:, C_VA:C_F], w_al_t[:C_ZA], cum, cos_t, sin_t, t["tm"])
    m_own = m_own[:, None, :]
    fox_args = (qat, ka, vat, m_own, t["bq"], t["bk"], t["chunk"])
    oat, lse, bad, pt = _fox_fwd(*fox_args, running_max=False)
    overflowed = jnp.max(bad) > 0.0
    oat, lse = lax.cond(overflowed, lambda: _fox_fwd(*fox_args, running_max=True)[:2], lambda: (oat, lse))
    inv_l = jnp.where(overflowed, 1.0, jnp.exp2(m_own - lse))
    sinks_g = sinks.reshape(SWA_KV_HEADS, 1, SWA_GROUP)
    obt = _swa_fwd(qbt, kb, vbt, sinks_g, t["tq"])

    dout, doat, delta_a, dza, dob, dobt, dzb, gw_out, dgate, dg_post, loss_part = _out_proj(
        oat, za, obt, zb, x2, tgt, w_out_all, w_out_t, gate, g_post, inv_l, t["tm_out"])

    bwd_args = (qat, ka, kat, va, doat, lse, delta_a)
    bwd_tiles = (t["bq"], t["bk_bwd"], t["chunk"], t["blk"])
    dqt, dkt, dvt = lax.cond(overflowed, lambda: _fox_bwd(*bwd_args, None, *bwd_tiles),
                             lambda: _fox_bwd(*bwd_args, pt, *bwd_tiles))
    dcum = dqt[:, :, HEAD_DIM, :].reshape(FOX_HEADS, s) - dkt[:, HEAD_DIM, :]
    df_rows, db_heads = _log_forget_cumsum_bwd(dcum.reshape(rows, LANES), f_rows, bias_rows, nc)
    df_pad = jnp.pad(df_rows.reshape(FOX_HEADS, s).T, ((0, 0), (0, LANES - FOX_HEADS)))
    dqb, dkb, dvb, dsinks = _swa_bwd(qb, qbt, kb, kbt, vb, sinks_g, dob, dobt, t["tq"])

    dproj = _assemble_dproj(dqt, dkt, dvt, dza, dqb, dzb, dkb, dvb, df_pad, cos_t, sin_t, t["blk"])
    gw_in = _unalign_w_in(_in_proj_bwd_w(h, dproj, t["tk"], t["tn"]))

    gin = jnp.pad(gw_in.reshape(D_MODEL, N_CHIPS, W_IN_SHARD).transpose(1, 0, 2),
                  ((0, 0), (0, 0), (0, W_IN_SHARD_PAD - W_IN_SHARD)))
    gout = gw_out.reshape(N_CHIPS, D_MODEL, W_OUT_SHARD)
    gbig = jnp.concatenate([gin, gout], axis=2)
    half = D_MODEL // 2
    gw = W_IN_SHARD_PAD + W_OUT_SHARD
    keep = lax.dynamic_slice_in_dim(gbig, core * half, half, axis=1)
    give = lax.dynamic_slice_in_dim(gbig, (1 - core) * half, half, axis=1)
    got = _swap_sibling(give.reshape(N_CHIPS * half, gw), "swap_grad_halves")
    pair = _add(keep.reshape(N_CHIPS * half, gw), got, "add_pair", BF).reshape(N_CHIPS, half, gw)
    grad_x, dshift, dscale, dg_pre, from_chips = _in_proj_bwd_x(
        dproj, w_al_t, x2, dout, g_pre, scale1p, t["tm_out"], pair)

    pad_lane = lambda vrow: jnp.pad(vrow, ((0, 0), (0, LANES - vrow.shape[1])))
    packed = jnp.concatenate([dshift, dscale, dgate, dg_pre, dg_post,
                              pad_lane(db_heads.reshape(1, FOX_HEADS)), pad_lane(dsinks.reshape(1, FOX_HEADS)),
                              pad_lane(loss_part)], axis=1)
    parts = _allgather_devices(packed, "gather_partials")
    tot = _sum_devices(parts)
    loss = tot[0, P_LOSS]
    g_b_ada = tot[:, P_DMOD:P_DMOD + 3 * D_MODEL]
    g_g_pre = tot[:, P_GPRE:P_GPRE + D_MODEL]
    g_g_post = tot[:, P_GPOST:P_GPOST + D_MODEL]
    g_b_fgate = tot[:, P_BF:P_BF + FOX_HEADS]
    g_sinks = tot[:, P_SINK:P_SINK + FOX_HEADS]
    dm_shard = lax.dynamic_slice_in_dim(parts[:, 0, :3 * D_MODEL], chip * W_ADA_SHARD, W_ADA_SHARD, axis=1)
    g_w_ada = _grad_w_ada(a_all.T, dm_shard)

    mine = _sum_chips(from_chips, "sum_chips")
    other = _swap_sibling(mine, "swap_grad_result")
    lo = jnp.where(core == 0, mine, other)
    hi = jnp.where(core == 0, other, mine)
    gfull = jnp.concatenate([lo, hi], axis=0)
    g_w_in = gfull[:, :W_IN_SHARD]
    g_w_out = gfull[:, W_IN_SHARD_PAD:].reshape(W_OUT_SHARD, D_MODEL)

    grads = dict(w_ada=g_w_ada, b_ada=g_b_ada, g_pre=g_g_pre, w_in=g_w_in, b_fgate=g_b_fgate, sinks=g_sinks,
                 w_out=g_w_out, g_post=g_g_post)
    weights = dict(w_ada=w_ada, b_ada=b_ada, g_pre=g_pre, w_in=w_in, b_fgate=b_fgate, sinks=sinks, w_out=w_out, g_post=g_post)
    moms = dict(w_ada=m_w_ada, b_ada=m_b_ada, g_pre=m_g_pre, w_in=m_w_in, b_fgate=m_b_fgate, sinks=m_sinks, w_out=m_w_out, g_post=m_g_post)
    vars_ = dict(w_ada=v_w_ada, b_ada=v_b_ada, g_pre=v_g_pre, w_in=v_w_in, b_fgate=v_b_fgate, sinks=v_sinks, w_out=v_w_out, g_post=v_g_post)
    names = ["w_ada", "b_ada", "g_pre", "w_in", "b_fgate", "sinks", "w_out", "g_post"]
    g_out, d_out, m_out, v_out = [], [], [], []
    for n in names:
        g2 = grads[n].reshape(weights[n].shape[-2:])
        go, d, nm, nv = _adamw(weights[n], g2, moms[n], vars_[n], "adamw_" + n)
        g_out.append(go)
        d_out.append(d)
        m_out.append(nm)
        v_out.append(nv)
    return (loss, grad_x.reshape(x.shape), *g_out, *d_out, *m_out, *v_out)
```

```python
import functools

import jax
import jax.numpy as jnp
from jax import lax
from jax.experimental import pallas as pl
from jax.experimental.pallas import tpu as pltpu

_INTERPRET = False

D_MODEL = 1024
HEAD_DIM = 64
HALF = HEAD_DIM // 2
AUG_DIM = 128
AUG_ROWS = 8
VT_ROWS = 80
LOG2E = 1.4426950408889634
LN2 = 0.6931471805599453
Q_SCALE = LOG2E * 64 ** -0.5
FOX_HEADS = 8
FOX_W = 512
SWA_W = 512
SWA_KV_HEADS = 2
SWA_GROUP = 4
SWA_KV_W = 128
WINDOW = 128
ROPE_THETA = 10000.0
RMS_EPS = 1e-6
IN_WIDTH = 3336
N_CHIPS = 4
N_DEV = 8
W_IN_SHARD = IN_WIDTH // N_CHIPS
W_IN_SHARD_PAD = 896
W_ADA_SHARD = 3 * D_MODEL // N_CHIPS
W_OUT_SHARD = D_MODEL // N_CHIPS
LANES = 128

_SRC = dict(qa=0, ka=512, va=1024, fa=1536, za=1544, qb=2056, kb=2568, vb=2696, zb=2824)
C_QA, C_KA, C_VA, C_ZA, C_QB, C_ZB, C_KB, C_VB, C_F = 0, 512, 1024, 1536, 2048, 2560, 3072, 3200, 3328
WP = 3456

ADAM_LR = 0.001
ADAM_B1 = 0.9
ADAM_B2 = 0.999
ADAM_EPS = 1e-08
ADAM_WD = 0.01
ADAM_STEP = 10

VMEM_LIMIT = 56 * 1024 * 1024
NEG = -1e30
OVERFLOW_GUARD = 1e30
MESH = pl.DeviceIdType.MESH
BF = jnp.bfloat16
F32 = jnp.float32

P_DMOD, P_GPRE, P_GPOST, P_BF, P_SINK, P_LOSS, P_LEN = 0, 3072, 4096, 5120, 5248, 5376, 5504


def _call(body, **kw):
    return pl.pallas_call(body, interpret=_INTERPRET, **kw)


def _params(sem=None, **kw):
    return pltpu.CompilerParams(dimension_semantics=sem, vmem_limit_bytes=VMEM_LIMIT, **kw)


def _full(shape):
    zeros = (0,) * len(shape)
    return pl.BlockSpec(shape, lambda *_: zeros)


def _dot(a, b):
    return jnp.dot(a, b, preferred_element_type=F32)


def _dot_nt(a, b):
    return lax.dot_general(a, b, (((1,), (1,)), ((), ())), preferred_element_type=F32)


def _dot_tn(a, b):
    return lax.dot_general(a, b, (((0,), (0,)), ((), ())), preferred_element_type=F32)


def _sigmoid(z):
    return 1.0 / (1.0 + jnp.exp(-z))


def _rope_partner(t):
    w = t.shape[-1]
    lane = lax.broadcasted_iota(jnp.int32, t.shape, t.ndim - 1)
    return jnp.where((lane & (HEAD_DIM - 1)) < HALF, pltpu.roll(t, w - HALF, t.ndim - 1), pltpu.roll(t, HALF, t.ndim - 1))


def _allgather_devices(v, name):
    r, cdim = v.shape
    masks = [(dx, dy, dc) for dx in (0, 1) for dy in (0, 1) for dc in (0, 1)][1:]

    def body(v_ref, out_ref, send_sems, recv_sems):
        x, y, c = lax.axis_index("x"), lax.axis_index("y"), lax.axis_index("c")
        me = 4 * x + 2 * y + c
        out_ref[me] = v_ref[...]
        copies = []
        for k, (dx, dy, dc) in enumerate(masks):
            cp = pltpu.make_async_remote_copy(
                src_ref=v_ref, dst_ref=out_ref.at[me], send_sem=send_sems.at[k], recv_sem=recv_sems.at[k],
                device_id=(x ^ dx, y ^ dy, c ^ dc), device_id_type=MESH)
            cp.start()
            copies.append(cp)
        for k, (dx, dy, dc) in enumerate(masks):
            peer = 4 * (x ^ dx) + 2 * (y ^ dy) + (c ^ dc)
            pltpu.make_async_remote_copy(
                src_ref=v_ref, dst_ref=out_ref.at[peer], send_sem=send_sems.at[k], recv_sem=recv_sems.at[k],
                device_id=(x ^ dx, y ^ dy, c ^ dc), device_id_type=MESH).wait_recv()
        for cp in copies:
            cp.wait_send()

    return _call(
        body, name=name, out_shape=jax.ShapeDtypeStruct((N_DEV, r, cdim), v.dtype),
        in_specs=[pl.BlockSpec(memory_space=pltpu.VMEM)], out_specs=pl.BlockSpec(memory_space=pltpu.VMEM),
        scratch_shapes=[pltpu.SemaphoreType.DMA((7,)), pltpu.SemaphoreType.DMA((7,))],
        compiler_params=pltpu.CompilerParams(has_side_effects=True),
    )(v)


def _allgather_chips(v, name):
    _, r, cdim = v.shape
    masks = [(1, 0), (0, 1), (1, 1)]
    n = len(masks)

    def body(v_ref, out_ref, send_sems, recv_sems, local_sem):
        x, y, c = lax.axis_index("x"), lax.axis_index("y"), lax.axis_index("c")
        me = 2 * x + y
        mine = pltpu.make_async_copy(v_ref, out_ref.at[me], local_sem)
        mine.start()

        def copy(k, chip, half, to):
            return pltpu.make_async_remote_copy(
                src_ref=v_ref.at[half] if k < n else out_ref.at[chip, half], dst_ref=out_ref.at[chip, half],
                send_sem=send_sems.at[k], recv_sem=recv_sems.at[k], device_id=to, device_id_type=MESH)

        first = [copy(k, me, c, (x ^ dx, y ^ dy, c)) for k, (dx, dy) in enumerate(masks)]
        for cp in first:
            cp.start()
        passed = []
        for k, (dx, dy) in enumerate(masks):
            peer = 2 * (x ^ dx) + (y ^ dy)
            copy(k, peer, c, (x, y, c)).wait_recv()
            cp = copy(n + k, peer, c, (x, y, 1 - c))
            cp.start()
            passed.append(cp)
        for k, (dx, dy) in enumerate(masks):
            copy(n + k, 2 * (x ^ dx) + (y ^ dy), 1 - c, (x, y, c)).wait_recv()
        for cp in first + passed:
            cp.wait_send()
        mine.wait()

    return _call(
        body, name=name, out_shape=jax.ShapeDtypeStruct((N_CHIPS, 2, r, cdim), v.dtype),
        in_specs=[pl.BlockSpec(memory_space=pl.ANY)], out_specs=pl.BlockSpec(memory_space=pl.ANY),
        scratch_shapes=[pltpu.SemaphoreType.DMA((2 * n,)), pltpu.SemaphoreType.DMA((2 * n,)), pltpu.SemaphoreType.DMA],
        compiler_params=pltpu.CompilerParams(has_side_effects=True),
    )(v)


def _swap_sibling(v, name):
    def body(v_ref, out_ref, send_sem, recv_sem):
        x, y, c = lax.axis_index("x"), lax.axis_index("y"), lax.axis_index("c")
        cp = pltpu.make_async_remote_copy(
            src_ref=v_ref, dst_ref=out_ref, send_sem=send_sem, recv_sem=recv_sem,
            device_id=(x, y, 1 - c), device_id_type=MESH)
        cp.start()
        cp.wait()

    return _call(
        body, name=name, out_shape=jax.ShapeDtypeStruct(v.shape, v.dtype),
        in_specs=[pl.BlockSpec(memory_space=pl.ANY)], out_specs=pl.BlockSpec(memory_space=pl.ANY),
        scratch_shapes=[pltpu.SemaphoreType.DMA, pltpu.SemaphoreType.DMA],
        compiler_params=pltpu.CompilerParams(has_side_effects=True),
    )(v)


def _ada_shard(c_all, w_ada_shard):
    def body(c_ref, w_ref, a_ref, mod_ref):
        cv = c_ref[...]
        a = cv * _sigmoid(cv)
        a_ref[...] = a
        mod_ref[...] = _dot(a.astype(BF), w_ref[...].astype(BF))

    return _call(
        body, name="ada_shard",
        out_shape=(jax.ShapeDtypeStruct((N_DEV, D_MODEL), F32), jax.ShapeDtypeStruct((N_DEV, W_ADA_SHARD), F32)),
        compiler_params=_params(),
    )(c_all, w_ada_shard)


def _grad_w_ada(a_t, dm_shard):
    def body(a_ref, dm_ref, out_ref):
        acc = jnp.zeros((D_MODEL, W_ADA_SHARD), F32)
        for b in range(N_DEV):
            acc = acc + a_ref[:, b:b + 1] * dm_ref[b:b + 1, :]
        out_ref[...] = acc

    return _call(body, name="grad_w_ada", out_shape=jax.ShapeDtypeStruct((D_MODEL, W_ADA_SHARD), F32),
                 compiler_params=_params())(a_t, dm_shard)


def _sum_devices(parts):
    n = parts.shape[-1]

    def body(p_ref, out_ref):
        acc = p_ref[0]
        for b in range(1, N_DEV):
            acc = acc + p_ref[b]
        out_ref[...] = acc

    return _call(body, name="sum_devices", out_shape=jax.ShapeDtypeStruct((1, n), F32), compiler_params=_params())(parts)


def _add(a, b, name, out_dtype):
    r, cdim = a.shape
    tr = min(r, 256)

    def body(a_ref, b_ref, o_ref):
        o_ref[...] = (a_ref[...] + b_ref[...]).astype(out_dtype)

    spec = pl.BlockSpec((tr, cdim), lambda i: (i, 0))
    return _call(body, name=name, out_shape=jax.ShapeDtypeStruct(a.shape, out_dtype), grid=(r // tr,),
                 in_specs=[spec, spec], out_specs=spec, compiler_params=_params(("parallel",)))(a, b)


def _sum_chips(parts, name):
    _, r, cdim = parts.shape
    tr = min(r, 128)

    def body(p_ref, o_ref):
        o_ref[...] = ((p_ref[0].astype(F32) + p_ref[1].astype(F32)) + p_ref[2].astype(F32)) + p_ref[3].astype(F32)

    return _call(body, name=name, out_shape=jax.ShapeDtypeStruct((r, cdim), F32), grid=(r // tr,),
                 in_specs=[pl.BlockSpec((N_CHIPS, tr, cdim), lambda i: (0, i, 0))],
                 out_specs=pl.BlockSpec((tr, cdim), lambda i: (i, 0)), compiler_params=_params(("parallel",)))(parts)


def _adamw(w, g, m, v, name):
    r, cdim = w.shape[-2:]
    lead = w.ndim - 2
    tr = r if r <= 256 else 256
    c1 = 1.0 / (1.0 - ADAM_B1 ** ADAM_STEP)
    c2 = 1.0 / (1.0 - ADAM_B2 ** ADAM_STEP)

    def body(w_ref, g_ref, m_ref, v_ref, go_ref, d_ref, nm_ref, nv_ref):
        gv = g_ref[...].reshape(go_ref.shape)
        nm = ADAM_B1 * m_ref[...] + (1.0 - ADAM_B1) * gv
        nv = ADAM_B2 * v_ref[...] + (1.0 - ADAM_B2) * (gv * gv)
        m_hat = nm * c1
        v_hat = nv * c2
        go_ref[...] = gv
        d_ref[...] = -ADAM_LR * (m_hat / (jnp.sqrt(v_hat) + ADAM_EPS) + ADAM_WD * w_ref[...])
        nm_ref[...] = nm
        nv_ref[...] = nv

    spec = pl.BlockSpec((1,) * lead + (tr, cdim), lambda i: (0,) * lead + (i, 0))
    shp = jax.ShapeDtypeStruct(w.shape, F32)
    return _call(body, name=name, out_shape=(shp,) * 4, grid=(r // tr,),
                 in_specs=[spec, pl.BlockSpec((tr, cdim), lambda i: (i, 0)), spec, spec],
                 out_specs=(spec,) * 4, compiler_params=_params(("parallel",)))(w, g, m, v)


def _head_of_row(r, nc):
    assert nc & (nc - 1) == 0
    return lax.shift_right_logical(r, nc.bit_length() - 1)


def _chunk_mats(rows, nc, reverse):
    ri = lax.broadcasted_iota(jnp.int32, (rows, rows), 0)
    ci = lax.broadcasted_iota(jnp.int32, (rows, rows), 1)
    same = _head_of_row(ri, nc) == _head_of_row(ci, nc)
    between = jnp.where(same & ((ci > ri) if reverse else (ci < ri)), 1.0, 0.0).astype(F32)
    li = lax.broadcasted_iota(jnp.int32, (LANES, LANES), 0)
    lj = lax.broadcasted_iota(jnp.int32, (LANES, LANES), 1)
    within = jnp.where((li >= lj) if reverse else (li <= lj), 1.0, 0.0).astype(F32)
    return between, within


def _dot_hi(a, b):
    return jnp.dot(a, b, preferred_element_type=F32, precision=lax.Precision.HIGHEST)


def _scan_rows(t, nc, reverse):
    between, within = _chunk_mats(t.shape[0], nc, reverse)
    inner = _dot_hi(t, within)
    tot = jnp.sum(t, axis=1, keepdims=True)
    return inner + _dot_hi(between, jnp.broadcast_to(tot, t.shape))


def _log_forget_cumsum(f_rows, bias_rows, nc):
    def body(f_ref, b_ref, cum_ref):
        z = f_ref[...] + b_ref[...]
        lf = jnp.minimum(z, 0.0) - jnp.log(1.0 + jnp.exp(-jnp.abs(z)))
        cum_ref[...] = _scan_rows(lf, nc, False)

    return _call(body, name="forget_cumsum", out_shape=jax.ShapeDtypeStruct(f_rows.shape, F32),
                 compiler_params=_params())(f_rows, bias_rows)


def _log_forget_cumsum_bwd(dcum_rows, f_rows, bias_rows, nc):
    rows = f_rows.shape[0]

    def body(d_ref, f_ref, b_ref, df_ref, db_ref):
        dlf = _scan_rows(d_ref[...], nc, True)
        z = f_ref[...] + b_ref[...]
        df = dlf * _sigmoid(-z)
        df_ref[...] = df
        hi = lax.broadcasted_iota(jnp.int32, (FOX_HEADS, rows), 0)
        ri = lax.broadcasted_iota(jnp.int32, (FOX_HEADS, rows), 1)
        sel = jnp.where(_head_of_row(ri, nc) == hi, 1.0, 0.0).astype(F32)
        db_ref[...] = jnp.sum(_dot_hi(sel, df), axis=1, keepdims=True)

    return _call(body, name="forget_cumsum_bwd",
                 out_shape=(jax.ShapeDtypeStruct(f_rows.shape, F32), jax.ShapeDtypeStruct((FOX_HEADS, 1), F32)),
                 compiler_params=_params())(dcum_rows, f_rows, bias_rows)


def _rms_hat(xv):
    rstd = lax.rsqrt(jnp.mean(xv * xv, axis=-1, keepdims=True) + RMS_EPS)
    return xv * rstd, rstd


def _modulated(x_ref, g_ref, sc_ref, sh_ref):
    xhat, _ = _rms_hat(x_ref[...])
    return ((xhat * g_ref[...]) * sc_ref[...] + sh_ref[...]).astype(BF)


def _forget_logits(x, g_pre, scale1p, shift, w_f, tm):
    s = x.shape[0]

    def body(x_ref, g_ref, sc_ref, sh_ref, w_ref, f_ref):
        f_ref[...] = _dot(_modulated(x_ref, g_ref, sc_ref, sh_ref), w_ref[...])

    vec = _full((1, D_MODEL))
    return _call(
        body, name="forget_logits", out_shape=jax.ShapeDtypeStruct((s, LANES), F32), grid=(s // tm,),
        in_specs=[pl.BlockSpec((tm, D_MODEL), lambda i: (i, 0)), vec, vec, vec, _full((D_MODEL, LANES))],
        out_specs=pl.BlockSpec((tm, LANES), lambda i: (i, 0)), compiler_params=_params(("parallel",)),
    )(x, g_pre, scale1p, shift, w_f)


def _split3(v):
    hi = v.astype(BF).astype(F32)
    mid = (v - hi).astype(BF).astype(F32)
    lo = ((v - hi) - mid).astype(BF).astype(F32)
    return hi, mid, lo


def _in_proj(x, g_pre, scale1p, shift, w_rows, w_t_fox, cum, cos_t, sin_t, tm):
    s = x.shape[0]
    r_va, r_za, r_qb, r_zb, r_kb, r_vb = 0, 512, 1024, 1536, 2048, 2176

    def body(x_ref, g_ref, sc_ref, sh_ref, w_ref, wt_ref, cum_ref, cos_ref, sin_ref,
             h_ref, qat_ref, ka_ref, kat_ref, v_ref, vt_ref, za_ref, zb_ref, qb_ref, kb_ref, vb_ref,
             qbt_ref, kbt_ref, vbt_ref, mo_ref):
        hb = _modulated(x_ref, g_ref, sc_ref, sh_ref)
        h_ref[...] = hb

        def sec(c0, width):
            return _dot(hb, w_ref[:, c0:c0 + width])

        def sec_t(r0):
            return _dot_nt(wt_ref[r0:r0 + FOX_W, :], hb)

        q_t = sec_t(0) * Q_SCALE
        k_t = sec_t(FOX_W)
        v_t = sec_t(2 * FOX_W)
        va = sec(r_va, FOX_W)
        zeros = jnp.zeros((AUG_DIM - HEAD_DIM - AUG_ROWS, tm), F32)
        ri = lax.broadcasted_iota(jnp.int32, (AUG_ROWS, tm), 0)
        const = jnp.where(ri == AUG_ROWS - 1, 0.0, 1.0)
        ri_v = lax.broadcasted_iota(jnp.int32, (VT_ROWS - HEAD_DIM, tm), 0)
        v_feat = jnp.where(ri_v == 0, 1.0, 0.0).astype(BF)
        for hd in range(FOX_HEADS):
            rows = slice(hd * HEAD_DIM, (hd + 1) * HEAD_DIM)
            cum2 = cum_ref[hd:hd + 1, :] * LOG2E
            hi, mid, lo = (jnp.broadcast_to(part, (AUG_ROWS, tm)) for part in _split3(cum2))
            q_feat = jnp.where(ri == 1, hi, jnp.where(ri == 2, mid, jnp.where(ri == 3, lo, const)))
            k_feat = jnp.where(ri == 4, -hi, jnp.where(ri == 5, -mid, jnp.where(ri == 6, -lo, const)))
            q_aug = jnp.concatenate([q_t[rows], q_feat, zeros], axis=0)
            k_aug = jnp.concatenate([k_t[rows], k_feat, zeros], axis=0)
            mo_ref[hd:hd + 1, :] = jnp.sum(q_t[rows] * k_t[rows], axis=0, keepdims=True) + 1.0
            qat_ref[hd] = q_aug.astype(BF)
            kat_ref[hd] = k_aug.astype(BF)
            ka_ref[hd] = k_aug.T.astype(BF)
            vt_ref[hd] = jnp.concatenate([v_t[rows].astype(BF), v_feat], axis=0)
            v_ref[hd] = va[:, rows].astype(BF)
        za_ref[...] = sec(r_za, FOX_W)
        zb_ref[...] = sec(r_zb, SWA_W)
        cos2, sin2 = cos_ref[...], sin_ref[...]
        cos8 = jnp.concatenate([cos2] * 4, axis=1)
        sin8 = jnp.concatenate([sin2] * 4, axis=1)
        qb = sec(r_qb, SWA_W)
        qb = (qb * cos8 + _rope_partner(qb) * sin8) * (HEAD_DIM ** -0.5)
        qb_ref[...] = qb.astype(BF)
        for a in range(SWA_W // LANES):
            qbt_ref[a * LANES:(a + 1) * LANES, :] = qb[:, a * LANES:(a + 1) * LANES].T.astype(BF)
        kb = sec(r_kb, SWA_KV_W)
        kb = kb * cos2 + _rope_partner(kb) * sin2
        vb = sec(r_vb, SWA_KV_W)
        kb_t, vb_t = kb.T, vb.T
        for hd in range(SWA_KV_HEADS):
            sl = slice(hd * HEAD_DIM, (hd + 1) * HEAD_DIM)
            kb_ref[hd] = kb[:, sl].astype(BF)
            vb_ref[hd] = vb[:, sl].astype(BF)
            kbt_ref[hd] = kb_t[sl].astype(BF)
            vbt_ref[hd] = jnp.concatenate([vb_t[sl].astype(BF), v_feat], axis=0)

    row = lambda w: pl.BlockSpec((tm, w), lambda i: (i, 0))
    heads = lambda n, w=HEAD_DIM: pl.BlockSpec((n, tm, w), lambda i: (0, i, 0))
    heads_t = lambda w: pl.BlockSpec((FOX_HEADS, w, tm), lambda i: (0, 0, i))
    vec = _full((1, D_MODEL))
    hs = lambda a, b: jax.ShapeDtypeStruct((FOX_HEADS, a, b), BF)
    out_shape = (
        jax.ShapeDtypeStruct((s, D_MODEL), BF),
        hs(AUG_DIM, s), hs(s, AUG_DIM), hs(AUG_DIM, s), hs(s, HEAD_DIM), hs(VT_ROWS, s),
        jax.ShapeDtypeStruct((s, FOX_W), F32), jax.ShapeDtypeStruct((s, SWA_W), F32),
        jax.ShapeDtypeStruct((s, SWA_W), BF),
        jax.ShapeDtypeStruct((SWA_KV_HEADS, s, HEAD_DIM), BF), jax.ShapeDtypeStruct((SWA_KV_HEADS, s, HEAD_DIM), BF),
        jax.ShapeDtypeStruct((SWA_W, s), BF),
        jax.ShapeDtypeStruct((SWA_KV_HEADS, HEAD_DIM, s), BF), jax.ShapeDtypeStruct((SWA_KV_HEADS, VT_ROWS, s), BF),
        jax.ShapeDtypeStruct((FOX_HEADS, s), F32),
    )
    kv_t = lambda w: pl.BlockSpec((SWA_KV_HEADS, w, tm), lambda i: (0, 0, i))
    return _call(
        body, name="in_proj", out_shape=out_shape, grid=(s // tm,),
        in_specs=[row(D_MODEL), vec, vec, vec, _full(w_rows.shape), _full(w_t_fox.shape),
                  pl.BlockSpec((FOX_HEADS, tm), lambda i: (0, i)), row(LANES), row(LANES)],
        out_specs=(row(D_MODEL), heads_t(AUG_DIM), heads(FOX_HEADS, AUG_DIM), heads_t(AUG_DIM), heads(FOX_HEADS),
                   heads_t(VT_ROWS), row(FOX_W), row(SWA_W), row(SWA_W), heads(SWA_KV_HEADS), heads(SWA_KV_HEADS),
                   pl.BlockSpec((SWA_W, tm), lambda i: (0, i)), kv_t(HEAD_DIM), kv_t(VT_ROWS),
                   pl.BlockSpec((FOX_HEADS, tm), lambda i: (0, i))),
        compiler_params=_params(("parallel",)),
    )(x, g_pre, scale1p, shift, w_rows, w_t_fox, cum, cos_t, sin_t)


def _diag_chunks(d, bq, bk, chunk):
    out = []
    for c0 in range(0, bq, chunk):
        if d is None or d * bk + bk - 1 <= c0:
            out.append((c0, None, bk))
        elif d * bk <= c0 + chunk - 1:
            n_keys = min(bk, c0 + chunk - d * bk)
            kpos = d * bk + lax.broadcasted_iota(jnp.int32, (n_keys, chunk), 0)
            qpos = c0 + lax.broadcasted_iota(jnp.int32, (n_keys, chunk), 1)
            out.append((c0, kpos <= qpos, n_keys))
    return out


def _fox_fwd(qat, ka, vt, m_own, bq, bk, chunk, running_max):
    nh, _, s = qat.shape
    r = bq // bk

    pairs = [(i, j) for i in range(s // bq) for j in range(i * r + r)]

    def body(i_tab, j_tab, ka_ref, qat_ref, vt_ref, mo_ref, o_ref, lse_ref, bad_ref, *rest):
        pt_ref, m_scr, acc_scr = (None,) * running_max + rest
        i, j = i_tab[pl.program_id(1)], j_tab[pl.program_id(1)]

        @pl.when(j == 0)
        def _():
            m_scr[...] = jnp.full(m_scr.shape, NEG, F32) if running_max else mo_ref[0]
            acc_scr[...] = jnp.zeros(acc_scr.shape, F32)

        def careful(d):
            kv, vtv = ka_ref[0], vt_ref[0]

            def one_chunk(n, carry):
                c0 = pl.multiple_of(n * chunk, chunk)
                cs = pl.ds(c0, chunk)
                sc = _dot(kv, qat_ref[0, :, cs])
                if d is not None:
                    kpos = d * bk + lax.broadcasted_iota(jnp.int32, (bk, chunk), 0)
                    qpos = c0 + lax.broadcasted_iota(jnp.int32, (bk, chunk), 1)
                    sc = jnp.where(kpos <= qpos, sc, NEG)
                m_prev = m_scr[:, cs]
                m_new = jnp.maximum(m_prev, jnp.max(sc, axis=0, keepdims=True))
                p = jnp.exp2(sc - m_new).astype(BF)
                acc_scr[:, cs] = jnp.exp2(m_prev - m_new) * acc_scr[:, cs] + _dot(vtv, p)
                m_scr[:, cs] = m_new
                return carry

            lax.fori_loop(0, bq // chunk, one_chunk, 0)

        def fast(d):
            todo = _diag_chunks(d, bq, bk, chunk)
            scores = lambda t: _dot(ka_ref[0, :t[2], :], qat_ref[0, :, t[0]:t[0] + chunk])
            sc_next = scores(todo[0])
            for n, (c0, mask, n_keys) in enumerate(todo):
                cs = slice(c0, c0 + chunk)
                sc = sc_next
                if n + 1 < len(todo):
                    sc_next = scores(todo[n + 1])
                if mask is not None:
                    sc = jnp.where(mask, sc, NEG)
                p = jnp.exp2(sc - m_scr[:, cs]).astype(BF)
                pt_ref[0, :n_keys, cs] = p
                acc_scr[:, cs] += _dot(vt_ref[0, :, :n_keys], p)

        step = careful if running_max else fast

        @pl.when(j < i * r)
        def _():
            step(None)

        for d in range(r):
            @pl.when(j == i * r + d)
            def _(d=d):
                step(d)

        @pl.when(j == i * r + r - 1)
        def _():
            l = acc_scr[HEAD_DIM:HEAD_DIM + 1, :]
            o_ref[0] = acc_scr[:HEAD_DIM, :] / l
            lse_ref[0] = m_scr[...] + jnp.log2(l)
            bad_ref[0] = jnp.where(l < OVERFLOW_GUARD, 0.0, 1.0)

    qmap_t = lambda h, t, it, jt: (h, 0, it[t])
    qrow = pl.BlockSpec((1, 1, bq), qmap_t)
    row_shape = jax.ShapeDtypeStruct((nh, 1, s), F32)
    out_shape = (jax.ShapeDtypeStruct((nh, HEAD_DIM, s), F32), row_shape, row_shape)
    out_specs = (pl.BlockSpec((1, HEAD_DIM, bq), qmap_t), qrow, qrow)
    if not running_max:
        out_shape += (jax.ShapeDtypeStruct((nh, s, s), BF),)
        out_specs += (pl.BlockSpec((1, bk, bq), lambda h, t, it, jt: (h, jt[t], it[t])),)
    grid_spec = pltpu.PrefetchScalarGridSpec(
        num_scalar_prefetch=2, grid=(nh, len(pairs)),
        in_specs=[pl.BlockSpec((1, bk, AUG_DIM), lambda h, t, it, jt: (h, jt[t], 0)), pl.BlockSpec((1, AUG_DIM, bq), qmap_t),
                  pl.BlockSpec((1, VT_ROWS, bk), lambda h, t, it, jt: (h, 0, jt[t])), qrow],
        out_specs=out_specs,
        scratch_shapes=[pltpu.VMEM((1, bq), F32), pltpu.VMEM((VT_ROWS, bq), F32)])
    return _call(
        body, name="fox_fwd_running_max" if running_max else "fox_fwd", out_shape=out_shape, grid_spec=grid_spec,
        compiler_params=_params(("parallel", "arbitrary")),
    )(jnp.asarray([p[0] for p in pairs], jnp.int32), jnp.asarray([p[1] for p in pairs], jnp.int32), ka, qat, vt, m_own)


def _fox_bwd(qat, ka, kat, v, dot_, lse, delta, pt, bq, bk, chunk, dq_blk):
    nh, _, s = qat.shape
    r = bq // bk
    nq = s // bq
    stored = pt is not None

    pairs = [(j, i) for j in range(s // bk) for i in range(j // r, nq)]

    def body(j_tab, i_tab, a_ref, b_ref, kat_ref, v_ref, qat_ref, do_ref, dl_ref, dq_ref, dk_ref, dv_ref, dk_scr, dv_scr):
        ka_ref, lse_ref, pt_ref = (None, None, a_ref) if stored else (a_ref, b_ref, None)
        j, i = j_tab[pl.program_id(1)], i_tab[pl.program_id(1)]

        @pl.when(pl.program_id(1) == 0)
        def _():
            dq_ref[...] = jnp.zeros(dq_ref.shape, F32)

        @pl.when(i * r <= j)
        def _():
            dk_scr[...] = jnp.zeros(dk_scr.shape, F32)
            dv_scr[...] = jnp.zeros(dv_scr.shape, F32)

        def step(d):
            todo = _diag_chunks(d, bq, bk, chunk)

            def products(t):
                cs = slice(t[0], t[0] + chunk)
                return (None if stored else _dot(ka_ref[0, :t[2], :], qat_ref[0, :, cs]),
                        _dot(v_ref[0, :t[2], :], do_ref[0, :, cs]))

            nxt = products(todo[0])
            for n, (c0, mask, n_keys) in enumerate(todo):
                cs = slice(c0, c0 + chunk)
                sc, dp = nxt
                if n + 1 < len(todo):
                    nxt = products(todo[n + 1])
                if stored:
                    p_bf = pt_ref[0, :n_keys, cs]
                    p = p_bf.astype(F32)
                else:
                    p = jnp.exp2(sc - lse_ref[0, :, cs])
                    if mask is not None:
                        p = jnp.where(mask, p, 0.0)
                    p_bf = p.astype(BF)
                ds = (p * (dp - dl_ref[0, :, cs])).astype(BF)
                dv_scr[:, :n_keys] += _dot_nt(do_ref[0, :, cs], p_bf)
                dk_scr[:, :n_keys] += _dot_nt(qat_ref[0, :VT_ROWS, cs], ds)
                c1 = c0 % dq_blk
                dq_ref[0, i * (bq // dq_blk) + c0 // dq_blk, :, c1:c1 + chunk] += _dot(kat_ref[0, :VT_ROWS, :n_keys], ds)

        @pl.when(i * r > j)
        def _():
            step(None)

        for d in range(r):
            @pl.when(j == i * r + d)
            def _(d=d):
                step(d)

        @pl.when(i == nq - 1)
        def _():
            dk_ref[0] = dk_scr[...]
            dv_ref[0] = dv_scr[...]

    qmap = lambda h, t, jt, it: (h, 0, it[t])
    kmap = lambda h, t, jt, it: (h, jt[t], 0)
    kmap_t = lambda h, t, jt, it: (h, 0, jt[t])
    if stored:
        first = [(pt, pl.BlockSpec((1, bk, bq), lambda h, t, jt, it: (h, jt[t], it[t]))),
                 (delta, pl.BlockSpec((1, 1, bq), qmap))]
    else:
        first = [(ka, pl.BlockSpec((1, bk, AUG_DIM), kmap)), (lse, pl.BlockSpec((1, 1, bq), qmap))]
    grid_spec = pltpu.PrefetchScalarGridSpec(
        num_scalar_prefetch=2, grid=(nh, len(pairs)),
        in_specs=[first[0][1], first[1][1], pl.BlockSpec((1, AUG_DIM, bk), kmap_t), pl.BlockSpec((1, bk, HEAD_DIM), kmap),
                  pl.BlockSpec((1, AUG_DIM, bq), qmap), pl.BlockSpec((1, HEAD_DIM, bq), qmap),
                  pl.BlockSpec((1, 1, bq), qmap)],
        out_specs=(pl.BlockSpec((1, s // dq_blk, VT_ROWS, dq_blk), lambda h, t, jt, it: (h, 0, 0, 0)),
                   pl.BlockSpec((1, VT_ROWS, bk), kmap_t), pl.BlockSpec((1, HEAD_DIM, bk), kmap_t)),
        scratch_shapes=[pltpu.VMEM((VT_ROWS, bk), F32), pltpu.VMEM((HEAD_DIM, bk), F32)])
    return _call(
        body, name="fox_bwd" if stored else "fox_bwd_recompute",
        out_shape=(jax.ShapeDtypeStruct((nh, s // dq_blk, VT_ROWS, dq_blk), F32),
                   jax.ShapeDtypeStruct((nh, VT_ROWS, s), F32), jax.ShapeDtypeStruct((nh, HEAD_DIM, s), F32)),
        grid_spec=grid_spec, compiler_params=_params(("parallel", "arbitrary")),
    )(jnp.asarray([p[0] for p in pairs], jnp.int32), jnp.asarray([p[1] for p in pairs], jnp.int32),
      first[0][0], first[1][0], kat, v, qat, dot_, delta)


def _swa_mask(i, tq):
    kpos = i * tq - WINDOW + lax.broadcasted_iota(jnp.int32, (tq + WINDOW, tq), 0)
    qpos = i * tq + lax.broadcasted_iota(jnp.int32, (tq + WINDOW, tq), 1)
    rel = qpos - kpos
    return (rel >= 0) & (rel < WINDOW) & (kpos >= 0)


def _swa_rows(ref, i, tq):
    before = pl.multiple_of(jnp.maximum(i * tq - WINDOW, 0), WINDOW)
    return jnp.concatenate([ref[0, pl.ds(before, WINDOW), :], ref[0, pl.ds(pl.multiple_of(i * tq, tq), tq), :]], axis=0)


def _swa_before(n_rows, tq):
    return pl.BlockSpec((1, n_rows, WINDOW), lambda g, i: (g, 0, jnp.maximum(i * (tq // WINDOW) - 1, 0)))


def _swa_probs_t(sc, mask, sink):
    sc = jnp.where(mask, sc, NEG)
    m = jnp.maximum(jnp.max(sc, axis=0, keepdims=True), sink)
    p = jnp.exp(sc - m)
    e_sink = jnp.exp(sink - m)
    inv_l = 1.0 / (jnp.sum(p, axis=0, keepdims=True) + e_sink)
    return p * inv_l, e_sink * inv_l


def _swa_fwd(qbt, kb, vbt, sinks, tq):
    s = qbt.shape[1]
    gw = SWA_GROUP * HEAD_DIM

    def body(q_ref, k_ref, vb_ref, vc_ref, s_ref, o_ref):
        i = pl.program_id(1)
        mask = _swa_mask(i, tq)
        kw = _swa_rows(k_ref, i, tq)
        vtw = jnp.concatenate([vb_ref[0], vc_ref[0]], axis=1)
        sk = s_ref[0]
        scores = lambda hh: _dot(kw, q_ref[hh * HEAD_DIM:(hh + 1) * HEAD_DIM, :])
        sc_next = scores(0)
        for hh in range(SWA_GROUP):
            rows = slice(hh * HEAD_DIM, (hh + 1) * HEAD_DIM)
            sink = sk[:, hh:hh + 1]
            sc = jnp.where(mask, sc_next, NEG)
            if hh + 1 < SWA_GROUP:
                sc_next = scores(hh + 1)
            m = jnp.maximum(jnp.max(sc, axis=0, keepdims=True), sink)
            acc = _dot(vtw, jnp.exp(sc - m).astype(BF))
            o_ref[rows, :] = acc[:HEAD_DIM] / (acc[HEAD_DIM:HEAD_DIM + 1] + jnp.exp(sink - m))

    kvspec = pl.BlockSpec((1, s, HEAD_DIM), lambda g, i: (g, 0, 0))
    qspec = pl.BlockSpec((gw, tq), lambda g, i: (g, i))
    return _call(
        body, name="swa_fwd", out_shape=jax.ShapeDtypeStruct((SWA_W, s), F32), grid=(SWA_KV_HEADS, s // tq),
        in_specs=[qspec, kvspec, _swa_before(VT_ROWS, tq), pl.BlockSpec((1, VT_ROWS, tq), lambda g, i: (g, 0, i)),
                  pl.BlockSpec((1, 1, SWA_GROUP), lambda g, i: (g, 0, 0))],
        out_specs=qspec, compiler_params=_params(("parallel", "parallel")),
    )(qbt, kb, vbt, vbt, sinks)


def _swa_bwd(qb, qbt, kb, kbt, vb, sinks, dob, dobt, tq):
    s = qb.shape[0]
    gw = SWA_GROUP * HEAD_DIM

    def body(q_ref, qt_ref, k_ref, ktb_ref, ktc_ref, v_ref, s_ref, do_ref, dot_ref, dq_ref, dk_ref, dv_ref, ds_ref):
        i = pl.program_id(1)

        @pl.when(i == 0)
        def _():
            dk_ref[...] = jnp.zeros(dk_ref.shape, F32)
            dv_ref[...] = jnp.zeros(dv_ref.shape, F32)
            ds_ref[...] = jnp.zeros(ds_ref.shape, F32)

        mask = _swa_mask(i, tq)
        kw = _swa_rows(k_ref, i, tq)
        vw = _swa_rows(v_ref, i, tq)
        ktw = jnp.concatenate([ktb_ref[0], ktc_ref[0]], axis=1)
        qv, dov = q_ref[...], do_ref[...]
        sk = s_ref[0]
        dsinks = []
        dk_acc = jnp.zeros((tq + WINDOW, HEAD_DIM), F32)
        dv_acc = jnp.zeros((tq + WINDOW, HEAD_DIM), F32)
        def products(hh):
            rows = slice(hh * HEAD_DIM, (hh + 1) * HEAD_DIM)
            return _dot(kw, qt_ref[rows, :]), _dot(vw, dot_ref[rows, :])

        nxt = products(0)
        for hh in range(SWA_GROUP):
            rows = slice(hh * HEAD_DIM, (hh + 1) * HEAD_DIM)
            sc, dp = nxt
            if hh + 1 < SWA_GROUP:
                nxt = products(hh + 1)
            p, p_sink = _swa_probs_t(sc, mask, sk[:, hh:hh + 1])
            delta = jnp.sum(p * dp, axis=0, keepdims=True)
            dsc = (p * (dp - delta)).astype(BF)
            dq_ref[rows, :] = _dot(ktw, dsc)
            dk_acc = dk_acc + _dot(dsc, qv[:, rows])
            dv_acc = dv_acc + _dot(p.astype(BF), dov[:, rows])
            dsinks.append(-jnp.sum(p_sink * delta, axis=1, keepdims=True))
        before = pl.ds(pl.multiple_of(jnp.maximum(i * tq - WINDOW, 0), WINDOW), WINDOW)
        own = pl.ds(pl.multiple_of(i * tq, tq), tq)
        dk_ref[0, before, :] += dk_acc[:WINDOW]
        dk_ref[0, own, :] += dk_acc[WINDOW:]
        dv_ref[0, before, :] += dv_acc[:WINDOW]
        dv_ref[0, own, :] += dv_acc[WINDOW:]
        ds_ref[0] += jnp.concatenate(dsinks, axis=1)

    kvspec = pl.BlockSpec((1, s, HEAD_DIM), lambda g, i: (g, 0, 0))
    qspec = pl.BlockSpec((tq, gw), lambda g, i: (i, g))
    qspec_t = pl.BlockSpec((gw, tq), lambda g, i: (g, i))
    sspec = pl.BlockSpec((1, 1, SWA_GROUP), lambda g, i: (g, 0, 0))
    kvshape = jax.ShapeDtypeStruct((SWA_KV_HEADS, s, HEAD_DIM), F32)
    return _call(
        body, name="swa_bwd",
        out_shape=(jax.ShapeDtypeStruct((SWA_W, s), F32), kvshape, kvshape,
                   jax.ShapeDtypeStruct((SWA_KV_HEADS, 1, SWA_GROUP), F32)),
        grid=(SWA_KV_HEADS, s // tq),
        in_specs=[qspec, qspec_t, kvspec, _swa_before(HEAD_DIM, tq),
                  pl.BlockSpec((1, HEAD_DIM, tq), lambda g, i: (g, 0, i)), kvspec, sspec, qspec, qspec_t],
        out_specs=(qspec_t, kvspec, kvspec, sspec),
        compiler_params=_params(("parallel", "arbitrary")),
    )(qb, qbt, kb, kbt, kbt, vb, sinks, dob, dobt)


def _pairs_to_rows(ref, n_rows=HEAD_DIM):
    parts = []
    for a in range(0, FOX_HEADS, 2):
        parts.append(jnp.concatenate([ref[a][:n_rows], ref[a + 1][:n_rows]], axis=0).T)
    return jnp.concatenate(parts, axis=1)


def _blocks_to_rows(ref):
    return jnp.concatenate([ref[a:a + LANES, :].T for a in range(0, ref.shape[0], LANES)], axis=1)


def _out_proj(oat, za, obt, zb, x, tgt, w_out, w_out_t, gate, g_post, inv_l, tm):
    s = x.shape[0]

    def body(oat_ref, za_ref, obt_ref, zb_ref, x_ref, t_ref, w_ref, wt_ref, gate_ref, gp_ref, il_ref,
             dout_ref, doat_ref, dla_ref, dza_ref, dob_ref, dobt_ref, dzb_ref, gw_ref, dgate_ref, dgp_ref, loss_ref):
        i = pl.program_id(0)

        @pl.when(i == 0)
        def _():
            gw_ref[...] = jnp.zeros(gw_ref.shape, F32)
            dgate_ref[...] = jnp.zeros(dgate_ref.shape, F32)
            dgp_ref[...] = jnp.zeros(dgp_ref.shape, F32)
            loss_ref[...] = jnp.zeros(loss_ref.shape, F32)

        oa_v = _pairs_to_rows(oat_ref)
        ob_v = _blocks_to_rows(obt_ref)
        za_v, zb_v = za_ref[...], zb_ref[...]
        sga, sgb = _sigmoid(za_v), _sigmoid(zb_v)
        sila, silb = za_v * sga, zb_v * sgb
        u = jnp.concatenate([oa_v * sila, ob_v * silb], axis=1).astype(BF)
        yv = _dot(u, w_ref[...])
        yhat, rstd = _rms_hat(yv)
        gp, gate_v = gp_ref[...], gate_ref[...]
        nrm = yhat * gp
        diff = (x_ref[...] + gate_v * nrm) - t_ref[...]
        loss_ref[...] += 0.5 * jnp.sum(jnp.sum(diff * diff, axis=1, keepdims=True), axis=0, keepdims=True) / D_MODEL
        dout = diff * (1.0 / D_MODEL)
        dout_ref[...] = dout
        dgate_ref[...] += jnp.sum(dout * nrm, axis=0, keepdims=True)
        dn = dout * gate_v
        dgp_ref[...] += jnp.sum(dn * yhat, axis=0, keepdims=True)
        dyhat = dn * gp
        dy = (rstd * (dyhat - yhat * jnp.mean(dyhat * yhat, axis=1, keepdims=True))).astype(BF)
        gw_ref[...] += _dot_tn(u, dy)
        du = _dot(dy, wt_ref[...])
        dua, dub = du[:, :FOX_W], du[:, FOX_W:]
        doa = dua * sila
        for a in range(0, FOX_HEADS, 2):
            pair_t = doa[:, a * HEAD_DIM:(a + 2) * HEAD_DIM].T
            for hd, rows in ((a, slice(0, HEAD_DIM)), (a + 1, slice(HEAD_DIM, 2 * HEAD_DIM))):
                inv_l = il_ref[hd]
                doat_ref[hd] = (pair_t[rows] * inv_l).astype(BF)
                dla_ref[hd] = jnp.sum(pair_t[rows] * oat_ref[hd], axis=0, keepdims=True) * inv_l
        dob = dub * silb
        dob_ref[...] = dob.astype(BF)
        for a in range(0, SWA_W, LANES):
            dobt_ref[a:a + LANES, :] = dob[:, a:a + LANES].T.astype(BF)
        dza_ref[...] = (dua * oa_v * (sga * (1.0 + za_v * (1.0 - sga)))).astype(BF)
        dzb_ref[...] = (dub * ob_v * (sgb * (1.0 + zb_v * (1.0 - sgb)))).astype(BF)

    row = lambda w: pl.BlockSpec((tm, w), lambda i: (i, 0))
    heads_t = lambda w: pl.BlockSpec((FOX_HEADS, w, tm), lambda i: (0, 0, i))
    vec = _full((1, D_MODEL))
    mat = _full((D_MODEL, D_MODEL))
    out_shape = (
        jax.ShapeDtypeStruct((s, D_MODEL), F32),
        jax.ShapeDtypeStruct((FOX_HEADS, HEAD_DIM, s), BF), jax.ShapeDtypeStruct((FOX_HEADS, 1, s), F32),
        jax.ShapeDtypeStruct((s, FOX_W), BF), jax.ShapeDtypeStruct((s, SWA_W), BF), jax.ShapeDtypeStruct((SWA_W, s), BF),
        jax.ShapeDtypeStruct((s, SWA_W), BF),
        jax.ShapeDtypeStruct((D_MODEL, D_MODEL), F32),
        jax.ShapeDtypeStruct((1, D_MODEL), F32), jax.ShapeDtypeStruct((1, D_MODEL), F32),
        jax.ShapeDtypeStruct((1, 1), F32),
    )
    col = pl.BlockSpec((SWA_W, tm), lambda i: (0, i))
    return _call(
        body, name="out_proj", out_shape=out_shape, grid=(s // tm,),
        in_specs=[heads_t(HEAD_DIM), row(FOX_W), col, row(SWA_W), row(D_MODEL), row(D_MODEL), mat, mat, vec, vec,
                  heads_t(1)],
        out_specs=(row(D_MODEL), heads_t(HEAD_DIM), heads_t(1), row(FOX_W), row(SWA_W), col, row(SWA_W), mat, vec, vec,
                   _full((1, 1))),
        compiler_params=_params(("arbitrary",)),
    )(oat, za, obt, zb, x, tgt, w_out, w_out_t, gate, g_post, inv_l)


def _assemble_dproj(dqt, dkt, dvt, dza, dqb, dzb, dkb, dvb, df, cos_t, sin_t, tm):
    s = dza.shape[0]

    def body(dqt_ref, dkt_ref, dvt_ref, dza_ref, dqb_ref, dzb_ref, dkb_ref, dvb_ref, df_ref, cos_ref, sin_ref, o_ref):
        def cat(ref, n):
            return jnp.concatenate([ref[hd] for hd in range(n)], axis=1)

        cos2, sin2 = cos_ref[...], sin_ref[...]
        cos8 = jnp.concatenate([cos2] * 4, axis=1)
        sin8 = jnp.concatenate([sin2] * 4, axis=1)
        scale = HEAD_DIM ** -0.5
        o_ref[:, C_QA:C_QA + FOX_W] = (_pairs_to_rows(dqt_ref.at[:, 0]) * scale).astype(BF)
        o_ref[:, C_KA:C_KA + FOX_W] = (_pairs_to_rows(dkt_ref) * LN2).astype(BF)
        o_ref[:, C_VA:C_VA + FOX_W] = _pairs_to_rows(dvt_ref).astype(BF)
        o_ref[:, C_ZA:C_ZA + FOX_W] = dza_ref[...]
        dq = _blocks_to_rows(dqb_ref) * scale
        o_ref[:, C_QB:C_QB + SWA_W] = (dq * cos8 - _rope_partner(dq) * sin8).astype(BF)
        o_ref[:, C_ZB:C_ZB + SWA_W] = dzb_ref[...]
        dk = cat(dkb_ref, SWA_KV_HEADS)
        o_ref[:, C_KB:C_KB + SWA_KV_W] = (dk * cos2 - _rope_partner(dk) * sin2).astype(BF)
        o_ref[:, C_VB:C_VB + SWA_KV_W] = cat(dvb_ref, SWA_KV_HEADS).astype(BF)
        o_ref[:, C_F:C_F + LANES] = df_ref[...].astype(BF)

    row = lambda w: pl.BlockSpec((tm, w), lambda i: (i, 0))
    heads = lambda n: pl.BlockSpec((n, tm, HEAD_DIM), lambda i: (0, i, 0))
    heads_t = lambda w: pl.BlockSpec((FOX_HEADS, w, tm), lambda i: (0, 0, i))
    return _call(
        body, name="assemble_dproj", out_shape=jax.ShapeDtypeStruct((s, WP), BF), grid=(s // tm,),
        in_specs=[pl.BlockSpec((FOX_HEADS, 1, VT_ROWS, tm), lambda i: (0, i, 0, 0)), heads_t(VT_ROWS), heads_t(HEAD_DIM),
                  row(FOX_W), pl.BlockSpec((SWA_W, tm), lambda i: (0, i)), row(SWA_W), heads(SWA_KV_HEADS),
                  heads(SWA_KV_HEADS), row(LANES), row(LANES), row(LANES)],
        out_specs=row(WP), compiler_params=_params(("parallel",)),
    )(dqt, dkt, dvt, dza, dqb, dzb, dkb, dvb, df, cos_t, sin_t)


def _in_proj_bwd_x(dproj, w_al_t, x, dout, g_pre, scale1p, tm, parts):
    s = x.shape[0]
    n_steps = s // tm
    masks = [(1, 0), (0, 1), (1, 1)]

    def body(dp_ref, wt_ref, x_ref, dout_ref, g_ref, sc_ref, parts_ref, gx_ref, dsh_ref, dsc_ref, dg_ref, got_ref,
             send_sems, recv_sems, local_sem):
        i = pl.program_id(0)
        cx, cy, cc = lax.axis_index("x"), lax.axis_index("y"), lax.axis_index("c")
        me = 2 * cx + cy
        own = pltpu.make_async_copy(parts_ref.at[me], got_ref.at[me], local_sem)

        def copy(k, send):
            dx, dy = masks[k]
            peer = 2 * (cx ^ dx) + (cy ^ dy)
            return pltpu.make_async_remote_copy(
                src_ref=parts_ref.at[peer if send else me], dst_ref=got_ref.at[me if send else peer],
                send_sem=send_sems.at[k], recv_sem=recv_sems.at[k], device_id=(cx ^ dx, cy ^ dy, cc), device_id_type=MESH)

        @pl.when(i == 0)
        def _():
            dsh_ref[...] = jnp.zeros(dsh_ref.shape, F32)
            dsc_ref[...] = jnp.zeros(dsc_ref.shape, F32)
            dg_ref[...] = jnp.zeros(dg_ref.shape, F32)
            own.start()
            for k in range(len(masks)):
                copy(k, True).start()

        @pl.when(i == n_steps - 1)
        def _():
            for k in range(len(masks)):
                copy(k, False).wait_recv()
            for k in range(len(masks)):
                copy(k, True).wait_send()
            own.wait()

        dh = _dot(dp_ref[...], wt_ref[...])
        xhat, rstd = _rms_hat(x_ref[...])
        g, sc = g_ref[...], sc_ref[...]
        dsh_ref[...] += jnp.sum(dh, axis=0, keepdims=True)
        dhx = dh * xhat
        dsc_ref[...] += jnp.sum(dhx * g, axis=0, keepdims=True)
        dg_ref[...] += jnp.sum(dhx * sc, axis=0, keepdims=True)
        dxhat = dh * (g * sc)
        gx_ref[...] = dout_ref[...] + rstd * (dxhat - xhat * jnp.mean(dxhat * xhat, axis=1, keepdims=True))

    row = lambda w: pl.BlockSpec((tm, w), lambda i: (i, 0))
    vec = _full((1, D_MODEL))
    vshape = jax.ShapeDtypeStruct((1, D_MODEL), F32)
    hbm = pl.BlockSpec(memory_space=pl.ANY)
    return _call(
        body, name="in_proj_bwd_x",
        out_shape=(jax.ShapeDtypeStruct((s, D_MODEL), F32), vshape, vshape, vshape,
                   jax.ShapeDtypeStruct(parts.shape, parts.dtype)),
        grid=(n_steps,),
        in_specs=[row(WP), _full((WP, D_MODEL)), row(D_MODEL), row(D_MODEL), vec, vec, hbm],
        out_specs=(row(D_MODEL), vec, vec, vec, hbm),
        scratch_shapes=[pltpu.SemaphoreType.DMA((3,)), pltpu.SemaphoreType.DMA((3,)), pltpu.SemaphoreType.DMA],
        compiler_params=_params(("arbitrary",), has_side_effects=True),
    )(dproj, w_al_t, x, dout, g_pre, scale1p, parts)


def _in_proj_bwd_w(h, dproj, tk, tn):
    s = h.shape[0]

    def body(h_ref, dp_ref, gw_ref):
        @pl.when(pl.program_id(1) == 0)
        def _():
            gw_ref[...] = jnp.zeros(gw_ref.shape, F32)

        gw_ref[...] += _dot_tn(h_ref[...], dp_ref[...])

    return _call(
        body, name="in_proj_bwd_w", out_shape=jax.ShapeDtypeStruct((D_MODEL, WP), F32), grid=(WP // tn, s // tk),
        in_specs=[pl.BlockSpec((tk, D_MODEL), lambda n, k: (k, 0)), pl.BlockSpec((tk, tn), lambda n, k: (k, n))],
        out_specs=pl.BlockSpec((D_MODEL, tn), lambda n, k: (0, n)),
        compiler_params=_params(("parallel", "arbitrary")),
    )(h, dproj)


def _align_w_in(w_cols):
    def part(name, width):
        return w_cols[:, _SRC[name]:_SRC[name] + width]

    fpad = jnp.pad(part("fa", FOX_HEADS), ((0, 0), (0, LANES - FOX_HEADS)))
    return jnp.concatenate([part("qa", FOX_W), part("ka", FOX_W), part("va", FOX_W), part("za", FOX_W),
                            part("qb", SWA_W), part("zb", SWA_W), part("kb", SWA_KV_W), part("vb", SWA_KV_W), fpad], axis=1)


def _unalign_w_in(g_al):
    def part(c0, width):
        return g_al[:, c0:c0 + width]

    return jnp.concatenate([part(C_QA, FOX_W), part(C_KA, FOX_W), part(C_VA, FOX_W), part(C_F, FOX_HEADS),
                            part(C_ZA, FOX_W), part(C_QB, SWA_W), part(C_KB, SWA_KV_W), part(C_VB, SWA_KV_W),
                            part(C_ZB, SWA_W)], axis=1)


def _rope_tables(positions):
    inv_freq = ROPE_THETA ** (-jnp.arange(HALF, dtype=F32) / HALF)
    ang = positions.astype(F32)[:, None] * inv_freq
    cos, sin = jnp.cos(ang), jnp.sin(ang)
    return jnp.concatenate([cos, cos, cos, cos], axis=1), jnp.concatenate([-sin, sin, -sin, sin], axis=1)


def _tiles(s):
    if s >= 4096:
        return dict(tm=512, blk=512, bq=2048, bk=2048, bk_bwd=2048, chunk=256, tq=256, tm_out=512, tk=1024, tn=1152)
    return dict(tm=128, blk=128, bq=256, bk=256, bk_bwd=256, chunk=128, tq=128, tm_out=128, tk=128, tn=1152)


def kernel(x, c, positions, w_ada, b_ada, g_pre, w_in, b_fgate, sinks, w_out, g_post, loss_target, m_w_ada, m_b_ada, m_g_pre, m_w_in, m_b_fgate, m_sinks, m_w_out, m_g_post, v_w_ada, v_b_ada, v_g_pre, v_w_in, v_b_fgate, v_sinks, v_w_out, v_g_post):
    s = x.shape[1]
    t = _tiles(s)
    nc = s // LANES
    rows = FOX_HEADS * nc
    me = 4 * lax.axis_index("x") + 2 * lax.axis_index("y") + lax.axis_index("c")
    chip = 2 * lax.axis_index("x") + lax.axis_index("y")
    core = lax.axis_index("c")
    x2, tgt = x[0], loss_target[0]

    c_all = _allgather_devices(c, "gather_c")[:, 0, :]
    a_all, mod_shard = _ada_shard(c_all, w_ada[0])
    mod_all = _allgather_devices(mod_shard, "gather_mod")
    mod_rows = lax.dynamic_index_in_dim(mod_all, me, axis=1, keepdims=False)
    mod = mod_rows.reshape(N_CHIPS, 2, W_ADA_SHARD)[:, 0, :].reshape(1, 3 * D_MODEL) + b_ada
    shift, scale1p, gate = mod[:, :D_MODEL], 1.0 + mod[:, D_MODEL:2 * D_MODEL], mod[:, 2 * D_MODEL:]

    w_in_pad = jnp.pad(w_in[0].astype(BF), ((0, 0), (0, W_IN_SHARD_PAD - W_IN_SHARD)))
    w_pack = jnp.concatenate([w_in_pad, w_out[0].astype(BF).reshape(D_MODEL, W_OUT_SHARD)], axis=1)
    w_all = _allgather_chips(w_pack.reshape(2, D_MODEL // 2, -1), "gather_weights").reshape(N_CHIPS, D_MODEL, -1)
    w_cols = jnp.concatenate([w_all[k, :, :W_IN_SHARD] for k in range(N_CHIPS)], axis=1)
    w_al = _align_w_in(w_cols)
    w_al_t = w_al.T
    w_out_all = w_all[:, :, W_IN_SHARD_PAD:].reshape(D_MODEL, D_MODEL)
    w_out_t = w_out_all.T

    cos_t, sin_t = _rope_tables(positions[0])

    f_pad = _forget_logits(x2, g_pre, scale1p, shift, w_al[:, C_F:], t["tm"])
    f_rows = f_pad[:, :FOX_HEADS].T.reshape(rows, LANES)
    bias_rows = jnp.repeat(b_fgate[0], nc)[:, None]
    cum = _log_forget_cumsum(f_rows, bias_rows, nc).reshape(FOX_HEADS, s)
    h, qat, ka, kat, va, vat, za, zb, qb, kb, vb, qbt, kbt, vbt, m_own = _in_proj(
        x2, g_pre, scale1p, shift, w_al[:, C_VA:C_F], w_al_t[:C_ZA], cum, cos_t, sin_t, t["tm"])
    m_own = m_own[:, None, :]
    fox_args = (qat, ka, vat, m_own, t["bq"], t["bk"], t["chunk"])
    oat, lse, bad, pt = _fox_fwd(*fox_args, running_max=False)
    overflowed = jnp.max(bad) > 0.0
    oat, lse = lax.cond(overflowed, lambda: _fox_fwd(*fox_args, running_max=True)[:2], lambda: (oat, lse))
    inv_l = jnp.where(overflowed, 1.0, jnp.exp2(m_own - lse))
    sinks_g = sinks.reshape(SWA_KV_HEADS, 1, SWA_GROUP)
    obt = _swa_fwd(qbt, kb, vbt, sinks_g, t["tq"])

    dout, doat, delta_a, dza, dob, dobt, dzb, gw_out, dgate, dg_post, loss_part = _out_proj(
        oat, za, obt, zb, x2, tgt, w_out_all, w_out_t, gate, g_post, inv_l, t["tm_out"])

    bwd_args = (qat, ka, kat, va, doat, lse, delta_a)
    bwd_tiles = (t["bq"], t["bk_bwd"], t["chunk"], t["blk"])
    dqt, dkt, dvt = lax.cond(overflowed, lambda: _fox_bwd(*bwd_args, None, *bwd_tiles),
                             lambda: _fox_bwd(*bwd_args, pt, *bwd_tiles))
    dcum = dqt[:, :, HEAD_DIM, :].reshape(FOX_HEADS, s) - dkt[:, HEAD_DIM, :]
    df_rows, db_heads = _log_forget_cumsum_bwd(dcum.reshape(rows, LANES), f_rows, bias_rows, nc)
    df_pad = jnp.pad(df_rows.reshape(FOX_HEADS, s).T, ((0, 0), (0, LANES - FOX_HEADS)))
    dqb, dkb, dvb, dsinks = _swa_bwd(qb, qbt, kb, kbt, vb, sinks_g, dob, dobt, t["tq"])

    dproj = _assemble_dproj(dqt, dkt, dvt, dza, dqb, dzb, dkb, dvb, df_pad, cos_t, sin_t, t["blk"])
    gw_in = _unalign_w_in(_in_proj_bwd_w(h, dproj, t["tk"], t["tn"]))

    gin = jnp.pad(gw_in.reshape(D_MODEL, N_CHIPS, W_IN_SHARD).transpose(1, 0, 2),
                  ((0, 0), (0, 0), (0, W_IN_SHARD_PAD - W_IN_SHARD)))
    gout = gw_out.reshape(N_CHIPS, D_MODEL, W_OUT_SHARD)
    gbig = jnp.concatenate([gin, gout], axis=2)
    half = D_MODEL // 2
    gw = W_IN_SHARD_PAD + W_OUT_SHARD
    keep = lax.dynamic_slice_in_dim(gbig, core * half, half, axis=1)
    give = lax.dynamic_slice_in_dim(gbig, (1 - core) * half, half, axis=1)
    got = _swap_sibling(give.reshape(N_CHIPS * half, gw), "swap_grad_halves")
    pair = _add(keep.reshape(N_CHIPS * half, gw), got, "add_pair", BF).reshape(N_CHIPS, half, gw)
    grad_x, dshift, dscale, dg_pre, from_chips = _in_proj_bwd_x(
        dproj, w_al_t, x2, dout, g_pre, scale1p, t["tm_out"], pair)

    pad_lane = lambda vrow: jnp.pad(vrow, ((0, 0), (0, LANES - vrow.shape[1])))
    packed = jnp.concatenate([dshift, dscale, dgate, dg_pre, dg_post,
                              pad_lane(db_heads.reshape(1, FOX_HEADS)), pad_lane(dsinks.reshape(1, FOX_HEADS)),
                              pad_lane(loss_part)], axis=1)
    parts = _allgather_devices(packed, "gather_partials")
    tot = _sum_devices(parts)
    loss = tot[0, P_LOSS]
    g_b_ada = tot[:, P_DMOD:P_DMOD + 3 * D_MODEL]
    g_g_pre = tot[:, P_GPRE:P_GPRE + D_MODEL]
    g_g_post = tot[:, P_GPOST:P_GPOST + D_MODEL]
    g_b_fgate = tot[:, P_BF:P_BF + FOX_HEADS]
    g_sinks = tot[:, P_SINK:P_SINK + FOX_HEADS]
    dm_shard = lax.dynamic_slice_in_dim(parts[:, 0, :3 * D_MODEL], chip * W_ADA_SHARD, W_ADA_SHARD, axis=1)
    g_w_ada = _grad_w_ada(a_all.T, dm_shard)

    mine = _sum_chips(from_chips, "sum_chips")
    other = _swap_sibling(mine, "swap_grad_result")
    lo = jnp.where(core == 0, mine, other)
    hi = jnp.where(core == 0, other, mine)
    gfull = jnp.concatenate([lo, hi], axis=0)
    g_w_in = gfull[:, :W_IN_SHARD]
    g_w_out = gfull[:, W_IN_SHARD_PAD:].reshape(W_OUT_SHARD, D_MODEL)

    grads = dict(w_ada=g_w_ada, b_ada=g_b_ada, g_pre=g_g_pre, w_in=g_w_in, b_fgate=g_b_fgate, sinks=g_sinks,
                 w_out=g_w_out, g_post=g_g_post)
    weights = dict(w_ada=w_ada, b_ada=b_ada, g_pre=g_pre, w_in=w_in, b_fgate=b_fgate, sinks=sinks, w_out=w_out, g_post=g_post)
    moms = dict(w_ada=m_w_ada, b_ada=m_b_ada, g_pre=m_g_pre, w_in=m_w_in, b_fgate=m_b_fgate, sinks=m_sinks, w_out=m_w_out, g_post=m_g_post)
    vars_ = dict(w_ada=v_w_ada, b_ada=v_b_ada, g_pre=v_g_pre, w_in=v_w_in, b_fgate=v_b_fgate, sinks=v_sinks, w_out=v_w_out, g_post=v_g_post)
    names = ["w_ada", "b_ada", "g_pre", "w_in", "b_fgate", "sinks", "w_out", "g_post"]
    g_out, d_out, m_out, v_out = [], [], [], []
    for n in names:
        g2 = grads[n].reshape(weights[n].shape[-2:])
        go, d, nm, nv = _adamw(weights[n], g2, moms[n], vars_[n], "adamw_" + n)
        g_out.append(go)
        d_out.append(d)
        m_out.append(nm)
        v_out.append(nv)
    return (loss, grad_x.reshape(x.shape), *g_out, *d_out, *m_out, *v_out)
```

```python
import functools

import jax
import jax.numpy as jnp
from jax import lax
from jax.experimental import pallas as pl
from jax.experimental.pallas import tpu as pltpu

_INTERPRET = False

D_MODEL = 1024
HEAD_DIM = 64
HALF = HEAD_DIM // 2
AUG_DIM = 128
AUG_ROWS = 8
VT_ROWS = 80
LOG2E = 1.4426950408889634
LN2 = 0.6931471805599453
Q_SCALE = LOG2E * 64 ** -0.5
FOX_HEADS = 8
FOX_W = 512
SWA_W = 512
SWA_KV_HEADS = 2
SWA_GROUP = 4
SWA_KV_W = 128
WINDOW = 128
ROPE_THETA = 10000.0
RMS_EPS = 1e-6
IN_WIDTH = 3336
N_CHIPS = 4
N_DEV = 8
W_IN_SHARD = IN_WIDTH // N_CHIPS
W_IN_SHARD_PAD = 896
W_ADA_SHARD = 3 * D_MODEL // N_CHIPS
W_OUT_SHARD = D_MODEL // N_CHIPS
LANES = 128

_SRC = dict(qa=0, ka=512, va=1024, fa=1536, za=1544, qb=2056, kb=2568, vb=2696, zb=2824)
C_QA, C_KA, C_VA, C_ZA, C_QB, C_ZB, C_KB, C_VB, C_F = 0, 512, 1024, 1536, 2048, 2560, 3072, 3200, 3328
WP = 3456

ADAM_LR = 0.001
ADAM_B1 = 0.9
ADAM_B2 = 0.999
ADAM_EPS = 1e-08
ADAM_WD = 0.01
ADAM_STEP = 10
ADAMW_BLOCK_ELEMS = 300_000

VMEM_LIMIT = 56 * 1024 * 1024
NEG = -1e30
OVERFLOW_GUARD = 1e30
MESH = pl.DeviceIdType.MESH
BF = jnp.bfloat16
F32 = jnp.float32

P_DMOD, P_GPRE, P_GPOST, P_BF, P_SINK, P_LOSS, P_LEN = 0, 3072, 4096, 5120, 5248, 5376, 5504


def _call(body, **kw):
    return pl.pallas_call(body, interpret=_INTERPRET, **kw)


def _params(sem=None, **kw):
    return pltpu.CompilerParams(dimension_semantics=sem, vmem_limit_bytes=VMEM_LIMIT, **kw)


def _full(shape):
    zeros = (0,) * len(shape)
    return pl.BlockSpec(shape, lambda *_: zeros)


def _dot(a, b):
    return jnp.dot(a, b, preferred_element_type=F32)


def _dot_nt(a, b):
    return lax.dot_general(a, b, (((1,), (1,)), ((), ())), preferred_element_type=F32)


def _dot_tn(a, b):
    return lax.dot_general(a, b, (((0,), (0,)), ((), ())), preferred_element_type=F32)


def _sigmoid(z):
    return 1.0 / (1.0 + jnp.exp(-z))


def _rope_partner(t):
    w = t.shape[-1]
    lane = lax.broadcasted_iota(jnp.int32, t.shape, t.ndim - 1)
    return jnp.where((lane & (HEAD_DIM - 1)) < HALF, pltpu.roll(t, w - HALF, t.ndim - 1), pltpu.roll(t, HALF, t.ndim - 1))


def _allgather_devices(v, name):
    r, cdim = v.shape
    masks = [(dx, dy, dc) for dx in (0, 1) for dy in (0, 1) for dc in (0, 1)][1:]

    def body(v_ref, out_ref, send_sems, recv_sems):
        x, y, c = lax.axis_index("x"), lax.axis_index("y"), lax.axis_index("c")
        me = 4 * x + 2 * y + c
        out_ref[me] = v_ref[...]
        copies = []
        for k, (dx, dy, dc) in enumerate(masks):
            cp = pltpu.make_async_remote_copy(
                src_ref=v_ref, dst_ref=out_ref.at[me], send_sem=send_sems.at[k], recv_sem=recv_sems.at[k],
                device_id=(x ^ dx, y ^ dy, c ^ dc), device_id_type=MESH)
            cp.start()
            copies.append(cp)
        for k, (dx, dy, dc) in enumerate(masks):
            peer = 4 * (x ^ dx) + 2 * (y ^ dy) + (c ^ dc)
            pltpu.make_async_remote_copy(
                src_ref=v_ref, dst_ref=out_ref.at[peer], send_sem=send_sems.at[k], recv_sem=recv_sems.at[k],
                device_id=(x ^ dx, y ^ dy, c ^ dc), device_id_type=MESH).wait_recv()
        for cp in copies:
            cp.wait_send()

    return _call(
        body, name=name, out_shape=jax.ShapeDtypeStruct((N_DEV, r, cdim), v.dtype),
        in_specs=[pl.BlockSpec(memory_space=pltpu.VMEM)], out_specs=pl.BlockSpec(memory_space=pltpu.VMEM),
        scratch_shapes=[pltpu.SemaphoreType.DMA((7,)), pltpu.SemaphoreType.DMA((7,))],
        compiler_params=pltpu.CompilerParams(has_side_effects=True),
    )(v)


def _allgather_chips(v, name):
    _, r, cdim = v.shape
    masks = [(1, 0), (0, 1), (1, 1)]
    n = len(masks)

    def body(v_ref, out_ref, send_sems, recv_sems, local_sem):
        x, y, c = lax.axis_index("x"), lax.axis_index("y"), lax.axis_index("c")
        me = 2 * x + y
        mine = pltpu.make_async_copy(v_ref, out_ref.at[me], local_sem)
        mine.start()

        def copy(k, chip, half, to):
            return pltpu.make_async_remote_copy(
                src_ref=v_ref.at[half] if k < n else out_ref.at[chip, half], dst_ref=out_ref.at[chip, half],
                send_sem=send_sems.at[k], recv_sem=recv_sems.at[k], device_id=to, device_id_type=MESH)

        first = [copy(k, me, c, (x ^ dx, y ^ dy, c)) for k, (dx, dy) in enumerate(masks)]
        for cp in first:
            cp.start()
        passed = []
        for k, (dx, dy) in enumerate(masks):
            peer = 2 * (x ^ dx) + (y ^ dy)
            copy(k, peer, c, (x, y, c)).wait_recv()
            cp = copy(n + k, peer, c, (x, y, 1 - c))
            cp.start()
            passed.append(cp)
        for k, (dx, dy) in enumerate(masks):
            copy(n + k, 2 * (x ^ dx) + (y ^ dy), 1 - c, (x, y, c)).wait_recv()
        for cp in first + passed:
            cp.wait_send()
        mine.wait()

    return _call(
        body, name=name, out_shape=jax.ShapeDtypeStruct((N_CHIPS, 2, r, cdim), v.dtype),
        in_specs=[pl.BlockSpec(memory_space=pl.ANY)], out_specs=pl.BlockSpec(memory_space=pl.ANY),
        scratch_shapes=[pltpu.SemaphoreType.DMA((2 * n,)), pltpu.SemaphoreType.DMA((2 * n,)), pltpu.SemaphoreType.DMA],
        compiler_params=pltpu.CompilerParams(has_side_effects=True),
    )(v)


def _swap_sibling(v, name):
    def body(v_ref, out_ref, send_sem, recv_sem):
        x, y, c = lax.axis_index("x"), lax.axis_index("y"), lax.axis_index("c")
        cp = pltpu.make_async_remote_copy(
            src_ref=v_ref, dst_ref=out_ref, send_sem=send_sem, recv_sem=recv_sem,
            device_id=(x, y, 1 - c), device_id_type=MESH)
        cp.start()
        cp.wait()

    return _call(
        body, name=name, out_shape=jax.ShapeDtypeStruct(v.shape, v.dtype),
        in_specs=[pl.BlockSpec(memory_space=pl.ANY)], out_specs=pl.BlockSpec(memory_space=pl.ANY),
        scratch_shapes=[pltpu.SemaphoreType.DMA, pltpu.SemaphoreType.DMA],
        compiler_params=pltpu.CompilerParams(has_side_effects=True),
    )(v)


def _ada_shard(c_all, w_ada_shard):
    def body(c_ref, w_ref, a_ref, mod_ref):
        cv = c_ref[...]
        a = cv * _sigmoid(cv)
        a_ref[...] = a
        mod_ref[...] = _dot(a.astype(BF), w_ref[...].astype(BF))

    return _call(
        body, name="ada_shard",
        out_shape=(jax.ShapeDtypeStruct((N_DEV, D_MODEL), F32), jax.ShapeDtypeStruct((N_DEV, W_ADA_SHARD), F32)),
        compiler_params=_params(),
    )(c_all, w_ada_shard)


def _grad_w_ada(a_t, dm_shard):
    def body(a_ref, dm_ref, out_ref):
        acc = jnp.zeros((D_MODEL, W_ADA_SHARD), F32)
        for b in range(N_DEV):
            acc = acc + a_ref[:, b:b + 1] * dm_ref[b:b + 1, :]
        out_ref[...] = acc

    return _call(body, name="grad_w_ada", out_shape=jax.ShapeDtypeStruct((D_MODEL, W_ADA_SHARD), F32),
                 compiler_params=_params())(a_t, dm_shard)


def _sum_devices(parts):
    n = parts.shape[-1]

    def body(p_ref, out_ref):
        acc = p_ref[0]
        for b in range(1, N_DEV):
            acc = acc + p_ref[b]
        out_ref[...] = acc

    return _call(body, name="sum_devices", out_shape=jax.ShapeDtypeStruct((1, n), F32), compiler_params=_params())(parts)


def _add(a, b, name, out_dtype):
    r, cdim = a.shape
    tr = min(r, 256)

    def body(a_ref, b_ref, o_ref):
        o_ref[...] = (a_ref[...].astype(F32) + b_ref[...].astype(F32)).astype(out_dtype)

    spec = pl.BlockSpec((tr, cdim), lambda i: (i, 0))
    return _call(body, name=name, out_shape=jax.ShapeDtypeStruct(a.shape, out_dtype), grid=(r // tr,),
                 in_specs=[spec, spec], out_specs=spec, compiler_params=_params(("parallel",)))(a, b)


def _sum_chips(parts, name):
    _, r, cdim = parts.shape
    tr = min(r, 128)

    def body(p_ref, o_ref):
        o_ref[...] = ((p_ref[0].astype(F32) + p_ref[1].astype(F32)) + p_ref[2].astype(F32)) + p_ref[3].astype(F32)

    return _call(body, name=name, out_shape=jax.ShapeDtypeStruct((r, cdim), F32), grid=(r // tr,),
                 in_specs=[pl.BlockSpec((N_CHIPS, tr, cdim), lambda i: (0, i, 0))],
                 out_specs=pl.BlockSpec((tr, cdim), lambda i: (i, 0)), compiler_params=_params(("parallel",)))(parts)


def _adamw(w, g, m, v, name):
    r, cdim = w.shape[-2:]
    lead = w.ndim - 2
    tr = r if r <= 256 else max(t for t in range(8, ADAMW_BLOCK_ELEMS // cdim + 1, 8) if r % t == 0)
    c1 = 1.0 / (1.0 - ADAM_B1 ** ADAM_STEP)
    c2 = 1.0 / (1.0 - ADAM_B2 ** ADAM_STEP)

    def body(w_ref, g_ref, m_ref, v_ref, go_ref, d_ref, nm_ref, nv_ref):
        gv = g_ref[...].reshape(go_ref.shape)
        nm = ADAM_B1 * m_ref[...] + (1.0 - ADAM_B1) * gv
        nv = ADAM_B2 * v_ref[...] + (1.0 - ADAM_B2) * (gv * gv)
        m_hat = nm * c1
        v_hat = nv * c2
        go_ref[...] = gv
        d_ref[...] = -ADAM_LR * (m_hat / (jnp.sqrt(v_hat) + ADAM_EPS) + ADAM_WD * w_ref[...])
        nm_ref[...] = nm
        nv_ref[...] = nv

    spec = pl.BlockSpec((1,) * lead + (tr, cdim), lambda i: (0,) * lead + (i, 0))
    shp = jax.ShapeDtypeStruct(w.shape, F32)
    return _call(body, name=name, out_shape=(shp,) * 4, grid=(r // tr,),
                 in_specs=[spec, pl.BlockSpec((tr, cdim), lambda i: (i, 0)), spec, spec],
                 out_specs=(spec,) * 4, compiler_params=_params(("parallel",)))(w, g, m, v)


def _head_of_row(r, nc):
    assert nc & (nc - 1) == 0
    return lax.shift_right_logical(r, nc.bit_length() - 1)


def _chunk_mats(rows, nc, reverse):
    ri = lax.broadcasted_iota(jnp.int32, (rows, rows), 0)
    ci = lax.broadcasted_iota(jnp.int32, (rows, rows), 1)
    same = _head_of_row(ri, nc) == _head_of_row(ci, nc)
    between = jnp.where(same & ((ci > ri) if reverse else (ci < ri)), 1.0, 0.0).astype(F32)
    li = lax.broadcasted_iota(jnp.int32, (LANES, LANES), 0)
    lj = lax.broadcasted_iota(jnp.int32, (LANES, LANES), 1)
    within = jnp.where((li >= lj) if reverse else (li <= lj), 1.0, 0.0).astype(F32)
    return between, within


def _dot_hi(a, b):
    return jnp.dot(a, b, preferred_element_type=F32, precision=lax.Precision.HIGHEST)


def _scan_rows(t, nc, reverse):
    between, within = _chunk_mats(t.shape[0], nc, reverse)
    inner = _dot_hi(t, within)
    tot = jnp.sum(t, axis=1, keepdims=True)
    return inner + _dot_hi(between, jnp.broadcast_to(tot, t.shape))


def _log_forget_cumsum(f_rows, bias_rows, nc):
    def body(f_ref, b_ref, cum_ref):
        z = f_ref[...] + b_ref[...]
        lf = jnp.minimum(z, 0.0) - jnp.log(1.0 + jnp.exp(-jnp.abs(z)))
        cum_ref[...] = _scan_rows(lf, nc, False)

    return _call(body, name="forget_cumsum", out_shape=jax.ShapeDtypeStruct(f_rows.shape, F32),
                 compiler_params=_params())(f_rows, bias_rows)


def _log_forget_cumsum_bwd(dcum_rows, f_rows, bias_rows, nc):
    rows = f_rows.shape[0]

    def body(d_ref, f_ref, b_ref, df_ref, db_ref):
        dlf = _scan_rows(d_ref[...], nc, True)
        z = f_ref[...] + b_ref[...]
        df = dlf * _sigmoid(-z)
        df_ref[...] = df
        hi = lax.broadcasted_iota(jnp.int32, (FOX_HEADS, rows), 0)
        ri = lax.broadcasted_iota(jnp.int32, (FOX_HEADS, rows), 1)
        sel = jnp.where(_head_of_row(ri, nc) == hi, 1.0, 0.0).astype(F32)
        db_ref[...] = jnp.sum(_dot_hi(sel, df), axis=1, keepdims=True)

    return _call(body, name="forget_cumsum_bwd",
                 out_shape=(jax.ShapeDtypeStruct(f_rows.shape, F32), jax.ShapeDtypeStruct((FOX_HEADS, 1), F32)),
                 compiler_params=_params())(dcum_rows, f_rows, bias_rows)


def _rms_hat(xv):
    rstd = lax.rsqrt(jnp.mean(xv * xv, axis=-1, keepdims=True) + RMS_EPS)
    return xv * rstd, rstd


def _modulated(x_ref, g_ref, sc_ref, sh_ref):
    xhat, _ = _rms_hat(x_ref[...])
    return ((xhat * g_ref[...]) * sc_ref[...] + sh_ref[...]).astype(BF)


def _forget_logits(x, g_pre, scale1p, shift, w_f, tm):
    s = x.shape[0]

    def body(x_ref, g_ref, sc_ref, sh_ref, w_ref, f_ref):
        f_ref[...] = _dot(_modulated(x_ref, g_ref, sc_ref, sh_ref), w_ref[...])

    vec = _full((1, D_MODEL))
    return _call(
        body, name="forget_logits", out_shape=jax.ShapeDtypeStruct((s, LANES), F32), grid=(s // tm,),
        in_specs=[pl.BlockSpec((tm, D_MODEL), lambda i: (i, 0)), vec, vec, vec, _full((D_MODEL, LANES))],
        out_specs=pl.BlockSpec((tm, LANES), lambda i: (i, 0)), compiler_params=_params(("parallel",)),
    )(x, g_pre, scale1p, shift, w_f)


def _split3(v):
    hi = v.astype(BF).astype(F32)
    mid = (v - hi).astype(BF).astype(F32)
    lo = ((v - hi) - mid).astype(BF).astype(F32)
    return hi, mid, lo


def _in_proj(x, g_pre, scale1p, shift, w_rows, w_t_fox, cum, cos_t, sin_t, tm):
    s = x.shape[0]
    r_va, r_za, r_qb, r_zb, r_kb, r_vb = 0, 512, 1024, 1536, 2048, 2176

    def body(x_ref, g_ref, sc_ref, sh_ref, w_ref, wt_ref, cum_ref, cos_ref, sin_ref,
             h_ref, qat_ref, ka_ref, kat_ref, v_ref, vt_ref, za_ref, zb_ref, qb_ref, kb_ref, vb_ref,
             qbt_ref, kbt_ref, vbt_ref, mo_ref):
        hb = _modulated(x_ref, g_ref, sc_ref, sh_ref)
        h_ref[...] = hb

        def sec(c0, width):
            return _dot(hb, w_ref[:, c0:c0 + width])

        def sec_t(r0):
            return _dot_nt(wt_ref[r0:r0 + FOX_W, :], hb)

        q_t = sec_t(0) * Q_SCALE
        k_t = sec_t(FOX_W)
        v_t = sec_t(2 * FOX_W)
        va = sec(r_va, FOX_W)
        zeros = jnp.zeros((AUG_DIM - HEAD_DIM - AUG_ROWS, tm), F32)
        ri = lax.broadcasted_iota(jnp.int32, (AUG_ROWS, tm), 0)
        const = jnp.where(ri == AUG_ROWS - 1, 0.0, 1.0)
        ri_v = lax.broadcasted_iota(jnp.int32, (VT_ROWS - HEAD_DIM, tm), 0)
        v_feat = jnp.where(ri_v == 0, 1.0, 0.0).astype(BF)
        for hd in range(FOX_HEADS):
            rows = slice(hd * HEAD_DIM, (hd + 1) * HEAD_DIM)
            cum2 = cum_ref[hd:hd + 1, :] * LOG2E
            hi, mid, lo = (jnp.broadcast_to(part, (AUG_ROWS, tm)) for part in _split3(cum2))
            q_feat = jnp.where(ri == 1, hi, jnp.where(ri == 2, mid, jnp.where(ri == 3, lo, const)))
            k_feat = jnp.where(ri == 4, -hi, jnp.where(ri == 5, -mid, jnp.where(ri == 6, -lo, const)))
            q_aug = jnp.concatenate([q_t[rows], q_feat, zeros], axis=0)
            k_aug = jnp.concatenate([k_t[rows], k_feat, zeros], axis=0)
            mo_ref[hd:hd + 1, :] = jnp.sum(q_t[rows] * k_t[rows], axis=0, keepdims=True) + 1.0
            qat_ref[hd] = q_aug.astype(BF)
            kat_ref[hd] = k_aug.astype(BF)
            ka_ref[hd] = k_aug.T.astype(BF)
            vt_ref[hd] = jnp.concatenate([v_t[rows].astype(BF), v_feat], axis=0)
            v_ref[hd] = va[:, rows].astype(BF)
        za_ref[...] = sec(r_za, FOX_W)
        zb_ref[...] = sec(r_zb, SWA_W)
        cos2, sin2 = cos_ref[...], sin_ref[...]
        cos8 = jnp.concatenate([cos2] * 4, axis=1)
        sin8 = jnp.concatenate([sin2] * 4, axis=1)
        qb = sec(r_qb, SWA_W)
        qb = (qb * cos8 + _rope_partner(qb) * sin8) * (HEAD_DIM ** -0.5)
        qb_ref[...] = qb.astype(BF)
        for a in range(SWA_W // LANES):
            qbt_ref[a * LANES:(a + 1) * LANES, :] = qb[:, a * LANES:(a + 1) * LANES].T.astype(BF)
        kb = sec(r_kb, SWA_KV_W)
        kb = kb * cos2 + _rope_partner(kb) * sin2
        vb = sec(r_vb, SWA_KV_W)
        kb_t, vb_t = kb.T, vb.T
        for hd in range(SWA_KV_HEADS):
            sl = slice(hd * HEAD_DIM, (hd + 1) * HEAD_DIM)
            kb_ref[hd] = kb[:, sl].astype(BF)
            vb_ref[hd] = vb[:, sl].astype(BF)
            kbt_ref[hd] = kb_t[sl].astype(BF)
            vbt_ref[hd] = jnp.concatenate([vb_t[sl].astype(BF), v_feat], axis=0)

    row = lambda w: pl.BlockSpec((tm, w), lambda i: (i, 0))
    heads = lambda n, w=HEAD_DIM: pl.BlockSpec((n, tm, w), lambda i: (0, i, 0))
    heads_t = lambda w: pl.BlockSpec((FOX_HEADS, w, tm), lambda i: (0, 0, i))
    vec = _full((1, D_MODEL))
    hs = lambda a, b: jax.ShapeDtypeStruct((FOX_HEADS, a, b), BF)
    out_shape = (
        jax.ShapeDtypeStruct((s, D_MODEL), BF),
        hs(AUG_DIM, s), hs(s, AUG_DIM), hs(AUG_DIM, s), hs(s, HEAD_DIM), hs(VT_ROWS, s),
        jax.ShapeDtypeStruct((s, FOX_W), F32), jax.ShapeDtypeStruct((s, SWA_W), F32),
        jax.ShapeDtypeStruct((s, SWA_W), BF),
        jax.ShapeDtypeStruct((SWA_KV_HEADS, s, HEAD_DIM), BF), jax.ShapeDtypeStruct((SWA_KV_HEADS, s, HEAD_DIM), BF),
        jax.ShapeDtypeStruct((SWA_W, s), BF),
        jax.ShapeDtypeStruct((SWA_KV_HEADS, HEAD_DIM, s), BF), jax.ShapeDtypeStruct((SWA_KV_HEADS, VT_ROWS, s), BF),
        jax.ShapeDtypeStruct((FOX_HEADS, s), F32),
    )
    kv_t = lambda w: pl.BlockSpec((SWA_KV_HEADS, w, tm), lambda i: (0, 0, i))
    return _call(
        body, name="in_proj", out_shape=out_shape, grid=(s // tm,),
        in_specs=[row(D_MODEL), vec, vec, vec, _full(w_rows.shape), _full(w_t_fox.shape),
                  pl.BlockSpec((FOX_HEADS, tm), lambda i: (0, i)), row(LANES), row(LANES)],
        out_specs=(row(D_MODEL), heads_t(AUG_DIM), heads(FOX_HEADS, AUG_DIM), heads_t(AUG_DIM), heads(FOX_HEADS),
                   heads_t(VT_ROWS), row(FOX_W), row(SWA_W), row(SWA_W), heads(SWA_KV_HEADS), heads(SWA_KV_HEADS),
                   pl.BlockSpec((SWA_W, tm), lambda i: (0, i)), kv_t(HEAD_DIM), kv_t(VT_ROWS),
                   pl.BlockSpec((FOX_HEADS, tm), lambda i: (0, i))),
        compiler_params=_params(("parallel",)),
    )(x, g_pre, scale1p, shift, w_rows, w_t_fox, cum, cos_t, sin_t)


def _diag_chunks(d, bq, bk, chunk):
    out = []
    for c0 in range(0, bq, chunk):
        if d is None or d * bk + bk - 1 <= c0:
            out.append((c0, None, bk))
        elif d * bk <= c0 + chunk - 1:
            n_keys = min(bk, c0 + chunk - d * bk)
            kpos = d * bk + lax.broadcasted_iota(jnp.int32, (n_keys, chunk), 0)
            qpos = c0 + lax.broadcasted_iota(jnp.int32, (n_keys, chunk), 1)
            out.append((c0, kpos <= qpos, n_keys))
    return out


def _fox_fwd(qat, ka, vt, m_own, bq, bk, chunk, running_max):
    nh, _, s = qat.shape
    r = bq // bk

    pairs = [(i, j) for i in range(s // bq) for j in range(i * r + r)]

    def body(i_tab, j_tab, ka_ref, qat_ref, vt_ref, mo_ref, o_ref, lse_ref, bad_ref, *rest):
        pt_ref, m_scr, acc_scr = (None,) * running_max + rest
        i, j = i_tab[pl.program_id(1)], j_tab[pl.program_id(1)]

        @pl.when(j == 0)
        def _():
            m_scr[...] = jnp.full(m_scr.shape, NEG, F32) if running_max else mo_ref[0]
            acc_scr[...] = jnp.zeros(acc_scr.shape, F32)

        def careful(d):
            kv, vtv = ka_ref[0], vt_ref[0]

            def one_chunk(n, carry):
                c0 = pl.multiple_of(n * chunk, chunk)
                cs = pl.ds(c0, chunk)
                sc = _dot(kv, qat_ref[0, :, cs])
                if d is not None:
                    kpos = d * bk + lax.broadcasted_iota(jnp.int32, (bk, chunk), 0)
                    qpos = c0 + lax.broadcasted_iota(jnp.int32, (bk, chunk), 1)
                    sc = jnp.where(kpos <= qpos, sc, NEG)
                m_prev = m_scr[:, cs]
                m_new = jnp.maximum(m_prev, jnp.max(sc, axis=0, keepdims=True))
                p = jnp.exp2(sc - m_new).astype(BF)
                acc_scr[:, cs] = jnp.exp2(m_prev - m_new) * acc_scr[:, cs] + _dot(vtv, p)
                m_scr[:, cs] = m_new
                return carry

            lax.fori_loop(0, bq // chunk, one_chunk, 0)

        def fast(d):
            todo = _diag_chunks(d, bq, bk, chunk)
            scores = lambda t: _dot(ka_ref[0, :t[2], :], qat_ref[0, :, t[0]:t[0] + chunk])
            sc_next = scores(todo[0])
            for n, (c0, mask, n_keys) in enumerate(todo):
                cs = slice(c0, c0 + chunk)
                sc = sc_next
                if n + 1 < len(todo):
                    sc_next = scores(todo[n + 1])
                if mask is not None:
                    sc = jnp.where(mask, sc, NEG)
                p = jnp.exp2(sc - m_scr[:, cs]).astype(BF)
                pt_ref[0, :n_keys, cs] = p
                acc_scr[:, cs] += _dot(vt_ref[0, :, :n_keys], p)

        step = careful if running_max else fast

        @pl.when(j < i * r)
        def _():
            step(None)

        for d in range(r):
            @pl.when(j == i * r + d)
            def _(d=d):
                step(d)

        @pl.when(j == i * r + r - 1)
        def _():
            l = acc_scr[HEAD_DIM:HEAD_DIM + 1, :]
            o_ref[0] = acc_scr[:HEAD_DIM, :] / l
            lse_ref[0] = m_scr[...] + jnp.log2(l)
            bad_ref[0] = jnp.where(l < OVERFLOW_GUARD, 0.0, 1.0)

    qmap_t = lambda h, t, it, jt: (h, 0, it[t])
    qrow = pl.BlockSpec((1, 1, bq), qmap_t)
    row_shape = jax.ShapeDtypeStruct((nh, 1, s), F32)
    out_shape = (jax.ShapeDtypeStruct((nh, HEAD_DIM, s), F32), row_shape, row_shape)
    out_specs = (pl.BlockSpec((1, HEAD_DIM, bq), qmap_t), qrow, qrow)
    if not running_max:
        out_shape += (jax.ShapeDtypeStruct((nh, s, s), BF),)
        out_specs += (pl.BlockSpec((1, bk, bq), lambda h, t, it, jt: (h, jt[t], it[t])),)
    grid_spec = pltpu.PrefetchScalarGridSpec(
        num_scalar_prefetch=2, grid=(nh, len(pairs)),
        in_specs=[pl.BlockSpec((1, bk, AUG_DIM), lambda h, t, it, jt: (h, jt[t], 0)), pl.BlockSpec((1, AUG_DIM, bq), qmap_t),
                  pl.BlockSpec((1, VT_ROWS, bk), lambda h, t, it, jt: (h, 0, jt[t])), qrow],
        out_specs=out_specs,
        scratch_shapes=[pltpu.VMEM((1, bq), F32), pltpu.VMEM((VT_ROWS, bq), F32)])
    return _call(
        body, name="fox_fwd_running_max" if running_max else "fox_fwd", out_shape=out_shape, grid_spec=grid_spec,
        compiler_params=_params(("parallel", "arbitrary")),
    )(jnp.asarray([p[0] for p in pairs], jnp.int32), jnp.asarray([p[1] for p in pairs], jnp.int32), ka, qat, vt, m_own)


def _fox_bwd(qat, ka, kat, v, dot_, lse, delta, pt, bq, bk, chunk, dq_blk):
    nh, _, s = qat.shape
    r = bq // bk
    nq = s // bq
    stored = pt is not None

    pairs = [(j, i) for j in range(s // bk) for i in range(j // r, nq)]

    def body(j_tab, i_tab, a_ref, b_ref, kat_ref, v_ref, qat_ref, do_ref, dl_ref, dq_ref, dk_ref, dv_ref, dk_scr, dv_scr):
        ka_ref, lse_ref, pt_ref = (None, None, a_ref) if stored else (a_ref, b_ref, None)
        j, i = j_tab[pl.program_id(1)], i_tab[pl.program_id(1)]

        @pl.when(pl.program_id(1) == 0)
        def _():
            dq_ref[...] = jnp.zeros(dq_ref.shape, F32)

        @pl.when(i * r <= j)
        def _():
            dk_scr[...] = jnp.zeros(dk_scr.shape, F32)
            dv_scr[...] = jnp.zeros(dv_scr.shape, F32)

        def step(d):
            todo = _diag_chunks(d, bq, bk, chunk)

            def products(t):
                cs = slice(t[0], t[0] + chunk)
                return (None if stored else _dot(ka_ref[0, :t[2], :], qat_ref[0, :, cs]),
                        _dot(v_ref[0, :t[2], :], do_ref[0, :, cs]))

            nxt = products(todo[0])
            for n, (c0, mask, n_keys) in enumerate(todo):
                cs = slice(c0, c0 + chunk)
                sc, dp = nxt
                if n + 1 < len(todo):
                    nxt = products(todo[n + 1])
                if stored:
                    p_bf = pt_ref[0, :n_keys, cs]
                    p = p_bf.astype(F32)
                else:
                    p = jnp.exp2(sc - lse_ref[0, :, cs])
                    if mask is not None:
                        p = jnp.where(mask, p, 0.0)
                    p_bf = p.astype(BF)
                ds = (p * (dp - dl_ref[0, :, cs])).astype(BF)
                dv_scr[:, :n_keys] += _dot_nt(do_ref[0, :, cs], p_bf)
                dk_scr[:, :n_keys] += _dot_nt(qat_ref[0, :VT_ROWS, cs], ds)
                c1 = c0 % dq_blk
                dq_ref[0, i * (bq // dq_blk) + c0 // dq_blk, :, c1:c1 + chunk] += _dot(kat_ref[0, :VT_ROWS, :n_keys], ds)

        @pl.when(i * r > j)
        def _():
            step(None)

        for d in range(r):
            @pl.when(j == i * r + d)
            def _(d=d):
                step(d)

        @pl.when(i == nq - 1)
        def _():
            dk_ref[0] = dk_scr[...]
            dv_ref[0] = dv_scr[...]

    qmap = lambda h, t, jt, it: (h, 0, it[t])
    kmap = lambda h, t, jt, it: (h, jt[t], 0)
    kmap_t = lambda h, t, jt, it: (h, 0, jt[t])
    if stored:
        first = [(pt, pl.BlockSpec((1, bk, bq), lambda h, t, jt, it: (h, jt[t], it[t]))),
                 (delta, pl.BlockSpec((1, 1, bq), qmap))]
    else:
        first = [(ka, pl.BlockSpec((1, bk, AUG_DIM), kmap)), (lse, pl.BlockSpec((1, 1, bq), qmap))]
    grid_spec = pltpu.PrefetchScalarGridSpec(
        num_scalar_prefetch=2, grid=(nh, len(pairs)),
        in_specs=[first[0][1], first[1][1], pl.BlockSpec((1, AUG_DIM, bk), kmap_t), pl.BlockSpec((1, bk, HEAD_DIM), kmap),
                  pl.BlockSpec((1, AUG_DIM, bq), qmap), pl.BlockSpec((1, HEAD_DIM, bq), qmap),
                  pl.BlockSpec((1, 1, bq), qmap)],
        out_specs=(pl.BlockSpec((1, s // dq_blk, VT_ROWS, dq_blk), lambda h, t, jt, it: (h, 0, 0, 0)),
                   pl.BlockSpec((1, VT_ROWS, bk), kmap_t), pl.BlockSpec((1, HEAD_DIM, bk), kmap_t)),
        scratch_shapes=[pltpu.VMEM((VT_ROWS, bk), F32), pltpu.VMEM((HEAD_DIM, bk), F32)])
    return _call(
        body, name="fox_bwd" if stored else "fox_bwd_recompute",
        out_shape=(jax.ShapeDtypeStruct((nh, s // dq_blk, VT_ROWS, dq_blk), F32),
                   jax.ShapeDtypeStruct((nh, VT_ROWS, s), F32), jax.ShapeDtypeStruct((nh, HEAD_DIM, s), F32)),
        grid_spec=grid_spec, compiler_params=_params(("parallel", "arbitrary")),
    )(jnp.asarray([p[0] for p in pairs], jnp.int32), jnp.asarray([p[1] for p in pairs], jnp.int32),
      first[0][0], first[1][0], kat, v, qat, dot_, delta)


def _swa_mask(i, tq):
    kpos = i * tq - WINDOW + lax.broadcasted_iota(jnp.int32, (tq + WINDOW, tq), 0)
    qpos = i * tq + lax.broadcasted_iota(jnp.int32, (tq + WINDOW, tq), 1)
    rel = qpos - kpos
    return (rel >= 0) & (rel < WINDOW) & (kpos >= 0)


def _swa_rows(ref, i, tq):
    before = pl.multiple_of(jnp.maximum(i * tq - WINDOW, 0), WINDOW)
    return jnp.concatenate([ref[0, pl.ds(before, WINDOW), :], ref[0, pl.ds(pl.multiple_of(i * tq, tq), tq), :]], axis=0)


def _swa_before(n_rows, tq):
    return pl.BlockSpec((1, n_rows, WINDOW), lambda g, i: (g, 0, jnp.maximum(i * (tq // WINDOW) - 1, 0)))


def _swa_probs_t(sc, mask, sink):
    sc = jnp.where(mask, sc, NEG)
    m = jnp.maximum(jnp.max(sc, axis=0, keepdims=True), sink)
    p = jnp.exp(sc - m)
    e_sink = jnp.exp(sink - m)
    inv_l = 1.0 / (jnp.sum(p, axis=0, keepdims=True) + e_sink)
    return p * inv_l, e_sink * inv_l


def _swa_fwd(qbt, kb, vbt, sinks, tq):
    s = qbt.shape[1]
    gw = SWA_GROUP * HEAD_DIM

    def body(q_ref, k_ref, vb_ref, vc_ref, s_ref, o_ref):
        i = pl.program_id(1)
        mask = _swa_mask(i, tq)
        kw = _swa_rows(k_ref, i, tq)
        vtw = jnp.concatenate([vb_ref[0], vc_ref[0]], axis=1)
        sk = s_ref[0]
        scores = lambda hh: _dot(kw, q_ref[hh * HEAD_DIM:(hh + 1) * HEAD_DIM, :])
        sc_next = scores(0)
        for hh in range(SWA_GROUP):
            rows = slice(hh * HEAD_DIM, (hh + 1) * HEAD_DIM)
            sink = sk[:, hh:hh + 1]
            sc = jnp.where(mask, sc_next, NEG)
            if hh + 1 < SWA_GROUP:
                sc_next = scores(hh + 1)
            m = jnp.maximum(jnp.max(sc, axis=0, keepdims=True), sink)
            acc = _dot(vtw, jnp.exp(sc - m).astype(BF))
            o_ref[rows, :] = acc[:HEAD_DIM] / (acc[HEAD_DIM:HEAD_DIM + 1] + jnp.exp(sink - m))

    kvspec = pl.BlockSpec((1, s, HEAD_DIM), lambda g, i: (g, 0, 0))
    qspec = pl.BlockSpec((gw, tq), lambda g, i: (g, i))
    return _call(
        body, name="swa_fwd", out_shape=jax.ShapeDtypeStruct((SWA_W, s), F32), grid=(SWA_KV_HEADS, s // tq),
        in_specs=[qspec, kvspec, _swa_before(VT_ROWS, tq), pl.BlockSpec((1, VT_ROWS, tq), lambda g, i: (g, 0, i)),
                  pl.BlockSpec((1, 1, SWA_GROUP), lambda g, i: (g, 0, 0))],
        out_specs=qspec, compiler_params=_params(("parallel", "parallel")),
    )(qbt, kb, vbt, vbt, sinks)


def _swa_bwd(qb, qbt, kb, kbt, vb, sinks, dob, dobt, tq):
    s = qb.shape[0]
    gw = SWA_GROUP * HEAD_DIM

    def body(q_ref, qt_ref, k_ref, ktb_ref, ktc_ref, v_ref, s_ref, do_ref, dot_ref, dq_ref, dk_ref, dv_ref, ds_ref):
        i = pl.program_id(1)

        @pl.when(i == 0)
        def _():
            dk_ref[...] = jnp.zeros(dk_ref.shape, F32)
            dv_ref[...] = jnp.zeros(dv_ref.shape, F32)
            ds_ref[...] = jnp.zeros(ds_ref.shape, F32)

        mask = _swa_mask(i, tq)
        kw = _swa_rows(k_ref, i, tq)
        vw = _swa_rows(v_ref, i, tq)
        ktw = jnp.concatenate([ktb_ref[0], ktc_ref[0]], axis=1)
        qv, dov = q_ref[...], do_ref[...]
        sk = s_ref[0]
        dsinks = []
        dk_acc = jnp.zeros((tq + WINDOW, HEAD_DIM), F32)
        dv_acc = jnp.zeros((tq + WINDOW, HEAD_DIM), F32)
        def products(hh):
            rows = slice(hh * HEAD_DIM, (hh + 1) * HEAD_DIM)
            return _dot(kw, qt_ref[rows, :]), _dot(vw, dot_ref[rows, :])

        nxt = products(0)
        for hh in range(SWA_GROUP):
            rows = slice(hh * HEAD_DIM, (hh + 1) * HEAD_DIM)
            sc, dp = nxt
            if hh + 1 < SWA_GROUP:
                nxt = products(hh + 1)
            p, p_sink = _swa_probs_t(sc, mask, sk[:, hh:hh + 1])
            delta = jnp.sum(p * dp, axis=0, keepdims=True)
            dsc = (p * (dp - delta)).astype(BF)
            dq_ref[rows, :] = _dot(ktw, dsc)
            dk_acc = dk_acc + _dot(dsc, qv[:, rows])
            dv_acc = dv_acc + _dot(p.astype(BF), dov[:, rows])
            dsinks.append(-jnp.sum(p_sink * delta, axis=1, keepdims=True))
        before = pl.ds(pl.multiple_of(jnp.maximum(i * tq - WINDOW, 0), WINDOW), WINDOW)
        own = pl.ds(pl.multiple_of(i * tq, tq), tq)
        dk_ref[0, before, :] += dk_acc[:WINDOW]
        dk_ref[0, own, :] += dk_acc[WINDOW:]
        dv_ref[0, before, :] += dv_acc[:WINDOW]
        dv_ref[0, own, :] += dv_acc[WINDOW:]
        ds_ref[0] += jnp.concatenate(dsinks, axis=1)

    kvspec = pl.BlockSpec((1, s, HEAD_DIM), lambda g, i: (g, 0, 0))
    qspec = pl.BlockSpec((tq, gw), lambda g, i: (i, g))
    qspec_t = pl.BlockSpec((gw, tq), lambda g, i: (g, i))
    sspec = pl.BlockSpec((1, 1, SWA_GROUP), lambda g, i: (g, 0, 0))
    kvshape = jax.ShapeDtypeStruct((SWA_KV_HEADS, s, HEAD_DIM), F32)
    return _call(
        body, name="swa_bwd",
        out_shape=(jax.ShapeDtypeStruct((SWA_W, s), F32), kvshape, kvshape,
                   jax.ShapeDtypeStruct((SWA_KV_HEADS, 1, SWA_GROUP), F32)),
        grid=(SWA_KV_HEADS, s // tq),
        in_specs=[qspec, qspec_t, kvspec, _swa_before(HEAD_DIM, tq),
                  pl.BlockSpec((1, HEAD_DIM, tq), lambda g, i: (g, 0, i)), kvspec, sspec, qspec, qspec_t],
        out_specs=(qspec_t, kvspec, kvspec, sspec),
        compiler_params=_params(("parallel", "arbitrary")),
    )(qb, qbt, kb, kbt, kbt, vb, sinks, dob, dobt)


def _pairs_to_rows(ref, n_rows=HEAD_DIM):
    parts = []
    for a in range(0, FOX_HEADS, 2):
        parts.append(jnp.concatenate([ref[a][:n_rows], ref[a + 1][:n_rows]], axis=0).T)
    return jnp.concatenate(parts, axis=1)


def _blocks_to_rows(ref):
    return jnp.concatenate([ref[a:a + LANES, :].T for a in range(0, ref.shape[0], LANES)], axis=1)


def _out_proj(oat, za, obt, zb, x, tgt, w_out, w_out_t, gate, g_post, inv_l, tm):
    s = x.shape[0]

    def body(oat_ref, za_ref, obt_ref, zb_ref, x_ref, t_ref, w_ref, wt_ref, gate_ref, gp_ref, il_ref,
             dout_ref, doat_ref, dla_ref, dza_ref, dob_ref, dobt_ref, dzb_ref, gw_ref, dgate_ref, dgp_ref, loss_ref):
        i = pl.program_id(0)

        @pl.when(i == 0)
        def _():
            gw_ref[...] = jnp.zeros(gw_ref.shape, F32)
            dgate_ref[...] = jnp.zeros(dgate_ref.shape, F32)
            dgp_ref[...] = jnp.zeros(dgp_ref.shape, F32)
            loss_ref[...] = jnp.zeros(loss_ref.shape, F32)

        oa_v = _pairs_to_rows(oat_ref)
        ob_v = _blocks_to_rows(obt_ref)
        za_v, zb_v = za_ref[...], zb_ref[...]
        sga, sgb = _sigmoid(za_v), _sigmoid(zb_v)
        sila, silb = za_v * sga, zb_v * sgb
        u = jnp.concatenate([oa_v * sila, ob_v * silb], axis=1).astype(BF)
        yv = _dot(u, w_ref[...])
        yhat, rstd = _rms_hat(yv)
        gp, gate_v = gp_ref[...], gate_ref[...]
        nrm = yhat * gp
        diff = (x_ref[...] + gate_v * nrm) - t_ref[...]
        loss_ref[...] += 0.5 * jnp.sum(jnp.sum(diff * diff, axis=1, keepdims=True), axis=0, keepdims=True) / D_MODEL
        dout = diff * (1.0 / D_MODEL)
        dout_ref[...] = dout
        dgate_ref[...] += jnp.sum(dout * nrm, axis=0, keepdims=True)
        dn = dout * gate_v
        dgp_ref[...] += jnp.sum(dn * yhat, axis=0, keepdims=True)
        dyhat = dn * gp
        dy = (rstd * (dyhat - yhat * jnp.mean(dyhat * yhat, axis=1, keepdims=True))).astype(BF)
        gw_ref[...] += _dot_tn(u, dy)
        du = _dot(dy, wt_ref[...])
        dua, dub = du[:, :FOX_W], du[:, FOX_W:]
        doa = dua * sila
        for a in range(0, FOX_HEADS, 2):
            pair_t = doa[:, a * HEAD_DIM:(a + 2) * HEAD_DIM].T
            for hd, rows in ((a, slice(0, HEAD_DIM)), (a + 1, slice(HEAD_DIM, 2 * HEAD_DIM))):
                inv_l = il_ref[hd]
                doat_ref[hd] = (pair_t[rows] * inv_l).astype(BF)
                dla_ref[hd] = jnp.sum(pair_t[rows] * oat_ref[hd], axis=0, keepdims=True) * inv_l
        dob = dub * silb
        dob_ref[...] = dob.astype(BF)
        for a in range(0, SWA_W, LANES):
            dobt_ref[a:a + LANES, :] = dob[:, a:a + LANES].T.astype(BF)
        dza_ref[...] = (dua * oa_v * (sga * (1.0 + za_v * (1.0 - sga)))).astype(BF)
        dzb_ref[...] = (dub * ob_v * (sgb * (1.0 + zb_v * (1.0 - sgb)))).astype(BF)

    row = lambda w: pl.BlockSpec((tm, w), lambda i: (i, 0))
    heads_t = lambda w: pl.BlockSpec((FOX_HEADS, w, tm), lambda i: (0, 0, i))
    vec = _full((1, D_MODEL))
    mat = _full((D_MODEL, D_MODEL))
    out_shape = (
        jax.ShapeDtypeStruct((s, D_MODEL), F32),
        jax.ShapeDtypeStruct((FOX_HEADS, HEAD_DIM, s), BF), jax.ShapeDtypeStruct((FOX_HEADS, 1, s), F32),
        jax.ShapeDtypeStruct((s, FOX_W), BF), jax.ShapeDtypeStruct((s, SWA_W), BF), jax.ShapeDtypeStruct((SWA_W, s), BF),
        jax.ShapeDtypeStruct((s, SWA_W), BF),
        jax.ShapeDtypeStruct((D_MODEL, D_MODEL), F32),
        jax.ShapeDtypeStruct((1, D_MODEL), F32), jax.ShapeDtypeStruct((1, D_MODEL), F32),
        jax.ShapeDtypeStruct((1, 1), F32),
    )
    col = pl.BlockSpec((SWA_W, tm), lambda i: (0, i))
    return _call(
        body, name="out_proj", out_shape=out_shape, grid=(s // tm,),
        in_specs=[heads_t(HEAD_DIM), row(FOX_W), col, row(SWA_W), row(D_MODEL), row(D_MODEL), mat, mat, vec, vec,
                  heads_t(1)],
        out_specs=(row(D_MODEL), heads_t(HEAD_DIM), heads_t(1), row(FOX_W), row(SWA_W), col, row(SWA_W), mat, vec, vec,
                   _full((1, 1))),
        compiler_params=_params(("arbitrary",)),
    )(oat, za, obt, zb, x, tgt, w_out, w_out_t, gate, g_post, inv_l)


def _assemble_dproj(dqt, dkt, dvt, dza, dqb, dzb, dkb, dvb, df, cos_t, sin_t, tm):
    s = dza.shape[0]

    def body(dqt_ref, dkt_ref, dvt_ref, dza_ref, dqb_ref, dzb_ref, dkb_ref, dvb_ref, df_ref, cos_ref, sin_ref, o_ref):
        def cat(ref, n):
            return jnp.concatenate([ref[hd] for hd in range(n)], axis=1)

        cos2, sin2 = cos_ref[...], sin_ref[...]
        cos8 = jnp.concatenate([cos2] * 4, axis=1)
        sin8 = jnp.concatenate([sin2] * 4, axis=1)
        scale = HEAD_DIM ** -0.5
        o_ref[:, C_QA:C_QA + FOX_W] = (_pairs_to_rows(dqt_ref.at[:, 0]) * scale).astype(BF)
        o_ref[:, C_KA:C_KA + FOX_W] = (_pairs_to_rows(dkt_ref) * LN2).astype(BF)
        o_ref[:, C_VA:C_VA + FOX_W] = _pairs_to_rows(dvt_ref).astype(BF)
        o_ref[:, C_ZA:C_ZA + FOX_W] = dza_ref[...]
        dq = _blocks_to_rows(dqb_ref) * scale
        o_ref[:, C_QB:C_QB + SWA_W] = (dq * cos8 - _rope_partner(dq) * sin8).astype(BF)
        o_ref[:, C_ZB:C_ZB + SWA_W] = dzb_ref[...]
        dk = cat(dkb_ref, SWA_KV_HEADS)
        o_ref[:, C_KB:C_KB + SWA_KV_W] = (dk * cos2 - _rope_partner(dk) * sin2).astype(BF)
        o_ref[:, C_VB:C_VB + SWA_KV_W] = cat(dvb_ref, SWA_KV_HEADS).astype(BF)
        o_ref[:, C_F:C_F + LANES] = df_ref[...].astype(BF)

    row = lambda w: pl.BlockSpec((tm, w), lambda i: (i, 0))
    heads = lambda n: pl.BlockSpec((n, tm, HEAD_DIM), lambda i: (0, i, 0))
    heads_t = lambda w: pl.BlockSpec((FOX_HEADS, w, tm), lambda i: (0, 0, i))
    return _call(
        body, name="assemble_dproj", out_shape=jax.ShapeDtypeStruct((s, WP), BF), grid=(s // tm,),
        in_specs=[pl.BlockSpec((FOX_HEADS, 1, VT_ROWS, tm), lambda i: (0, i, 0, 0)), heads_t(VT_ROWS), heads_t(HEAD_DIM),
                  row(FOX_W), pl.BlockSpec((SWA_W, tm), lambda i: (0, i)), row(SWA_W), heads(SWA_KV_HEADS),
                  heads(SWA_KV_HEADS), row(LANES), row(LANES), row(LANES)],
        out_specs=row(WP), compiler_params=_params(("parallel",)),
    )(dqt, dkt, dvt, dza, dqb, dzb, dkb, dvb, df, cos_t, sin_t)


def _in_proj_bwd_x(dproj, w_al_t, x, dout, g_pre, scale1p, tm, parts):
    s = x.shape[0]
    n_steps = s // tm
    masks = [(1, 0), (0, 1), (1, 1)]

    def body(dp_ref, wt_ref, x_ref, dout_ref, g_ref, sc_ref, parts_ref, gx_ref, dsh_ref, dsc_ref, dg_ref, got_ref,
             send_sems, recv_sems, local_sem):
        i = pl.program_id(0)
        cx, cy, cc = lax.axis_index("x"), lax.axis_index("y"), lax.axis_index("c")
        me = 2 * cx + cy
        own = pltpu.make_async_copy(parts_ref.at[me], got_ref.at[me], local_sem)

        def copy(k, send):
            dx, dy = masks[k]
            peer = 2 * (cx ^ dx) + (cy ^ dy)
            return pltpu.make_async_remote_copy(
                src_ref=parts_ref.at[peer if send else me], dst_ref=got_ref.at[me if send else peer],
                send_sem=send_sems.at[k], recv_sem=recv_sems.at[k], device_id=(cx ^ dx, cy ^ dy, cc), device_id_type=MESH)

        @pl.when(i == 0)
        def _():
            dsh_ref[...] = jnp.zeros(dsh_ref.shape, F32)
            dsc_ref[...] = jnp.zeros(dsc_ref.shape, F32)
            dg_ref[...] = jnp.zeros(dg_ref.shape, F32)
            own.start()
            for k in range(len(masks)):
                copy(k, True).start()

        @pl.when(i == n_steps - 1)
        def _():
            for k in range(len(masks)):
                copy(k, False).wait_recv()
            for k in range(len(masks)):
                copy(k, True).wait_send()
            own.wait()

        dh = _dot(dp_ref[...], wt_ref[...])
        xhat, rstd = _rms_hat(x_ref[...])
        g, sc = g_ref[...], sc_ref[...]
        dsh_ref[...] += jnp.sum(dh, axis=0, keepdims=True)
        dhx = dh * xhat
        dsc_ref[...] += jnp.sum(dhx * g, axis=0, keepdims=True)
        dg_ref[...] += jnp.sum(dhx * sc, axis=0, keepdims=True)
        dxhat = dh * (g * sc)
        gx_ref[...] = dout_ref[...] + rstd * (dxhat - xhat * jnp.mean(dxhat * xhat, axis=1, keepdims=True))

    row = lambda w: pl.BlockSpec((tm, w), lambda i: (i, 0))
    vec = _full((1, D_MODEL))
    vshape = jax.ShapeDtypeStruct((1, D_MODEL), F32)
    hbm = pl.BlockSpec(memory_space=pl.ANY)
    return _call(
        body, name="in_proj_bwd_x",
        out_shape=(jax.ShapeDtypeStruct((s, D_MODEL), F32), vshape, vshape, vshape,
                   jax.ShapeDtypeStruct(parts.shape, parts.dtype)),
        grid=(n_steps,),
        in_specs=[row(WP), _full((WP, D_MODEL)), row(D_MODEL), row(D_MODEL), vec, vec, hbm],
        out_specs=(row(D_MODEL), vec, vec, vec, hbm),
        scratch_shapes=[pltpu.SemaphoreType.DMA((3,)), pltpu.SemaphoreType.DMA((3,)), pltpu.SemaphoreType.DMA],
        compiler_params=_params(("arbitrary",), has_side_effects=True),
    )(dproj, w_al_t, x, dout, g_pre, scale1p, parts)


def _in_proj_bwd_w(h, dproj, tk, tn):
    s = h.shape[0]
    n_k = s // tk

    def body(h_ref, dp_ref, gw_ref, acc_scr):
        k = pl.program_id(1)

        @pl.when(k == 0)
        def _():
            acc_scr[...] = jnp.zeros(acc_scr.shape, F32)

        acc_scr[...] += _dot_tn(h_ref[...], dp_ref[...])

        @pl.when(k == n_k - 1)
        def _():
            gw_ref[...] = acc_scr[...].astype(BF)

    return _call(
        body, name="in_proj_bwd_w", out_shape=jax.ShapeDtypeStruct((D_MODEL, WP), BF), grid=(WP // tn, n_k),
        in_specs=[pl.BlockSpec((tk, D_MODEL), lambda n, k: (k, 0)), pl.BlockSpec((tk, tn), lambda n, k: (k, n))],
        out_specs=pl.BlockSpec((D_MODEL, tn), lambda n, k: (0, n)),
        scratch_shapes=[pltpu.VMEM((D_MODEL, tn), F32)],
        compiler_params=_params(("parallel", "arbitrary")),
    )(h, dproj)


def _align_w_in(w_cols):
    def part(name, width):
        return w_cols[:, _SRC[name]:_SRC[name] + width]

    fpad = jnp.pad(part("fa", FOX_HEADS), ((0, 0), (0, LANES - FOX_HEADS)))
    return jnp.concatenate([part("qa", FOX_W), part("ka", FOX_W), part("va", FOX_W), part("za", FOX_W),
                            part("qb", SWA_W), part("zb", SWA_W), part("kb", SWA_KV_W), part("vb", SWA_KV_W), fpad], axis=1)


def _unalign_w_in(g_al):
    def part(c0, width):
        return g_al[:, c0:c0 + width]

    return jnp.concatenate([part(C_QA, FOX_W), part(C_KA, FOX_W), part(C_VA, FOX_W), part(C_F, FOX_HEADS),
                            part(C_ZA, FOX_W), part(C_QB, SWA_W), part(C_KB, SWA_KV_W), part(C_VB, SWA_KV_W),
                            part(C_ZB, SWA_W)], axis=1)


def _rope_tables(positions):
    inv_freq = ROPE_THETA ** (-jnp.arange(HALF, dtype=F32) / HALF)
    ang = positions.astype(F32)[:, None] * inv_freq
    cos, sin = jnp.cos(ang), jnp.sin(ang)
    return jnp.concatenate([cos, cos, cos, cos], axis=1), jnp.concatenate([-sin, sin, -sin, sin], axis=1)


def _tiles(s):
    if s >= 4096:
        return dict(tm=512, blk=512, bq=2048, bk=2048, bk_bwd=2048, chunk=256, tq=256, tm_out=512, tk=1024, tn=1152)
    return dict(tm=128, blk=128, bq=256, bk=256, bk_bwd=256, chunk=128, tq=128, tm_out=128, tk=128, tn=1152)


def kernel(x, c, positions, w_ada, b_ada, g_pre, w_in, b_fgate, sinks, w_out, g_post, loss_target, m_w_ada, m_b_ada, m_g_pre, m_w_in, m_b_fgate, m_sinks, m_w_out, m_g_post, v_w_ada, v_b_ada, v_g_pre, v_w_in, v_b_fgate, v_sinks, v_w_out, v_g_post):
    s = x.shape[1]
    t = _tiles(s)
    nc = s // LANES
    rows = FOX_HEADS * nc
    me = 4 * lax.axis_index("x") + 2 * lax.axis_index("y") + lax.axis_index("c")
    chip = 2 * lax.axis_index("x") + lax.axis_index("y")
    core = lax.axis_index("c")
    x2, tgt = x[0], loss_target[0]

    c_all = _allgather_devices(c, "gather_c")[:, 0, :]
    a_all, mod_shard = _ada_shard(c_all, w_ada[0])
    mod_all = _allgather_devices(mod_shard, "gather_mod")
    mod_rows = lax.dynamic_index_in_dim(mod_all, me, axis=1, keepdims=False)
    mod = mod_rows.reshape(N_CHIPS, 2, W_ADA_SHARD)[:, 0, :].reshape(1, 3 * D_MODEL) + b_ada
    shift, scale1p, gate = mod[:, :D_MODEL], 1.0 + mod[:, D_MODEL:2 * D_MODEL], mod[:, 2 * D_MODEL:]

    w_in_pad = jnp.pad(w_in[0].astype(BF), ((0, 0), (0, W_IN_SHARD_PAD - W_IN_SHARD)))
    w_pack = jnp.concatenate([w_in_pad, w_out[0].astype(BF).reshape(D_MODEL, W_OUT_SHARD)], axis=1)
    w_all = _allgather_chips(w_pack.reshape(2, D_MODEL // 2, -1), "gather_weights").reshape(N_CHIPS, D_MODEL, -1)
    w_cols = jnp.concatenate([w_all[k, :, :W_IN_SHARD] for k in range(N_CHIPS)], axis=1)
    w_al = _align_w_in(w_cols)
    w_al_t = w_al.T
    w_out_all = w_all[:, :, W_IN_SHARD_PAD:].reshape(D_MODEL, D_MODEL)
    w_out_t = w_out_all.T

    cos_t, sin_t = _rope_tables(positions[0])

    f_pad = _forget_logits(x2, g_pre, scale1p, shift, w_al[:, C_F:], t["tm"])
    f_rows = f_pad[:, :FOX_HEADS].T.reshape(rows, LANES)
    bias_rows = jnp.repeat(b_fgate[0], nc)[:, None]
    cum = _log_forget_cumsum(f_rows, bias_rows, nc).reshape(FOX_HEADS, s)
    h, qat, ka, kat, va, vat, za, zb, qb, kb, vb, qbt, kbt, vbt, m_own = _in_proj(
        x2, g_pre, scale1p, shift, w_al[:, C_VA:C_F], w_al_t[:C_ZA], cum, cos_t, sin_t, t["tm"])
    m_own = m_own[:, None, :]
    fox_args = (qat, ka, vat, m_own, t["bq"], t["bk"], t["chunk"])
    oat, lse, bad, pt = _fox_fwd(*fox_args, running_max=False)
    overflowed = jnp.max(bad) > 0.0
    oat, lse = lax.cond(overflowed, lambda: _fox_fwd(*fox_args, running_max=True)[:2], lambda: (oat, lse))
    inv_l = jnp.where(overflowed, 1.0, jnp.exp2(m_own - lse))
    sinks_g = sinks.reshape(SWA_KV_HEADS, 1, SWA_GROUP)
    obt = _swa_fwd(qbt, kb, vbt, sinks_g, t["tq"])

    dout, doat, delta_a, dza, dob, dobt, dzb, gw_out, dgate, dg_post, loss_part = _out_proj(
        oat, za, obt, zb, x2, tgt, w_out_all, w_out_t, gate, g_post, inv_l, t["tm_out"])

    bwd_args = (qat, ka, kat, va, doat, lse, delta_a)
    bwd_tiles = (t["bq"], t["bk_bwd"], t["chunk"], t["blk"])
    dqt, dkt, dvt = lax.cond(overflowed, lambda: _fox_bwd(*bwd_args, None, *bwd_tiles),
                             lambda: _fox_bwd(*bwd_args, pt, *bwd_tiles))
    dcum = dqt[:, :, HEAD_DIM, :].reshape(FOX_HEADS, s) - dkt[:, HEAD_DIM, :]
    df_rows, db_heads = _log_forget_cumsum_bwd(dcum.reshape(rows, LANES), f_rows, bias_rows, nc)
    df_pad = jnp.pad(df_rows.reshape(FOX_HEADS, s).T, ((0, 0), (0, LANES - FOX_HEADS)))
    dqb, dkb, dvb, dsinks = _swa_bwd(qb, qbt, kb, kbt, vb, sinks_g, dob, dobt, t["tq"])

    dproj = _assemble_dproj(dqt, dkt, dvt, dza, dqb, dzb, dkb, dvb, df_pad, cos_t, sin_t, t["blk"])
    gw_in = _unalign_w_in(_in_proj_bwd_w(h, dproj, t["tk"], t["tn"]))

    gin = jnp.pad(gw_in.reshape(D_MODEL, N_CHIPS, W_IN_SHARD).transpose(1, 0, 2),
                  ((0, 0), (0, 0), (0, W_IN_SHARD_PAD - W_IN_SHARD)))
    gout = gw_out.astype(BF).reshape(N_CHIPS, D_MODEL, W_OUT_SHARD)
    gbig = jnp.concatenate([gin, gout], axis=2)
    half = D_MODEL // 2
    gw = W_IN_SHARD_PAD + W_OUT_SHARD
    keep = lax.dynamic_slice_in_dim(gbig, core * half, half, axis=1)
    give = lax.dynamic_slice_in_dim(gbig, (1 - core) * half, half, axis=1)
    got = _swap_sibling(give.reshape(N_CHIPS * half, gw), "swap_grad_halves")
    pair = _add(keep.reshape(N_CHIPS * half, gw), got, "add_pair", BF).reshape(N_CHIPS, half, gw)
    grad_x, dshift, dscale, dg_pre, from_chips = _in_proj_bwd_x(
        dproj, w_al_t, x2, dout, g_pre, scale1p, t["tm_out"], pair)

    pad_lane = lambda vrow: jnp.pad(vrow, ((0, 0), (0, LANES - vrow.shape[1])))
    packed = jnp.concatenate([dshift, dscale, dgate, dg_pre, dg_post,
                              pad_lane(db_heads.reshape(1, FOX_HEADS)), pad_lane(dsinks.reshape(1, FOX_HEADS)),
                              pad_lane(loss_part)], axis=1)
    parts = _allgather_devices(packed, "gather_partials")
    tot = _sum_devices(parts)
    loss = tot[0, P_LOSS]
    g_b_ada = tot[:, P_DMOD:P_DMOD + 3 * D_MODEL]
    g_g_pre = tot[:, P_GPRE:P_GPRE + D_MODEL]
    g_g_post = tot[:, P_GPOST:P_GPOST + D_MODEL]
    g_b_fgate = tot[:, P_BF:P_BF + FOX_HEADS]
    g_sinks = tot[:, P_SINK:P_SINK + FOX_HEADS]
    dm_shard = lax.dynamic_slice_in_dim(parts[:, 0, :3 * D_MODEL], chip * W_ADA_SHARD, W_ADA_SHARD, axis=1)
    g_w_ada = _grad_w_ada(a_all.T, dm_shard)

    mine = _sum_chips(from_chips, "sum_chips")
    other = _swap_sibling(mine, "swap_grad_result")
    lo = jnp.where(core == 0, mine, other)
    hi = jnp.where(core == 0, other, mine)
    gfull = jnp.concatenate([lo, hi], axis=0)
    g_w_in = gfull[:, :W_IN_SHARD]
    g_w_out = gfull[:, W_IN_SHARD_PAD:].reshape(W_OUT_SHARD, D_MODEL)

    grads = dict(w_ada=g_w_ada, b_ada=g_b_ada, g_pre=g_g_pre, w_in=g_w_in, b_fgate=g_b_fgate, sinks=g_sinks,
                 w_out=g_w_out, g_post=g_g_post)
    weights = dict(w_ada=w_ada, b_ada=b_ada, g_pre=g_pre, w_in=w_in, b_fgate=b_fgate, sinks=sinks, w_out=w_out, g_post=g_post)
    moms = dict(w_ada=m_w_ada, b_ada=m_b_ada, g_pre=m_g_pre, w_in=m_w_in, b_fgate=m_b_fgate, sinks=m_sinks, w_out=m_w_out, g_post=m_g_post)
    vars_ = dict(w_ada=v_w_ada, b_ada=v_b_ada, g_pre=v_g_pre, w_in=v_w_in, b_fgate=v_b_fgate, sinks=v_sinks, w_out=v_w_out, g_post=v_g_post)
    names = ["w_ada", "b_ada", "g_pre", "w_in", "b_fgate", "sinks", "w_out", "g_post"]
    g_out, d_out, m_out, v_out = [], [], [], []
    for n in names:
        if n == "w_in":
            flat = lambda a: jnp.transpose(a, (2, 0, 1)).reshape(W_IN_SHARD * D_MODEL // LANES, LANES)
            unflat = lambda a: jnp.transpose(a.reshape(W_IN_SHARD, 1, D_MODEL), (1, 2, 0))
            outs = _adamw(flat(w_in), flat(grads[n][None]), flat(moms[n]), flat(vars_[n]), "adamw_" + n)
            go, d, nm, nv = (unflat(a) for a in outs)
        else:
            g2 = grads[n].reshape(weights[n].shape[-2:])
            go, d, nm, nv = _adamw(weights[n], g2, moms[n], vars_[n], "adamw_" + n)
        g_out.append(go)
        d_out.append(d)
        m_out.append(nm)
        v_out.append(nv)
    return (loss, grad_x.reshape(x.shape), *g_out, *d_out, *m_out, *v_out)
```

```python
import functools

import jax
import jax.numpy as jnp
from jax import lax
from jax.experimental import pallas as pl
from jax.experimental.pallas import tpu as pltpu

_INTERPRET = False

D_MODEL = 1024
HEAD_DIM = 64
HALF = HEAD_DIM // 2
AUG_DIM = 128
AUG_ROWS = 8
VT_ROWS = 80
LOG2E = 1.4426950408889634
LN2 = 0.6931471805599453
Q_SCALE = LOG2E * 64 ** -0.5
FOX_HEADS = 8
FOX_W = 512
SWA_W = 512
SWA_KV_HEADS = 2
SWA_GROUP = 4
SWA_KV_W = 128
WINDOW = 128
ROPE_THETA = 10000.0
RMS_EPS = 1e-6
IN_WIDTH = 3336
N_CHIPS = 4
N_DEV = 8
W_IN_SHARD = IN_WIDTH // N_CHIPS
W_IN_SHARD_PAD = 896
W_ADA_SHARD = 3 * D_MODEL // N_CHIPS
W_OUT_SHARD = D_MODEL // N_CHIPS
LANES = 128

_SRC = dict(qa=0, ka=512, va=1024, fa=1536, za=1544, qb=2056, kb=2568, vb=2696, zb=2824)
C_QA, C_KA, C_VA, C_ZA, C_QB, C_ZB, C_KB, C_VB, C_F = 0, 512, 1024, 1536, 2048, 2560, 3072, 3200, 3328
WP = 3456

ADAM_LR = 0.001
ADAM_B1 = 0.9
ADAM_B2 = 0.999
ADAM_EPS = 1e-08
ADAM_WD = 0.01
ADAM_STEP = 10
ADAMW_BLOCK_ELEMS = 300_000

VMEM_LIMIT = 56 * 1024 * 1024
NEG = -1e30
OVERFLOW_GUARD = 1e30
MESH = pl.DeviceIdType.MESH
BF = jnp.bfloat16
F32 = jnp.float32

P_DMOD, P_GPRE, P_GPOST, P_BF, P_SINK, P_LOSS, P_LEN = 0, 3072, 4096, 5120, 5248, 5376, 5504


def _call(body, **kw):
    return pl.pallas_call(body, interpret=_INTERPRET, **kw)


def _params(sem=None, **kw):
    return pltpu.CompilerParams(dimension_semantics=sem, vmem_limit_bytes=VMEM_LIMIT, **kw)


def _full(shape):
    zeros = (0,) * len(shape)
    return pl.BlockSpec(shape, lambda *_: zeros)


def _dot(a, b):
    return jnp.dot(a, b, preferred_element_type=F32)


def _dot_nt(a, b):
    return lax.dot_general(a, b, (((1,), (1,)), ((), ())), preferred_element_type=F32)


def _dot_tn(a, b):
    return lax.dot_general(a, b, (((0,), (0,)), ((), ())), preferred_element_type=F32)


def _sigmoid(z):
    return 1.0 / (1.0 + jnp.exp(-z))


def _rope_partner(t):
    w = t.shape[-1]
    lane = lax.broadcasted_iota(jnp.int32, t.shape, t.ndim - 1)
    return jnp.where((lane & (HEAD_DIM - 1)) < HALF, pltpu.roll(t, w - HALF, t.ndim - 1), pltpu.roll(t, HALF, t.ndim - 1))


def _allgather_devices(v, name):
    r, cdim = v.shape
    masks = [(dx, dy, dc) for dx in (0, 1) for dy in (0, 1) for dc in (0, 1)][1:]

    def body(v_ref, out_ref, send_sems, recv_sems):
        x, y, c = lax.axis_index("x"), lax.axis_index("y"), lax.axis_index("c")
        me = 4 * x + 2 * y + c
        out_ref[me] = v_ref[...]
        copies = []
        for k, (dx, dy, dc) in enumerate(masks):
            cp = pltpu.make_async_remote_copy(
                src_ref=v_ref, dst_ref=out_ref.at[me], send_sem=send_sems.at[k], recv_sem=recv_sems.at[k],
                device_id=(x ^ dx, y ^ dy, c ^ dc), device_id_type=MESH)
            cp.start()
            copies.append(cp)
        for k, (dx, dy, dc) in enumerate(masks):
            peer = 4 * (x ^ dx) + 2 * (y ^ dy) + (c ^ dc)
            pltpu.make_async_remote_copy(
                src_ref=v_ref, dst_ref=out_ref.at[peer], send_sem=send_sems.at[k], recv_sem=recv_sems.at[k],
                device_id=(x ^ dx, y ^ dy, c ^ dc), device_id_type=MESH).wait_recv()
        for cp in copies:
            cp.wait_send()

    return _call(
        body, name=name, out_shape=jax.ShapeDtypeStruct((N_DEV, r, cdim), v.dtype),
        in_specs=[pl.BlockSpec(memory_space=pltpu.VMEM)], out_specs=pl.BlockSpec(memory_space=pltpu.VMEM),
        scratch_shapes=[pltpu.SemaphoreType.DMA((7,)), pltpu.SemaphoreType.DMA((7,))],
        compiler_params=pltpu.CompilerParams(has_side_effects=True),
    )(v)


def _allgather_chips(v, name):
    _, r, cdim = v.shape
    masks = [(1, 0), (0, 1), (1, 1)]
    n = len(masks)

    def body(v_ref, out_ref, send_sems, recv_sems, local_sem):
        x, y, c = lax.axis_index("x"), lax.axis_index("y"), lax.axis_index("c")
        me = 2 * x + y
        mine = pltpu.make_async_copy(v_ref, out_ref.at[me], local_sem)
        mine.start()

        def copy(k, chip, half, to):
            return pltpu.make_async_remote_copy(
                src_ref=v_ref.at[half] if k < n else out_ref.at[chip, half], dst_ref=out_ref.at[chip, half],
                send_sem=send_sems.at[k], recv_sem=recv_sems.at[k], device_id=to, device_id_type=MESH)

        first = [copy(k, me, c, (x ^ dx, y ^ dy, c)) for k, (dx, dy) in enumerate(masks)]
        for cp in first:
            cp.start()
        passed = []
        for k, (dx, dy) in enumerate(masks):
            peer = 2 * (x ^ dx) + (y ^ dy)
            copy(k, peer, c, (x, y, c)).wait_recv()
            cp = copy(n + k, peer, c, (x, y, 1 - c))
            cp.start()
            passed.append(cp)
        for k, (dx, dy) in enumerate(masks):
            copy(n + k, 2 * (x ^ dx) + (y ^ dy), 1 - c, (x, y, c)).wait_recv()
        for cp in first + passed:
            cp.wait_send()
        mine.wait()

    return _call(
        body, name=name, out_shape=jax.ShapeDtypeStruct((N_CHIPS, 2, r, cdim), v.dtype),
        in_specs=[pl.BlockSpec(memory_space=pl.ANY)], out_specs=pl.BlockSpec(memory_space=pl.ANY),
        scratch_shapes=[pltpu.SemaphoreType.DMA((2 * n,)), pltpu.SemaphoreType.DMA((2 * n,)), pltpu.SemaphoreType.DMA],
        compiler_params=pltpu.CompilerParams(has_side_effects=True),
    )(v)


def _swap_sibling(v, name):
    def body(v_ref, out_ref, send_sem, recv_sem):
        x, y, c = lax.axis_index("x"), lax.axis_index("y"), lax.axis_index("c")
        cp = pltpu.make_async_remote_copy(
            src_ref=v_ref, dst_ref=out_ref, send_sem=send_sem, recv_sem=recv_sem,
            device_id=(x, y, 1 - c), device_id_type=MESH)
        cp.start()
        cp.wait()

    return _call(
        body, name=name, out_shape=jax.ShapeDtypeStruct(v.shape, v.dtype),
        in_specs=[pl.BlockSpec(memory_space=pl.ANY)], out_specs=pl.BlockSpec(memory_space=pl.ANY),
        scratch_shapes=[pltpu.SemaphoreType.DMA, pltpu.SemaphoreType.DMA],
        compiler_params=pltpu.CompilerParams(has_side_effects=True),
    )(v)


def _ada_shard(c_all, w_ada_shard):
    def body(c_ref, w_ref, a_ref, mod_ref):
        cv = c_ref[...]
        a = cv * _sigmoid(cv)
        a_ref[...] = a
        mod_ref[...] = _dot(a.astype(BF), w_ref[...].astype(BF))

    return _call(
        body, name="ada_shard",
        out_shape=(jax.ShapeDtypeStruct((N_DEV, D_MODEL), F32), jax.ShapeDtypeStruct((N_DEV, W_ADA_SHARD), F32)),
        compiler_params=_params(),
    )(c_all, w_ada_shard)


def _grad_w_ada(a_t, dm_shard):
    def body(a_ref, dm_ref, out_ref):
        acc = jnp.zeros((D_MODEL, W_ADA_SHARD), F32)
        for b in range(N_DEV):
            acc = acc + a_ref[:, b:b + 1] * dm_ref[b:b + 1, :]
        out_ref[...] = acc

    return _call(body, name="grad_w_ada", out_shape=jax.ShapeDtypeStruct((D_MODEL, W_ADA_SHARD), F32),
                 compiler_params=_params())(a_t, dm_shard)


def _sum_devices(parts):
    n = parts.shape[-1]

    def body(p_ref, out_ref):
        acc = p_ref[0]
        for b in range(1, N_DEV):
            acc = acc + p_ref[b]
        out_ref[...] = acc

    return _call(body, name="sum_devices", out_shape=jax.ShapeDtypeStruct((1, n), F32), compiler_params=_params())(parts)


def _add(a, b, name, out_dtype):
    r, cdim = a.shape
    tr = min(r, 256)

    def body(a_ref, b_ref, o_ref):
        o_ref[...] = (a_ref[...].astype(F32) + b_ref[...].astype(F32)).astype(out_dtype)

    spec = pl.BlockSpec((tr, cdim), lambda i: (i, 0))
    return _call(body, name=name, out_shape=jax.ShapeDtypeStruct(a.shape, out_dtype), grid=(r // tr,),
                 in_specs=[spec, spec], out_specs=spec, compiler_params=_params(("parallel",)))(a, b)


def _sum_chips(parts, name):
    _, r, cdim = parts.shape
    tr = min(r, 128)

    def body(p_ref, o_ref):
        o_ref[...] = ((p_ref[0].astype(F32) + p_ref[1].astype(F32)) + p_ref[2].astype(F32)) + p_ref[3].astype(F32)

    return _call(body, name=name, out_shape=jax.ShapeDtypeStruct((r, cdim), F32), grid=(r // tr,),
                 in_specs=[pl.BlockSpec((N_CHIPS, tr, cdim), lambda i: (0, i, 0))],
                 out_specs=pl.BlockSpec((tr, cdim), lambda i: (i, 0)), compiler_params=_params(("parallel",)))(parts)


def _adamw(w, g, m, v, name):
    r, cdim = w.shape[-2:]
    lead = w.ndim - 2
    tr = r if r <= 256 else max(t for t in range(8, ADAMW_BLOCK_ELEMS // cdim + 1, 8) if r % t == 0)
    c1 = 1.0 / (1.0 - ADAM_B1 ** ADAM_STEP)
    c2 = 1.0 / (1.0 - ADAM_B2 ** ADAM_STEP)

    def body(w_ref, g_ref, m_ref, v_ref, go_ref, d_ref, nm_ref, nv_ref):
        gv = g_ref[...].reshape(go_ref.shape)
        nm = ADAM_B1 * m_ref[...] + (1.0 - ADAM_B1) * gv
        nv = ADAM_B2 * v_ref[...] + (1.0 - ADAM_B2) * (gv * gv)
        m_hat = nm * c1
        v_hat = nv * c2
        go_ref[...] = gv
        d_ref[...] = -ADAM_LR * (m_hat / (jnp.sqrt(v_hat) + ADAM_EPS) + ADAM_WD * w_ref[...])
        nm_ref[...] = nm
        nv_ref[...] = nv

    spec = pl.BlockSpec((1,) * lead + (tr, cdim), lambda i: (0,) * lead + (i, 0))
    shp = jax.ShapeDtypeStruct(w.shape, F32)
    return _call(body, name=name, out_shape=(shp,) * 4, grid=(r // tr,),
                 in_specs=[spec, pl.BlockSpec((tr, cdim), lambda i: (i, 0)), spec, spec],
                 out_specs=(spec,) * 4, compiler_params=_params(("parallel",)))(w, g, m, v)


def _head_of_row(r, nc):
    assert nc & (nc - 1) == 0
    return lax.shift_right_logical(r, nc.bit_length() - 1)


def _chunk_mats(rows, nc, reverse):
    ri = lax.broadcasted_iota(jnp.int32, (rows, rows), 0)
    ci = lax.broadcasted_iota(jnp.int32, (rows, rows), 1)
    same = _head_of_row(ri, nc) == _head_of_row(ci, nc)
    between = jnp.where(same & ((ci > ri) if reverse else (ci < ri)), 1.0, 0.0).astype(F32)
    li = lax.broadcasted_iota(jnp.int32, (LANES, LANES), 0)
    lj = lax.broadcasted_iota(jnp.int32, (LANES, LANES), 1)
    within = jnp.where((li >= lj) if reverse else (li <= lj), 1.0, 0.0).astype(F32)
    return between, within


def _dot_hi(a, b):
    return jnp.dot(a, b, preferred_element_type=F32, precision=lax.Precision.HIGHEST)


def _scan_rows(t, nc, reverse):
    between, within = _chunk_mats(t.shape[0], nc, reverse)
    inner = _dot_hi(t, within)
    tot = jnp.sum(t, axis=1, keepdims=True)
    return inner + _dot_hi(between, jnp.broadcast_to(tot, t.shape))


def _log_forget_cumsum(f_rows, bias_rows, nc):
    def body(f_ref, b_ref, cum_ref):
        z = f_ref[...] + b_ref[...]
        lf = jnp.minimum(z, 0.0) - jnp.log(1.0 + jnp.exp(-jnp.abs(z)))
        cum_ref[...] = _scan_rows(lf, nc, False)

    return _call(body, name="forget_cumsum", out_shape=jax.ShapeDtypeStruct(f_rows.shape, F32),
                 compiler_params=_params())(f_rows, bias_rows)


def _log_forget_cumsum_bwd(dcum_rows, f_rows, bias_rows, nc):
    rows = f_rows.shape[0]

    def body(d_ref, f_ref, b_ref, df_ref, db_ref):
        dlf = _scan_rows(d_ref[...], nc, True)
        z = f_ref[...] + b_ref[...]
        df = dlf * _sigmoid(-z)
        df_ref[...] = df
        hi = lax.broadcasted_iota(jnp.int32, (FOX_HEADS, rows), 0)
        ri = lax.broadcasted_iota(jnp.int32, (FOX_HEADS, rows), 1)
        sel = jnp.where(_head_of_row(ri, nc) == hi, 1.0, 0.0).astype(F32)
        db_ref[...] = jnp.sum(_dot_hi(sel, df), axis=1, keepdims=True)

    return _call(body, name="forget_cumsum_bwd",
                 out_shape=(jax.ShapeDtypeStruct(f_rows.shape, F32), jax.ShapeDtypeStruct((FOX_HEADS, 1), F32)),
                 compiler_params=_params())(dcum_rows, f_rows, bias_rows)


def _rms_hat(xv):
    rstd = lax.rsqrt(jnp.mean(xv * xv, axis=-1, keepdims=True) + RMS_EPS)
    return xv * rstd, rstd


def _modulated(x_ref, g_ref, sc_ref, sh_ref):
    xhat, _ = _rms_hat(x_ref[...])
    return ((xhat * g_ref[...]) * sc_ref[...] + sh_ref[...]).astype(BF)


def _forget_logits(x, g_pre, scale1p, shift, w_f, tm):
    s = x.shape[0]

    def body(x_ref, g_ref, sc_ref, sh_ref, w_ref, f_ref):
        f_ref[...] = _dot(_modulated(x_ref, g_ref, sc_ref, sh_ref), w_ref[...])

    vec = _full((1, D_MODEL))
    return _call(
        body, name="forget_logits", out_shape=jax.ShapeDtypeStruct((s, LANES), F32), grid=(s // tm,),
        in_specs=[pl.BlockSpec((tm, D_MODEL), lambda i: (i, 0)), vec, vec, vec, _full((D_MODEL, LANES))],
        out_specs=pl.BlockSpec((tm, LANES), lambda i: (i, 0)), compiler_params=_params(("parallel",)),
    )(x, g_pre, scale1p, shift, w_f)


def _split3(v):
    hi = v.astype(BF).astype(F32)
    mid = (v - hi).astype(BF).astype(F32)
    lo = ((v - hi) - mid).astype(BF).astype(F32)
    return hi, mid, lo


def _in_proj(x, g_pre, scale1p, shift, w_rows, w_t_fox, cum, cos_t, sin_t, tm):
    s = x.shape[0]
    r_va, r_za, r_qb, r_zb, r_kb, r_vb = 0, 512, 1024, 1536, 2048, 2176

    def body(x_ref, g_ref, sc_ref, sh_ref, w_ref, wt_ref, cum_ref, cos_ref, sin_ref,
             h_ref, qat_ref, ka_ref, kat_ref, v_ref, vt_ref, za_ref, zb_ref, qb_ref, kb_ref, vb_ref,
             qbt_ref, kbt_ref, vbt_ref, mo_ref):
        hb = _modulated(x_ref, g_ref, sc_ref, sh_ref)
        h_ref[...] = hb

        def sec(c0, width):
            return _dot(hb, w_ref[:, c0:c0 + width])

        def sec_t(r0):
            return _dot_nt(wt_ref[r0:r0 + FOX_W, :], hb)

        q_t = sec_t(0) * Q_SCALE
        k_t = sec_t(FOX_W)
        v_t = sec_t(2 * FOX_W)
        va = sec(r_va, FOX_W)
        zeros = jnp.zeros((AUG_DIM - HEAD_DIM - AUG_ROWS, tm), F32)
        ri = lax.broadcasted_iota(jnp.int32, (AUG_ROWS, tm), 0)
        const = jnp.where(ri == AUG_ROWS - 1, 0.0, 1.0)
        ri_v = lax.broadcasted_iota(jnp.int32, (VT_ROWS - HEAD_DIM, tm), 0)
        v_feat = jnp.where(ri_v == 0, 1.0, 0.0).astype(BF)
        for hd in range(FOX_HEADS):
            rows = slice(hd * HEAD_DIM, (hd + 1) * HEAD_DIM)
            cum2 = cum_ref[hd:hd + 1, :] * LOG2E
            hi, mid, lo = (jnp.broadcast_to(part, (AUG_ROWS, tm)) for part in _split3(cum2))
            q_feat = jnp.where(ri == 1, hi, jnp.where(ri == 2, mid, jnp.where(ri == 3, lo, const)))
            k_feat = jnp.where(ri == 4, -hi, jnp.where(ri == 5, -mid, jnp.where(ri == 6, -lo, const)))
            q_aug = jnp.concatenate([q_t[rows], q_feat, zeros], axis=0)
            k_aug = jnp.concatenate([k_t[rows], k_feat, zeros], axis=0)
            mo_ref[hd:hd + 1, :] = jnp.sum(q_t[rows] * k_t[rows], axis=0, keepdims=True) + 1.0
            qat_ref[hd] = q_aug.astype(BF)
            kat_ref[hd] = k_aug.astype(BF)
            ka_ref[hd] = k_aug.T.astype(BF)
            vt_ref[hd] = jnp.concatenate([v_t[rows].astype(BF), v_feat], axis=0)
            v_ref[hd] = va[:, rows].astype(BF)
        za_ref[...] = sec(r_za, FOX_W)
        zb_ref[...] = sec(r_zb, SWA_W)
        cos2, sin2 = cos_ref[...], sin_ref[...]
        cos8 = jnp.concatenate([cos2] * 4, axis=1)
        sin8 = jnp.concatenate([sin2] * 4, axis=1)
        qb = sec(r_qb, SWA_W)
        qb = (qb * cos8 + _rope_partner(qb) * sin8) * (HEAD_DIM ** -0.5)
        qb_ref[...] = qb.astype(BF)
        for a in range(SWA_W // LANES):
            qbt_ref[a * LANES:(a + 1) * LANES, :] = qb[:, a * LANES:(a + 1) * LANES].T.astype(BF)
        kb = sec(r_kb, SWA_KV_W)
        kb = kb * cos2 + _rope_partner(kb) * sin2
        vb = sec(r_vb, SWA_KV_W)
        kb_t, vb_t = kb.T, vb.T
        for hd in range(SWA_KV_HEADS):
            sl = slice(hd * HEAD_DIM, (hd + 1) * HEAD_DIM)
            kb_ref[hd] = kb[:, sl].astype(BF)
            vb_ref[hd] = vb[:, sl].astype(BF)
            kbt_ref[hd] = kb_t[sl].astype(BF)
            vbt_ref[hd] = jnp.concatenate([vb_t[sl].astype(BF), v_feat], axis=0)

    row = lambda w: pl.BlockSpec((tm, w), lambda i: (i, 0))
    heads = lambda n, w=HEAD_DIM: pl.BlockSpec((n, tm, w), lambda i: (0, i, 0))
    heads_t = lambda w: pl.BlockSpec((FOX_HEADS, w, tm), lambda i: (0, 0, i))
    vec = _full((1, D_MODEL))
    hs = lambda a, b: jax.ShapeDtypeStruct((FOX_HEADS, a, b), BF)
    out_shape = (
        jax.ShapeDtypeStruct((s, D_MODEL), BF),
        hs(AUG_DIM, s), hs(s, AUG_DIM), hs(AUG_DIM, s), hs(s, HEAD_DIM), hs(VT_ROWS, s),
        jax.ShapeDtypeStruct((s, FOX_W), F32), jax.ShapeDtypeStruct((s, SWA_W), F32),
        jax.ShapeDtypeStruct((s, SWA_W), BF),
        jax.ShapeDtypeStruct((SWA_KV_HEADS, s, HEAD_DIM), BF), jax.ShapeDtypeStruct((SWA_KV_HEADS, s, HEAD_DIM), BF),
        jax.ShapeDtypeStruct((SWA_W, s), BF),
        jax.ShapeDtypeStruct((SWA_KV_HEADS, HEAD_DIM, s), BF), jax.ShapeDtypeStruct((SWA_KV_HEADS, VT_ROWS, s), BF),
        jax.ShapeDtypeStruct((FOX_HEADS, s), F32),
    )
    kv_t = lambda w: pl.BlockSpec((SWA_KV_HEADS, w, tm), lambda i: (0, 0, i))
    return _call(
        body, name="in_proj", out_shape=out_shape, grid=(s // tm,),
        in_specs=[row(D_MODEL), vec, vec, vec, _full(w_rows.shape), _full(w_t_fox.shape),
                  pl.BlockSpec((FOX_HEADS, tm), lambda i: (0, i)), row(LANES), row(LANES)],
        out_specs=(row(D_MODEL), heads_t(AUG_DIM), heads(FOX_HEADS, AUG_DIM), heads_t(AUG_DIM), heads(FOX_HEADS),
                   heads_t(VT_ROWS), row(FOX_W), row(SWA_W), row(SWA_W), heads(SWA_KV_HEADS), heads(SWA_KV_HEADS),
                   pl.BlockSpec((SWA_W, tm), lambda i: (0, i)), kv_t(HEAD_DIM), kv_t(VT_ROWS),
                   pl.BlockSpec((FOX_HEADS, tm), lambda i: (0, i))),
        compiler_params=_params(("parallel",)),
    )(x, g_pre, scale1p, shift, w_rows, w_t_fox, cum, cos_t, sin_t)


def _diag_chunks(d, bq, bk, chunk):
    out = []
    for c0 in range(0, bq, chunk):
        if d is None or d * bk + bk - 1 <= c0:
            out.append((c0, None, bk))
        elif d * bk <= c0 + chunk - 1:
            n_keys = min(bk, c0 + chunk - d * bk)
            kpos = d * bk + lax.broadcasted_iota(jnp.int32, (n_keys, chunk), 0)
            qpos = c0 + lax.broadcasted_iota(jnp.int32, (n_keys, chunk), 1)
            out.append((c0, kpos <= qpos, n_keys))
    return out


def _fox_fwd(qat, ka, vt, m_own, bq, bk, chunk, running_max):
    nh, _, s = qat.shape
    r = bq // bk

    pairs = [(i, j) for i in range(s // bq) for j in range(i * r + r)]

    def body(i_tab, j_tab, ka_ref, qat_ref, vt_ref, mo_ref, o_ref, lse_ref, bad_ref, *rest):
        pt_ref, m_scr, acc_scr = (None,) * running_max + rest
        i, j = i_tab[pl.program_id(1)], j_tab[pl.program_id(1)]

        @pl.when(j == 0)
        def _():
            m_scr[...] = jnp.full(m_scr.shape, NEG, F32) if running_max else mo_ref[0]
            acc_scr[...] = jnp.zeros(acc_scr.shape, F32)

        def careful(d):
            kv, vtv = ka_ref[0], vt_ref[0]

            def one_chunk(n, carry):
                c0 = pl.multiple_of(n * chunk, chunk)
                cs = pl.ds(c0, chunk)
                sc = _dot(kv, qat_ref[0, :, cs])
                if d is not None:
                    kpos = d * bk + lax.broadcasted_iota(jnp.int32, (bk, chunk), 0)
                    qpos = c0 + lax.broadcasted_iota(jnp.int32, (bk, chunk), 1)
                    sc = jnp.where(kpos <= qpos, sc, NEG)
                m_prev = m_scr[:, cs]
                m_new = jnp.maximum(m_prev, jnp.max(sc, axis=0, keepdims=True))
                p = jnp.exp2(sc - m_new).astype(BF)
                acc_scr[:, cs] = jnp.exp2(m_prev - m_new) * acc_scr[:, cs] + _dot(vtv, p)
                m_scr[:, cs] = m_new
                return carry

            lax.fori_loop(0, bq // chunk, one_chunk, 0)

        def fast(d):
            todo = _diag_chunks(d, bq, bk, chunk)
            scores = lambda t: _dot(ka_ref[0, :t[2], :], qat_ref[0, :, t[0]:t[0] + chunk])
            sc_next = scores(todo[0])
            for n, (c0, mask, n_keys) in enumerate(todo):
                cs = slice(c0, c0 + chunk)
                sc = sc_next
                if n + 1 < len(todo):
                    sc_next = scores(todo[n + 1])
                if mask is not None:
                    sc = jnp.where(mask, sc, NEG)
                p = jnp.exp2(sc - m_scr[:, cs]).astype(BF)
                pt_ref[0, :n_keys, cs] = p
                acc_scr[:, cs] += _dot(vt_ref[0, :, :n_keys], p)

        step = careful if running_max else fast

        @pl.when(j < i * r)
        def _():
            step(None)

        for d in range(r):
            @pl.when(j == i * r + d)
            def _(d=d):
                step(d)

        @pl.when(j == i * r + r - 1)
        def _():
            l = acc_scr[HEAD_DIM:HEAD_DIM + 1, :]
            o_ref[0] = acc_scr[:HEAD_DIM, :] / l
            lse_ref[0] = m_scr[...] + jnp.log2(l)
            bad_ref[0] = jnp.where(l < OVERFLOW_GUARD, 0.0, 1.0)

    qmap_t = lambda h, t, it, jt: (h, 0, it[t])
    qrow = pl.BlockSpec((1, 1, bq), qmap_t)
    row_shape = jax.ShapeDtypeStruct((nh, 1, s), F32)
    out_shape = (jax.ShapeDtypeStruct((nh, HEAD_DIM, s), F32), row_shape, row_shape)
    out_specs = (pl.BlockSpec((1, HEAD_DIM, bq), qmap_t), qrow, qrow)
    if not running_max:
        out_shape += (jax.ShapeDtypeStruct((nh, s, s), BF),)
        out_specs += (pl.BlockSpec((1, bk, bq), lambda h, t, it, jt: (h, jt[t], it[t])),)
    grid_spec = pltpu.PrefetchScalarGridSpec(
        num_scalar_prefetch=2, grid=(nh, len(pairs)),
        in_specs=[pl.BlockSpec((1, bk, AUG_DIM), lambda h, t, it, jt: (h, jt[t], 0)), pl.BlockSpec((1, AUG_DIM, bq), qmap_t),
                  pl.BlockSpec((1, VT_ROWS, bk), lambda h, t, it, jt: (h, 0, jt[t])), qrow],
        out_specs=out_specs,
        scratch_shapes=[pltpu.VMEM((1, bq), F32), pltpu.VMEM((VT_ROWS, bq), F32)])
    return _call(
        body, name="fox_fwd_running_max" if running_max else "fox_fwd", out_shape=out_shape, grid_spec=grid_spec,
        compiler_params=_params(("parallel", "arbitrary")),
    )(jnp.asarray([p[0] for p in pairs], jnp.int32), jnp.asarray([p[1] for p in pairs], jnp.int32), ka, qat, vt, m_own)


def _fox_bwd(qat, ka, kat, v, dot_, lse, delta, pt, bq, bk, chunk, dq_blk):
    nh, _, s = qat.shape
    r = bq // bk
    nq = s // bq
    stored = pt is not None

    pairs = [(j, i) for j in range(s // bk) for i in range(j // r, nq)]

    def body(j_tab, i_tab, a_ref, b_ref, kat_ref, v_ref, qat_ref, do_ref, dl_ref, dq_ref, dk_ref, dv_ref, dk_scr, dv_scr):
        ka_ref, lse_ref, pt_ref = (None, None, a_ref) if stored else (a_ref, b_ref, None)
        j, i = j_tab[pl.program_id(1)], i_tab[pl.program_id(1)]

        @pl.when(pl.program_id(1) == 0)
        def _():
            dq_ref[...] = jnp.zeros(dq_ref.shape, F32)

        @pl.when(i * r <= j)
        def _():
            dk_scr[...] = jnp.zeros(dk_scr.shape, F32)
            dv_scr[...] = jnp.zeros(dv_scr.shape, F32)

        def step(d):
            todo = _diag_chunks(d, bq, bk, chunk)

            def products(t):
                cs = slice(t[0], t[0] + chunk)
                return (None if stored else _dot(ka_ref[0, :t[2], :], qat_ref[0, :, cs]),
                        _dot(v_ref[0, :t[2], :], do_ref[0, :, cs]))

            nxt = products(todo[0])
            for n, (c0, mask, n_keys) in enumerate(todo):
                cs = slice(c0, c0 + chunk)
                sc, dp = nxt
                if n + 1 < len(todo):
                    nxt = products(todo[n + 1])
                if stored:
                    p_bf = pt_ref[0, :n_keys, cs]
                    p = p_bf.astype(F32)
                else:
                    p = jnp.exp2(sc - lse_ref[0, :, cs])
                    if mask is not None:
                        p = jnp.where(mask, p, 0.0)
                    p_bf = p.astype(BF)
                ds = (p * (dp - dl_ref[0, :, cs])).astype(BF)
                dv_scr[:, :n_keys] += _dot_nt(do_ref[0, :, cs], p_bf)
                dk_scr[:, :n_keys] += _dot_nt(qat_ref[0, :VT_ROWS, cs], ds)
                c1 = c0 % dq_blk
                dq_ref[0, i * (bq // dq_blk) + c0 // dq_blk, :, c1:c1 + chunk] += _dot(kat_ref[0, :VT_ROWS, :n_keys], ds)

        @pl.when(i * r > j)
        def _():
            step(None)

        for d in range(r):
            @pl.when(j == i * r + d)
            def _(d=d):
                step(d)

        @pl.when(i == nq - 1)
        def _():
            dk_ref[0] = dk_scr[...]
            dv_ref[0] = dv_scr[...]

    qmap = lambda h, t, jt, it: (h, 0, it[t])
    kmap = lambda h, t, jt, it: (h, jt[t], 0)
    kmap_t = lambda h, t, jt, it: (h, 0, jt[t])
    if stored:
        first = [(pt, pl.BlockSpec((1, bk, bq), lambda h, t, jt, it: (h, jt[t], it[t]))),
                 (delta, pl.BlockSpec((1, 1, bq), qmap))]
    else:
        first = [(ka, pl.BlockSpec((1, bk, AUG_DIM), kmap)), (lse, pl.BlockSpec((1, 1, bq), qmap))]
    grid_spec = pltpu.PrefetchScalarGridSpec(
        num_scalar_prefetch=2, grid=(nh, len(pairs)),
        in_specs=[first[0][1], first[1][1], pl.BlockSpec((1, AUG_DIM, bk), kmap_t), pl.BlockSpec((1, bk, HEAD_DIM), kmap),
                  pl.BlockSpec((1, AUG_DIM, bq), qmap), pl.BlockSpec((1, HEAD_DIM, bq), qmap),
                  pl.BlockSpec((1, 1, bq), qmap)],
        out_specs=(pl.BlockSpec((1, s // dq_blk, VT_ROWS, dq_blk), lambda h, t, jt, it: (h, 0, 0, 0)),
                   pl.BlockSpec((1, VT_ROWS, bk), kmap_t), pl.BlockSpec((1, HEAD_DIM, bk), kmap_t)),
        scratch_shapes=[pltpu.VMEM((VT_ROWS, bk), F32), pltpu.VMEM((HEAD_DIM, bk), F32)])
    return _call(
        body, name="fox_bwd" if stored else "fox_bwd_recompute",
        out_shape=(jax.ShapeDtypeStruct((nh, s // dq_blk, VT_ROWS, dq_blk), F32),
                   jax.ShapeDtypeStruct((nh, VT_ROWS, s), F32), jax.ShapeDtypeStruct((nh, HEAD_DIM, s), F32)),
        grid_spec=grid_spec, compiler_params=_params(("parallel", "arbitrary")),
    )(jnp.asarray([p[0] for p in pairs], jnp.int32), jnp.asarray([p[1] for p in pairs], jnp.int32),
      first[0][0], first[1][0], kat, v, qat, dot_, delta)


def _swa_mask(i, tq):
    kpos = i * tq - WINDOW + lax.broadcasted_iota(jnp.int32, (tq + WINDOW, tq), 0)
    qpos = i * tq + lax.broadcasted_iota(jnp.int32, (tq + WINDOW, tq), 1)
    rel = qpos - kpos
    return (rel >= 0) & (rel < WINDOW) & (kpos >= 0)


def _swa_rows(ref, i, tq):
    before = pl.multiple_of(jnp.maximum(i * tq - WINDOW, 0), WINDOW)
    return jnp.concatenate([ref[0, pl.ds(before, WINDOW), :], ref[0, pl.ds(pl.multiple_of(i * tq, tq), tq), :]], axis=0)


def _swa_before(n_rows, tq):
    return pl.BlockSpec((1, n_rows, WINDOW), lambda g, i: (g, 0, jnp.maximum(i * (tq // WINDOW) - 1, 0)))


def _swa_probs_t(sc, mask, sink):
    sc = jnp.where(mask, sc, NEG)
    m = jnp.maximum(jnp.max(sc, axis=0, keepdims=True), sink)
    p = jnp.exp(sc - m)
    e_sink = jnp.exp(sink - m)
    inv_l = 1.0 / (jnp.sum(p, axis=0, keepdims=True) + e_sink)
    return p * inv_l, e_sink * inv_l


def _swa_fwd(qbt, kb, vbt, sinks, tq):
    s = qbt.shape[1]
    gw = SWA_GROUP * HEAD_DIM

    def body(q_ref, k_ref, vb_ref, vc_ref, s_ref, o_ref):
        i = pl.program_id(1)
        mask = _swa_mask(i, tq)
        kw = _swa_rows(k_ref, i, tq)
        vtw = jnp.concatenate([vb_ref[0], vc_ref[0]], axis=1)
        sk = s_ref[0]
        scores = lambda hh: _dot(kw, q_ref[hh * HEAD_DIM:(hh + 1) * HEAD_DIM, :])
        sc_next = scores(0)
        for hh in range(SWA_GROUP):
            rows = slice(hh * HEAD_DIM, (hh + 1) * HEAD_DIM)
            sink = sk[:, hh:hh + 1]
            sc = jnp.where(mask, sc_next, NEG)
            if hh + 1 < SWA_GROUP:
                sc_next = scores(hh + 1)
            m = jnp.maximum(jnp.max(sc, axis=0, keepdims=True), sink)
            acc = _dot(vtw, jnp.exp(sc - m).astype(BF))
            o_ref[rows, :] = acc[:HEAD_DIM] / (acc[HEAD_DIM:HEAD_DIM + 1] + jnp.exp(sink - m))

    kvspec = pl.BlockSpec((1, s, HEAD_DIM), lambda g, i: (g, 0, 0))
    qspec = pl.BlockSpec((gw, tq), lambda g, i: (g, i))
    return _call(
        body, name="swa_fwd", out_shape=jax.ShapeDtypeStruct((SWA_W, s), F32), grid=(SWA_KV_HEADS, s // tq),
        in_specs=[qspec, kvspec, _swa_before(VT_ROWS, tq), pl.BlockSpec((1, VT_ROWS, tq), lambda g, i: (g, 0, i)),
                  pl.BlockSpec((1, 1, SWA_GROUP), lambda g, i: (g, 0, 0))],
        out_specs=qspec, compiler_params=_params(("parallel", "parallel")),
    )(qbt, kb, vbt, vbt, sinks)


def _swa_bwd(qb, qbt, kb, kbt, vb, sinks, dob, dobt, tq):
    s = qb.shape[0]
    gw = SWA_GROUP * HEAD_DIM

    def body(q_ref, qt_ref, k_ref, ktb_ref, ktc_ref, v_ref, s_ref, do_ref, dot_ref, dq_ref, dk_ref, dv_ref, ds_ref):
        i = pl.program_id(1)

        @pl.when(i == 0)
        def _():
            dk_ref[...] = jnp.zeros(dk_ref.shape, F32)
            dv_ref[...] = jnp.zeros(dv_ref.shape, F32)
            ds_ref[...] = jnp.zeros(ds_ref.shape, F32)

        mask = _swa_mask(i, tq)
        kw = _swa_rows(k_ref, i, tq)
        vw = _swa_rows(v_ref, i, tq)
        ktw = jnp.concatenate([ktb_ref[0], ktc_ref[0]], axis=1)
        qv, dov = q_ref[...], do_ref[...]
        sk = s_ref[0]
        dsinks = []
        dk_acc = jnp.zeros((tq + WINDOW, HEAD_DIM), F32)
        dv_acc = jnp.zeros((tq + WINDOW, HEAD_DIM), F32)
        def products(hh):
            rows = slice(hh * HEAD_DIM, (hh + 1) * HEAD_DIM)
            return _dot(kw, qt_ref[rows, :]), _dot(vw, dot_ref[rows, :])

        nxt = products(0)
        for hh in range(SWA_GROUP):
            rows = slice(hh * HEAD_DIM, (hh + 1) * HEAD_DIM)
            sc, dp = nxt
            if hh + 1 < SWA_GROUP:
                nxt = products(hh + 1)
            p, p_sink = _swa_probs_t(sc, mask, sk[:, hh:hh + 1])
            delta = jnp.sum(p * dp, axis=0, keepdims=True)
            dsc = (p * (dp - delta)).astype(BF)
            dq_ref[rows, :] = _dot(ktw, dsc)
            dk_acc = dk_acc + _dot(dsc, qv[:, rows])
            dv_acc = dv_acc + _dot(p.astype(BF), dov[:, rows])
            dsinks.append(-jnp.sum(p_sink * delta, axis=1, keepdims=True))
        before = pl.ds(pl.multiple_of(jnp.maximum(i * tq - WINDOW, 0), WINDOW), WINDOW)
        own = pl.ds(pl.multiple_of(i * tq, tq), tq)
        dk_ref[0, before, :] += dk_acc[:WINDOW]
        dk_ref[0, own, :] += dk_acc[WINDOW:]
        dv_ref[0, before, :] += dv_acc[:WINDOW]
        dv_ref[0, own, :] += dv_acc[WINDOW:]
        ds_ref[0] += jnp.concatenate(dsinks, axis=1)

    kvspec = pl.BlockSpec((1, s, HEAD_DIM), lambda g, i: (g, 0, 0))
    qspec = pl.BlockSpec((tq, gw), lambda g, i: (i, g))
    qspec_t = pl.BlockSpec((gw, tq), lambda g, i: (g, i))
    sspec = pl.BlockSpec((1, 1, SWA_GROUP), lambda g, i: (g, 0, 0))
    kvshape = jax.ShapeDtypeStruct((SWA_KV_HEADS, s, HEAD_DIM), F32)
    return _call(
        body, name="swa_bwd",
        out_shape=(jax.ShapeDtypeStruct((SWA_W, s), F32), kvshape, kvshape,
                   jax.ShapeDtypeStruct((SWA_KV_HEADS, 1, SWA_GROUP), F32)),
        grid=(SWA_KV_HEADS, s // tq),
        in_specs=[qspec, qspec_t, kvspec, _swa_before(HEAD_DIM, tq),
                  pl.BlockSpec((1, HEAD_DIM, tq), lambda g, i: (g, 0, i)), kvspec, sspec, qspec, qspec_t],
        out_specs=(qspec_t, kvspec, kvspec, sspec),
        compiler_params=_params(("parallel", "arbitrary")),
    )(qb, qbt, kb, kbt, kbt, vb, sinks, dob, dobt)


def _pairs_to_rows(ref, n_rows=HEAD_DIM):
    parts = []
    for a in range(0, FOX_HEADS, 2):
        parts.append(jnp.concatenate([ref[a][:n_rows], ref[a + 1][:n_rows]], axis=0).T)
    return jnp.concatenate(parts, axis=1)


def _blocks_to_rows(ref):
    return jnp.concatenate([ref[a:a + LANES, :].T for a in range(0, ref.shape[0], LANES)], axis=1)


def _out_proj(oat, za, obt, zb, x, tgt, w_out, w_out_t, gate, g_post, inv_l, tm):
    s = x.shape[0]

    def body(oat_ref, za_ref, obt_ref, zb_ref, x_ref, t_ref, w_ref, wt_ref, gate_ref, gp_ref, il_ref,
             dout_ref, doat_ref, dla_ref, dza_ref, dob_ref, dobt_ref, dzb_ref, gw_ref, dgate_ref, dgp_ref, loss_ref):
        i = pl.program_id(0)

        @pl.when(i == 0)
        def _():
            gw_ref[...] = jnp.zeros(gw_ref.shape, F32)
            dgate_ref[...] = jnp.zeros(dgate_ref.shape, F32)
            dgp_ref[...] = jnp.zeros(dgp_ref.shape, F32)
            loss_ref[...] = jnp.zeros(loss_ref.shape, F32)

        oa_v = _pairs_to_rows(oat_ref)
        ob_v = _blocks_to_rows(obt_ref)
        za_v, zb_v = za_ref[...], zb_ref[...]
        sga, sgb = _sigmoid(za_v), _sigmoid(zb_v)
        sila, silb = za_v * sga, zb_v * sgb
        u = jnp.concatenate([oa_v * sila, ob_v * silb], axis=1).astype(BF)
        yv = _dot(u, w_ref[...])
        yhat, rstd = _rms_hat(yv)
        gp, gate_v = gp_ref[...], gate_ref[...]
        nrm = yhat * gp
        diff = (x_ref[...] + gate_v * nrm) - t_ref[...]
        loss_ref[...] += 0.5 * jnp.sum(jnp.sum(diff * diff, axis=1, keepdims=True), axis=0, keepdims=True) / D_MODEL
        dout = diff * (1.0 / D_MODEL)
        dout_ref[...] = dout
        dgate_ref[...] += jnp.sum(dout * nrm, axis=0, keepdims=True)
        dn = dout * gate_v
        dgp_ref[...] += jnp.sum(dn * yhat, axis=0, keepdims=True)
        dyhat = dn * gp
        dy = (rstd * (dyhat - yhat * jnp.mean(dyhat * yhat, axis=1, keepdims=True))).astype(BF)
        gw_ref[...] += _dot_tn(u, dy)
        du = _dot(dy, wt_ref[...])
        dua, dub = du[:, :FOX_W], du[:, FOX_W:]
        doa = dua * sila
        for a in range(0, FOX_HEADS, 2):
            pair_t = doa[:, a * HEAD_DIM:(a + 2) * HEAD_DIM].T
            for hd, rows in ((a, slice(0, HEAD_DIM)), (a + 1, slice(HEAD_DIM, 2 * HEAD_DIM))):
                inv_l = il_ref[hd]
                doat_ref[hd] = (pair_t[rows] * inv_l).astype(BF)
                dla_ref[hd] = jnp.sum(pair_t[rows] * oat_ref[hd], axis=0, keepdims=True) * inv_l
        dob = dub * silb
        dob_ref[...] = dob.astype(BF)
        for a in range(0, SWA_W, LANES):
            dobt_ref[a:a + LANES, :] = dob[:, a:a + LANES].T.astype(BF)
        dza_ref[...] = (dua * oa_v * (sga * (1.0 + za_v * (1.0 - sga)))).astype(BF)
        dzb_ref[...] = (dub * ob_v * (sgb * (1.0 + zb_v * (1.0 - sgb)))).astype(BF)

    row = lambda w: pl.BlockSpec((tm, w), lambda i: (i, 0))
    heads_t = lambda w: pl.BlockSpec((FOX_HEADS, w, tm), lambda i: (0, 0, i))
    vec = _full((1, D_MODEL))
    mat = _full((D_MODEL, D_MODEL))
    out_shape = (
        jax.ShapeDtypeStruct((s, D_MODEL), F32),
        jax.ShapeDtypeStruct((FOX_HEADS, HEAD_DIM, s), BF), jax.ShapeDtypeStruct((FOX_HEADS, 1, s), F32),
        jax.ShapeDtypeStruct((s, FOX_W), BF), jax.ShapeDtypeStruct((s, SWA_W), BF), jax.ShapeDtypeStruct((SWA_W, s), BF),
        jax.ShapeDtypeStruct((s, SWA_W), BF),
        jax.ShapeDtypeStruct((D_MODEL, D_MODEL), F32),
        jax.ShapeDtypeStruct((1, D_MODEL), F32), jax.ShapeDtypeStruct((1, D_MODEL), F32),
        jax.ShapeDtypeStruct((1, 1), F32),
    )
    col = pl.BlockSpec((SWA_W, tm), lambda i: (0, i))
    return _call(
        body, name="out_proj", out_shape=out_shape, grid=(s // tm,),
        in_specs=[heads_t(HEAD_DIM), row(FOX_W), col, row(SWA_W), row(D_MODEL), row(D_MODEL), mat, mat, vec, vec,
                  heads_t(1)],
        out_specs=(row(D_MODEL), heads_t(HEAD_DIM), heads_t(1), row(FOX_W), row(SWA_W), col, row(SWA_W), mat, vec, vec,
                   _full((1, 1))),
        compiler_params=_params(("arbitrary",)),
    )(oat, za, obt, zb, x, tgt, w_out, w_out_t, gate, g_post, inv_l)


def _assemble_dproj(dqt, dkt, dvt, dza, dqb, dzb, dkb, dvb, df, cos_t, sin_t, tm):
    s = dza.shape[0]

    def body(dqt_ref, dkt_ref, dvt_ref, dza_ref, dqb_ref, dzb_ref, dkb_ref, dvb_ref, df_ref, cos_ref, sin_ref, o_ref):
        def cat(ref, n):
            return jnp.concatenate([ref[hd] for hd in range(n)], axis=1)

        cos2, sin2 = cos_ref[...], sin_ref[...]
        cos8 = jnp.concatenate([cos2] * 4, axis=1)
        sin8 = jnp.concatenate([sin2] * 4, axis=1)
        scale = HEAD_DIM ** -0.5
        o_ref[:, C_QA:C_QA + FOX_W] = (_pairs_to_rows(dqt_ref.at[:, 0]) * scale).astype(BF)
        o_ref[:, C_KA:C_KA + FOX_W] = (_pairs_to_rows(dkt_ref) * LN2).astype(BF)
        o_ref[:, C_VA:C_VA + FOX_W] = _pairs_to_rows(dvt_ref).astype(BF)
        o_ref[:, C_ZA:C_ZA + FOX_W] = dza_ref[...]
        dq = _blocks_to_rows(dqb_ref) * scale
        o_ref[:, C_QB:C_QB + SWA_W] = (dq * cos8 - _rope_partner(dq) * sin8).astype(BF)
        o_ref[:, C_ZB:C_ZB + SWA_W] = dzb_ref[...]
        dk = cat(dkb_ref, SWA_KV_HEADS)
        o_ref[:, C_KB:C_KB + SWA_KV_W] = (dk * cos2 - _rope_partner(dk) * sin2).astype(BF)
        o_ref[:, C_VB:C_VB + SWA_KV_W] = cat(dvb_ref, SWA_KV_HEADS).astype(BF)
        o_ref[:, C_F:C_F + LANES] = df_ref[...].astype(BF)

    row = lambda w: pl.BlockSpec((tm, w), lambda i: (i, 0))
    heads = lambda n: pl.BlockSpec((n, tm, HEAD_DIM), lambda i: (0, i, 0))
    heads_t = lambda w: pl.BlockSpec((FOX_HEADS, w, tm), lambda i: (0, 0, i))
    return _call(
        body, name="assemble_dproj", out_shape=jax.ShapeDtypeStruct((s, WP), BF), grid=(s // tm,),
        in_specs=[pl.BlockSpec((FOX_HEADS, 1, VT_ROWS, tm), lambda i: (0, i, 0, 0)), heads_t(VT_ROWS), heads_t(HEAD_DIM),
                  row(FOX_W), pl.BlockSpec((SWA_W, tm), lambda i: (0, i)), row(SWA_W), heads(SWA_KV_HEADS),
                  heads(SWA_KV_HEADS), row(LANES), row(LANES), row(LANES)],
        out_specs=row(WP), compiler_params=_params(("parallel",)),
    )(dqt, dkt, dvt, dza, dqb, dzb, dkb, dvb, df, cos_t, sin_t)


def _in_proj_bwd_x(dproj, w_al_t, x, dout, g_pre, scale1p, tm, parts):
    s = x.shape[0]
    n_steps = s // tm
    masks = [(1, 0), (0, 1), (1, 1)]

    def body(dp_ref, wt_ref, x_ref, dout_ref, g_ref, sc_ref, parts_ref, gx_ref, dsh_ref, dsc_ref, dg_ref, got_ref,
             send_sems, recv_sems, local_sem):
        i = pl.program_id(0)
        cx, cy, cc = lax.axis_index("x"), lax.axis_index("y"), lax.axis_index("c")
        me = 2 * cx + cy
        own = pltpu.make_async_copy(parts_ref.at[me], got_ref.at[me], local_sem)

        def copy(k, send):
            dx, dy = masks[k]
            peer = 2 * (cx ^ dx) + (cy ^ dy)
            return pltpu.make_async_remote_copy(
                src_ref=parts_ref.at[peer if send else me], dst_ref=got_ref.at[me if send else peer],
                send_sem=send_sems.at[k], recv_sem=recv_sems.at[k], device_id=(cx ^ dx, cy ^ dy, cc), device_id_type=MESH)

        @pl.when(i == 0)
        def _():
            dsh_ref[...] = jnp.zeros(dsh_ref.shape, F32)
            dsc_ref[...] = jnp.zeros(dsc_ref.shape, F32)
            dg_ref[...] = jnp.zeros(dg_ref.shape, F32)
            own.start()
            for k in range(len(masks)):
                copy(k, True).start()

        @pl.when(i == n_steps - 1)
        def _():
            for k in range(len(masks)):
                copy(k, False).wait_recv()
            for k in range(len(masks)):
                copy(k, True).wait_send()
            own.wait()

        dh = _dot(dp_ref[...], wt_ref[...])
        xhat, rstd = _rms_hat(x_ref[...])
        g, sc = g_ref[...], sc_ref[...]
        dsh_ref[...] += jnp.sum(dh, axis=0, keepdims=True)
        dhx = dh * xhat
        dsc_ref[...] += jnp.sum(dhx * g, axis=0, keepdims=True)
        dg_ref[...] += jnp.sum(dhx * sc, axis=0, keepdims=True)
        dxhat = dh * (g * sc)
        gx_ref[...] = dout_ref[...] + rstd * (dxhat - xhat * jnp.mean(dxhat * xhat, axis=1, keepdims=True))

    row = lambda w: pl.BlockSpec((tm, w), lambda i: (i, 0))
    vec = _full((1, D_MODEL))
    vshape = jax.ShapeDtypeStruct((1, D_MODEL), F32)
    hbm = pl.BlockSpec(memory_space=pl.ANY)
    return _call(
        body, name="in_proj_bwd_x",
        out_shape=(jax.ShapeDtypeStruct((s, D_MODEL), F32), vshape, vshape, vshape,
                   jax.ShapeDtypeStruct(parts.shape, parts.dtype)),
        grid=(n_steps,),
        in_specs=[row(WP), _full((WP, D_MODEL)), row(D_MODEL), row(D_MODEL), vec, vec, hbm],
        out_specs=(row(D_MODEL), vec, vec, vec, hbm),
        scratch_shapes=[pltpu.SemaphoreType.DMA((3,)), pltpu.SemaphoreType.DMA((3,)), pltpu.SemaphoreType.DMA],
        compiler_params=_params(("arbitrary",), has_side_effects=True),
    )(dproj, w_al_t, x, dout, g_pre, scale1p, parts)


def _in_proj_bwd_w(h, dproj, tk, tn):
    s = h.shape[0]
    n_k = s // tk

    def body(h_ref, dp_ref, gw_ref, acc_scr):
        k = pl.program_id(1)

        @pl.when(k == 0)
        def _():
            acc_scr[...] = jnp.zeros(acc_scr.shape, F32)

        acc_scr[...] += _dot_tn(h_ref[...], dp_ref[...])

        @pl.when(k == n_k - 1)
        def _():
            gw_ref[...] = acc_scr[...].astype(BF)

    return _call(
        body, name="in_proj_bwd_w", out_shape=jax.ShapeDtypeStruct((D_MODEL, WP), BF), grid=(WP // tn, n_k),
        in_specs=[pl.BlockSpec((tk, D_MODEL), lambda n, k: (k, 0)), pl.BlockSpec((tk, tn), lambda n, k: (k, n))],
        out_specs=pl.BlockSpec((D_MODEL, tn), lambda n, k: (0, n)),
        scratch_shapes=[pltpu.VMEM((D_MODEL, tn), F32)],
        compiler_params=_params(("parallel", "arbitrary")),
    )(h, dproj)


def _align_w_in(w_cols):
    def part(name, width):
        return w_cols[:, _SRC[name]:_SRC[name] + width]

    fpad = jnp.pad(part("fa", FOX_HEADS), ((0, 0), (0, LANES - FOX_HEADS)))
    return jnp.concatenate([part("qa", FOX_W), part("ka", FOX_W), part("va", FOX_W), part("za", FOX_W),
                            part("qb", SWA_W), part("zb", SWA_W), part("kb", SWA_KV_W), part("vb", SWA_KV_W), fpad], axis=1)


def _unalign_w_in(g_al):
    def part(c0, width):
        return g_al[:, c0:c0 + width]

    return jnp.concatenate([part(C_QA, FOX_W), part(C_KA, FOX_W), part(C_VA, FOX_W), part(C_F, FOX_HEADS),
                            part(C_ZA, FOX_W), part(C_QB, SWA_W), part(C_KB, SWA_KV_W), part(C_VB, SWA_KV_W),
                            part(C_ZB, SWA_W)], axis=1)


def _rope_tables(positions):
    inv_freq = ROPE_THETA ** (-jnp.arange(HALF, dtype=F32) / HALF)
    ang = positions.astype(F32)[:, None] * inv_freq
    cos, sin = jnp.cos(ang), jnp.sin(ang)
    return jnp.concatenate([cos, cos, cos, cos], axis=1), jnp.concatenate([-sin, sin, -sin, sin], axis=1)


def _tiles(s):
    if s >= 4096:
        return dict(tm=512, blk=512, bq=2048, bk=2048, bk_bwd=2048, chunk=256, tq=256, tm_out=512, tk=1024, tn=1152)
    return dict(tm=128, blk=128, bq=256, bk=256, bk_bwd=256, chunk=128, tq=128, tm_out=128, tk=128, tn=1152)


def kernel(x, c, positions, w_ada, b_ada, g_pre, w_in, b_fgate, sinks, w_out, g_post, loss_target, m_w_ada, m_b_ada, m_g_pre, m_w_in, m_b_fgate, m_sinks, m_w_out, m_g_post, v_w_ada, v_b_ada, v_g_pre, v_w_in, v_b_fgate, v_sinks, v_w_out, v_g_post):
    s = x.shape[1]
    t = _tiles(s)
    nc = s // LANES
    rows = FOX_HEADS * nc
    me = 4 * lax.axis_index("x") + 2 * lax.axis_index("y") + lax.axis_index("c")
    chip = 2 * lax.axis_index("x") + lax.axis_index("y")
    core = lax.axis_index("c")
    x2, tgt = x[0], loss_target[0]

    c_all = _allgather_devices(c, "gather_c")[:, 0, :]
    a_all, mod_shard = _ada_shard(c_all, w_ada[0])
    mod_all = _allgather_devices(mod_shard, "gather_mod")
    mod_rows = lax.dynamic_index_in_dim(mod_all, me, axis=1, keepdims=False)
    mod = mod_rows.reshape(N_CHIPS, 2, W_ADA_SHARD)[:, 0, :].reshape(1, 3 * D_MODEL) + b_ada
    shift, scale1p, gate = mod[:, :D_MODEL], 1.0 + mod[:, D_MODEL:2 * D_MODEL], mod[:, 2 * D_MODEL:]

    w_in_pad = jnp.pad(w_in[0].astype(BF), ((0, 0), (0, W_IN_SHARD_PAD - W_IN_SHARD)))
    w_pack = jnp.concatenate([w_in_pad, w_out[0].astype(BF).reshape(D_MODEL, W_OUT_SHARD)], axis=1)
    w_all = _allgather_chips(w_pack.reshape(2, D_MODEL // 2, -1), "gather_weights").reshape(N_CHIPS, D_MODEL, -1)
    w_cols = jnp.concatenate([w_all[k, :, :W_IN_SHARD] for k in range(N_CHIPS)], axis=1)
    w_al = _align_w_in(w_cols)
    w_al_t = w_al.T
    w_out_all = w_all[:, :, W_IN_SHARD_PAD:].reshape(D_MODEL, D_MODEL)
    w_out_t = w_out_all.T

    cos_t, sin_t = _rope_tables(positions[0])

    f_pad = _forget_logits(x2, g_pre, scale1p, shift, w_al[:, C_F:], t["tm"])
    f_rows = f_pad[:, :FOX_HEADS].T.reshape(rows, LANES)
    bias_rows = jnp.repeat(b_fgate[0], nc)[:, None]
    cum = _log_forget_cumsum(f_rows, bias_rows, nc).reshape(FOX_HEADS, s)
    h, qat, ka, kat, va, vat, za, zb, qb, kb, vb, qbt, kbt, vbt, m_own = _in_proj(
        x2, g_pre, scale1p, shift, w_al[:, C_VA:C_F], w_al_t[:C_ZA], cum, cos_t, sin_t, t["tm"])
    m_own = m_own[:, None, :]
    fox_args = (qat, ka, vat, m_own, t["bq"], t["bk"], t["chunk"])
    oat, lse, bad, pt = _fox_fwd(*fox_args, running_max=False)
    overflowed = jnp.max(bad) > 0.0
    oat, lse = lax.cond(overflowed, lambda: _fox_fwd(*fox_args, running_max=True)[:2], lambda: (oat, lse))
    inv_l = jnp.where(overflowed, 1.0, jnp.exp2(m_own - lse))
    sinks_g = sinks.reshape(SWA_KV_HEADS, 1, SWA_GROUP)
    obt = _swa_fwd(qbt, kb, vbt, sinks_g, t["tq"])

    dout, doat, delta_a, dza, dob, dobt, dzb, gw_out, dgate, dg_post, loss_part = _out_proj(
        oat, za, obt, zb, x2, tgt, w_out_all, w_out_t, gate, g_post, inv_l, t["tm_out"])

    bwd_args = (qat, ka, kat, va, doat, lse, delta_a)
    bwd_tiles = (t["bq"], t["bk_bwd"], t["chunk"], t["blk"])
    dqt, dkt, dvt = lax.cond(overflowed, lambda: _fox_bwd(*bwd_args, None, *bwd_tiles),
                             lambda: _fox_bwd(*bwd_args, pt, *bwd_tiles))
    dcum = dqt[:, :, HEAD_DIM, :].reshape(FOX_HEADS, s) - dkt[:, HEAD_DIM, :]
    df_rows, db_heads = _log_forget_cumsum_bwd(dcum.reshape(rows, LANES), f_rows, bias_rows, nc)
    df_pad = jnp.pad(df_rows.reshape(FOX_HEADS, s).T, ((0, 0), (0, LANES - FOX_HEADS)))
    dqb, dkb, dvb, dsinks = _swa_bwd(qb, qbt, kb, kbt, vb, sinks_g, dob, dobt, t["tq"])

    dproj = _assemble_dproj(dqt, dkt, dvt, dza, dqb, dzb, dkb, dvb, df_pad, cos_t, sin_t, t["blk"])
    gw_in = _unalign_w_in(_in_proj_bwd_w(h, dproj, t["tk"], t["tn"]))

    gin = jnp.stack([jnp.pad(gw_in[:, k * W_IN_SHARD:(k + 1) * W_IN_SHARD], ((0, 0), (0, W_IN_SHARD_PAD - W_IN_SHARD)))
                     for k in range(N_CHIPS)])
    gout = gw_out.astype(BF).reshape(N_CHIPS, D_MODEL, W_OUT_SHARD)
    gbig = jnp.concatenate([gin, gout], axis=2)
    half = D_MODEL // 2
    gw = W_IN_SHARD_PAD + W_OUT_SHARD
    keep = lax.dynamic_slice_in_dim(gbig, core * half, half, axis=1)
    give = lax.dynamic_slice_in_dim(gbig, (1 - core) * half, half, axis=1)
    got = _swap_sibling(give.reshape(N_CHIPS * half, gw), "swap_grad_halves")
    pair = _add(keep.reshape(N_CHIPS * half, gw), got, "add_pair", BF).reshape(N_CHIPS, half, gw)
    grad_x, dshift, dscale, dg_pre, from_chips = _in_proj_bwd_x(
        dproj, w_al_t, x2, dout, g_pre, scale1p, t["tm_out"], pair)

    pad_lane = lambda vrow: jnp.pad(vrow, ((0, 0), (0, LANES - vrow.shape[1])))
    packed = jnp.concatenate([dshift, dscale, dgate, dg_pre, dg_post,
                              pad_lane(db_heads.reshape(1, FOX_HEADS)), pad_lane(dsinks.reshape(1, FOX_HEADS)),
                              pad_lane(loss_part)], axis=1)
    parts = _allgather_devices(packed, "gather_partials")
    tot = _sum_devices(parts)
    loss = tot[0, P_LOSS]
    g_b_ada = tot[:, P_DMOD:P_DMOD + 3 * D_MODEL]
    g_g_pre = tot[:, P_GPRE:P_GPRE + D_MODEL]
    g_g_post = tot[:, P_GPOST:P_GPOST + D_MODEL]
    g_b_fgate = tot[:, P_BF:P_BF + FOX_HEADS]
    g_sinks = tot[:, P_SINK:P_SINK + FOX_HEADS]
    dm_shard = lax.dynamic_slice_in_dim(parts[:, 0, :3 * D_MODEL], chip * W_ADA_SHARD, W_ADA_SHARD, axis=1)
    g_w_ada = _grad_w_ada(a_all.T, dm_shard)

    mine = _sum_chips(from_chips, "sum_chips")
    other = _swap_sibling(mine, "swap_grad_result")
    lo = jnp.where(core == 0, mine, other)
    hi = jnp.where(core == 0, other, mine)
    gfull = jnp.concatenate([lo, hi], axis=0)
    g_w_in = gfull[:, :W_IN_SHARD]
    g_w_out = gfull[:, W_IN_SHARD_PAD:].reshape(W_OUT_SHARD, D_MODEL)

    grads = dict(w_ada=g_w_ada, b_ada=g_b_ada, g_pre=g_g_pre, w_in=g_w_in, b_fgate=g_b_fgate, sinks=g_sinks,
                 w_out=g_w_out, g_post=g_g_post)
    weights = dict(w_ada=w_ada, b_ada=b_ada, g_pre=g_pre, w_in=w_in, b_fgate=b_fgate, sinks=sinks, w_out=w_out, g_post=g_post)
    moms = dict(w_ada=m_w_ada, b_ada=m_b_ada, g_pre=m_g_pre, w_in=m_w_in, b_fgate=m_b_fgate, sinks=m_sinks, w_out=m_w_out, g_post=m_g_post)
    vars_ = dict(w_ada=v_w_ada, b_ada=v_b_ada, g_pre=v_g_pre, w_in=v_w_in, b_fgate=v_b_fgate, sinks=v_sinks, w_out=v_w_out, g_post=v_g_post)
    names = ["w_ada", "b_ada", "g_pre", "w_in", "b_fgate", "sinks", "w_out", "g_post"]
    g_out, d_out, m_out, v_out = [], [], [], []
    for n in names:
        if n == "w_in":
            flat = lambda a: jnp.transpose(a, (2, 0, 1)).reshape(W_IN_SHARD * D_MODEL // LANES, LANES)
            unflat = lambda a: jnp.transpose(a.reshape(W_IN_SHARD, 1, D_MODEL), (1, 2, 0))
            outs = _adamw(flat(w_in), flat(grads[n][None]), flat(moms[n]), flat(vars_[n]), "adamw_" + n)
            go, d, nm, nv = (unflat(a) for a in outs)
        else:
            g2 = grads[n].reshape(weights[n].shape[-2:])
            go, d, nm, nv = _adamw(weights[n], g2, moms[n], vars_[n], "adamw_" + n)
        g_out.append(go)
        d_out.append(d)
        m_out.append(nm)
        v_out.append(nv)
    return (loss, grad_x.reshape(x.shape), *g_out, *d_out, *m_out, *v_out)
```

```python
import jax
import jax.numpy as jnp
from jax import lax
from jax.experimental import pallas as pl
from jax.experimental.pallas import tpu as pltpu

_INTERPRET = False

D_MODEL = 1024
HEAD_DIM = 64
HALF = HEAD_DIM // 2
AUG_DIM = 128
AUG_ROWS = 8
VT_ROWS = 80
LOG2E = 1.4426950408889634
LN2 = 0.6931471805599453
Q_SCALE = LOG2E * 64 ** -0.5
FOX_HEADS = 8
FOX_W = 512
SWA_W = 512
SWA_KV_HEADS = 2
SWA_GROUP = 4
SWA_KV_W = 128
WINDOW = 128
ROPE_THETA = 10000.0
RMS_EPS = 1e-6
IN_WIDTH = 3336
N_CHIPS = 4
N_DEV = 8
W_IN_SHARD = IN_WIDTH // N_CHIPS
W_IN_SHARD_PAD = 896
W_ADA_SHARD = 3 * D_MODEL // N_CHIPS
W_OUT_SHARD = D_MODEL // N_CHIPS
LANES = 128

_SRC = dict(qa=0, ka=512, va=1024, fa=1536, za=1544, qb=2056, kb=2568, vb=2696, zb=2824)
C_QA, C_KA, C_VA, C_ZA, C_QB, C_ZB, C_KB, C_VB, C_F = 0, 512, 1024, 1536, 2048, 2560, 3072, 3200, 3328
WP = 3456

ADAM_LR = 0.001
ADAM_B1 = 0.9
ADAM_B2 = 0.999
ADAM_EPS = 1e-08
ADAM_WD = 0.01
ADAM_STEP = 10
ADAMW_BLOCK_ELEMS = 300_000

VMEM_LIMIT = 56 * 1024 * 1024
NEG = -1e30
OVERFLOW_GUARD = 1e30
MESH = pl.DeviceIdType.MESH
BF = jnp.bfloat16
F32 = jnp.float32

P_DMOD, P_GPRE, P_GPOST, P_BF, P_SINK, P_LOSS, P_LEN = 0, 3072, 4096, 5120, 5248, 5376, 5504


def _call(body, **kw):
    return pl.pallas_call(body, interpret=_INTERPRET, **kw)


def _params(sem=None, **kw):
    return pltpu.CompilerParams(dimension_semantics=sem, vmem_limit_bytes=VMEM_LIMIT, **kw)


def _full(shape):
    zeros = (0,) * len(shape)
    return pl.BlockSpec(shape, lambda *_: zeros)


def _dot(a, b):
    return jnp.dot(a, b, preferred_element_type=F32)


def _dot_nt(a, b):
    return lax.dot_general(a, b, (((1,), (1,)), ((), ())), preferred_element_type=F32)


def _dot_tn(a, b):
    return lax.dot_general(a, b, (((0,), (0,)), ((), ())), preferred_element_type=F32)


def _sigmoid(z):
    return 1.0 / (1.0 + jnp.exp(-z))


def _rope_partner(t):
    w = t.shape[-1]
    lane = lax.broadcasted_iota(jnp.int32, t.shape, t.ndim - 1)
    return jnp.where((lane & (HEAD_DIM - 1)) < HALF, pltpu.roll(t, w - HALF, t.ndim - 1), pltpu.roll(t, HALF, t.ndim - 1))


def _allgather_devices(v, name):
    r, cdim = v.shape
    masks = [(dx, dy, dc) for dx in (0, 1) for dy in (0, 1) for dc in (0, 1)][1:]

    def body(v_ref, out_ref, send_sems, recv_sems):
        x, y, c = lax.axis_index("x"), lax.axis_index("y"), lax.axis_index("c")
        me = 4 * x + 2 * y + c
        out_ref[me] = v_ref[...]
        copies = []
        for k, (dx, dy, dc) in enumerate(masks):
            cp = pltpu.make_async_remote_copy(
                src_ref=v_ref, dst_ref=out_ref.at[me], send_sem=send_sems.at[k], recv_sem=recv_sems.at[k],
                device_id=(x ^ dx, y ^ dy, c ^ dc), device_id_type=MESH)
            cp.start()
            copies.append(cp)
        for k, (dx, dy, dc) in enumerate(masks):
            peer = 4 * (x ^ dx) + 2 * (y ^ dy) + (c ^ dc)
            pltpu.make_async_remote_copy(
                src_ref=v_ref, dst_ref=out_ref.at[peer], send_sem=send_sems.at[k], recv_sem=recv_sems.at[k],
                device_id=(x ^ dx, y ^ dy, c ^ dc), device_id_type=MESH).wait_recv()
        for cp in copies:
            cp.wait_send()

    return _call(
        body, name=name, out_shape=jax.ShapeDtypeStruct((N_DEV, r, cdim), v.dtype),
        in_specs=[pl.BlockSpec(memory_space=pltpu.VMEM)], out_specs=pl.BlockSpec(memory_space=pltpu.VMEM),
        scratch_shapes=[pltpu.SemaphoreType.DMA((7,)), pltpu.SemaphoreType.DMA((7,))],
        compiler_params=pltpu.CompilerParams(has_side_effects=True),
    )(v)


def _allgather_chips(v, name):
    _, r, cdim = v.shape
    masks = [(1, 0), (0, 1), (1, 1)]
    n = len(masks)

    def body(v_ref, out_ref, send_sems, recv_sems, local_sem):
        x, y, c = lax.axis_index("x"), lax.axis_index("y"), lax.axis_index("c")
        me = 2 * x + y
        mine = pltpu.make_async_copy(v_ref, out_ref.at[me], local_sem)
        mine.start()

        def copy(k, chip, half, to):
            return pltpu.make_async_remote_copy(
                src_ref=v_ref.at[half] if k < n else out_ref.at[chip, half], dst_ref=out_ref.at[chip, half],
                send_sem=send_sems.at[k], recv_sem=recv_sems.at[k], device_id=to, device_id_type=MESH)

        first = [copy(k, me, c, (x ^ dx, y ^ dy, c)) for k, (dx, dy) in enumerate(masks)]
        for cp in first:
            cp.start()
        passed = []
        for k, (dx, dy) in enumerate(masks):
            peer = 2 * (x ^ dx) + (y ^ dy)
            copy(k, peer, c, (x, y, c)).wait_recv()
            cp = copy(n + k, peer, c, (x, y, 1 - c))
            cp.start()
            passed.append(cp)
        for k, (dx, dy) in enumerate(masks):
            copy(n + k, 2 * (x ^ dx) + (y ^ dy), 1 - c, (x, y, c)).wait_recv()
        for cp in first + passed:
            cp.wait_send()
        mine.wait()

    return _call(
        body, name=name, out_shape=jax.ShapeDtypeStruct((N_CHIPS, 2, r, cdim), v.dtype),
        in_specs=[pl.BlockSpec(memory_space=pl.ANY)], out_specs=pl.BlockSpec(memory_space=pl.ANY),
        scratch_shapes=[pltpu.SemaphoreType.DMA((2 * n,)), pltpu.SemaphoreType.DMA((2 * n,)), pltpu.SemaphoreType.DMA],
        compiler_params=pltpu.CompilerParams(has_side_effects=True),
    )(v)


def _swap_sibling(v, name):
    def body(v_ref, out_ref, send_sem, recv_sem):
        x, y, c = lax.axis_index("x"), lax.axis_index("y"), lax.axis_index("c")
        cp = pltpu.make_async_remote_copy(
            src_ref=v_ref, dst_ref=out_ref, send_sem=send_sem, recv_sem=recv_sem,
            device_id=(x, y, 1 - c), device_id_type=MESH)
        cp.start()
        cp.wait()

    return _call(
        body, name=name, out_shape=jax.ShapeDtypeStruct(v.shape, v.dtype),
        in_specs=[pl.BlockSpec(memory_space=pl.ANY)], out_specs=pl.BlockSpec(memory_space=pl.ANY),
        scratch_shapes=[pltpu.SemaphoreType.DMA, pltpu.SemaphoreType.DMA],
        compiler_params=pltpu.CompilerParams(has_side_effects=True),
    )(v)


def _ada_shard(c_all, w_ada_shard):
    def body(c_ref, w_ref, a_ref, mod_ref):
        cv = c_ref[...]
        a = cv * _sigmoid(cv)
        a_ref[...] = a
        mod_ref[...] = _dot(a.astype(BF), w_ref[...].astype(BF))

    return _call(
        body, name="ada_shard",
        out_shape=(jax.ShapeDtypeStruct((N_DEV, D_MODEL), F32), jax.ShapeDtypeStruct((N_DEV, W_ADA_SHARD), F32)),
        compiler_params=_params(),
    )(c_all, w_ada_shard)


def _grad_w_ada(a_t, dm_shard):
    def body(a_ref, dm_ref, out_ref):
        acc = jnp.zeros((D_MODEL, W_ADA_SHARD), F32)
        for b in range(N_DEV):
            acc = acc + a_ref[:, b:b + 1] * dm_ref[b:b + 1, :]
        out_ref[...] = acc

    return _call(body, name="grad_w_ada", out_shape=jax.ShapeDtypeStruct((D_MODEL, W_ADA_SHARD), F32),
                 compiler_params=_params())(a_t, dm_shard)


def _sum_devices(parts):
    n = parts.shape[-1]

    def body(p_ref, out_ref):
        acc = p_ref[0]
        for b in range(1, N_DEV):
            acc = acc + p_ref[b]
        out_ref[...] = acc

    return _call(body, name="sum_devices", out_shape=jax.ShapeDtypeStruct((1, n), F32), compiler_params=_params())(parts)


def _add(a, b, name, out_dtype):
    r, cdim = a.shape
    tr = min(r, 256)

    def body(a_ref, b_ref, o_ref):
        o_ref[...] = (a_ref[...].astype(F32) + b_ref[...].astype(F32)).astype(out_dtype)

    spec = pl.BlockSpec((tr, cdim), lambda i: (i, 0))
    return _call(body, name=name, out_shape=jax.ShapeDtypeStruct(a.shape, out_dtype), grid=(r // tr,),
                 in_specs=[spec, spec], out_specs=spec, compiler_params=_params(("parallel",)))(a, b)


def _sum_chips(parts, name):
    _, r, cdim = parts.shape
    tr = min(r, 128)

    def body(p_ref, o_ref):
        o_ref[...] = ((p_ref[0].astype(F32) + p_ref[1].astype(F32)) + p_ref[2].astype(F32)) + p_ref[3].astype(F32)

    return _call(body, name=name, out_shape=jax.ShapeDtypeStruct((r, cdim), F32), grid=(r // tr,),
                 in_specs=[pl.BlockSpec((N_CHIPS, tr, cdim), lambda i: (0, i, 0))],
                 out_specs=pl.BlockSpec((tr, cdim), lambda i: (i, 0)), compiler_params=_params(("parallel",)))(parts)


def _adamw(w, g, m, v, name):
    r, cdim = w.shape[-2:]
    lead = w.ndim - 2
    tr = r if r <= 256 else max(t for t in range(8, ADAMW_BLOCK_ELEMS // cdim + 1, 8) if r % t == 0)
    c1 = 1.0 / (1.0 - ADAM_B1 ** ADAM_STEP)
    c2 = 1.0 / (1.0 - ADAM_B2 ** ADAM_STEP)

    def body(w_ref, g_ref, m_ref, v_ref, go_ref, d_ref, nm_ref, nv_ref):
        gv = g_ref[...].reshape(go_ref.shape)
        nm = ADAM_B1 * m_ref[...] + (1.0 - ADAM_B1) * gv
        nv = ADAM_B2 * v_ref[...] + (1.0 - ADAM_B2) * (gv * gv)
        m_hat = nm * c1
        v_hat = nv * c2
        go_ref[...] = gv
        d_ref[...] = -ADAM_LR * (m_hat / (jnp.sqrt(v_hat) + ADAM_EPS) + ADAM_WD * w_ref[...])
        nm_ref[...] = nm
        nv_ref[...] = nv

    spec = pl.BlockSpec((1,) * lead + (tr, cdim), lambda i: (0,) * lead + (i, 0))
    shp = jax.ShapeDtypeStruct(w.shape, F32)
    return _call(body, name=name, out_shape=(shp,) * 4, grid=(r // tr,),
                 in_specs=[spec, pl.BlockSpec((tr, cdim), lambda i: (i, 0)), spec, spec],
                 out_specs=(spec,) * 4, compiler_params=_params(("parallel",)))(w, g, m, v)


def _head_of_row(r, nc):
    assert nc & (nc - 1) == 0
    return lax.shift_right_logical(r, nc.bit_length() - 1)


def _chunk_mats(rows, nc, reverse):
    ri = lax.broadcasted_iota(jnp.int32, (rows, rows), 0)
    ci = lax.broadcasted_iota(jnp.int32, (rows, rows), 1)
    same = _head_of_row(ri, nc) == _head_of_row(ci, nc)
    between = jnp.where(same & ((ci > ri) if reverse else (ci < ri)), 1.0, 0.0).astype(F32)
    li = lax.broadcasted_iota(jnp.int32, (LANES, LANES), 0)
    lj = lax.broadcasted_iota(jnp.int32, (LANES, LANES), 1)
    within = jnp.where((li >= lj) if reverse else (li <= lj), 1.0, 0.0).astype(F32)
    return between, within


def _dot_hi(a, b):
    return jnp.dot(a, b, preferred_element_type=F32, precision=lax.Precision.HIGHEST)


def _scan_rows(t, nc, reverse):
    between, within = _chunk_mats(t.shape[0], nc, reverse)
    inner = _dot_hi(t, within)
    tot = jnp.sum(t, axis=1, keepdims=True)
    return inner + _dot_hi(between, jnp.broadcast_to(tot, t.shape))


def _log_forget_cumsum(f_rows, bias_rows, nc):
    def body(f_ref, b_ref, cum_ref):
        z = f_ref[...] + b_ref[...]
        lf = jnp.minimum(z, 0.0) - jnp.log(1.0 + jnp.exp(-jnp.abs(z)))
        cum_ref[...] = _scan_rows(lf, nc, False)

    return _call(body, name="forget_cumsum", out_shape=jax.ShapeDtypeStruct(f_rows.shape, F32),
                 compiler_params=_params())(f_rows, bias_rows)


def _log_forget_cumsum_bwd(dcum_rows, f_rows, bias_rows, nc):
    rows = f_rows.shape[0]

    def body(d_ref, f_ref, b_ref, df_ref, db_ref):
        dlf = _scan_rows(d_ref[...], nc, True)
        z = f_ref[...] + b_ref[...]
        df = dlf * _sigmoid(-z)
        df_ref[...] = df
        hi = lax.broadcasted_iota(jnp.int32, (FOX_HEADS, rows), 0)
        ri = lax.broadcasted_iota(jnp.int32, (FOX_HEADS, rows), 1)
        sel = jnp.where(_head_of_row(ri, nc) == hi, 1.0, 0.0).astype(F32)
        db_ref[...] = jnp.sum(_dot_hi(sel, df), axis=1, keepdims=True)

    return _call(body, name="forget_cumsum_bwd",
                 out_shape=(jax.ShapeDtypeStruct(f_rows.shape, F32), jax.ShapeDtypeStruct((FOX_HEADS, 1), F32)),
                 compiler_params=_params())(dcum_rows, f_rows, bias_rows)


def _rms_hat(xv):
    rstd = lax.rsqrt(jnp.mean(xv * xv, axis=-1, keepdims=True) + RMS_EPS)
    return xv * rstd, rstd


def _modulated(x_ref, g_ref, sc_ref, sh_ref):
    xhat, _ = _rms_hat(x_ref[...])
    return ((xhat * g_ref[...]) * sc_ref[...] + sh_ref[...]).astype(BF)


def _forget_logits(x, g_pre, scale1p, shift, w_f, tm):
    s = x.shape[0]

    def body(x_ref, g_ref, sc_ref, sh_ref, w_ref, f_ref):
        f_ref[...] = _dot(_modulated(x_ref, g_ref, sc_ref, sh_ref), w_ref[...])

    vec = _full((1, D_MODEL))
    return _call(
        body, name="forget_logits", out_shape=jax.ShapeDtypeStruct((s, LANES), F32), grid=(s // tm,),
        in_specs=[pl.BlockSpec((tm, D_MODEL), lambda i: (i, 0)), vec, vec, vec, _full((D_MODEL, LANES))],
        out_specs=pl.BlockSpec((tm, LANES), lambda i: (i, 0)), compiler_params=_params(("parallel",)),
    )(x, g_pre, scale1p, shift, w_f)


def _split3(v):
    hi = v.astype(BF).astype(F32)
    mid = (v - hi).astype(BF).astype(F32)
    lo = ((v - hi) - mid).astype(BF).astype(F32)
    return hi, mid, lo


def _in_proj(x, g_pre, scale1p, shift, w_rows, w_t_fox, cum, cos_t, sin_t, tm):
    s = x.shape[0]
    r_va, r_za, r_qb, r_zb, r_kb, r_vb = 0, 512, 1024, 1536, 2048, 2176

    def body(x_ref, g_ref, sc_ref, sh_ref, w_ref, wt_ref, cum_ref, cos_ref, sin_ref,
             h_ref, qat_ref, ka_ref, kat_ref, v_ref, vt_ref, za_ref, zb_ref, qb_ref, kb_ref, vb_ref,
             qbt_ref, kbt_ref, vbt_ref, mo_ref):
        hb = _modulated(x_ref, g_ref, sc_ref, sh_ref)
        h_ref[...] = hb

        def sec(c0, width):
            return _dot(hb, w_ref[:, c0:c0 + width])

        def sec_t(r0):
            return _dot_nt(wt_ref[r0:r0 + FOX_W, :], hb)

        q_t = sec_t(0) * Q_SCALE
        k_t = sec_t(FOX_W)
        v_t = sec_t(2 * FOX_W)
        va = sec(r_va, FOX_W)
        zeros = jnp.zeros((AUG_DIM - HEAD_DIM - AUG_ROWS, tm), F32)
        ri = lax.broadcasted_iota(jnp.int32, (AUG_ROWS, tm), 0)
        const = jnp.where(ri == AUG_ROWS - 1, 0.0, 1.0)
        ri_v = lax.broadcasted_iota(jnp.int32, (VT_ROWS - HEAD_DIM, tm), 0)
        v_feat = jnp.where(ri_v == 0, 1.0, 0.0).astype(BF)
        for hd in range(FOX_HEADS):
            rows = slice(hd * HEAD_DIM, (hd + 1) * HEAD_DIM)
            cum2 = cum_ref[hd:hd + 1, :] * LOG2E
            hi, mid, lo = (jnp.broadcast_to(part, (AUG_ROWS, tm)) for part in _split3(cum2))
            q_feat = jnp.where(ri == 1, hi, jnp.where(ri == 2, mid, jnp.where(ri == 3, lo, const)))
            k_feat = jnp.where(ri == 4, -hi, jnp.where(ri == 5, -mid, jnp.where(ri == 6, -lo, const)))
            q_aug = jnp.concatenate([q_t[rows], q_feat, zeros], axis=0)
            k_aug = jnp.concatenate([k_t[rows], k_feat, zeros], axis=0)
            mo_ref[hd:hd + 1, :] = jnp.sum(q_t[rows] * k_t[rows], axis=0, keepdims=True) + 1.0
            qat_ref[hd] = q_aug.astype(BF)
            kat_ref[hd] = k_aug.astype(BF)
            ka_ref[hd] = k_aug.T.astype(BF)
            vt_ref[hd] = jnp.concatenate([v_t[rows].astype(BF), v_feat], axis=0)
            v_ref[hd] = va[:, rows].astype(BF)
        za_ref[...] = sec(r_za, FOX_W)
        zb_ref[...] = sec(r_zb, SWA_W)
        cos2, sin2 = cos_ref[...], sin_ref[...]
        cos8 = jnp.concatenate([cos2] * 4, axis=1)
        sin8 = jnp.concatenate([sin2] * 4, axis=1)
        qb = sec(r_qb, SWA_W)
        qb = (qb * cos8 + _rope_partner(qb) * sin8) * (HEAD_DIM ** -0.5)
        qb_ref[...] = qb.astype(BF)
        for a in range(SWA_W // LANES):
            qbt_ref[a * LANES:(a + 1) * LANES, :] = qb[:, a * LANES:(a + 1) * LANES].T.astype(BF)
        kb = sec(r_kb, SWA_KV_W)
        kb = kb * cos2 + _rope_partner(kb) * sin2
        vb = sec(r_vb, SWA_KV_W)
        kb_t, vb_t = kb.T, vb.T
        for hd in range(SWA_KV_HEADS):
            sl = slice(hd * HEAD_DIM, (hd + 1) * HEAD_DIM)
            kb_ref[hd] = kb[:, sl].astype(BF)
            vb_ref[hd] = vb[:, sl].astype(BF)
            kbt_ref[hd] = kb_t[sl].astype(BF)
            vbt_ref[hd] = jnp.concatenate([vb_t[sl].astype(BF), v_feat], axis=0)

    row = lambda w: pl.BlockSpec((tm, w), lambda i: (i, 0))
    heads = lambda n, w=HEAD_DIM: pl.BlockSpec((n, tm, w), lambda i: (0, i, 0))
    heads_t = lambda w: pl.BlockSpec((FOX_HEADS, w, tm), lambda i: (0, 0, i))
    vec = _full((1, D_MODEL))
    hs = lambda a, b: jax.ShapeDtypeStruct((FOX_HEADS, a, b), BF)
    out_shape = (
        jax.ShapeDtypeStruct((s, D_MODEL), BF),
        hs(AUG_DIM, s), hs(s, AUG_DIM), hs(AUG_DIM, s), hs(s, HEAD_DIM), hs(VT_ROWS, s),
        jax.ShapeDtypeStruct((s, FOX_W), F32), jax.ShapeDtypeStruct((s, SWA_W), F32),
        jax.ShapeDtypeStruct((s, SWA_W), BF),
        jax.ShapeDtypeStruct((SWA_KV_HEADS, s, HEAD_DIM), BF), jax.ShapeDtypeStruct((SWA_KV_HEADS, s, HEAD_DIM), BF),
        jax.ShapeDtypeStruct((SWA_W, s), BF),
        jax.ShapeDtypeStruct((SWA_KV_HEADS, HEAD_DIM, s), BF), jax.ShapeDtypeStruct((SWA_KV_HEADS, VT_ROWS, s), BF),
        jax.ShapeDtypeStruct((FOX_HEADS, s), F32),
    )
    kv_t = lambda w: pl.BlockSpec((SWA_KV_HEADS, w, tm), lambda i: (0, 0, i))
    return _call(
        body, name="in_proj", out_shape=out_shape, grid=(s // tm,),
        in_specs=[row(D_MODEL), vec, vec, vec, _full(w_rows.shape), _full(w_t_fox.shape),
                  pl.BlockSpec((FOX_HEADS, tm), lambda i: (0, i)), row(LANES), row(LANES)],
        out_specs=(row(D_MODEL), heads_t(AUG_DIM), heads(FOX_HEADS, AUG_DIM), heads_t(AUG_DIM), heads(FOX_HEADS),
                   heads_t(VT_ROWS), row(FOX_W), row(SWA_W), row(SWA_W), heads(SWA_KV_HEADS), heads(SWA_KV_HEADS),
                   pl.BlockSpec((SWA_W, tm), lambda i: (0, i)), kv_t(HEAD_DIM), kv_t(VT_ROWS),
                   pl.BlockSpec((FOX_HEADS, tm), lambda i: (0, i))),
        compiler_params=_params(("parallel",)),
    )(x, g_pre, scale1p, shift, w_rows, w_t_fox, cum, cos_t, sin_t)


def _diag_chunks(d, bq, bk, chunk):
    out = []
    for c0 in range(0, bq, chunk):
        if d is None or d * bk + bk - 1 <= c0:
            out.append((c0, None, bk))
        elif d * bk <= c0 + chunk - 1:
            n_keys = min(bk, c0 + chunk - d * bk)
            kpos = d * bk + lax.broadcasted_iota(jnp.int32, (n_keys, chunk), 0)
            qpos = c0 + lax.broadcasted_iota(jnp.int32, (n_keys, chunk), 1)
            out.append((c0, kpos <= qpos, n_keys))
    return out


def _fox_fwd(qat, ka, vt, m_own, bq, bk, chunk, running_max):
    nh, _, s = qat.shape
    r = bq // bk

    pairs = [(i, j) for i in range(s // bq) for j in range(i * r + r)]

    def body(i_tab, j_tab, ka_ref, qat_ref, vt_ref, mo_ref, o_ref, lse_ref, bad_ref, *rest):
        pt_ref, m_scr, acc_scr = (None,) * running_max + rest
        i, j = i_tab[pl.program_id(1)], j_tab[pl.program_id(1)]

        @pl.when(j == 0)
        def _():
            m_scr[...] = jnp.full(m_scr.shape, NEG, F32) if running_max else mo_ref[0]
            acc_scr[...] = jnp.zeros(acc_scr.shape, F32)

        def careful(d):
            kv, vtv = ka_ref[0], vt_ref[0]

            def one_chunk(n, carry):
                c0 = pl.multiple_of(n * chunk, chunk)
                cs = pl.ds(c0, chunk)
                sc = _dot(kv, qat_ref[0, :, cs])
                if d is not None:
                    kpos = d * bk + lax.broadcasted_iota(jnp.int32, (bk, chunk), 0)
                    qpos = c0 + lax.broadcasted_iota(jnp.int32, (bk, chunk), 1)
                    sc = jnp.where(kpos <= qpos, sc, NEG)
                m_prev = m_scr[:, cs]
                m_new = jnp.maximum(m_prev, jnp.max(sc, axis=0, keepdims=True))
                p = jnp.exp2(sc - m_new).astype(BF)
                acc_scr[:, cs] = jnp.exp2(m_prev - m_new) * acc_scr[:, cs] + _dot(vtv, p)
                m_scr[:, cs] = m_new
                return carry

            lax.fori_loop(0, bq // chunk, one_chunk, 0)

        def fast(d):
            todo = _diag_chunks(d, bq, bk, chunk)
            scores = lambda t: _dot(ka_ref[0, :t[2], :], qat_ref[0, :, t[0]:t[0] + chunk])
            sc_next = scores(todo[0])
            for n, (c0, mask, n_keys) in enumerate(todo):
                cs = slice(c0, c0 + chunk)
                sc = sc_next
                if n + 1 < len(todo):
                    sc_next = scores(todo[n + 1])
                if mask is not None:
                    sc = jnp.where(mask, sc, NEG)
                p = jnp.exp2(sc - m_scr[:, cs]).astype(BF)
                pt_ref[0, :n_keys, cs] = p
                acc_scr[:, cs] += _dot(vt_ref[0, :, :n_keys], p)

        step = careful if running_max else fast

        @pl.when(j < i * r)
        def _():
            step(None)

        for d in range(r):
            @pl.when(j == i * r + d)
            def _(d=d):
                step(d)

        @pl.when(j == i * r + r - 1)
        def _():
            l = acc_scr[HEAD_DIM:HEAD_DIM + 1, :]
            o_ref[0] = acc_scr[:HEAD_DIM, :] / l
            lse_ref[0] = m_scr[...] + jnp.log2(l)
            bad_ref[0] = jnp.where(l < OVERFLOW_GUARD, 0.0, 1.0)

    qmap_t = lambda h, t, it, jt: (h, 0, it[t])
    qrow = pl.BlockSpec((1, 1, bq), qmap_t)
    row_shape = jax.ShapeDtypeStruct((nh, 1, s), F32)
    out_shape = (jax.ShapeDtypeStruct((nh, HEAD_DIM, s), F32), row_shape, row_shape)
    out_specs = (pl.BlockSpec((1, HEAD_DIM, bq), qmap_t), qrow, qrow)
    if not running_max:
        out_shape += (jax.ShapeDtypeStruct((nh, s, s), BF),)
        out_specs += (pl.BlockSpec((1, bk, bq), lambda h, t, it, jt: (h, jt[t], it[t])),)
    grid_spec = pltpu.PrefetchScalarGridSpec(
        num_scalar_prefetch=2, grid=(nh, len(pairs)),
        in_specs=[pl.BlockSpec((1, bk, AUG_DIM), lambda h, t, it, jt: (h, jt[t], 0)), pl.BlockSpec((1, AUG_DIM, bq), qmap_t),
                  pl.BlockSpec((1, VT_ROWS, bk), lambda h, t, it, jt: (h, 0, jt[t])), qrow],
        out_specs=out_specs,
        scratch_shapes=[pltpu.VMEM((1, bq), F32), pltpu.VMEM((VT_ROWS, bq), F32)])
    return _call(
        body, name="fox_fwd_running_max" if running_max else "fox_fwd", out_shape=out_shape, grid_spec=grid_spec,
        compiler_params=_params(("parallel", "arbitrary")),
    )(jnp.asarray([p[0] for p in pairs], jnp.int32), jnp.asarray([p[1] for p in pairs], jnp.int32), ka, qat, vt, m_own)


def _fox_bwd(qat, ka, kat, v, dot_, lse, delta, pt, bq, bk, chunk, dq_blk):
    nh, _, s = qat.shape
    r = bq // bk
    nq = s // bq
    stored = pt is not None

    pairs = [(j, i) for j in range(s // bk) for i in range(j // r, nq)]

    def body(j_tab, i_tab, a_ref, b_ref, kat_ref, v_ref, qat_ref, do_ref, dl_ref, dq_ref, dk_ref, dv_ref, dk_scr, dv_scr):
        ka_ref, lse_ref, pt_ref = (None, None, a_ref) if stored else (a_ref, b_ref, None)
        j, i = j_tab[pl.program_id(1)], i_tab[pl.program_id(1)]

        @pl.when(pl.program_id(1) == 0)
        def _():
            dq_ref[...] = jnp.zeros(dq_ref.shape, F32)

        @pl.when(i * r <= j)
        def _():
            dk_scr[...] = jnp.zeros(dk_scr.shape, F32)
            dv_scr[...] = jnp.zeros(dv_scr.shape, F32)

        def step(d):
            todo = _diag_chunks(d, bq, bk, chunk)

            def products(t):
                cs = slice(t[0], t[0] + chunk)
                return (None if stored else _dot(ka_ref[0, :t[2], :], qat_ref[0, :, cs]),
                        _dot(v_ref[0, :t[2], :], do_ref[0, :, cs]))

            nxt = products(todo[0])
            for n, (c0, mask, n_keys) in enumerate(todo):
                cs = slice(c0, c0 + chunk)
                sc, dp = nxt
                if n + 1 < len(todo):
                    nxt = products(todo[n + 1])
                if stored:
                    p_bf = pt_ref[0, :n_keys, cs]
                    p = p_bf.astype(F32)
                else:
                    p = jnp.exp2(sc - lse_ref[0, :, cs])
                    if mask is not None:
                        p = jnp.where(mask, p, 0.0)
                    p_bf = p.astype(BF)
                ds = (p * (dp - dl_ref[0, :, cs])).astype(BF)
                dv_scr[:, :n_keys] += _dot_nt(do_ref[0, :, cs], p_bf)
                dk_scr[:, :n_keys] += _dot_nt(qat_ref[0, :VT_ROWS, cs], ds)
                c1 = c0 % dq_blk
                dq_ref[0, i * (bq // dq_blk) + c0 // dq_blk, :, c1:c1 + chunk] += _dot(kat_ref[0, :VT_ROWS, :n_keys], ds)

        @pl.when(i * r > j)
        def _():
            step(None)

        for d in range(r):
            @pl.when(j == i * r + d)
            def _(d=d):
                step(d)

        @pl.when(i == nq - 1)
        def _():
            dk_ref[0] = dk_scr[...]
            dv_ref[0] = dv_scr[...]

    qmap = lambda h, t, jt, it: (h, 0, it[t])
    kmap = lambda h, t, jt, it: (h, jt[t], 0)
    kmap_t = lambda h, t, jt, it: (h, 0, jt[t])
    if stored:
        first = [(pt, pl.BlockSpec((1, bk, bq), lambda h, t, jt, it: (h, jt[t], it[t]))),
                 (delta, pl.BlockSpec((1, 1, bq), qmap))]
    else:
        first = [(ka, pl.BlockSpec((1, bk, AUG_DIM), kmap)), (lse, pl.BlockSpec((1, 1, bq), qmap))]
    grid_spec = pltpu.PrefetchScalarGridSpec(
        num_scalar_prefetch=2, grid=(nh, len(pairs)),
        in_specs=[first[0][1], first[1][1], pl.BlockSpec((1, AUG_DIM, bk), kmap_t), pl.BlockSpec((1, bk, HEAD_DIM), kmap),
                  pl.BlockSpec((1, AUG_DIM, bq), qmap), pl.BlockSpec((1, HEAD_DIM, bq), qmap),
                  pl.BlockSpec((1, 1, bq), qmap)],
        out_specs=(pl.BlockSpec((1, s // dq_blk, VT_ROWS, dq_blk), lambda h, t, jt, it: (h, 0, 0, 0)),
                   pl.BlockSpec((1, VT_ROWS, bk), kmap_t), pl.BlockSpec((1, HEAD_DIM, bk), kmap_t)),
        scratch_shapes=[pltpu.VMEM((VT_ROWS, bk), F32), pltpu.VMEM((HEAD_DIM, bk), F32)])
    return _call(
        body, name="fox_bwd" if stored else "fox_bwd_recompute",
        out_shape=(jax.ShapeDtypeStruct((nh, s // dq_blk, VT_ROWS, dq_blk), F32),
                   jax.ShapeDtypeStruct((nh, VT_ROWS, s), F32), jax.ShapeDtypeStruct((nh, HEAD_DIM, s), F32)),
        grid_spec=grid_spec, compiler_params=_params(("parallel", "arbitrary")),
    )(jnp.asarray([p[0] for p in pairs], jnp.int32), jnp.asarray([p[1] for p in pairs], jnp.int32),
      first[0][0], first[1][0], kat, v, qat, dot_, delta)


def _swa_mask(i, tq):
    kpos = i * tq - WINDOW + lax.broadcasted_iota(jnp.int32, (tq + WINDOW, tq), 0)
    qpos = i * tq + lax.broadcasted_iota(jnp.int32, (tq + WINDOW, tq), 1)
    rel = qpos - kpos
    return (rel >= 0) & (rel < WINDOW) & (kpos >= 0)


def _swa_rows(ref, i, tq):
    before = pl.multiple_of(jnp.maximum(i * tq - WINDOW, 0), WINDOW)
    return jnp.concatenate([ref[0, pl.ds(before, WINDOW), :], ref[0, pl.ds(pl.multiple_of(i * tq, tq), tq), :]], axis=0)


def _swa_before(n_rows, tq):
    return pl.BlockSpec((1, n_rows, WINDOW), lambda g, i: (g, 0, jnp.maximum(i * (tq // WINDOW) - 1, 0)))


def _swa_probs_t(sc, mask, sink):
    sc = jnp.where(mask, sc, NEG)
    m = jnp.maximum(jnp.max(sc, axis=0, keepdims=True), sink)
    p = jnp.exp(sc - m)
    e_sink = jnp.exp(sink - m)
    inv_l = 1.0 / (jnp.sum(p, axis=0, keepdims=True) + e_sink)
    return p * inv_l, e_sink * inv_l


def _swa_fwd(qbt, kb, vbt, sinks, tq):
    s = qbt.shape[1]
    gw = SWA_GROUP * HEAD_DIM

    def body(q_ref, k_ref, vb_ref, vc_ref, s_ref, o_ref):
        i = pl.program_id(1)
        mask = _swa_mask(i, tq)
        kw = _swa_rows(k_ref, i, tq)
        vtw = jnp.concatenate([vb_ref[0], vc_ref[0]], axis=1)
        sk = s_ref[0]
        scores = lambda hh: _dot(kw, q_ref[hh * HEAD_DIM:(hh + 1) * HEAD_DIM, :])
        sc_next = scores(0)
        for hh in range(SWA_GROUP):
            rows = slice(hh * HEAD_DIM, (hh + 1) * HEAD_DIM)
            sink = sk[:, hh:hh + 1]
            sc = jnp.where(mask, sc_next, NEG)
            if hh + 1 < SWA_GROUP:
                sc_next = scores(hh + 1)
            m = jnp.maximum(jnp.max(sc, axis=0, keepdims=True), sink)
            acc = _dot(vtw, jnp.exp(sc - m).astype(BF))
            o_ref[rows, :] = acc[:HEAD_DIM] / (acc[HEAD_DIM:HEAD_DIM + 1] + jnp.exp(sink - m))

    kvspec = pl.BlockSpec((1, s, HEAD_DIM), lambda g, i: (g, 0, 0))
    qspec = pl.BlockSpec((gw, tq), lambda g, i: (g, i))
    return _call(
        body, name="swa_fwd", out_shape=jax.ShapeDtypeStruct((SWA_W, s), F32), grid=(SWA_KV_HEADS, s // tq),
        in_specs=[qspec, kvspec, _swa_before(VT_ROWS, tq), pl.BlockSpec((1, VT_ROWS, tq), lambda g, i: (g, 0, i)),
                  pl.BlockSpec((1, 1, SWA_GROUP), lambda g, i: (g, 0, 0))],
        out_specs=qspec, compiler_params=_params(("parallel", "parallel")),
    )(qbt, kb, vbt, vbt, sinks)


def _swa_bwd(qb, qbt, kb, kbt, vb, sinks, dob, dobt, tq):
    s = qb.shape[0]
    gw = SWA_GROUP * HEAD_DIM

    def body(q_ref, qt_ref, k_ref, ktb_ref, ktc_ref, v_ref, s_ref, do_ref, dot_ref, dq_ref, dk_ref, dv_ref, ds_ref):
        i = pl.program_id(1)

        @pl.when(i == 0)
        def _():
            dk_ref[...] = jnp.zeros(dk_ref.shape, F32)
            dv_ref[...] = jnp.zeros(dv_ref.shape, F32)
            ds_ref[...] = jnp.zeros(ds_ref.shape, F32)

        mask = _swa_mask(i, tq)
        kw = _swa_rows(k_ref, i, tq)
        vw = _swa_rows(v_ref, i, tq)
        ktw = jnp.concatenate([ktb_ref[0], ktc_ref[0]], axis=1)
        qv, dov = q_ref[...], do_ref[...]
        sk = s_ref[0]
        dsinks = []
        dk_acc = jnp.zeros((tq + WINDOW, HEAD_DIM), F32)
        dv_acc = jnp.zeros((tq + WINDOW, HEAD_DIM), F32)
        def products(hh):
            rows = slice(hh * HEAD_DIM, (hh + 1) * HEAD_DIM)
            return _dot(kw, qt_ref[rows, :]), _dot(vw, dot_ref[rows, :])

        nxt = products(0)
        for hh in range(SWA_GROUP):
            rows = slice(hh * HEAD_DIM, (hh + 1) * HEAD_DIM)
            sc, dp = nxt
            if hh + 1 < SWA_GROUP:
                nxt = products(hh + 1)
            p, p_sink = _swa_probs_t(sc, mask, sk[:, hh:hh + 1])
            delta = jnp.sum(p * dp, axis=0, keepdims=True)
            dsc = (p * (dp - delta)).astype(BF)
            dq_ref[rows, :] = _dot(ktw, dsc)
            dk_acc = dk_acc + _dot(dsc, qv[:, rows])
            dv_acc = dv_acc + _dot(p.astype(BF), dov[:, rows])
            dsinks.append(-jnp.sum(p_sink * delta, axis=1, keepdims=True))
        before = pl.ds(pl.multiple_of(jnp.maximum(i * tq - WINDOW, 0), WINDOW), WINDOW)
        own = pl.ds(pl.multiple_of(i * tq, tq), tq)
        dk_ref[0, before, :] += dk_acc[:WINDOW]
        dk_ref[0, own, :] += dk_acc[WINDOW:]
        dv_ref[0, before, :] += dv_acc[:WINDOW]
        dv_ref[0, own, :] += dv_acc[WINDOW:]
        ds_ref[0] += jnp.concatenate(dsinks, axis=1)

    kvspec = pl.BlockSpec((1, s, HEAD_DIM), lambda g, i: (g, 0, 0))
    qspec = pl.BlockSpec((tq, gw), lambda g, i: (i, g))
    qspec_t = pl.BlockSpec((gw, tq), lambda g, i: (g, i))
    sspec = pl.BlockSpec((1, 1, SWA_GROUP), lambda g, i: (g, 0, 0))
    kvshape = jax.ShapeDtypeStruct((SWA_KV_HEADS, s, HEAD_DIM), F32)
    return _call(
        body, name="swa_bwd",
        out_shape=(jax.ShapeDtypeStruct((SWA_W, s), F32), kvshape, kvshape,
                   jax.ShapeDtypeStruct((SWA_KV_HEADS, 1, SWA_GROUP), F32)),
        grid=(SWA_KV_HEADS, s // tq),
        in_specs=[qspec, qspec_t, kvspec, _swa_before(HEAD_DIM, tq),
                  pl.BlockSpec((1, HEAD_DIM, tq), lambda g, i: (g, 0, i)), kvspec, sspec, qspec, qspec_t],
        out_specs=(qspec_t, kvspec, kvspec, sspec),
        compiler_params=_params(("parallel", "arbitrary")),
    )(qb, qbt, kb, kbt, kbt, vb, sinks, dob, dobt)


def _pairs_to_rows(ref, n_rows=HEAD_DIM):
    parts = []
    for a in range(0, FOX_HEADS, 2):
        parts.append(jnp.concatenate([ref[a][:n_rows], ref[a + 1][:n_rows]], axis=0).T)
    return jnp.concatenate(parts, axis=1)


def _blocks_to_rows(ref):
    return jnp.concatenate([ref[a:a + LANES, :].T for a in range(0, ref.shape[0], LANES)], axis=1)


def _out_proj(oat, za, obt, zb, x, tgt, w_out, w_out_t, gate, g_post, inv_l, tm):
    s = x.shape[0]

    def body(oat_ref, za_ref, obt_ref, zb_ref, x_ref, t_ref, w_ref, wt_ref, gate_ref, gp_ref, il_ref,
             dout_ref, doat_ref, dla_ref, dza_ref, dob_ref, dobt_ref, dzb_ref, gw_ref, dgate_ref, dgp_ref, loss_ref):
        i = pl.program_id(0)

        @pl.when(i == 0)
        def _():
            gw_ref[...] = jnp.zeros(gw_ref.shape, F32)
            dgate_ref[...] = jnp.zeros(dgate_ref.shape, F32)
            dgp_ref[...] = jnp.zeros(dgp_ref.shape, F32)
            loss_ref[...] = jnp.zeros(loss_ref.shape, F32)

        oa_v = _pairs_to_rows(oat_ref)
        ob_v = _blocks_to_rows(obt_ref)
        za_v, zb_v = za_ref[...], zb_ref[...]
        sga, sgb = _sigmoid(za_v), _sigmoid(zb_v)
        sila, silb = za_v * sga, zb_v * sgb
        u = jnp.concatenate([oa_v * sila, ob_v * silb], axis=1).astype(BF)
        yv = _dot(u, w_ref[...])
        yhat, rstd = _rms_hat(yv)
        gp, gate_v = gp_ref[...], gate_ref[...]
        nrm = yhat * gp
        diff = (x_ref[...] + gate_v * nrm) - t_ref[...]
        loss_ref[...] += 0.5 * jnp.sum(jnp.sum(diff * diff, axis=1, keepdims=True), axis=0, keepdims=True) / D_MODEL
        dout = diff * (1.0 / D_MODEL)
        dout_ref[...] = dout
        dgate_ref[...] += jnp.sum(dout * nrm, axis=0, keepdims=True)
        dn = dout * gate_v
        dgp_ref[...] += jnp.sum(dn * yhat, axis=0, keepdims=True)
        dyhat = dn * gp
        dy = (rstd * (dyhat - yhat * jnp.mean(dyhat * yhat, axis=1, keepdims=True))).astype(BF)
        gw_ref[...] += _dot_tn(u, dy)
        du = _dot(dy, wt_ref[...])
        dua, dub = du[:, :FOX_W], du[:, FOX_W:]
        doa = dua * sila
        for a in range(0, FOX_HEADS, 2):
            pair_t = doa[:, a * HEAD_DIM:(a + 2) * HEAD_DIM].T
            for hd, rows in ((a, slice(0, HEAD_DIM)), (a + 1, slice(HEAD_DIM, 2 * HEAD_DIM))):
                inv_l = il_ref[hd]
                doat_ref[hd] = (pair_t[rows] * inv_l).astype(BF)
                dla_ref[hd] = jnp.sum(pair_t[rows] * oat_ref[hd], axis=0, keepdims=True) * inv_l
        dob = dub * silb
        dob_ref[...] = dob.astype(BF)
        for a in range(0, SWA_W, LANES):
            dobt_ref[a:a + LANES, :] = dob[:, a:a + LANES].T.astype(BF)
        dza_ref[...] = (dua * oa_v * (sga * (1.0 + za_v * (1.0 - sga)))).astype(BF)
        dzb_ref[...] = (dub * ob_v * (sgb * (1.0 + zb_v * (1.0 - sgb)))).astype(BF)

    row = lambda w: pl.BlockSpec((tm, w), lambda i: (i, 0))
    heads_t = lambda w: pl.BlockSpec((FOX_HEADS, w, tm), lambda i: (0, 0, i))
    vec = _full((1, D_MODEL))
    mat = _full((D_MODEL, D_MODEL))
    out_shape = (
        jax.ShapeDtypeStruct((s, D_MODEL), F32),
        jax.ShapeDtypeStruct((FOX_HEADS, HEAD_DIM, s), BF), jax.ShapeDtypeStruct((FOX_HEADS, 1, s), F32),
        jax.ShapeDtypeStruct((s, FOX_W), BF), jax.ShapeDtypeStruct((s, SWA_W), BF), jax.ShapeDtypeStruct((SWA_W, s), BF),
        jax.ShapeDtypeStruct((s, SWA_W), BF),
        jax.ShapeDtypeStruct((D_MODEL, D_MODEL), F32),
        jax.ShapeDtypeStruct((1, D_MODEL), F32), jax.ShapeDtypeStruct((1, D_MODEL), F32),
        jax.ShapeDtypeStruct((1, 1), F32),
    )
    col = pl.BlockSpec((SWA_W, tm), lambda i: (0, i))
    return _call(
        body, name="out_proj", out_shape=out_shape, grid=(s // tm,),
        in_specs=[heads_t(HEAD_DIM), row(FOX_W), col, row(SWA_W), row(D_MODEL), row(D_MODEL), mat, mat, vec, vec,
                  heads_t(1)],
        out_specs=(row(D_MODEL), heads_t(HEAD_DIM), heads_t(1), row(FOX_W), row(SWA_W), col, row(SWA_W), mat, vec, vec,
                   _full((1, 1))),
        compiler_params=_params(("arbitrary",)),
    )(oat, za, obt, zb, x, tgt, w_out, w_out_t, gate, g_post, inv_l)


def _assemble_dproj(dqt, dkt, dvt, dza, dqb, dzb, dkb, dvb, df, cos_t, sin_t, tm):
    s = dza.shape[0]

    def body(dqt_ref, dkt_ref, dvt_ref, dza_ref, dqb_ref, dzb_ref, dkb_ref, dvb_ref, df_ref, cos_ref, sin_ref, o_ref):
        def cat(ref, n):
            return jnp.concatenate([ref[hd] for hd in range(n)], axis=1)

        cos2, sin2 = cos_ref[...], sin_ref[...]
        cos8 = jnp.concatenate([cos2] * 4, axis=1)
        sin8 = jnp.concatenate([sin2] * 4, axis=1)
        scale = HEAD_DIM ** -0.5
        o_ref[:, C_QA:C_QA + FOX_W] = (_pairs_to_rows(dqt_ref.at[:, 0]) * scale).astype(BF)
        o_ref[:, C_KA:C_KA + FOX_W] = (_pairs_to_rows(dkt_ref) * LN2).astype(BF)
        o_ref[:, C_VA:C_VA + FOX_W] = _pairs_to_rows(dvt_ref).astype(BF)
        o_ref[:, C_ZA:C_ZA + FOX_W] = dza_ref[...]
        dq = _blocks_to_rows(dqb_ref) * scale
        o_ref[:, C_QB:C_QB + SWA_W] = (dq * cos8 - _rope_partner(dq) * sin8).astype(BF)
        o_ref[:, C_ZB:C_ZB + SWA_W] = dzb_ref[...]
        dk = cat(dkb_ref, SWA_KV_HEADS)
        o_ref[:, C_KB:C_KB + SWA_KV_W] = (dk * cos2 - _rope_partner(dk) * sin2).astype(BF)
        o_ref[:, C_VB:C_VB + SWA_KV_W] = cat(dvb_ref, SWA_KV_HEADS).astype(BF)
        o_ref[:, C_F:C_F + LANES] = df_ref[...].astype(BF)

    row = lambda w: pl.BlockSpec((tm, w), lambda i: (i, 0))
    heads = lambda n: pl.BlockSpec((n, tm, HEAD_DIM), lambda i: (0, i, 0))
    heads_t = lambda w: pl.BlockSpec((FOX_HEADS, w, tm), lambda i: (0, 0, i))
    return _call(
        body, name="assemble_dproj", out_shape=jax.ShapeDtypeStruct((s, WP), BF), grid=(s // tm,),
        in_specs=[pl.BlockSpec((FOX_HEADS, 1, VT_ROWS, tm), lambda i: (0, i, 0, 0)), heads_t(VT_ROWS), heads_t(HEAD_DIM),
                  row(FOX_W), pl.BlockSpec((SWA_W, tm), lambda i: (0, i)), row(SWA_W), heads(SWA_KV_HEADS),
                  heads(SWA_KV_HEADS), row(LANES), row(LANES), row(LANES)],
        out_specs=row(WP), compiler_params=_params(("parallel",)),
    )(dqt, dkt, dvt, dza, dqb, dzb, dkb, dvb, df, cos_t, sin_t)


def _in_proj_bwd_x(dproj, w_al_t, x, dout, g_pre, scale1p, tm, parts):
    s = x.shape[0]
    n_steps = s // tm
    masks = [(1, 0), (0, 1), (1, 1)]

    def body(dp_ref, wt_ref, x_ref, dout_ref, g_ref, sc_ref, parts_ref, gx_ref, dsh_ref, dsc_ref, dg_ref, got_ref,
             send_sems, recv_sems, local_sem):
        i = pl.program_id(0)
        cx, cy, cc = lax.axis_index("x"), lax.axis_index("y"), lax.axis_index("c")
        me = 2 * cx + cy
        own = pltpu.make_async_copy(parts_ref.at[me], got_ref.at[me], local_sem)

        def copy(k, send):
            dx, dy = masks[k]
            peer = 2 * (cx ^ dx) + (cy ^ dy)
            return pltpu.make_async_remote_copy(
                src_ref=parts_ref.at[peer if send else me], dst_ref=got_ref.at[me if send else peer],
                send_sem=send_sems.at[k], recv_sem=recv_sems.at[k], device_id=(cx ^ dx, cy ^ dy, cc), device_id_type=MESH)

        @pl.when(i == 0)
        def _():
            dsh_ref[...] = jnp.zeros(dsh_ref.shape, F32)
            dsc_ref[...] = jnp.zeros(dsc_ref.shape, F32)
            dg_ref[...] = jnp.zeros(dg_ref.shape, F32)
            own.start()
            for k in range(len(masks)):
                copy(k, True).start()

        @pl.when(i == n_steps - 1)
        def _():
            for k in range(len(masks)):
                copy(k, False).wait_recv()
            for k in range(len(masks)):
                copy(k, True).wait_send()
            own.wait()

        dh = _dot(dp_ref[...], wt_ref[...])
        xhat, rstd = _rms_hat(x_ref[...])
        g, sc = g_ref[...], sc_ref[...]
        dsh_ref[...] += jnp.sum(dh, axis=0, keepdims=True)
        dhx = dh * xhat
        dsc_ref[...] += jnp.sum(dhx * g, axis=0, keepdims=True)
        dg_ref[...] += jnp.sum(dhx * sc, axis=0, keepdims=True)
        dxhat = dh * (g * sc)
        gx_ref[...] = dout_ref[...] + rstd * (dxhat - xhat * jnp.mean(dxhat * xhat, axis=1, keepdims=True))

    row = lambda w: pl.BlockSpec((tm, w), lambda i: (i, 0))
    vec = _full((1, D_MODEL))
    vshape = jax.ShapeDtypeStruct((1, D_MODEL), F32)
    hbm = pl.BlockSpec(memory_space=pl.ANY)
    return _call(
        body, name="in_proj_bwd_x",
        out_shape=(jax.ShapeDtypeStruct((s, D_MODEL), F32), vshape, vshape, vshape,
                   jax.ShapeDtypeStruct(parts.shape, parts.dtype)),
        grid=(n_steps,),
        in_specs=[row(WP), _full((WP, D_MODEL)), row(D_MODEL), row(D_MODEL), vec, vec, hbm],
        out_specs=(row(D_MODEL), vec, vec, vec, hbm),
        scratch_shapes=[pltpu.SemaphoreType.DMA((3,)), pltpu.SemaphoreType.DMA((3,)), pltpu.SemaphoreType.DMA],
        compiler_params=_params(("arbitrary",), has_side_effects=True),
    )(dproj, w_al_t, x, dout, g_pre, scale1p, parts)


def _in_proj_bwd_w(h, dproj, tk, tn):
    s = h.shape[0]
    n_k = s // tk

    def body(h_ref, dp_ref, gw_ref, acc_scr):
        k = pl.program_id(1)

        @pl.when(k == 0)
        def _():
            acc_scr[...] = jnp.zeros(acc_scr.shape, F32)

        acc_scr[...] += _dot_tn(h_ref[...], dp_ref[...])

        @pl.when(k == n_k - 1)
        def _():
            gw_ref[...] = acc_scr[...].astype(BF)

    return _call(
        body, name="in_proj_bwd_w", out_shape=jax.ShapeDtypeStruct((D_MODEL, WP), BF), grid=(WP // tn, n_k),
        in_specs=[pl.BlockSpec((tk, D_MODEL), lambda n, k: (k, 0)), pl.BlockSpec((tk, tn), lambda n, k: (k, n))],
        out_specs=pl.BlockSpec((D_MODEL, tn), lambda n, k: (0, n)),
        scratch_shapes=[pltpu.VMEM((D_MODEL, tn), F32)],
        compiler_params=_params(("parallel", "arbitrary")),
    )(h, dproj)


def _align_w_in(w_cols):
    def part(name, width):
        return w_cols[:, _SRC[name]:_SRC[name] + width]

    fpad = jnp.pad(part("fa", FOX_HEADS), ((0, 0), (0, LANES - FOX_HEADS)))
    return jnp.concatenate([part("qa", FOX_W), part("ka", FOX_W), part("va", FOX_W), part("za", FOX_W),
                            part("qb", SWA_W), part("zb", SWA_W), part("kb", SWA_KV_W), part("vb", SWA_KV_W), fpad], axis=1)


def _unalign_w_in(g_al):
    def part(c0, width):
        return g_al[:, c0:c0 + width]

    return jnp.concatenate([part(C_QA, FOX_W), part(C_KA, FOX_W), part(C_VA, FOX_W), part(C_F, FOX_HEADS),
                            part(C_ZA, FOX_W), part(C_QB, SWA_W), part(C_KB, SWA_KV_W), part(C_VB, SWA_KV_W),
                            part(C_ZB, SWA_W)], axis=1)


def _rope_tables(positions):
    inv_freq = ROPE_THETA ** (-jnp.arange(HALF, dtype=F32) / HALF)
    ang = positions.astype(F32)[:, None] * inv_freq
    cos, sin = jnp.cos(ang), jnp.sin(ang)
    return jnp.concatenate([cos, cos, cos, cos], axis=1), jnp.concatenate([-sin, sin, -sin, sin], axis=1)


def _tiles(s):
    if s >= 4096:
        return dict(tm=512, blk=512, bq=2048, bk=2048, bk_bwd=2048, chunk=256, tq=256, tm_out=512, tk=1024, tn=1152)
    return dict(tm=128, blk=128, bq=256, bk=256, bk_bwd=256, chunk=128, tq=128, tm_out=128, tk=128, tn=1152)


def kernel(x, c, positions, w_ada, b_ada, g_pre, w_in, b_fgate, sinks, w_out, g_post, loss_target, m_w_ada, m_b_ada, m_g_pre, m_w_in, m_b_fgate, m_sinks, m_w_out, m_g_post, v_w_ada, v_b_ada, v_g_pre, v_w_in, v_b_fgate, v_sinks, v_w_out, v_g_post):
    s = x.shape[1]
    t = _tiles(s)
    nc = s // LANES
    rows = FOX_HEADS * nc
    me = 4 * lax.axis_index("x") + 2 * lax.axis_index("y") + lax.axis_index("c")
    chip = 2 * lax.axis_index("x") + lax.axis_index("y")
    core = lax.axis_index("c")
    x2, tgt = x[0], loss_target[0]

    c_all = _allgather_devices(c, "gather_c")[:, 0, :]
    a_all, mod_shard = _ada_shard(c_all, w_ada[0])
    mod_all = _allgather_devices(mod_shard, "gather_mod")
    mod_rows = lax.dynamic_index_in_dim(mod_all, me, axis=1, keepdims=False)
    mod = mod_rows.reshape(N_CHIPS, 2, W_ADA_SHARD)[:, 0, :].reshape(1, 3 * D_MODEL) + b_ada
    shift, scale1p, gate = mod[:, :D_MODEL], 1.0 + mod[:, D_MODEL:2 * D_MODEL], mod[:, 2 * D_MODEL:]

    w_in_pad = jnp.pad(w_in[0].astype(BF), ((0, 0), (0, W_IN_SHARD_PAD - W_IN_SHARD)))
    w_pack = jnp.concatenate([w_in_pad, w_out[0].astype(BF).reshape(D_MODEL, W_OUT_SHARD)], axis=1)
    w_all = _allgather_chips(w_pack.reshape(2, D_MODEL // 2, -1), "gather_weights").reshape(N_CHIPS, D_MODEL, -1)
    w_cols = jnp.concatenate([w_all[k, :, :W_IN_SHARD] for k in range(N_CHIPS)], axis=1)
    w_al = _align_w_in(w_cols)
    w_al_t = w_al.T
    w_out_all = w_all[:, :, W_IN_SHARD_PAD:].reshape(D_MODEL, D_MODEL)
    w_out_t = w_out_all.T

    cos_t, sin_t = _rope_tables(positions[0])

    f_pad = _forget_logits(x2, g_pre, scale1p, shift, w_al[:, C_F:], t["tk"])
    f_rows = f_pad[:, :FOX_HEADS].T.reshape(rows, LANES)
    bias_rows = jnp.repeat(b_fgate[0], nc)[:, None]
    cum = _log_forget_cumsum(f_rows, bias_rows, nc).reshape(FOX_HEADS, s)
    h, qat, ka, kat, va, vat, za, zb, qb, kb, vb, qbt, kbt, vbt, m_own = _in_proj(
        x2, g_pre, scale1p, shift, w_al[:, C_VA:C_F], w_al_t[:C_ZA], cum, cos_t, sin_t, t["tm"])
    m_own = m_own[:, None, :]
    fox_args = (qat, ka, vat, m_own, t["bq"], t["bk"], t["chunk"])
    oat, lse, bad, pt = _fox_fwd(*fox_args, running_max=False)
    overflowed = jnp.max(bad) > 0.0
    oat, lse = lax.cond(overflowed, lambda: _fox_fwd(*fox_args, running_max=True)[:2], lambda: (oat, lse))
    inv_l = jnp.where(overflowed, 1.0, jnp.exp2(m_own - lse))
    sinks_g = sinks.reshape(SWA_KV_HEADS, 1, SWA_GROUP)
    obt = _swa_fwd(qbt, kb, vbt, sinks_g, t["tq"])

    dout, doat, delta_a, dza, dob, dobt, dzb, gw_out, dgate, dg_post, loss_part = _out_proj(
        oat, za, obt, zb, x2, tgt, w_out_all, w_out_t, gate, g_post, inv_l, t["tm_out"])

    bwd_args = (qat, ka, kat, va, doat, lse, delta_a)
    bwd_tiles = (t["bq"], t["bk_bwd"], t["chunk"], t["blk"])
    dqt, dkt, dvt = lax.cond(overflowed, lambda: _fox_bwd(*bwd_args, None, *bwd_tiles),
                             lambda: _fox_bwd(*bwd_args, pt, *bwd_tiles))
    dcum = dqt[:, :, HEAD_DIM, :].reshape(FOX_HEADS, s) - dkt[:, HEAD_DIM, :]
    df_rows, db_heads = _log_forget_cumsum_bwd(dcum.reshape(rows, LANES), f_rows, bias_rows, nc)
    df_pad = jnp.pad(df_rows.reshape(FOX_HEADS, s).T, ((0, 0), (0, LANES - FOX_HEADS)))
    dqb, dkb, dvb, dsinks = _swa_bwd(qb, qbt, kb, kbt, vb, sinks_g, dob, dobt, t["tq"])

    dproj = _assemble_dproj(dqt, dkt, dvt, dza, dqb, dzb, dkb, dvb, df_pad, cos_t, sin_t, t["blk"])
    gw_in = _unalign_w_in(_in_proj_bwd_w(h, dproj, t["tk"], t["tn"]))

    gin = jnp.stack([jnp.pad(gw_in[:, k * W_IN_SHARD:(k + 1) * W_IN_SHARD], ((0, 0), (0, W_IN_SHARD_PAD - W_IN_SHARD)))
                     for k in range(N_CHIPS)])
    gout = gw_out.astype(BF).reshape(N_CHIPS, D_MODEL, W_OUT_SHARD)
    gbig = jnp.concatenate([gin, gout], axis=2)
    half = D_MODEL // 2
    gw = W_IN_SHARD_PAD + W_OUT_SHARD
    keep = lax.dynamic_slice_in_dim(gbig, core * half, half, axis=1)
    give = lax.dynamic_slice_in_dim(gbig, (1 - core) * half, half, axis=1)
    got = _swap_sibling(give.reshape(N_CHIPS * half, gw), "swap_grad_halves")
    pair = _add(keep.reshape(N_CHIPS * half, gw), got, "add_pair", BF).reshape(N_CHIPS, half, gw)
    grad_x, dshift, dscale, dg_pre, from_chips = _in_proj_bwd_x(
        dproj, w_al_t, x2, dout, g_pre, scale1p, t["tm_out"], pair)

    pad_lane = lambda vrow: jnp.pad(vrow, ((0, 0), (0, LANES - vrow.shape[1])))
    packed = jnp.concatenate([dshift, dscale, dgate, dg_pre, dg_post,
                              pad_lane(db_heads.reshape(1, FOX_HEADS)), pad_lane(dsinks.reshape(1, FOX_HEADS)),
                              pad_lane(loss_part)], axis=1)
    parts = _allgather_devices(packed, "gather_partials")
    tot = _sum_devices(parts)
    loss = tot[0, P_LOSS]
    g_b_ada = tot[:, P_DMOD:P_DMOD + 3 * D_MODEL]
    g_g_pre = tot[:, P_GPRE:P_GPRE + D_MODEL]
    g_g_post = tot[:, P_GPOST:P_GPOST + D_MODEL]
    g_b_fgate = tot[:, P_BF:P_BF + FOX_HEADS]
    g_sinks = tot[:, P_SINK:P_SINK + FOX_HEADS]
    dm_shard = lax.dynamic_slice_in_dim(parts[:, 0, :3 * D_MODEL], chip * W_ADA_SHARD, W_ADA_SHARD, axis=1)
    g_w_ada = _grad_w_ada(a_all.T, dm_shard)

    mine = _sum_chips(from_chips, "sum_chips")
    other = _swap_sibling(mine, "swap_grad_result")
    lo = jnp.where(core == 0, mine, other)
    hi = jnp.where(core == 0, other, mine)
    gfull = jnp.concatenate([lo, hi], axis=0)
    g_w_in = gfull[:, :W_IN_SHARD]
    g_w_out = gfull[:, W_IN_SHARD_PAD:].reshape(W_OUT_SHARD, D_MODEL)

    grads = dict(w_ada=g_w_ada, b_ada=g_b_ada, g_pre=g_g_pre, w_in=g_w_in, b_fgate=g_b_fgate, sinks=g_sinks,
                 w_out=g_w_out, g_post=g_g_post)
    weights = dict(w_ada=w_ada, b_ada=b_ada, g_pre=g_pre, w_in=w_in, b_fgate=b_fgate, sinks=sinks, w_out=w_out, g_post=g_post)
    moms = dict(w_ada=m_w_ada, b_ada=m_b_ada, g_pre=m_g_pre, w_in=m_w_in, b_fgate=m_b_fgate, sinks=m_sinks, w_out=m_w_out, g_post=m_g_post)
    vars_ = dict(w_ada=v_w_ada, b_ada=v_b_ada, g_pre=v_g_pre, w_in=v_w_in, b_fgate=v_b_fgate, sinks=v_sinks, w_out=v_w_out, g_post=v_g_post)
    names = ["w_ada", "b_ada", "g_pre", "w_in", "b_fgate", "sinks", "w_out", "g_post"]
    g_out, d_out, m_out, v_out = [], [], [], []
    for n in names:
        if n == "w_in":
            flat = lambda a: jnp.transpose(a, (2, 0, 1)).reshape(W_IN_SHARD * D_MODEL // LANES, LANES)
            unflat = lambda a: jnp.transpose(a.reshape(W_IN_SHARD, 1, D_MODEL), (1, 2, 0))
            outs = _adamw(flat(w_in), flat(grads[n][None]), flat(moms[n]), flat(vars_[n]), "adamw_" + n)
            go, d, nm, nv = (unflat(a) for a in outs)
        else:
            g2 = grads[n].reshape(weights[n].shape[-2:])
            go, d, nm, nv = _adamw(weights[n], g2, moms[n], vars_[n], "adamw_" + n)
        g_out.append(go)
        d_out.append(d)
        m_out.append(nm)
        v_out.append(nv)
    return (loss, grad_x.reshape(x.shape), *g_out, *d_out, *m_out, *v_out)
```

```python
import jax
import jax.numpy as jnp
from jax import lax
from jax.experimental import pallas as pl
from jax.experimental.pallas import tpu as pltpu

_INTERPRET = False

D_MODEL = 1024
HEAD_DIM = 64
HALF = HEAD_DIM // 2
AUG_DIM = 128
AUG_ROWS = 8
VT_ROWS = 80
LOG2E = 1.4426950408889634
LN2 = 0.6931471805599453
Q_SCALE = LOG2E * 64 ** -0.5
FOX_HEADS = 8
FOX_W = 512
SWA_W = 512
SWA_KV_HEADS = 2
SWA_GROUP = 4
SWA_KV_W = 128
WINDOW = 128
ROPE_THETA = 10000.0
RMS_EPS = 1e-6
IN_WIDTH = 3336
N_CHIPS = 4
N_DEV = 8
W_IN_SHARD = IN_WIDTH // N_CHIPS
W_IN_SHARD_PAD = 896
W_ADA_SHARD = 3 * D_MODEL // N_CHIPS
W_OUT_SHARD = D_MODEL // N_CHIPS
LANES = 128

_SRC = dict(qa=0, ka=512, va=1024, fa=1536, za=1544, qb=2056, kb=2568, vb=2696, zb=2824)
C_QA, C_KA, C_VA, C_ZA, C_QB, C_ZB, C_KB, C_VB, C_F = 0, 512, 1024, 1536, 2048, 2560, 3072, 3200, 3328
WP = 3456

ADAM_LR = 0.001
ADAM_B1 = 0.9
ADAM_B2 = 0.999
ADAM_EPS = 1e-08
ADAM_WD = 0.01
ADAM_STEP = 10
ADAMW_BLOCK_ELEMS = 300_000

VMEM_LIMIT = 56 * 1024 * 1024
NEG = -1e30
OVERFLOW_GUARD = 1e30
MESH = pl.DeviceIdType.MESH
BF = jnp.bfloat16
F32 = jnp.float32

P_DMOD, P_GPRE, P_GPOST, P_BF, P_SINK, P_LOSS, P_LEN = 0, 3072, 4096, 5120, 5248, 5376, 5504


def _call(body, **kw):
    return pl.pallas_call(body, interpret=_INTERPRET, **kw)


def _params(sem=None, **kw):
    return pltpu.CompilerParams(dimension_semantics=sem, vmem_limit_bytes=VMEM_LIMIT, **kw)


def _full(shape):
    zeros = (0,) * len(shape)
    return pl.BlockSpec(shape, lambda *_: zeros)


def _dot(a, b):
    return jnp.dot(a, b, preferred_element_type=F32)


def _dot_nt(a, b):
    return lax.dot_general(a, b, (((1,), (1,)), ((), ())), preferred_element_type=F32)


def _dot_tn(a, b):
    return lax.dot_general(a, b, (((0,), (0,)), ((), ())), preferred_element_type=F32)


def _sigmoid(z):
    return 1.0 / (1.0 + jnp.exp(-z))


def _rope_partner(t):
    w = t.shape[-1]
    lane = lax.broadcasted_iota(jnp.int32, t.shape, t.ndim - 1)
    return jnp.where((lane & (HEAD_DIM - 1)) < HALF, pltpu.roll(t, w - HALF, t.ndim - 1), pltpu.roll(t, HALF, t.ndim - 1))


def _allgather_devices(v, name):
    r, cdim = v.shape
    masks = [(dx, dy, dc) for dx in (0, 1) for dy in (0, 1) for dc in (0, 1)][1:]

    def body(v_ref, out_ref, send_sems, recv_sems):
        x, y, c = lax.axis_index("x"), lax.axis_index("y"), lax.axis_index("c")
        me = 4 * x + 2 * y + c
        out_ref[me] = v_ref[...]
        copies = []
        for k, (dx, dy, dc) in enumerate(masks):
            cp = pltpu.make_async_remote_copy(
                src_ref=v_ref, dst_ref=out_ref.at[me], send_sem=send_sems.at[k], recv_sem=recv_sems.at[k],
                device_id=(x ^ dx, y ^ dy, c ^ dc), device_id_type=MESH)
            cp.start()
            copies.append(cp)
        for k, (dx, dy, dc) in enumerate(masks):
            peer = 4 * (x ^ dx) + 2 * (y ^ dy) + (c ^ dc)
            pltpu.make_async_remote_copy(
                src_ref=v_ref, dst_ref=out_ref.at[peer], send_sem=send_sems.at[k], recv_sem=recv_sems.at[k],
                device_id=(x ^ dx, y ^ dy, c ^ dc), device_id_type=MESH).wait_recv()
        for cp in copies:
            cp.wait_send()

    return _call(
        body, name=name, out_shape=jax.ShapeDtypeStruct((N_DEV, r, cdim), v.dtype),
        in_specs=[pl.BlockSpec(memory_space=pltpu.VMEM)], out_specs=pl.BlockSpec(memory_space=pltpu.VMEM),
        scratch_shapes=[pltpu.SemaphoreType.DMA((7,)), pltpu.SemaphoreType.DMA((7,))],
        compiler_params=pltpu.CompilerParams(has_side_effects=True),
    )(v)


def _allgather_chips(v, name):
    _, r, cdim = v.shape
    masks = [(1, 0), (0, 1), (1, 1)]
    n = len(masks)

    def body(v_ref, out_ref, send_sems, recv_sems, local_sem):
        x, y, c = lax.axis_index("x"), lax.axis_index("y"), lax.axis_index("c")
        me = 2 * x + y
        mine = pltpu.make_async_copy(v_ref, out_ref.at[me], local_sem)
        mine.start()

        def copy(k, chip, half, to):
            return pltpu.make_async_remote_copy(
                src_ref=v_ref.at[half] if k < n else out_ref.at[chip, half], dst_ref=out_ref.at[chip, half],
                send_sem=send_sems.at[k], recv_sem=recv_sems.at[k], device_id=to, device_id_type=MESH)

        first = [copy(k, me, c, (x ^ dx, y ^ dy, c)) for k, (dx, dy) in enumerate(masks)]
        for cp in first:
            cp.start()
        passed = []
        for k, (dx, dy) in enumerate(masks):
            peer = 2 * (x ^ dx) + (y ^ dy)
            copy(k, peer, c, (x, y, c)).wait_recv()
            cp = copy(n + k, peer, c, (x, y, 1 - c))
            cp.start()
            passed.append(cp)
        for k, (dx, dy) in enumerate(masks):
            copy(n + k, 2 * (x ^ dx) + (y ^ dy), 1 - c, (x, y, c)).wait_recv()
        for cp in first + passed:
            cp.wait_send()
        mine.wait()

    return _call(
        body, name=name, out_shape=jax.ShapeDtypeStruct((N_CHIPS, 2, r, cdim), v.dtype),
        in_specs=[pl.BlockSpec(memory_space=pl.ANY)], out_specs=pl.BlockSpec(memory_space=pl.ANY),
        scratch_shapes=[pltpu.SemaphoreType.DMA((2 * n,)), pltpu.SemaphoreType.DMA((2 * n,)), pltpu.SemaphoreType.DMA],
        compiler_params=pltpu.CompilerParams(has_side_effects=True),
    )(v)


def _swap_sibling(v, name):
    def body(v_ref, out_ref, send_sem, recv_sem):
        x, y, c = lax.axis_index("x"), lax.axis_index("y"), lax.axis_index("c")
        cp = pltpu.make_async_remote_copy(
            src_ref=v_ref, dst_ref=out_ref, send_sem=send_sem, recv_sem=recv_sem,
            device_id=(x, y, 1 - c), device_id_type=MESH)
        cp.start()
        cp.wait()

    return _call(
        body, name=name, out_shape=jax.ShapeDtypeStruct(v.shape, v.dtype),
        in_specs=[pl.BlockSpec(memory_space=pl.ANY)], out_specs=pl.BlockSpec(memory_space=pl.ANY),
        scratch_shapes=[pltpu.SemaphoreType.DMA, pltpu.SemaphoreType.DMA],
        compiler_params=pltpu.CompilerParams(has_side_effects=True),
    )(v)


def _ada_shard(c_all, w_ada_shard):
    def body(c_ref, w_ref, a_ref, mod_ref):
        cv = c_ref[...]
        a = cv * _sigmoid(cv)
        a_ref[...] = a
        mod_ref[...] = _dot(a.astype(BF), w_ref[...].astype(BF))

    return _call(
        body, name="ada_shard",
        out_shape=(jax.ShapeDtypeStruct((N_DEV, D_MODEL), F32), jax.ShapeDtypeStruct((N_DEV, W_ADA_SHARD), F32)),
        compiler_params=_params(),
    )(c_all, w_ada_shard)


def _grad_w_ada(a_t, dm_shard):
    def body(a_ref, dm_ref, out_ref):
        acc = jnp.zeros((D_MODEL, W_ADA_SHARD), F32)
        for b in range(N_DEV):
            acc = acc + a_ref[:, b:b + 1] * dm_ref[b:b + 1, :]
        out_ref[...] = acc

    return _call(body, name="grad_w_ada", out_shape=jax.ShapeDtypeStruct((D_MODEL, W_ADA_SHARD), F32),
                 compiler_params=_params())(a_t, dm_shard)


def _sum_devices(parts):
    n = parts.shape[-1]

    def body(p_ref, out_ref):
        acc = p_ref[0]
        for b in range(1, N_DEV):
            acc = acc + p_ref[b]
        out_ref[...] = acc

    return _call(body, name="sum_devices", out_shape=jax.ShapeDtypeStruct((1, n), F32), compiler_params=_params())(parts)


def _add(a, b, name, out_dtype):
    r, cdim = a.shape
    tr = min(r, 256)

    def body(a_ref, b_ref, o_ref):
        o_ref[...] = (a_ref[...].astype(F32) + b_ref[...].astype(F32)).astype(out_dtype)

    spec = pl.BlockSpec((tr, cdim), lambda i: (i, 0))
    return _call(body, name=name, out_shape=jax.ShapeDtypeStruct(a.shape, out_dtype), grid=(r // tr,),
                 in_specs=[spec, spec], out_specs=spec, compiler_params=_params(("parallel",)))(a, b)


def _sum_chips(parts, name):
    _, r, cdim = parts.shape
    tr = min(r, 128)

    def body(p_ref, o_ref):
        o_ref[...] = ((p_ref[0].astype(F32) + p_ref[1].astype(F32)) + p_ref[2].astype(F32)) + p_ref[3].astype(F32)

    return _call(body, name=name, out_shape=jax.ShapeDtypeStruct((r, cdim), F32), grid=(r // tr,),
                 in_specs=[pl.BlockSpec((N_CHIPS, tr, cdim), lambda i: (0, i, 0))],
                 out_specs=pl.BlockSpec((tr, cdim), lambda i: (i, 0)), compiler_params=_params(("parallel",)))(parts)


def _adamw(w, g, m, v, name):
    r, cdim = w.shape[-2:]
    lead = w.ndim - 2
    tr = r if r <= 256 else max(t for t in range(8, ADAMW_BLOCK_ELEMS // cdim + 1, 8) if r % t == 0)
    c1 = 1.0 / (1.0 - ADAM_B1 ** ADAM_STEP)
    c2 = 1.0 / (1.0 - ADAM_B2 ** ADAM_STEP)

    def body(w_ref, g_ref, m_ref, v_ref, go_ref, d_ref, nm_ref, nv_ref):
        gv = g_ref[...].reshape(go_ref.shape)
        nm = ADAM_B1 * m_ref[...] + (1.0 - ADAM_B1) * gv
        nv = ADAM_B2 * v_ref[...] + (1.0 - ADAM_B2) * (gv * gv)
        m_hat = nm * c1
        v_hat = nv * c2
        go_ref[...] = gv
        d_ref[...] = -ADAM_LR * (m_hat / (jnp.sqrt(v_hat) + ADAM_EPS) + ADAM_WD * w_ref[...])
        nm_ref[...] = nm
        nv_ref[...] = nv

    spec = pl.BlockSpec((1,) * lead + (tr, cdim), lambda i: (0,) * lead + (i, 0))
    shp = jax.ShapeDtypeStruct(w.shape, F32)
    return _call(body, name=name, out_shape=(shp,) * 4, grid=(r // tr,),
                 in_specs=[spec, pl.BlockSpec((tr, cdim), lambda i: (i, 0)), spec, spec],
                 out_specs=(spec,) * 4, compiler_params=_params(("parallel",)))(w, g, m, v)


def _head_of_row(r, nc):
    assert nc & (nc - 1) == 0
    return lax.shift_right_logical(r, nc.bit_length() - 1)


def _chunk_mats(rows, nc, reverse):
    ri = lax.broadcasted_iota(jnp.int32, (rows, rows), 0)
    ci = lax.broadcasted_iota(jnp.int32, (rows, rows), 1)
    same = _head_of_row(ri, nc) == _head_of_row(ci, nc)
    between = jnp.where(same & ((ci > ri) if reverse else (ci < ri)), 1.0, 0.0).astype(F32)
    li = lax.broadcasted_iota(jnp.int32, (LANES, LANES), 0)
    lj = lax.broadcasted_iota(jnp.int32, (LANES, LANES), 1)
    within = jnp.where((li >= lj) if reverse else (li <= lj), 1.0, 0.0).astype(F32)
    return between, within


def _dot_hi(a, b):
    return jnp.dot(a, b, preferred_element_type=F32, precision=lax.Precision.HIGHEST)


def _scan_rows(t, nc, reverse):
    between, within = _chunk_mats(t.shape[0], nc, reverse)
    inner = _dot_hi(t, within)
    tot = jnp.sum(t, axis=1, keepdims=True)
    return inner + _dot_hi(between, jnp.broadcast_to(tot, t.shape))


def _log_forget_cumsum(f_rows, bias_rows, nc):
    def body(f_ref, b_ref, cum_ref):
        z = f_ref[...] + b_ref[...]
        lf = jnp.minimum(z, 0.0) - jnp.log(1.0 + jnp.exp(-jnp.abs(z)))
        cum_ref[...] = _scan_rows(lf, nc, False)

    return _call(body, name="forget_cumsum", out_shape=jax.ShapeDtypeStruct(f_rows.shape, F32),
                 compiler_params=_params())(f_rows, bias_rows)


def _log_forget_cumsum_bwd(dcum_rows, f_rows, bias_rows, nc):
    rows = f_rows.shape[0]

    def body(d_ref, f_ref, b_ref, df_ref, db_ref):
        dlf = _scan_rows(d_ref[...], nc, True)
        z = f_ref[...] + b_ref[...]
        df = dlf * _sigmoid(-z)
        df_ref[...] = df
        hi = lax.broadcasted_iota(jnp.int32, (FOX_HEADS, rows), 0)
        ri = lax.broadcasted_iota(jnp.int32, (FOX_HEADS, rows), 1)
        sel = jnp.where(_head_of_row(ri, nc) == hi, 1.0, 0.0).astype(F32)
        db_ref[...] = jnp.sum(_dot_hi(sel, df), axis=1, keepdims=True)

    return _call(body, name="forget_cumsum_bwd",
                 out_shape=(jax.ShapeDtypeStruct(f_rows.shape, F32), jax.ShapeDtypeStruct((FOX_HEADS, 1), F32)),
                 compiler_params=_params())(dcum_rows, f_rows, bias_rows)


def _rms_hat(xv):
    rstd = lax.rsqrt(jnp.mean(xv * xv, axis=-1, keepdims=True) + RMS_EPS)
    return xv * rstd, rstd


def _modulated(x_ref, g_ref, sc_ref, sh_ref):
    xhat, _ = _rms_hat(x_ref[...])
    return ((xhat * g_ref[...]) * sc_ref[...] + sh_ref[...]).astype(BF)


def _forget_logits(x, g_pre, scale1p, shift, w_f, tm):
    s = x.shape[0]

    def body(x_ref, g_ref, sc_ref, sh_ref, w_ref, f_ref):
        f_ref[...] = _dot(_modulated(x_ref, g_ref, sc_ref, sh_ref), w_ref[...])

    vec = _full((1, D_MODEL))
    return _call(
        body, name="forget_logits", out_shape=jax.ShapeDtypeStruct((s, LANES), F32), grid=(s // tm,),
        in_specs=[pl.BlockSpec((tm, D_MODEL), lambda i: (i, 0)), vec, vec, vec, _full((D_MODEL, LANES))],
        out_specs=pl.BlockSpec((tm, LANES), lambda i: (i, 0)), compiler_params=_params(("parallel",)),
    )(x, g_pre, scale1p, shift, w_f)


def _split3(v):
    hi = v.astype(BF).astype(F32)
    mid = (v - hi).astype(BF).astype(F32)
    lo = ((v - hi) - mid).astype(BF).astype(F32)
    return hi, mid, lo


def _in_proj(x, g_pre, scale1p, shift, w_rows, w_t_fox, cum, cos_t, sin_t, tm):
    s = x.shape[0]
    r_va, r_za, r_qb, r_zb, r_kb, r_vb = 0, 512, 1024, 1536, 2048, 2176

    def body(x_ref, g_ref, sc_ref, sh_ref, w_ref, wt_ref, cum_ref, cos_ref, sin_ref,
             h_ref, qat_ref, ka_ref, kat_ref, v_ref, vt_ref, za_ref, zb_ref, qb_ref, kb_ref, vb_ref,
             qbt_ref, kbt_ref, vbt_ref, mo_ref):
        hb = _modulated(x_ref, g_ref, sc_ref, sh_ref)
        h_ref[...] = hb

        def sec(c0, width):
            return _dot(hb, w_ref[:, c0:c0 + width])

        def sec_t(r0):
            return _dot_nt(wt_ref[r0:r0 + FOX_W, :], hb)

        q_t = sec_t(0) * Q_SCALE
        k_t = sec_t(FOX_W)
        v_t = sec_t(2 * FOX_W)
        va = sec(r_va, FOX_W)
        zeros = jnp.zeros((AUG_DIM - HEAD_DIM - AUG_ROWS, tm), F32)
        ri = lax.broadcasted_iota(jnp.int32, (AUG_ROWS, tm), 0)
        const = jnp.where(ri == AUG_ROWS - 1, 0.0, 1.0)
        ri_v = lax.broadcasted_iota(jnp.int32, (VT_ROWS - HEAD_DIM, tm), 0)
        v_feat = jnp.where(ri_v == 0, 1.0, 0.0).astype(BF)
        for hd in range(FOX_HEADS):
            rows = slice(hd * HEAD_DIM, (hd + 1) * HEAD_DIM)
            cum2 = cum_ref[hd:hd + 1, :] * LOG2E
            hi, mid, lo = (jnp.broadcast_to(part, (AUG_ROWS, tm)) for part in _split3(cum2))
            q_feat = jnp.where(ri == 1, hi, jnp.where(ri == 2, mid, jnp.where(ri == 3, lo, const)))
            k_feat = jnp.where(ri == 4, -hi, jnp.where(ri == 5, -mid, jnp.where(ri == 6, -lo, const)))
            q_aug = jnp.concatenate([q_t[rows], q_feat, zeros], axis=0)
            k_aug = jnp.concatenate([k_t[rows], k_feat, zeros], axis=0)
            mo_ref[hd:hd + 1, :] = jnp.sum(q_t[rows] * k_t[rows], axis=0, keepdims=True) + 1.0
            qat_ref[hd] = q_aug.astype(BF)
            kat_ref[hd] = k_aug.astype(BF)
            ka_ref[hd] = k_aug.T.astype(BF)
            vt_ref[hd] = jnp.concatenate([v_t[rows].astype(BF), v_feat], axis=0)
            v_ref[hd] = va[:, rows].astype(BF)
        za_ref[...] = sec(r_za, FOX_W)
        zb_ref[...] = sec(r_zb, SWA_W)
        cos2, sin2 = cos_ref[...], sin_ref[...]
        cos8 = jnp.concatenate([cos2] * 4, axis=1)
        sin8 = jnp.concatenate([sin2] * 4, axis=1)
        qb = sec(r_qb, SWA_W)
        qb = (qb * cos8 + _rope_partner(qb) * sin8) * (HEAD_DIM ** -0.5)
        qb_ref[...] = qb.astype(BF)
        for a in range(SWA_W // LANES):
            qbt_ref[a * LANES:(a + 1) * LANES, :] = qb[:, a * LANES:(a + 1) * LANES].T.astype(BF)
        kb = sec(r_kb, SWA_KV_W)
        kb = kb * cos2 + _rope_partner(kb) * sin2
        vb = sec(r_vb, SWA_KV_W)
        kb_t, vb_t = kb.T, vb.T
        for hd in range(SWA_KV_HEADS):
            sl = slice(hd * HEAD_DIM, (hd + 1) * HEAD_DIM)
            kb_ref[hd] = kb[:, sl].astype(BF)
            vb_ref[hd] = vb[:, sl].astype(BF)
            kbt_ref[hd] = kb_t[sl].astype(BF)
            vbt_ref[hd] = jnp.concatenate([vb_t[sl].astype(BF), v_feat], axis=0)

    row = lambda w: pl.BlockSpec((tm, w), lambda i: (i, 0))
    heads = lambda n, w=HEAD_DIM: pl.BlockSpec((n, tm, w), lambda i: (0, i, 0))
    heads_t = lambda w: pl.BlockSpec((FOX_HEADS, w, tm), lambda i: (0, 0, i))
    vec = _full((1, D_MODEL))
    hs = lambda a, b: jax.ShapeDtypeStruct((FOX_HEADS, a, b), BF)
    out_shape = (
        jax.ShapeDtypeStruct((s, D_MODEL), BF),
        hs(AUG_DIM, s), hs(s, AUG_DIM), hs(AUG_DIM, s), hs(s, HEAD_DIM), hs(VT_ROWS, s),
        jax.ShapeDtypeStruct((s, FOX_W), F32), jax.ShapeDtypeStruct((s, SWA_W), F32),
        jax.ShapeDtypeStruct((s, SWA_W), BF),
        jax.ShapeDtypeStruct((SWA_KV_HEADS, s, HEAD_DIM), BF), jax.ShapeDtypeStruct((SWA_KV_HEADS, s, HEAD_DIM), BF),
        jax.ShapeDtypeStruct((SWA_W, s), BF),
        jax.ShapeDtypeStruct((SWA_KV_HEADS, HEAD_DIM, s), BF), jax.ShapeDtypeStruct((SWA_KV_HEADS, VT_ROWS, s), BF),
        jax.ShapeDtypeStruct((FOX_HEADS, s), F32),
    )
    kv_t = lambda w: pl.BlockSpec((SWA_KV_HEADS, w, tm), lambda i: (0, 0, i))
    return _call(
        body, name="in_proj", out_shape=out_shape, grid=(s // tm,),
        in_specs=[row(D_MODEL), vec, vec, vec, _full(w_rows.shape), _full(w_t_fox.shape),
                  pl.BlockSpec((FOX_HEADS, tm), lambda i: (0, i)), row(LANES), row(LANES)],
        out_specs=(row(D_MODEL), heads_t(AUG_DIM), heads(FOX_HEADS, AUG_DIM), heads_t(AUG_DIM), heads(FOX_HEADS),
                   heads_t(VT_ROWS), row(FOX_W), row(SWA_W), row(SWA_W), heads(SWA_KV_HEADS), heads(SWA_KV_HEADS),
                   pl.BlockSpec((SWA_W, tm), lambda i: (0, i)), kv_t(HEAD_DIM), kv_t(VT_ROWS),
                   pl.BlockSpec((FOX_HEADS, tm), lambda i: (0, i))),
        compiler_params=_params(("parallel",)),
    )(x, g_pre, scale1p, shift, w_rows, w_t_fox, cum, cos_t, sin_t)


def _diag_chunks(d, bq, bk, chunk):
    out = []
    for c0 in range(0, bq, chunk):
        if d is None or d * bk + bk - 1 <= c0:
            out.append((c0, None, bk))
        elif d * bk <= c0 + chunk - 1:
            n_keys = min(bk, c0 + chunk - d * bk)
            kpos = d * bk + lax.broadcasted_iota(jnp.int32, (n_keys, chunk), 0)
            qpos = c0 + lax.broadcasted_iota(jnp.int32, (n_keys, chunk), 1)
            out.append((c0, kpos <= qpos, n_keys))
    return out


def _fox_fwd(qat, ka, vt, m_own, bq, bk, chunk, running_max):
    nh, _, s = qat.shape
    r = bq // bk

    pairs = [(i, j) for i in range(s // bq) for j in range(i * r + r)]

    def body(i_tab, j_tab, ka_ref, qat_ref, vt_ref, mo_ref, o_ref, lse_ref, bad_ref, *rest):
        pt_ref, m_scr, acc_scr = (None,) * running_max + rest
        i, j = i_tab[pl.program_id(1)], j_tab[pl.program_id(1)]

        @pl.when(j == 0)
        def _():
            m_scr[...] = jnp.full(m_scr.shape, NEG, F32) if running_max else mo_ref[0]
            acc_scr[...] = jnp.zeros(acc_scr.shape, F32)

        def careful(d):
            kv, vtv = ka_ref[0], vt_ref[0]

            def one_chunk(n, carry):
                c0 = pl.multiple_of(n * chunk, chunk)
                cs = pl.ds(c0, chunk)
                sc = _dot(kv, qat_ref[0, :, cs])
                if d is not None:
                    kpos = d * bk + lax.broadcasted_iota(jnp.int32, (bk, chunk), 0)
                    qpos = c0 + lax.broadcasted_iota(jnp.int32, (bk, chunk), 1)
                    sc = jnp.where(kpos <= qpos, sc, NEG)
                m_prev = m_scr[:, cs]
                m_new = jnp.maximum(m_prev, jnp.max(sc, axis=0, keepdims=True))
                p = jnp.exp2(sc - m_new).astype(BF)
                acc_scr[:, cs] = jnp.exp2(m_prev - m_new) * acc_scr[:, cs] + _dot(vtv, p)
                m_scr[:, cs] = m_new
                return carry

            lax.fori_loop(0, bq // chunk, one_chunk, 0)

        def fast(d):
            todo = _diag_chunks(d, bq, bk, chunk)
            scores = lambda t: _dot(ka_ref[0, :t[2], :], qat_ref[0, :, t[0]:t[0] + chunk])
            sc_next = scores(todo[0])
            for n, (c0, mask, n_keys) in enumerate(todo):
                cs = slice(c0, c0 + chunk)
                sc = sc_next
                if n + 1 < len(todo):
                    sc_next = scores(todo[n + 1])
                if mask is not None:
                    sc = jnp.where(mask, sc, NEG)
                p = jnp.exp2(sc - m_scr[:, cs]).astype(BF)
                pt_ref[0, :n_keys, cs] = p
                acc_scr[:, cs] += _dot(vt_ref[0, :, :n_keys], p)

        step = careful if running_max else fast

        @pl.when(j < i * r)
        def _():
            step(None)

        for d in range(r):
            @pl.when(j == i * r + d)
            def _(d=d):
                step(d)

        @pl.when(j == i * r + r - 1)
        def _():
            l = acc_scr[HEAD_DIM:HEAD_DIM + 1, :]
            o_ref[0] = acc_scr[:HEAD_DIM, :] / l
            lse_ref[0] = m_scr[...] + jnp.log2(l)
            bad_ref[0] = jnp.where(l < OVERFLOW_GUARD, 0.0, 1.0)

    qmap_t = lambda h, t, it, jt: (h, 0, it[t])
    qrow = pl.BlockSpec((1, 1, bq), qmap_t)
    row_shape = jax.ShapeDtypeStruct((nh, 1, s), F32)
    out_shape = (jax.ShapeDtypeStruct((nh, HEAD_DIM, s), F32), row_shape, row_shape)
    out_specs = (pl.BlockSpec((1, HEAD_DIM, bq), qmap_t), qrow, qrow)
    if not running_max:
        out_shape += (jax.ShapeDtypeStruct((nh, s, s), BF),)
        out_specs += (pl.BlockSpec((1, bk, bq), lambda h, t, it, jt: (h, jt[t], it[t])),)
    grid_spec = pltpu.PrefetchScalarGridSpec(
        num_scalar_prefetch=2, grid=(nh, len(pairs)),
        in_specs=[pl.BlockSpec((1, bk, AUG_DIM), lambda h, t, it, jt: (h, jt[t], 0)), pl.BlockSpec((1, AUG_DIM, bq), qmap_t),
                  pl.BlockSpec((1, VT_ROWS, bk), lambda h, t, it, jt: (h, 0, jt[t])), qrow],
        out_specs=out_specs,
        scratch_shapes=[pltpu.VMEM((1, bq), F32), pltpu.VMEM((VT_ROWS, bq), F32)])
    return _call(
        body, name="fox_fwd_running_max" if running_max else "fox_fwd", out_shape=out_shape, grid_spec=grid_spec,
        compiler_params=_params(("parallel", "arbitrary")),
    )(jnp.asarray([p[0] for p in pairs], jnp.int32), jnp.asarray([p[1] for p in pairs], jnp.int32), ka, qat, vt, m_own)


def _fox_bwd(qat, ka, kat, v, dot_, lse, delta, pt, bq, bk, chunk, dq_blk):
    nh, _, s = qat.shape
    r = bq // bk
    nq = s // bq
    stored = pt is not None

    pairs = [(j, i) for j in range(s // bk) for i in range(j // r, nq)]

    def body(j_tab, i_tab, a_ref, b_ref, kat_ref, v_ref, qat_ref, do_ref, dl_ref, dq_ref, dk_ref, dv_ref, dk_scr, dv_scr):
        ka_ref, lse_ref, pt_ref = (None, None, a_ref) if stored else (a_ref, b_ref, None)
        j, i = j_tab[pl.program_id(1)], i_tab[pl.program_id(1)]

        @pl.when(pl.program_id(1) == 0)
        def _():
            dq_ref[...] = jnp.zeros(dq_ref.shape, F32)

        @pl.when(i * r <= j)
        def _():
            dk_scr[...] = jnp.zeros(dk_scr.shape, F32)
            dv_scr[...] = jnp.zeros(dv_scr.shape, F32)

        def step(d):
            todo = _diag_chunks(d, bq, bk, chunk)

            def products(t):
                cs = slice(t[0], t[0] + chunk)
                return (None if stored else _dot(ka_ref[0, :t[2], :], qat_ref[0, :, cs]),
                        _dot(v_ref[0, :t[2], :], do_ref[0, :, cs]))

            nxt = products(todo[0])
            for n, (c0, mask, n_keys) in enumerate(todo):
                cs = slice(c0, c0 + chunk)
                sc, dp = nxt
                if n + 1 < len(todo):
                    nxt = products(todo[n + 1])
                if stored:
                    p_bf = pt_ref[0, :n_keys, cs]
                    p = p_bf.astype(F32)
                else:
                    p = jnp.exp2(sc - lse_ref[0, :, cs])
                    if mask is not None:
                        p = jnp.where(mask, p, 0.0)
                    p_bf = p.astype(BF)
                ds = (p * (dp - dl_ref[0, :, cs])).astype(BF)
                dv_scr[:, :n_keys] += _dot_nt(do_ref[0, :, cs], p_bf)
                dk_scr[:, :n_keys] += _dot_nt(qat_ref[0, :VT_ROWS, cs], ds)
                c1 = c0 % dq_blk
                dq_ref[0, i * (bq // dq_blk) + c0 // dq_blk, :, c1:c1 + chunk] += _dot(kat_ref[0, :VT_ROWS, :n_keys], ds)

        @pl.when(i * r > j)
        def _():
            step(None)

        for d in range(r):
            @pl.when(j == i * r + d)
            def _(d=d):
                step(d)

        @pl.when(i == nq - 1)
        def _():
            dk_ref[0] = dk_scr[...]
            dv_ref[0] = dv_scr[...]

    qmap = lambda h, t, jt, it: (h, 0, it[t])
    kmap = lambda h, t, jt, it: (h, jt[t], 0)
    kmap_t = lambda h, t, jt, it: (h, 0, jt[t])
    if stored:
        first = [(pt, pl.BlockSpec((1, bk, bq), lambda h, t, jt, it: (h, jt[t], it[t]))),
                 (delta, pl.BlockSpec((1, 1, bq), qmap))]
    else:
        first = [(ka, pl.BlockSpec((1, bk, AUG_DIM), kmap)), (lse, pl.BlockSpec((1, 1, bq), qmap))]
    grid_spec = pltpu.PrefetchScalarGridSpec(
        num_scalar_prefetch=2, grid=(nh, len(pairs)),
        in_specs=[first[0][1], first[1][1], pl.BlockSpec((1, AUG_DIM, bk), kmap_t), pl.BlockSpec((1, bk, HEAD_DIM), kmap),
                  pl.BlockSpec((1, AUG_DIM, bq), qmap), pl.BlockSpec((1, HEAD_DIM, bq), qmap),
                  pl.BlockSpec((1, 1, bq), qmap)],
        out_specs=(pl.BlockSpec((1, s // dq_blk, VT_ROWS, dq_blk), lambda h, t, jt, it: (h, 0, 0, 0)),
                   pl.BlockSpec((1, VT_ROWS, bk), kmap_t), pl.BlockSpec((1, HEAD_DIM, bk), kmap_t)),
        scratch_shapes=[pltpu.VMEM((VT_ROWS, bk), F32), pltpu.VMEM((HEAD_DIM, bk), F32)])
    return _call(
        body, name="fox_bwd" if stored else "fox_bwd_recompute",
        out_shape=(jax.ShapeDtypeStruct((nh, s // dq_blk, VT_ROWS, dq_blk), F32),
                   jax.ShapeDtypeStruct((nh, VT_ROWS, s), F32), jax.ShapeDtypeStruct((nh, HEAD_DIM, s), F32)),
        grid_spec=grid_spec, compiler_params=_params(("parallel", "arbitrary")),
    )(jnp.asarray([p[0] for p in pairs], jnp.int32), jnp.asarray([p[1] for p in pairs], jnp.int32),
      first[0][0], first[1][0], kat, v, qat, dot_, delta)


def _swa_mask(i, tq):
    kpos = i * tq - WINDOW + lax.broadcasted_iota(jnp.int32, (tq + WINDOW, tq), 0)
    qpos = i * tq + lax.broadcasted_iota(jnp.int32, (tq + WINDOW, tq), 1)
    rel = qpos - kpos
    return (rel >= 0) & (rel < WINDOW) & (kpos >= 0)


def _swa_rows(ref, g, i, tq):
    before = pl.multiple_of(jnp.maximum(i * tq - WINDOW, 0), WINDOW)
    return jnp.concatenate([ref[g, pl.ds(before, WINDOW), :], ref[g, pl.ds(pl.multiple_of(i * tq, tq), tq), :]], axis=0)


def _swa_before(n_rows, tq):
    return pl.BlockSpec((SWA_KV_HEADS, n_rows, WINDOW), lambda i: (0, 0, jnp.maximum(i * (tq // WINDOW) - 1, 0)))


def _swa_probs_t(sc, mask, sink):
    sc = jnp.where(mask, sc, NEG)
    m = jnp.maximum(jnp.max(sc, axis=0, keepdims=True), sink)
    p = jnp.exp(sc - m)
    e_sink = jnp.exp(sink - m)
    inv_l = 1.0 / (jnp.sum(p, axis=0, keepdims=True) + e_sink)
    return p * inv_l, e_sink * inv_l


def _swa_fwd(qbt, kb, vbt, sinks, tq):
    s = qbt.shape[1]
    n_heads = SWA_KV_HEADS * SWA_GROUP

    def body(q_ref, k_ref, vb_ref, vc_ref, s_ref, o_ref):
        i = pl.program_id(0)
        mask = _swa_mask(i, tq)
        kw = [_swa_rows(k_ref, g, i, tq) for g in range(SWA_KV_HEADS)]
        vtw = [jnp.concatenate([vb_ref[g], vc_ref[g]], axis=1) for g in range(SWA_KV_HEADS)]
        scores = lambda hd: _dot(kw[hd // SWA_GROUP], q_ref[hd * HEAD_DIM:(hd + 1) * HEAD_DIM, :])
        sc_next = scores(0)
        for hd in range(n_heads):
            g, hh = divmod(hd, SWA_GROUP)
            rows = slice(hd * HEAD_DIM, (hd + 1) * HEAD_DIM)
            sink = s_ref[g][:, hh:hh + 1]
            sc = jnp.where(mask, sc_next, NEG)
            if hd + 1 < n_heads:
                sc_next = scores(hd + 1)
            m = jnp.maximum(jnp.max(sc, axis=0, keepdims=True), sink)
            acc = _dot(vtw[g], jnp.exp(sc - m).astype(BF))
            o_ref[rows, :] = acc[:HEAD_DIM] / (acc[HEAD_DIM:HEAD_DIM + 1] + jnp.exp(sink - m))

    kvspec = _full((SWA_KV_HEADS, s, HEAD_DIM))
    qspec = pl.BlockSpec((SWA_W, tq), lambda i: (0, i))
    return _call(
        body, name="swa_fwd", out_shape=jax.ShapeDtypeStruct((SWA_W, s), F32), grid=(s // tq,),
        in_specs=[qspec, kvspec, _swa_before(VT_ROWS, tq), pl.BlockSpec((SWA_KV_HEADS, VT_ROWS, tq), lambda i: (0, 0, i)),
                  _full((SWA_KV_HEADS, 1, SWA_GROUP))],
        out_specs=qspec, compiler_params=_params(("parallel",)),
    )(qbt, kb, vbt, vbt, sinks)


def _swa_bwd(qb, qbt, kb, kbt, vb, sinks, dob, dobt, tq):
    s = qb.shape[0]
    n_heads = SWA_KV_HEADS * SWA_GROUP

    def body(q_ref, qt_ref, k_ref, ktb_ref, ktc_ref, v_ref, s_ref, do_ref, dot_ref, dq_ref, dk_ref, dv_ref, ds_ref):
        i = pl.program_id(0)

        @pl.when(i == 0)
        def _():
            dk_ref[...] = jnp.zeros(dk_ref.shape, F32)
            dv_ref[...] = jnp.zeros(dv_ref.shape, F32)
            ds_ref[...] = jnp.zeros(ds_ref.shape, F32)

        mask = _swa_mask(i, tq)
        kw = [_swa_rows(k_ref, g, i, tq) for g in range(SWA_KV_HEADS)]
        vw = [_swa_rows(v_ref, g, i, tq) for g in range(SWA_KV_HEADS)]
        ktw = [jnp.concatenate([ktb_ref[g], ktc_ref[g]], axis=1) for g in range(SWA_KV_HEADS)]
        before = pl.ds(pl.multiple_of(jnp.maximum(i * tq - WINDOW, 0), WINDOW), WINDOW)
        own = pl.ds(pl.multiple_of(i * tq, tq), tq)

        def products(hd):
            rows = slice(hd * HEAD_DIM, (hd + 1) * HEAD_DIM)
            return _dot(kw[hd // SWA_GROUP], qt_ref[rows, :]), _dot(vw[hd // SWA_GROUP], dot_ref[rows, :])

        nxt = products(0)
        for g in range(SWA_KV_HEADS):
            dsinks = []
            dk_acc = jnp.zeros((tq + WINDOW, HEAD_DIM), F32)
            dv_acc = jnp.zeros((tq + WINDOW, HEAD_DIM), F32)
            for hh in range(SWA_GROUP):
                hd = g * SWA_GROUP + hh
                rows = slice(hd * HEAD_DIM, (hd + 1) * HEAD_DIM)
                sc, dp = nxt
                if hd + 1 < n_heads:
                    nxt = products(hd + 1)
                p, p_sink = _swa_probs_t(sc, mask, s_ref[g][:, hh:hh + 1])
                delta = jnp.sum(p * dp, axis=0, keepdims=True)
                dsc = (p * (dp - delta)).astype(BF)
                dq_ref[rows, :] = _dot(ktw[g], dsc)
                dk_acc = dk_acc + _dot(dsc, q_ref[:, rows])
                dv_acc = dv_acc + _dot(p.astype(BF), do_ref[:, rows])
                dsinks.append(-jnp.sum(p_sink * delta, axis=1, keepdims=True))
            dk_ref[g, before, :] += dk_acc[:WINDOW]
            dk_ref[g, own, :] += dk_acc[WINDOW:]
            dv_ref[g, before, :] += dv_acc[:WINDOW]
            dv_ref[g, own, :] += dv_acc[WINDOW:]
            ds_ref[g] += jnp.concatenate(dsinks, axis=1)

    kvspec = _full((SWA_KV_HEADS, s, HEAD_DIM))
    qspec = pl.BlockSpec((tq, SWA_W), lambda i: (i, 0))
    qspec_t = pl.BlockSpec((SWA_W, tq), lambda i: (0, i))
    sspec = _full((SWA_KV_HEADS, 1, SWA_GROUP))
    kvshape = jax.ShapeDtypeStruct((SWA_KV_HEADS, s, HEAD_DIM), F32)
    return _call(
        body, name="swa_bwd",
        out_shape=(jax.ShapeDtypeStruct((SWA_W, s), F32), kvshape, kvshape,
                   jax.ShapeDtypeStruct((SWA_KV_HEADS, 1, SWA_GROUP), F32)),
        grid=(s // tq,),
        in_specs=[qspec, qspec_t, kvspec, _swa_before(HEAD_DIM, tq),
                  pl.BlockSpec((SWA_KV_HEADS, HEAD_DIM, tq), lambda i: (0, 0, i)), kvspec, sspec, qspec, qspec_t],
        out_specs=(qspec_t, kvspec, kvspec, sspec),
        compiler_params=_params(("arbitrary",)),
    )(qb, qbt, kb, kbt, kbt, vb, sinks, dob, dobt)


def _pairs_to_rows(ref, n_rows=HEAD_DIM):
    parts = []
    for a in range(0, FOX_HEADS, 2):
        parts.append(jnp.concatenate([ref[a][:n_rows], ref[a + 1][:n_rows]], axis=0).T)
    return jnp.concatenate(parts, axis=1)


def _blocks_to_rows(ref):
    return jnp.concatenate([ref[a:a + LANES, :].T for a in range(0, ref.shape[0], LANES)], axis=1)


def _out_proj(oat, za, obt, zb, x, tgt, w_out, w_out_t, gate, g_post, inv_l, tm):
    s = x.shape[0]

    def body(oat_ref, za_ref, obt_ref, zb_ref, x_ref, t_ref, w_ref, wt_ref, gate_ref, gp_ref, il_ref,
             dout_ref, doat_ref, dla_ref, dza_ref, dob_ref, dobt_ref, dzb_ref, gw_ref, dgate_ref, dgp_ref, loss_ref):
        i = pl.program_id(0)

        @pl.when(i == 0)
        def _():
            gw_ref[...] = jnp.zeros(gw_ref.shape, F32)
            dgate_ref[...] = jnp.zeros(dgate_ref.shape, F32)
            dgp_ref[...] = jnp.zeros(dgp_ref.shape, F32)
            loss_ref[...] = jnp.zeros(loss_ref.shape, F32)

        oa_v = _pairs_to_rows(oat_ref)
        ob_v = _blocks_to_rows(obt_ref)
        za_v, zb_v = za_ref[...], zb_ref[...]
        sga, sgb = _sigmoid(za_v), _sigmoid(zb_v)
        sila, silb = za_v * sga, zb_v * sgb
        u = jnp.concatenate([oa_v * sila, ob_v * silb], axis=1).astype(BF)
        yv = _dot(u, w_ref[...])
        yhat, rstd = _rms_hat(yv)
        gp, gate_v = gp_ref[...], gate_ref[...]
        nrm = yhat * gp
        diff = (x_ref[...] + gate_v * nrm) - t_ref[...]
        loss_ref[...] += 0.5 * jnp.sum(jnp.sum(diff * diff, axis=1, keepdims=True), axis=0, keepdims=True) / D_MODEL
        dout = diff * (1.0 / D_MODEL)
        dout_ref[...] = dout
        dgate_ref[...] += jnp.sum(dout * nrm, axis=0, keepdims=True)
        dn = dout * gate_v
        dgp_ref[...] += jnp.sum(dn * yhat, axis=0, keepdims=True)
        dyhat = dn * gp
        dy = (rstd * (dyhat - yhat * jnp.mean(dyhat * yhat, axis=1, keepdims=True))).astype(BF)
        gw_ref[...] += _dot_tn(u, dy)
        du = _dot(dy, wt_ref[...])
        dua, dub = du[:, :FOX_W], du[:, FOX_W:]
        doa = dua * sila
        for a in range(0, FOX_HEADS, 2):
            pair_t = doa[:, a * HEAD_DIM:(a + 2) * HEAD_DIM].T
            for hd, rows in ((a, slice(0, HEAD_DIM)), (a + 1, slice(HEAD_DIM, 2 * HEAD_DIM))):
                inv_l = il_ref[hd]
                doat_ref[hd] = (pair_t[rows] * inv_l).astype(BF)
                dla_ref[hd] = jnp.sum(pair_t[rows] * oat_ref[hd], axis=0, keepdims=True) * inv_l
        dob = dub * silb
        dob_ref[...] = dob.astype(BF)
        for a in range(0, SWA_W, LANES):
            dobt_ref[a:a + LANES, :] = dob[:, a:a + LANES].T.astype(BF)
        dza_ref[...] = (dua * oa_v * (sga * (1.0 + za_v * (1.0 - sga)))).astype(BF)
        dzb_ref[...] = (dub * ob_v * (sgb * (1.0 + zb_v * (1.0 - sgb)))).astype(BF)

    row = lambda w: pl.BlockSpec((tm, w), lambda i: (i, 0))
    heads_t = lambda w: pl.BlockSpec((FOX_HEADS, w, tm), lambda i: (0, 0, i))
    vec = _full((1, D_MODEL))
    mat = _full((D_MODEL, D_MODEL))
    out_shape = (
        jax.ShapeDtypeStruct((s, D_MODEL), F32),
        jax.ShapeDtypeStruct((FOX_HEADS, HEAD_DIM, s), BF), jax.ShapeDtypeStruct((FOX_HEADS, 1, s), F32),
        jax.ShapeDtypeStruct((s, FOX_W), BF), jax.ShapeDtypeStruct((s, SWA_W), BF), jax.ShapeDtypeStruct((SWA_W, s), BF),
        jax.ShapeDtypeStruct((s, SWA_W), BF),
        jax.ShapeDtypeStruct((D_MODEL, D_MODEL), F32),
        jax.ShapeDtypeStruct((1, D_MODEL), F32), jax.ShapeDtypeStruct((1, D_MODEL), F32),
        jax.ShapeDtypeStruct((1, 1), F32),
    )
    col = pl.BlockSpec((SWA_W, tm), lambda i: (0, i))
    return _call(
        body, name="out_proj", out_shape=out_shape, grid=(s // tm,),
        in_specs=[heads_t(HEAD_DIM), row(FOX_W), col, row(SWA_W), row(D_MODEL), row(D_MODEL), mat, mat, vec, vec,
                  heads_t(1)],
        out_specs=(row(D_MODEL), heads_t(HEAD_DIM), heads_t(1), row(FOX_W), row(SWA_W), col, row(SWA_W), mat, vec, vec,
                   _full((1, 1))),
        compiler_params=_params(("arbitrary",)),
    )(oat, za, obt, zb, x, tgt, w_out, w_out_t, gate, g_post, inv_l)


def _assemble_dproj(dqt, dkt, dvt, dza, dqb, dzb, dkb, dvb, df, cos_t, sin_t, tm):
    s = dza.shape[0]

    def body(dqt_ref, dkt_ref, dvt_ref, dza_ref, dqb_ref, dzb_ref, dkb_ref, dvb_ref, df_ref, cos_ref, sin_ref, o_ref):
        def cat(ref, n):
            return jnp.concatenate([ref[hd] for hd in range(n)], axis=1)

        cos2, sin2 = cos_ref[...], sin_ref[...]
        cos8 = jnp.concatenate([cos2] * 4, axis=1)
        sin8 = jnp.concatenate([sin2] * 4, axis=1)
        scale = HEAD_DIM ** -0.5
        o_ref[:, C_QA:C_QA + FOX_W] = (_pairs_to_rows(dqt_ref.at[:, 0]) * scale).astype(BF)
        o_ref[:, C_KA:C_KA + FOX_W] = (_pairs_to_rows(dkt_ref) * LN2).astype(BF)
        o_ref[:, C_VA:C_VA + FOX_W] = _pairs_to_rows(dvt_ref).astype(BF)
        o_ref[:, C_ZA:C_ZA + FOX_W] = dza_ref[...]
        dq = _blocks_to_rows(dqb_ref) * scale
        o_ref[:, C_QB:C_QB + SWA_W] = (dq * cos8 - _rope_partner(dq) * sin8).astype(BF)
        o_ref[:, C_ZB:C_ZB + SWA_W] = dzb_ref[...]
        dk = cat(dkb_ref, SWA_KV_HEADS)
        o_ref[:, C_KB:C_KB + SWA_KV_W] = (dk * cos2 - _rope_partner(dk) * sin2).astype(BF)
        o_ref[:, C_VB:C_VB + SWA_KV_W] = cat(dvb_ref, SWA_KV_HEADS).astype(BF)
        o_ref[:, C_F:C_F + LANES] = df_ref[...].astype(BF)

    row = lambda w: pl.BlockSpec((tm, w), lambda i: (i, 0))
    heads = lambda n: pl.BlockSpec((n, tm, HEAD_DIM), lambda i: (0, i, 0))
    heads_t = lambda w: pl.BlockSpec((FOX_HEADS, w, tm), lambda i: (0, 0, i))
    return _call(
        body, name="assemble_dproj", out_shape=jax.ShapeDtypeStruct((s, WP), BF), grid=(s // tm,),
        in_specs=[pl.BlockSpec((FOX_HEADS, 1, VT_ROWS, tm), lambda i: (0, i, 0, 0)), heads_t(VT_ROWS), heads_t(HEAD_DIM),
                  row(FOX_W), pl.BlockSpec((SWA_W, tm), lambda i: (0, i)), row(SWA_W), heads(SWA_KV_HEADS),
                  heads(SWA_KV_HEADS), row(LANES), row(LANES), row(LANES)],
        out_specs=row(WP), compiler_params=_params(("parallel",)),
    )(dqt, dkt, dvt, dza, dqb, dzb, dkb, dvb, df, cos_t, sin_t)


def _in_proj_bwd_x(dproj, w_al_t, x, dout, g_pre, scale1p, tm, parts):
    s = x.shape[0]
    n_steps = s // tm
    masks = [(1, 0), (0, 1), (1, 1)]

    def body(dp_ref, wt_ref, x_ref, dout_ref, g_ref, sc_ref, parts_ref, gx_ref, dsh_ref, dsc_ref, dg_ref, got_ref,
             send_sems, recv_sems, local_sem):
        i = pl.program_id(0)
        cx, cy, cc = lax.axis_index("x"), lax.axis_index("y"), lax.axis_index("c")
        me = 2 * cx + cy
        own = pltpu.make_async_copy(parts_ref.at[me], got_ref.at[me], local_sem)

        def copy(k, send):
            dx, dy = masks[k]
            peer = 2 * (cx ^ dx) + (cy ^ dy)
            return pltpu.make_async_remote_copy(
                src_ref=parts_ref.at[peer if send else me], dst_ref=got_ref.at[me if send else peer],
                send_sem=send_sems.at[k], recv_sem=recv_sems.at[k], device_id=(cx ^ dx, cy ^ dy, cc), device_id_type=MESH)

        @pl.when(i == 0)
        def _():
            dsh_ref[...] = jnp.zeros(dsh_ref.shape, F32)
            dsc_ref[...] = jnp.zeros(dsc_ref.shape, F32)
            dg_ref[...] = jnp.zeros(dg_ref.shape, F32)
            own.start()
            for k in range(len(masks)):
                copy(k, True).start()

        @pl.when(i == n_steps - 1)
        def _():
            for k in range(len(masks)):
                copy(k, False).wait_recv()
            for k in range(len(masks)):
                copy(k, True).wait_send()
            own.wait()

        dh = _dot(dp_ref[...], wt_ref[...])
        xhat, rstd = _rms_hat(x_ref[...])
        g, sc = g_ref[...], sc_ref[...]
        dsh_ref[...] += jnp.sum(dh, axis=0, keepdims=True)
        dhx = dh * xhat
        dsc_ref[...] += jnp.sum(dhx * g, axis=0, keepdims=True)
        dg_ref[...] += jnp.sum(dhx * sc, axis=0, keepdims=True)
        dxhat = dh * (g * sc)
        gx_ref[...] = dout_ref[...] + rstd * (dxhat - xhat * jnp.mean(dxhat * xhat, axis=1, keepdims=True))

    row = lambda w: pl.BlockSpec((tm, w), lambda i: (i, 0))
    vec = _full((1, D_MODEL))
    vshape = jax.ShapeDtypeStruct((1, D_MODEL), F32)
    hbm = pl.BlockSpec(memory_space=pl.ANY)
    return _call(
        body, name="in_proj_bwd_x",
        out_shape=(jax.ShapeDtypeStruct((s, D_MODEL), F32), vshape, vshape, vshape,
                   jax.ShapeDtypeStruct(parts.shape, parts.dtype)),
        grid=(n_steps,),
        in_specs=[row(WP), _full((WP, D_MODEL)), row(D_MODEL), row(D_MODEL), vec, vec, hbm],
        out_specs=(row(D_MODEL), vec, vec, vec, hbm),
        scratch_shapes=[pltpu.SemaphoreType.DMA((3,)), pltpu.SemaphoreType.DMA((3,)), pltpu.SemaphoreType.DMA],
        compiler_params=_params(("arbitrary",), has_side_effects=True),
    )(dproj, w_al_t, x, dout, g_pre, scale1p, parts)


def _in_proj_bwd_w(h, dproj, tk, tn):
    s = h.shape[0]
    n_k = s // tk

    def body(h_ref, dp_ref, gw_ref, acc_scr):
        k = pl.program_id(1)

        @pl.when(k == 0)
        def _():
            acc_scr[...] = jnp.zeros(acc_scr.shape, F32)

        acc_scr[...] += _dot_tn(h_ref[...], dp_ref[...])

        @pl.when(k == n_k - 1)
        def _():
            gw_ref[...] = acc_scr[...].astype(BF)

    return _call(
        body, name="in_proj_bwd_w", out_shape=jax.ShapeDtypeStruct((D_MODEL, WP), BF), grid=(WP // tn, n_k),
        in_specs=[pl.BlockSpec((tk, D_MODEL), lambda n, k: (k, 0)), pl.BlockSpec((tk, tn), lambda n, k: (k, n))],
        out_specs=pl.BlockSpec((D_MODEL, tn), lambda n, k: (0, n)),
        scratch_shapes=[pltpu.VMEM((D_MODEL, tn), F32)],
        compiler_params=_params(("parallel", "arbitrary")),
    )(h, dproj)


def _align_w_in(w_cols):
    def part(name, width):
        return w_cols[:, _SRC[name]:_SRC[name] + width]

    fpad = jnp.pad(part("fa", FOX_HEADS), ((0, 0), (0, LANES - FOX_HEADS)))
    return jnp.concatenate([part("qa", FOX_W), part("ka", FOX_W), part("va", FOX_W), part("za", FOX_W),
                            part("qb", SWA_W), part("zb", SWA_W), part("kb", SWA_KV_W), part("vb", SWA_KV_W), fpad], axis=1)


def _unalign_w_in(g_al):
    def part(c0, width):
        return g_al[:, c0:c0 + width]

    return jnp.concatenate([part(C_QA, FOX_W), part(C_KA, FOX_W), part(C_VA, FOX_W), part(C_F, FOX_HEADS),
                            part(C_ZA, FOX_W), part(C_QB, SWA_W), part(C_KB, SWA_KV_W), part(C_VB, SWA_KV_W),
                            part(C_ZB, SWA_W)], axis=1)


def _rope_tables(positions):
    inv_freq = ROPE_THETA ** (-jnp.arange(HALF, dtype=F32) / HALF)
    ang = positions.astype(F32)[:, None] * inv_freq
    cos, sin = jnp.cos(ang), jnp.sin(ang)
    return jnp.concatenate([cos, cos, cos, cos], axis=1), jnp.concatenate([-sin, sin, -sin, sin], axis=1)


def _tiles(s):
    if s >= 4096:
        return dict(tm=512, blk=512, bq=2048, bk=2048, bk_bwd=2048, chunk=256, tq=256, tm_out=512, tk=1024, tn=1152)
    return dict(tm=128, blk=128, bq=256, bk=256, bk_bwd=256, chunk=128, tq=128, tm_out=128, tk=128, tn=1152)


def kernel(x, c, positions, w_ada, b_ada, g_pre, w_in, b_fgate, sinks, w_out, g_post, loss_target, m_w_ada, m_b_ada, m_g_pre, m_w_in, m_b_fgate, m_sinks, m_w_out, m_g_post, v_w_ada, v_b_ada, v_g_pre, v_w_in, v_b_fgate, v_sinks, v_w_out, v_g_post):
    s = x.shape[1]
    t = _tiles(s)
    nc = s // LANES
    rows = FOX_HEADS * nc
    me = 4 * lax.axis_index("x") + 2 * lax.axis_index("y") + lax.axis_index("c")
    chip = 2 * lax.axis_index("x") + lax.axis_index("y")
    core = lax.axis_index("c")
    x2, tgt = x[0], loss_target[0]

    c_all = _allgather_devices(c, "gather_c")[:, 0, :]
    a_all, mod_shard = _ada_shard(c_all, w_ada[0])
    mod_all = _allgather_devices(mod_shard, "gather_mod")
    mod_rows = lax.dynamic_index_in_dim(mod_all, me, axis=1, keepdims=False)
    mod = mod_rows.reshape(N_CHIPS, 2, W_ADA_SHARD)[:, 0, :].reshape(1, 3 * D_MODEL) + b_ada
    shift, scale1p, gate = mod[:, :D_MODEL], 1.0 + mod[:, D_MODEL:2 * D_MODEL], mod[:, 2 * D_MODEL:]

    w_in_pad = jnp.pad(w_in[0].astype(BF), ((0, 0), (0, W_IN_SHARD_PAD - W_IN_SHARD)))
    w_pack = jnp.concatenate([w_in_pad, w_out[0].astype(BF).reshape(D_MODEL, W_OUT_SHARD)], axis=1)
    w_all = _allgather_chips(w_pack.reshape(2, D_MODEL // 2, -1), "gather_weights").reshape(N_CHIPS, D_MODEL, -1)
    w_cols = jnp.concatenate([w_all[k, :, :W_IN_SHARD] for k in range(N_CHIPS)], axis=1)
    w_al = _align_w_in(w_cols)
    w_al_t = w_al.T
    w_out_all = w_all[:, :, W_IN_SHARD_PAD:].reshape(D_MODEL, D_MODEL)
    w_out_t = w_out_all.T

    cos_t, sin_t = _rope_tables(positions[0])

    f_pad = _forget_logits(x2, g_pre, scale1p, shift, w_al[:, C_F:], t["tk"])
    f_rows = f_pad[:, :FOX_HEADS].T.reshape(rows, LANES)
    bias_rows = jnp.repeat(b_fgate[0], nc)[:, None]
    cum = _log_forget_cumsum(f_rows, bias_rows, nc).reshape(FOX_HEADS, s)
    h, qat, ka, kat, va, vat, za, zb, qb, kb, vb, qbt, kbt, vbt, m_own = _in_proj(
        x2, g_pre, scale1p, shift, w_al[:, C_VA:C_F], w_al_t[:C_ZA], cum, cos_t, sin_t, t["tm"])
    m_own = m_own[:, None, :]
    fox_args = (qat, ka, vat, m_own, t["bq"], t["bk"], t["chunk"])
    oat, lse, bad, pt = _fox_fwd(*fox_args, running_max=False)
    overflowed = jnp.max(bad) > 0.0
    oat, lse = lax.cond(overflowed, lambda: _fox_fwd(*fox_args, running_max=True)[:2], lambda: (oat, lse))
    inv_l = jnp.where(overflowed, 1.0, jnp.exp2(m_own - lse))
    sinks_g = sinks.reshape(SWA_KV_HEADS, 1, SWA_GROUP)
    obt = _swa_fwd(qbt, kb, vbt, sinks_g, t["tq"])

    dout, doat, delta_a, dza, dob, dobt, dzb, gw_out, dgate, dg_post, loss_part = _out_proj(
        oat, za, obt, zb, x2, tgt, w_out_all, w_out_t, gate, g_post, inv_l, t["tm_out"])

    bwd_args = (qat, ka, kat, va, doat, lse, delta_a)
    bwd_tiles = (t["bq"], t["bk_bwd"], t["chunk"], t["blk"])
    dqt, dkt, dvt = lax.cond(overflowed, lambda: _fox_bwd(*bwd_args, None, *bwd_tiles),
                             lambda: _fox_bwd(*bwd_args, pt, *bwd_tiles))
    dcum = dqt[:, :, HEAD_DIM, :].reshape(FOX_HEADS, s) - dkt[:, HEAD_DIM, :]
    df_rows, db_heads = _log_forget_cumsum_bwd(dcum.reshape(rows, LANES), f_rows, bias_rows, nc)
    df_pad = jnp.pad(df_rows.reshape(FOX_HEADS, s).T, ((0, 0), (0, LANES - FOX_HEADS)))
    dqb, dkb, dvb, dsinks = _swa_bwd(qb, qbt, kb, kbt, vb, sinks_g, dob, dobt, t["tq"])

    dproj = _assemble_dproj(dqt, dkt, dvt, dza, dqb, dzb, dkb, dvb, df_pad, cos_t, sin_t, t["blk"])
    gw_in = _unalign_w_in(_in_proj_bwd_w(h, dproj, t["tk"], t["tn"]))

    gin = jnp.stack([jnp.pad(gw_in[:, k * W_IN_SHARD:(k + 1) * W_IN_SHARD], ((0, 0), (0, W_IN_SHARD_PAD - W_IN_SHARD)))
                     for k in range(N_CHIPS)])
    gout = gw_out.astype(BF).reshape(N_CHIPS, D_MODEL, W_OUT_SHARD)
    gbig = jnp.concatenate([gin, gout], axis=2)
    half = D_MODEL // 2
    gw = W_IN_SHARD_PAD + W_OUT_SHARD
    keep = lax.dynamic_slice_in_dim(gbig, core * half, half, axis=1)
    give = lax.dynamic_slice_in_dim(gbig, (1 - core) * half, half, axis=1)
    got = _swap_sibling(give.reshape(N_CHIPS * half, gw), "swap_grad_halves")
    pair = _add(keep.reshape(N_CHIPS * half, gw), got, "add_pair", BF).reshape(N_CHIPS, half, gw)
    grad_x, dshift, dscale, dg_pre, from_chips = _in_proj_bwd_x(
        dproj, w_al_t, x2, dout, g_pre, scale1p, t["tm_out"], pair)

    pad_lane = lambda vrow: jnp.pad(vrow, ((0, 0), (0, LANES - vrow.shape[1])))
    packed = jnp.concatenate([dshift, dscale, dgate, dg_pre, dg_post,
                              pad_lane(db_heads.reshape(1, FOX_HEADS)), pad_lane(dsinks.reshape(1, FOX_HEADS)),
                              pad_lane(loss_part)], axis=1)
    parts = _allgather_devices(packed, "gather_partials")
    tot = _sum_devices(parts)
    loss = tot[0, P_LOSS]
    g_b_ada = tot[:, P_DMOD:P_DMOD + 3 * D_MODEL]
    g_g_pre = tot[:, P_GPRE:P_GPRE + D_MODEL]
    g_g_post = tot[:, P_GPOST:P_GPOST + D_MODEL]
    g_b_fgate = tot[:, P_BF:P_BF + FOX_HEADS]
    g_sinks = tot[:, P_SINK:P_SINK + FOX_HEADS]
    dm_shard = lax.dynamic_slice_in_dim(parts[:, 0, :3 * D_MODEL], chip * W_ADA_SHARD, W_ADA_SHARD, axis=1)
    g_w_ada = _grad_w_ada(a_all.T, dm_shard)

    mine = _sum_chips(from_chips, "sum_chips")
    other = _swap_sibling(mine, "swap_grad_result")
    lo = jnp.where(core == 0, mine, other)
    hi = jnp.where(core == 0, other, mine)
    gfull = jnp.concatenate([lo, hi], axis=0)
    g_w_in = gfull[:, :W_IN_SHARD]
    g_w_out = gfull[:, W_IN_SHARD_PAD:].reshape(W_OUT_SHARD, D_MODEL)

    grads = dict(w_ada=g_w_ada, b_ada=g_b_ada, g_pre=g_g_pre, w_in=g_w_in, b_fgate=g_b_fgate, sinks=g_sinks,
                 w_out=g_w_out, g_post=g_g_post)
    weights = dict(w_ada=w_ada, b_ada=b_ada, g_pre=g_pre, w_in=w_in, b_fgate=b_fgate, sinks=sinks, w_out=w_out, g_post=g_post)
    moms = dict(w_ada=m_w_ada, b_ada=m_b_ada, g_pre=m_g_pre, w_in=m_w_in, b_fgate=m_b_fgate, sinks=m_sinks, w_out=m_w_out, g_post=m_g_post)
    vars_ = dict(w_ada=v_w_ada, b_ada=v_b_ada, g_pre=v_g_pre, w_in=v_w_in, b_fgate=v_b_fgate, sinks=v_sinks, w_out=v_w_out, g_post=v_g_post)
    names = ["w_ada", "b_ada", "g_pre", "w_in", "b_fgate", "sinks", "w_out", "g_post"]
    g_out, d_out, m_out, v_out = [], [], [], []
    for n in names:
        if n == "w_in":
            flat = lambda a: jnp.transpose(a, (2, 0, 1)).reshape(W_IN_SHARD * D_MODEL // LANES, LANES)
            unflat = lambda a: jnp.transpose(a.reshape(W_IN_SHARD, 1, D_MODEL), (1, 2, 0))
            outs = _adamw(flat(w_in), flat(grads[n][None]), flat(moms[n]), flat(vars_[n]), "adamw_" + n)
            go, d, nm, nv = (unflat(a) for a in outs)
        else:
            g2 = grads[n].reshape(weights[n].shape[-2:])
            go, d, nm, nv = _adamw(weights[n], g2, moms[n], vars_[n], "adamw_" + n)
        g_out.append(go)
        d_out.append(d)
        m_out.append(nm)
        v_out.append(nv)
    return (loss, grad_x.reshape(x.shape), *g_out, *d_out, *m_out, *v_out)
```

```python
import jax
import jax.numpy as jnp
from jax import lax
from jax.experimental import pallas as pl
from jax.experimental.pallas import tpu as pltpu

_INTERPRET = False

D_MODEL = 1024
HEAD_DIM = 64
HALF = HEAD_DIM // 2
AUG_DIM = 128
AUG_ROWS = 8
VT_ROWS = 80
LOG2E = 1.4426950408889634
LN2 = 0.6931471805599453
Q_SCALE = LOG2E * 64 ** -0.5
FOX_HEADS = 8
FOX_W = 512
SWA_W = 512
SWA_KV_HEADS = 2
SWA_GROUP = 4
SWA_KV_W = 128
WINDOW = 128
ROPE_THETA = 10000.0
RMS_EPS = 1e-6
IN_WIDTH = 3336
N_CHIPS = 4
N_DEV = 8
W_IN_SHARD = IN_WIDTH // N_CHIPS
W_IN_SHARD_PAD = 896
W_ADA_SHARD = 3 * D_MODEL // N_CHIPS
W_OUT_SHARD = D_MODEL // N_CHIPS
LANES = 128

_SRC = dict(qa=0, ka=512, va=1024, fa=1536, za=1544, qb=2056, kb=2568, vb=2696, zb=2824)
C_QA, C_KA, C_VA, C_ZA, C_QB, C_ZB, C_KB, C_VB, C_F = 0, 512, 1024, 1536, 2048, 2560, 3072, 3200, 3328
WP = 3456

ADAM_LR = 0.001
ADAM_B1 = 0.9
ADAM_B2 = 0.999
ADAM_EPS = 1e-08
ADAM_WD = 0.01
ADAM_STEP = 10
ADAMW_BLOCK_ELEMS = 300_000

VMEM_LIMIT = 56 * 1024 * 1024
NEG = -1e30
OVERFLOW_GUARD = 1e30
MESH = pl.DeviceIdType.MESH
BF = jnp.bfloat16
F32 = jnp.float32

P_DMOD, P_GPRE, P_GPOST, P_BF, P_SINK, P_LOSS, P_LEN = 0, 3072, 4096, 5120, 5248, 5376, 5504


def _call(body, **kw):
    return pl.pallas_call(body, interpret=_INTERPRET, **kw)


def _params(sem=None, **kw):
    return pltpu.CompilerParams(dimension_semantics=sem, vmem_limit_bytes=VMEM_LIMIT, **kw)


def _full(shape):
    zeros = (0,) * len(shape)
    return pl.BlockSpec(shape, lambda *_: zeros)


def _dot(a, b):
    return jnp.dot(a, b, preferred_element_type=F32)


def _dot_nt(a, b):
    return lax.dot_general(a, b, (((1,), (1,)), ((), ())), preferred_element_type=F32)


def _dot_tn(a, b):
    return lax.dot_general(a, b, (((0,), (0,)), ((), ())), preferred_element_type=F32)


def _sigmoid(z):
    return 1.0 / (1.0 + jnp.exp(-z))


def _rope_partner(t):
    w = t.shape[-1]
    lane = lax.broadcasted_iota(jnp.int32, t.shape, t.ndim - 1)
    return jnp.where((lane & (HEAD_DIM - 1)) < HALF, pltpu.roll(t, w - HALF, t.ndim - 1), pltpu.roll(t, HALF, t.ndim - 1))


def _allgather_devices(v, name):
    r, cdim = v.shape
    masks = [(dx, dy, dc) for dx in (0, 1) for dy in (0, 1) for dc in (0, 1)][1:]

    def body(v_ref, out_ref, send_sems, recv_sems):
        x, y, c = lax.axis_index("x"), lax.axis_index("y"), lax.axis_index("c")
        me = 4 * x + 2 * y + c
        out_ref[me] = v_ref[...]
        copies = []
        for k, (dx, dy, dc) in enumerate(masks):
            cp = pltpu.make_async_remote_copy(
                src_ref=v_ref, dst_ref=out_ref.at[me], send_sem=send_sems.at[k], recv_sem=recv_sems.at[k],
                device_id=(x ^ dx, y ^ dy, c ^ dc), device_id_type=MESH)
            cp.start()
            copies.append(cp)
        for k, (dx, dy, dc) in enumerate(masks):
            peer = 4 * (x ^ dx) + 2 * (y ^ dy) + (c ^ dc)
            pltpu.make_async_remote_copy(
                src_ref=v_ref, dst_ref=out_ref.at[peer], send_sem=send_sems.at[k], recv_sem=recv_sems.at[k],
                device_id=(x ^ dx, y ^ dy, c ^ dc), device_id_type=MESH).wait_recv()
        for cp in copies:
            cp.wait_send()

    return _call(
        body, name=name, out_shape=jax.ShapeDtypeStruct((N_DEV, r, cdim), v.dtype),
        in_specs=[pl.BlockSpec(memory_space=pltpu.VMEM)], out_specs=pl.BlockSpec(memory_space=pltpu.VMEM),
        scratch_shapes=[pltpu.SemaphoreType.DMA((7,)), pltpu.SemaphoreType.DMA((7,))],
        compiler_params=pltpu.CompilerParams(has_side_effects=True),
    )(v)


def _allgather_chips(v, name):
    _, r, cdim = v.shape
    masks = [(1, 0), (0, 1), (1, 1)]
    n = len(masks)

    def body(v_ref, out_ref, send_sems, recv_sems, local_sem):
        x, y, c = lax.axis_index("x"), lax.axis_index("y"), lax.axis_index("c")
        me = 2 * x + y
        mine = pltpu.make_async_copy(v_ref, out_ref.at[me], local_sem)
        mine.start()

        def copy(k, chip, half, to):
            return pltpu.make_async_remote_copy(
                src_ref=v_ref.at[half] if k < n else out_ref.at[chip, half], dst_ref=out_ref.at[chip, half],
                send_sem=send_sems.at[k], recv_sem=recv_sems.at[k], device_id=to, device_id_type=MESH)

        first = [copy(k, me, c, (x ^ dx, y ^ dy, c)) for k, (dx, dy) in enumerate(masks)]
        for cp in first:
            cp.start()
        passed = []
        for k, (dx, dy) in enumerate(masks):
            peer = 2 * (x ^ dx) + (y ^ dy)
            copy(k, peer, c, (x, y, c)).wait_recv()
            cp = copy(n + k, peer, c, (x, y, 1 - c))
            cp.start()
            passed.append(cp)
        for k, (dx, dy) in enumerate(masks):
            copy(n + k, 2 * (x ^ dx) + (y ^ dy), 1 - c, (x, y, c)).wait_recv()
        for cp in first + passed:
            cp.wait_send()
        mine.wait()

    return _call(
        body, name=name, out_shape=jax.ShapeDtypeStruct((N_CHIPS, 2, r, cdim), v.dtype),
        in_specs=[pl.BlockSpec(memory_space=pl.ANY)], out_specs=pl.BlockSpec(memory_space=pl.ANY),
        scratch_shapes=[pltpu.SemaphoreType.DMA((2 * n,)), pltpu.SemaphoreType.DMA((2 * n,)), pltpu.SemaphoreType.DMA],
        compiler_params=pltpu.CompilerParams(has_side_effects=True),
    )(v)


def _swap_sibling(v, name):
    def body(v_ref, out_ref, send_sem, recv_sem):
        x, y, c = lax.axis_index("x"), lax.axis_index("y"), lax.axis_index("c")
        cp = pltpu.make_async_remote_copy(
            src_ref=v_ref, dst_ref=out_ref, send_sem=send_sem, recv_sem=recv_sem,
            device_id=(x, y, 1 - c), device_id_type=MESH)
        cp.start()
        cp.wait()

    return _call(
        body, name=name, out_shape=jax.ShapeDtypeStruct(v.shape, v.dtype),
        in_specs=[pl.BlockSpec(memory_space=pl.ANY)], out_specs=pl.BlockSpec(memory_space=pl.ANY),
        scratch_shapes=[pltpu.SemaphoreType.DMA, pltpu.SemaphoreType.DMA],
        compiler_params=pltpu.CompilerParams(has_side_effects=True),
    )(v)


def _ada_exchange(c, w_ada_shard):
    masks = [(dx, dy, dc) for dx in (0, 1) for dy in (0, 1) for dc in (0, 1)][1:]
    n = len(masks)

    def body(c_ref, w_ref, a_ref, mod_ref, c_all, send_sems, recv_sems):
        x, y, cc = lax.axis_index("x"), lax.axis_index("y"), lax.axis_index("c")
        me = 4 * x + 2 * y + cc

        def gather(src_ref, dst_ref, first):
            sends = []
            for k, (dx, dy, dc) in enumerate(masks):
                cp = pltpu.make_async_remote_copy(
                    src_ref=src_ref, dst_ref=dst_ref.at[me], send_sem=send_sems.at[first + k],
                    recv_sem=recv_sems.at[first + k], device_id=(x ^ dx, y ^ dy, cc ^ dc), device_id_type=MESH)
                cp.start()
                sends.append(cp)
            for k, (dx, dy, dc) in enumerate(masks):
                peer = 4 * (x ^ dx) + 2 * (y ^ dy) + (cc ^ dc)
                pltpu.make_async_remote_copy(
                    src_ref=src_ref, dst_ref=dst_ref.at[peer], send_sem=send_sems.at[first + k],
                    recv_sem=recv_sems.at[first + k], device_id=(x ^ dx, y ^ dy, cc ^ dc), device_id_type=MESH).wait_recv()
            return sends

        c_all[me] = c_ref[...]
        sends = gather(c_ref, c_all, 0)
        w_bf = w_ref[...].astype(BF)
        for d in range(N_DEV):
            cv = c_all[d]
            a = cv * _sigmoid(cv)
            a_ref[d:d + 1, :] = a
            mod_ref[me, d:d + 1, :] = _dot(a.astype(BF), w_bf)
        sends += gather(mod_ref.at[me], mod_ref, n)
        for cp in sends:
            cp.wait_send()

    vmem = pl.BlockSpec(memory_space=pltpu.VMEM)
    return _call(
        body, name="ada_exchange",
        out_shape=(jax.ShapeDtypeStruct((N_DEV, D_MODEL), F32), jax.ShapeDtypeStruct((N_DEV, N_DEV, W_ADA_SHARD), F32)),
        in_specs=[vmem, vmem], out_specs=(vmem, vmem),
        scratch_shapes=[pltpu.VMEM((N_DEV, 1, D_MODEL), F32), pltpu.SemaphoreType.DMA((2 * n,)),
                        pltpu.SemaphoreType.DMA((2 * n,))],
        compiler_params=_params(has_side_effects=True),
    )(c, w_ada_shard)


def _grad_w_ada(a_t, dm_shard):
    def body(a_ref, dm_ref, out_ref):
        acc = jnp.zeros((D_MODEL, W_ADA_SHARD), F32)
        for b in range(N_DEV):
            acc = acc + a_ref[:, b:b + 1] * dm_ref[b:b + 1, :]
        out_ref[...] = acc

    return _call(body, name="grad_w_ada", out_shape=jax.ShapeDtypeStruct((D_MODEL, W_ADA_SHARD), F32),
                 compiler_params=_params())(a_t, dm_shard)


def _sum_devices(parts):
    n = parts.shape[-1]

    def body(p_ref, out_ref):
        acc = p_ref[0]
        for b in range(1, N_DEV):
            acc = acc + p_ref[b]
        out_ref[...] = acc

    return _call(body, name="sum_devices", out_shape=jax.ShapeDtypeStruct((1, n), F32), compiler_params=_params())(parts)


def _add(a, b, name, out_dtype):
    r, cdim = a.shape
    tr = min(r, 256)

    def body(a_ref, b_ref, o_ref):
        o_ref[...] = (a_ref[...].astype(F32) + b_ref[...].astype(F32)).astype(out_dtype)

    spec = pl.BlockSpec((tr, cdim), lambda i: (i, 0))
    return _call(body, name=name, out_shape=jax.ShapeDtypeStruct(a.shape, out_dtype), grid=(r // tr,),
                 in_specs=[spec, spec], out_specs=spec, compiler_params=_params(("parallel",)))(a, b)


def _sum_chips(parts, name):
    _, r, cdim = parts.shape
    tr = min(r, 128)

    def body(p_ref, o_ref):
        o_ref[...] = ((p_ref[0].astype(F32) + p_ref[1].astype(F32)) + p_ref[2].astype(F32)) + p_ref[3].astype(F32)

    return _call(body, name=name, out_shape=jax.ShapeDtypeStruct((r, cdim), F32), grid=(r // tr,),
                 in_specs=[pl.BlockSpec((N_CHIPS, tr, cdim), lambda i: (0, i, 0))],
                 out_specs=pl.BlockSpec((tr, cdim), lambda i: (i, 0)), compiler_params=_params(("parallel",)))(parts)


def _adamw(w, g, m, v, name):
    r, cdim = w.shape[-2:]
    lead = w.ndim - 2
    tr = r if r <= 256 else max(t for t in range(8, ADAMW_BLOCK_ELEMS // cdim + 1, 8) if r % t == 0)
    c1 = 1.0 / (1.0 - ADAM_B1 ** ADAM_STEP)
    c2 = 1.0 / (1.0 - ADAM_B2 ** ADAM_STEP)

    def body(w_ref, g_ref, m_ref, v_ref, go_ref, d_ref, nm_ref, nv_ref):
        gv = g_ref[...].reshape(go_ref.shape)
        nm = ADAM_B1 * m_ref[...] + (1.0 - ADAM_B1) * gv
        nv = ADAM_B2 * v_ref[...] + (1.0 - ADAM_B2) * (gv * gv)
        m_hat = nm * c1
        v_hat = nv * c2
        go_ref[...] = gv
        d_ref[...] = -ADAM_LR * (m_hat / (jnp.sqrt(v_hat) + ADAM_EPS) + ADAM_WD * w_ref[...])
        nm_ref[...] = nm
        nv_ref[...] = nv

    spec = pl.BlockSpec((1,) * lead + (tr, cdim), lambda i: (0,) * lead + (i, 0))
    shp = jax.ShapeDtypeStruct(w.shape, F32)
    return _call(body, name=name, out_shape=(shp,) * 4, grid=(r // tr,),
                 in_specs=[spec, pl.BlockSpec((tr, cdim), lambda i: (i, 0)), spec, spec],
                 out_specs=(spec,) * 4, compiler_params=_params(("parallel",)))(w, g, m, v)


def _head_of_row(r, nc):
    assert nc & (nc - 1) == 0
    return lax.shift_right_logical(r, nc.bit_length() - 1)


def _chunk_mats(rows, nc, reverse):
    ri = lax.broadcasted_iota(jnp.int32, (rows, rows), 0)
    ci = lax.broadcasted_iota(jnp.int32, (rows, rows), 1)
    same = _head_of_row(ri, nc) == _head_of_row(ci, nc)
    between = jnp.where(same & ((ci > ri) if reverse else (ci < ri)), 1.0, 0.0).astype(F32)
    li = lax.broadcasted_iota(jnp.int32, (LANES, LANES), 0)
    lj = lax.broadcasted_iota(jnp.int32, (LANES, LANES), 1)
    within = jnp.where((li >= lj) if reverse else (li <= lj), 1.0, 0.0).astype(F32)
    return between, within


def _dot_hi(a, b):
    return jnp.dot(a, b, preferred_element_type=F32, precision=lax.Precision.HIGHEST)


def _scan_rows(t, nc, reverse):
    between, within = _chunk_mats(t.shape[0], nc, reverse)
    inner = _dot_hi(t, within)
    tot = jnp.sum(t, axis=1, keepdims=True)
    return inner + _dot_hi(between, jnp.broadcast_to(tot, t.shape))


def _log_forget_cumsum(f_rows, bias_rows, nc):
    def body(f_ref, b_ref, cum_ref):
        z = f_ref[...] + b_ref[...]
        lf = jnp.minimum(z, 0.0) - jnp.log(1.0 + jnp.exp(-jnp.abs(z)))
        cum_ref[...] = _scan_rows(lf, nc, False)

    return _call(body, name="forget_cumsum", out_shape=jax.ShapeDtypeStruct(f_rows.shape, F32),
                 compiler_params=_params())(f_rows, bias_rows)


def _log_forget_cumsum_bwd(dcum_rows, f_rows, bias_rows, nc):
    rows = f_rows.shape[0]

    def body(d_ref, f_ref, b_ref, df_ref, db_ref):
        dlf = _scan_rows(d_ref[...], nc, True)
        z = f_ref[...] + b_ref[...]
        df = dlf * _sigmoid(-z)
        df_ref[...] = df
        hi = lax.broadcasted_iota(jnp.int32, (FOX_HEADS, rows), 0)
        ri = lax.broadcasted_iota(jnp.int32, (FOX_HEADS, rows), 1)
        sel = jnp.where(_head_of_row(ri, nc) == hi, 1.0, 0.0).astype(F32)
        db_ref[...] = jnp.sum(_dot_hi(sel, df), axis=1, keepdims=True)

    return _call(body, name="forget_cumsum_bwd",
                 out_shape=(jax.ShapeDtypeStruct(f_rows.shape, F32), jax.ShapeDtypeStruct((FOX_HEADS, 1), F32)),
                 compiler_params=_params())(dcum_rows, f_rows, bias_rows)


def _rms_hat(xv):
    rstd = lax.rsqrt(jnp.mean(xv * xv, axis=-1, keepdims=True) + RMS_EPS)
    return xv * rstd, rstd


def _modulated(x_ref, g_ref, sc_ref, sh_ref):
    xhat, _ = _rms_hat(x_ref[...])
    return ((xhat * g_ref[...]) * sc_ref[...] + sh_ref[...]).astype(BF)


def _forget_logits(x, g_pre, scale1p, shift, w_f, tm):
    s = x.shape[0]

    def body(x_ref, g_ref, sc_ref, sh_ref, w_ref, f_ref):
        f_ref[...] = _dot(_modulated(x_ref, g_ref, sc_ref, sh_ref), w_ref[...])

    vec = _full((1, D_MODEL))
    return _call(
        body, name="forget_logits", out_shape=jax.ShapeDtypeStruct((s, LANES), F32), grid=(s // tm,),
        in_specs=[pl.BlockSpec((tm, D_MODEL), lambda i: (i, 0)), vec, vec, vec, _full((D_MODEL, LANES))],
        out_specs=pl.BlockSpec((tm, LANES), lambda i: (i, 0)), compiler_params=_params(("parallel",)),
    )(x, g_pre, scale1p, shift, w_f)


def _split3(v):
    hi = v.astype(BF).astype(F32)
    mid = (v - hi).astype(BF).astype(F32)
    lo = ((v - hi) - mid).astype(BF).astype(F32)
    return hi, mid, lo


def _in_proj(x, g_pre, scale1p, shift, w_rows, w_t_fox, cum, cos_t, sin_t, tm):
    s = x.shape[0]
    r_va, r_za, r_qb, r_zb, r_kb, r_vb = 0, 512, 1024, 1536, 2048, 2176

    def body(x_ref, g_ref, sc_ref, sh_ref, w_ref, wt_ref, cum_ref, cos_ref, sin_ref,
             h_ref, qat_ref, ka_ref, kat_ref, v_ref, vt_ref, za_ref, zb_ref, qb_ref, kb_ref, vb_ref,
             qbt_ref, kbt_ref, vbt_ref, mo_ref):
        hb = _modulated(x_ref, g_ref, sc_ref, sh_ref)
        h_ref[...] = hb

        def sec(c0, width):
            return _dot(hb, w_ref[:, c0:c0 + width])

        def sec_t(r0):
            return _dot_nt(wt_ref[r0:r0 + FOX_W, :], hb)

        q_t = sec_t(0) * Q_SCALE
        k_t = sec_t(FOX_W)
        v_t = sec_t(2 * FOX_W)
        va = sec(r_va, FOX_W)
        zeros = jnp.zeros((AUG_DIM - HEAD_DIM - AUG_ROWS, tm), F32)
        ri = lax.broadcasted_iota(jnp.int32, (AUG_ROWS, tm), 0)
        const = jnp.where(ri == AUG_ROWS - 1, 0.0, 1.0)
        ri_v = lax.broadcasted_iota(jnp.int32, (VT_ROWS - HEAD_DIM, tm), 0)
        v_feat = jnp.where(ri_v == 0, 1.0, 0.0).astype(BF)
        for hd in range(FOX_HEADS):
            rows = slice(hd * HEAD_DIM, (hd + 1) * HEAD_DIM)
            cum2 = cum_ref[hd:hd + 1, :] * LOG2E
            hi, mid, lo = (jnp.broadcast_to(part, (AUG_ROWS, tm)) for part in _split3(cum2))
            q_feat = jnp.where(ri == 1, hi, jnp.where(ri == 2, mid, jnp.where(ri == 3, lo, const)))
            k_feat = jnp.where(ri == 4, -hi, jnp.where(ri == 5, -mid, jnp.where(ri == 6, -lo, const)))
            q_aug = jnp.concatenate([q_t[rows], q_feat, zeros], axis=0)
            k_aug = jnp.concatenate([k_t[rows], k_feat, zeros], axis=0)
            mo_ref[hd:hd + 1, :] = jnp.sum(q_t[rows] * k_t[rows], axis=0, keepdims=True) + 1.0
            qat_ref[hd] = q_aug.astype(BF)
            kat_ref[hd] = k_aug.astype(BF)
            ka_ref[hd] = k_aug.T.astype(BF)
            vt_ref[hd] = jnp.concatenate([v_t[rows].astype(BF), v_feat], axis=0)
            v_ref[hd] = va[:, rows].astype(BF)
        za_ref[...] = sec(r_za, FOX_W)
        zb_ref[...] = sec(r_zb, SWA_W)
        cos2, sin2 = cos_ref[...], sin_ref[...]
        cos8 = jnp.concatenate([cos2] * 4, axis=1)
        sin8 = jnp.concatenate([sin2] * 4, axis=1)
        qb = sec(r_qb, SWA_W)
        qb = (qb * cos8 + _rope_partner(qb) * sin8) * (HEAD_DIM ** -0.5)
        qb_ref[...] = qb.astype(BF)
        for a in range(SWA_W // LANES):
            qbt_ref[a * LANES:(a + 1) * LANES, :] = qb[:, a * LANES:(a + 1) * LANES].T.astype(BF)
        kb = sec(r_kb, SWA_KV_W)
        kb = kb * cos2 + _rope_partner(kb) * sin2
        vb = sec(r_vb, SWA_KV_W)
        kb_t, vb_t = kb.T, vb.T
        for hd in range(SWA_KV_HEADS):
            sl = slice(hd * HEAD_DIM, (hd + 1) * HEAD_DIM)
            kb_ref[hd] = kb[:, sl].astype(BF)
            vb_ref[hd] = vb[:, sl].astype(BF)
            kbt_ref[hd] = kb_t[sl].astype(BF)
            vbt_ref[hd] = jnp.concatenate([vb_t[sl].astype(BF), v_feat], axis=0)

    row = lambda w: pl.BlockSpec((tm, w), lambda i: (i, 0))
    heads = lambda n, w=HEAD_DIM: pl.BlockSpec((n, tm, w), lambda i: (0, i, 0))
    heads_t = lambda w: pl.BlockSpec((FOX_HEADS, w, tm), lambda i: (0, 0, i))
    vec = _full((1, D_MODEL))
    hs = lambda a, b: jax.ShapeDtypeStruct((FOX_HEADS, a, b), BF)
    out_shape = (
        jax.ShapeDtypeStruct((s, D_MODEL), BF),
        hs(AUG_DIM, s), hs(s, AUG_DIM), hs(AUG_DIM, s), hs(s, HEAD_DIM), hs(VT_ROWS, s),
        jax.ShapeDtypeStruct((s, FOX_W), F32), jax.ShapeDtypeStruct((s, SWA_W), F32),
        jax.ShapeDtypeStruct((s, SWA_W), BF),
        jax.ShapeDtypeStruct((SWA_KV_HEADS, s, HEAD_DIM), BF), jax.ShapeDtypeStruct((SWA_KV_HEADS, s, HEAD_DIM), BF),
        jax.ShapeDtypeStruct((SWA_W, s), BF),
        jax.ShapeDtypeStruct((SWA_KV_HEADS, HEAD_DIM, s), BF), jax.ShapeDtypeStruct((SWA_KV_HEADS, VT_ROWS, s), BF),
        jax.ShapeDtypeStruct((FOX_HEADS, s), F32),
    )
    kv_t = lambda w: pl.BlockSpec((SWA_KV_HEADS, w, tm), lambda i: (0, 0, i))
    return _call(
        body, name="in_proj", out_shape=out_shape, grid=(s // tm,),
        in_specs=[row(D_MODEL), vec, vec, vec, _full(w_rows.shape), _full(w_t_fox.shape),
                  pl.BlockSpec((FOX_HEADS, tm), lambda i: (0, i)), row(LANES), row(LANES)],
        out_specs=(row(D_MODEL), heads_t(AUG_DIM), heads(FOX_HEADS, AUG_DIM), heads_t(AUG_DIM), heads(FOX_HEADS),
                   heads_t(VT_ROWS), row(FOX_W), row(SWA_W), row(SWA_W), heads(SWA_KV_HEADS), heads(SWA_KV_HEADS),
                   pl.BlockSpec((SWA_W, tm), lambda i: (0, i)), kv_t(HEAD_DIM), kv_t(VT_ROWS),
                   pl.BlockSpec((FOX_HEADS, tm), lambda i: (0, i))),
        compiler_params=_params(("parallel",)),
    )(x, g_pre, scale1p, shift, w_rows, w_t_fox, cum, cos_t, sin_t)


def _diag_chunks(d, bq, bk, chunk):
    out = []
    for c0 in range(0, bq, chunk):
        if d is None or d * bk + bk - 1 <= c0:
            out.append((c0, None, bk))
        elif d * bk <= c0 + chunk - 1:
            n_keys = min(bk, c0 + chunk - d * bk)
            kpos = d * bk + lax.broadcasted_iota(jnp.int32, (n_keys, chunk), 0)
            qpos = c0 + lax.broadcasted_iota(jnp.int32, (n_keys, chunk), 1)
            out.append((c0, kpos <= qpos, n_keys))
    return out


def _fox_fwd(qat, ka, vt, m_own, bq, bk, chunk, running_max):
    nh, _, s = qat.shape
    r = bq // bk

    pairs = [(i, j) for i in range(s // bq) for j in range(i * r + r)]

    def body(i_tab, j_tab, ka_ref, qat_ref, vt_ref, mo_ref, o_ref, lse_ref, bad_ref, *rest):
        pt_ref, m_scr, acc_scr = (None,) * running_max + rest
        i, j = i_tab[pl.program_id(1)], j_tab[pl.program_id(1)]

        @pl.when(j == 0)
        def _():
            m_scr[...] = jnp.full(m_scr.shape, NEG, F32) if running_max else mo_ref[0]
            acc_scr[...] = jnp.zeros(acc_scr.shape, F32)

        def careful(d):
            kv, vtv = ka_ref[0], vt_ref[0]

            def one_chunk(n, carry):
                c0 = pl.multiple_of(n * chunk, chunk)
                cs = pl.ds(c0, chunk)
                sc = _dot(kv, qat_ref[0, :, cs])
                if d is not None:
                    kpos = d * bk + lax.broadcasted_iota(jnp.int32, (bk, chunk), 0)
                    qpos = c0 + lax.broadcasted_iota(jnp.int32, (bk, chunk), 1)
                    sc = jnp.where(kpos <= qpos, sc, NEG)
                m_prev = m_scr[:, cs]
                m_new = jnp.maximum(m_prev, jnp.max(sc, axis=0, keepdims=True))
                p = jnp.exp2(sc - m_new).astype(BF)
                acc_scr[:, cs] = jnp.exp2(m_prev - m_new) * acc_scr[:, cs] + _dot(vtv, p)
                m_scr[:, cs] = m_new
                return carry

            lax.fori_loop(0, bq // chunk, one_chunk, 0)

        def fast(d):
            todo = _diag_chunks(d, bq, bk, chunk)
            scores = lambda t: _dot(ka_ref[0, :t[2], :], qat_ref[0, :, t[0]:t[0] + chunk])
            sc_next = scores(todo[0])
            for n, (c0, mask, n_keys) in enumerate(todo):
                cs = slice(c0, c0 + chunk)
                sc = sc_next
                if n + 1 < len(todo):
                    sc_next = scores(todo[n + 1])
                if mask is not None:
                    sc = jnp.where(mask, sc, NEG)
                p = jnp.exp2(sc - m_scr[:, cs]).astype(BF)
                pt_ref[0, :n_keys, cs] = p
                acc_scr[:, cs] += _dot(vt_ref[0, :, :n_keys], p)

        step = careful if running_max else fast

        @pl.when(j < i * r)
        def _():
            step(None)

        for d in range(r):
            @pl.when(j == i * r + d)
            def _(d=d):
                step(d)

        @pl.when(j == i * r + r - 1)
        def _():
            l = acc_scr[HEAD_DIM:HEAD_DIM + 1, :]
            o_ref[0] = acc_scr[:HEAD_DIM, :] / l
            lse_ref[0] = m_scr[...] + jnp.log2(l)
            bad_ref[0] = jnp.where(l < OVERFLOW_GUARD, 0.0, 1.0)

    qmap_t = lambda h, t, it, jt: (h, 0, it[t])
    qrow = pl.BlockSpec((1, 1, bq), qmap_t)
    row_shape = jax.ShapeDtypeStruct((nh, 1, s), F32)
    out_shape = (jax.ShapeDtypeStruct((nh, HEAD_DIM, s), F32), row_shape, row_shape)
    out_specs = (pl.BlockSpec((1, HEAD_DIM, bq), qmap_t), qrow, qrow)
    if not running_max:
        out_shape += (jax.ShapeDtypeStruct((nh, s, s), BF),)
        out_specs += (pl.BlockSpec((1, bk, bq), lambda h, t, it, jt: (h, jt[t], it[t])),)
    grid_spec = pltpu.PrefetchScalarGridSpec(
        num_scalar_prefetch=2, grid=(nh, len(pairs)),
        in_specs=[pl.BlockSpec((1, bk, AUG_DIM), lambda h, t, it, jt: (h, jt[t], 0)), pl.BlockSpec((1, AUG_DIM, bq), qmap_t),
                  pl.BlockSpec((1, VT_ROWS, bk), lambda h, t, it, jt: (h, 0, jt[t])), qrow],
        out_specs=out_specs,
        scratch_shapes=[pltpu.VMEM((1, bq), F32), pltpu.VMEM((VT_ROWS, bq), F32)])
    return _call(
        body, name="fox_fwd_running_max" if running_max else "fox_fwd", out_shape=out_shape, grid_spec=grid_spec,
        compiler_params=_params(("parallel", "arbitrary")),
    )(jnp.asarray([p[0] for p in pairs], jnp.int32), jnp.asarray([p[1] for p in pairs], jnp.int32), ka, qat, vt, m_own)


def _fox_bwd(qat, ka, kat, v, dot_, lse, delta, pt, bq, bk, chunk, dq_blk):
    nh, _, s = qat.shape
    r = bq // bk
    nq = s // bq
    stored = pt is not None

    pairs = [(j, i) for j in range(s // bk) for i in range(j // r, nq)]

    def body(j_tab, i_tab, a_ref, b_ref, kat_ref, v_ref, qat_ref, do_ref, dl_ref, dq_ref, dk_ref, dv_ref, dk_scr, dv_scr):
        ka_ref, lse_ref, pt_ref = (None, None, a_ref) if stored else (a_ref, b_ref, None)
        j, i = j_tab[pl.program_id(1)], i_tab[pl.program_id(1)]

        @pl.when(pl.program_id(1) == 0)
        def _():
            dq_ref[...] = jnp.zeros(dq_ref.shape, F32)

        @pl.when(i * r <= j)
        def _():
            dk_scr[...] = jnp.zeros(dk_scr.shape, F32)
            dv_scr[...] = jnp.zeros(dv_scr.shape, F32)

        def step(d):
            todo = _diag_chunks(d, bq, bk, chunk)

            def products(t):
                cs = slice(t[0], t[0] + chunk)
                return (None if stored else _dot(ka_ref[0, :t[2], :], qat_ref[0, :, cs]),
                        _dot(v_ref[0, :t[2], :], do_ref[0, :, cs]))

            nxt = products(todo[0])
            for n, (c0, mask, n_keys) in enumerate(todo):
                cs = slice(c0, c0 + chunk)
                sc, dp = nxt
                if n + 1 < len(todo):
                    nxt = products(todo[n + 1])
                if stored:
                    p_bf = pt_ref[0, :n_keys, cs]
                    p = p_bf.astype(F32)
                else:
                    p = jnp.exp2(sc - lse_ref[0, :, cs])
                    if mask is not None:
                        p = jnp.where(mask, p, 0.0)
                    p_bf = p.astype(BF)
                ds = (p * (dp - dl_ref[0, :, cs])).astype(BF)
                dv_scr[:, :n_keys] += _dot_nt(do_ref[0, :, cs], p_bf)
                dk_scr[:, :n_keys] += _dot_nt(qat_ref[0, :VT_ROWS, cs], ds)
                c1 = c0 % dq_blk
                dq_ref[0, i * (bq // dq_blk) + c0 // dq_blk, :, c1:c1 + chunk] += _dot(kat_ref[0, :VT_ROWS, :n_keys], ds)

        @pl.when(i * r > j)
        def _():
            step(None)

        for d in range(r):
            @pl.when(j == i * r + d)
            def _(d=d):
                step(d)

        @pl.when(i == nq - 1)
        def _():
            dk_ref[0] = dk_scr[...]
            dv_ref[0] = dv_scr[...]

    qmap = lambda h, t, jt, it: (h, 0, it[t])
    kmap = lambda h, t, jt, it: (h, jt[t], 0)
    kmap_t = lambda h, t, jt, it: (h, 0, jt[t])
    if stored:
        first = [(pt, pl.BlockSpec((1, bk, bq), lambda h, t, jt, it: (h, jt[t], it[t]))),
                 (delta, pl.BlockSpec((1, 1, bq), qmap))]
    else:
        first = [(ka, pl.BlockSpec((1, bk, AUG_DIM), kmap)), (lse, pl.BlockSpec((1, 1, bq), qmap))]
    grid_spec = pltpu.PrefetchScalarGridSpec(
        num_scalar_prefetch=2, grid=(nh, len(pairs)),
        in_specs=[first[0][1], first[1][1], pl.BlockSpec((1, AUG_DIM, bk), kmap_t), pl.BlockSpec((1, bk, HEAD_DIM), kmap),
                  pl.BlockSpec((1, AUG_DIM, bq), qmap), pl.BlockSpec((1, HEAD_DIM, bq), qmap),
                  pl.BlockSpec((1, 1, bq), qmap)],
        out_specs=(pl.BlockSpec((1, s // dq_blk, VT_ROWS, dq_blk), lambda h, t, jt, it: (h, 0, 0, 0)),
                   pl.BlockSpec((1, VT_ROWS, bk), kmap_t), pl.BlockSpec((1, HEAD_DIM, bk), kmap_t)),
        scratch_shapes=[pltpu.VMEM((VT_ROWS, bk), F32), pltpu.VMEM((HEAD_DIM, bk), F32)])
    return _call(
        body, name="fox_bwd" if stored else "fox_bwd_recompute",
        out_shape=(jax.ShapeDtypeStruct((nh, s // dq_blk, VT_ROWS, dq_blk), F32),
                   jax.ShapeDtypeStruct((nh, VT_ROWS, s), F32), jax.ShapeDtypeStruct((nh, HEAD_DIM, s), F32)),
        grid_spec=grid_spec, compiler_params=_params(("parallel", "arbitrary")),
    )(jnp.asarray([p[0] for p in pairs], jnp.int32), jnp.asarray([p[1] for p in pairs], jnp.int32),
      first[0][0], first[1][0], kat, v, qat, dot_, delta)


def _swa_mask(i, tq):
    kpos = i * tq - WINDOW + lax.broadcasted_iota(jnp.int32, (tq + WINDOW, tq), 0)
    qpos = i * tq + lax.broadcasted_iota(jnp.int32, (tq + WINDOW, tq), 1)
    rel = qpos - kpos
    return (rel >= 0) & (rel < WINDOW) & (kpos >= 0)


def _swa_rows(ref, g, i, tq):
    before = pl.multiple_of(jnp.maximum(i * tq - WINDOW, 0), WINDOW)
    return jnp.concatenate([ref[g, pl.ds(before, WINDOW), :], ref[g, pl.ds(pl.multiple_of(i * tq, tq), tq), :]], axis=0)


def _swa_before(n_rows, tq):
    return pl.BlockSpec((SWA_KV_HEADS, n_rows, WINDOW), lambda i: (0, 0, jnp.maximum(i * (tq // WINDOW) - 1, 0)))


def _swa_probs_t(sc, mask, sink):
    sc = jnp.where(mask, sc, NEG)
    m = jnp.maximum(jnp.max(sc, axis=0, keepdims=True), sink)
    p = jnp.exp(sc - m)
    e_sink = jnp.exp(sink - m)
    inv_l = 1.0 / (jnp.sum(p, axis=0, keepdims=True) + e_sink)
    return p * inv_l, e_sink * inv_l


def _swa_fwd(qbt, kb, vbt, sinks, tq):
    s = qbt.shape[1]
    n_heads = SWA_KV_HEADS * SWA_GROUP

    def body(q_ref, k_ref, vb_ref, vc_ref, s_ref, o_ref):
        i = pl.program_id(0)
        mask = _swa_mask(i, tq)
        kw = [_swa_rows(k_ref, g, i, tq) for g in range(SWA_KV_HEADS)]
        vtw = [jnp.concatenate([vb_ref[g], vc_ref[g]], axis=1) for g in range(SWA_KV_HEADS)]
        scores = lambda hd: _dot(kw[hd // SWA_GROUP], q_ref[hd * HEAD_DIM:(hd + 1) * HEAD_DIM, :])
        sc_next = scores(0)
        for hd in range(n_heads):
            g, hh = divmod(hd, SWA_GROUP)
            rows = slice(hd * HEAD_DIM, (hd + 1) * HEAD_DIM)
            sink = s_ref[g][:, hh:hh + 1]
            sc = jnp.where(mask, sc_next, NEG)
            if hd + 1 < n_heads:
                sc_next = scores(hd + 1)
            m = jnp.maximum(jnp.max(sc, axis=0, keepdims=True), sink)
            acc = _dot(vtw[g], jnp.exp(sc - m).astype(BF))
            o_ref[rows, :] = acc[:HEAD_DIM] / (acc[HEAD_DIM:HEAD_DIM + 1] + jnp.exp(sink - m))

    kvspec = _full((SWA_KV_HEADS, s, HEAD_DIM))
    qspec = pl.BlockSpec((SWA_W, tq), lambda i: (0, i))
    return _call(
        body, name="swa_fwd", out_shape=jax.ShapeDtypeStruct((SWA_W, s), F32), grid=(s // tq,),
        in_specs=[qspec, kvspec, _swa_before(VT_ROWS, tq), pl.BlockSpec((SWA_KV_HEADS, VT_ROWS, tq), lambda i: (0, 0, i)),
                  _full((SWA_KV_HEADS, 1, SWA_GROUP))],
        out_specs=qspec, compiler_params=_params(("parallel",)),
    )(qbt, kb, vbt, vbt, sinks)


def _swa_bwd(qb, qbt, kb, kbt, vb, sinks, dob, dobt, tq):
    s = qb.shape[0]
    n_heads = SWA_KV_HEADS * SWA_GROUP

    def body(q_ref, qt_ref, k_ref, ktb_ref, ktc_ref, v_ref, s_ref, do_ref, dot_ref, dq_ref, dk_ref, dv_ref, ds_ref):
        i = pl.program_id(0)

        @pl.when(i == 0)
        def _():
            dk_ref[...] = jnp.zeros(dk_ref.shape, F32)
            dv_ref[...] = jnp.zeros(dv_ref.shape, F32)
            ds_ref[...] = jnp.zeros(ds_ref.shape, F32)

        mask = _swa_mask(i, tq)
        kw = [_swa_rows(k_ref, g, i, tq) for g in range(SWA_KV_HEADS)]
        vw = [_swa_rows(v_ref, g, i, tq) for g in range(SWA_KV_HEADS)]
        ktw = [jnp.concatenate([ktb_ref[g], ktc_ref[g]], axis=1) for g in range(SWA_KV_HEADS)]
        before = pl.ds(pl.multiple_of(jnp.maximum(i * tq - WINDOW, 0), WINDOW), WINDOW)
        own = pl.ds(pl.multiple_of(i * tq, tq), tq)

        def products(hd):
            rows = slice(hd * HEAD_DIM, (hd + 1) * HEAD_DIM)
            return _dot(kw[hd // SWA_GROUP], qt_ref[rows, :]), _dot(vw[hd // SWA_GROUP], dot_ref[rows, :])

        nxt = products(0)
        for g in range(SWA_KV_HEADS):
            dsinks = []
            dk_acc = jnp.zeros((tq + WINDOW, HEAD_DIM), F32)
            dv_acc = jnp.zeros((tq + WINDOW, HEAD_DIM), F32)
            for hh in range(SWA_GROUP):
                hd = g * SWA_GROUP + hh
                rows = slice(hd * HEAD_DIM, (hd + 1) * HEAD_DIM)
                sc, dp = nxt
                if hd + 1 < n_heads:
                    nxt = products(hd + 1)
                p, p_sink = _swa_probs_t(sc, mask, s_ref[g][:, hh:hh + 1])
                delta = jnp.sum(p * dp, axis=0, keepdims=True)
                dsc = (p * (dp - delta)).astype(BF)
                dq_ref[rows, :] = _dot(ktw[g], dsc)
                dk_acc = dk_acc + _dot(dsc, q_ref[:, rows])
                dv_acc = dv_acc + _dot(p.astype(BF), do_ref[:, rows])
                dsinks.append(-jnp.sum(p_sink * delta, axis=1, keepdims=True))
            dk_ref[g, before, :] += dk_acc[:WINDOW]
            dk_ref[g, own, :] += dk_acc[WINDOW:]
            dv_ref[g, before, :] += dv_acc[:WINDOW]
            dv_ref[g, own, :] += dv_acc[WINDOW:]
            ds_ref[g] += jnp.concatenate(dsinks, axis=1)

    kvspec = _full((SWA_KV_HEADS, s, HEAD_DIM))
    qspec = pl.BlockSpec((tq, SWA_W), lambda i: (i, 0))
    qspec_t = pl.BlockSpec((SWA_W, tq), lambda i: (0, i))
    sspec = _full((SWA_KV_HEADS, 1, SWA_GROUP))
    kvshape = jax.ShapeDtypeStruct((SWA_KV_HEADS, s, HEAD_DIM), F32)
    return _call(
        body, name="swa_bwd",
        out_shape=(jax.ShapeDtypeStruct((SWA_W, s), F32), kvshape, kvshape,
                   jax.ShapeDtypeStruct((SWA_KV_HEADS, 1, SWA_GROUP), F32)),
        grid=(s // tq,),
        in_specs=[qspec, qspec_t, kvspec, _swa_before(HEAD_DIM, tq),
                  pl.BlockSpec((SWA_KV_HEADS, HEAD_DIM, tq), lambda i: (0, 0, i)), kvspec, sspec, qspec, qspec_t],
        out_specs=(qspec_t, kvspec, kvspec, sspec),
        compiler_params=_params(("arbitrary",)),
    )(qb, qbt, kb, kbt, kbt, vb, sinks, dob, dobt)


def _pairs_to_rows(ref, n_rows=HEAD_DIM):
    parts = []
    for a in range(0, FOX_HEADS, 2):
        parts.append(jnp.concatenate([ref[a][:n_rows], ref[a + 1][:n_rows]], axis=0).T)
    return jnp.concatenate(parts, axis=1)


def _blocks_to_rows(ref):
    return jnp.concatenate([ref[a:a + LANES, :].T for a in range(0, ref.shape[0], LANES)], axis=1)


def _out_proj(oat, za, obt, zb, x, tgt, w_out, w_out_t, gate, g_post, inv_l, tm):
    s = x.shape[0]

    def body(oat_ref, za_ref, obt_ref, zb_ref, x_ref, t_ref, w_ref, wt_ref, gate_ref, gp_ref, il_ref,
             dout_ref, doat_ref, dla_ref, dza_ref, dob_ref, dobt_ref, dzb_ref, gw_ref, dgate_ref, dgp_ref, loss_ref):
        i = pl.program_id(0)

        @pl.when(i == 0)
        def _():
            gw_ref[...] = jnp.zeros(gw_ref.shape, F32)
            dgate_ref[...] = jnp.zeros(dgate_ref.shape, F32)
            dgp_ref[...] = jnp.zeros(dgp_ref.shape, F32)
            loss_ref[...] = jnp.zeros(loss_ref.shape, F32)

        oa_v = _pairs_to_rows(oat_ref)
        ob_v = _blocks_to_rows(obt_ref)
        za_v, zb_v = za_ref[...], zb_ref[...]
        sga, sgb = _sigmoid(za_v), _sigmoid(zb_v)
        sila, silb = za_v * sga, zb_v * sgb
        u = jnp.concatenate([oa_v * sila, ob_v * silb], axis=1).astype(BF)
        yv = _dot(u, w_ref[...])
        yhat, rstd = _rms_hat(yv)
        gp, gate_v = gp_ref[...], gate_ref[...]
        nrm = yhat * gp
        diff = (x_ref[...] + gate_v * nrm) - t_ref[...]
        loss_ref[...] += 0.5 * jnp.sum(jnp.sum(diff * diff, axis=1, keepdims=True), axis=0, keepdims=True) / D_MODEL
        dout = diff * (1.0 / D_MODEL)
        dout_ref[...] = dout
        dgate_ref[...] += jnp.sum(dout * nrm, axis=0, keepdims=True)
        dn = dout * gate_v
        dgp_ref[...] += jnp.sum(dn * yhat, axis=0, keepdims=True)
        dyhat = dn * gp
        dy = (rstd * (dyhat - yhat * jnp.mean(dyhat * yhat, axis=1, keepdims=True))).astype(BF)
        gw_ref[...] += _dot_tn(u, dy)
        du = _dot(dy, wt_ref[...])
        dua, dub = du[:, :FOX_W], du[:, FOX_W:]
        doa = dua * sila
        for a in range(0, FOX_HEADS, 2):
            pair_t = doa[:, a * HEAD_DIM:(a + 2) * HEAD_DIM].T
            for hd, rows in ((a, slice(0, HEAD_DIM)), (a + 1, slice(HEAD_DIM, 2 * HEAD_DIM))):
                inv_l = il_ref[hd]
                doat_ref[hd] = (pair_t[rows] * inv_l).astype(BF)
                dla_ref[hd] = jnp.sum(pair_t[rows] * oat_ref[hd], axis=0, keepdims=True) * inv_l
        dob = dub * silb
        dob_ref[...] = dob.astype(BF)
        for a in range(0, SWA_W, LANES):
            dobt_ref[a:a + LANES, :] = dob[:, a:a + LANES].T.astype(BF)
        dza_ref[...] = (dua * oa_v * (sga * (1.0 + za_v * (1.0 - sga)))).astype(BF)
        dzb_ref[...] = (dub * ob_v * (sgb * (1.0 + zb_v * (1.0 - sgb)))).astype(BF)

    row = lambda w: pl.BlockSpec((tm, w), lambda i: (i, 0))
    heads_t = lambda w: pl.BlockSpec((FOX_HEADS, w, tm), lambda i: (0, 0, i))
    vec = _full((1, D_MODEL))
    mat = _full((D_MODEL, D_MODEL))
    out_shape = (
        jax.ShapeDtypeStruct((s, D_MODEL), F32),
        jax.ShapeDtypeStruct((FOX_HEADS, HEAD_DIM, s), BF), jax.ShapeDtypeStruct((FOX_HEADS, 1, s), F32),
        jax.ShapeDtypeStruct((s, FOX_W), BF), jax.ShapeDtypeStruct((s, SWA_W), BF), jax.ShapeDtypeStruct((SWA_W, s), BF),
        jax.ShapeDtypeStruct((s, SWA_W), BF),
        jax.ShapeDtypeStruct((D_MODEL, D_MODEL), F32),
        jax.ShapeDtypeStruct((1, D_MODEL), F32), jax.ShapeDtypeStruct((1, D_MODEL), F32),
        jax.ShapeDtypeStruct((1, 1), F32),
    )
    col = pl.BlockSpec((SWA_W, tm), lambda i: (0, i))
    return _call(
        body, name="out_proj", out_shape=out_shape, grid=(s // tm,),
        in_specs=[heads_t(HEAD_DIM), row(FOX_W), col, row(SWA_W), row(D_MODEL), row(D_MODEL), mat, mat, vec, vec,
                  heads_t(1)],
        out_specs=(row(D_MODEL), heads_t(HEAD_DIM), heads_t(1), row(FOX_W), row(SWA_W), col, row(SWA_W), mat, vec, vec,
                   _full((1, 1))),
        compiler_params=_params(("arbitrary",)),
    )(oat, za, obt, zb, x, tgt, w_out, w_out_t, gate, g_post, inv_l)


def _assemble_dproj(dqt, dkt, dvt, dza, dqb, dzb, dkb, dvb, df, cos_t, sin_t, tm):
    s = dza.shape[0]

    def body(dqt_ref, dkt_ref, dvt_ref, dza_ref, dqb_ref, dzb_ref, dkb_ref, dvb_ref, df_ref, cos_ref, sin_ref, o_ref):
        def cat(ref, n):
            return jnp.concatenate([ref[hd] for hd in range(n)], axis=1)

        cos2, sin2 = cos_ref[...], sin_ref[...]
        cos8 = jnp.concatenate([cos2] * 4, axis=1)
        sin8 = jnp.concatenate([sin2] * 4, axis=1)
        scale = HEAD_DIM ** -0.5
        o_ref[:, C_QA:C_QA + FOX_W] = (_pairs_to_rows(dqt_ref.at[:, 0]) * scale).astype(BF)
        o_ref[:, C_KA:C_KA + FOX_W] = (_pairs_to_rows(dkt_ref) * LN2).astype(BF)
        o_ref[:, C_VA:C_VA + FOX_W] = _pairs_to_rows(dvt_ref).astype(BF)
        o_ref[:, C_ZA:C_ZA + FOX_W] = dza_ref[...]
        dq = _blocks_to_rows(dqb_ref) * scale
        o_ref[:, C_QB:C_QB + SWA_W] = (dq * cos8 - _rope_partner(dq) * sin8).astype(BF)
        o_ref[:, C_ZB:C_ZB + SWA_W] = dzb_ref[...]
        dk = cat(dkb_ref, SWA_KV_HEADS)
        o_ref[:, C_KB:C_KB + SWA_KV_W] = (dk * cos2 - _rope_partner(dk) * sin2).astype(BF)
        o_ref[:, C_VB:C_VB + SWA_KV_W] = cat(dvb_ref, SWA_KV_HEADS).astype(BF)
        o_ref[:, C_F:C_F + LANES] = df_ref[...].astype(BF)

    row = lambda w: pl.BlockSpec((tm, w), lambda i: (i, 0))
    heads = lambda n: pl.BlockSpec((n, tm, HEAD_DIM), lambda i: (0, i, 0))
    heads_t = lambda w: pl.BlockSpec((FOX_HEADS, w, tm), lambda i: (0, 0, i))
    return _call(
        body, name="assemble_dproj", out_shape=jax.ShapeDtypeStruct((s, WP), BF), grid=(s // tm,),
        in_specs=[pl.BlockSpec((FOX_HEADS, 1, VT_ROWS, tm), lambda i: (0, i, 0, 0)), heads_t(VT_ROWS), heads_t(HEAD_DIM),
                  row(FOX_W), pl.BlockSpec((SWA_W, tm), lambda i: (0, i)), row(SWA_W), heads(SWA_KV_HEADS),
                  heads(SWA_KV_HEADS), row(LANES), row(LANES), row(LANES)],
        out_specs=row(WP), compiler_params=_params(("parallel",)),
    )(dqt, dkt, dvt, dza, dqb, dzb, dkb, dvb, df, cos_t, sin_t)


def _in_proj_bwd_x(dproj, w_al_t, x, dout, g_pre, scale1p, tm, parts):
    s = x.shape[0]
    n_steps = s // tm
    masks = [(1, 0), (0, 1), (1, 1)]

    def body(dp_ref, wt_ref, x_ref, dout_ref, g_ref, sc_ref, parts_ref, gx_ref, dsh_ref, dsc_ref, dg_ref, got_ref,
             send_sems, recv_sems, local_sem):
        i = pl.program_id(0)
        cx, cy, cc = lax.axis_index("x"), lax.axis_index("y"), lax.axis_index("c")
        me = 2 * cx + cy
        own = pltpu.make_async_copy(parts_ref.at[me], got_ref.at[me], local_sem)

        def copy(k, send):
            dx, dy = masks[k]
            peer = 2 * (cx ^ dx) + (cy ^ dy)
            return pltpu.make_async_remote_copy(
                src_ref=parts_ref.at[peer if send else me], dst_ref=got_ref.at[me if send else peer],
                send_sem=send_sems.at[k], recv_sem=recv_sems.at[k], device_id=(cx ^ dx, cy ^ dy, cc), device_id_type=MESH)

        @pl.when(i == 0)
        def _():
            dsh_ref[...] = jnp.zeros(dsh_ref.shape, F32)
            dsc_ref[...] = jnp.zeros(dsc_ref.shape, F32)
            dg_ref[...] = jnp.zeros(dg_ref.shape, F32)
            own.start()
            for k in range(len(masks)):
                copy(k, True).start()

        @pl.when(i == n_steps - 1)
        def _():
            for k in range(len(masks)):
                copy(k, False).wait_recv()
            for k in range(len(masks)):
                copy(k, True).wait_send()
            own.wait()

        dh = _dot(dp_ref[...], wt_ref[...])
        xhat, rstd = _rms_hat(x_ref[...])
        g, sc = g_ref[...], sc_ref[...]
        dsh_ref[...] += jnp.sum(dh, axis=0, keepdims=True)
        dhx = dh * xhat
        dsc_ref[...] += jnp.sum(dhx * g, axis=0, keepdims=True)
        dg_ref[...] += jnp.sum(dhx * sc, axis=0, keepdims=True)
        dxhat = dh * (g * sc)
        gx_ref[...] = dout_ref[...] + rstd * (dxhat - xhat * jnp.mean(dxhat * xhat, axis=1, keepdims=True))

    row = lambda w: pl.BlockSpec((tm, w), lambda i: (i, 0))
    vec = _full((1, D_MODEL))
    vshape = jax.ShapeDtypeStruct((1, D_MODEL), F32)
    hbm = pl.BlockSpec(memory_space=pl.ANY)
    return _call(
        body, name="in_proj_bwd_x",
        out_shape=(jax.ShapeDtypeStruct((s, D_MODEL), F32), vshape, vshape, vshape,
                   jax.ShapeDtypeStruct(parts.shape, parts.dtype)),
        grid=(n_steps,),
        in_specs=[row(WP), _full((WP, D_MODEL)), row(D_MODEL), row(D_MODEL), vec, vec, hbm],
        out_specs=(row(D_MODEL), vec, vec, vec, hbm),
        scratch_shapes=[pltpu.SemaphoreType.DMA((3,)), pltpu.SemaphoreType.DMA((3,)), pltpu.SemaphoreType.DMA],
        compiler_params=_params(("arbitrary",), has_side_effects=True),
    )(dproj, w_al_t, x, dout, g_pre, scale1p, parts)


def _in_proj_bwd_w(h, dproj, tk, tn):
    s = h.shape[0]
    n_k = s // tk

    def body(h_ref, dp_ref, gw_ref, acc_scr):
        k = pl.program_id(1)

        @pl.when(k == 0)
        def _():
            acc_scr[...] = jnp.zeros(acc_scr.shape, F32)

        acc_scr[...] += _dot_tn(h_ref[...], dp_ref[...])

        @pl.when(k == n_k - 1)
        def _():
            gw_ref[...] = acc_scr[...].astype(BF)

    return _call(
        body, name="in_proj_bwd_w", out_shape=jax.ShapeDtypeStruct((D_MODEL, WP), BF), grid=(WP // tn, n_k),
        in_specs=[pl.BlockSpec((tk, D_MODEL), lambda n, k: (k, 0)), pl.BlockSpec((tk, tn), lambda n, k: (k, n))],
        out_specs=pl.BlockSpec((D_MODEL, tn), lambda n, k: (0, n)),
        scratch_shapes=[pltpu.VMEM((D_MODEL, tn), F32)],
        compiler_params=_params(("parallel", "arbitrary")),
    )(h, dproj)


def _align_w_in(w_cols):
    def part(name, width):
        return w_cols[:, _SRC[name]:_SRC[name] + width]

    fpad = jnp.pad(part("fa", FOX_HEADS), ((0, 0), (0, LANES - FOX_HEADS)))
    return jnp.concatenate([part("qa", FOX_W), part("ka", FOX_W), part("va", FOX_W), part("za", FOX_W),
                            part("qb", SWA_W), part("zb", SWA_W), part("kb", SWA_KV_W), part("vb", SWA_KV_W), fpad], axis=1)


def _unalign_w_in(g_al):
    def part(c0, width):
        return g_al[:, c0:c0 + width]

    return jnp.concatenate([part(C_QA, FOX_W), part(C_KA, FOX_W), part(C_VA, FOX_W), part(C_F, FOX_HEADS),
                            part(C_ZA, FOX_W), part(C_QB, SWA_W), part(C_KB, SWA_KV_W), part(C_VB, SWA_KV_W),
                            part(C_ZB, SWA_W)], axis=1)


def _rope_tables(positions):
    inv_freq = ROPE_THETA ** (-jnp.arange(HALF, dtype=F32) / HALF)
    ang = positions.astype(F32)[:, None] * inv_freq
    cos, sin = jnp.cos(ang), jnp.sin(ang)
    return jnp.concatenate([cos, cos, cos, cos], axis=1), jnp.concatenate([-sin, sin, -sin, sin], axis=1)


def _tiles(s):
    if s >= 4096:
        return dict(tm=512, blk=512, bq=2048, bk=2048, bk_bwd=2048, chunk=256, tq=256, tm_out=512, tk=1024, tn=1152)
    return dict(tm=128, blk=128, bq=256, bk=256, bk_bwd=256, chunk=128, tq=128, tm_out=128, tk=128, tn=1152)


def kernel(x, c, positions, w_ada, b_ada, g_pre, w_in, b_fgate, sinks, w_out, g_post, loss_target, m_w_ada, m_b_ada, m_g_pre, m_w_in, m_b_fgate, m_sinks, m_w_out, m_g_post, v_w_ada, v_b_ada, v_g_pre, v_w_in, v_b_fgate, v_sinks, v_w_out, v_g_post):
    s = x.shape[1]
    t = _tiles(s)
    nc = s // LANES
    rows = FOX_HEADS * nc
    me = 4 * lax.axis_index("x") + 2 * lax.axis_index("y") + lax.axis_index("c")
    chip = 2 * lax.axis_index("x") + lax.axis_index("y")
    core = lax.axis_index("c")
    x2, tgt = x[0], loss_target[0]

    a_all, mod_all = _ada_exchange(c, w_ada[0])
    mod_rows = lax.dynamic_index_in_dim(mod_all, me, axis=1, keepdims=False)
    mod = mod_rows.reshape(N_CHIPS, 2, W_ADA_SHARD)[:, 0, :].reshape(1, 3 * D_MODEL) + b_ada
    shift, scale1p, gate = mod[:, :D_MODEL], 1.0 + mod[:, D_MODEL:2 * D_MODEL], mod[:, 2 * D_MODEL:]

    w_in_pad = jnp.pad(w_in[0].astype(BF), ((0, 0), (0, W_IN_SHARD_PAD - W_IN_SHARD)))
    w_pack = jnp.concatenate([w_in_pad, w_out[0].astype(BF).reshape(D_MODEL, W_OUT_SHARD)], axis=1)
    w_all = _allgather_chips(w_pack.reshape(2, D_MODEL // 2, -1), "gather_weights").reshape(N_CHIPS, D_MODEL, -1)
    w_cols = jnp.concatenate([w_all[k, :, :W_IN_SHARD] for k in range(N_CHIPS)], axis=1)
    w_al = _align_w_in(w_cols)
    w_al_t = w_al.T
    w_out_all = w_all[:, :, W_IN_SHARD_PAD:].reshape(D_MODEL, D_MODEL)
    w_out_t = w_out_all.T

    cos_t, sin_t = _rope_tables(positions[0])

    f_pad = _forget_logits(x2, g_pre, scale1p, shift, w_al[:, C_F:], t["tk"])
    f_rows = f_pad[:, :FOX_HEADS].T.reshape(rows, LANES)
    bias_rows = jnp.repeat(b_fgate[0], nc)[:, None]
    cum = _log_forget_cumsum(f_rows, bias_rows, nc).reshape(FOX_HEADS, s)
    h, qat, ka, kat, va, vat, za, zb, qb, kb, vb, qbt, kbt, vbt, m_own = _in_proj(
        x2, g_pre, scale1p, shift, w_al[:, C_VA:C_F], w_al_t[:C_ZA], cum, cos_t, sin_t, t["tm"])
    m_own = m_own[:, None, :]
    fox_args = (qat, ka, vat, m_own, t["bq"], t["bk"], t["chunk"])
    oat, lse, bad, pt = _fox_fwd(*fox_args, running_max=False)
    overflowed = jnp.max(bad) > 0.0
    oat, lse = lax.cond(overflowed, lambda: _fox_fwd(*fox_args, running_max=True)[:2], lambda: (oat, lse))
    inv_l = jnp.where(overflowed, 1.0, jnp.exp2(m_own - lse))
    sinks_g = sinks.reshape(SWA_KV_HEADS, 1, SWA_GROUP)
    obt = _swa_fwd(qbt, kb, vbt, sinks_g, t["tq"])

    dout, doat, delta_a, dza, dob, dobt, dzb, gw_out, dgate, dg_post, loss_part = _out_proj(
        oat, za, obt, zb, x2, tgt, w_out_all, w_out_t, gate, g_post, inv_l, t["tm_out"])

    bwd_args = (qat, ka, kat, va, doat, lse, delta_a)
    bwd_tiles = (t["bq"], t["bk_bwd"], t["chunk"], t["blk"])
    dqt, dkt, dvt = lax.cond(overflowed, lambda: _fox_bwd(*bwd_args, None, *bwd_tiles),
                             lambda: _fox_bwd(*bwd_args, pt, *bwd_tiles))
    dcum = dqt[:, :, HEAD_DIM, :].reshape(FOX_HEADS, s) - dkt[:, HEAD_DIM, :]
    df_rows, db_heads = _log_forget_cumsum_bwd(dcum.reshape(rows, LANES), f_rows, bias_rows, nc)
    df_pad = jnp.pad(df_rows.reshape(FOX_HEADS, s).T, ((0, 0), (0, LANES - FOX_HEADS)))
    dqb, dkb, dvb, dsinks = _swa_bwd(qb, qbt, kb, kbt, vb, sinks_g, dob, dobt, t["tq"])

    dproj = _assemble_dproj(dqt, dkt, dvt, dza, dqb, dzb, dkb, dvb, df_pad, cos_t, sin_t, t["blk"])
    gw_in = _unalign_w_in(_in_proj_bwd_w(h, dproj, t["tk"], t["tn"]))

    gin = jnp.stack([jnp.pad(gw_in[:, k * W_IN_SHARD:(k + 1) * W_IN_SHARD], ((0, 0), (0, W_IN_SHARD_PAD - W_IN_SHARD)))
                     for k in range(N_CHIPS)])
    gout = gw_out.astype(BF).reshape(N_CHIPS, D_MODEL, W_OUT_SHARD)
    gbig = jnp.concatenate([gin, gout], axis=2)
    half = D_MODEL // 2
    gw = W_IN_SHARD_PAD + W_OUT_SHARD
    keep = lax.dynamic_slice_in_dim(gbig, core * half, half, axis=1)
    give = lax.dynamic_slice_in_dim(gbig, (1 - core) * half, half, axis=1)
    got = _swap_sibling(give.reshape(N_CHIPS * half, gw), "swap_grad_halves")
    pair = _add(keep.reshape(N_CHIPS * half, gw), got, "add_pair", BF).reshape(N_CHIPS, half, gw)
    grad_x, dshift, dscale, dg_pre, from_chips = _in_proj_bwd_x(
        dproj, w_al_t, x2, dout, g_pre, scale1p, t["tm_out"], pair)

    pad_lane = lambda vrow: jnp.pad(vrow, ((0, 0), (0, LANES - vrow.shape[1])))
    packed = jnp.concatenate([dshift, dscale, dgate, dg_pre, dg_post,
                              pad_lane(db_heads.reshape(1, FOX_HEADS)), pad_lane(dsinks.reshape(1, FOX_HEADS)),
                              pad_lane(loss_part)], axis=1)
    parts = _allgather_devices(packed, "gather_partials")
    tot = _sum_devices(parts)
    loss = tot[0, P_LOSS]
    g_b_ada = tot[:, P_DMOD:P_DMOD + 3 * D_MODEL]
    g_g_pre = tot[:, P_GPRE:P_GPRE + D_MODEL]
    g_g_post = tot[:, P_GPOST:P_GPOST + D_MODEL]
    g_b_fgate = tot[:, P_BF:P_BF + FOX_HEADS]
    g_sinks = tot[:, P_SINK:P_SINK + FOX_HEADS]
    dm_shard = lax.dynamic_slice_in_dim(parts[:, 0, :3 * D_MODEL], chip * W_ADA_SHARD, W_ADA_SHARD, axis=1)
    g_w_ada = _grad_w_ada(a_all.T, dm_shard)

    mine = _sum_chips(from_chips, "sum_chips")
    other = _swap_sibling(mine, "swap_grad_result")
    lo = jnp.where(core == 0, mine, other)
    hi = jnp.where(core == 0, other, mine)
    gfull = jnp.concatenate([lo, hi], axis=0)
    g_w_in = gfull[:, :W_IN_SHARD]
    g_w_out = gfull[:, W_IN_SHARD_PAD:].reshape(W_OUT_SHARD, D_MODEL)

    grads = dict(w_ada=g_w_ada, b_ada=g_b_ada, g_pre=g_g_pre, w_in=g_w_in, b_fgate=g_b_fgate, sinks=g_sinks,
                 w_out=g_w_out, g_post=g_g_post)
    weights = dict(w_ada=w_ada, b_ada=b_ada, g_pre=g_pre, w_in=w_in, b_fgate=b_fgate, sinks=sinks, w_out=w_out, g_post=g_post)
    moms = dict(w_ada=m_w_ada, b_ada=m_b_ada, g_pre=m_g_pre, w_in=m_w_in, b_fgate=m_b_fgate, sinks=m_sinks, w_out=m_w_out, g_post=m_g_post)
    vars_ = dict(w_ada=v_w_ada, b_ada=v_b_ada, g_pre=v_g_pre, w_in=v_w_in, b_fgate=v_b_fgate, sinks=v_sinks, w_out=v_w_out, g_post=v_g_post)
    names = ["w_ada", "b_ada", "g_pre", "w_in", "b_fgate", "sinks", "w_out", "g_post"]
    g_out, d_out, m_out, v_out = [], [], [], []
    for n in names:
        if n == "w_in":
            flat = lambda a: jnp.transpose(a, (2, 0, 1)).reshape(W_IN_SHARD * D_MODEL // LANES, LANES)
            unflat = lambda a: jnp.transpose(a.reshape(W_IN_SHARD, 1, D_MODEL), (1, 2, 0))
            outs = _adamw(flat(w_in), flat(grads[n][None]), flat(moms[n]), flat(vars_[n]), "adamw_" + n)
            go, d, nm, nv = (unflat(a) for a in outs)
        else:
            g2 = grads[n].reshape(weights[n].shape[-2:])
            go, d, nm, nv = _adamw(weights[n], g2, moms[n], vars_[n], "adamw_" + n)
        g_out.append(go)
        d_out.append(d)
        m_out.append(nm)
        v_out.append(nv)
    return (loss, grad_x.reshape(x.shape), *g_out, *d_out, *m_out, *v_out)
```

```python
import jax
import jax.numpy as jnp
from jax import lax
from jax.experimental import pallas as pl
from jax.experimental.pallas import tpu as pltpu

_INTERPRET = False

D_MODEL = 1024
HEAD_DIM = 64
HALF = HEAD_DIM // 2
AUG_DIM = 128
AUG_ROWS = 8
VT_ROWS = 80
LOG2E = 1.4426950408889634
LN2 = 0.6931471805599453
Q_SCALE = LOG2E * 64 ** -0.5
FOX_HEADS = 8
FOX_W = 512
SWA_W = 512
SWA_KV_HEADS = 2
SWA_GROUP = 4
SWA_KV_W = 128
WINDOW = 128
ROPE_THETA = 10000.0
RMS_EPS = 1e-6
IN_WIDTH = 3336
N_CHIPS = 4
N_DEV = 8
W_IN_SHARD = IN_WIDTH // N_CHIPS
W_IN_SHARD_PAD = 896
W_ADA_SHARD = 3 * D_MODEL // N_CHIPS
W_OUT_SHARD = D_MODEL // N_CHIPS
LANES = 128

_SRC = dict(qa=0, ka=512, va=1024, fa=1536, za=1544, qb=2056, kb=2568, vb=2696, zb=2824)
C_QA, C_KA, C_VA, C_ZA, C_QB, C_ZB, C_KB, C_VB, C_F = 0, 512, 1024, 1536, 2048, 2560, 3072, 3200, 3328
WP = 3456

ADAM_LR = 0.001
ADAM_B1 = 0.9
ADAM_B2 = 0.999
ADAM_EPS = 1e-08
ADAM_WD = 0.01
ADAM_STEP = 10
ADAMW_BLOCK_ELEMS = 300_000

VMEM_LIMIT = 56 * 1024 * 1024
NEG = -1e30
OVERFLOW_GUARD = 1e30
MESH = pl.DeviceIdType.MESH
BF = jnp.bfloat16
F32 = jnp.float32

P_DMOD, P_GPRE, P_GPOST, P_BF, P_SINK, P_LOSS, P_LEN = 0, 3072, 4096, 5120, 5248, 5376, 5504


def _call(body, **kw):
    return pl.pallas_call(body, interpret=_INTERPRET, **kw)


def _params(sem=None, **kw):
    return pltpu.CompilerParams(dimension_semantics=sem, vmem_limit_bytes=VMEM_LIMIT, **kw)


def _full(shape):
    zeros = (0,) * len(shape)
    return pl.BlockSpec(shape, lambda *_: zeros)


def _dot(a, b):
    return jnp.dot(a, b, preferred_element_type=F32)


def _dot_nt(a, b):
    return lax.dot_general(a, b, (((1,), (1,)), ((), ())), preferred_element_type=F32)


def _dot_tn(a, b):
    return lax.dot_general(a, b, (((0,), (0,)), ((), ())), preferred_element_type=F32)


def _sigmoid(z):
    return 1.0 / (1.0 + jnp.exp(-z))


def _rope_partner(t):
    w = t.shape[-1]
    lane = lax.broadcasted_iota(jnp.int32, t.shape, t.ndim - 1)
    return jnp.where((lane & (HEAD_DIM - 1)) < HALF, pltpu.roll(t, w - HALF, t.ndim - 1), pltpu.roll(t, HALF, t.ndim - 1))


def _allgather_devices(v, name):
    r, cdim = v.shape
    masks = [(dx, dy, dc) for dx in (0, 1) for dy in (0, 1) for dc in (0, 1)][1:]

    def body(v_ref, out_ref, send_sems, recv_sems):
        x, y, c = lax.axis_index("x"), lax.axis_index("y"), lax.axis_index("c")
        me = 4 * x + 2 * y + c
        out_ref[me] = v_ref[...]
        copies = []
        for k, (dx, dy, dc) in enumerate(masks):
            cp = pltpu.make_async_remote_copy(
                src_ref=v_ref, dst_ref=out_ref.at[me], send_sem=send_sems.at[k], recv_sem=recv_sems.at[k],
                device_id=(x ^ dx, y ^ dy, c ^ dc), device_id_type=MESH)
            cp.start()
            copies.append(cp)
        for k, (dx, dy, dc) in enumerate(masks):
            peer = 4 * (x ^ dx) + 2 * (y ^ dy) + (c ^ dc)
            pltpu.make_async_remote_copy(
                src_ref=v_ref, dst_ref=out_ref.at[peer], send_sem=send_sems.at[k], recv_sem=recv_sems.at[k],
                device_id=(x ^ dx, y ^ dy, c ^ dc), device_id_type=MESH).wait_recv()
        for cp in copies:
            cp.wait_send()

    return _call(
        body, name=name, out_shape=jax.ShapeDtypeStruct((N_DEV, r, cdim), v.dtype),
        in_specs=[pl.BlockSpec(memory_space=pltpu.VMEM)], out_specs=pl.BlockSpec(memory_space=pltpu.VMEM),
        scratch_shapes=[pltpu.SemaphoreType.DMA((7,)), pltpu.SemaphoreType.DMA((7,))],
        compiler_params=pltpu.CompilerParams(has_side_effects=True),
    )(v)


CHIP_MASKS = [(1, 0), (0, 1), (1, 1)]
CHIP_GATHER_SEMS = [pltpu.SemaphoreType.DMA((2 * len(CHIP_MASKS),)), pltpu.SemaphoreType.DMA((2 * len(CHIP_MASKS),)),
                    pltpu.SemaphoreType.DMA]


def _chip_gather(v_ref, out_ref, send_sems, recv_sems, local_sem):
    n = len(CHIP_MASKS)
    x, y, c = lax.axis_index("x"), lax.axis_index("y"), lax.axis_index("c")
    me = 2 * x + y
    mine = pltpu.make_async_copy(v_ref, out_ref.at[me], local_sem)

    def copy(k, chip, half, to):
        return pltpu.make_async_remote_copy(
            src_ref=v_ref.at[half] if k < n else out_ref.at[chip, half], dst_ref=out_ref.at[chip, half],
            send_sem=send_sems.at[k], recv_sem=recv_sems.at[k], device_id=to, device_id_type=MESH)

    def start():
        mine.start()
        for k, (dx, dy) in enumerate(CHIP_MASKS):
            copy(k, me, c, (x ^ dx, y ^ dy, c)).start()

    def finish():
        passed = []
        for k, (dx, dy) in enumerate(CHIP_MASKS):
            peer = 2 * (x ^ dx) + (y ^ dy)
            copy(k, peer, c, (x, y, c)).wait_recv()
            cp = copy(n + k, peer, c, (x, y, 1 - c))
            cp.start()
            passed.append(cp)
        for k, (dx, dy) in enumerate(CHIP_MASKS):
            copy(n + k, 2 * (x ^ dx) + (y ^ dy), 1 - c, (x, y, c)).wait_recv()
        for k, (dx, dy) in enumerate(CHIP_MASKS):
            copy(k, me, c, (x ^ dx, y ^ dy, c)).wait_send()
        for cp in passed:
            cp.wait_send()
        mine.wait()

    return start, finish


def _allgather_chips(v, name):
    _, r, cdim = v.shape
    n = len(CHIP_MASKS)

    def body(v_ref, out_ref, send_sems, recv_sems, local_sem):
        start, finish = _chip_gather(v_ref, out_ref, send_sems, recv_sems, local_sem)
        start()
        finish()

    return _call(
        body, name=name, out_shape=jax.ShapeDtypeStruct((N_CHIPS, 2, r, cdim), v.dtype),
        in_specs=[pl.BlockSpec(memory_space=pl.ANY)], out_specs=pl.BlockSpec(memory_space=pl.ANY),
        scratch_shapes=[pltpu.SemaphoreType.DMA((2 * n,)), pltpu.SemaphoreType.DMA((2 * n,)), pltpu.SemaphoreType.DMA],
        compiler_params=pltpu.CompilerParams(has_side_effects=True),
    )(v)


def _swap_sibling(v, name):
    def body(v_ref, out_ref, send_sem, recv_sem):
        x, y, c = lax.axis_index("x"), lax.axis_index("y"), lax.axis_index("c")
        cp = pltpu.make_async_remote_copy(
            src_ref=v_ref, dst_ref=out_ref, send_sem=send_sem, recv_sem=recv_sem,
            device_id=(x, y, 1 - c), device_id_type=MESH)
        cp.start()
        cp.wait()

    return _call(
        body, name=name, out_shape=jax.ShapeDtypeStruct(v.shape, v.dtype),
        in_specs=[pl.BlockSpec(memory_space=pl.ANY)], out_specs=pl.BlockSpec(memory_space=pl.ANY),
        scratch_shapes=[pltpu.SemaphoreType.DMA, pltpu.SemaphoreType.DMA],
        compiler_params=pltpu.CompilerParams(has_side_effects=True),
    )(v)


def _ada_exchange(c, w_ada_shard):
    masks = [(dx, dy, dc) for dx in (0, 1) for dy in (0, 1) for dc in (0, 1)][1:]
    n = len(masks)

    def body(c_ref, w_ref, a_ref, mod_ref, c_all, send_sems, recv_sems):
        x, y, cc = lax.axis_index("x"), lax.axis_index("y"), lax.axis_index("c")
        me = 4 * x + 2 * y + cc

        def gather(src_ref, dst_ref, first):
            sends = []
            for k, (dx, dy, dc) in enumerate(masks):
                cp = pltpu.make_async_remote_copy(
                    src_ref=src_ref, dst_ref=dst_ref.at[me], send_sem=send_sems.at[first + k],
                    recv_sem=recv_sems.at[first + k], device_id=(x ^ dx, y ^ dy, cc ^ dc), device_id_type=MESH)
                cp.start()
                sends.append(cp)
            for k, (dx, dy, dc) in enumerate(masks):
                peer = 4 * (x ^ dx) + 2 * (y ^ dy) + (cc ^ dc)
                pltpu.make_async_remote_copy(
                    src_ref=src_ref, dst_ref=dst_ref.at[peer], send_sem=send_sems.at[first + k],
                    recv_sem=recv_sems.at[first + k], device_id=(x ^ dx, y ^ dy, cc ^ dc), device_id_type=MESH).wait_recv()
            return sends

        c_all[me] = c_ref[...]
        sends = gather(c_ref, c_all, 0)
        w_bf = w_ref[...].astype(BF)
        for d in range(N_DEV):
            cv = c_all[d]
            a = cv * _sigmoid(cv)
            a_ref[d:d + 1, :] = a
            mod_ref[me, d:d + 1, :] = _dot(a.astype(BF), w_bf)
        sends += gather(mod_ref.at[me], mod_ref, n)
        for cp in sends:
            cp.wait_send()

    vmem = pl.BlockSpec(memory_space=pltpu.VMEM)
    return _call(
        body, name="ada_exchange",
        out_shape=(jax.ShapeDtypeStruct((N_DEV, D_MODEL), F32), jax.ShapeDtypeStruct((N_DEV, N_DEV, W_ADA_SHARD), F32)),
        in_specs=[vmem, vmem], out_specs=(vmem, vmem),
        scratch_shapes=[pltpu.VMEM((N_DEV, 1, D_MODEL), F32), pltpu.SemaphoreType.DMA((2 * n,)),
                        pltpu.SemaphoreType.DMA((2 * n,))],
        compiler_params=_params(has_side_effects=True),
    )(c, w_ada_shard)


def _grad_w_ada(a_t, dm_shard):
    def body(a_ref, dm_ref, out_ref):
        acc = jnp.zeros((D_MODEL, W_ADA_SHARD), F32)
        for b in range(N_DEV):
            acc = acc + a_ref[:, b:b + 1] * dm_ref[b:b + 1, :]
        out_ref[...] = acc

    return _call(body, name="grad_w_ada", out_shape=jax.ShapeDtypeStruct((D_MODEL, W_ADA_SHARD), F32),
                 compiler_params=_params())(a_t, dm_shard)


def _sum_devices(parts):
    n = parts.shape[-1]

    def body(p_ref, out_ref):
        acc = p_ref[0]
        for b in range(1, N_DEV):
            acc = acc + p_ref[b]
        out_ref[...] = acc

    return _call(body, name="sum_devices", out_shape=jax.ShapeDtypeStruct((1, n), F32), compiler_params=_params())(parts)


def _add(a, b, name, out_dtype):
    r, cdim = a.shape
    tr = min(r, 256)

    def body(a_ref, b_ref, o_ref):
        o_ref[...] = (a_ref[...].astype(F32) + b_ref[...].astype(F32)).astype(out_dtype)

    spec = pl.BlockSpec((tr, cdim), lambda i: (i, 0))
    return _call(body, name=name, out_shape=jax.ShapeDtypeStruct(a.shape, out_dtype), grid=(r // tr,),
                 in_specs=[spec, spec], out_specs=spec, compiler_params=_params(("parallel",)))(a, b)


def _sum_chips(parts, name):
    _, r, cdim = parts.shape
    tr = min(r, 128)

    def body(p_ref, o_ref):
        o_ref[...] = ((p_ref[0].astype(F32) + p_ref[1].astype(F32)) + p_ref[2].astype(F32)) + p_ref[3].astype(F32)

    return _call(body, name=name, out_shape=jax.ShapeDtypeStruct((r, cdim), F32), grid=(r // tr,),
                 in_specs=[pl.BlockSpec((N_CHIPS, tr, cdim), lambda i: (0, i, 0))],
                 out_specs=pl.BlockSpec((tr, cdim), lambda i: (i, 0)), compiler_params=_params(("parallel",)))(parts)


def _adamw(w, g, m, v, name):
    r, cdim = w.shape[-2:]
    lead = w.ndim - 2
    tr = r if r <= 256 else max(t for t in range(8, ADAMW_BLOCK_ELEMS // cdim + 1, 8) if r % t == 0)
    c1 = 1.0 / (1.0 - ADAM_B1 ** ADAM_STEP)
    c2 = 1.0 / (1.0 - ADAM_B2 ** ADAM_STEP)

    def body(w_ref, g_ref, m_ref, v_ref, go_ref, d_ref, nm_ref, nv_ref):
        gv = g_ref[...].reshape(go_ref.shape)
        nm = ADAM_B1 * m_ref[...] + (1.0 - ADAM_B1) * gv
        nv = ADAM_B2 * v_ref[...] + (1.0 - ADAM_B2) * (gv * gv)
        m_hat = nm * c1
        v_hat = nv * c2
        go_ref[...] = gv
        d_ref[...] = -ADAM_LR * (m_hat / (jnp.sqrt(v_hat) + ADAM_EPS) + ADAM_WD * w_ref[...])
        nm_ref[...] = nm
        nv_ref[...] = nv

    spec = pl.BlockSpec((1,) * lead + (tr, cdim), lambda i: (0,) * lead + (i, 0))
    shp = jax.ShapeDtypeStruct(w.shape, F32)
    return _call(body, name=name, out_shape=(shp,) * 4, grid=(r // tr,),
                 in_specs=[spec, pl.BlockSpec((tr, cdim), lambda i: (i, 0)), spec, spec],
                 out_specs=(spec,) * 4, compiler_params=_params(("parallel",)))(w, g, m, v)


def _head_of_row(r, nc):
    assert nc & (nc - 1) == 0
    return lax.shift_right_logical(r, nc.bit_length() - 1)


def _chunk_mats(rows, nc, reverse):
    ri = lax.broadcasted_iota(jnp.int32, (rows, rows), 0)
    ci = lax.broadcasted_iota(jnp.int32, (rows, rows), 1)
    same = _head_of_row(ri, nc) == _head_of_row(ci, nc)
    between = jnp.where(same & ((ci > ri) if reverse else (ci < ri)), 1.0, 0.0).astype(F32)
    li = lax.broadcasted_iota(jnp.int32, (LANES, LANES), 0)
    lj = lax.broadcasted_iota(jnp.int32, (LANES, LANES), 1)
    within = jnp.where((li >= lj) if reverse else (li <= lj), 1.0, 0.0).astype(F32)
    return between, within


def _dot_hi(a, b):
    return jnp.dot(a, b, preferred_element_type=F32, precision=lax.Precision.HIGHEST)


def _scan_rows(t, nc, reverse):
    between, within = _chunk_mats(t.shape[0], nc, reverse)
    inner = _dot_hi(t, within)
    tot = jnp.sum(t, axis=1, keepdims=True)
    return inner + _dot_hi(between, jnp.broadcast_to(tot, t.shape))


def _log_forget_cumsum(f_rows, bias_rows, nc):
    def body(f_ref, b_ref, cum_ref):
        z = f_ref[...] + b_ref[...]
        lf = jnp.minimum(z, 0.0) - jnp.log(1.0 + jnp.exp(-jnp.abs(z)))
        cum_ref[...] = _scan_rows(lf, nc, False)

    return _call(body, name="forget_cumsum", out_shape=jax.ShapeDtypeStruct(f_rows.shape, F32),
                 compiler_params=_params())(f_rows, bias_rows)


def _log_forget_cumsum_bwd(dcum_rows, f_rows, bias_rows, nc):
    rows = f_rows.shape[0]

    def body(d_ref, f_ref, b_ref, df_ref, db_ref):
        dlf = _scan_rows(d_ref[...], nc, True)
        z = f_ref[...] + b_ref[...]
        df = dlf * _sigmoid(-z)
        df_ref[...] = df
        hi = lax.broadcasted_iota(jnp.int32, (FOX_HEADS, rows), 0)
        ri = lax.broadcasted_iota(jnp.int32, (FOX_HEADS, rows), 1)
        sel = jnp.where(_head_of_row(ri, nc) == hi, 1.0, 0.0).astype(F32)
        db_ref[...] = jnp.sum(_dot_hi(sel, df), axis=1, keepdims=True)

    return _call(body, name="forget_cumsum_bwd",
                 out_shape=(jax.ShapeDtypeStruct(f_rows.shape, F32), jax.ShapeDtypeStruct((FOX_HEADS, 1), F32)),
                 compiler_params=_params())(dcum_rows, f_rows, bias_rows)


def _rms_hat(xv):
    rstd = lax.rsqrt(jnp.mean(xv * xv, axis=-1, keepdims=True) + RMS_EPS)
    return xv * rstd, rstd


def _modulated(x_ref, g_ref, sc_ref, sh_ref):
    xhat, _ = _rms_hat(x_ref[...])
    return ((xhat * g_ref[...]) * sc_ref[...] + sh_ref[...]).astype(BF)


def _forget_logits(x, g_pre, scale1p, shift, w_f, tm):
    s = x.shape[0]

    def body(x_ref, g_ref, sc_ref, sh_ref, w_ref, f_ref):
        f_ref[...] = _dot(_modulated(x_ref, g_ref, sc_ref, sh_ref), w_ref[...])

    vec = _full((1, D_MODEL))
    return _call(
        body, name="forget_logits", out_shape=jax.ShapeDtypeStruct((s, LANES), F32), grid=(s // tm,),
        in_specs=[pl.BlockSpec((tm, D_MODEL), lambda i: (i, 0)), vec, vec, vec, _full((D_MODEL, LANES))],
        out_specs=pl.BlockSpec((tm, LANES), lambda i: (i, 0)), compiler_params=_params(("parallel",)),
    )(x, g_pre, scale1p, shift, w_f)


def _split3(v):
    hi = v.astype(BF).astype(F32)
    mid = (v - hi).astype(BF).astype(F32)
    lo = ((v - hi) - mid).astype(BF).astype(F32)
    return hi, mid, lo


def _in_proj(x, g_pre, scale1p, shift, w_rows, w_t_fox, cum, cos_t, sin_t, w_out_halves, tm):
    s = x.shape[0]
    r_va, r_za, r_qb, r_zb, r_kb, r_vb = 0, 512, 1024, 1536, 2048, 2176

    def body(x_ref, g_ref, sc_ref, sh_ref, w_ref, wt_ref, cum_ref, cos_ref, sin_ref, wo_ref,
             h_ref, qat_ref, ka_ref, kat_ref, v_ref, vt_ref, za_ref, zb_ref, qb_ref, kb_ref, vb_ref,
             qbt_ref, kbt_ref, vbt_ref, mo_ref, wo_all_ref, send_sems, recv_sems, local_sem):
        start_gather, finish_gather = _chip_gather(wo_ref, wo_all_ref, send_sems, recv_sems, local_sem)

        @pl.when(pl.program_id(0) == 0)
        def _():
            start_gather()

        @pl.when(pl.program_id(0) == s // tm - 1)
        def _():
            finish_gather()

        hb = _modulated(x_ref, g_ref, sc_ref, sh_ref)
        h_ref[...] = hb

        def sec(c0, width):
            return _dot(hb, w_ref[:, c0:c0 + width])

        def sec_t(r0):
            return _dot_nt(wt_ref[r0:r0 + FOX_W, :], hb)

        q_t = sec_t(0) * Q_SCALE
        k_t = sec_t(FOX_W)
        v_t = sec_t(2 * FOX_W)
        va = sec(r_va, FOX_W)
        zeros = jnp.zeros((AUG_DIM - HEAD_DIM - AUG_ROWS, tm), F32)
        ri = lax.broadcasted_iota(jnp.int32, (AUG_ROWS, tm), 0)
        const = jnp.where(ri == AUG_ROWS - 1, 0.0, 1.0)
        ri_v = lax.broadcasted_iota(jnp.int32, (VT_ROWS - HEAD_DIM, tm), 0)
        v_feat = jnp.where(ri_v == 0, 1.0, 0.0).astype(BF)
        for hd in range(FOX_HEADS):
            rows = slice(hd * HEAD_DIM, (hd + 1) * HEAD_DIM)
            cum2 = cum_ref[hd:hd + 1, :] * LOG2E
            hi, mid, lo = (jnp.broadcast_to(part, (AUG_ROWS, tm)) for part in _split3(cum2))
            q_feat = jnp.where(ri == 1, hi, jnp.where(ri == 2, mid, jnp.where(ri == 3, lo, const)))
            k_feat = jnp.where(ri == 4, -hi, jnp.where(ri == 5, -mid, jnp.where(ri == 6, -lo, const)))
            q_aug = jnp.concatenate([q_t[rows], q_feat, zeros], axis=0)
            k_aug = jnp.concatenate([k_t[rows], k_feat, zeros], axis=0)
            mo_ref[hd:hd + 1, :] = jnp.sum(q_t[rows] * k_t[rows], axis=0, keepdims=True) + 1.0
            qat_ref[hd] = q_aug.astype(BF)
            kat_ref[hd] = k_aug.astype(BF)
            ka_ref[hd] = k_aug.T.astype(BF)
            vt_ref[hd] = jnp.concatenate([v_t[rows].astype(BF), v_feat], axis=0)
            v_ref[hd] = va[:, rows].astype(BF)
        za_ref[...] = sec(r_za, FOX_W)
        zb_ref[...] = sec(r_zb, SWA_W)
        cos2, sin2 = cos_ref[...], sin_ref[...]
        cos8 = jnp.concatenate([cos2] * 4, axis=1)
        sin8 = jnp.concatenate([sin2] * 4, axis=1)
        qb = sec(r_qb, SWA_W)
        qb = (qb * cos8 + _rope_partner(qb) * sin8) * (HEAD_DIM ** -0.5)
        qb_ref[...] = qb.astype(BF)
        for a in range(SWA_W // LANES):
            qbt_ref[a * LANES:(a + 1) * LANES, :] = qb[:, a * LANES:(a + 1) * LANES].T.astype(BF)
        kb = sec(r_kb, SWA_KV_W)
        kb = kb * cos2 + _rope_partner(kb) * sin2
        vb = sec(r_vb, SWA_KV_W)
        kb_t, vb_t = kb.T, vb.T
        for hd in range(SWA_KV_HEADS):
            sl = slice(hd * HEAD_DIM, (hd + 1) * HEAD_DIM)
            kb_ref[hd] = kb[:, sl].astype(BF)
            vb_ref[hd] = vb[:, sl].astype(BF)
            kbt_ref[hd] = kb_t[sl].astype(BF)
            vbt_ref[hd] = jnp.concatenate([vb_t[sl].astype(BF), v_feat], axis=0)

    row = lambda w: pl.BlockSpec((tm, w), lambda i: (i, 0))
    heads = lambda n, w=HEAD_DIM: pl.BlockSpec((n, tm, w), lambda i: (0, i, 0))
    heads_t = lambda w: pl.BlockSpec((FOX_HEADS, w, tm), lambda i: (0, 0, i))
    vec = _full((1, D_MODEL))
    hs = lambda a, b: jax.ShapeDtypeStruct((FOX_HEADS, a, b), BF)
    out_shape = (
        jax.ShapeDtypeStruct((s, D_MODEL), BF),
        hs(AUG_DIM, s), hs(s, AUG_DIM), hs(AUG_DIM, s), hs(s, HEAD_DIM), hs(VT_ROWS, s),
        jax.ShapeDtypeStruct((s, FOX_W), F32), jax.ShapeDtypeStruct((s, SWA_W), F32),
        jax.ShapeDtypeStruct((s, SWA_W), BF),
        jax.ShapeDtypeStruct((SWA_KV_HEADS, s, HEAD_DIM), BF), jax.ShapeDtypeStruct((SWA_KV_HEADS, s, HEAD_DIM), BF),
        jax.ShapeDtypeStruct((SWA_W, s), BF),
        jax.ShapeDtypeStruct((SWA_KV_HEADS, HEAD_DIM, s), BF), jax.ShapeDtypeStruct((SWA_KV_HEADS, VT_ROWS, s), BF),
        jax.ShapeDtypeStruct((FOX_HEADS, s), F32),
        jax.ShapeDtypeStruct((N_CHIPS,) + w_out_halves.shape, w_out_halves.dtype),
    )
    kv_t = lambda w: pl.BlockSpec((SWA_KV_HEADS, w, tm), lambda i: (0, 0, i))
    hbm = pl.BlockSpec(memory_space=pl.ANY)
    return _call(
        body, name="in_proj", out_shape=out_shape, grid=(s // tm,),
        in_specs=[row(D_MODEL), vec, vec, vec, _full(w_rows.shape), _full(w_t_fox.shape),
                  pl.BlockSpec((FOX_HEADS, tm), lambda i: (0, i)), row(LANES), row(LANES), hbm],
        out_specs=(row(D_MODEL), heads_t(AUG_DIM), heads(FOX_HEADS, AUG_DIM), heads_t(AUG_DIM), heads(FOX_HEADS),
                   heads_t(VT_ROWS), row(FOX_W), row(SWA_W), row(SWA_W), heads(SWA_KV_HEADS), heads(SWA_KV_HEADS),
                   pl.BlockSpec((SWA_W, tm), lambda i: (0, i)), kv_t(HEAD_DIM), kv_t(VT_ROWS),
                   pl.BlockSpec((FOX_HEADS, tm), lambda i: (0, i)), hbm),
        scratch_shapes=list(CHIP_GATHER_SEMS),
        compiler_params=_params(("arbitrary",), has_side_effects=True),
    )(x, g_pre, scale1p, shift, w_rows, w_t_fox, cum, cos_t, sin_t, w_out_halves)


def _diag_chunks(d, bq, bk, chunk):
    out = []
    for c0 in range(0, bq, chunk):
        if d is None or d * bk + bk - 1 <= c0:
            out.append((c0, None, bk))
        elif d * bk <= c0 + chunk - 1:
            n_keys = min(bk, c0 + chunk - d * bk)
            kpos = d * bk + lax.broadcasted_iota(jnp.int32, (n_keys, chunk), 0)
            qpos = c0 + lax.broadcasted_iota(jnp.int32, (n_keys, chunk), 1)
            out.append((c0, kpos <= qpos, n_keys))
    return out


def _fox_fwd(qat, ka, vt, m_own, bq, bk, chunk, running_max):
    nh, _, s = qat.shape
    r = bq // bk

    pairs = [(i, j) for i in range(s // bq) for j in range(i * r + r)]

    def body(i_tab, j_tab, ka_ref, qat_ref, vt_ref, mo_ref, o_ref, lse_ref, bad_ref, *rest):
        pt_ref, m_scr, acc_scr = (None,) * running_max + rest
        i, j = i_tab[pl.program_id(1)], j_tab[pl.program_id(1)]

        @pl.when(j == 0)
        def _():
            m_scr[...] = jnp.full(m_scr.shape, NEG, F32) if running_max else mo_ref[0]
            acc_scr[...] = jnp.zeros(acc_scr.shape, F32)

        def careful(d):
            kv, vtv = ka_ref[0], vt_ref[0]

            def one_chunk(n, carry):
                c0 = pl.multiple_of(n * chunk, chunk)
                cs = pl.ds(c0, chunk)
                sc = _dot(kv, qat_ref[0, :, cs])
                if d is not None:
                    kpos = d * bk + lax.broadcasted_iota(jnp.int32, (bk, chunk), 0)
                    qpos = c0 + lax.broadcasted_iota(jnp.int32, (bk, chunk), 1)
                    sc = jnp.where(kpos <= qpos, sc, NEG)
                m_prev = m_scr[:, cs]
                m_new = jnp.maximum(m_prev, jnp.max(sc, axis=0, keepdims=True))
                p = jnp.exp2(sc - m_new).astype(BF)
                acc_scr[:, cs] = jnp.exp2(m_prev - m_new) * acc_scr[:, cs] + _dot(vtv, p)
                m_scr[:, cs] = m_new
                return carry

            lax.fori_loop(0, bq // chunk, one_chunk, 0)

        def fast(d):
            todo = _diag_chunks(d, bq, bk, chunk)
            scores = lambda t: _dot(ka_ref[0, :t[2], :], qat_ref[0, :, t[0]:t[0] + chunk])
            sc_next = scores(todo[0])
            for n, (c0, mask, n_keys) in enumerate(todo):
                cs = slice(c0, c0 + chunk)
                sc = sc_next
                if n + 1 < len(todo):
                    sc_next = scores(todo[n + 1])
                if mask is not None:
                    sc = jnp.where(mask, sc, NEG)
                p = jnp.exp2(sc - m_scr[:, cs]).astype(BF)
                pt_ref[0, :n_keys, cs] = p
                acc_scr[:, cs] += _dot(vt_ref[0, :, :n_keys], p)

        step = careful if running_max else fast

        @pl.when(j < i * r)
        def _():
            step(None)

        for d in range(r):
            @pl.when(j == i * r + d)
            def _(d=d):
                step(d)

        @pl.when(j == i * r + r - 1)
        def _():
            l = acc_scr[HEAD_DIM:HEAD_DIM + 1, :]
            o_ref[0] = acc_scr[:HEAD_DIM, :] / l
            lse_ref[0] = m_scr[...] + jnp.log2(l)
            bad_ref[0] = jnp.where(l < OVERFLOW_GUARD, 0.0, 1.0)

    qmap_t = lambda h, t, it, jt: (h, 0, it[t])
    qrow = pl.BlockSpec((1, 1, bq), qmap_t)
    row_shape = jax.ShapeDtypeStruct((nh, 1, s), F32)
    out_shape = (jax.ShapeDtypeStruct((nh, HEAD_DIM, s), F32), row_shape, row_shape)
    out_specs = (pl.BlockSpec((1, HEAD_DIM, bq), qmap_t), qrow, qrow)
    if not running_max:
        out_shape += (jax.ShapeDtypeStruct((nh, s, s), BF),)
        out_specs += (pl.BlockSpec((1, bk, bq), lambda h, t, it, jt: (h, jt[t], it[t])),)
    grid_spec = pltpu.PrefetchScalarGridSpec(
        num_scalar_prefetch=2, grid=(nh, len(pairs)),
        in_specs=[pl.BlockSpec((1, bk, AUG_DIM), lambda h, t, it, jt: (h, jt[t], 0)), pl.BlockSpec((1, AUG_DIM, bq), qmap_t),
                  pl.BlockSpec((1, VT_ROWS, bk), lambda h, t, it, jt: (h, 0, jt[t])), qrow],
        out_specs=out_specs,
        scratch_shapes=[pltpu.VMEM((1, bq), F32), pltpu.VMEM((VT_ROWS, bq), F32)])
    return _call(
        body, name="fox_fwd_running_max" if running_max else "fox_fwd", out_shape=out_shape, grid_spec=grid_spec,
        compiler_params=_params(("parallel", "arbitrary")),
    )(jnp.asarray([p[0] for p in pairs], jnp.int32), jnp.asarray([p[1] for p in pairs], jnp.int32), ka, qat, vt, m_own)


def _fox_bwd(qat, ka, kat, v, dot_, lse, delta, pt, bq, bk, chunk, dq_blk):
    nh, _, s = qat.shape
    r = bq // bk
    nq = s // bq
    stored = pt is not None

    pairs = [(j, i) for j in range(s // bk) for i in range(j // r, nq)]

    def body(j_tab, i_tab, a_ref, b_ref, kat_ref, v_ref, qat_ref, do_ref, dl_ref, dq_ref, dk_ref, dv_ref, dk_scr, dv_scr):
        ka_ref, lse_ref, pt_ref = (None, None, a_ref) if stored else (a_ref, b_ref, None)
        j, i = j_tab[pl.program_id(1)], i_tab[pl.program_id(1)]

        @pl.when(pl.program_id(1) == 0)
        def _():
            dq_ref[...] = jnp.zeros(dq_ref.shape, F32)

        @pl.when(i * r <= j)
        def _():
            dk_scr[...] = jnp.zeros(dk_scr.shape, F32)
            dv_scr[...] = jnp.zeros(dv_scr.shape, F32)

        def step(d):
            todo = _diag_chunks(d, bq, bk, chunk)

            def products(t):
                cs = slice(t[0], t[0] + chunk)
                return (None if stored else _dot(ka_ref[0, :t[2], :], qat_ref[0, :, cs]),
                        _dot(v_ref[0, :t[2], :], do_ref[0, :, cs]))

            nxt = products(todo[0])
            for n, (c0, mask, n_keys) in enumerate(todo):
                cs = slice(c0, c0 + chunk)
                sc, dp = nxt
                if n + 1 < len(todo):
                    nxt = products(todo[n + 1])
                if stored:
                    p_bf = pt_ref[0, :n_keys, cs]
                    p = p_bf.astype(F32)
                else:
                    p = jnp.exp2(sc - lse_ref[0, :, cs])
                    if mask is not None:
                        p = jnp.where(mask, p, 0.0)
                    p_bf = p.astype(BF)
                ds = (p * (dp - dl_ref[0, :, cs])).astype(BF)
                dv_scr[:, :n_keys] += _dot_nt(do_ref[0, :, cs], p_bf)
                dk_scr[:, :n_keys] += _dot_nt(qat_ref[0, :VT_ROWS, cs], ds)
                c1 = c0 % dq_blk
                dq_ref[0, i * (bq // dq_blk) + c0 // dq_blk, :, c1:c1 + chunk] += _dot(kat_ref[0, :VT_ROWS, :n_keys], ds)

        @pl.when(i * r > j)
        def _():
            step(None)

        for d in range(r):
            @pl.when(j == i * r + d)
            def _(d=d):
                step(d)

        @pl.when(i == nq - 1)
        def _():
            dk_ref[0] = dk_scr[...]
            dv_ref[0] = dv_scr[...]

    qmap = lambda h, t, jt, it: (h, 0, it[t])
    kmap = lambda h, t, jt, it: (h, jt[t], 0)
    kmap_t = lambda h, t, jt, it: (h, 0, jt[t])
    if stored:
        first = [(pt, pl.BlockSpec((1, bk, bq), lambda h, t, jt, it: (h, jt[t], it[t]))),
                 (delta, pl.BlockSpec((1, 1, bq), qmap))]
    else:
        first = [(ka, pl.BlockSpec((1, bk, AUG_DIM), kmap)), (lse, pl.BlockSpec((1, 1, bq), qmap))]
    grid_spec = pltpu.PrefetchScalarGridSpec(
        num_scalar_prefetch=2, grid=(nh, len(pairs)),
        in_specs=[first[0][1], first[1][1], pl.BlockSpec((1, AUG_DIM, bk), kmap_t), pl.BlockSpec((1, bk, HEAD_DIM), kmap),
                  pl.BlockSpec((1, AUG_DIM, bq), qmap), pl.BlockSpec((1, HEAD_DIM, bq), qmap),
                  pl.BlockSpec((1, 1, bq), qmap)],
        out_specs=(pl.BlockSpec((1, s // dq_blk, VT_ROWS, dq_blk), lambda h, t, jt, it: (h, 0, 0, 0)),
                   pl.BlockSpec((1, VT_ROWS, bk), kmap_t), pl.BlockSpec((1, HEAD_DIM, bk), kmap_t)),
        scratch_shapes=[pltpu.VMEM((VT_ROWS, bk), F32), pltpu.VMEM((HEAD_DIM, bk), F32)])
    return _call(
        body, name="fox_bwd" if stored else "fox_bwd_recompute",
        out_shape=(jax.ShapeDtypeStruct((nh, s // dq_blk, VT_ROWS, dq_blk), F32),
                   jax.ShapeDtypeStruct((nh, VT_ROWS, s), F32), jax.ShapeDtypeStruct((nh, HEAD_DIM, s), F32)),
        grid_spec=grid_spec, compiler_params=_params(("parallel", "arbitrary")),
    )(jnp.asarray([p[0] for p in pairs], jnp.int32), jnp.asarray([p[1] for p in pairs], jnp.int32),
      first[0][0], first[1][0], kat, v, qat, dot_, delta)


def _swa_mask(i, tq):
    kpos = i * tq - WINDOW + lax.broadcasted_iota(jnp.int32, (tq + WINDOW, tq), 0)
    qpos = i * tq + lax.broadcasted_iota(jnp.int32, (tq + WINDOW, tq), 1)
    rel = qpos - kpos
    return (rel >= 0) & (rel < WINDOW) & (kpos >= 0)


def _swa_rows(ref, g, i, tq):
    before = pl.multiple_of(jnp.maximum(i * tq - WINDOW, 0), WINDOW)
    return jnp.concatenate([ref[g, pl.ds(before, WINDOW), :], ref[g, pl.ds(pl.multiple_of(i * tq, tq), tq), :]], axis=0)


def _swa_before(n_rows, tq):
    return pl.BlockSpec((SWA_KV_HEADS, n_rows, WINDOW), lambda i: (0, 0, jnp.maximum(i * (tq // WINDOW) - 1, 0)))


def _swa_probs_t(sc, mask, sink):
    sc = jnp.where(mask, sc, NEG)
    m = jnp.maximum(jnp.max(sc, axis=0, keepdims=True), sink)
    p = jnp.exp(sc - m)
    e_sink = jnp.exp(sink - m)
    inv_l = 1.0 / (jnp.sum(p, axis=0, keepdims=True) + e_sink)
    return p * inv_l, e_sink * inv_l


def _swa_fwd(qbt, kb, vbt, sinks, tq):
    s = qbt.shape[1]
    n_heads = SWA_KV_HEADS * SWA_GROUP

    def body(q_ref, k_ref, vb_ref, vc_ref, s_ref, o_ref):
        i = pl.program_id(0)
        mask = _swa_mask(i, tq)
        kw = [_swa_rows(k_ref, g, i, tq) for g in range(SWA_KV_HEADS)]
        vtw = [jnp.concatenate([vb_ref[g], vc_ref[g]], axis=1) for g in range(SWA_KV_HEADS)]
        scores = lambda hd: _dot(kw[hd // SWA_GROUP], q_ref[hd * HEAD_DIM:(hd + 1) * HEAD_DIM, :])
        sc_next = scores(0)
        for hd in range(n_heads):
            g, hh = divmod(hd, SWA_GROUP)
            rows = slice(hd * HEAD_DIM, (hd + 1) * HEAD_DIM)
            sink = s_ref[g][:, hh:hh + 1]
            sc = jnp.where(mask, sc_next, NEG)
            if hd + 1 < n_heads:
                sc_next = scores(hd + 1)
            m = jnp.maximum(jnp.max(sc, axis=0, keepdims=True), sink)
            acc = _dot(vtw[g], jnp.exp(sc - m).astype(BF))
            o_ref[rows, :] = acc[:HEAD_DIM] / (acc[HEAD_DIM:HEAD_DIM + 1] + jnp.exp(sink - m))

    kvspec = _full((SWA_KV_HEADS, s, HEAD_DIM))
    qspec = pl.BlockSpec((SWA_W, tq), lambda i: (0, i))
    return _call(
        body, name="swa_fwd", out_shape=jax.ShapeDtypeStruct((SWA_W, s), F32), grid=(s // tq,),
        in_specs=[qspec, kvspec, _swa_before(VT_ROWS, tq), pl.BlockSpec((SWA_KV_HEADS, VT_ROWS, tq), lambda i: (0, 0, i)),
                  _full((SWA_KV_HEADS, 1, SWA_GROUP))],
        out_specs=qspec, compiler_params=_params(("parallel",)),
    )(qbt, kb, vbt, vbt, sinks)


def _swa_bwd(qb, qbt, kb, kbt, vb, sinks, dob, dobt, tq):
    s = qb.shape[0]
    n_heads = SWA_KV_HEADS * SWA_GROUP

    def body(q_ref, qt_ref, k_ref, ktb_ref, ktc_ref, v_ref, s_ref, do_ref, dot_ref, dq_ref, dk_ref, dv_ref, ds_ref):
        i = pl.program_id(0)

        @pl.when(i == 0)
        def _():
            dk_ref[...] = jnp.zeros(dk_ref.shape, F32)
            dv_ref[...] = jnp.zeros(dv_ref.shape, F32)
            ds_ref[...] = jnp.zeros(ds_ref.shape, F32)

        mask = _swa_mask(i, tq)
        kw = [_swa_rows(k_ref, g, i, tq) for g in range(SWA_KV_HEADS)]
        vw = [_swa_rows(v_ref, g, i, tq) for g in range(SWA_KV_HEADS)]
        ktw = [jnp.concatenate([ktb_ref[g], ktc_ref[g]], axis=1) for g in range(SWA_KV_HEADS)]
        before = pl.ds(pl.multiple_of(jnp.maximum(i * tq - WINDOW, 0), WINDOW), WINDOW)
        own = pl.ds(pl.multiple_of(i * tq, tq), tq)

        def products(hd):
            rows = slice(hd * HEAD_DIM, (hd + 1) * HEAD_DIM)
            return _dot(kw[hd // SWA_GROUP], qt_ref[rows, :]), _dot(vw[hd // SWA_GROUP], dot_ref[rows, :])

        nxt = products(0)
        for g in range(SWA_KV_HEADS):
            dsinks = []
            dk_acc = jnp.zeros((tq + WINDOW, HEAD_DIM), F32)
            dv_acc = jnp.zeros((tq + WINDOW, HEAD_DIM), F32)
            for hh in range(SWA_GROUP):
                hd = g * SWA_GROUP + hh
                rows = slice(hd * HEAD_DIM, (hd + 1) * HEAD_DIM)
                sc, dp = nxt
                if hd + 1 < n_heads:
                    nxt = products(hd + 1)
                p, p_sink = _swa_probs_t(sc, mask, s_ref[g][:, hh:hh + 1])
                delta = jnp.sum(p * dp, axis=0, keepdims=True)
                dsc = (p * (dp - delta)).astype(BF)
                dq_ref[rows, :] = _dot(ktw[g], dsc)
                dk_acc = dk_acc + _dot(dsc, q_ref[:, rows])
                dv_acc = dv_acc + _dot(p.astype(BF), do_ref[:, rows])
                dsinks.append(-jnp.sum(p_sink * delta, axis=1, keepdims=True))
            dk_ref[g, before, :] += dk_acc[:WINDOW]
            dk_ref[g, own, :] += dk_acc[WINDOW:]
            dv_ref[g, before, :] += dv_acc[:WINDOW]
            dv_ref[g, own, :] += dv_acc[WINDOW:]
            ds_ref[g] += jnp.concatenate(dsinks, axis=1)

    kvspec = _full((SWA_KV_HEADS, s, HEAD_DIM))
    qspec = pl.BlockSpec((tq, SWA_W), lambda i: (i, 0))
    qspec_t = pl.BlockSpec((SWA_W, tq), lambda i: (0, i))
    sspec = _full((SWA_KV_HEADS, 1, SWA_GROUP))
    kvshape = jax.ShapeDtypeStruct((SWA_KV_HEADS, s, HEAD_DIM), F32)
    return _call(
        body, name="swa_bwd",
        out_shape=(jax.ShapeDtypeStruct((SWA_W, s), F32), kvshape, kvshape,
                   jax.ShapeDtypeStruct((SWA_KV_HEADS, 1, SWA_GROUP), F32)),
        grid=(s // tq,),
        in_specs=[qspec, qspec_t, kvspec, _swa_before(HEAD_DIM, tq),
                  pl.BlockSpec((SWA_KV_HEADS, HEAD_DIM, tq), lambda i: (0, 0, i)), kvspec, sspec, qspec, qspec_t],
        out_specs=(qspec_t, kvspec, kvspec, sspec),
        compiler_params=_params(("arbitrary",)),
    )(qb, qbt, kb, kbt, kbt, vb, sinks, dob, dobt)


def _pairs_to_rows(ref, n_rows=HEAD_DIM):
    parts = []
    for a in range(0, FOX_HEADS, 2):
        parts.append(jnp.concatenate([ref[a][:n_rows], ref[a + 1][:n_rows]], axis=0).T)
    return jnp.concatenate(parts, axis=1)


def _blocks_to_rows(ref):
    return jnp.concatenate([ref[a:a + LANES, :].T for a in range(0, ref.shape[0], LANES)], axis=1)


def _out_proj(oat, za, obt, zb, x, tgt, w_out, w_out_t, gate, g_post, inv_l, tm):
    s = x.shape[0]

    def body(oat_ref, za_ref, obt_ref, zb_ref, x_ref, t_ref, w_ref, wt_ref, gate_ref, gp_ref, il_ref,
             dout_ref, doat_ref, dla_ref, dza_ref, dob_ref, dobt_ref, dzb_ref, gw_ref, dgate_ref, dgp_ref, loss_ref):
        i = pl.program_id(0)

        @pl.when(i == 0)
        def _():
            gw_ref[...] = jnp.zeros(gw_ref.shape, F32)
            dgate_ref[...] = jnp.zeros(dgate_ref.shape, F32)
            dgp_ref[...] = jnp.zeros(dgp_ref.shape, F32)
            loss_ref[...] = jnp.zeros(loss_ref.shape, F32)

        oa_v = _pairs_to_rows(oat_ref)
        ob_v = _blocks_to_rows(obt_ref)
        za_v, zb_v = za_ref[...], zb_ref[...]
        sga, sgb = _sigmoid(za_v), _sigmoid(zb_v)
        sila, silb = za_v * sga, zb_v * sgb
        u = jnp.concatenate([oa_v * sila, ob_v * silb], axis=1).astype(BF)
        yv = _dot(u, w_ref[...])
        yhat, rstd = _rms_hat(yv)
        gp, gate_v = gp_ref[...], gate_ref[...]
        nrm = yhat * gp
        diff = (x_ref[...] + gate_v * nrm) - t_ref[...]
        loss_ref[...] += 0.5 * jnp.sum(jnp.sum(diff * diff, axis=1, keepdims=True), axis=0, keepdims=True) / D_MODEL
        dout = diff * (1.0 / D_MODEL)
        dout_ref[...] = dout
        dgate_ref[...] += jnp.sum(dout * nrm, axis=0, keepdims=True)
        dn = dout * gate_v
        dgp_ref[...] += jnp.sum(dn * yhat, axis=0, keepdims=True)
        dyhat = dn * gp
        dy = (rstd * (dyhat - yhat * jnp.mean(dyhat * yhat, axis=1, keepdims=True))).astype(BF)
        gw_ref[...] += _dot_tn(u, dy)
        du = _dot(dy, wt_ref[...])
        dua, dub = du[:, :FOX_W], du[:, FOX_W:]
        doa = dua * sila
        for a in range(0, FOX_HEADS, 2):
            pair_t = doa[:, a * HEAD_DIM:(a + 2) * HEAD_DIM].T
            for hd, rows in ((a, slice(0, HEAD_DIM)), (a + 1, slice(HEAD_DIM, 2 * HEAD_DIM))):
                inv_l = il_ref[hd]
                doat_ref[hd] = (pair_t[rows] * inv_l).astype(BF)
                dla_ref[hd] = jnp.sum(pair_t[rows] * oat_ref[hd], axis=0, keepdims=True) * inv_l
        dob = dub * silb
        dob_ref[...] = dob.astype(BF)
        for a in range(0, SWA_W, LANES):
            dobt_ref[a:a + LANES, :] = dob[:, a:a + LANES].T.astype(BF)
        dza_ref[...] = (dua * oa_v * (sga * (1.0 + za_v * (1.0 - sga)))).astype(BF)
        dzb_ref[...] = (dub * ob_v * (sgb * (1.0 + zb_v * (1.0 - sgb)))).astype(BF)

    row = lambda w: pl.BlockSpec((tm, w), lambda i: (i, 0))
    heads_t = lambda w: pl.BlockSpec((FOX_HEADS, w, tm), lambda i: (0, 0, i))
    vec = _full((1, D_MODEL))
    mat = _full((D_MODEL, D_MODEL))
    out_shape = (
        jax.ShapeDtypeStruct((s, D_MODEL), F32),
        jax.ShapeDtypeStruct((FOX_HEADS, HEAD_DIM, s), BF), jax.ShapeDtypeStruct((FOX_HEADS, 1, s), F32),
        jax.ShapeDtypeStruct((s, FOX_W), BF), jax.ShapeDtypeStruct((s, SWA_W), BF), jax.ShapeDtypeStruct((SWA_W, s), BF),
        jax.ShapeDtypeStruct((s, SWA_W), BF),
        jax.ShapeDtypeStruct((D_MODEL, D_MODEL), F32),
        jax.ShapeDtypeStruct((1, D_MODEL), F32), jax.ShapeDtypeStruct((1, D_MODEL), F32),
        jax.ShapeDtypeStruct((1, 1), F32),
    )
    col = pl.BlockSpec((SWA_W, tm), lambda i: (0, i))
    return _call(
        body, name="out_proj", out_shape=out_shape, grid=(s // tm,),
        in_specs=[heads_t(HEAD_DIM), row(FOX_W), col, row(SWA_W), row(D_MODEL), row(D_MODEL), mat, mat, vec, vec,
                  heads_t(1)],
        out_specs=(row(D_MODEL), heads_t(HEAD_DIM), heads_t(1), row(FOX_W), row(SWA_W), col, row(SWA_W), mat, vec, vec,
                   _full((1, 1))),
        compiler_params=_params(("arbitrary",)),
    )(oat, za, obt, zb, x, tgt, w_out, w_out_t, gate, g_post, inv_l)


def _assemble_dproj(dqt, dkt, dvt, dza, dqb, dzb, dkb, dvb, df, cos_t, sin_t, tm):
    s = dza.shape[0]

    def body(dqt_ref, dkt_ref, dvt_ref, dza_ref, dqb_ref, dzb_ref, dkb_ref, dvb_ref, df_ref, cos_ref, sin_ref, o_ref):
        def cat(ref, n):
            return jnp.concatenate([ref[hd] for hd in range(n)], axis=1)

        cos2, sin2 = cos_ref[...], sin_ref[...]
        cos8 = jnp.concatenate([cos2] * 4, axis=1)
        sin8 = jnp.concatenate([sin2] * 4, axis=1)
        scale = HEAD_DIM ** -0.5
        o_ref[:, C_QA:C_QA + FOX_W] = (_pairs_to_rows(dqt_ref.at[:, 0]) * scale).astype(BF)
        o_ref[:, C_KA:C_KA + FOX_W] = (_pairs_to_rows(dkt_ref) * LN2).astype(BF)
        o_ref[:, C_VA:C_VA + FOX_W] = _pairs_to_rows(dvt_ref).astype(BF)
        o_ref[:, C_ZA:C_ZA + FOX_W] = dza_ref[...]
        dq = _blocks_to_rows(dqb_ref) * scale
        o_ref[:, C_QB:C_QB + SWA_W] = (dq * cos8 - _rope_partner(dq) * sin8).astype(BF)
        o_ref[:, C_ZB:C_ZB + SWA_W] = dzb_ref[...]
        dk = cat(dkb_ref, SWA_KV_HEADS)
        o_ref[:, C_KB:C_KB + SWA_KV_W] = (dk * cos2 - _rope_partner(dk) * sin2).astype(BF)
        o_ref[:, C_VB:C_VB + SWA_KV_W] = cat(dvb_ref, SWA_KV_HEADS).astype(BF)
        o_ref[:, C_F:C_F + LANES] = df_ref[...].astype(BF)

    row = lambda w: pl.BlockSpec((tm, w), lambda i: (i, 0))
    heads = lambda n: pl.BlockSpec((n, tm, HEAD_DIM), lambda i: (0, i, 0))
    heads_t = lambda w: pl.BlockSpec((FOX_HEADS, w, tm), lambda i: (0, 0, i))
    return _call(
        body, name="assemble_dproj", out_shape=jax.ShapeDtypeStruct((s, WP), BF), grid=(s // tm,),
        in_specs=[pl.BlockSpec((FOX_HEADS, 1, VT_ROWS, tm), lambda i: (0, i, 0, 0)), heads_t(VT_ROWS), heads_t(HEAD_DIM),
                  row(FOX_W), pl.BlockSpec((SWA_W, tm), lambda i: (0, i)), row(SWA_W), heads(SWA_KV_HEADS),
                  heads(SWA_KV_HEADS), row(LANES), row(LANES), row(LANES)],
        out_specs=row(WP), compiler_params=_params(("parallel",)),
    )(dqt, dkt, dvt, dza, dqb, dzb, dkb, dvb, df, cos_t, sin_t)


def _in_proj_bwd_x(dproj, w_al_t, x, dout, g_pre, scale1p, tm, parts):
    s = x.shape[0]
    n_steps = s // tm
    masks = [(1, 0), (0, 1), (1, 1)]

    def body(dp_ref, wt_ref, x_ref, dout_ref, g_ref, sc_ref, parts_ref, gx_ref, dsh_ref, dsc_ref, dg_ref, got_ref,
             send_sems, recv_sems, local_sem):
        i = pl.program_id(0)
        cx, cy, cc = lax.axis_index("x"), lax.axis_index("y"), lax.axis_index("c")
        me = 2 * cx + cy
        own = pltpu.make_async_copy(parts_ref.at[me], got_ref.at[me], local_sem)

        def copy(k, send):
            dx, dy = masks[k]
            peer = 2 * (cx ^ dx) + (cy ^ dy)
            return pltpu.make_async_remote_copy(
                src_ref=parts_ref.at[peer if send else me], dst_ref=got_ref.at[me if send else peer],
                send_sem=send_sems.at[k], recv_sem=recv_sems.at[k], device_id=(cx ^ dx, cy ^ dy, cc), device_id_type=MESH)

        @pl.when(i == 0)
        def _():
            dsh_ref[...] = jnp.zeros(dsh_ref.shape, F32)
            dsc_ref[...] = jnp.zeros(dsc_ref.shape, F32)
            dg_ref[...] = jnp.zeros(dg_ref.shape, F32)
            own.start()
            for k in range(len(masks)):
                copy(k, True).start()

        @pl.when(i == n_steps - 1)
        def _():
            for k in range(len(masks)):
                copy(k, False).wait_recv()
            for k in range(len(masks)):
                copy(k, True).wait_send()
            own.wait()

        dh = _dot(dp_ref[...], wt_ref[...])
        xhat, rstd = _rms_hat(x_ref[...])
        g, sc = g_ref[...], sc_ref[...]
        dsh_ref[...] += jnp.sum(dh, axis=0, keepdims=True)
        dhx = dh * xhat
        dsc_ref[...] += jnp.sum(dhx * g, axis=0, keepdims=True)
        dg_ref[...] += jnp.sum(dhx * sc, axis=0, keepdims=True)
        dxhat = dh * (g * sc)
        gx_ref[...] = dout_ref[...] + rstd * (dxhat - xhat * jnp.mean(dxhat * xhat, axis=1, keepdims=True))

    row = lambda w: pl.BlockSpec((tm, w), lambda i: (i, 0))
    vec = _full((1, D_MODEL))
    vshape = jax.ShapeDtypeStruct((1, D_MODEL), F32)
    hbm = pl.BlockSpec(memory_space=pl.ANY)
    return _call(
        body, name="in_proj_bwd_x",
        out_shape=(jax.ShapeDtypeStruct((s, D_MODEL), F32), vshape, vshape, vshape,
                   jax.ShapeDtypeStruct(parts.shape, parts.dtype)),
        grid=(n_steps,),
        in_specs=[row(WP), _full((WP, D_MODEL)), row(D_MODEL), row(D_MODEL), vec, vec, hbm],
        out_specs=(row(D_MODEL), vec, vec, vec, hbm),
        scratch_shapes=[pltpu.SemaphoreType.DMA((3,)), pltpu.SemaphoreType.DMA((3,)), pltpu.SemaphoreType.DMA],
        compiler_params=_params(("arbitrary",), has_side_effects=True),
    )(dproj, w_al_t, x, dout, g_pre, scale1p, parts)


def _in_proj_bwd_w(h, dproj, tk, tn):
    s = h.shape[0]
    n_k = s // tk

    def body(h_ref, dp_ref, gw_ref, acc_scr):
        k = pl.program_id(1)

        @pl.when(k == 0)
        def _():
            acc_scr[...] = jnp.zeros(acc_scr.shape, F32)

        acc_scr[...] += _dot_tn(h_ref[...], dp_ref[...])

        @pl.when(k == n_k - 1)
        def _():
            gw_ref[...] = acc_scr[...].astype(BF)

    return _call(
        body, name="in_proj_bwd_w", out_shape=jax.ShapeDtypeStruct((D_MODEL, WP), BF), grid=(WP // tn, n_k),
        in_specs=[pl.BlockSpec((tk, D_MODEL), lambda n, k: (k, 0)), pl.BlockSpec((tk, tn), lambda n, k: (k, n))],
        out_specs=pl.BlockSpec((D_MODEL, tn), lambda n, k: (0, n)),
        scratch_shapes=[pltpu.VMEM((D_MODEL, tn), F32)],
        compiler_params=_params(("parallel", "arbitrary")),
    )(h, dproj)


def _align_w_in(w_cols):
    def part(name, width):
        return w_cols[:, _SRC[name]:_SRC[name] + width]

    fpad = jnp.pad(part("fa", FOX_HEADS), ((0, 0), (0, LANES - FOX_HEADS)))
    return jnp.concatenate([part("qa", FOX_W), part("ka", FOX_W), part("va", FOX_W), part("za", FOX_W),
                            part("qb", SWA_W), part("zb", SWA_W), part("kb", SWA_KV_W), part("vb", SWA_KV_W), fpad], axis=1)


def _unalign_w_in(g_al):
    def part(c0, width):
        return g_al[:, c0:c0 + width]

    return jnp.concatenate([part(C_QA, FOX_W), part(C_KA, FOX_W), part(C_VA, FOX_W), part(C_F, FOX_HEADS),
                            part(C_ZA, FOX_W), part(C_QB, SWA_W), part(C_KB, SWA_KV_W), part(C_VB, SWA_KV_W),
                            part(C_ZB, SWA_W)], axis=1)


def _rope_tables(positions):
    inv_freq = ROPE_THETA ** (-jnp.arange(HALF, dtype=F32) / HALF)
    ang = positions.astype(F32)[:, None] * inv_freq
    cos, sin = jnp.cos(ang), jnp.sin(ang)
    return jnp.concatenate([cos, cos, cos, cos], axis=1), jnp.concatenate([-sin, sin, -sin, sin], axis=1)


def _tiles(s):
    if s >= 4096:
        return dict(tm=512, blk=512, bq=2048, bk=2048, bk_bwd=2048, chunk=256, tq=256, tm_out=512, tk=1024, tn=1152)
    return dict(tm=128, blk=128, bq=256, bk=256, bk_bwd=256, chunk=128, tq=128, tm_out=128, tk=128, tn=1152)


def kernel(x, c, positions, w_ada, b_ada, g_pre, w_in, b_fgate, sinks, w_out, g_post, loss_target, m_w_ada, m_b_ada, m_g_pre, m_w_in, m_b_fgate, m_sinks, m_w_out, m_g_post, v_w_ada, v_b_ada, v_g_pre, v_w_in, v_b_fgate, v_sinks, v_w_out, v_g_post):
    s = x.shape[1]
    t = _tiles(s)
    nc = s // LANES
    rows = FOX_HEADS * nc
    me = 4 * lax.axis_index("x") + 2 * lax.axis_index("y") + lax.axis_index("c")
    chip = 2 * lax.axis_index("x") + lax.axis_index("y")
    core = lax.axis_index("c")
    x2, tgt = x[0], loss_target[0]

    a_all, mod_all = _ada_exchange(c, w_ada[0])
    mod_rows = lax.dynamic_index_in_dim(mod_all, me, axis=1, keepdims=False)
    mod = mod_rows.reshape(N_CHIPS, 2, W_ADA_SHARD)[:, 0, :].reshape(1, 3 * D_MODEL) + b_ada
    shift, scale1p, gate = mod[:, :D_MODEL], 1.0 + mod[:, D_MODEL:2 * D_MODEL], mod[:, 2 * D_MODEL:]

    w_in_pad = jnp.pad(w_in[0].astype(BF), ((0, 0), (0, W_IN_SHARD_PAD - W_IN_SHARD)))
    w_all = _allgather_chips(w_in_pad.reshape(2, D_MODEL // 2, -1), "gather_w_in").reshape(N_CHIPS, D_MODEL, -1)
    w_cols = jnp.concatenate([w_all[k, :, :W_IN_SHARD] for k in range(N_CHIPS)], axis=1)
    w_al = _align_w_in(w_cols)
    w_al_t = w_al.T

    cos_t, sin_t = _rope_tables(positions[0])

    f_pad = _forget_logits(x2, g_pre, scale1p, shift, w_al[:, C_F:], t["tk"])
    f_rows = f_pad[:, :FOX_HEADS].T.reshape(rows, LANES)
    bias_rows = jnp.repeat(b_fgate[0], nc)[:, None]
    cum = _log_forget_cumsum(f_rows, bias_rows, nc).reshape(FOX_HEADS, s)
    h, qat, ka, kat, va, vat, za, zb, qb, kb, vb, qbt, kbt, vbt, m_own, w_out_all = _in_proj(
        x2, g_pre, scale1p, shift, w_al[:, C_VA:C_F], w_al_t[:C_ZA], cum, cos_t, sin_t,
        w_out[0].astype(BF).reshape(2, W_OUT_SHARD // 2, D_MODEL), t["tm"])
    w_out_all = w_out_all.reshape(D_MODEL, D_MODEL)
    w_out_t = w_out_all.T
    m_own = m_own[:, None, :]
    fox_args = (qat, ka, vat, m_own, t["bq"], t["bk"], t["chunk"])
    oat, lse, bad, pt = _fox_fwd(*fox_args, running_max=False)
    overflowed = jnp.max(bad) > 0.0
    oat, lse = lax.cond(overflowed, lambda: _fox_fwd(*fox_args, running_max=True)[:2], lambda: (oat, lse))
    inv_l = jnp.where(overflowed, 1.0, jnp.exp2(m_own - lse))
    sinks_g = sinks.reshape(SWA_KV_HEADS, 1, SWA_GROUP)
    obt = _swa_fwd(qbt, kb, vbt, sinks_g, t["tq"])

    dout, doat, delta_a, dza, dob, dobt, dzb, gw_out, dgate, dg_post, loss_part = _out_proj(
        oat, za, obt, zb, x2, tgt, w_out_all, w_out_t, gate, g_post, inv_l, t["tm_out"])

    bwd_args = (qat, ka, kat, va, doat, lse, delta_a)
    bwd_tiles = (t["bq"], t["bk_bwd"], t["chunk"], t["blk"])
    dqt, dkt, dvt = lax.cond(overflowed, lambda: _fox_bwd(*bwd_args, None, *bwd_tiles),
                             lambda: _fox_bwd(*bwd_args, pt, *bwd_tiles))
    dcum = dqt[:, :, HEAD_DIM, :].reshape(FOX_HEADS, s) - dkt[:, HEAD_DIM, :]
    df_rows, db_heads = _log_forget_cumsum_bwd(dcum.reshape(rows, LANES), f_rows, bias_rows, nc)
    df_pad = jnp.pad(df_rows.reshape(FOX_HEADS, s).T, ((0, 0), (0, LANES - FOX_HEADS)))
    dqb, dkb, dvb, dsinks = _swa_bwd(qb, qbt, kb, kbt, vb, sinks_g, dob, dobt, t["tq"])

    dproj = _assemble_dproj(dqt, dkt, dvt, dza, dqb, dzb, dkb, dvb, df_pad, cos_t, sin_t, t["blk"])
    gw_in = _unalign_w_in(_in_proj_bwd_w(h, dproj, t["tk"], t["tn"]))

    gin = jnp.stack([jnp.pad(gw_in[:, k * W_IN_SHARD:(k + 1) * W_IN_SHARD], ((0, 0), (0, W_IN_SHARD_PAD - W_IN_SHARD)))
                     for k in range(N_CHIPS)])
    gout = gw_out.astype(BF).reshape(N_CHIPS, D_MODEL, W_OUT_SHARD)
    gbig = jnp.concatenate([gin, gout], axis=2)
    half = D_MODEL // 2
    gw = W_IN_SHARD_PAD + W_OUT_SHARD
    keep = lax.dynamic_slice_in_dim(gbig, core * half, half, axis=1)
    give = lax.dynamic_slice_in_dim(gbig, (1 - core) * half, half, axis=1)
    got = _swap_sibling(give.reshape(N_CHIPS * half, gw), "swap_grad_halves")
    pair = _add(keep.reshape(N_CHIPS * half, gw), got, "add_pair", BF).reshape(N_CHIPS, half, gw)
    grad_x, dshift, dscale, dg_pre, from_chips = _in_proj_bwd_x(
        dproj, w_al_t, x2, dout, g_pre, scale1p, t["tm_out"], pair)

    pad_lane = lambda vrow: jnp.pad(vrow, ((0, 0), (0, LANES - vrow.shape[1])))
    packed = jnp.concatenate([dshift, dscale, dgate, dg_pre, dg_post,
                              pad_lane(db_heads.reshape(1, FOX_HEADS)), pad_lane(dsinks.reshape(1, FOX_HEADS)),
                              pad_lane(loss_part)], axis=1)
    parts = _allgather_devices(packed, "gather_partials")
    tot = _sum_devices(parts)
    loss = tot[0, P_LOSS]
    g_b_ada = tot[:, P_DMOD:P_DMOD + 3 * D_MODEL]
    g_g_pre = tot[:, P_GPRE:P_GPRE + D_MODEL]
    g_g_post = tot[:, P_GPOST:P_GPOST + D_MODEL]
    g_b_fgate = tot[:, P_BF:P_BF + FOX_HEADS]
    g_sinks = tot[:, P_SINK:P_SINK + FOX_HEADS]
    dm_shard = lax.dynamic_slice_in_dim(parts[:, 0, :3 * D_MODEL], chip * W_ADA_SHARD, W_ADA_SHARD, axis=1)
    g_w_ada = _grad_w_ada(a_all.T, dm_shard)

    mine = _sum_chips(from_chips, "sum_chips")
    other = _swap_sibling(mine, "swap_grad_result")
    lo = jnp.where(core == 0, mine, other)
    hi = jnp.where(core == 0, other, mine)
    gfull = jnp.concatenate([lo, hi], axis=0)
    g_w_in = gfull[:, :W_IN_SHARD]
    g_w_out = gfull[:, W_IN_SHARD_PAD:].reshape(W_OUT_SHARD, D_MODEL)

    grads = dict(w_ada=g_w_ada, b_ada=g_b_ada, g_pre=g_g_pre, w_in=g_w_in, b_fgate=g_b_fgate, sinks=g_sinks,
                 w_out=g_w_out, g_post=g_g_post)
    weights = dict(w_ada=w_ada, b_ada=b_ada, g_pre=g_pre, w_in=w_in, b_fgate=b_fgate, sinks=sinks, w_out=w_out, g_post=g_post)
    moms = dict(w_ada=m_w_ada, b_ada=m_b_ada, g_pre=m_g_pre, w_in=m_w_in, b_fgate=m_b_fgate, sinks=m_sinks, w_out=m_w_out, g_post=m_g_post)
    vars_ = dict(w_ada=v_w_ada, b_ada=v_b_ada, g_pre=v_g_pre, w_in=v_w_in, b_fgate=v_b_fgate, sinks=v_sinks, w_out=v_w_out, g_post=v_g_post)
    names = ["w_ada", "b_ada", "g_pre", "w_in", "b_fgate", "sinks", "w_out", "g_post"]
    g_out, d_out, m_out, v_out = [], [], [], []
    for n in names:
        if n == "w_in":
            flat = lambda a: jnp.transpose(a, (2, 0, 1)).reshape(W_IN_SHARD * D_MODEL // LANES, LANES)
            unflat = lambda a: jnp.transpose(a.reshape(W_IN_SHARD, 1, D_MODEL), (1, 2, 0))
            outs = _adamw(flat(w_in), flat(grads[n][None]), flat(moms[n]), flat(vars_[n]), "adamw_" + n)
            go, d, nm, nv = (unflat(a) for a in outs)
        else:
            g2 = grads[n].reshape(weights[n].shape[-2:])
            go, d, nm, nv = _adamw(weights[n], g2, moms[n], vars_[n], "adamw_" + n)
        g_out.append(go)
        d_out.append(d)
        m_out.append(nm)
        v_out.append(nv)
    return (loss, grad_x.reshape(x.shape), *g_out, *d_out, *m_out, *v_out)
```

```python
import jax
import jax.numpy as jnp
from jax import lax
from jax.experimental import pallas as pl
from jax.experimental.pallas import tpu as pltpu

_INTERPRET = False

D_MODEL = 1024
HEAD_DIM = 64
HALF = HEAD_DIM // 2
AUG_DIM = 128
AUG_ROWS = 8
VT_ROWS = 80
LOG2E = 1.4426950408889634
LN2 = 0.6931471805599453
Q_SCALE = LOG2E * 64 ** -0.5
FOX_HEADS = 8
FOX_W = 512
SWA_W = 512
SWA_KV_HEADS = 2
SWA_GROUP = 4
SWA_KV_W = 128
WINDOW = 128
ROPE_THETA = 10000.0
RMS_EPS = 1e-6
IN_WIDTH = 3336
N_CHIPS = 4
N_DEV = 8
W_IN_SHARD = IN_WIDTH // N_CHIPS
W_IN_SHARD_PAD = 896
W_ADA_SHARD = 3 * D_MODEL // N_CHIPS
W_OUT_SHARD = D_MODEL // N_CHIPS
LANES = 128

_SRC = dict(qa=0, ka=512, va=1024, fa=1536, za=1544, qb=2056, kb=2568, vb=2696, zb=2824)
C_QA, C_KA, C_VA, C_ZA, C_QB, C_ZB, C_KB, C_VB, C_F = 0, 512, 1024, 1536, 2048, 2560, 3072, 3200, 3328
WP = 3456

ADAM_LR = 0.001
ADAM_B1 = 0.9
ADAM_B2 = 0.999
ADAM_EPS = 1e-08
ADAM_WD = 0.01
ADAM_STEP = 10
ADAMW_BLOCK_ELEMS = 300_000

VMEM_LIMIT = 56 * 1024 * 1024
NEG = -1e30
OVERFLOW_GUARD = 1e30
MESH = pl.DeviceIdType.MESH
BF = jnp.bfloat16
F32 = jnp.float32

P_DMOD, P_GPRE, P_GPOST, P_BF, P_SINK, P_LOSS, P_LEN = 0, 3072, 4096, 5120, 5248, 5376, 5504


def _call(body, **kw):
    return pl.pallas_call(body, interpret=_INTERPRET, **kw)


def _params(sem=None, **kw):
    return pltpu.CompilerParams(dimension_semantics=sem, vmem_limit_bytes=VMEM_LIMIT, **kw)


def _full(shape):
    zeros = (0,) * len(shape)
    return pl.BlockSpec(shape, lambda *_: zeros)


def _dot(a, b):
    return jnp.dot(a, b, preferred_element_type=F32)


def _dot_nt(a, b):
    return lax.dot_general(a, b, (((1,), (1,)), ((), ())), preferred_element_type=F32)


def _dot_tn(a, b):
    return lax.dot_general(a, b, (((0,), (0,)), ((), ())), preferred_element_type=F32)


def _sigmoid(z):
    return 1.0 / (1.0 + jnp.exp(-z))


def _rope_partner(t):
    w = t.shape[-1]
    lane = lax.broadcasted_iota(jnp.int32, t.shape, t.ndim - 1)
    return jnp.where((lane & (HEAD_DIM - 1)) < HALF, pltpu.roll(t, w - HALF, t.ndim - 1), pltpu.roll(t, HALF, t.ndim - 1))


def _allgather_devices(v, name):
    r, cdim = v.shape
    masks = [(dx, dy, dc) for dx in (0, 1) for dy in (0, 1) for dc in (0, 1)][1:]

    def body(v_ref, out_ref, send_sems, recv_sems):
        x, y, c = lax.axis_index("x"), lax.axis_index("y"), lax.axis_index("c")
        me = 4 * x + 2 * y + c
        out_ref[me] = v_ref[...]
        copies = []
        for k, (dx, dy, dc) in enumerate(masks):
            cp = pltpu.make_async_remote_copy(
                src_ref=v_ref, dst_ref=out_ref.at[me], send_sem=send_sems.at[k], recv_sem=recv_sems.at[k],
                device_id=(x ^ dx, y ^ dy, c ^ dc), device_id_type=MESH)
            cp.start()
            copies.append(cp)
        for k, (dx, dy, dc) in enumerate(masks):
            peer = 4 * (x ^ dx) + 2 * (y ^ dy) + (c ^ dc)
            pltpu.make_async_remote_copy(
                src_ref=v_ref, dst_ref=out_ref.at[peer], send_sem=send_sems.at[k], recv_sem=recv_sems.at[k],
                device_id=(x ^ dx, y ^ dy, c ^ dc), device_id_type=MESH).wait_recv()
        for cp in copies:
            cp.wait_send()

    return _call(
        body, name=name, out_shape=jax.ShapeDtypeStruct((N_DEV, r, cdim), v.dtype),
        in_specs=[pl.BlockSpec(memory_space=pltpu.VMEM)], out_specs=pl.BlockSpec(memory_space=pltpu.VMEM),
        scratch_shapes=[pltpu.SemaphoreType.DMA((7,)), pltpu.SemaphoreType.DMA((7,))],
        compiler_params=pltpu.CompilerParams(has_side_effects=True),
    )(v)


CHIP_MASKS = [(1, 0), (0, 1), (1, 1)]
CHIP_GATHER_SEMS = [pltpu.SemaphoreType.DMA((2 * len(CHIP_MASKS),)), pltpu.SemaphoreType.DMA((2 * len(CHIP_MASKS),)),
                    pltpu.SemaphoreType.DMA]


def _chip_gather(v_ref, out_ref, send_sems, recv_sems, local_sem):
    n = len(CHIP_MASKS)
    x, y, c = lax.axis_index("x"), lax.axis_index("y"), lax.axis_index("c")
    me = 2 * x + y
    mine = pltpu.make_async_copy(v_ref, out_ref.at[me], local_sem)

    def copy(k, chip, half, to):
        return pltpu.make_async_remote_copy(
            src_ref=v_ref.at[half] if k < n else out_ref.at[chip, half], dst_ref=out_ref.at[chip, half],
            send_sem=send_sems.at[k], recv_sem=recv_sems.at[k], device_id=to, device_id_type=MESH)

    def start():
        mine.start()
        for k, (dx, dy) in enumerate(CHIP_MASKS):
            copy(k, me, c, (x ^ dx, y ^ dy, c)).start()

    def finish():
        passed = []
        for k, (dx, dy) in enumerate(CHIP_MASKS):
            peer = 2 * (x ^ dx) + (y ^ dy)
            copy(k, peer, c, (x, y, c)).wait_recv()
            cp = copy(n + k, peer, c, (x, y, 1 - c))
            cp.start()
            passed.append(cp)
        for k, (dx, dy) in enumerate(CHIP_MASKS):
            copy(n + k, 2 * (x ^ dx) + (y ^ dy), 1 - c, (x, y, c)).wait_recv()
        for k, (dx, dy) in enumerate(CHIP_MASKS):
            copy(k, me, c, (x ^ dx, y ^ dy, c)).wait_send()
        for cp in passed:
            cp.wait_send()
        mine.wait()

    return start, finish


def _allgather_chips(v, name):
    _, r, cdim = v.shape
    n = len(CHIP_MASKS)

    def body(v_ref, out_ref, send_sems, recv_sems, local_sem):
        start, finish = _chip_gather(v_ref, out_ref, send_sems, recv_sems, local_sem)
        start()
        finish()

    return _call(
        body, name=name, out_shape=jax.ShapeDtypeStruct((N_CHIPS, 2, r, cdim), v.dtype),
        in_specs=[pl.BlockSpec(memory_space=pl.ANY)], out_specs=pl.BlockSpec(memory_space=pl.ANY),
        scratch_shapes=[pltpu.SemaphoreType.DMA((2 * n,)), pltpu.SemaphoreType.DMA((2 * n,)), pltpu.SemaphoreType.DMA],
        compiler_params=pltpu.CompilerParams(has_side_effects=True),
    )(v)


def _swap_sibling(v, name):
    def body(v_ref, out_ref, send_sem, recv_sem):
        x, y, c = lax.axis_index("x"), lax.axis_index("y"), lax.axis_index("c")
        cp = pltpu.make_async_remote_copy(
            src_ref=v_ref, dst_ref=out_ref, send_sem=send_sem, recv_sem=recv_sem,
            device_id=(x, y, 1 - c), device_id_type=MESH)
        cp.start()
        cp.wait()

    return _call(
        body, name=name, out_shape=jax.ShapeDtypeStruct(v.shape, v.dtype),
        in_specs=[pl.BlockSpec(memory_space=pl.ANY)], out_specs=pl.BlockSpec(memory_space=pl.ANY),
        scratch_shapes=[pltpu.SemaphoreType.DMA, pltpu.SemaphoreType.DMA],
        compiler_params=pltpu.CompilerParams(has_side_effects=True),
    )(v)


def _ada_exchange(c, w_ada_shard):
    masks = [(dx, dy, dc) for dx in (0, 1) for dy in (0, 1) for dc in (0, 1)][1:]
    n = len(masks)

    def body(c_ref, w_ref, a_ref, mod_ref, c_all, send_sems, recv_sems):
        x, y, cc = lax.axis_index("x"), lax.axis_index("y"), lax.axis_index("c")
        me = 4 * x + 2 * y + cc

        def gather(src_ref, dst_ref, first):
            sends = []
            for k, (dx, dy, dc) in enumerate(masks):
                cp = pltpu.make_async_remote_copy(
                    src_ref=src_ref, dst_ref=dst_ref.at[me], send_sem=send_sems.at[first + k],
                    recv_sem=recv_sems.at[first + k], device_id=(x ^ dx, y ^ dy, cc ^ dc), device_id_type=MESH)
                cp.start()
                sends.append(cp)
            for k, (dx, dy, dc) in enumerate(masks):
                peer = 4 * (x ^ dx) + 2 * (y ^ dy) + (cc ^ dc)
                pltpu.make_async_remote_copy(
                    src_ref=src_ref, dst_ref=dst_ref.at[peer], send_sem=send_sems.at[first + k],
                    recv_sem=recv_sems.at[first + k], device_id=(x ^ dx, y ^ dy, cc ^ dc), device_id_type=MESH).wait_recv()
            return sends

        c_all[me] = c_ref[...]
        sends = gather(c_ref, c_all, 0)
        w_bf = w_ref[...].astype(BF)
        for d in range(N_DEV):
            cv = c_all[d]
            a = cv * _sigmoid(cv)
            a_ref[d:d + 1, :] = a
            mod_ref[me, d:d + 1, :] = _dot(a.astype(BF), w_bf)
        sends += gather(mod_ref.at[me], mod_ref, n)
        for cp in sends:
            cp.wait_send()

    vmem = pl.BlockSpec(memory_space=pltpu.VMEM)
    return _call(
        body, name="ada_exchange",
        out_shape=(jax.ShapeDtypeStruct((N_DEV, D_MODEL), F32), jax.ShapeDtypeStruct((N_DEV, N_DEV, W_ADA_SHARD), F32)),
        in_specs=[vmem, vmem], out_specs=(vmem, vmem),
        scratch_shapes=[pltpu.VMEM((N_DEV, 1, D_MODEL), F32), pltpu.SemaphoreType.DMA((2 * n,)),
                        pltpu.SemaphoreType.DMA((2 * n,))],
        compiler_params=_params(has_side_effects=True),
    )(c, w_ada_shard)


def _grad_w_ada(a_t, dm_shard):
    def body(a_ref, dm_ref, out_ref):
        acc = jnp.zeros((D_MODEL, W_ADA_SHARD), F32)
        for b in range(N_DEV):
            acc = acc + a_ref[:, b:b + 1] * dm_ref[b:b + 1, :]
        out_ref[...] = acc

    return _call(body, name="grad_w_ada", out_shape=jax.ShapeDtypeStruct((D_MODEL, W_ADA_SHARD), F32),
                 compiler_params=_params())(a_t, dm_shard)


def _sum_devices(parts):
    n = parts.shape[-1]

    def body(p_ref, out_ref):
        acc = p_ref[0]
        for b in range(1, N_DEV):
            acc = acc + p_ref[b]
        out_ref[...] = acc

    return _call(body, name="sum_devices", out_shape=jax.ShapeDtypeStruct((1, n), F32), compiler_params=_params())(parts)


def _add(a, b, name, out_dtype):
    r, cdim = a.shape
    tr = min(r, 256)

    def body(a_ref, b_ref, o_ref):
        o_ref[...] = (a_ref[...].astype(F32) + b_ref[...].astype(F32)).astype(out_dtype)

    spec = pl.BlockSpec((tr, cdim), lambda i: (i, 0))
    return _call(body, name=name, out_shape=jax.ShapeDtypeStruct(a.shape, out_dtype), grid=(r // tr,),
                 in_specs=[spec, spec], out_specs=spec, compiler_params=_params(("parallel",)))(a, b)


def _sum_chips_and_share(parts, name):
    _, r, cdim = parts.shape
    tr = 128

    def body(p_ref, o_ref, send_sem, recv_sem):
        x, y, c = lax.axis_index("x"), lax.axis_index("y"), lax.axis_index("c")
        mine = pl.multiple_of(c * r, r)

        def rows(n, carry):
            sl = pl.ds(pl.multiple_of(n * tr, tr), tr)
            p = [p_ref[k, sl, :].astype(F32) for k in range(N_CHIPS)]
            o_ref[pl.ds(pl.multiple_of(mine + n * tr, tr), tr), :] = ((p[0] + p[1]) + p[2]) + p[3]
            return carry

        lax.fori_loop(0, r // tr, rows, 0)
        half = o_ref.at[pl.ds(mine, r), :]
        cp = pltpu.make_async_remote_copy(src_ref=half, dst_ref=half, send_sem=send_sem, recv_sem=recv_sem,
                                          device_id=(x, y, 1 - c), device_id_type=MESH)
        cp.start()
        other = o_ref.at[pl.ds(pl.multiple_of((1 - c) * r, r), r), :]
        pltpu.make_async_remote_copy(src_ref=other, dst_ref=other, send_sem=send_sem, recv_sem=recv_sem,
                                     device_id=(x, y, 1 - c), device_id_type=MESH).wait_recv()
        cp.wait_send()

    vmem = pl.BlockSpec(memory_space=pltpu.VMEM)
    return _call(body, name=name, out_shape=jax.ShapeDtypeStruct((2 * r, cdim), F32), in_specs=[vmem], out_specs=vmem,
                 scratch_shapes=[pltpu.SemaphoreType.DMA, pltpu.SemaphoreType.DMA],
                 compiler_params=_params(has_side_effects=True))(parts)


def _adamw(w, g, m, v, name):
    r, cdim = w.shape[-2:]
    lead = w.ndim - 2
    tr = r if r <= 256 else max(t for t in range(8, ADAMW_BLOCK_ELEMS // cdim + 1, 8) if r % t == 0)
    c1 = 1.0 / (1.0 - ADAM_B1 ** ADAM_STEP)
    c2 = 1.0 / (1.0 - ADAM_B2 ** ADAM_STEP)

    def body(w_ref, g_ref, m_ref, v_ref, go_ref, d_ref, nm_ref, nv_ref):
        gv = g_ref[...].reshape(go_ref.shape)
        nm = ADAM_B1 * m_ref[...] + (1.0 - ADAM_B1) * gv
        nv = ADAM_B2 * v_ref[...] + (1.0 - ADAM_B2) * (gv * gv)
        m_hat = nm * c1
        v_hat = nv * c2
        go_ref[...] = gv
        d_ref[...] = -ADAM_LR * (m_hat / (jnp.sqrt(v_hat) + ADAM_EPS) + ADAM_WD * w_ref[...])
        nm_ref[...] = nm
        nv_ref[...] = nv

    spec = pl.BlockSpec((1,) * lead + (tr, cdim), lambda i: (0,) * lead + (i, 0))
    shp = jax.ShapeDtypeStruct(w.shape, F32)
    return _call(body, name=name, out_shape=(shp,) * 4, grid=(r // tr,),
                 in_specs=[spec, pl.BlockSpec((tr, cdim), lambda i: (i, 0)), spec, spec],
                 out_specs=(spec,) * 4, compiler_params=_params(("parallel",)))(w, g, m, v)


def _head_of_row(r, nc):
    assert nc & (nc - 1) == 0
    return lax.shift_right_logical(r, nc.bit_length() - 1)


def _chunk_mats(rows, nc, reverse):
    ri = lax.broadcasted_iota(jnp.int32, (rows, rows), 0)
    ci = lax.broadcasted_iota(jnp.int32, (rows, rows), 1)
    same = _head_of_row(ri, nc) == _head_of_row(ci, nc)
    between = jnp.where(same & ((ci > ri) if reverse else (ci < ri)), 1.0, 0.0).astype(F32)
    li = lax.broadcasted_iota(jnp.int32, (LANES, LANES), 0)
    lj = lax.broadcasted_iota(jnp.int32, (LANES, LANES), 1)
    within = jnp.where((li >= lj) if reverse else (li <= lj), 1.0, 0.0).astype(F32)
    return between, within


def _dot_hi(a, b):
    return jnp.dot(a, b, preferred_element_type=F32, precision=lax.Precision.HIGHEST)


def _scan_rows(t, nc, reverse):
    between, within = _chunk_mats(t.shape[0], nc, reverse)
    inner = _dot_hi(t, within)
    tot = jnp.sum(t, axis=1, keepdims=True)
    return inner + _dot_hi(between, jnp.broadcast_to(tot, t.shape))


def _log_forget_cumsum(f_rows, bias_rows, nc):
    def body(f_ref, b_ref, cum_ref):
        z = f_ref[...] + b_ref[...]
        lf = jnp.minimum(z, 0.0) - jnp.log(1.0 + jnp.exp(-jnp.abs(z)))
        cum_ref[...] = _scan_rows(lf, nc, False)

    return _call(body, name="forget_cumsum", out_shape=jax.ShapeDtypeStruct(f_rows.shape, F32),
                 compiler_params=_params())(f_rows, bias_rows)


def _log_forget_cumsum_bwd(dcum_rows, f_rows, bias_rows, nc):
    rows = f_rows.shape[0]

    def body(d_ref, f_ref, b_ref, df_ref, db_ref):
        dlf = _scan_rows(d_ref[...], nc, True)
        z = f_ref[...] + b_ref[...]
        df = dlf * _sigmoid(-z)
        df_ref[...] = df
        hi = lax.broadcasted_iota(jnp.int32, (FOX_HEADS, rows), 0)
        ri = lax.broadcasted_iota(jnp.int32, (FOX_HEADS, rows), 1)
        sel = jnp.where(_head_of_row(ri, nc) == hi, 1.0, 0.0).astype(F32)
        db_ref[...] = jnp.sum(_dot_hi(sel, df), axis=1, keepdims=True)

    return _call(body, name="forget_cumsum_bwd",
                 out_shape=(jax.ShapeDtypeStruct(f_rows.shape, F32), jax.ShapeDtypeStruct((FOX_HEADS, 1), F32)),
                 compiler_params=_params())(dcum_rows, f_rows, bias_rows)


def _rms_hat(xv):
    rstd = lax.rsqrt(jnp.mean(xv * xv, axis=-1, keepdims=True) + RMS_EPS)
    return xv * rstd, rstd


def _modulated(x_ref, g_ref, sc_ref, sh_ref):
    xhat, _ = _rms_hat(x_ref[...])
    return ((xhat * g_ref[...]) * sc_ref[...] + sh_ref[...]).astype(BF)


def _forget_logits(x, g_pre, scale1p, shift, w_f, tm):
    s = x.shape[0]

    def body(x_ref, g_ref, sc_ref, sh_ref, w_ref, f_ref):
        f_ref[...] = _dot(_modulated(x_ref, g_ref, sc_ref, sh_ref), w_ref[...])

    vec = _full((1, D_MODEL))
    return _call(
        body, name="forget_logits", out_shape=jax.ShapeDtypeStruct((s, LANES), F32), grid=(s // tm,),
        in_specs=[pl.BlockSpec((tm, D_MODEL), lambda i: (i, 0)), vec, vec, vec, _full((D_MODEL, LANES))],
        out_specs=pl.BlockSpec((tm, LANES), lambda i: (i, 0)), compiler_params=_params(("parallel",)),
    )(x, g_pre, scale1p, shift, w_f)


def _split3(v):
    hi = v.astype(BF).astype(F32)
    mid = (v - hi).astype(BF).astype(F32)
    lo = ((v - hi) - mid).astype(BF).astype(F32)
    return hi, mid, lo


def _in_proj(x, g_pre, scale1p, shift, w_rows, w_t_fox, cum, cos_t, sin_t, w_out_halves, tm):
    s = x.shape[0]
    r_va, r_za, r_qb, r_zb, r_kb, r_vb = 0, 512, 1024, 1536, 2048, 2176

    def body(x_ref, g_ref, sc_ref, sh_ref, w_ref, wt_ref, cum_ref, cos_ref, sin_ref, wo_ref,
             h_ref, qat_ref, ka_ref, kat_ref, v_ref, vt_ref, za_ref, zb_ref, qb_ref, kb_ref, vb_ref,
             qbt_ref, kbt_ref, vbt_ref, mo_ref, wo_all_ref, send_sems, recv_sems, local_sem):
        start_gather, finish_gather = _chip_gather(wo_ref, wo_all_ref, send_sems, recv_sems, local_sem)

        @pl.when(pl.program_id(0) == 0)
        def _():
            start_gather()

        @pl.when(pl.program_id(0) == s // tm - 1)
        def _():
            finish_gather()

        hb = _modulated(x_ref, g_ref, sc_ref, sh_ref)
        h_ref[...] = hb

        def sec(c0, width):
            return _dot(hb, w_ref[:, c0:c0 + width])

        def sec_t(r0):
            return _dot_nt(wt_ref[r0:r0 + FOX_W, :], hb)

        q_t = sec_t(0) * Q_SCALE
        k_t = sec_t(FOX_W)
        v_t = sec_t(2 * FOX_W)
        va = sec(r_va, FOX_W)
        zeros = jnp.zeros((AUG_DIM - HEAD_DIM - AUG_ROWS, tm), F32)
        ri = lax.broadcasted_iota(jnp.int32, (AUG_ROWS, tm), 0)
        const = jnp.where(ri == AUG_ROWS - 1, 0.0, 1.0)
        ri_v = lax.broadcasted_iota(jnp.int32, (VT_ROWS - HEAD_DIM, tm), 0)
        v_feat = jnp.where(ri_v == 0, 1.0, 0.0).astype(BF)
        for hd in range(FOX_HEADS):
            rows = slice(hd * HEAD_DIM, (hd + 1) * HEAD_DIM)
            cum2 = cum_ref[hd:hd + 1, :] * LOG2E
            hi, mid, lo = (jnp.broadcast_to(part, (AUG_ROWS, tm)) for part in _split3(cum2))
            q_feat = jnp.where(ri == 1, hi, jnp.where(ri == 2, mid, jnp.where(ri == 3, lo, const)))
            k_feat = jnp.where(ri == 4, -hi, jnp.where(ri == 5, -mid, jnp.where(ri == 6, -lo, const)))
            q_aug = jnp.concatenate([q_t[rows], q_feat, zeros], axis=0)
            k_aug = jnp.concatenate([k_t[rows], k_feat, zeros], axis=0)
            mo_ref[hd:hd + 1, :] = jnp.sum(q_t[rows] * k_t[rows], axis=0, keepdims=True) + 1.0
            qat_ref[hd] = q_aug.astype(BF)
            kat_ref[hd] = k_aug.astype(BF)
            ka_ref[hd] = k_aug.T.astype(BF)
            vt_ref[hd] = jnp.concatenate([v_t[rows].astype(BF), v_feat], axis=0)
            v_ref[hd] = va[:, rows].astype(BF)
        za_ref[...] = sec(r_za, FOX_W)
        zb_ref[...] = sec(r_zb, SWA_W)
        cos2, sin2 = cos_ref[...], sin_ref[...]
        cos8 = jnp.concatenate([cos2] * 4, axis=1)
        sin8 = jnp.concatenate([sin2] * 4, axis=1)
        qb = sec(r_qb, SWA_W)
        qb = (qb * cos8 + _rope_partner(qb) * sin8) * (HEAD_DIM ** -0.5)
        qb_ref[...] = qb.astype(BF)
        for a in range(SWA_W // LANES):
            qbt_ref[a * LANES:(a + 1) * LANES, :] = qb[:, a * LANES:(a + 1) * LANES].T.astype(BF)
        kb = sec(r_kb, SWA_KV_W)
        kb = kb * cos2 + _rope_partner(kb) * sin2
        vb = sec(r_vb, SWA_KV_W)
        kb_t, vb_t = kb.T, vb.T
        for hd in range(SWA_KV_HEADS):
            sl = slice(hd * HEAD_DIM, (hd + 1) * HEAD_DIM)
            kb_ref[hd] = kb[:, sl].astype(BF)
            vb_ref[hd] = vb[:, sl].astype(BF)
            kbt_ref[hd] = kb_t[sl].astype(BF)
            vbt_ref[hd] = jnp.concatenate([vb_t[sl].astype(BF), v_feat], axis=0)

    row = lambda w: pl.BlockSpec((tm, w), lambda i: (i, 0))
    heads = lambda n, w=HEAD_DIM: pl.BlockSpec((n, tm, w), lambda i: (0, i, 0))
    heads_t = lambda w: pl.BlockSpec((FOX_HEADS, w, tm), lambda i: (0, 0, i))
    vec = _full((1, D_MODEL))
    hs = lambda a, b: jax.ShapeDtypeStruct((FOX_HEADS, a, b), BF)
    out_shape = (
        jax.ShapeDtypeStruct((s, D_MODEL), BF),
        hs(AUG_DIM, s), hs(s, AUG_DIM), hs(AUG_DIM, s), hs(s, HEAD_DIM), hs(VT_ROWS, s),
        jax.ShapeDtypeStruct((s, FOX_W), F32), jax.ShapeDtypeStruct((s, SWA_W), F32),
        jax.ShapeDtypeStruct((s, SWA_W), BF),
        jax.ShapeDtypeStruct((SWA_KV_HEADS, s, HEAD_DIM), BF), jax.ShapeDtypeStruct((SWA_KV_HEADS, s, HEAD_DIM), BF),
        jax.ShapeDtypeStruct((SWA_W, s), BF),
        jax.ShapeDtypeStruct((SWA_KV_HEADS, HEAD_DIM, s), BF), jax.ShapeDtypeStruct((SWA_KV_HEADS, VT_ROWS, s), BF),
        jax.ShapeDtypeStruct((FOX_HEADS, s), F32),
        jax.ShapeDtypeStruct((N_CHIPS,) + w_out_halves.shape, w_out_halves.dtype),
    )
    kv_t = lambda w: pl.BlockSpec((SWA_KV_HEADS, w, tm), lambda i: (0, 0, i))
    hbm = pl.BlockSpec(memory_space=pl.ANY)
    return _call(
        body, name="in_proj", out_shape=out_shape, grid=(s // tm,),
        in_specs=[row(D_MODEL), vec, vec, vec, _full(w_rows.shape), _full(w_t_fox.shape),
                  pl.BlockSpec((FOX_HEADS, tm), lambda i: (0, i)), row(LANES), row(LANES), hbm],
        out_specs=(row(D_MODEL), heads_t(AUG_DIM), heads(FOX_HEADS, AUG_DIM), heads_t(AUG_DIM), heads(FOX_HEADS),
                   heads_t(VT_ROWS), row(FOX_W), row(SWA_W), row(SWA_W), heads(SWA_KV_HEADS), heads(SWA_KV_HEADS),
                   pl.BlockSpec((SWA_W, tm), lambda i: (0, i)), kv_t(HEAD_DIM), kv_t(VT_ROWS),
                   pl.BlockSpec((FOX_HEADS, tm), lambda i: (0, i)), hbm),
        scratch_shapes=list(CHIP_GATHER_SEMS),
        compiler_params=_params(("arbitrary",), has_side_effects=True),
    )(x, g_pre, scale1p, shift, w_rows, w_t_fox, cum, cos_t, sin_t, w_out_halves)


def _diag_chunks(d, bq, bk, chunk):
    out = []
    for c0 in range(0, bq, chunk):
        if d is None or d * bk + bk - 1 <= c0:
            out.append((c0, None, bk))
        elif d * bk <= c0 + chunk - 1:
            n_keys = min(bk, c0 + chunk - d * bk)
            kpos = d * bk + lax.broadcasted_iota(jnp.int32, (n_keys, chunk), 0)
            qpos = c0 + lax.broadcasted_iota(jnp.int32, (n_keys, chunk), 1)
            out.append((c0, kpos <= qpos, n_keys))
    return out


def _fox_fwd(qat, ka, vt, m_own, bq, bk, chunk, running_max):
    nh, _, s = qat.shape
    r = bq // bk

    pairs = [(i, j) for i in range(s // bq) for j in range(i * r + r)]

    def body(i_tab, j_tab, ka_ref, qat_ref, vt_ref, mo_ref, o_ref, lse_ref, bad_ref, *rest):
        pt_ref, m_scr, acc_scr = (None,) * running_max + rest
        i, j = i_tab[pl.program_id(1)], j_tab[pl.program_id(1)]

        @pl.when(j == 0)
        def _():
            m_scr[...] = jnp.full(m_scr.shape, NEG, F32) if running_max else mo_ref[0]
            acc_scr[...] = jnp.zeros(acc_scr.shape, F32)

        def careful(d):
            kv, vtv = ka_ref[0], vt_ref[0]

            def one_chunk(n, carry):
                c0 = pl.multiple_of(n * chunk, chunk)
                cs = pl.ds(c0, chunk)
                sc = _dot(kv, qat_ref[0, :, cs])
                if d is not None:
                    kpos = d * bk + lax.broadcasted_iota(jnp.int32, (bk, chunk), 0)
                    qpos = c0 + lax.broadcasted_iota(jnp.int32, (bk, chunk), 1)
                    sc = jnp.where(kpos <= qpos, sc, NEG)
                m_prev = m_scr[:, cs]
                m_new = jnp.maximum(m_prev, jnp.max(sc, axis=0, keepdims=True))
                p = jnp.exp2(sc - m_new).astype(BF)
                acc_scr[:, cs] = jnp.exp2(m_prev - m_new) * acc_scr[:, cs] + _dot(vtv, p)
                m_scr[:, cs] = m_new
                return carry

            lax.fori_loop(0, bq // chunk, one_chunk, 0)

        def fast(d):
            todo = _diag_chunks(d, bq, bk, chunk)
            scores = lambda t: _dot(ka_ref[0, :t[2], :], qat_ref[0, :, t[0]:t[0] + chunk])
            sc_next = scores(todo[0])
            for n, (c0, mask, n_keys) in enumerate(todo):
                cs = slice(c0, c0 + chunk)
                sc = sc_next
                if n + 1 < len(todo):
                    sc_next = scores(todo[n + 1])
                if mask is not None:
                    sc = jnp.where(mask, sc, NEG)
                p = jnp.exp2(sc - m_scr[:, cs]).astype(BF)
                pt_ref[0, :n_keys, cs] = p
                acc_scr[:, cs] += _dot(vt_ref[0, :, :n_keys], p)

        step = careful if running_max else fast

        @pl.when(j < i * r)
        def _():
            step(None)

        for d in range(r):
            @pl.when(j == i * r + d)
            def _(d=d):
                step(d)

        @pl.when(j == i * r + r - 1)
        def _():
            l = acc_scr[HEAD_DIM:HEAD_DIM + 1, :]
            o_ref[0] = acc_scr[:HEAD_DIM, :] / l
            lse_ref[0] = m_scr[...] + jnp.log2(l)
            bad_ref[0] = jnp.where(l < OVERFLOW_GUARD, 0.0, 1.0)

    qmap_t = lambda h, t, it, jt: (h, 0, it[t])
    qrow = pl.BlockSpec((1, 1, bq), qmap_t)
    row_shape = jax.ShapeDtypeStruct((nh, 1, s), F32)
    out_shape = (jax.ShapeDtypeStruct((nh, HEAD_DIM, s), F32), row_shape, row_shape)
    out_specs = (pl.BlockSpec((1, HEAD_DIM, bq), qmap_t), qrow, qrow)
    if not running_max:
        out_shape += (jax.ShapeDtypeStruct((nh, s, s), BF),)
        out_specs += (pl.BlockSpec((1, bk, bq), lambda h, t, it, jt: (h, jt[t], it[t])),)
    grid_spec = pltpu.PrefetchScalarGridSpec(
        num_scalar_prefetch=2, grid=(nh, len(pairs)),
        in_specs=[pl.BlockSpec((1, bk, AUG_DIM), lambda h, t, it, jt: (h, jt[t], 0)), pl.BlockSpec((1, AUG_DIM, bq), qmap_t),
                  pl.BlockSpec((1, VT_ROWS, bk), lambda h, t, it, jt: (h, 0, jt[t])), qrow],
        out_specs=out_specs,
        scratch_shapes=[pltpu.VMEM((1, bq), F32), pltpu.VMEM((VT_ROWS, bq), F32)])
    return _call(
        body, name="fox_fwd_running_max" if running_max else "fox_fwd", out_shape=out_shape, grid_spec=grid_spec,
        compiler_params=_params(("parallel", "arbitrary")),
    )(jnp.asarray([p[0] for p in pairs], jnp.int32), jnp.asarray([p[1] for p in pairs], jnp.int32), ka, qat, vt, m_own)


def _fox_bwd(qat, ka, kat, v, dot_, lse, delta, pt, bq, bk, chunk, dq_blk):
    nh, _, s = qat.shape
    r = bq // bk
    nq = s // bq
    stored = pt is not None

    pairs = [(j, i) for j in range(s // bk) for i in range(j // r, nq)]

    def body(j_tab, i_tab, a_ref, b_ref, kat_ref, v_ref, qat_ref, do_ref, dl_ref, dq_ref, dk_ref, dv_ref, dk_scr, dv_scr):
        ka_ref, lse_ref, pt_ref = (None, None, a_ref) if stored else (a_ref, b_ref, None)
        j, i = j_tab[pl.program_id(1)], i_tab[pl.program_id(1)]

        @pl.when(pl.program_id(1) == 0)
        def _():
            dq_ref[...] = jnp.zeros(dq_ref.shape, F32)

        @pl.when(i * r <= j)
        def _():
            dk_scr[...] = jnp.zeros(dk_scr.shape, F32)
            dv_scr[...] = jnp.zeros(dv_scr.shape, F32)

        def step(d):
            todo = _diag_chunks(d, bq, bk, chunk)

            def products(t):
                cs = slice(t[0], t[0] + chunk)
                return (None if stored else _dot(ka_ref[0, :t[2], :], qat_ref[0, :, cs]),
                        _dot(v_ref[0, :t[2], :], do_ref[0, :, cs]))

            nxt = products(todo[0])
            for n, (c0, mask, n_keys) in enumerate(todo):
                cs = slice(c0, c0 + chunk)
                sc, dp = nxt
                if n + 1 < len(todo):
                    nxt = products(todo[n + 1])
                if stored:
                    p_bf = pt_ref[0, :n_keys, cs]
                    p = p_bf.astype(F32)
                else:
                    p = jnp.exp2(sc - lse_ref[0, :, cs])
                    if mask is not None:
                        p = jnp.where(mask, p, 0.0)
                    p_bf = p.astype(BF)
                ds = (p * (dp - dl_ref[0, :, cs])).astype(BF)
                dv_scr[:, :n_keys] += _dot_nt(do_ref[0, :, cs], p_bf)
                dk_scr[:, :n_keys] += _dot_nt(qat_ref[0, :VT_ROWS, cs], ds)
                c1 = c0 % dq_blk
                dq_ref[0, i * (bq // dq_blk) + c0 // dq_blk, :, c1:c1 + chunk] += _dot(kat_ref[0, :VT_ROWS, :n_keys], ds)

        @pl.when(i * r > j)
        def _():
            step(None)

        for d in range(r):
            @pl.when(j == i * r + d)
            def _(d=d):
                step(d)

        @pl.when(i == nq - 1)
        def _():
            dk_ref[0] = dk_scr[...]
            dv_ref[0] = dv_scr[...]

    qmap = lambda h, t, jt, it: (h, 0, it[t])
    kmap = lambda h, t, jt, it: (h, jt[t], 0)
    kmap_t = lambda h, t, jt, it: (h, 0, jt[t])
    if stored:
        first = [(pt, pl.BlockSpec((1, bk, bq), lambda h, t, jt, it: (h, jt[t], it[t]))),
                 (delta, pl.BlockSpec((1, 1, bq), qmap))]
    else:
        first = [(ka, pl.BlockSpec((1, bk, AUG_DIM), kmap)), (lse, pl.BlockSpec((1, 1, bq), qmap))]
    grid_spec = pltpu.PrefetchScalarGridSpec(
        num_scalar_prefetch=2, grid=(nh, len(pairs)),
        in_specs=[first[0][1], first[1][1], pl.BlockSpec((1, AUG_DIM, bk), kmap_t), pl.BlockSpec((1, bk, HEAD_DIM), kmap),
                  pl.BlockSpec((1, AUG_DIM, bq), qmap), pl.BlockSpec((1, HEAD_DIM, bq), qmap),
                  pl.BlockSpec((1, 1, bq), qmap)],
        out_specs=(pl.BlockSpec((1, s // dq_blk, VT_ROWS, dq_blk), lambda h, t, jt, it: (h, 0, 0, 0)),
                   pl.BlockSpec((1, VT_ROWS, bk), kmap_t), pl.BlockSpec((1, HEAD_DIM, bk), kmap_t)),
        scratch_shapes=[pltpu.VMEM((VT_ROWS, bk), F32), pltpu.VMEM((HEAD_DIM, bk), F32)])
    return _call(
        body, name="fox_bwd" if stored else "fox_bwd_recompute",
        out_shape=(jax.ShapeDtypeStruct((nh, s // dq_blk, VT_ROWS, dq_blk), F32),
                   jax.ShapeDtypeStruct((nh, VT_ROWS, s), F32), jax.ShapeDtypeStruct((nh, HEAD_DIM, s), F32)),
        grid_spec=grid_spec, compiler_params=_params(("parallel", "arbitrary")),
    )(jnp.asarray([p[0] for p in pairs], jnp.int32), jnp.asarray([p[1] for p in pairs], jnp.int32),
      first[0][0], first[1][0], kat, v, qat, dot_, delta)


def _swa_mask(i, tq):
    kpos = i * tq - WINDOW + lax.broadcasted_iota(jnp.int32, (tq + WINDOW, tq), 0)
    qpos = i * tq + lax.broadcasted_iota(jnp.int32, (tq + WINDOW, tq), 1)
    rel = qpos - kpos
    return (rel >= 0) & (rel < WINDOW) & (kpos >= 0)


def _swa_rows(ref, g, i, tq):
    before = pl.multiple_of(jnp.maximum(i * tq - WINDOW, 0), WINDOW)
    return jnp.concatenate([ref[g, pl.ds(before, WINDOW), :], ref[g, pl.ds(pl.multiple_of(i * tq, tq), tq), :]], axis=0)


def _swa_before(n_rows, tq):
    return pl.BlockSpec((SWA_KV_HEADS, n_rows, WINDOW), lambda i: (0, 0, jnp.maximum(i * (tq // WINDOW) - 1, 0)))


def _swa_probs_t(sc, mask, sink):
    sc = jnp.where(mask, sc, NEG)
    m = jnp.maximum(jnp.max(sc, axis=0, keepdims=True), sink)
    p = jnp.exp(sc - m)
    e_sink = jnp.exp(sink - m)
    inv_l = 1.0 / (jnp.sum(p, axis=0, keepdims=True) + e_sink)
    return p * inv_l, e_sink * inv_l


def _swa_fwd(qbt, kb, vbt, sinks, tq):
    s = qbt.shape[1]
    n_heads = SWA_KV_HEADS * SWA_GROUP

    def body(q_ref, k_ref, vb_ref, vc_ref, s_ref, o_ref):
        i = pl.program_id(0)
        mask = _swa_mask(i, tq)
        kw = [_swa_rows(k_ref, g, i, tq) for g in range(SWA_KV_HEADS)]
        vtw = [jnp.concatenate([vb_ref[g], vc_ref[g]], axis=1) for g in range(SWA_KV_HEADS)]
        scores = lambda hd: _dot(kw[hd // SWA_GROUP], q_ref[hd * HEAD_DIM:(hd + 1) * HEAD_DIM, :])
        sc_next = scores(0)
        for hd in range(n_heads):
            g, hh = divmod(hd, SWA_GROUP)
            rows = slice(hd * HEAD_DIM, (hd + 1) * HEAD_DIM)
            sink = s_ref[g][:, hh:hh + 1]
            sc = jnp.where(mask, sc_next, NEG)
            if hd + 1 < n_heads:
                sc_next = scores(hd + 1)
            m = jnp.maximum(jnp.max(sc, axis=0, keepdims=True), sink)
            acc = _dot(vtw[g], jnp.exp(sc - m).astype(BF))
            o_ref[rows, :] = acc[:HEAD_DIM] / (acc[HEAD_DIM:HEAD_DIM + 1] + jnp.exp(sink - m))

    kvspec = _full((SWA_KV_HEADS, s, HEAD_DIM))
    qspec = pl.BlockSpec((SWA_W, tq), lambda i: (0, i))
    return _call(
        body, name="swa_fwd", out_shape=jax.ShapeDtypeStruct((SWA_W, s), F32), grid=(s // tq,),
        in_specs=[qspec, kvspec, _swa_before(VT_ROWS, tq), pl.BlockSpec((SWA_KV_HEADS, VT_ROWS, tq), lambda i: (0, 0, i)),
                  _full((SWA_KV_HEADS, 1, SWA_GROUP))],
        out_specs=qspec, compiler_params=_params(("parallel",)),
    )(qbt, kb, vbt, vbt, sinks)


def _swa_bwd(qb, qbt, kb, kbt, vb, sinks, dob, dobt, tq):
    s = qb.shape[0]
    n_heads = SWA_KV_HEADS * SWA_GROUP

    def body(q_ref, qt_ref, k_ref, ktb_ref, ktc_ref, v_ref, s_ref, do_ref, dot_ref, dq_ref, dk_ref, dv_ref, ds_ref):
        i = pl.program_id(0)

        @pl.when(i == 0)
        def _():
            dk_ref[...] = jnp.zeros(dk_ref.shape, F32)
            dv_ref[...] = jnp.zeros(dv_ref.shape, F32)
            ds_ref[...] = jnp.zeros(ds_ref.shape, F32)

        mask = _swa_mask(i, tq)
        kw = [_swa_rows(k_ref, g, i, tq) for g in range(SWA_KV_HEADS)]
        vw = [_swa_rows(v_ref, g, i, tq) for g in range(SWA_KV_HEADS)]
        ktw = [jnp.concatenate([ktb_ref[g], ktc_ref[g]], axis=1) for g in range(SWA_KV_HEADS)]
        before = pl.ds(pl.multiple_of(jnp.maximum(i * tq - WINDOW, 0), WINDOW), WINDOW)
        own = pl.ds(pl.multiple_of(i * tq, tq), tq)

        def products(hd):
            rows = slice(hd * HEAD_DIM, (hd + 1) * HEAD_DIM)
            return _dot(kw[hd // SWA_GROUP], qt_ref[rows, :]), _dot(vw[hd // SWA_GROUP], dot_ref[rows, :])

        nxt = products(0)
        for g in range(SWA_KV_HEADS):
            dsinks = []
            dk_acc = jnp.zeros((tq + WINDOW, HEAD_DIM), F32)
            dv_acc = jnp.zeros((tq + WINDOW, HEAD_DIM), F32)
            for hh in range(SWA_GROUP):
                hd = g * SWA_GROUP + hh
                rows = slice(hd * HEAD_DIM, (hd + 1) * HEAD_DIM)
                sc, dp = nxt
                if hd + 1 < n_heads:
                    nxt = products(hd + 1)
                p, p_sink = _swa_probs_t(sc, mask, s_ref[g][:, hh:hh + 1])
                delta = jnp.sum(p * dp, axis=0, keepdims=True)
                dsc = (p * (dp - delta)).astype(BF)
                dq_ref[rows, :] = _dot(ktw[g], dsc)
                dk_acc = dk_acc + _dot(dsc, q_ref[:, rows])
                dv_acc = dv_acc + _dot(p.astype(BF), do_ref[:, rows])
                dsinks.append(-jnp.sum(p_sink * delta, axis=1, keepdims=True))
            dk_ref[g, before, :] += dk_acc[:WINDOW]
            dk_ref[g, own, :] += dk_acc[WINDOW:]
            dv_ref[g, before, :] += dv_acc[:WINDOW]
            dv_ref[g, own, :] += dv_acc[WINDOW:]
            ds_ref[g] += jnp.concatenate(dsinks, axis=1)

    kvspec = _full((SWA_KV_HEADS, s, HEAD_DIM))
    qspec = pl.BlockSpec((tq, SWA_W), lambda i: (i, 0))
    qspec_t = pl.BlockSpec((SWA_W, tq), lambda i: (0, i))
    sspec = _full((SWA_KV_HEADS, 1, SWA_GROUP))
    kvshape = jax.ShapeDtypeStruct((SWA_KV_HEADS, s, HEAD_DIM), F32)
    return _call(
        body, name="swa_bwd",
        out_shape=(jax.ShapeDtypeStruct((SWA_W, s), F32), kvshape, kvshape,
                   jax.ShapeDtypeStruct((SWA_KV_HEADS, 1, SWA_GROUP), F32)),
        grid=(s // tq,),
        in_specs=[qspec, qspec_t, kvspec, _swa_before(HEAD_DIM, tq),
                  pl.BlockSpec((SWA_KV_HEADS, HEAD_DIM, tq), lambda i: (0, 0, i)), kvspec, sspec, qspec, qspec_t],
        out_specs=(qspec_t, kvspec, kvspec, sspec),
        compiler_params=_params(("arbitrary",)),
    )(qb, qbt, kb, kbt, kbt, vb, sinks, dob, dobt)


def _pairs_to_rows(ref, n_rows=HEAD_DIM):
    parts = []
    for a in range(0, FOX_HEADS, 2):
        parts.append(jnp.concatenate([ref[a][:n_rows], ref[a + 1][:n_rows]], axis=0).T)
    return jnp.concatenate(parts, axis=1)


def _blocks_to_rows(ref):
    return jnp.concatenate([ref[a:a + LANES, :].T for a in range(0, ref.shape[0], LANES)], axis=1)


def _out_proj(oat, za, obt, zb, x, tgt, w_out, w_out_t, gate, g_post, inv_l, tm):
    s = x.shape[0]

    def body(oat_ref, za_ref, obt_ref, zb_ref, x_ref, t_ref, w_ref, wt_ref, gate_ref, gp_ref, il_ref,
             dout_ref, doat_ref, dla_ref, dza_ref, dob_ref, dobt_ref, dzb_ref, gw_ref, dgate_ref, dgp_ref, loss_ref):
        i = pl.program_id(0)

        @pl.when(i == 0)
        def _():
            gw_ref[...] = jnp.zeros(gw_ref.shape, F32)
            dgate_ref[...] = jnp.zeros(dgate_ref.shape, F32)
            dgp_ref[...] = jnp.zeros(dgp_ref.shape, F32)
            loss_ref[...] = jnp.zeros(loss_ref.shape, F32)

        oa_v = _pairs_to_rows(oat_ref)
        ob_v = _blocks_to_rows(obt_ref)
        za_v, zb_v = za_ref[...], zb_ref[...]
        sga, sgb = _sigmoid(za_v), _sigmoid(zb_v)
        sila, silb = za_v * sga, zb_v * sgb
        u = jnp.concatenate([oa_v * sila, ob_v * silb], axis=1).astype(BF)
        yv = _dot(u, w_ref[...])
        yhat, rstd = _rms_hat(yv)
        gp, gate_v = gp_ref[...], gate_ref[...]
        nrm = yhat * gp
        diff = (x_ref[...] + gate_v * nrm) - t_ref[...]
        loss_ref[...] += 0.5 * jnp.sum(jnp.sum(diff * diff, axis=1, keepdims=True), axis=0, keepdims=True) / D_MODEL
        dout = diff * (1.0 / D_MODEL)
        dout_ref[...] = dout
        dgate_ref[...] += jnp.sum(dout * nrm, axis=0, keepdims=True)
        dn = dout * gate_v
        dgp_ref[...] += jnp.sum(dn * yhat, axis=0, keepdims=True)
        dyhat = dn * gp
        dy = (rstd * (dyhat - yhat * jnp.mean(dyhat * yhat, axis=1, keepdims=True))).astype(BF)
        gw_ref[...] += _dot_tn(u, dy)
        du = _dot(dy, wt_ref[...])
        dua, dub = du[:, :FOX_W], du[:, FOX_W:]
        doa = dua * sila
        for a in range(0, FOX_HEADS, 2):
            pair_t = doa[:, a * HEAD_DIM:(a + 2) * HEAD_DIM].T
            for hd, rows in ((a, slice(0, HEAD_DIM)), (a + 1, slice(HEAD_DIM, 2 * HEAD_DIM))):
                inv_l = il_ref[hd]
                doat_ref[hd] = (pair_t[rows] * inv_l).astype(BF)
                dla_ref[hd] = jnp.sum(pair_t[rows] * oat_ref[hd], axis=0, keepdims=True) * inv_l
        dob = dub * silb
        dob_ref[...] = dob.astype(BF)
        for a in range(0, SWA_W, LANES):
            dobt_ref[a:a + LANES, :] = dob[:, a:a + LANES].T.astype(BF)
        dza_ref[...] = (dua * oa_v * (sga * (1.0 + za_v * (1.0 - sga)))).astype(BF)
        dzb_ref[...] = (dub * ob_v * (sgb * (1.0 + zb_v * (1.0 - sgb)))).astype(BF)

    row = lambda w: pl.BlockSpec((tm, w), lambda i: (i, 0))
    heads_t = lambda w: pl.BlockSpec((FOX_HEADS, w, tm), lambda i: (0, 0, i))
    vec = _full((1, D_MODEL))
    mat = _full((D_MODEL, D_MODEL))
    out_shape = (
        jax.ShapeDtypeStruct((s, D_MODEL), F32),
        jax.ShapeDtypeStruct((FOX_HEADS, HEAD_DIM, s), BF), jax.ShapeDtypeStruct((FOX_HEADS, 1, s), F32),
        jax.ShapeDtypeStruct((s, FOX_W), BF), jax.ShapeDtypeStruct((s, SWA_W), BF), jax.ShapeDtypeStruct((SWA_W, s), BF),
        jax.ShapeDtypeStruct((s, SWA_W), BF),
        jax.ShapeDtypeStruct((D_MODEL, D_MODEL), F32),
        jax.ShapeDtypeStruct((1, D_MODEL), F32), jax.ShapeDtypeStruct((1, D_MODEL), F32),
        jax.ShapeDtypeStruct((1, 1), F32),
    )
    col = pl.BlockSpec((SWA_W, tm), lambda i: (0, i))
    return _call(
        body, name="out_proj", out_shape=out_shape, grid=(s // tm,),
        in_specs=[heads_t(HEAD_DIM), row(FOX_W), col, row(SWA_W), row(D_MODEL), row(D_MODEL), mat, mat, vec, vec,
                  heads_t(1)],
        out_specs=(row(D_MODEL), heads_t(HEAD_DIM), heads_t(1), row(FOX_W), row(SWA_W), col, row(SWA_W), mat, vec, vec,
                   _full((1, 1))),
        compiler_params=_params(("arbitrary",)),
    )(oat, za, obt, zb, x, tgt, w_out, w_out_t, gate, g_post, inv_l)


def _assemble_dproj(dqt, dkt, dvt, dza, dqb, dzb, dkb, dvb, df, cos_t, sin_t, tm):
    s = dza.shape[0]

    def body(dqt_ref, dkt_ref, dvt_ref, dza_ref, dqb_ref, dzb_ref, dkb_ref, dvb_ref, df_ref, cos_ref, sin_ref, o_ref):
        def cat(ref, n):
            return jnp.concatenate([ref[hd] for hd in range(n)], axis=1)

        cos2, sin2 = cos_ref[...], sin_ref[...]
        cos8 = jnp.concatenate([cos2] * 4, axis=1)
        sin8 = jnp.concatenate([sin2] * 4, axis=1)
        scale = HEAD_DIM ** -0.5
        o_ref[:, C_QA:C_QA + FOX_W] = (_pairs_to_rows(dqt_ref.at[:, 0]) * scale).astype(BF)
        o_ref[:, C_KA:C_KA + FOX_W] = (_pairs_to_rows(dkt_ref) * LN2).astype(BF)
        o_ref[:, C_VA:C_VA + FOX_W] = _pairs_to_rows(dvt_ref).astype(BF)
        o_ref[:, C_ZA:C_ZA + FOX_W] = dza_ref[...]
        dq = _blocks_to_rows(dqb_ref) * scale
        o_ref[:, C_QB:C_QB + SWA_W] = (dq * cos8 - _rope_partner(dq) * sin8).astype(BF)
        o_ref[:, C_ZB:C_ZB + SWA_W] = dzb_ref[...]
        dk = cat(dkb_ref, SWA_KV_HEADS)
        o_ref[:, C_KB:C_KB + SWA_KV_W] = (dk * cos2 - _rope_partner(dk) * sin2).astype(BF)
        o_ref[:, C_VB:C_VB + SWA_KV_W] = cat(dvb_ref, SWA_KV_HEADS).astype(BF)
        o_ref[:, C_F:C_F + LANES] = df_ref[...].astype(BF)

    row = lambda w: pl.BlockSpec((tm, w), lambda i: (i, 0))
    heads = lambda n: pl.BlockSpec((n, tm, HEAD_DIM), lambda i: (0, i, 0))
    heads_t = lambda w: pl.BlockSpec((FOX_HEADS, w, tm), lambda i: (0, 0, i))
    return _call(
        body, name="assemble_dproj", out_shape=jax.ShapeDtypeStruct((s, WP), BF), grid=(s // tm,),
        in_specs=[pl.BlockSpec((FOX_HEADS, 1, VT_ROWS, tm), lambda i: (0, i, 0, 0)), heads_t(VT_ROWS), heads_t(HEAD_DIM),
                  row(FOX_W), pl.BlockSpec((SWA_W, tm), lambda i: (0, i)), row(SWA_W), heads(SWA_KV_HEADS),
                  heads(SWA_KV_HEADS), row(LANES), row(LANES), row(LANES)],
        out_specs=row(WP), compiler_params=_params(("parallel",)),
    )(dqt, dkt, dvt, dza, dqb, dzb, dkb, dvb, df, cos_t, sin_t)


def _in_proj_bwd_x(dproj, w_al_t, x, dout, g_pre, scale1p, tm, parts):
    s = x.shape[0]
    n_steps = s // tm
    masks = [(1, 0), (0, 1), (1, 1)]

    def body(dp_ref, wt_ref, x_ref, dout_ref, g_ref, sc_ref, parts_ref, gx_ref, dsh_ref, dsc_ref, dg_ref, got_ref,
             send_sems, recv_sems, local_sem):
        i = pl.program_id(0)
        cx, cy, cc = lax.axis_index("x"), lax.axis_index("y"), lax.axis_index("c")
        me = 2 * cx + cy
        own = pltpu.make_async_copy(parts_ref.at[me], got_ref.at[me], local_sem)

        def copy(k, send):
            dx, dy = masks[k]
            peer = 2 * (cx ^ dx) + (cy ^ dy)
            return pltpu.make_async_remote_copy(
                src_ref=parts_ref.at[peer if send else me], dst_ref=got_ref.at[me if send else peer],
                send_sem=send_sems.at[k], recv_sem=recv_sems.at[k], device_id=(cx ^ dx, cy ^ dy, cc), device_id_type=MESH)

        @pl.when(i == 0)
        def _():
            dsh_ref[...] = jnp.zeros(dsh_ref.shape, F32)
            dsc_ref[...] = jnp.zeros(dsc_ref.shape, F32)
            dg_ref[...] = jnp.zeros(dg_ref.shape, F32)
            own.start()
            for k in range(len(masks)):
                copy(k, True).start()

        @pl.when(i == n_steps - 1)
        def _():
            for k in range(len(masks)):
                copy(k, False).wait_recv()
            for k in range(len(masks)):
                copy(k, True).wait_send()
            own.wait()

        dh = _dot(dp_ref[...], wt_ref[...])
        xhat, rstd = _rms_hat(x_ref[...])
        g, sc = g_ref[...], sc_ref[...]
        dsh_ref[...] += jnp.sum(dh, axis=0, keepdims=True)
        dhx = dh * xhat
        dsc_ref[...] += jnp.sum(dhx * g, axis=0, keepdims=True)
        dg_ref[...] += jnp.sum(dhx * sc, axis=0, keepdims=True)
        dxhat = dh * (g * sc)
        gx_ref[...] = dout_ref[...] + rstd * (dxhat - xhat * jnp.mean(dxhat * xhat, axis=1, keepdims=True))

    row = lambda w: pl.BlockSpec((tm, w), lambda i: (i, 0))
    vec = _full((1, D_MODEL))
    vshape = jax.ShapeDtypeStruct((1, D_MODEL), F32)
    hbm = pl.BlockSpec(memory_space=pl.ANY)
    return _call(
        body, name="in_proj_bwd_x",
        out_shape=(jax.ShapeDtypeStruct((s, D_MODEL), F32), vshape, vshape, vshape,
                   jax.ShapeDtypeStruct(parts.shape, parts.dtype)),
        grid=(n_steps,),
        in_specs=[row(WP), _full((WP, D_MODEL)), row(D_MODEL), row(D_MODEL), vec, vec, hbm],
        out_specs=(row(D_MODEL), vec, vec, vec, hbm),
        scratch_shapes=[pltpu.SemaphoreType.DMA((3,)), pltpu.SemaphoreType.DMA((3,)), pltpu.SemaphoreType.DMA],
        compiler_params=_params(("arbitrary",), has_side_effects=True),
    )(dproj, w_al_t, x, dout, g_pre, scale1p, parts)


def _in_proj_bwd_w(h, dproj, tk, tn):
    s = h.shape[0]
    n_k = s // tk

    def body(h_ref, dp_ref, gw_ref, acc_scr):
        k = pl.program_id(1)

        @pl.when(k == 0)
        def _():
            acc_scr[...] = jnp.zeros(acc_scr.shape, F32)

        acc_scr[...] += _dot_tn(h_ref[...], dp_ref[...])

        @pl.when(k == n_k - 1)
        def _():
            gw_ref[...] = acc_scr[...].astype(BF)

    return _call(
        body, name="in_proj_bwd_w", out_shape=jax.ShapeDtypeStruct((D_MODEL, WP), BF), grid=(WP // tn, n_k),
        in_specs=[pl.BlockSpec((tk, D_MODEL), lambda n, k: (k, 0)), pl.BlockSpec((tk, tn), lambda n, k: (k, n))],
        out_specs=pl.BlockSpec((D_MODEL, tn), lambda n, k: (0, n)),
        scratch_shapes=[pltpu.VMEM((D_MODEL, tn), F32)],
        compiler_params=_params(("parallel", "arbitrary")),
    )(h, dproj)


def _align_w_in(w_cols):
    def part(name, width):
        return w_cols[:, _SRC[name]:_SRC[name] + width]

    fpad = jnp.pad(part("fa", FOX_HEADS), ((0, 0), (0, LANES - FOX_HEADS)))
    return jnp.concatenate([part("qa", FOX_W), part("ka", FOX_W), part("va", FOX_W), part("za", FOX_W),
                            part("qb", SWA_W), part("zb", SWA_W), part("kb", SWA_KV_W), part("vb", SWA_KV_W), fpad], axis=1)


def _unalign_w_in(g_al):
    def part(c0, width):
        return g_al[:, c0:c0 + width]

    return jnp.concatenate([part(C_QA, FOX_W), part(C_KA, FOX_W), part(C_VA, FOX_W), part(C_F, FOX_HEADS),
                            part(C_ZA, FOX_W), part(C_QB, SWA_W), part(C_KB, SWA_KV_W), part(C_VB, SWA_KV_W),
                            part(C_ZB, SWA_W)], axis=1)


def _rope_tables(positions):
    inv_freq = ROPE_THETA ** (-jnp.arange(HALF, dtype=F32) / HALF)
    ang = positions.astype(F32)[:, None] * inv_freq
    cos, sin = jnp.cos(ang), jnp.sin(ang)
    return jnp.concatenate([cos, cos, cos, cos], axis=1), jnp.concatenate([-sin, sin, -sin, sin], axis=1)


def _tiles(s):
    if s >= 4096:
        return dict(tm=512, blk=512, bq=2048, bk=2048, bk_bwd=2048, chunk=256, tq=256, tm_out=512, tk=1024, tn=1152)
    return dict(tm=128, blk=128, bq=256, bk=256, bk_bwd=256, chunk=128, tq=128, tm_out=128, tk=128, tn=1152)


def kernel(x, c, positions, w_ada, b_ada, g_pre, w_in, b_fgate, sinks, w_out, g_post, loss_target, m_w_ada, m_b_ada, m_g_pre, m_w_in, m_b_fgate, m_sinks, m_w_out, m_g_post, v_w_ada, v_b_ada, v_g_pre, v_w_in, v_b_fgate, v_sinks, v_w_out, v_g_post):
    s = x.shape[1]
    t = _tiles(s)
    nc = s // LANES
    rows = FOX_HEADS * nc
    me = 4 * lax.axis_index("x") + 2 * lax.axis_index("y") + lax.axis_index("c")
    chip = 2 * lax.axis_index("x") + lax.axis_index("y")
    core = lax.axis_index("c")
    x2, tgt = x[0], loss_target[0]

    a_all, mod_all = _ada_exchange(c, w_ada[0])
    mod_rows = lax.dynamic_index_in_dim(mod_all, me, axis=1, keepdims=False)
    mod = mod_rows.reshape(N_CHIPS, 2, W_ADA_SHARD)[:, 0, :].reshape(1, 3 * D_MODEL) + b_ada
    shift, scale1p, gate = mod[:, :D_MODEL], 1.0 + mod[:, D_MODEL:2 * D_MODEL], mod[:, 2 * D_MODEL:]

    w_in_pad = jnp.pad(w_in[0].astype(BF), ((0, 0), (0, W_IN_SHARD_PAD - W_IN_SHARD)))
    w_all = _allgather_chips(w_in_pad.reshape(2, D_MODEL // 2, -1), "gather_w_in").reshape(N_CHIPS, D_MODEL, -1)
    w_cols = jnp.concatenate([w_all[k, :, :W_IN_SHARD] for k in range(N_CHIPS)], axis=1)
    w_al = _align_w_in(w_cols)
    w_al_t = w_al.T

    cos_t, sin_t = _rope_tables(positions[0])

    f_pad = _forget_logits(x2, g_pre, scale1p, shift, w_al[:, C_F:], t["tk"])
    f_rows = f_pad[:, :FOX_HEADS].T.reshape(rows, LANES)
    bias_rows = jnp.repeat(b_fgate[0], nc)[:, None]
    cum = _log_forget_cumsum(f_rows, bias_rows, nc).reshape(FOX_HEADS, s)
    h, qat, ka, kat, va, vat, za, zb, qb, kb, vb, qbt, kbt, vbt, m_own, w_out_all = _in_proj(
        x2, g_pre, scale1p, shift, w_al[:, C_VA:C_F], w_al_t[:C_ZA], cum, cos_t, sin_t,
        w_out[0].astype(BF).reshape(2, W_OUT_SHARD // 2, D_MODEL), t["tm"])
    w_out_all = w_out_all.reshape(D_MODEL, D_MODEL)
    w_out_t = w_out_all.T
    m_own = m_own[:, None, :]
    fox_args = (qat, ka, vat, m_own, t["bq"], t["bk"], t["chunk"])
    oat, lse, bad, pt = _fox_fwd(*fox_args, running_max=False)
    overflowed = jnp.max(bad) > 0.0
    oat, lse = lax.cond(overflowed, lambda: _fox_fwd(*fox_args, running_max=True)[:2], lambda: (oat, lse))
    inv_l = jnp.where(overflowed, 1.0, jnp.exp2(m_own - lse))
    sinks_g = sinks.reshape(SWA_KV_HEADS, 1, SWA_GROUP)
    obt = _swa_fwd(qbt, kb, vbt, sinks_g, t["tq"])

    dout, doat, delta_a, dza, dob, dobt, dzb, gw_out, dgate, dg_post, loss_part = _out_proj(
        oat, za, obt, zb, x2, tgt, w_out_all, w_out_t, gate, g_post, inv_l, t["tm_out"])

    bwd_args = (qat, ka, kat, va, doat, lse, delta_a)
    bwd_tiles = (t["bq"], t["bk_bwd"], t["chunk"], t["blk"])
    dqt, dkt, dvt = lax.cond(overflowed, lambda: _fox_bwd(*bwd_args, None, *bwd_tiles),
                             lambda: _fox_bwd(*bwd_args, pt, *bwd_tiles))
    dcum = dqt[:, :, HEAD_DIM, :].reshape(FOX_HEADS, s) - dkt[:, HEAD_DIM, :]
    df_rows, db_heads = _log_forget_cumsum_bwd(dcum.reshape(rows, LANES), f_rows, bias_rows, nc)
    df_pad = jnp.pad(df_rows.reshape(FOX_HEADS, s).T, ((0, 0), (0, LANES - FOX_HEADS)))
    dqb, dkb, dvb, dsinks = _swa_bwd(qb, qbt, kb, kbt, vb, sinks_g, dob, dobt, t["tq"])

    dproj = _assemble_dproj(dqt, dkt, dvt, dza, dqb, dzb, dkb, dvb, df_pad, cos_t, sin_t, t["blk"])
    gw_in = _unalign_w_in(_in_proj_bwd_w(h, dproj, t["tk"], t["tn"]))

    gin = jnp.stack([jnp.pad(gw_in[:, k * W_IN_SHARD:(k + 1) * W_IN_SHARD], ((0, 0), (0, W_IN_SHARD_PAD - W_IN_SHARD)))
                     for k in range(N_CHIPS)])
    gout = gw_out.astype(BF).reshape(N_CHIPS, D_MODEL, W_OUT_SHARD)
    gbig = jnp.concatenate([gin, gout], axis=2)
    half = D_MODEL // 2
    gw = W_IN_SHARD_PAD + W_OUT_SHARD
    keep = lax.dynamic_slice_in_dim(gbig, core * half, half, axis=1)
    give = lax.dynamic_slice_in_dim(gbig, (1 - core) * half, half, axis=1)
    got = _swap_sibling(give.reshape(N_CHIPS * half, gw), "swap_grad_halves")
    pair = _add(keep.reshape(N_CHIPS * half, gw), got, "add_pair", BF).reshape(N_CHIPS, half, gw)
    grad_x, dshift, dscale, dg_pre, from_chips = _in_proj_bwd_x(
        dproj, w_al_t, x2, dout, g_pre, scale1p, t["tm_out"], pair)

    pad_lane = lambda vrow: jnp.pad(vrow, ((0, 0), (0, LANES - vrow.shape[1])))
    packed = jnp.concatenate([dshift, dscale, dgate, dg_pre, dg_post,
                              pad_lane(db_heads.reshape(1, FOX_HEADS)), pad_lane(dsinks.reshape(1, FOX_HEADS)),
                              pad_lane(loss_part)], axis=1)
    parts = _allgather_devices(packed, "gather_partials")
    tot = _sum_devices(parts)
    loss = tot[0, P_LOSS]
    g_b_ada = tot[:, P_DMOD:P_DMOD + 3 * D_MODEL]
    g_g_pre = tot[:, P_GPRE:P_GPRE + D_MODEL]
    g_g_post = tot[:, P_GPOST:P_GPOST + D_MODEL]
    g_b_fgate = tot[:, P_BF:P_BF + FOX_HEADS]
    g_sinks = tot[:, P_SINK:P_SINK + FOX_HEADS]
    dm_shard = lax.dynamic_slice_in_dim(parts[:, 0, :3 * D_MODEL], chip * W_ADA_SHARD, W_ADA_SHARD, axis=1)
    g_w_ada = _grad_w_ada(a_all.T, dm_shard)

    gfull = _sum_chips_and_share(from_chips, "sum_chips_and_share")
    g_w_in = gfull[:, :W_IN_SHARD]
    g_w_out = gfull[:, W_IN_SHARD_PAD:].reshape(W_OUT_SHARD, D_MODEL)

    grads = dict(w_ada=g_w_ada, b_ada=g_b_ada, g_pre=g_g_pre, w_in=g_w_in, b_fgate=g_b_fgate, sinks=g_sinks,
                 w_out=g_w_out, g_post=g_g_post)
    weights = dict(w_ada=w_ada, b_ada=b_ada, g_pre=g_pre, w_in=w_in, b_fgate=b_fgate, sinks=sinks, w_out=w_out, g_post=g_post)
    moms = dict(w_ada=m_w_ada, b_ada=m_b_ada, g_pre=m_g_pre, w_in=m_w_in, b_fgate=m_b_fgate, sinks=m_sinks, w_out=m_w_out, g_post=m_g_post)
    vars_ = dict(w_ada=v_w_ada, b_ada=v_b_ada, g_pre=v_g_pre, w_in=v_w_in, b_fgate=v_b_fgate, sinks=v_sinks, w_out=v_w_out, g_post=v_g_post)
    names = ["w_ada", "b_ada", "g_pre", "w_in", "b_fgate", "sinks", "w_out", "g_post"]
    g_out, d_out, m_out, v_out = [], [], [], []
    for n in names:
        if n == "w_in":
            flat = lambda a: jnp.transpose(a, (2, 0, 1)).reshape(W_IN_SHARD * D_MODEL // LANES, LANES)
            unflat = lambda a: jnp.transpose(a.reshape(W_IN_SHARD, 1, D_MODEL), (1, 2, 0))
            outs = _adamw(flat(w_in), flat(grads[n][None]), flat(moms[n]), flat(vars_[n]), "adamw_" + n)
            go, d, nm, nv = (unflat(a) for a in outs)
        else:
            g2 = grads[n].reshape(weights[n].shape[-2:])
            go, d, nm, nv = _adamw(weights[n], g2, moms[n], vars_[n], "adamw_" + n)
        g_out.append(go)
        d_out.append(d)
        m_out.append(nm)
        v_out.append(nv)
    return (loss, grad_x.reshape(x.shape), *g_out, *d_out, *m_out, *v_out)
```

```python
import jax
import jax.numpy as jnp
from jax import lax
from jax.experimental import pallas as pl
from jax.experimental.pallas import tpu as pltpu

_INTERPRET = False

D_MODEL = 1024
HEAD_DIM = 64
HALF = HEAD_DIM // 2
AUG_DIM = 128
AUG_ROWS = 8
VT_ROWS = 80
LOG2E = 1.4426950408889634
LN2 = 0.6931471805599453
Q_SCALE = LOG2E * 64 ** -0.5
FOX_HEADS = 8
FOX_W = 512
SWA_W = 512
SWA_KV_HEADS = 2
SWA_GROUP = 4
SWA_KV_W = 128
WINDOW = 128
ROPE_THETA = 10000.0
RMS_EPS = 1e-6
IN_WIDTH = 3336
N_CHIPS = 4
N_DEV = 8
W_IN_SHARD = IN_WIDTH // N_CHIPS
W_IN_SHARD_PAD = 896
W_ADA_SHARD = 3 * D_MODEL // N_CHIPS
W_OUT_SHARD = D_MODEL // N_CHIPS
LANES = 128

_SRC = dict(qa=0, ka=512, va=1024, fa=1536, za=1544, qb=2056, kb=2568, vb=2696, zb=2824)
C_QA, C_KA, C_VA, C_ZA, C_QB, C_ZB, C_KB, C_VB, C_F = 0, 512, 1024, 1536, 2048, 2560, 3072, 3200, 3328
WP = 3456

ADAM_LR = 0.001
ADAM_B1 = 0.9
ADAM_B2 = 0.999
ADAM_EPS = 1e-08
ADAM_WD = 0.01
ADAM_STEP = 10
ADAMW_BLOCK_ELEMS = 300_000

VMEM_LIMIT = 56 * 1024 * 1024
NEG = -1e30
OVERFLOW_GUARD = 1e30
MESH = pl.DeviceIdType.MESH
BF = jnp.bfloat16
F32 = jnp.float32

P_DMOD, P_GPRE, P_GPOST, P_BF, P_SINK, P_LOSS, P_LEN = 0, 3072, 4096, 5120, 5248, 5376, 5504


def _call(body, **kw):
    return pl.pallas_call(body, interpret=_INTERPRET, **kw)


def _params(sem=None, **kw):
    return pltpu.CompilerParams(dimension_semantics=sem, vmem_limit_bytes=VMEM_LIMIT, **kw)


def _full(shape):
    zeros = (0,) * len(shape)
    return pl.BlockSpec(shape, lambda *_: zeros)


def _dot(a, b):
    return jnp.dot(a, b, preferred_element_type=F32)


def _dot_nt(a, b):
    return lax.dot_general(a, b, (((1,), (1,)), ((), ())), preferred_element_type=F32)


def _dot_tn(a, b):
    return lax.dot_general(a, b, (((0,), (0,)), ((), ())), preferred_element_type=F32)


def _sigmoid(z):
    return 1.0 / (1.0 + jnp.exp(-z))


def _rope_partner(t):
    w = t.shape[-1]
    lane = lax.broadcasted_iota(jnp.int32, t.shape, t.ndim - 1)
    return jnp.where((lane & (HEAD_DIM - 1)) < HALF, pltpu.roll(t, w - HALF, t.ndim - 1), pltpu.roll(t, HALF, t.ndim - 1))


def _allgather_devices(v, name):
    r, cdim = v.shape
    masks = [(dx, dy, dc) for dx in (0, 1) for dy in (0, 1) for dc in (0, 1)][1:]

    def body(v_ref, out_ref, send_sems, recv_sems):
        x, y, c = lax.axis_index("x"), lax.axis_index("y"), lax.axis_index("c")
        me = 4 * x + 2 * y + c
        out_ref[me] = v_ref[...]
        copies = []
        for k, (dx, dy, dc) in enumerate(masks):
            cp = pltpu.make_async_remote_copy(
                src_ref=v_ref, dst_ref=out_ref.at[me], send_sem=send_sems.at[k], recv_sem=recv_sems.at[k],
                device_id=(x ^ dx, y ^ dy, c ^ dc), device_id_type=MESH)
            cp.start()
            copies.append(cp)
        for k, (dx, dy, dc) in enumerate(masks):
            peer = 4 * (x ^ dx) + 2 * (y ^ dy) + (c ^ dc)
            pltpu.make_async_remote_copy(
                src_ref=v_ref, dst_ref=out_ref.at[peer], send_sem=send_sems.at[k], recv_sem=recv_sems.at[k],
                device_id=(x ^ dx, y ^ dy, c ^ dc), device_id_type=MESH).wait_recv()
        for cp in copies:
            cp.wait_send()

    return _call(
        body, name=name, out_shape=jax.ShapeDtypeStruct((N_DEV, r, cdim), v.dtype),
        in_specs=[pl.BlockSpec(memory_space=pltpu.VMEM)], out_specs=pl.BlockSpec(memory_space=pltpu.VMEM),
        scratch_shapes=[pltpu.SemaphoreType.DMA((7,)), pltpu.SemaphoreType.DMA((7,))],
        compiler_params=pltpu.CompilerParams(has_side_effects=True),
    )(v)


CHIP_MASKS = [(1, 0), (0, 1), (1, 1)]
CHIP_GATHER_SEMS = [pltpu.SemaphoreType.DMA((2 * len(CHIP_MASKS),)), pltpu.SemaphoreType.DMA((2 * len(CHIP_MASKS),)),
                    pltpu.SemaphoreType.DMA]


def _chip_gather(v_ref, out_ref, send_sems, recv_sems, local_sem):
    n = len(CHIP_MASKS)
    x, y, c = lax.axis_index("x"), lax.axis_index("y"), lax.axis_index("c")
    me = 2 * x + y
    mine = pltpu.make_async_copy(v_ref, out_ref.at[me], local_sem)

    def copy(k, chip, half, to):
        return pltpu.make_async_remote_copy(
            src_ref=v_ref.at[half] if k < n else out_ref.at[chip, half], dst_ref=out_ref.at[chip, half],
            send_sem=send_sems.at[k], recv_sem=recv_sems.at[k], device_id=to, device_id_type=MESH)

    def start():
        mine.start()
        for k, (dx, dy) in enumerate(CHIP_MASKS):
            copy(k, me, c, (x ^ dx, y ^ dy, c)).start()

    def finish():
        passed = []
        for k, (dx, dy) in enumerate(CHIP_MASKS):
            peer = 2 * (x ^ dx) + (y ^ dy)
            copy(k, peer, c, (x, y, c)).wait_recv()
            cp = copy(n + k, peer, c, (x, y, 1 - c))
            cp.start()
            passed.append(cp)
        for k, (dx, dy) in enumerate(CHIP_MASKS):
            copy(n + k, 2 * (x ^ dx) + (y ^ dy), 1 - c, (x, y, c)).wait_recv()
        for k, (dx, dy) in enumerate(CHIP_MASKS):
            copy(k, me, c, (x ^ dx, y ^ dy, c)).wait_send()
        for cp in passed:
            cp.wait_send()
        mine.wait()

    return start, finish


def _allgather_chips(v, name):
    _, r, cdim = v.shape
    n = len(CHIP_MASKS)

    def body(v_ref, out_ref, send_sems, recv_sems, local_sem):
        start, finish = _chip_gather(v_ref, out_ref, send_sems, recv_sems, local_sem)
        start()
        finish()

    return _call(
        body, name=name, out_shape=jax.ShapeDtypeStruct((N_CHIPS, 2, r, cdim), v.dtype),
        in_specs=[pl.BlockSpec(memory_space=pl.ANY)], out_specs=pl.BlockSpec(memory_space=pl.ANY),
        scratch_shapes=[pltpu.SemaphoreType.DMA((2 * n,)), pltpu.SemaphoreType.DMA((2 * n,)), pltpu.SemaphoreType.DMA],
        compiler_params=pltpu.CompilerParams(has_side_effects=True),
    )(v)


def _ada_exchange(c, w_ada_shard):
    masks = [(dx, dy, dc) for dx in (0, 1) for dy in (0, 1) for dc in (0, 1)][1:]
    n = len(masks)

    def body(c_ref, w_ref, a_ref, mod_ref, c_all, send_sems, recv_sems):
        x, y, cc = lax.axis_index("x"), lax.axis_index("y"), lax.axis_index("c")
        me = 4 * x + 2 * y + cc

        def gather(src_ref, dst_ref, first):
            sends = []
            for k, (dx, dy, dc) in enumerate(masks):
                cp = pltpu.make_async_remote_copy(
                    src_ref=src_ref, dst_ref=dst_ref.at[me], send_sem=send_sems.at[first + k],
                    recv_sem=recv_sems.at[first + k], device_id=(x ^ dx, y ^ dy, cc ^ dc), device_id_type=MESH)
                cp.start()
                sends.append(cp)
            for k, (dx, dy, dc) in enumerate(masks):
                peer = 4 * (x ^ dx) + 2 * (y ^ dy) + (cc ^ dc)
                pltpu.make_async_remote_copy(
                    src_ref=src_ref, dst_ref=dst_ref.at[peer], send_sem=send_sems.at[first + k],
                    recv_sem=recv_sems.at[first + k], device_id=(x ^ dx, y ^ dy, cc ^ dc), device_id_type=MESH).wait_recv()
            return sends

        c_all[me] = c_ref[...]
        sends = gather(c_ref, c_all, 0)
        w_bf = w_ref[...].astype(BF)
        for d in range(N_DEV):
            cv = c_all[d]
            a = cv * _sigmoid(cv)
            a_ref[d:d + 1, :] = a
            mod_ref[me, d:d + 1, :] = _dot(a.astype(BF), w_bf)
        sends += gather(mod_ref.at[me], mod_ref, n)
        for cp in sends:
            cp.wait_send()

    vmem = pl.BlockSpec(memory_space=pltpu.VMEM)
    return _call(
        body, name="ada_exchange",
        out_shape=(jax.ShapeDtypeStruct((N_DEV, D_MODEL), F32), jax.ShapeDtypeStruct((N_DEV, N_DEV, W_ADA_SHARD), F32)),
        in_specs=[vmem, vmem], out_specs=(vmem, vmem),
        scratch_shapes=[pltpu.VMEM((N_DEV, 1, D_MODEL), F32), pltpu.SemaphoreType.DMA((2 * n,)),
                        pltpu.SemaphoreType.DMA((2 * n,))],
        compiler_params=_params(has_side_effects=True),
    )(c, w_ada_shard)


def _grad_w_ada(a_t, dm_shard):
    def body(a_ref, dm_ref, out_ref):
        acc = jnp.zeros((D_MODEL, W_ADA_SHARD), F32)
        for b in range(N_DEV):
            acc = acc + a_ref[:, b:b + 1] * dm_ref[b:b + 1, :]
        out_ref[...] = acc

    return _call(body, name="grad_w_ada", out_shape=jax.ShapeDtypeStruct((D_MODEL, W_ADA_SHARD), F32),
                 compiler_params=_params())(a_t, dm_shard)


def _sum_devices(parts):
    n = parts.shape[-1]

    def body(p_ref, out_ref):
        acc = p_ref[0]
        for b in range(1, N_DEV):
            acc = acc + p_ref[b]
        out_ref[...] = acc

    return _call(body, name="sum_devices", out_shape=jax.ShapeDtypeStruct((1, n), F32), compiler_params=_params())(parts)


def _pair_sum(parts, name):
    n, r2, cdim = parts.shape
    r = r2 // 2
    tr = 128

    def body(p_ref, o_ref, land_ref, send_sems, recv_sems):
        x, y, c = lax.axis_index("x"), lax.axis_index("y"), lax.axis_index("c")
        mine = pl.multiple_of(c * r, r)
        theirs = pl.multiple_of((1 - c) * r, r)
        copies = []
        for k in range(n):
            cp = pltpu.make_async_remote_copy(
                src_ref=p_ref.at[k, pl.ds(theirs, r), :], dst_ref=land_ref.at[k], send_sem=send_sems.at[k],
                recv_sem=recv_sems.at[k], device_id=(x, y, 1 - c), device_id_type=MESH)
            cp.start()
            copies.append(cp)
        for k in range(n):
            copies[k].wait_recv()

            def rows(i, carry, k=k):
                sl = pl.ds(pl.multiple_of(i * tr, tr), tr)
                own = p_ref[k, pl.ds(pl.multiple_of(mine + i * tr, tr), tr), :].astype(F32)
                o_ref[k, sl, :] = (own + land_ref[k, sl, :].astype(F32)).astype(BF)
                return carry

            lax.fori_loop(0, r // tr, rows, 0)
        for cp in copies:
            cp.wait_send()

    vmem = pl.BlockSpec(memory_space=pltpu.VMEM)
    return _call(body, name=name, out_shape=jax.ShapeDtypeStruct((n, r, cdim), BF), in_specs=[vmem], out_specs=vmem,
                 scratch_shapes=[pltpu.VMEM((n, r, cdim), BF), pltpu.SemaphoreType.DMA((n,)), pltpu.SemaphoreType.DMA((n,))],
                 compiler_params=_params(has_side_effects=True))(parts)


def _sum_chips_and_share(parts, name):
    _, r, cdim = parts.shape
    tr = 128

    def body(p_ref, o_ref, send_sem, recv_sem):
        x, y, c = lax.axis_index("x"), lax.axis_index("y"), lax.axis_index("c")
        mine = pl.multiple_of(c * r, r)

        def rows(n, carry):
            sl = pl.ds(pl.multiple_of(n * tr, tr), tr)
            p = [p_ref[k, sl, :].astype(F32) for k in range(N_CHIPS)]
            o_ref[pl.ds(pl.multiple_of(mine + n * tr, tr), tr), :] = ((p[0] + p[1]) + p[2]) + p[3]
            return carry

        lax.fori_loop(0, r // tr, rows, 0)
        half = o_ref.at[pl.ds(mine, r), :]
        cp = pltpu.make_async_remote_copy(src_ref=half, dst_ref=half, send_sem=send_sem, recv_sem=recv_sem,
                                          device_id=(x, y, 1 - c), device_id_type=MESH)
        cp.start()
        other = o_ref.at[pl.ds(pl.multiple_of((1 - c) * r, r), r), :]
        pltpu.make_async_remote_copy(src_ref=other, dst_ref=other, send_sem=send_sem, recv_sem=recv_sem,
                                     device_id=(x, y, 1 - c), device_id_type=MESH).wait_recv()
        cp.wait_send()

    vmem = pl.BlockSpec(memory_space=pltpu.VMEM)
    return _call(body, name=name, out_shape=jax.ShapeDtypeStruct((2 * r, cdim), F32), in_specs=[vmem], out_specs=vmem,
                 scratch_shapes=[pltpu.SemaphoreType.DMA, pltpu.SemaphoreType.DMA],
                 compiler_params=_params(has_side_effects=True))(parts)


def _adamw(w, g, m, v, name):
    r, cdim = w.shape[-2:]
    lead = w.ndim - 2
    tr = r if r <= 256 else max(t for t in range(8, ADAMW_BLOCK_ELEMS // cdim + 1, 8) if r % t == 0)
    c1 = 1.0 / (1.0 - ADAM_B1 ** ADAM_STEP)
    c2 = 1.0 / (1.0 - ADAM_B2 ** ADAM_STEP)

    def body(w_ref, g_ref, m_ref, v_ref, go_ref, d_ref, nm_ref, nv_ref):
        gv = g_ref[...].reshape(go_ref.shape)
        nm = ADAM_B1 * m_ref[...] + (1.0 - ADAM_B1) * gv
        nv = ADAM_B2 * v_ref[...] + (1.0 - ADAM_B2) * (gv * gv)
        m_hat = nm * c1
        v_hat = nv * c2
        go_ref[...] = gv
        d_ref[...] = -ADAM_LR * (m_hat / (jnp.sqrt(v_hat) + ADAM_EPS) + ADAM_WD * w_ref[...])
        nm_ref[...] = nm
        nv_ref[...] = nv

    spec = pl.BlockSpec((1,) * lead + (tr, cdim), lambda i: (0,) * lead + (i, 0))
    shp = jax.ShapeDtypeStruct(w.shape, F32)
    return _call(body, name=name, out_shape=(shp,) * 4, grid=(r // tr,),
                 in_specs=[spec, pl.BlockSpec((tr, cdim), lambda i: (i, 0)), spec, spec],
                 out_specs=(spec,) * 4, compiler_params=_params(("parallel",)))(w, g, m, v)


def _head_of_row(r, nc):
    assert nc & (nc - 1) == 0
    return lax.shift_right_logical(r, nc.bit_length() - 1)


def _chunk_mats(rows, nc, reverse):
    ri = lax.broadcasted_iota(jnp.int32, (rows, rows), 0)
    ci = lax.broadcasted_iota(jnp.int32, (rows, rows), 1)
    same = _head_of_row(ri, nc) == _head_of_row(ci, nc)
    between = jnp.where(same & ((ci > ri) if reverse else (ci < ri)), 1.0, 0.0).astype(F32)
    li = lax.broadcasted_iota(jnp.int32, (LANES, LANES), 0)
    lj = lax.broadcasted_iota(jnp.int32, (LANES, LANES), 1)
    within = jnp.where((li >= lj) if reverse else (li <= lj), 1.0, 0.0).astype(F32)
    return between, within


def _dot_hi(a, b):
    return jnp.dot(a, b, preferred_element_type=F32, precision=lax.Precision.HIGHEST)


def _scan_rows(t, nc, reverse):
    between, within = _chunk_mats(t.shape[0], nc, reverse)
    inner = _dot_hi(t, within)
    tot = jnp.sum(t, axis=1, keepdims=True)
    return inner + _dot_hi(between, jnp.broadcast_to(tot, t.shape))


def _log_forget_cumsum(f_rows, bias_rows, nc):
    def body(f_ref, b_ref, cum_ref):
        z = f_ref[...] + b_ref[...]
        lf = jnp.minimum(z, 0.0) - jnp.log(1.0 + jnp.exp(-jnp.abs(z)))
        cum_ref[...] = _scan_rows(lf, nc, False)

    return _call(body, name="forget_cumsum", out_shape=jax.ShapeDtypeStruct(f_rows.shape, F32),
                 compiler_params=_params())(f_rows, bias_rows)


def _log_forget_cumsum_bwd(dcum_rows, f_rows, bias_rows, nc):
    rows = f_rows.shape[0]

    def body(d_ref, f_ref, b_ref, df_ref, db_ref):
        dlf = _scan_rows(d_ref[...], nc, True)
        z = f_ref[...] + b_ref[...]
        df = dlf * _sigmoid(-z)
        df_ref[...] = df
        hi = lax.broadcasted_iota(jnp.int32, (FOX_HEADS, rows), 0)
        ri = lax.broadcasted_iota(jnp.int32, (FOX_HEADS, rows), 1)
        sel = jnp.where(_head_of_row(ri, nc) == hi, 1.0, 0.0).astype(F32)
        db_ref[...] = jnp.sum(_dot_hi(sel, df), axis=1, keepdims=True)

    return _call(body, name="forget_cumsum_bwd",
                 out_shape=(jax.ShapeDtypeStruct(f_rows.shape, F32), jax.ShapeDtypeStruct((FOX_HEADS, 1), F32)),
                 compiler_params=_params())(dcum_rows, f_rows, bias_rows)


def _rms_hat(xv):
    rstd = lax.rsqrt(jnp.mean(xv * xv, axis=-1, keepdims=True) + RMS_EPS)
    return xv * rstd, rstd


def _modulated(x_ref, g_ref, sc_ref, sh_ref):
    xhat, _ = _rms_hat(x_ref[...])
    return ((xhat * g_ref[...]) * sc_ref[...] + sh_ref[...]).astype(BF)


def _forget_logits(x, g_pre, scale1p, shift, w_f, tm):
    s = x.shape[0]

    def body(x_ref, g_ref, sc_ref, sh_ref, w_ref, f_ref):
        f_ref[...] = _dot(_modulated(x_ref, g_ref, sc_ref, sh_ref), w_ref[...])

    vec = _full((1, D_MODEL))
    return _call(
        body, name="forget_logits", out_shape=jax.ShapeDtypeStruct((s, LANES), F32), grid=(s // tm,),
        in_specs=[pl.BlockSpec((tm, D_MODEL), lambda i: (i, 0)), vec, vec, vec, _full((D_MODEL, LANES))],
        out_specs=pl.BlockSpec((tm, LANES), lambda i: (i, 0)), compiler_params=_params(("parallel",)),
    )(x, g_pre, scale1p, shift, w_f)


def _split3(v):
    hi = v.astype(BF).astype(F32)
    mid = (v - hi).astype(BF).astype(F32)
    lo = ((v - hi) - mid).astype(BF).astype(F32)
    return hi, mid, lo


def _in_proj(x, g_pre, scale1p, shift, w_rows, w_t_fox, cum, cos_t, sin_t, w_out_halves, tm):
    s = x.shape[0]
    r_va, r_za, r_qb, r_zb, r_kb, r_vb = 0, 512, 1024, 1536, 2048, 2176

    def body(x_ref, g_ref, sc_ref, sh_ref, w_ref, wt_ref, cum_ref, cos_ref, sin_ref, wo_ref,
             h_ref, qat_ref, ka_ref, kat_ref, v_ref, vt_ref, za_ref, zb_ref, qb_ref, kb_ref, vb_ref,
             qbt_ref, kbt_ref, vbt_ref, mo_ref, wo_all_ref, send_sems, recv_sems, local_sem):
        start_gather, finish_gather = _chip_gather(wo_ref, wo_all_ref, send_sems, recv_sems, local_sem)

        @pl.when(pl.program_id(0) == 0)
        def _():
            start_gather()

        @pl.when(pl.program_id(0) == s // tm - 1)
        def _():
            finish_gather()

        hb = _modulated(x_ref, g_ref, sc_ref, sh_ref)
        h_ref[...] = hb

        def sec(c0, width):
            return _dot(hb, w_ref[:, c0:c0 + width])

        def sec_t(r0):
            return _dot_nt(wt_ref[r0:r0 + FOX_W, :], hb)

        q_t = sec_t(0) * Q_SCALE
        k_t = sec_t(FOX_W)
        v_t = sec_t(2 * FOX_W)
        va = sec(r_va, FOX_W)
        zeros = jnp.zeros((AUG_DIM - HEAD_DIM - AUG_ROWS, tm), F32)
        ri = lax.broadcasted_iota(jnp.int32, (AUG_ROWS, tm), 0)
        const = jnp.where(ri == AUG_ROWS - 1, 0.0, 1.0)
        ri_v = lax.broadcasted_iota(jnp.int32, (VT_ROWS - HEAD_DIM, tm), 0)
        v_feat = jnp.where(ri_v == 0, 1.0, 0.0).astype(BF)
        for hd in range(FOX_HEADS):
            rows = slice(hd * HEAD_DIM, (hd + 1) * HEAD_DIM)
            cum2 = cum_ref[hd:hd + 1, :] * LOG2E
            hi, mid, lo = (jnp.broadcast_to(part, (AUG_ROWS, tm)) for part in _split3(cum2))
            q_feat = jnp.where(ri == 1, hi, jnp.where(ri == 2, mid, jnp.where(ri == 3, lo, const)))
            k_feat = jnp.where(ri == 4, -hi, jnp.where(ri == 5, -mid, jnp.where(ri == 6, -lo, const)))
            q_aug = jnp.concatenate([q_t[rows], q_feat, zeros], axis=0)
            k_aug = jnp.concatenate([k_t[rows], k_feat, zeros], axis=0)
            mo_ref[hd:hd + 1, :] = jnp.sum(q_t[rows] * k_t[rows], axis=0, keepdims=True) + 1.0
            qat_ref[hd] = q_aug.astype(BF)
            kat_ref[hd] = k_aug.astype(BF)
            ka_ref[hd] = k_aug.T.astype(BF)
            vt_ref[hd] = jnp.concatenate([v_t[rows].astype(BF), v_feat], axis=0)
            v_ref[hd] = va[:, rows].astype(BF)
        za_ref[...] = sec(r_za, FOX_W)
        zb_ref[...] = sec(r_zb, SWA_W)
        cos2, sin2 = cos_ref[...], sin_ref[...]
        cos8 = jnp.concatenate([cos2] * 4, axis=1)
        sin8 = jnp.concatenate([sin2] * 4, axis=1)
        qb = sec(r_qb, SWA_W)
        qb = (qb * cos8 + _rope_partner(qb) * sin8) * (HEAD_DIM ** -0.5)
        qb_ref[...] = qb.astype(BF)
        for a in range(SWA_W // LANES):
            qbt_ref[a * LANES:(a + 1) * LANES, :] = qb[:, a * LANES:(a + 1) * LANES].T.astype(BF)
        kb = sec(r_kb, SWA_KV_W)
        kb = kb * cos2 + _rope_partner(kb) * sin2
        vb = sec(r_vb, SWA_KV_W)
        kb_t, vb_t = kb.T, vb.T
        for hd in range(SWA_KV_HEADS):
            sl = slice(hd * HEAD_DIM, (hd + 1) * HEAD_DIM)
            kb_ref[hd] = kb[:, sl].astype(BF)
            vb_ref[hd] = vb[:, sl].astype(BF)
            kbt_ref[hd] = kb_t[sl].astype(BF)
            vbt_ref[hd] = jnp.concatenate([vb_t[sl].astype(BF), v_feat], axis=0)

    row = lambda w: pl.BlockSpec((tm, w), lambda i: (i, 0))
    heads = lambda n, w=HEAD_DIM: pl.BlockSpec((n, tm, w), lambda i: (0, i, 0))
    heads_t = lambda w: pl.BlockSpec((FOX_HEADS, w, tm), lambda i: (0, 0, i))
    vec = _full((1, D_MODEL))
    hs = lambda a, b: jax.ShapeDtypeStruct((FOX_HEADS, a, b), BF)
    out_shape = (
        jax.ShapeDtypeStruct((s, D_MODEL), BF),
        hs(AUG_DIM, s), hs(s, AUG_DIM), hs(AUG_DIM, s), hs(s, HEAD_DIM), hs(VT_ROWS, s),
        jax.ShapeDtypeStruct((s, FOX_W), F32), jax.ShapeDtypeStruct((s, SWA_W), F32),
        jax.ShapeDtypeStruct((s, SWA_W), BF),
        jax.ShapeDtypeStruct((SWA_KV_HEADS, s, HEAD_DIM), BF), jax.ShapeDtypeStruct((SWA_KV_HEADS, s, HEAD_DIM), BF),
        jax.ShapeDtypeStruct((SWA_W, s), BF),
        jax.ShapeDtypeStruct((SWA_KV_HEADS, HEAD_DIM, s), BF), jax.ShapeDtypeStruct((SWA_KV_HEADS, VT_ROWS, s), BF),
        jax.ShapeDtypeStruct((FOX_HEADS, s), F32),
        jax.ShapeDtypeStruct((N_CHIPS,) + w_out_halves.shape, w_out_halves.dtype),
    )
    kv_t = lambda w: pl.BlockSpec((SWA_KV_HEADS, w, tm), lambda i: (0, 0, i))
    hbm = pl.BlockSpec(memory_space=pl.ANY)
    return _call(
        body, name="in_proj", out_shape=out_shape, grid=(s // tm,),
        in_specs=[row(D_MODEL), vec, vec, vec, _full(w_rows.shape), _full(w_t_fox.shape),
                  pl.BlockSpec((FOX_HEADS, tm), lambda i: (0, i)), row(LANES), row(LANES), hbm],
        out_specs=(row(D_MODEL), heads_t(AUG_DIM), heads(FOX_HEADS, AUG_DIM), heads_t(AUG_DIM), heads(FOX_HEADS),
                   heads_t(VT_ROWS), row(FOX_W), row(SWA_W), row(SWA_W), heads(SWA_KV_HEADS), heads(SWA_KV_HEADS),
                   pl.BlockSpec((SWA_W, tm), lambda i: (0, i)), kv_t(HEAD_DIM), kv_t(VT_ROWS),
                   pl.BlockSpec((FOX_HEADS, tm), lambda i: (0, i)), hbm),
        scratch_shapes=list(CHIP_GATHER_SEMS),
        compiler_params=_params(("arbitrary",), has_side_effects=True),
    )(x, g_pre, scale1p, shift, w_rows, w_t_fox, cum, cos_t, sin_t, w_out_halves)


def _diag_chunks(d, bq, bk, chunk):
    out = []
    for c0 in range(0, bq, chunk):
        if d is None or d * bk + bk - 1 <= c0:
            out.append((c0, None, bk))
        elif d * bk <= c0 + chunk - 1:
            n_keys = min(bk, c0 + chunk - d * bk)
            kpos = d * bk + lax.broadcasted_iota(jnp.int32, (n_keys, chunk), 0)
            qpos = c0 + lax.broadcasted_iota(jnp.int32, (n_keys, chunk), 1)
            out.append((c0, kpos <= qpos, n_keys))
    return out


def _fox_fwd(qat, ka, vt, m_own, bq, bk, chunk, running_max):
    nh, _, s = qat.shape
    r = bq // bk

    pairs = [(i, j) for i in range(s // bq) for j in range(i * r + r)]

    def body(i_tab, j_tab, ka_ref, qat_ref, vt_ref, mo_ref, o_ref, lse_ref, bad_ref, *rest):
        pt_ref, m_scr, acc_scr = (None,) * running_max + rest
        i, j = i_tab[pl.program_id(1)], j_tab[pl.program_id(1)]

        @pl.when(j == 0)
        def _():
            m_scr[...] = jnp.full(m_scr.shape, NEG, F32) if running_max else mo_ref[0]
            acc_scr[...] = jnp.zeros(acc_scr.shape, F32)

        def careful(d):
            kv, vtv = ka_ref[0], vt_ref[0]

            def one_chunk(n, carry):
                c0 = pl.multiple_of(n * chunk, chunk)
                cs = pl.ds(c0, chunk)
                sc = _dot(kv, qat_ref[0, :, cs])
                if d is not None:
                    kpos = d * bk + lax.broadcasted_iota(jnp.int32, (bk, chunk), 0)
                    qpos = c0 + lax.broadcasted_iota(jnp.int32, (bk, chunk), 1)
                    sc = jnp.where(kpos <= qpos, sc, NEG)
                m_prev = m_scr[:, cs]
                m_new = jnp.maximum(m_prev, jnp.max(sc, axis=0, keepdims=True))
                p = jnp.exp2(sc - m_new).astype(BF)
                acc_scr[:, cs] = jnp.exp2(m_prev - m_new) * acc_scr[:, cs] + _dot(vtv, p)
                m_scr[:, cs] = m_new
                return carry

            lax.fori_loop(0, bq // chunk, one_chunk, 0)

        def fast(d):
            todo = _diag_chunks(d, bq, bk, chunk)
            scores = lambda t: _dot(ka_ref[0, :t[2], :], qat_ref[0, :, t[0]:t[0] + chunk])
            sc_next = scores(todo[0])
            for n, (c0, mask, n_keys) in enumerate(todo):
                cs = slice(c0, c0 + chunk)
                sc = sc_next
                if n + 1 < len(todo):
                    sc_next = scores(todo[n + 1])
                if mask is not None:
                    sc = jnp.where(mask, sc, NEG)
                p = jnp.exp2(sc - m_scr[:, cs]).astype(BF)
                pt_ref[0, :n_keys, cs] = p
                acc_scr[:, cs] += _dot(vt_ref[0, :, :n_keys], p)

        step = careful if running_max else fast

        @pl.when(j < i * r)
        def _():
            step(None)

        for d in range(r):
            @pl.when(j == i * r + d)
            def _(d=d):
                step(d)

        @pl.when(j == i * r + r - 1)
        def _():
            l = acc_scr[HEAD_DIM:HEAD_DIM + 1, :]
            o_ref[0] = acc_scr[:HEAD_DIM, :] / l
            lse_ref[0] = m_scr[...] + jnp.log2(l)
            bad_ref[0] = jnp.where(l < OVERFLOW_GUARD, 0.0, 1.0)

    qmap_t = lambda h, t, it, jt: (h, 0, it[t])
    qrow = pl.BlockSpec((1, 1, bq), qmap_t)
    row_shape = jax.ShapeDtypeStruct((nh, 1, s), F32)
    out_shape = (jax.ShapeDtypeStruct((nh, HEAD_DIM, s), F32), row_shape, row_shape)
    out_specs = (pl.BlockSpec((1, HEAD_DIM, bq), qmap_t), qrow, qrow)
    if not running_max:
        out_shape += (jax.ShapeDtypeStruct((nh, s, s), BF),)
        out_specs += (pl.BlockSpec((1, bk, bq), lambda h, t, it, jt: (h, jt[t], it[t])),)
    grid_spec = pltpu.PrefetchScalarGridSpec(
        num_scalar_prefetch=2, grid=(nh, len(pairs)),
        in_specs=[pl.BlockSpec((1, bk, AUG_DIM), lambda h, t, it, jt: (h, jt[t], 0)), pl.BlockSpec((1, AUG_DIM, bq), qmap_t),
                  pl.BlockSpec((1, VT_ROWS, bk), lambda h, t, it, jt: (h, 0, jt[t])), qrow],
        out_specs=out_specs,
        scratch_shapes=[pltpu.VMEM((1, bq), F32), pltpu.VMEM((VT_ROWS, bq), F32)])
    return _call(
        body, name="fox_fwd_running_max" if running_max else "fox_fwd", out_shape=out_shape, grid_spec=grid_spec,
        compiler_params=_params(("parallel", "arbitrary")),
    )(jnp.asarray([p[0] for p in pairs], jnp.int32), jnp.asarray([p[1] for p in pairs], jnp.int32), ka, qat, vt, m_own)


def _fox_bwd(qat, ka, kat, v, dot_, lse, delta, pt, bq, bk, chunk, dq_blk):
    nh, _, s = qat.shape
    r = bq // bk
    nq = s // bq
    stored = pt is not None

    pairs = [(j, i) for j in range(s // bk) for i in range(j // r, nq)]

    def body(j_tab, i_tab, a_ref, b_ref, kat_ref, v_ref, qat_ref, do_ref, dl_ref, dq_ref, dk_ref, dv_ref, dk_scr, dv_scr):
        ka_ref, lse_ref, pt_ref = (None, None, a_ref) if stored else (a_ref, b_ref, None)
        j, i = j_tab[pl.program_id(1)], i_tab[pl.program_id(1)]

        @pl.when(pl.program_id(1) == 0)
        def _():
            dq_ref[...] = jnp.zeros(dq_ref.shape, F32)

        @pl.when(i * r <= j)
        def _():
            dk_scr[...] = jnp.zeros(dk_scr.shape, F32)
            dv_scr[...] = jnp.zeros(dv_scr.shape, F32)

        def step(d):
            todo = _diag_chunks(d, bq, bk, chunk)

            def products(t):
                cs = slice(t[0], t[0] + chunk)
                return (None if stored else _dot(ka_ref[0, :t[2], :], qat_ref[0, :, cs]),
                        _dot(v_ref[0, :t[2], :], do_ref[0, :, cs]))

            nxt = products(todo[0])
            for n, (c0, mask, n_keys) in enumerate(todo):
                cs = slice(c0, c0 + chunk)
                sc, dp = nxt
                if n + 1 < len(todo):
                    nxt = products(todo[n + 1])
                if stored:
                    p_bf = pt_ref[0, :n_keys, cs]
                    p = p_bf.astype(F32)
                else:
                    p = jnp.exp2(sc - lse_ref[0, :, cs])
                    if mask is not None:
                        p = jnp.where(mask, p, 0.0)
                    p_bf = p.astype(BF)
                ds = (p * (dp - dl_ref[0, :, cs])).astype(BF)
                dv_scr[:, :n_keys] += _dot_nt(do_ref[0, :, cs], p_bf)
                dk_scr[:, :n_keys] += _dot_nt(qat_ref[0, :VT_ROWS, cs], ds)
                c1 = c0 % dq_blk
                dq_ref[0, i * (bq // dq_blk) + c0 // dq_blk, :, c1:c1 + chunk] += _dot(kat_ref[0, :VT_ROWS, :n_keys], ds)

        @pl.when(i * r > j)
        def _():
            step(None)

        for d in range(r):
            @pl.when(j == i * r + d)
            def _(d=d):
                step(d)

        @pl.when(i == nq - 1)
        def _():
            dk_ref[0] = dk_scr[...]
            dv_ref[0] = dv_scr[...]

    qmap = lambda h, t, jt, it: (h, 0, it[t])
    kmap = lambda h, t, jt, it: (h, jt[t], 0)
    kmap_t = lambda h, t, jt, it: (h, 0, jt[t])
    if stored:
        first = [(pt, pl.BlockSpec((1, bk, bq), lambda h, t, jt, it: (h, jt[t], it[t]))),
                 (delta, pl.BlockSpec((1, 1, bq), qmap))]
    else:
        first = [(ka, pl.BlockSpec((1, bk, AUG_DIM), kmap)), (lse, pl.BlockSpec((1, 1, bq), qmap))]
    grid_spec = pltpu.PrefetchScalarGridSpec(
        num_scalar_prefetch=2, grid=(nh, len(pairs)),
        in_specs=[first[0][1], first[1][1], pl.BlockSpec((1, AUG_DIM, bk), kmap_t), pl.BlockSpec((1, bk, HEAD_DIM), kmap),
                  pl.BlockSpec((1, AUG_DIM, bq), qmap), pl.BlockSpec((1, HEAD_DIM, bq), qmap),
                  pl.BlockSpec((1, 1, bq), qmap)],
        out_specs=(pl.BlockSpec((1, s // dq_blk, VT_ROWS, dq_blk), lambda h, t, jt, it: (h, 0, 0, 0)),
                   pl.BlockSpec((1, VT_ROWS, bk), kmap_t), pl.BlockSpec((1, HEAD_DIM, bk), kmap_t)),
        scratch_shapes=[pltpu.VMEM((VT_ROWS, bk), F32), pltpu.VMEM((HEAD_DIM, bk), F32)])
    return _call(
        body, name="fox_bwd" if stored else "fox_bwd_recompute",
        out_shape=(jax.ShapeDtypeStruct((nh, s // dq_blk, VT_ROWS, dq_blk), F32),
                   jax.ShapeDtypeStruct((nh, VT_ROWS, s), F32), jax.ShapeDtypeStruct((nh, HEAD_DIM, s), F32)),
        grid_spec=grid_spec, compiler_params=_params(("parallel", "arbitrary")),
    )(jnp.asarray([p[0] for p in pairs], jnp.int32), jnp.asarray([p[1] for p in pairs], jnp.int32),
      first[0][0], first[1][0], kat, v, qat, dot_, delta)


def _swa_mask(i, tq):
    kpos = i * tq - WINDOW + lax.broadcasted_iota(jnp.int32, (tq + WINDOW, tq), 0)
    qpos = i * tq + lax.broadcasted_iota(jnp.int32, (tq + WINDOW, tq), 1)
    rel = qpos - kpos
    return (rel >= 0) & (rel < WINDOW) & (kpos >= 0)


def _swa_rows(ref, g, i, tq):
    before = pl.multiple_of(jnp.maximum(i * tq - WINDOW, 0), WINDOW)
    return jnp.concatenate([ref[g, pl.ds(before, WINDOW), :], ref[g, pl.ds(pl.multiple_of(i * tq, tq), tq), :]], axis=0)


def _swa_before(n_rows, tq):
    return pl.BlockSpec((SWA_KV_HEADS, n_rows, WINDOW), lambda i: (0, 0, jnp.maximum(i * (tq // WINDOW) - 1, 0)))


def _swa_probs_t(sc, mask, sink):
    sc = jnp.where(mask, sc, NEG)
    m = jnp.maximum(jnp.max(sc, axis=0, keepdims=True), sink)
    p = jnp.exp(sc - m)
    e_sink = jnp.exp(sink - m)
    inv_l = 1.0 / (jnp.sum(p, axis=0, keepdims=True) + e_sink)
    return p * inv_l, e_sink * inv_l


def _swa_fwd(qbt, kb, vbt, sinks, tq):
    s = qbt.shape[1]
    n_heads = SWA_KV_HEADS * SWA_GROUP

    def body(q_ref, k_ref, vb_ref, vc_ref, s_ref, o_ref):
        i = pl.program_id(0)
        mask = _swa_mask(i, tq)
        kw = [_swa_rows(k_ref, g, i, tq) for g in range(SWA_KV_HEADS)]
        vtw = [jnp.concatenate([vb_ref[g], vc_ref[g]], axis=1) for g in range(SWA_KV_HEADS)]
        scores = lambda hd: _dot(kw[hd // SWA_GROUP], q_ref[hd * HEAD_DIM:(hd + 1) * HEAD_DIM, :])
        sc_next = scores(0)
        for hd in range(n_heads):
            g, hh = divmod(hd, SWA_GROUP)
            rows = slice(hd * HEAD_DIM, (hd + 1) * HEAD_DIM)
            sink = s_ref[g][:, hh:hh + 1]
            sc = jnp.where(mask, sc_next, NEG)
            if hd + 1 < n_heads:
                sc_next = scores(hd + 1)
            m = jnp.maximum(jnp.max(sc, axis=0, keepdims=True), sink)
            acc = _dot(vtw[g], jnp.exp(sc - m).astype(BF))
            o_ref[rows, :] = acc[:HEAD_DIM] / (acc[HEAD_DIM:HEAD_DIM + 1] + jnp.exp(sink - m))

    kvspec = _full((SWA_KV_HEADS, s, HEAD_DIM))
    qspec = pl.BlockSpec((SWA_W, tq), lambda i: (0, i))
    return _call(
        body, name="swa_fwd", out_shape=jax.ShapeDtypeStruct((SWA_W, s), F32), grid=(s // tq,),
        in_specs=[qspec, kvspec, _swa_before(VT_ROWS, tq), pl.BlockSpec((SWA_KV_HEADS, VT_ROWS, tq), lambda i: (0, 0, i)),
                  _full((SWA_KV_HEADS, 1, SWA_GROUP))],
        out_specs=qspec, compiler_params=_params(("parallel",)),
    )(qbt, kb, vbt, vbt, sinks)


def _swa_bwd(qb, qbt, kb, kbt, vb, sinks, dob, dobt, tq):
    s = qb.shape[0]
    n_heads = SWA_KV_HEADS * SWA_GROUP

    def body(q_ref, qt_ref, k_ref, ktb_ref, ktc_ref, v_ref, s_ref, do_ref, dot_ref, dq_ref, dk_ref, dv_ref, ds_ref):
        i = pl.program_id(0)

        @pl.when(i == 0)
        def _():
            dk_ref[...] = jnp.zeros(dk_ref.shape, F32)
            dv_ref[...] = jnp.zeros(dv_ref.shape, F32)
            ds_ref[...] = jnp.zeros(ds_ref.shape, F32)

        mask = _swa_mask(i, tq)
        kw = [_swa_rows(k_ref, g, i, tq) for g in range(SWA_KV_HEADS)]
        vw = [_swa_rows(v_ref, g, i, tq) for g in range(SWA_KV_HEADS)]
        ktw = [jnp.concatenate([ktb_ref[g], ktc_ref[g]], axis=1) for g in range(SWA_KV_HEADS)]
        before = pl.ds(pl.multiple_of(jnp.maximum(i * tq - WINDOW, 0), WINDOW), WINDOW)
        own = pl.ds(pl.multiple_of(i * tq, tq), tq)

        def products(hd):
            rows = slice(hd * HEAD_DIM, (hd + 1) * HEAD_DIM)
            return _dot(kw[hd // SWA_GROUP], qt_ref[rows, :]), _dot(vw[hd // SWA_GROUP], dot_ref[rows, :])

        nxt = products(0)
        for g in range(SWA_KV_HEADS):
            dsinks = []
            dk_acc = jnp.zeros((tq + WINDOW, HEAD_DIM), F32)
            dv_acc = jnp.zeros((tq + WINDOW, HEAD_DIM), F32)
            for hh in range(SWA_GROUP):
                hd = g * SWA_GROUP + hh
                rows = slice(hd * HEAD_DIM, (hd + 1) * HEAD_DIM)
                sc, dp = nxt
                if hd + 1 < n_heads:
                    nxt = products(hd + 1)
                p, p_sink = _swa_probs_t(sc, mask, s_ref[g][:, hh:hh + 1])
                delta = jnp.sum(p * dp, axis=0, keepdims=True)
                dsc = (p * (dp - delta)).astype(BF)
                dq_ref[rows, :] = _dot(ktw[g], dsc)
                dk_acc = dk_acc + _dot(dsc, q_ref[:, rows])
                dv_acc = dv_acc + _dot(p.astype(BF), do_ref[:, rows])
                dsinks.append(-jnp.sum(p_sink * delta, axis=1, keepdims=True))
            dk_ref[g, before, :] += dk_acc[:WINDOW]
            dk_ref[g, own, :] += dk_acc[WINDOW:]
            dv_ref[g, before, :] += dv_acc[:WINDOW]
            dv_ref[g, own, :] += dv_acc[WINDOW:]
            ds_ref[g] += jnp.concatenate(dsinks, axis=1)

    kvspec = _full((SWA_KV_HEADS, s, HEAD_DIM))
    qspec = pl.BlockSpec((tq, SWA_W), lambda i: (i, 0))
    qspec_t = pl.BlockSpec((SWA_W, tq), lambda i: (0, i))
    sspec = _full((SWA_KV_HEADS, 1, SWA_GROUP))
    kvshape = jax.ShapeDtypeStruct((SWA_KV_HEADS, s, HEAD_DIM), F32)
    return _call(
        body, name="swa_bwd",
        out_shape=(jax.ShapeDtypeStruct((SWA_W, s), F32), kvshape, kvshape,
                   jax.ShapeDtypeStruct((SWA_KV_HEADS, 1, SWA_GROUP), F32)),
        grid=(s // tq,),
        in_specs=[qspec, qspec_t, kvspec, _swa_before(HEAD_DIM, tq),
                  pl.BlockSpec((SWA_KV_HEADS, HEAD_DIM, tq), lambda i: (0, 0, i)), kvspec, sspec, qspec, qspec_t],
        out_specs=(qspec_t, kvspec, kvspec, sspec),
        compiler_params=_params(("arbitrary",)),
    )(qb, qbt, kb, kbt, kbt, vb, sinks, dob, dobt)


def _pairs_to_rows(ref, n_rows=HEAD_DIM):
    parts = []
    for a in range(0, FOX_HEADS, 2):
        parts.append(jnp.concatenate([ref[a][:n_rows], ref[a + 1][:n_rows]], axis=0).T)
    return jnp.concatenate(parts, axis=1)


def _blocks_to_rows(ref):
    return jnp.concatenate([ref[a:a + LANES, :].T for a in range(0, ref.shape[0], LANES)], axis=1)


def _out_proj(oat, za, obt, zb, x, tgt, w_out, w_out_t, gate, g_post, inv_l, tm):
    s = x.shape[0]

    def body(oat_ref, za_ref, obt_ref, zb_ref, x_ref, t_ref, w_ref, wt_ref, gate_ref, gp_ref, il_ref,
             dout_ref, doat_ref, dla_ref, dza_ref, dob_ref, dobt_ref, dzb_ref, gw_ref, dgate_ref, dgp_ref, loss_ref):
        i = pl.program_id(0)

        @pl.when(i == 0)
        def _():
            gw_ref[...] = jnp.zeros(gw_ref.shape, F32)
            dgate_ref[...] = jnp.zeros(dgate_ref.shape, F32)
            dgp_ref[...] = jnp.zeros(dgp_ref.shape, F32)
            loss_ref[...] = jnp.zeros(loss_ref.shape, F32)

        oa_v = _pairs_to_rows(oat_ref)
        ob_v = _blocks_to_rows(obt_ref)
        za_v, zb_v = za_ref[...], zb_ref[...]
        sga, sgb = _sigmoid(za_v), _sigmoid(zb_v)
        sila, silb = za_v * sga, zb_v * sgb
        u = jnp.concatenate([oa_v * sila, ob_v * silb], axis=1).astype(BF)
        yv = _dot(u, w_ref[...])
        yhat, rstd = _rms_hat(yv)
        gp, gate_v = gp_ref[...], gate_ref[...]
        nrm = yhat * gp
        diff = (x_ref[...] + gate_v * nrm) - t_ref[...]
        loss_ref[...] += 0.5 * jnp.sum(jnp.sum(diff * diff, axis=1, keepdims=True), axis=0, keepdims=True) / D_MODEL
        dout = diff * (1.0 / D_MODEL)
        dout_ref[...] = dout
        dgate_ref[...] += jnp.sum(dout * nrm, axis=0, keepdims=True)
        dn = dout * gate_v
        dgp_ref[...] += jnp.sum(dn * yhat, axis=0, keepdims=True)
        dyhat = dn * gp
        dy = (rstd * (dyhat - yhat * jnp.mean(dyhat * yhat, axis=1, keepdims=True))).astype(BF)
        gw_ref[...] += _dot_tn(u, dy)
        du = _dot(dy, wt_ref[...])
        dua, dub = du[:, :FOX_W], du[:, FOX_W:]
        doa = dua * sila
        for a in range(0, FOX_HEADS, 2):
            pair_t = doa[:, a * HEAD_DIM:(a + 2) * HEAD_DIM].T
            for hd, rows in ((a, slice(0, HEAD_DIM)), (a + 1, slice(HEAD_DIM, 2 * HEAD_DIM))):
                inv_l = il_ref[hd]
                doat_ref[hd] = (pair_t[rows] * inv_l).astype(BF)
                dla_ref[hd] = jnp.sum(pair_t[rows] * oat_ref[hd], axis=0, keepdims=True) * inv_l
        dob = dub * silb
        dob_ref[...] = dob.astype(BF)
        for a in range(0, SWA_W, LANES):
            dobt_ref[a:a + LANES, :] = dob[:, a:a + LANES].T.astype(BF)
        dza_ref[...] = (dua * oa_v * (sga * (1.0 + za_v * (1.0 - sga)))).astype(BF)
        dzb_ref[...] = (dub * ob_v * (sgb * (1.0 + zb_v * (1.0 - sgb)))).astype(BF)

    row = lambda w: pl.BlockSpec((tm, w), lambda i: (i, 0))
    heads_t = lambda w: pl.BlockSpec((FOX_HEADS, w, tm), lambda i: (0, 0, i))
    vec = _full((1, D_MODEL))
    mat = _full((D_MODEL, D_MODEL))
    out_shape = (
        jax.ShapeDtypeStruct((s, D_MODEL), F32),
        jax.ShapeDtypeStruct((FOX_HEADS, HEAD_DIM, s), BF), jax.ShapeDtypeStruct((FOX_HEADS, 1, s), F32),
        jax.ShapeDtypeStruct((s, FOX_W), BF), jax.ShapeDtypeStruct((s, SWA_W), BF), jax.ShapeDtypeStruct((SWA_W, s), BF),
        jax.ShapeDtypeStruct((s, SWA_W), BF),
        jax.ShapeDtypeStruct((D_MODEL, D_MODEL), F32),
        jax.ShapeDtypeStruct((1, D_MODEL), F32), jax.ShapeDtypeStruct((1, D_MODEL), F32),
        jax.ShapeDtypeStruct((1, 1), F32),
    )
    col = pl.BlockSpec((SWA_W, tm), lambda i: (0, i))
    return _call(
        body, name="out_proj", out_shape=out_shape, grid=(s // tm,),
        in_specs=[heads_t(HEAD_DIM), row(FOX_W), col, row(SWA_W), row(D_MODEL), row(D_MODEL), mat, mat, vec, vec,
                  heads_t(1)],
        out_specs=(row(D_MODEL), heads_t(HEAD_DIM), heads_t(1), row(FOX_W), row(SWA_W), col, row(SWA_W), mat, vec, vec,
                   _full((1, 1))),
        compiler_params=_params(("arbitrary",)),
    )(oat, za, obt, zb, x, tgt, w_out, w_out_t, gate, g_post, inv_l)


def _assemble_dproj(dqt, dkt, dvt, dza, dqb, dzb, dkb, dvb, df, cos_t, sin_t, tm):
    s = dza.shape[0]

    def body(dqt_ref, dkt_ref, dvt_ref, dza_ref, dqb_ref, dzb_ref, dkb_ref, dvb_ref, df_ref, cos_ref, sin_ref, o_ref):
        def cat(ref, n):
            return jnp.concatenate([ref[hd] for hd in range(n)], axis=1)

        cos2, sin2 = cos_ref[...], sin_ref[...]
        cos8 = jnp.concatenate([cos2] * 4, axis=1)
        sin8 = jnp.concatenate([sin2] * 4, axis=1)
        scale = HEAD_DIM ** -0.5
        o_ref[:, C_QA:C_QA + FOX_W] = (_pairs_to_rows(dqt_ref.at[:, 0]) * scale).astype(BF)
        o_ref[:, C_KA:C_KA + FOX_W] = (_pairs_to_rows(dkt_ref) * LN2).astype(BF)
        o_ref[:, C_VA:C_VA + FOX_W] = _pairs_to_rows(dvt_ref).astype(BF)
        o_ref[:, C_ZA:C_ZA + FOX_W] = dza_ref[...]
        dq = _blocks_to_rows(dqb_ref) * scale
        o_ref[:, C_QB:C_QB + SWA_W] = (dq * cos8 - _rope_partner(dq) * sin8).astype(BF)
        o_ref[:, C_ZB:C_ZB + SWA_W] = dzb_ref[...]
        dk = cat(dkb_ref, SWA_KV_HEADS)
        o_ref[:, C_KB:C_KB + SWA_KV_W] = (dk * cos2 - _rope_partner(dk) * sin2).astype(BF)
        o_ref[:, C_VB:C_VB + SWA_KV_W] = cat(dvb_ref, SWA_KV_HEADS).astype(BF)
        o_ref[:, C_F:C_F + LANES] = df_ref[...].astype(BF)

    row = lambda w: pl.BlockSpec((tm, w), lambda i: (i, 0))
    heads = lambda n: pl.BlockSpec((n, tm, HEAD_DIM), lambda i: (0, i, 0))
    heads_t = lambda w: pl.BlockSpec((FOX_HEADS, w, tm), lambda i: (0, 0, i))
    return _call(
        body, name="assemble_dproj", out_shape=jax.ShapeDtypeStruct((s, WP), BF), grid=(s // tm,),
        in_specs=[pl.BlockSpec((FOX_HEADS, 1, VT_ROWS, tm), lambda i: (0, i, 0, 0)), heads_t(VT_ROWS), heads_t(HEAD_DIM),
                  row(FOX_W), pl.BlockSpec((SWA_W, tm), lambda i: (0, i)), row(SWA_W), heads(SWA_KV_HEADS),
                  heads(SWA_KV_HEADS), row(LANES), row(LANES), row(LANES)],
        out_specs=row(WP), compiler_params=_params(("parallel",)),
    )(dqt, dkt, dvt, dza, dqb, dzb, dkb, dvb, df, cos_t, sin_t)


def _in_proj_bwd_x(dproj, w_al_t, x, dout, g_pre, scale1p, tm, parts):
    s = x.shape[0]
    n_steps = s // tm
    masks = [(1, 0), (0, 1), (1, 1)]

    def body(dp_ref, wt_ref, x_ref, dout_ref, g_ref, sc_ref, parts_ref, gx_ref, dsh_ref, dsc_ref, dg_ref, got_ref,
             send_sems, recv_sems, local_sem):
        i = pl.program_id(0)
        cx, cy, cc = lax.axis_index("x"), lax.axis_index("y"), lax.axis_index("c")
        me = 2 * cx + cy
        own = pltpu.make_async_copy(parts_ref.at[me], got_ref.at[me], local_sem)

        def copy(k, send):
            dx, dy = masks[k]
            peer = 2 * (cx ^ dx) + (cy ^ dy)
            return pltpu.make_async_remote_copy(
                src_ref=parts_ref.at[peer if send else me], dst_ref=got_ref.at[me if send else peer],
                send_sem=send_sems.at[k], recv_sem=recv_sems.at[k], device_id=(cx ^ dx, cy ^ dy, cc), device_id_type=MESH)

        @pl.when(i == 0)
        def _():
            dsh_ref[...] = jnp.zeros(dsh_ref.shape, F32)
            dsc_ref[...] = jnp.zeros(dsc_ref.shape, F32)
            dg_ref[...] = jnp.zeros(dg_ref.shape, F32)
            own.start()
            for k in range(len(masks)):
                copy(k, True).start()

        @pl.when(i == n_steps - 1)
        def _():
            for k in range(len(masks)):
                copy(k, False).wait_recv()
            for k in range(len(masks)):
                copy(k, True).wait_send()
            own.wait()

        dh = _dot(dp_ref[...], wt_ref[...])
        xhat, rstd = _rms_hat(x_ref[...])
        g, sc = g_ref[...], sc_ref[...]
        dsh_ref[...] += jnp.sum(dh, axis=0, keepdims=True)
        dhx = dh * xhat
        dsc_ref[...] += jnp.sum(dhx * g, axis=0, keepdims=True)
        dg_ref[...] += jnp.sum(dhx * sc, axis=0, keepdims=True)
        dxhat = dh * (g * sc)
        gx_ref[...] = dout_ref[...] + rstd * (dxhat - xhat * jnp.mean(dxhat * xhat, axis=1, keepdims=True))

    row = lambda w: pl.BlockSpec((tm, w), lambda i: (i, 0))
    vec = _full((1, D_MODEL))
    vshape = jax.ShapeDtypeStruct((1, D_MODEL), F32)
    hbm = pl.BlockSpec(memory_space=pl.ANY)
    return _call(
        body, name="in_proj_bwd_x",
        out_shape=(jax.ShapeDtypeStruct((s, D_MODEL), F32), vshape, vshape, vshape,
                   jax.ShapeDtypeStruct(parts.shape, parts.dtype)),
        grid=(n_steps,),
        in_specs=[row(WP), _full((WP, D_MODEL)), row(D_MODEL), row(D_MODEL), vec, vec, hbm],
        out_specs=(row(D_MODEL), vec, vec, vec, hbm),
        scratch_shapes=[pltpu.SemaphoreType.DMA((3,)), pltpu.SemaphoreType.DMA((3,)), pltpu.SemaphoreType.DMA],
        compiler_params=_params(("arbitrary",), has_side_effects=True),
    )(dproj, w_al_t, x, dout, g_pre, scale1p, parts)


def _in_proj_bwd_w(h, dproj, tk, tn):
    s = h.shape[0]
    n_k = s // tk

    def body(h_ref, dp_ref, gw_ref, acc_scr):
        k = pl.program_id(1)

        @pl.when(k == 0)
        def _():
            acc_scr[...] = jnp.zeros(acc_scr.shape, F32)

        acc_scr[...] += _dot_tn(h_ref[...], dp_ref[...])

        @pl.when(k == n_k - 1)
        def _():
            gw_ref[...] = acc_scr[...].astype(BF)

    return _call(
        body, name="in_proj_bwd_w", out_shape=jax.ShapeDtypeStruct((D_MODEL, WP), BF), grid=(WP // tn, n_k),
        in_specs=[pl.BlockSpec((tk, D_MODEL), lambda n, k: (k, 0)), pl.BlockSpec((tk, tn), lambda n, k: (k, n))],
        out_specs=pl.BlockSpec((D_MODEL, tn), lambda n, k: (0, n)),
        scratch_shapes=[pltpu.VMEM((D_MODEL, tn), F32)],
        compiler_params=_params(("parallel", "arbitrary")),
    )(h, dproj)


def _align_w_in(w_cols):
    def part(name, width):
        return w_cols[:, _SRC[name]:_SRC[name] + width]

    fpad = jnp.pad(part("fa", FOX_HEADS), ((0, 0), (0, LANES - FOX_HEADS)))
    return jnp.concatenate([part("qa", FOX_W), part("ka", FOX_W), part("va", FOX_W), part("za", FOX_W),
                            part("qb", SWA_W), part("zb", SWA_W), part("kb", SWA_KV_W), part("vb", SWA_KV_W), fpad], axis=1)


def _unalign_w_in(g_al):
    def part(c0, width):
        return g_al[:, c0:c0 + width]

    return jnp.concatenate([part(C_QA, FOX_W), part(C_KA, FOX_W), part(C_VA, FOX_W), part(C_F, FOX_HEADS),
                            part(C_ZA, FOX_W), part(C_QB, SWA_W), part(C_KB, SWA_KV_W), part(C_VB, SWA_KV_W),
                            part(C_ZB, SWA_W)], axis=1)


def _rope_tables(positions):
    inv_freq = ROPE_THETA ** (-jnp.arange(HALF, dtype=F32) / HALF)
    ang = positions.astype(F32)[:, None] * inv_freq
    cos, sin = jnp.cos(ang), jnp.sin(ang)
    return jnp.concatenate([cos, cos, cos, cos], axis=1), jnp.concatenate([-sin, sin, -sin, sin], axis=1)


def _tiles(s):
    if s >= 4096:
        return dict(tm=512, blk=512, bq=2048, bk=2048, bk_bwd=2048, chunk=256, tq=256, tm_out=512, tk=1024, tn=1152)
    return dict(tm=128, blk=128, bq=256, bk=256, bk_bwd=256, chunk=128, tq=128, tm_out=128, tk=128, tn=1152)


def kernel(x, c, positions, w_ada, b_ada, g_pre, w_in, b_fgate, sinks, w_out, g_post, loss_target, m_w_ada, m_b_ada, m_g_pre, m_w_in, m_b_fgate, m_sinks, m_w_out, m_g_post, v_w_ada, v_b_ada, v_g_pre, v_w_in, v_b_fgate, v_sinks, v_w_out, v_g_post):
    s = x.shape[1]
    t = _tiles(s)
    nc = s // LANES
    rows = FOX_HEADS * nc
    me = 4 * lax.axis_index("x") + 2 * lax.axis_index("y") + lax.axis_index("c")
    chip = 2 * lax.axis_index("x") + lax.axis_index("y")
    x2, tgt = x[0], loss_target[0]

    a_all, mod_all = _ada_exchange(c, w_ada[0])
    mod_rows = lax.dynamic_index_in_dim(mod_all, me, axis=1, keepdims=False)
    mod = mod_rows.reshape(N_CHIPS, 2, W_ADA_SHARD)[:, 0, :].reshape(1, 3 * D_MODEL) + b_ada
    shift, scale1p, gate = mod[:, :D_MODEL], 1.0 + mod[:, D_MODEL:2 * D_MODEL], mod[:, 2 * D_MODEL:]

    w_in_pad = jnp.pad(w_in[0].astype(BF), ((0, 0), (0, W_IN_SHARD_PAD - W_IN_SHARD)))
    w_all = _allgather_chips(w_in_pad.reshape(2, D_MODEL // 2, -1), "gather_w_in").reshape(N_CHIPS, D_MODEL, -1)
    w_cols = jnp.concatenate([w_all[k, :, :W_IN_SHARD] for k in range(N_CHIPS)], axis=1)
    w_al = _align_w_in(w_cols)
    w_al_t = w_al.T

    cos_t, sin_t = _rope_tables(positions[0])

    f_pad = _forget_logits(x2, g_pre, scale1p, shift, w_al[:, C_F:], t["tk"])
    f_rows = f_pad[:, :FOX_HEADS].T.reshape(rows, LANES)
    bias_rows = jnp.repeat(b_fgate[0], nc)[:, None]
    cum = _log_forget_cumsum(f_rows, bias_rows, nc).reshape(FOX_HEADS, s)
    h, qat, ka, kat, va, vat, za, zb, qb, kb, vb, qbt, kbt, vbt, m_own, w_out_all = _in_proj(
        x2, g_pre, scale1p, shift, w_al[:, C_VA:C_F], w_al_t[:C_ZA], cum, cos_t, sin_t,
        w_out[0].astype(BF).reshape(2, W_OUT_SHARD // 2, D_MODEL), t["tm"])
    w_out_all = w_out_all.reshape(D_MODEL, D_MODEL)
    w_out_t = w_out_all.T
    m_own = m_own[:, None, :]
    fox_args = (qat, ka, vat, m_own, t["bq"], t["bk"], t["chunk"])
    oat, lse, bad, pt = _fox_fwd(*fox_args, running_max=False)
    overflowed = jnp.max(bad) > 0.0
    oat, lse = lax.cond(overflowed, lambda: _fox_fwd(*fox_args, running_max=True)[:2], lambda: (oat, lse))
    inv_l = jnp.where(overflowed, 1.0, jnp.exp2(m_own - lse))
    sinks_g = sinks.reshape(SWA_KV_HEADS, 1, SWA_GROUP)
    obt = _swa_fwd(qbt, kb, vbt, sinks_g, t["tq"])

    dout, doat, delta_a, dza, dob, dobt, dzb, gw_out, dgate, dg_post, loss_part = _out_proj(
        oat, za, obt, zb, x2, tgt, w_out_all, w_out_t, gate, g_post, inv_l, t["tm_out"])

    bwd_args = (qat, ka, kat, va, doat, lse, delta_a)
    bwd_tiles = (t["bq"], t["bk_bwd"], t["chunk"], t["blk"])
    dqt, dkt, dvt = lax.cond(overflowed, lambda: _fox_bwd(*bwd_args, None, *bwd_tiles),
                             lambda: _fox_bwd(*bwd_args, pt, *bwd_tiles))
    dcum = dqt[:, :, HEAD_DIM, :].reshape(FOX_HEADS, s) - dkt[:, HEAD_DIM, :]
    df_rows, db_heads = _log_forget_cumsum_bwd(dcum.reshape(rows, LANES), f_rows, bias_rows, nc)
    df_pad = jnp.pad(df_rows.reshape(FOX_HEADS, s).T, ((0, 0), (0, LANES - FOX_HEADS)))
    dqb, dkb, dvb, dsinks = _swa_bwd(qb, qbt, kb, kbt, vb, sinks_g, dob, dobt, t["tq"])

    dproj = _assemble_dproj(dqt, dkt, dvt, dza, dqb, dzb, dkb, dvb, df_pad, cos_t, sin_t, t["blk"])
    gw_in = _unalign_w_in(_in_proj_bwd_w(h, dproj, t["tk"], t["tn"]))

    gin = jnp.stack([jnp.pad(gw_in[:, k * W_IN_SHARD:(k + 1) * W_IN_SHARD], ((0, 0), (0, W_IN_SHARD_PAD - W_IN_SHARD)))
                     for k in range(N_CHIPS)])
    gout = gw_out.astype(BF).reshape(N_CHIPS, D_MODEL, W_OUT_SHARD)
    gbig = jnp.concatenate([gin, gout], axis=2)
    pair = _pair_sum(gbig, "pair_sum")
    grad_x, dshift, dscale, dg_pre, from_chips = _in_proj_bwd_x(
        dproj, w_al_t, x2, dout, g_pre, scale1p, t["tm_out"], pair)

    pad_lane = lambda vrow: jnp.pad(vrow, ((0, 0), (0, LANES - vrow.shape[1])))
    packed = jnp.concatenate([dshift, dscale, dgate, dg_pre, dg_post,
                              pad_lane(db_heads.reshape(1, FOX_HEADS)), pad_lane(dsinks.reshape(1, FOX_HEADS)),
                              pad_lane(loss_part)], axis=1)
    parts = _allgather_devices(packed, "gather_partials")
    tot = _sum_devices(parts)
    loss = tot[0, P_LOSS]
    g_b_ada = tot[:, P_DMOD:P_DMOD + 3 * D_MODEL]
    g_g_pre = tot[:, P_GPRE:P_GPRE + D_MODEL]
    g_g_post = tot[:, P_GPOST:P_GPOST + D_MODEL]
    g_b_fgate = tot[:, P_BF:P_BF + FOX_HEADS]
    g_sinks = tot[:, P_SINK:P_SINK + FOX_HEADS]
    dm_shard = lax.dynamic_slice_in_dim(parts[:, 0, :3 * D_MODEL], chip * W_ADA_SHARD, W_ADA_SHARD, axis=1)
    g_w_ada = _grad_w_ada(a_all.T, dm_shard)

    gfull = _sum_chips_and_share(from_chips, "sum_chips_and_share")
    g_w_in = gfull[:, :W_IN_SHARD]
    g_w_out = gfull[:, W_IN_SHARD_PAD:].reshape(W_OUT_SHARD, D_MODEL)

    grads = dict(w_ada=g_w_ada, b_ada=g_b_ada, g_pre=g_g_pre, w_in=g_w_in, b_fgate=g_b_fgate, sinks=g_sinks,
                 w_out=g_w_out, g_post=g_g_post)
    weights = dict(w_ada=w_ada, b_ada=b_ada, g_pre=g_pre, w_in=w_in, b_fgate=b_fgate, sinks=sinks, w_out=w_out, g_post=g_post)
    moms = dict(w_ada=m_w_ada, b_ada=m_b_ada, g_pre=m_g_pre, w_in=m_w_in, b_fgate=m_b_fgate, sinks=m_sinks, w_out=m_w_out, g_post=m_g_post)
    vars_ = dict(w_ada=v_w_ada, b_ada=v_b_ada, g_pre=v_g_pre, w_in=v_w_in, b_fgate=v_b_fgate, sinks=v_sinks, w_out=v_w_out, g_post=v_g_post)
    names = ["w_ada", "b_ada", "g_pre", "w_in", "b_fgate", "sinks", "w_out", "g_post"]
    g_out, d_out, m_out, v_out = [], [], [], []
    for n in names:
        if n == "w_in":
            flat = lambda a: jnp.transpose(a, (2, 0, 1)).reshape(W_IN_SHARD * D_MODEL // LANES, LANES)
            unflat = lambda a: jnp.transpose(a.reshape(W_IN_SHARD, 1, D_MODEL), (1, 2, 0))
            outs = _adamw(flat(w_in), flat(grads[n][None]), flat(moms[n]), flat(vars_[n]), "adamw_" + n)
            go, d, nm, nv = (unflat(a) for a in outs)
        else:
            g2 = grads[n].reshape(weights[n].shape[-2:])
            go, d, nm, nv = _adamw(weights[n], g2, moms[n], vars_[n], "adamw_" + n)
        g_out.append(go)
        d_out.append(d)
        m_out.append(nm)
        v_out.append(nv)
    return (loss, grad_x.reshape(x.shape), *g_out, *d_out, *m_out, *v_out)
```

```python
import jax
import jax.numpy as jnp
from jax import lax
from jax.experimental import pallas as pl
from jax.experimental.pallas import tpu as pltpu

_INTERPRET = False

D_MODEL = 1024
HEAD_DIM = 64
HALF = HEAD_DIM // 2
AUG_DIM = 128
AUG_ROWS = 8
VT_ROWS = 80
LOG2E = 1.4426950408889634
LN2 = 0.6931471805599453
Q_SCALE = LOG2E * 64 ** -0.5
FOX_HEADS = 8
FOX_W = 512
SWA_W = 512
SWA_KV_HEADS = 2
SWA_GROUP = 4
SWA_KV_W = 128
WINDOW = 128
ROPE_THETA = 10000.0
RMS_EPS = 1e-6
IN_WIDTH = 3336
N_CHIPS = 4
N_DEV = 8
W_IN_SHARD = IN_WIDTH // N_CHIPS
W_IN_SHARD_PAD = 896
W_ADA_SHARD = 3 * D_MODEL // N_CHIPS
W_OUT_SHARD = D_MODEL // N_CHIPS
LANES = 128

_SRC = dict(qa=0, ka=512, va=1024, fa=1536, za=1544, qb=2056, kb=2568, vb=2696, zb=2824)
C_QA, C_KA, C_VA, C_ZA, C_QB, C_ZB, C_KB, C_VB, C_F = 0, 512, 1024, 1536, 2048, 2560, 3072, 3200, 3328
WP = 3456

ADAM_LR = 0.001
ADAM_B1 = 0.9
ADAM_B2 = 0.999
ADAM_EPS = 1e-08
ADAM_WD = 0.01
ADAM_STEP = 10
ADAMW_BLOCK_ELEMS = 300_000

VMEM_LIMIT = 56 * 1024 * 1024
NEG = -1e30
OVERFLOW_GUARD = 1e30
MESH = pl.DeviceIdType.MESH
BF = jnp.bfloat16
F32 = jnp.float32

P_DMOD, P_GPRE, P_GPOST, P_BF, P_SINK, P_LOSS, P_LEN = 0, 3072, 4096, 5120, 5248, 5376, 5504


def _call(body, **kw):
    return pl.pallas_call(body, interpret=_INTERPRET, **kw)


def _params(sem=None, **kw):
    return pltpu.CompilerParams(dimension_semantics=sem, vmem_limit_bytes=VMEM_LIMIT, **kw)


def _full(shape):
    zeros = (0,) * len(shape)
    return pl.BlockSpec(shape, lambda *_: zeros)


def _dot(a, b):
    return jnp.dot(a, b, preferred_element_type=F32)


def _dot_nt(a, b):
    return lax.dot_general(a, b, (((1,), (1,)), ((), ())), preferred_element_type=F32)


def _dot_tn(a, b):
    return lax.dot_general(a, b, (((0,), (0,)), ((), ())), preferred_element_type=F32)


def _sigmoid(z):
    return 1.0 / (1.0 + jnp.exp(-z))


def _rope_partner(t):
    w = t.shape[-1]
    lane = lax.broadcasted_iota(jnp.int32, t.shape, t.ndim - 1)
    return jnp.where((lane & (HEAD_DIM - 1)) < HALF, pltpu.roll(t, w - HALF, t.ndim - 1), pltpu.roll(t, HALF, t.ndim - 1))


def _allgather_devices(v, name):
    r, cdim = v.shape
    masks = [(dx, dy, dc) for dx in (0, 1) for dy in (0, 1) for dc in (0, 1)][1:]

    def body(v_ref, out_ref, send_sems, recv_sems):
        x, y, c = lax.axis_index("x"), lax.axis_index("y"), lax.axis_index("c")
        me = 4 * x + 2 * y + c
        out_ref[me] = v_ref[...]
        copies = []
        for k, (dx, dy, dc) in enumerate(masks):
            cp = pltpu.make_async_remote_copy(
                src_ref=v_ref, dst_ref=out_ref.at[me], send_sem=send_sems.at[k], recv_sem=recv_sems.at[k],
                device_id=(x ^ dx, y ^ dy, c ^ dc), device_id_type=MESH)
            cp.start()
            copies.append(cp)
        for k, (dx, dy, dc) in enumerate(masks):
            peer = 4 * (x ^ dx) + 2 * (y ^ dy) + (c ^ dc)
            pltpu.make_async_remote_copy(
                src_ref=v_ref, dst_ref=out_ref.at[peer], send_sem=send_sems.at[k], recv_sem=recv_sems.at[k],
                device_id=(x ^ dx, y ^ dy, c ^ dc), device_id_type=MESH).wait_recv()
        for cp in copies:
            cp.wait_send()

    return _call(
        body, name=name, out_shape=jax.ShapeDtypeStruct((N_DEV, r, cdim), v.dtype),
        in_specs=[pl.BlockSpec(memory_space=pltpu.VMEM)], out_specs=pl.BlockSpec(memory_space=pltpu.VMEM),
        scratch_shapes=[pltpu.SemaphoreType.DMA((7,)), pltpu.SemaphoreType.DMA((7,))],
        compiler_params=pltpu.CompilerParams(has_side_effects=True),
    )(v)


CHIP_MASKS = [(1, 0), (0, 1), (1, 1)]
CHIP_GATHER_SEMS = [pltpu.SemaphoreType.DMA((2 * len(CHIP_MASKS),)), pltpu.SemaphoreType.DMA((2 * len(CHIP_MASKS),)),
                    pltpu.SemaphoreType.DMA]


def _chip_gather(v_ref, out_ref, send_sems, recv_sems, local_sem):
    n = len(CHIP_MASKS)
    x, y, c = lax.axis_index("x"), lax.axis_index("y"), lax.axis_index("c")
    me = 2 * x + y
    mine = pltpu.make_async_copy(v_ref, out_ref.at[me], local_sem)

    def copy(k, chip, half, to):
        return pltpu.make_async_remote_copy(
            src_ref=v_ref.at[half] if k < n else out_ref.at[chip, half], dst_ref=out_ref.at[chip, half],
            send_sem=send_sems.at[k], recv_sem=recv_sems.at[k], device_id=to, device_id_type=MESH)

    def start():
        mine.start()
        for k, (dx, dy) in enumerate(CHIP_MASKS):
            copy(k, me, c, (x ^ dx, y ^ dy, c)).start()

    def finish():
        passed = []
        for k, (dx, dy) in enumerate(CHIP_MASKS):
            peer = 2 * (x ^ dx) + (y ^ dy)
            copy(k, peer, c, (x, y, c)).wait_recv()
            cp = copy(n + k, peer, c, (x, y, 1 - c))
            cp.start()
            passed.append(cp)
        for k, (dx, dy) in enumerate(CHIP_MASKS):
            copy(n + k, 2 * (x ^ dx) + (y ^ dy), 1 - c, (x, y, c)).wait_recv()
        for k, (dx, dy) in enumerate(CHIP_MASKS):
            copy(k, me, c, (x ^ dx, y ^ dy, c)).wait_send()
        for cp in passed:
            cp.wait_send()
        mine.wait()

    return start, finish


def _allgather_chips(v, name):
    _, r, cdim = v.shape
    n = len(CHIP_MASKS)

    def body(v_ref, out_ref, send_sems, recv_sems, local_sem):
        start, finish = _chip_gather(v_ref, out_ref, send_sems, recv_sems, local_sem)
        start()
        finish()

    return _call(
        body, name=name, out_shape=jax.ShapeDtypeStruct((N_CHIPS, 2, r, cdim), v.dtype),
        in_specs=[pl.BlockSpec(memory_space=pl.ANY)], out_specs=pl.BlockSpec(memory_space=pl.ANY),
        scratch_shapes=[pltpu.SemaphoreType.DMA((2 * n,)), pltpu.SemaphoreType.DMA((2 * n,)), pltpu.SemaphoreType.DMA],
        compiler_params=pltpu.CompilerParams(has_side_effects=True),
    )(v)


def _ada_exchange(c, w_ada_shard):
    masks = [(dx, dy, dc) for dx in (0, 1) for dy in (0, 1) for dc in (0, 1)][1:]
    n = len(masks)

    def body(c_ref, w_ref, a_ref, mod_ref, c_all, send_sems, recv_sems):
        x, y, cc = lax.axis_index("x"), lax.axis_index("y"), lax.axis_index("c")
        me = 4 * x + 2 * y + cc

        def gather(src_ref, dst_ref, first):
            sends = []
            for k, (dx, dy, dc) in enumerate(masks):
                cp = pltpu.make_async_remote_copy(
                    src_ref=src_ref, dst_ref=dst_ref.at[me], send_sem=send_sems.at[first + k],
                    recv_sem=recv_sems.at[first + k], device_id=(x ^ dx, y ^ dy, cc ^ dc), device_id_type=MESH)
                cp.start()
                sends.append(cp)
            for k, (dx, dy, dc) in enumerate(masks):
                peer = 4 * (x ^ dx) + 2 * (y ^ dy) + (cc ^ dc)
                pltpu.make_async_remote_copy(
                    src_ref=src_ref, dst_ref=dst_ref.at[peer], send_sem=send_sems.at[first + k],
                    recv_sem=recv_sems.at[first + k], device_id=(x ^ dx, y ^ dy, cc ^ dc), device_id_type=MESH).wait_recv()
            return sends

        c_all[me] = c_ref[...]
        sends = gather(c_ref, c_all, 0)
        w_bf = w_ref[...].astype(BF)
        for d in range(N_DEV):
            cv = c_all[d]
            a = cv * _sigmoid(cv)
            a_ref[d:d + 1, :] = a
            mod_ref[me, d:d + 1, :] = _dot(a.astype(BF), w_bf)
        sends += gather(mod_ref.at[me], mod_ref, n)
        for cp in sends:
            cp.wait_send()

    vmem = pl.BlockSpec(memory_space=pltpu.VMEM)
    return _call(
        body, name="ada_exchange",
        out_shape=(jax.ShapeDtypeStruct((N_DEV, D_MODEL), F32), jax.ShapeDtypeStruct((N_DEV, N_DEV, W_ADA_SHARD), F32)),
        in_specs=[vmem, vmem], out_specs=(vmem, vmem),
        scratch_shapes=[pltpu.VMEM((N_DEV, 1, D_MODEL), F32), pltpu.SemaphoreType.DMA((2 * n,)),
                        pltpu.SemaphoreType.DMA((2 * n,))],
        compiler_params=_params(has_side_effects=True),
    )(c, w_ada_shard)


def _grad_w_ada(a_t, dm_shard):
    def body(a_ref, dm_ref, out_ref):
        acc = jnp.zeros((D_MODEL, W_ADA_SHARD), F32)
        for b in range(N_DEV):
            acc = acc + a_ref[:, b:b + 1] * dm_ref[b:b + 1, :]
        out_ref[...] = acc

    return _call(body, name="grad_w_ada", out_shape=jax.ShapeDtypeStruct((D_MODEL, W_ADA_SHARD), F32),
                 compiler_params=_params())(a_t, dm_shard)


def _sum_devices(parts):
    n = parts.shape[-1]

    def body(p_ref, out_ref):
        acc = p_ref[0]
        for b in range(1, N_DEV):
            acc = acc + p_ref[b]
        out_ref[...] = acc

    return _call(body, name="sum_devices", out_shape=jax.ShapeDtypeStruct((1, n), F32), compiler_params=_params())(parts)


def _pair_sum(part_a, part_b, name):
    n, r2, ca = part_a.shape
    cb = part_b.shape[2]
    r = r2 // 2
    tr = 128

    def body(a_ref, b_ref, o_ref, land_a, land_b, send_sems, recv_sems):
        x, y, c = lax.axis_index("x"), lax.axis_index("y"), lax.axis_index("c")
        mine = pl.multiple_of(c * r, r)
        theirs = pl.multiple_of((1 - c) * r, r)
        copies = []
        for k in range(n):
            for j, (src, land) in enumerate(((a_ref, land_a), (b_ref, land_b))):
                cp = pltpu.make_async_remote_copy(
                    src_ref=src.at[k, pl.ds(theirs, r), :], dst_ref=land.at[k], send_sem=send_sems.at[2 * k + j],
                    recv_sem=recv_sems.at[2 * k + j], device_id=(x, y, 1 - c), device_id_type=MESH)
                cp.start()
                copies.append(cp)
        for k in range(n):
            copies[2 * k].wait_recv()
            copies[2 * k + 1].wait_recv()

            def rows(i, carry, k=k):
                sl = pl.ds(pl.multiple_of(i * tr, tr), tr)
                own = pl.ds(pl.multiple_of(mine + i * tr, tr), tr)
                o_ref[k, sl, :ca] = (a_ref[k, own, :].astype(F32) + land_a[k, sl, :].astype(F32)).astype(BF)
                o_ref[k, sl, ca:] = (b_ref[k, own, :].astype(F32) + land_b[k, sl, :].astype(F32)).astype(BF)
                return carry

            lax.fori_loop(0, r // tr, rows, 0)
        for cp in copies:
            cp.wait_send()

    vmem = pl.BlockSpec(memory_space=pltpu.VMEM)
    return _call(body, name=name, out_shape=jax.ShapeDtypeStruct((n, r, ca + cb), BF), in_specs=[vmem, vmem],
                 out_specs=vmem,
                 scratch_shapes=[pltpu.VMEM((n, r, ca), BF), pltpu.VMEM((n, r, cb), BF),
                                 pltpu.SemaphoreType.DMA((2 * n,)), pltpu.SemaphoreType.DMA((2 * n,))],
                 compiler_params=_params(has_side_effects=True))(part_a, part_b)


def _sum_chips_and_share(parts, name):
    _, r, cdim = parts.shape
    tr = 128

    def body(p_ref, o_ref, send_sem, recv_sem):
        x, y, c = lax.axis_index("x"), lax.axis_index("y"), lax.axis_index("c")
        mine = pl.multiple_of(c * r, r)

        def rows(n, carry):
            sl = pl.ds(pl.multiple_of(n * tr, tr), tr)
            p = [p_ref[k, sl, :].astype(F32) for k in range(N_CHIPS)]
            o_ref[pl.ds(pl.multiple_of(mine + n * tr, tr), tr), :] = ((p[0] + p[1]) + p[2]) + p[3]
            return carry

        lax.fori_loop(0, r // tr, rows, 0)
        half = o_ref.at[pl.ds(mine, r), :]
        cp = pltpu.make_async_remote_copy(src_ref=half, dst_ref=half, send_sem=send_sem, recv_sem=recv_sem,
                                          device_id=(x, y, 1 - c), device_id_type=MESH)
        cp.start()
        other = o_ref.at[pl.ds(pl.multiple_of((1 - c) * r, r), r), :]
        pltpu.make_async_remote_copy(src_ref=other, dst_ref=other, send_sem=send_sem, recv_sem=recv_sem,
                                     device_id=(x, y, 1 - c), device_id_type=MESH).wait_recv()
        cp.wait_send()

    vmem = pl.BlockSpec(memory_space=pltpu.VMEM)
    return _call(body, name=name, out_shape=jax.ShapeDtypeStruct((2 * r, cdim), F32), in_specs=[vmem], out_specs=vmem,
                 scratch_shapes=[pltpu.SemaphoreType.DMA, pltpu.SemaphoreType.DMA],
                 compiler_params=_params(has_side_effects=True))(parts)


def _adamw(w, g, m, v, name):
    r, cdim = w.shape[-2:]
    lead = w.ndim - 2
    tr = r if r <= 256 else max(t for t in range(8, ADAMW_BLOCK_ELEMS // cdim + 1, 8) if r % t == 0)
    c1 = 1.0 / (1.0 - ADAM_B1 ** ADAM_STEP)
    c2 = 1.0 / (1.0 - ADAM_B2 ** ADAM_STEP)

    def body(w_ref, g_ref, m_ref, v_ref, go_ref, d_ref, nm_ref, nv_ref):
        gv = g_ref[...].reshape(go_ref.shape)
        nm = ADAM_B1 * m_ref[...] + (1.0 - ADAM_B1) * gv
        nv = ADAM_B2 * v_ref[...] + (1.0 - ADAM_B2) * (gv * gv)
        m_hat = nm * c1
        v_hat = nv * c2
        go_ref[...] = gv
        d_ref[...] = -ADAM_LR * (m_hat / (jnp.sqrt(v_hat) + ADAM_EPS) + ADAM_WD * w_ref[...])
        nm_ref[...] = nm
        nv_ref[...] = nv

    spec = pl.BlockSpec((1,) * lead + (tr, cdim), lambda i: (0,) * lead + (i, 0))
    shp = jax.ShapeDtypeStruct(w.shape, F32)
    return _call(body, name=name, out_shape=(shp,) * 4, grid=(r // tr,),
                 in_specs=[spec, pl.BlockSpec((tr, cdim), lambda i: (i, 0)), spec, spec],
                 out_specs=(spec,) * 4, compiler_params=_params(("parallel",)))(w, g, m, v)


def _head_of_row(r, nc):
    assert nc & (nc - 1) == 0
    return lax.shift_right_logical(r, nc.bit_length() - 1)


def _chunk_mats(rows, nc, reverse):
    ri = lax.broadcasted_iota(jnp.int32, (rows, rows), 0)
    ci = lax.broadcasted_iota(jnp.int32, (rows, rows), 1)
    same = _head_of_row(ri, nc) == _head_of_row(ci, nc)
    between = jnp.where(same & ((ci > ri) if reverse else (ci < ri)), 1.0, 0.0).astype(F32)
    li = lax.broadcasted_iota(jnp.int32, (LANES, LANES), 0)
    lj = lax.broadcasted_iota(jnp.int32, (LANES, LANES), 1)
    within = jnp.where((li >= lj) if reverse else (li <= lj), 1.0, 0.0).astype(F32)
    return between, within


def _dot_hi(a, b):
    return jnp.dot(a, b, preferred_element_type=F32, precision=lax.Precision.HIGHEST)


def _scan_rows(t, nc, reverse):
    between, within = _chunk_mats(t.shape[0], nc, reverse)
    inner = _dot_hi(t, within)
    tot = jnp.sum(t, axis=1, keepdims=True)
    return inner + _dot_hi(between, jnp.broadcast_to(tot, t.shape))


def _log_forget_cumsum(f_rows, bias_rows, nc):
    def body(f_ref, b_ref, cum_ref):
        z = f_ref[...] + b_ref[...]
        lf = jnp.minimum(z, 0.0) - jnp.log(1.0 + jnp.exp(-jnp.abs(z)))
        cum_ref[...] = _scan_rows(lf, nc, False)

    return _call(body, name="forget_cumsum", out_shape=jax.ShapeDtypeStruct(f_rows.shape, F32),
                 compiler_params=_params())(f_rows, bias_rows)


def _log_forget_cumsum_bwd(dcum_rows, f_rows, bias_rows, nc):
    rows = f_rows.shape[0]

    def body(d_ref, f_ref, b_ref, df_ref, db_ref):
        dlf = _scan_rows(d_ref[...], nc, True)
        z = f_ref[...] + b_ref[...]
        df = dlf * _sigmoid(-z)
        df_ref[...] = df
        hi = lax.broadcasted_iota(jnp.int32, (FOX_HEADS, rows), 0)
        ri = lax.broadcasted_iota(jnp.int32, (FOX_HEADS, rows), 1)
        sel = jnp.where(_head_of_row(ri, nc) == hi, 1.0, 0.0).astype(F32)
        db_ref[...] = jnp.sum(_dot_hi(sel, df), axis=1, keepdims=True)

    return _call(body, name="forget_cumsum_bwd",
                 out_shape=(jax.ShapeDtypeStruct(f_rows.shape, F32), jax.ShapeDtypeStruct((FOX_HEADS, 1), F32)),
                 compiler_params=_params())(dcum_rows, f_rows, bias_rows)


def _rms_hat(xv):
    rstd = lax.rsqrt(jnp.mean(xv * xv, axis=-1, keepdims=True) + RMS_EPS)
    return xv * rstd, rstd


def _modulated(x_ref, g_ref, sc_ref, sh_ref):
    xhat, _ = _rms_hat(x_ref[...])
    return ((xhat * g_ref[...]) * sc_ref[...] + sh_ref[...]).astype(BF)


def _forget_logits(x, g_pre, scale1p, shift, w_f, tm):
    s = x.shape[0]

    def body(x_ref, g_ref, sc_ref, sh_ref, w_ref, f_ref):
        f_ref[...] = _dot(_modulated(x_ref, g_ref, sc_ref, sh_ref), w_ref[...])

    vec = _full((1, D_MODEL))
    return _call(
        body, name="forget_logits", out_shape=jax.ShapeDtypeStruct((s, LANES), F32), grid=(s // tm,),
        in_specs=[pl.BlockSpec((tm, D_MODEL), lambda i: (i, 0)), vec, vec, vec, _full((D_MODEL, LANES))],
        out_specs=pl.BlockSpec((tm, LANES), lambda i: (i, 0)), compiler_params=_params(("parallel",)),
    )(x, g_pre, scale1p, shift, w_f)


def _split3(v):
    hi = v.astype(BF).astype(F32)
    mid = (v - hi).astype(BF).astype(F32)
    lo = ((v - hi) - mid).astype(BF).astype(F32)
    return hi, mid, lo


def _in_proj(x, g_pre, scale1p, shift, w_rows, w_t_fox, cum, cos_t, sin_t, w_out_halves, tm):
    s = x.shape[0]
    r_va, r_za, r_qb, r_zb, r_kb, r_vb = 0, 512, 1024, 1536, 2048, 2176

    def body(x_ref, g_ref, sc_ref, sh_ref, w_ref, wt_ref, cum_ref, cos_ref, sin_ref, wo_ref,
             h_ref, qat_ref, ka_ref, kat_ref, v_ref, vt_ref, za_ref, zb_ref, qb_ref, kb_ref, vb_ref,
             qbt_ref, kbt_ref, vbt_ref, mo_ref, wo_all_ref, send_sems, recv_sems, local_sem):
        start_gather, finish_gather = _chip_gather(wo_ref, wo_all_ref, send_sems, recv_sems, local_sem)

        @pl.when(pl.program_id(0) == 0)
        def _():
            start_gather()

        @pl.when(pl.program_id(0) == s // tm - 1)
        def _():
            finish_gather()

        hb = _modulated(x_ref, g_ref, sc_ref, sh_ref)
        h_ref[...] = hb

        def sec(c0, width):
            return _dot(hb, w_ref[:, c0:c0 + width])

        def sec_t(r0):
            return _dot_nt(wt_ref[r0:r0 + FOX_W, :], hb)

        q_t = sec_t(0) * Q_SCALE
        k_t = sec_t(FOX_W)
        v_t = sec_t(2 * FOX_W)
        va = sec(r_va, FOX_W)
        zeros = jnp.zeros((AUG_DIM - HEAD_DIM - AUG_ROWS, tm), F32)
        ri = lax.broadcasted_iota(jnp.int32, (AUG_ROWS, tm), 0)
        const = jnp.where(ri == AUG_ROWS - 1, 0.0, 1.0)
        ri_v = lax.broadcasted_iota(jnp.int32, (VT_ROWS - HEAD_DIM, tm), 0)
        v_feat = jnp.where(ri_v == 0, 1.0, 0.0).astype(BF)
        for hd in range(FOX_HEADS):
            rows = slice(hd * HEAD_DIM, (hd + 1) * HEAD_DIM)
            cum2 = cum_ref[hd:hd + 1, :] * LOG2E
            hi, mid, lo = (jnp.broadcast_to(part, (AUG_ROWS, tm)) for part in _split3(cum2))
            q_feat = jnp.where(ri == 1, hi, jnp.where(ri == 2, mid, jnp.where(ri == 3, lo, const)))
            k_feat = jnp.where(ri == 4, -hi, jnp.where(ri == 5, -mid, jnp.where(ri == 6, -lo, const)))
            q_aug = jnp.concatenate([q_t[rows], q_feat, zeros], axis=0)
            k_aug = jnp.concatenate([k_t[rows], k_feat, zeros], axis=0)
            mo_ref[hd:hd + 1, :] = jnp.sum(q_t[rows] * k_t[rows], axis=0, keepdims=True) + 1.0
            qat_ref[hd] = q_aug.astype(BF)
            kat_ref[hd] = k_aug.astype(BF)
            ka_ref[hd] = k_aug.T.astype(BF)
            vt_ref[hd] = jnp.concatenate([v_t[rows].astype(BF), v_feat], axis=0)
            v_ref[hd] = va[:, rows].astype(BF)
        za_ref[...] = sec(r_za, FOX_W)
        zb_ref[...] = sec(r_zb, SWA_W)
        cos2, sin2 = cos_ref[...], sin_ref[...]
        cos8 = jnp.concatenate([cos2] * 4, axis=1)
        sin8 = jnp.concatenate([sin2] * 4, axis=1)
        qb = sec(r_qb, SWA_W)
        qb = (qb * cos8 + _rope_partner(qb) * sin8) * (HEAD_DIM ** -0.5)
        qb_ref[...] = qb.astype(BF)
        for a in range(SWA_W // LANES):
            qbt_ref[a * LANES:(a + 1) * LANES, :] = qb[:, a * LANES:(a + 1) * LANES].T.astype(BF)
        kb = sec(r_kb, SWA_KV_W)
        kb = kb * cos2 + _rope_partner(kb) * sin2
        vb = sec(r_vb, SWA_KV_W)
        kb_t, vb_t = kb.T, vb.T
        for hd in range(SWA_KV_HEADS):
            sl = slice(hd * HEAD_DIM, (hd + 1) * HEAD_DIM)
            kb_ref[hd] = kb[:, sl].astype(BF)
            vb_ref[hd] = vb[:, sl].astype(BF)
            kbt_ref[hd] = kb_t[sl].astype(BF)
            vbt_ref[hd] = jnp.concatenate([vb_t[sl].astype(BF), v_feat], axis=0)

    row = lambda w: pl.BlockSpec((tm, w), lambda i: (i, 0))
    heads = lambda n, w=HEAD_DIM: pl.BlockSpec((n, tm, w), lambda i: (0, i, 0))
    heads_t = lambda w: pl.BlockSpec((FOX_HEADS, w, tm), lambda i: (0, 0, i))
    vec = _full((1, D_MODEL))
    hs = lambda a, b: jax.ShapeDtypeStruct((FOX_HEADS, a, b), BF)
    out_shape = (
        jax.ShapeDtypeStruct((s, D_MODEL), BF),
        hs(AUG_DIM, s), hs(s, AUG_DIM), hs(AUG_DIM, s), hs(s, HEAD_DIM), hs(VT_ROWS, s),
        jax.ShapeDtypeStruct((s, FOX_W), F32), jax.ShapeDtypeStruct((s, SWA_W), F32),
        jax.ShapeDtypeStruct((s, SWA_W), BF),
        jax.ShapeDtypeStruct((SWA_KV_HEADS, s, HEAD_DIM), BF), jax.ShapeDtypeStruct((SWA_KV_HEADS, s, HEAD_DIM), BF),
        jax.ShapeDtypeStruct((SWA_W, s), BF),
        jax.ShapeDtypeStruct((SWA_KV_HEADS, HEAD_DIM, s), BF), jax.ShapeDtypeStruct((SWA_KV_HEADS, VT_ROWS, s), BF),
        jax.ShapeDtypeStruct((FOX_HEADS, s), F32),
        jax.ShapeDtypeStruct((N_CHIPS,) + w_out_halves.shape, w_out_halves.dtype),
    )
    kv_t = lambda w: pl.BlockSpec((SWA_KV_HEADS, w, tm), lambda i: (0, 0, i))
    hbm = pl.BlockSpec(memory_space=pl.ANY)
    return _call(
        body, name="in_proj", out_shape=out_shape, grid=(s // tm,),
        in_specs=[row(D_MODEL), vec, vec, vec, _full(w_rows.shape), _full(w_t_fox.shape),
                  pl.BlockSpec((FOX_HEADS, tm), lambda i: (0, i)), row(LANES), row(LANES), hbm],
        out_specs=(row(D_MODEL), heads_t(AUG_DIM), heads(FOX_HEADS, AUG_DIM), heads_t(AUG_DIM), heads(FOX_HEADS),
                   heads_t(VT_ROWS), row(FOX_W), row(SWA_W), row(SWA_W), heads(SWA_KV_HEADS), heads(SWA_KV_HEADS),
                   pl.BlockSpec((SWA_W, tm), lambda i: (0, i)), kv_t(HEAD_DIM), kv_t(VT_ROWS),
                   pl.BlockSpec((FOX_HEADS, tm), lambda i: (0, i)), hbm),
        scratch_shapes=list(CHIP_GATHER_SEMS),
        compiler_params=_params(("arbitrary",), has_side_effects=True),
    )(x, g_pre, scale1p, shift, w_rows, w_t_fox, cum, cos_t, sin_t, w_out_halves)


def _diag_chunks(d, bq, bk, chunk):
    out = []
    for c0 in range(0, bq, chunk):
        if d is None or d * bk + bk - 1 <= c0:
            out.append((c0, None, bk))
        elif d * bk <= c0 + chunk - 1:
            n_keys = min(bk, c0 + chunk - d * bk)
            kpos = d * bk + lax.broadcasted_iota(jnp.int32, (n_keys, chunk), 0)
            qpos = c0 + lax.broadcasted_iota(jnp.int32, (n_keys, chunk), 1)
            out.append((c0, kpos <= qpos, n_keys))
    return out


def _fox_fwd(qat, ka, vt, m_own, bq, bk, chunk, running_max):
    nh, _, s = qat.shape
    r = bq // bk

    pairs = [(i, j) for i in range(s // bq) for j in range(i * r + r)]

    def body(i_tab, j_tab, ka_ref, qat_ref, vt_ref, mo_ref, o_ref, lse_ref, bad_ref, *rest):
        pt_ref, m_scr, acc_scr = (None,) * running_max + rest
        i, j = i_tab[pl.program_id(1)], j_tab[pl.program_id(1)]

        @pl.when(j == 0)
        def _():
            m_scr[...] = jnp.full(m_scr.shape, NEG, F32) if running_max else mo_ref[0]
            acc_scr[...] = jnp.zeros(acc_scr.shape, F32)

        def careful(d):
            kv, vtv = ka_ref[0], vt_ref[0]

            def one_chunk(n, carry):
                c0 = pl.multiple_of(n * chunk, chunk)
                cs = pl.ds(c0, chunk)
                sc = _dot(kv, qat_ref[0, :, cs])
                if d is not None:
                    kpos = d * bk + lax.broadcasted_iota(jnp.int32, (bk, chunk), 0)
                    qpos = c0 + lax.broadcasted_iota(jnp.int32, (bk, chunk), 1)
                    sc = jnp.where(kpos <= qpos, sc, NEG)
                m_prev = m_scr[:, cs]
                m_new = jnp.maximum(m_prev, jnp.max(sc, axis=0, keepdims=True))
                p = jnp.exp2(sc - m_new).astype(BF)
                acc_scr[:, cs] = jnp.exp2(m_prev - m_new) * acc_scr[:, cs] + _dot(vtv, p)
                m_scr[:, cs] = m_new
                return carry

            lax.fori_loop(0, bq // chunk, one_chunk, 0)

        def fast(d):
            todo = _diag_chunks(d, bq, bk, chunk)
            scores = lambda t: _dot(ka_ref[0, :t[2], :], qat_ref[0, :, t[0]:t[0] + chunk])
            sc_next = scores(todo[0])
            for n, (c0, mask, n_keys) in enumerate(todo):
                cs = slice(c0, c0 + chunk)
                sc = sc_next
                if n + 1 < len(todo):
                    sc_next = scores(todo[n + 1])
                if mask is not None:
                    sc = jnp.where(mask, sc, NEG)
                p = jnp.exp2(sc - m_scr[:, cs]).astype(BF)
                pt_ref[0, :n_keys, cs] = p
                acc_scr[:, cs] += _dot(vt_ref[0, :, :n_keys], p)

        step = careful if running_max else fast

        @pl.when(j < i * r)
        def _():
            step(None)

        for d in range(r):
            @pl.when(j == i * r + d)
            def _(d=d):
                step(d)

        @pl.when(j == i * r + r - 1)
        def _():
            l = acc_scr[HEAD_DIM:HEAD_DIM + 1, :]
            o_ref[0] = acc_scr[:HEAD_DIM, :] / l
            lse_ref[0] = m_scr[...] + jnp.log2(l)
            bad_ref[0] = jnp.where(l < OVERFLOW_GUARD, 0.0, 1.0)

    qmap_t = lambda h, t, it, jt: (h, 0, it[t])
    qrow = pl.BlockSpec((1, 1, bq), qmap_t)
    row_shape = jax.ShapeDtypeStruct((nh, 1, s), F32)
    out_shape = (jax.ShapeDtypeStruct((nh, HEAD_DIM, s), F32), row_shape, row_shape)
    out_specs = (pl.BlockSpec((1, HEAD_DIM, bq), qmap_t), qrow, qrow)
    if not running_max:
        out_shape += (jax.ShapeDtypeStruct((nh, s, s), BF),)
        out_specs += (pl.BlockSpec((1, bk, bq), lambda h, t, it, jt: (h, jt[t], it[t])),)
    grid_spec = pltpu.PrefetchScalarGridSpec(
        num_scalar_prefetch=2, grid=(nh, len(pairs)),
        in_specs=[pl.BlockSpec((1, bk, AUG_DIM), lambda h, t, it, jt: (h, jt[t], 0)), pl.BlockSpec((1, AUG_DIM, bq), qmap_t),
                  pl.BlockSpec((1, VT_ROWS, bk), lambda h, t, it, jt: (h, 0, jt[t])), qrow],
        out_specs=out_specs,
        scratch_shapes=[pltpu.VMEM((1, bq), F32), pltpu.VMEM((VT_ROWS, bq), F32)])
    return _call(
        body, name="fox_fwd_running_max" if running_max else "fox_fwd", out_shape=out_shape, grid_spec=grid_spec,
        compiler_params=_params(("parallel", "arbitrary")),
    )(jnp.asarray([p[0] for p in pairs], jnp.int32), jnp.asarray([p[1] for p in pairs], jnp.int32), ka, qat, vt, m_own)


def _fox_bwd(qat, ka, kat, v, dot_, lse, delta, pt, bq, bk, chunk, dq_blk):
    nh, _, s = qat.shape
    r = bq // bk
    nq = s // bq
    stored = pt is not None

    pairs = [(j, i) for j in range(s // bk) for i in range(j // r, nq)]

    def body(j_tab, i_tab, a_ref, b_ref, kat_ref, v_ref, qat_ref, do_ref, dl_ref, dq_ref, dk_ref, dv_ref, dk_scr, dv_scr):
        ka_ref, lse_ref, pt_ref = (None, None, a_ref) if stored else (a_ref, b_ref, None)
        j, i = j_tab[pl.program_id(1)], i_tab[pl.program_id(1)]

        @pl.when(pl.program_id(1) == 0)
        def _():
            dq_ref[...] = jnp.zeros(dq_ref.shape, F32)

        @pl.when(i * r <= j)
        def _():
            dk_scr[...] = jnp.zeros(dk_scr.shape, F32)
            dv_scr[...] = jnp.zeros(dv_scr.shape, F32)

        def step(d):
            todo = _diag_chunks(d, bq, bk, chunk)

            def products(t):
                cs = slice(t[0], t[0] + chunk)
                return (None if stored else _dot(ka_ref[0, :t[2], :], qat_ref[0, :, cs]),
                        _dot(v_ref[0, :t[2], :], do_ref[0, :, cs]))

            nxt = products(todo[0])
            for n, (c0, mask, n_keys) in enumerate(todo):
                cs = slice(c0, c0 + chunk)
                sc, dp = nxt
                if n + 1 < len(todo):
                    nxt = products(todo[n + 1])
                if stored:
                    p_bf = pt_ref[0, :n_keys, cs]
                    p = p_bf.astype(F32)
                else:
                    p = jnp.exp2(sc - lse_ref[0, :, cs])
                    if mask is not None:
                        p = jnp.where(mask, p, 0.0)
                    p_bf = p.astype(BF)
                ds = (p * (dp - dl_ref[0, :, cs])).astype(BF)
                dv_scr[:, :n_keys] += _dot_nt(do_ref[0, :, cs], p_bf)
                dk_scr[:, :n_keys] += _dot_nt(qat_ref[0, :VT_ROWS, cs], ds)
                c1 = c0 % dq_blk
                dq_ref[0, i * (bq // dq_blk) + c0 // dq_blk, :, c1:c1 + chunk] += _dot(kat_ref[0, :VT_ROWS, :n_keys], ds)

        @pl.when(i * r > j)
        def _():
            step(None)

        for d in range(r):
            @pl.when(j == i * r + d)
            def _(d=d):
                step(d)

        @pl.when(i == nq - 1)
        def _():
            dk_ref[0] = dk_scr[...]
            dv_ref[0] = dv_scr[...]

    qmap = lambda h, t, jt, it: (h, 0, it[t])
    kmap = lambda h, t, jt, it: (h, jt[t], 0)
    kmap_t = lambda h, t, jt, it: (h, 0, jt[t])
    if stored:
        first = [(pt, pl.BlockSpec((1, bk, bq), lambda h, t, jt, it: (h, jt[t], it[t]))),
                 (delta, pl.BlockSpec((1, 1, bq), qmap))]
    else:
        first = [(ka, pl.BlockSpec((1, bk, AUG_DIM), kmap)), (lse, pl.BlockSpec((1, 1, bq), qmap))]
    grid_spec = pltpu.PrefetchScalarGridSpec(
        num_scalar_prefetch=2, grid=(nh, len(pairs)),
        in_specs=[first[0][1], first[1][1], pl.BlockSpec((1, AUG_DIM, bk), kmap_t), pl.BlockSpec((1, bk, HEAD_DIM), kmap),
                  pl.BlockSpec((1, AUG_DIM, bq), qmap), pl.BlockSpec((1, HEAD_DIM, bq), qmap),
                  pl.BlockSpec((1, 1, bq), qmap)],
        out_specs=(pl.BlockSpec((1, s // dq_blk, VT_ROWS, dq_blk), lambda h, t, jt, it: (h, 0, 0, 0)),
                   pl.BlockSpec((1, VT_ROWS, bk), kmap_t), pl.BlockSpec((1, HEAD_DIM, bk), kmap_t)),
        scratch_shapes=[pltpu.VMEM((VT_ROWS, bk), F32), pltpu.VMEM((HEAD_DIM, bk), F32)])
    return _call(
        body, name="fox_bwd" if stored else "fox_bwd_recompute",
        out_shape=(jax.ShapeDtypeStruct((nh, s // dq_blk, VT_ROWS, dq_blk), F32),
                   jax.ShapeDtypeStruct((nh, VT_ROWS, s), F32), jax.ShapeDtypeStruct((nh, HEAD_DIM, s), F32)),
        grid_spec=grid_spec, compiler_params=_params(("parallel", "arbitrary")),
    )(jnp.asarray([p[0] for p in pairs], jnp.int32), jnp.asarray([p[1] for p in pairs], jnp.int32),
      first[0][0], first[1][0], kat, v, qat, dot_, delta)


def _swa_mask(i, tq):
    kpos = i * tq - WINDOW + lax.broadcasted_iota(jnp.int32, (tq + WINDOW, tq), 0)
    qpos = i * tq + lax.broadcasted_iota(jnp.int32, (tq + WINDOW, tq), 1)
    rel = qpos - kpos
    return (rel >= 0) & (rel < WINDOW) & (kpos >= 0)


def _swa_rows(ref, g, i, tq):
    before = pl.multiple_of(jnp.maximum(i * tq - WINDOW, 0), WINDOW)
    return jnp.concatenate([ref[g, pl.ds(before, WINDOW), :], ref[g, pl.ds(pl.multiple_of(i * tq, tq), tq), :]], axis=0)


def _swa_before(n_rows, tq):
    return pl.BlockSpec((SWA_KV_HEADS, n_rows, WINDOW), lambda i: (0, 0, jnp.maximum(i * (tq // WINDOW) - 1, 0)))


def _swa_probs_t(sc, mask, sink):
    sc = jnp.where(mask, sc, NEG)
    m = jnp.maximum(jnp.max(sc, axis=0, keepdims=True), sink)
    p = jnp.exp(sc - m)
    e_sink = jnp.exp(sink - m)
    inv_l = 1.0 / (jnp.sum(p, axis=0, keepdims=True) + e_sink)
    return p * inv_l, e_sink * inv_l


def _swa_fwd(qbt, kb, vbt, sinks, tq):
    s = qbt.shape[1]
    n_heads = SWA_KV_HEADS * SWA_GROUP

    def body(q_ref, k_ref, vb_ref, vc_ref, s_ref, o_ref):
        i = pl.program_id(0)
        mask = _swa_mask(i, tq)
        kw = [_swa_rows(k_ref, g, i, tq) for g in range(SWA_KV_HEADS)]
        vtw = [jnp.concatenate([vb_ref[g], vc_ref[g]], axis=1) for g in range(SWA_KV_HEADS)]
        scores = lambda hd: _dot(kw[hd // SWA_GROUP], q_ref[hd * HEAD_DIM:(hd + 1) * HEAD_DIM, :])
        sc_next = scores(0)
        for hd in range(n_heads):
            g, hh = divmod(hd, SWA_GROUP)
            rows = slice(hd * HEAD_DIM, (hd + 1) * HEAD_DIM)
            sink = s_ref[g][:, hh:hh + 1]
            sc = jnp.where(mask, sc_next, NEG)
            if hd + 1 < n_heads:
                sc_next = scores(hd + 1)
            m = jnp.maximum(jnp.max(sc, axis=0, keepdims=True), sink)
            acc = _dot(vtw[g], jnp.exp(sc - m).astype(BF))
            o_ref[rows, :] = acc[:HEAD_DIM] / (acc[HEAD_DIM:HEAD_DIM + 1] + jnp.exp(sink - m))

    kvspec = _full((SWA_KV_HEADS, s, HEAD_DIM))
    qspec = pl.BlockSpec((SWA_W, tq), lambda i: (0, i))
    return _call(
        body, name="swa_fwd", out_shape=jax.ShapeDtypeStruct((SWA_W, s), F32), grid=(s // tq,),
        in_specs=[qspec, kvspec, _swa_before(VT_ROWS, tq), pl.BlockSpec((SWA_KV_HEADS, VT_ROWS, tq), lambda i: (0, 0, i)),
                  _full((SWA_KV_HEADS, 1, SWA_GROUP))],
        out_specs=qspec, compiler_params=_params(("parallel",)),
    )(qbt, kb, vbt, vbt, sinks)


def _swa_bwd(qb, qbt, kb, kbt, vb, sinks, dob, dobt, tq):
    s = qb.shape[0]
    n_heads = SWA_KV_HEADS * SWA_GROUP

    def body(q_ref, qt_ref, k_ref, ktb_ref, ktc_ref, v_ref, s_ref, do_ref, dot_ref, dq_ref, dk_ref, dv_ref, ds_ref):
        i = pl.program_id(0)

        @pl.when(i == 0)
        def _():
            dk_ref[...] = jnp.zeros(dk_ref.shape, F32)
            dv_ref[...] = jnp.zeros(dv_ref.shape, F32)
            ds_ref[...] = jnp.zeros(ds_ref.shape, F32)

        mask = _swa_mask(i, tq)
        kw = [_swa_rows(k_ref, g, i, tq) for g in range(SWA_KV_HEADS)]
        vw = [_swa_rows(v_ref, g, i, tq) for g in range(SWA_KV_HEADS)]
        ktw = [jnp.concatenate([ktb_ref[g], ktc_ref[g]], axis=1) for g in range(SWA_KV_HEADS)]
        before = pl.ds(pl.multiple_of(jnp.maximum(i * tq - WINDOW, 0), WINDOW), WINDOW)
        own = pl.ds(pl.multiple_of(i * tq, tq), tq)

        def products(hd):
            rows = slice(hd * HEAD_DIM, (hd + 1) * HEAD_DIM)
            return _dot(kw[hd // SWA_GROUP], qt_ref[rows, :]), _dot(vw[hd // SWA_GROUP], dot_ref[rows, :])

        nxt = products(0)
        for g in range(SWA_KV_HEADS):
            dsinks = []
            dk_acc = jnp.zeros((tq + WINDOW, HEAD_DIM), F32)
            dv_acc = jnp.zeros((tq + WINDOW, HEAD_DIM), F32)
            for hh in range(SWA_GROUP):
                hd = g * SWA_GROUP + hh
                rows = slice(hd * HEAD_DIM, (hd + 1) * HEAD_DIM)
                sc, dp = nxt
                if hd + 1 < n_heads:
                    nxt = products(hd + 1)
                p, p_sink = _swa_probs_t(sc, mask, s_ref[g][:, hh:hh + 1])
                delta = jnp.sum(p * dp, axis=0, keepdims=True)
                dsc = (p * (dp - delta)).astype(BF)
                dq_ref[rows, :] = _dot(ktw[g], dsc)
                dk_acc = dk_acc + _dot(dsc, q_ref[:, rows])
                dv_acc = dv_acc + _dot(p.astype(BF), do_ref[:, rows])
                dsinks.append(-jnp.sum(p_sink * delta, axis=1, keepdims=True))
            dk_ref[g, before, :] += dk_acc[:WINDOW]
            dk_ref[g, own, :] += dk_acc[WINDOW:]
            dv_ref[g, before, :] += dv_acc[:WINDOW]
            dv_ref[g, own, :] += dv_acc[WINDOW:]
            ds_ref[g] += jnp.concatenate(dsinks, axis=1)

    kvspec = _full((SWA_KV_HEADS, s, HEAD_DIM))
    qspec = pl.BlockSpec((tq, SWA_W), lambda i: (i, 0))
    qspec_t = pl.BlockSpec((SWA_W, tq), lambda i: (0, i))
    sspec = _full((SWA_KV_HEADS, 1, SWA_GROUP))
    kvshape = jax.ShapeDtypeStruct((SWA_KV_HEADS, s, HEAD_DIM), F32)
    return _call(
        body, name="swa_bwd",
        out_shape=(jax.ShapeDtypeStruct((SWA_W, s), F32), kvshape, kvshape,
                   jax.ShapeDtypeStruct((SWA_KV_HEADS, 1, SWA_GROUP), F32)),
        grid=(s // tq,),
        in_specs=[qspec, qspec_t, kvspec, _swa_before(HEAD_DIM, tq),
                  pl.BlockSpec((SWA_KV_HEADS, HEAD_DIM, tq), lambda i: (0, 0, i)), kvspec, sspec, qspec, qspec_t],
        out_specs=(qspec_t, kvspec, kvspec, sspec),
        compiler_params=_params(("arbitrary",)),
    )(qb, qbt, kb, kbt, kbt, vb, sinks, dob, dobt)


def _pairs_to_rows(ref, n_rows=HEAD_DIM):
    parts = []
    for a in range(0, FOX_HEADS, 2):
        parts.append(jnp.concatenate([ref[a][:n_rows], ref[a + 1][:n_rows]], axis=0).T)
    return jnp.concatenate(parts, axis=1)


def _blocks_to_rows(ref):
    return jnp.concatenate([ref[a:a + LANES, :].T for a in range(0, ref.shape[0], LANES)], axis=1)


def _out_proj(oat, za, obt, zb, x, tgt, w_out, w_out_t, gate, g_post, inv_l, tm):
    s = x.shape[0]

    def body(oat_ref, za_ref, obt_ref, zb_ref, x_ref, t_ref, w_ref, wt_ref, gate_ref, gp_ref, il_ref,
             dout_ref, doat_ref, dla_ref, dza_ref, dob_ref, dobt_ref, dzb_ref, gw_ref, dgate_ref, dgp_ref, loss_ref):
        i = pl.program_id(0)

        @pl.when(i == 0)
        def _():
            gw_ref[...] = jnp.zeros(gw_ref.shape, F32)
            dgate_ref[...] = jnp.zeros(dgate_ref.shape, F32)
            dgp_ref[...] = jnp.zeros(dgp_ref.shape, F32)
            loss_ref[...] = jnp.zeros(loss_ref.shape, F32)

        oa_v = _pairs_to_rows(oat_ref)
        ob_v = _blocks_to_rows(obt_ref)
        za_v, zb_v = za_ref[...], zb_ref[...]
        sga, sgb = _sigmoid(za_v), _sigmoid(zb_v)
        sila, silb = za_v * sga, zb_v * sgb
        u = jnp.concatenate([oa_v * sila, ob_v * silb], axis=1).astype(BF)
        yv = _dot(u, w_ref[...])
        yhat, rstd = _rms_hat(yv)
        gp, gate_v = gp_ref[...], gate_ref[...]
        nrm = yhat * gp
        diff = (x_ref[...] + gate_v * nrm) - t_ref[...]
        loss_ref[...] += 0.5 * jnp.sum(jnp.sum(diff * diff, axis=1, keepdims=True), axis=0, keepdims=True) / D_MODEL
        dout = diff * (1.0 / D_MODEL)
        dout_ref[...] = dout
        dgate_ref[...] += jnp.sum(dout * nrm, axis=0, keepdims=True)
        dn = dout * gate_v
        dgp_ref[...] += jnp.sum(dn * yhat, axis=0, keepdims=True)
        dyhat = dn * gp
        dy = (rstd * (dyhat - yhat * jnp.mean(dyhat * yhat, axis=1, keepdims=True))).astype(BF)
        gw_ref[...] += _dot_tn(u, dy)
        du = _dot(dy, wt_ref[...])
        dua, dub = du[:, :FOX_W], du[:, FOX_W:]
        doa = dua * sila
        for a in range(0, FOX_HEADS, 2):
            pair_t = doa[:, a * HEAD_DIM:(a + 2) * HEAD_DIM].T
            for hd, rows in ((a, slice(0, HEAD_DIM)), (a + 1, slice(HEAD_DIM, 2 * HEAD_DIM))):
                inv_l = il_ref[hd]
                doat_ref[hd] = (pair_t[rows] * inv_l).astype(BF)
                dla_ref[hd] = jnp.sum(pair_t[rows] * oat_ref[hd], axis=0, keepdims=True) * inv_l
        dob = dub * silb
        dob_ref[...] = dob.astype(BF)
        for a in range(0, SWA_W, LANES):
            dobt_ref[a:a + LANES, :] = dob[:, a:a + LANES].T.astype(BF)
        dza_ref[...] = (dua * oa_v * (sga * (1.0 + za_v * (1.0 - sga)))).astype(BF)
        dzb_ref[...] = (dub * ob_v * (sgb * (1.0 + zb_v * (1.0 - sgb)))).astype(BF)

    row = lambda w: pl.BlockSpec((tm, w), lambda i: (i, 0))
    heads_t = lambda w: pl.BlockSpec((FOX_HEADS, w, tm), lambda i: (0, 0, i))
    vec = _full((1, D_MODEL))
    mat = _full((D_MODEL, D_MODEL))
    out_shape = (
        jax.ShapeDtypeStruct((s, D_MODEL), F32),
        jax.ShapeDtypeStruct((FOX_HEADS, HEAD_DIM, s), BF), jax.ShapeDtypeStruct((FOX_HEADS, 1, s), F32),
        jax.ShapeDtypeStruct((s, FOX_W), BF), jax.ShapeDtypeStruct((s, SWA_W), BF), jax.ShapeDtypeStruct((SWA_W, s), BF),
        jax.ShapeDtypeStruct((s, SWA_W), BF),
        jax.ShapeDtypeStruct((D_MODEL, D_MODEL), F32),
        jax.ShapeDtypeStruct((1, D_MODEL), F32), jax.ShapeDtypeStruct((1, D_MODEL), F32),
        jax.ShapeDtypeStruct((1, 1), F32),
    )
    col = pl.BlockSpec((SWA_W, tm), lambda i: (0, i))
    return _call(
        body, name="out_proj", out_shape=out_shape, grid=(s // tm,),
        in_specs=[heads_t(HEAD_DIM), row(FOX_W), col, row(SWA_W), row(D_MODEL), row(D_MODEL), mat, mat, vec, vec,
                  heads_t(1)],
        out_specs=(row(D_MODEL), heads_t(HEAD_DIM), heads_t(1), row(FOX_W), row(SWA_W), col, row(SWA_W), mat, vec, vec,
                   _full((1, 1))),
        compiler_params=_params(("arbitrary",)),
    )(oat, za, obt, zb, x, tgt, w_out, w_out_t, gate, g_post, inv_l)


def _assemble_dproj(dqt, dkt, dvt, dza, dqb, dzb, dkb, dvb, df, cos_t, sin_t, tm):
    s = dza.shape[0]

    def body(dqt_ref, dkt_ref, dvt_ref, dza_ref, dqb_ref, dzb_ref, dkb_ref, dvb_ref, df_ref, cos_ref, sin_ref, o_ref):
        def cat(ref, n):
            return jnp.concatenate([ref[hd] for hd in range(n)], axis=1)

        cos2, sin2 = cos_ref[...], sin_ref[...]
        cos8 = jnp.concatenate([cos2] * 4, axis=1)
        sin8 = jnp.concatenate([sin2] * 4, axis=1)
        scale = HEAD_DIM ** -0.5
        o_ref[:, C_QA:C_QA + FOX_W] = (_pairs_to_rows(dqt_ref.at[:, 0]) * scale).astype(BF)
        o_ref[:, C_KA:C_KA + FOX_W] = (_pairs_to_rows(dkt_ref) * LN2).astype(BF)
        o_ref[:, C_VA:C_VA + FOX_W] = _pairs_to_rows(dvt_ref).astype(BF)
        o_ref[:, C_ZA:C_ZA + FOX_W] = dza_ref[...]
        dq = _blocks_to_rows(dqb_ref) * scale
        o_ref[:, C_QB:C_QB + SWA_W] = (dq * cos8 - _rope_partner(dq) * sin8).astype(BF)
        o_ref[:, C_ZB:C_ZB + SWA_W] = dzb_ref[...]
        dk = cat(dkb_ref, SWA_KV_HEADS)
        o_ref[:, C_KB:C_KB + SWA_KV_W] = (dk * cos2 - _rope_partner(dk) * sin2).astype(BF)
        o_ref[:, C_VB:C_VB + SWA_KV_W] = cat(dvb_ref, SWA_KV_HEADS).astype(BF)
        o_ref[:, C_F:C_F + LANES] = df_ref[...].astype(BF)

    row = lambda w: pl.BlockSpec((tm, w), lambda i: (i, 0))
    heads = lambda n: pl.BlockSpec((n, tm, HEAD_DIM), lambda i: (0, i, 0))
    heads_t = lambda w: pl.BlockSpec((FOX_HEADS, w, tm), lambda i: (0, 0, i))
    return _call(
        body, name="assemble_dproj", out_shape=jax.ShapeDtypeStruct((s, WP), BF), grid=(s // tm,),
        in_specs=[pl.BlockSpec((FOX_HEADS, 1, VT_ROWS, tm), lambda i: (0, i, 0, 0)), heads_t(VT_ROWS), heads_t(HEAD_DIM),
                  row(FOX_W), pl.BlockSpec((SWA_W, tm), lambda i: (0, i)), row(SWA_W), heads(SWA_KV_HEADS),
                  heads(SWA_KV_HEADS), row(LANES), row(LANES), row(LANES)],
        out_specs=row(WP), compiler_params=_params(("parallel",)),
    )(dqt, dkt, dvt, dza, dqb, dzb, dkb, dvb, df, cos_t, sin_t)


def _in_proj_bwd_x(dproj, w_al_t, x, dout, g_pre, scale1p, tm, parts):
    s = x.shape[0]
    n_steps = s // tm
    masks = [(1, 0), (0, 1), (1, 1)]

    def body(dp_ref, wt_ref, x_ref, dout_ref, g_ref, sc_ref, parts_ref, gx_ref, dsh_ref, dsc_ref, dg_ref, got_ref,
             send_sems, recv_sems, local_sem):
        i = pl.program_id(0)
        cx, cy, cc = lax.axis_index("x"), lax.axis_index("y"), lax.axis_index("c")
        me = 2 * cx + cy
        own = pltpu.make_async_copy(parts_ref.at[me], got_ref.at[me], local_sem)

        def copy(k, send):
            dx, dy = masks[k]
            peer = 2 * (cx ^ dx) + (cy ^ dy)
            return pltpu.make_async_remote_copy(
                src_ref=parts_ref.at[peer if send else me], dst_ref=got_ref.at[me if send else peer],
                send_sem=send_sems.at[k], recv_sem=recv_sems.at[k], device_id=(cx ^ dx, cy ^ dy, cc), device_id_type=MESH)

        @pl.when(i == 0)
        def _():
            dsh_ref[...] = jnp.zeros(dsh_ref.shape, F32)
            dsc_ref[...] = jnp.zeros(dsc_ref.shape, F32)
            dg_ref[...] = jnp.zeros(dg_ref.shape, F32)
            own.start()
            for k in range(len(masks)):
                copy(k, True).start()

        @pl.when(i == n_steps - 1)
        def _():
            for k in range(len(masks)):
                copy(k, False).wait_recv()
            for k in range(len(masks)):
                copy(k, True).wait_send()
            own.wait()

        dh = _dot(dp_ref[...], wt_ref[...])
        xhat, rstd = _rms_hat(x_ref[...])
        g, sc = g_ref[...], sc_ref[...]
        dsh_ref[...] += jnp.sum(dh, axis=0, keepdims=True)
        dhx = dh * xhat
        dsc_ref[...] += jnp.sum(dhx * g, axis=0, keepdims=True)
        dg_ref[...] += jnp.sum(dhx * sc, axis=0, keepdims=True)
        dxhat = dh * (g * sc)
        gx_ref[...] = dout_ref[...] + rstd * (dxhat - xhat * jnp.mean(dxhat * xhat, axis=1, keepdims=True))

    row = lambda w: pl.BlockSpec((tm, w), lambda i: (i, 0))
    vec = _full((1, D_MODEL))
    vshape = jax.ShapeDtypeStruct((1, D_MODEL), F32)
    hbm = pl.BlockSpec(memory_space=pl.ANY)
    return _call(
        body, name="in_proj_bwd_x",
        out_shape=(jax.ShapeDtypeStruct((s, D_MODEL), F32), vshape, vshape, vshape,
                   jax.ShapeDtypeStruct(parts.shape, parts.dtype)),
        grid=(n_steps,),
        in_specs=[row(WP), _full((WP, D_MODEL)), row(D_MODEL), row(D_MODEL), vec, vec, hbm],
        out_specs=(row(D_MODEL), vec, vec, vec, hbm),
        scratch_shapes=[pltpu.SemaphoreType.DMA((3,)), pltpu.SemaphoreType.DMA((3,)), pltpu.SemaphoreType.DMA],
        compiler_params=_params(("arbitrary",), has_side_effects=True),
    )(dproj, w_al_t, x, dout, g_pre, scale1p, parts)


def _in_proj_bwd_w(h, dproj, tk, tn):
    s = h.shape[0]
    n_k = s // tk

    def body(h_ref, dp_ref, gw_ref, acc_scr):
        k = pl.program_id(1)

        @pl.when(k == 0)
        def _():
            acc_scr[...] = jnp.zeros(acc_scr.shape, F32)

        acc_scr[...] += _dot_tn(h_ref[...], dp_ref[...])

        @pl.when(k == n_k - 1)
        def _():
            gw_ref[...] = acc_scr[...].astype(BF)

    return _call(
        body, name="in_proj_bwd_w", out_shape=jax.ShapeDtypeStruct((D_MODEL, WP), BF), grid=(WP // tn, n_k),
        in_specs=[pl.BlockSpec((tk, D_MODEL), lambda n, k: (k, 0)), pl.BlockSpec((tk, tn), lambda n, k: (k, n))],
        out_specs=pl.BlockSpec((D_MODEL, tn), lambda n, k: (0, n)),
        scratch_shapes=[pltpu.VMEM((D_MODEL, tn), F32)],
        compiler_params=_params(("parallel", "arbitrary")),
    )(h, dproj)


def _align_w_in(w_cols):
    def part(name, width):
        return w_cols[:, _SRC[name]:_SRC[name] + width]

    fpad = jnp.pad(part("fa", FOX_HEADS), ((0, 0), (0, LANES - FOX_HEADS)))
    return jnp.concatenate([part("qa", FOX_W), part("ka", FOX_W), part("va", FOX_W), part("za", FOX_W),
                            part("qb", SWA_W), part("zb", SWA_W), part("kb", SWA_KV_W), part("vb", SWA_KV_W), fpad], axis=1)


def _unalign_w_in(g_al):
    def part(c0, width):
        return g_al[:, c0:c0 + width]

    return jnp.concatenate([part(C_QA, FOX_W), part(C_KA, FOX_W), part(C_VA, FOX_W), part(C_F, FOX_HEADS),
                            part(C_ZA, FOX_W), part(C_QB, SWA_W), part(C_KB, SWA_KV_W), part(C_VB, SWA_KV_W),
                            part(C_ZB, SWA_W)], axis=1)


def _rope_tables(positions):
    inv_freq = ROPE_THETA ** (-jnp.arange(HALF, dtype=F32) / HALF)
    ang = positions.astype(F32)[:, None] * inv_freq
    cos, sin = jnp.cos(ang), jnp.sin(ang)
    return jnp.concatenate([cos, cos, cos, cos], axis=1), jnp.concatenate([-sin, sin, -sin, sin], axis=1)


def _tiles(s):
    if s >= 4096:
        return dict(tm=512, blk=512, bq=2048, bk=2048, bk_bwd=2048, chunk=256, tq=256, tm_out=512, tk=1024, tn=1152)
    return dict(tm=128, blk=128, bq=256, bk=256, bk_bwd=256, chunk=128, tq=128, tm_out=128, tk=128, tn=1152)


def kernel(x, c, positions, w_ada, b_ada, g_pre, w_in, b_fgate, sinks, w_out, g_post, loss_target, m_w_ada, m_b_ada, m_g_pre, m_w_in, m_b_fgate, m_sinks, m_w_out, m_g_post, v_w_ada, v_b_ada, v_g_pre, v_w_in, v_b_fgate, v_sinks, v_w_out, v_g_post):
    s = x.shape[1]
    t = _tiles(s)
    nc = s // LANES
    rows = FOX_HEADS * nc
    me = 4 * lax.axis_index("x") + 2 * lax.axis_index("y") + lax.axis_index("c")
    chip = 2 * lax.axis_index("x") + lax.axis_index("y")
    x2, tgt = x[0], loss_target[0]

    a_all, mod_all = _ada_exchange(c, w_ada[0])
    mod_rows = lax.dynamic_index_in_dim(mod_all, me, axis=1, keepdims=False)
    mod = mod_rows.reshape(N_CHIPS, 2, W_ADA_SHARD)[:, 0, :].reshape(1, 3 * D_MODEL) + b_ada
    shift, scale1p, gate = mod[:, :D_MODEL], 1.0 + mod[:, D_MODEL:2 * D_MODEL], mod[:, 2 * D_MODEL:]

    w_in_pad = jnp.pad(w_in[0].astype(BF), ((0, 0), (0, W_IN_SHARD_PAD - W_IN_SHARD)))
    w_all = _allgather_chips(w_in_pad.reshape(2, D_MODEL // 2, -1), "gather_w_in").reshape(N_CHIPS, D_MODEL, -1)
    w_cols = jnp.concatenate([w_all[k, :, :W_IN_SHARD] for k in range(N_CHIPS)], axis=1)
    w_al = _align_w_in(w_cols)
    w_al_t = w_al.T

    cos_t, sin_t = _rope_tables(positions[0])

    f_pad = _forget_logits(x2, g_pre, scale1p, shift, w_al[:, C_F:], t["tk"])
    f_rows = f_pad[:, :FOX_HEADS].T.reshape(rows, LANES)
    bias_rows = jnp.repeat(b_fgate[0], nc)[:, None]
    cum = _log_forget_cumsum(f_rows, bias_rows, nc).reshape(FOX_HEADS, s)
    h, qat, ka, kat, va, vat, za, zb, qb, kb, vb, qbt, kbt, vbt, m_own, w_out_all = _in_proj(
        x2, g_pre, scale1p, shift, w_al[:, C_VA:C_F], w_al_t[:C_ZA], cum, cos_t, sin_t,
        w_out[0].astype(BF).reshape(2, W_OUT_SHARD // 2, D_MODEL), t["tm"])
    w_out_all = w_out_all.reshape(D_MODEL, D_MODEL)
    w_out_t = w_out_all.T
    m_own = m_own[:, None, :]
    fox_args = (qat, ka, vat, m_own, t["bq"], t["bk"], t["chunk"])
    oat, lse, bad, pt = _fox_fwd(*fox_args, running_max=False)
    overflowed = jnp.max(bad) > 0.0
    oat, lse = lax.cond(overflowed, lambda: _fox_fwd(*fox_args, running_max=True)[:2], lambda: (oat, lse))
    inv_l = jnp.where(overflowed, 1.0, jnp.exp2(m_own - lse))
    sinks_g = sinks.reshape(SWA_KV_HEADS, 1, SWA_GROUP)
    obt = _swa_fwd(qbt, kb, vbt, sinks_g, t["tq"])

    dout, doat, delta_a, dza, dob, dobt, dzb, gw_out, dgate, dg_post, loss_part = _out_proj(
        oat, za, obt, zb, x2, tgt, w_out_all, w_out_t, gate, g_post, inv_l, t["tm_out"])

    bwd_args = (qat, ka, kat, va, doat, lse, delta_a)
    bwd_tiles = (t["bq"], t["bk_bwd"], t["chunk"], t["blk"])
    dqt, dkt, dvt = lax.cond(overflowed, lambda: _fox_bwd(*bwd_args, None, *bwd_tiles),
                             lambda: _fox_bwd(*bwd_args, pt, *bwd_tiles))
    dcum = dqt[:, :, HEAD_DIM, :].reshape(FOX_HEADS, s) - dkt[:, HEAD_DIM, :]
    df_rows, db_heads = _log_forget_cumsum_bwd(dcum.reshape(rows, LANES), f_rows, bias_rows, nc)
    df_pad = jnp.pad(df_rows.reshape(FOX_HEADS, s).T, ((0, 0), (0, LANES - FOX_HEADS)))
    dqb, dkb, dvb, dsinks = _swa_bwd(qb, qbt, kb, kbt, vb, sinks_g, dob, dobt, t["tq"])

    dproj = _assemble_dproj(dqt, dkt, dvt, dza, dqb, dzb, dkb, dvb, df_pad, cos_t, sin_t, t["blk"])
    gw_in = _unalign_w_in(_in_proj_bwd_w(h, dproj, t["tk"], t["tn"]))

    gin = jnp.stack([jnp.pad(gw_in[:, k * W_IN_SHARD:(k + 1) * W_IN_SHARD], ((0, 0), (0, W_IN_SHARD_PAD - W_IN_SHARD)))
                     for k in range(N_CHIPS)])
    gout = gw_out.astype(BF).reshape(N_CHIPS, D_MODEL, W_OUT_SHARD)
    pair = _pair_sum(gin, gout, "pair_sum")
    grad_x, dshift, dscale, dg_pre, from_chips = _in_proj_bwd_x(
        dproj, w_al_t, x2, dout, g_pre, scale1p, t["tm_out"], pair)

    pad_lane = lambda vrow: jnp.pad(vrow, ((0, 0), (0, LANES - vrow.shape[1])))
    packed = jnp.concatenate([dshift, dscale, dgate, dg_pre, dg_post,
                              pad_lane(db_heads.reshape(1, FOX_HEADS)), pad_lane(dsinks.reshape(1, FOX_HEADS)),
                              pad_lane(loss_part)], axis=1)
    parts = _allgather_devices(packed, "gather_partials")
    tot = _sum_devices(parts)
    loss = tot[0, P_LOSS]
    g_b_ada = tot[:, P_DMOD:P_DMOD + 3 * D_MODEL]
    g_g_pre = tot[:, P_GPRE:P_GPRE + D_MODEL]
    g_g_post = tot[:, P_GPOST:P_GPOST + D_MODEL]
    g_b_fgate = tot[:, P_BF:P_BF + FOX_HEADS]
    g_sinks = tot[:, P_SINK:P_SINK + FOX_HEADS]
    dm_shard = lax.dynamic_slice_in_dim(parts[:, 0, :3 * D_MODEL], chip * W_ADA_SHARD, W_ADA_SHARD, axis=1)
    g_w_ada = _grad_w_ada(a_all.T, dm_shard)

    gfull = _sum_chips_and_share(from_chips, "sum_chips_and_share")
    g_w_in = gfull[:, :W_IN_SHARD]
    g_w_out = gfull[:, W_IN_SHARD_PAD:].reshape(W_OUT_SHARD, D_MODEL)

    grads = dict(w_ada=g_w_ada, b_ada=g_b_ada, g_pre=g_g_pre, w_in=g_w_in, b_fgate=g_b_fgate, sinks=g_sinks,
                 w_out=g_w_out, g_post=g_g_post)
    weights = dict(w_ada=w_ada, b_ada=b_ada, g_pre=g_pre, w_in=w_in, b_fgate=b_fgate, sinks=sinks, w_out=w_out, g_post=g_post)
    moms = dict(w_ada=m_w_ada, b_ada=m_b_ada, g_pre=m_g_pre, w_in=m_w_in, b_fgate=m_b_fgate, sinks=m_sinks, w_out=m_w_out, g_post=m_g_post)
    vars_ = dict(w_ada=v_w_ada, b_ada=v_b_ada, g_pre=v_g_pre, w_in=v_w_in, b_fgate=v_b_fgate, sinks=v_sinks, w_out=v_w_out, g_post=v_g_post)
    names = ["w_ada", "b_ada", "g_pre", "w_in", "b_fgate", "sinks", "w_out", "g_post"]
    g_out, d_out, m_out, v_out = [], [], [], []
    for n in names:
        if n == "w_in":
            flat = lambda a: jnp.transpose(a, (2, 0, 1)).reshape(W_IN_SHARD * D_MODEL // LANES, LANES)
            unflat = lambda a: jnp.transpose(a.reshape(W_IN_SHARD, 1, D_MODEL), (1, 2, 0))
            outs = _adamw(flat(w_in), flat(grads[n][None]), flat(moms[n]), flat(vars_[n]), "adamw_" + n)
            go, d, nm, nv = (unflat(a) for a in outs)
        else:
            g2 = grads[n].reshape(weights[n].shape[-2:])
            go, d, nm, nv = _adamw(weights[n], g2, moms[n], vars_[n], "adamw_" + n)
        g_out.append(go)
        d_out.append(d)
        m_out.append(nm)
        v_out.append(nv)
    return (loss, grad_x.reshape(x.shape), *g_out, *d_out, *m_out, *v_out)
```

```python
import jax
import jax.numpy as jnp
from jax import lax
from jax.experimental import pallas as pl
from jax.experimental.pallas import tpu as pltpu

_INTERPRET = False

D_MODEL = 1024
HEAD_DIM = 64
HALF = HEAD_DIM // 2
AUG_DIM = 128
AUG_ROWS = 8
VT_ROWS = 80
LOG2E = 1.4426950408889634
LN2 = 0.6931471805599453
Q_SCALE = LOG2E * 64 ** -0.5
FOX_HEADS = 8
FOX_W = 512
SWA_W = 512
SWA_KV_HEADS = 2
SWA_GROUP = 4
SWA_KV_W = 128
WINDOW = 128
ROPE_THETA = 10000.0
RMS_EPS = 1e-6
IN_WIDTH = 3336
N_CHIPS = 4
N_DEV = 8
W_IN_SHARD = IN_WIDTH // N_CHIPS
W_IN_SHARD_PAD = 896
W_ADA_SHARD = 3 * D_MODEL // N_CHIPS
W_OUT_SHARD = D_MODEL // N_CHIPS
LANES = 128

_SRC = dict(qa=0, ka=512, va=1024, fa=1536, za=1544, qb=2056, kb=2568, vb=2696, zb=2824)
C_QA, C_KA, C_VA, C_ZA, C_QB, C_ZB, C_KB, C_VB, C_F = 0, 512, 1024, 1536, 2048, 2560, 3072, 3200, 3328
WP = 3456

ADAM_LR = 0.001
ADAM_B1 = 0.9
ADAM_B2 = 0.999
ADAM_EPS = 1e-08
ADAM_WD = 0.01
ADAM_STEP = 10
ADAMW_BLOCK_ELEMS = 300_000

VMEM_LIMIT = 56 * 1024 * 1024
NEG = -1e30
OVERFLOW_GUARD = 1e30
MESH = pl.DeviceIdType.MESH
BF = jnp.bfloat16
F32 = jnp.float32

P_DMOD, P_GPRE, P_GPOST, P_BF, P_SINK, P_LOSS, P_LEN = 0, 3072, 4096, 5120, 5248, 5376, 5504


def _call(body, **kw):
    return pl.pallas_call(body, interpret=_INTERPRET, **kw)


def _params(sem=None, **kw):
    return pltpu.CompilerParams(dimension_semantics=sem, vmem_limit_bytes=VMEM_LIMIT, **kw)


def _full(shape):
    zeros = (0,) * len(shape)
    return pl.BlockSpec(shape, lambda *_: zeros)


def _dot(a, b):
    return jnp.dot(a, b, preferred_element_type=F32)


def _dot_nt(a, b):
    return lax.dot_general(a, b, (((1,), (1,)), ((), ())), preferred_element_type=F32)


def _dot_tn(a, b):
    return lax.dot_general(a, b, (((0,), (0,)), ((), ())), preferred_element_type=F32)


def _sigmoid(z):
    return 1.0 / (1.0 + jnp.exp(-z))


def _rope_partner(t):
    w = t.shape[-1]
    lane = lax.broadcasted_iota(jnp.int32, t.shape, t.ndim - 1)
    return jnp.where((lane & (HEAD_DIM - 1)) < HALF, pltpu.roll(t, w - HALF, t.ndim - 1), pltpu.roll(t, HALF, t.ndim - 1))


def _allgather_devices(v, name):
    r, cdim = v.shape
    masks = [(dx, dy, dc) for dx in (0, 1) for dy in (0, 1) for dc in (0, 1)][1:]

    def body(v_ref, out_ref, send_sems, recv_sems):
        x, y, c = lax.axis_index("x"), lax.axis_index("y"), lax.axis_index("c")
        me = 4 * x + 2 * y + c
        out_ref[me] = v_ref[...]
        copies = []
        for k, (dx, dy, dc) in enumerate(masks):
            cp = pltpu.make_async_remote_copy(
                src_ref=v_ref, dst_ref=out_ref.at[me], send_sem=send_sems.at[k], recv_sem=recv_sems.at[k],
                device_id=(x ^ dx, y ^ dy, c ^ dc), device_id_type=MESH)
            cp.start()
            copies.append(cp)
        for k, (dx, dy, dc) in enumerate(masks):
            peer = 4 * (x ^ dx) + 2 * (y ^ dy) + (c ^ dc)
            pltpu.make_async_remote_copy(
                src_ref=v_ref, dst_ref=out_ref.at[peer], send_sem=send_sems.at[k], recv_sem=recv_sems.at[k],
                device_id=(x ^ dx, y ^ dy, c ^ dc), device_id_type=MESH).wait_recv()
        for cp in copies:
            cp.wait_send()

    return _call(
        body, name=name, out_shape=jax.ShapeDtypeStruct((N_DEV, r, cdim), v.dtype),
        in_specs=[pl.BlockSpec(memory_space=pltpu.VMEM)], out_specs=pl.BlockSpec(memory_space=pltpu.VMEM),
        scratch_shapes=[pltpu.SemaphoreType.DMA((7,)), pltpu.SemaphoreType.DMA((7,))],
        compiler_params=pltpu.CompilerParams(has_side_effects=True),
    )(v)


CHIP_MASKS = [(1, 0), (0, 1), (1, 1)]
CHIP_GATHER_SEMS = [pltpu.SemaphoreType.DMA((2 * len(CHIP_MASKS),)), pltpu.SemaphoreType.DMA((2 * len(CHIP_MASKS),)),
                    pltpu.SemaphoreType.DMA]


def _chip_gather(v_ref, out_ref, send_sems, recv_sems, local_sem):
    n = len(CHIP_MASKS)
    x, y, c = lax.axis_index("x"), lax.axis_index("y"), lax.axis_index("c")
    me = 2 * x + y
    mine = pltpu.make_async_copy(v_ref, out_ref.at[me], local_sem)

    def copy(k, chip, half, to):
        return pltpu.make_async_remote_copy(
            src_ref=v_ref.at[half] if k < n else out_ref.at[chip, half], dst_ref=out_ref.at[chip, half],
            send_sem=send_sems.at[k], recv_sem=recv_sems.at[k], device_id=to, device_id_type=MESH)

    def start():
        mine.start()
        for k, (dx, dy) in enumerate(CHIP_MASKS):
            copy(k, me, c, (x ^ dx, y ^ dy, c)).start()

    def finish():
        passed = []
        for k, (dx, dy) in enumerate(CHIP_MASKS):
            peer = 2 * (x ^ dx) + (y ^ dy)
            copy(k, peer, c, (x, y, c)).wait_recv()
            cp = copy(n + k, peer, c, (x, y, 1 - c))
            cp.start()
            passed.append(cp)
        for k, (dx, dy) in enumerate(CHIP_MASKS):
            copy(n + k, 2 * (x ^ dx) + (y ^ dy), 1 - c, (x, y, c)).wait_recv()
        for k, (dx, dy) in enumerate(CHIP_MASKS):
            copy(k, me, c, (x ^ dx, y ^ dy, c)).wait_send()
        for cp in passed:
            cp.wait_send()
        mine.wait()

    return start, finish


def _allgather_chips(v, name):
    _, r, cdim = v.shape
    n = len(CHIP_MASKS)

    def body(v_ref, out_ref, send_sems, recv_sems, local_sem):
        start, finish = _chip_gather(v_ref, out_ref, send_sems, recv_sems, local_sem)
        start()
        finish()

    return _call(
        body, name=name, out_shape=jax.ShapeDtypeStruct((N_CHIPS, 2, r, cdim), v.dtype),
        in_specs=[pl.BlockSpec(memory_space=pl.ANY)], out_specs=pl.BlockSpec(memory_space=pl.ANY),
        scratch_shapes=[pltpu.SemaphoreType.DMA((2 * n,)), pltpu.SemaphoreType.DMA((2 * n,)), pltpu.SemaphoreType.DMA],
        compiler_params=pltpu.CompilerParams(has_side_effects=True),
    )(v)


def _ada_exchange(c, w_ada_shard):
    masks = [(dx, dy, dc) for dx in (0, 1) for dy in (0, 1) for dc in (0, 1)][1:]
    n = len(masks)

    def body(c_ref, w_ref, a_ref, mod_ref, c_all, send_sems, recv_sems):
        x, y, cc = lax.axis_index("x"), lax.axis_index("y"), lax.axis_index("c")
        me = 4 * x + 2 * y + cc

        def gather(src_ref, dst_ref, first):
            sends = []
            for k, (dx, dy, dc) in enumerate(masks):
                cp = pltpu.make_async_remote_copy(
                    src_ref=src_ref, dst_ref=dst_ref.at[me], send_sem=send_sems.at[first + k],
                    recv_sem=recv_sems.at[first + k], device_id=(x ^ dx, y ^ dy, cc ^ dc), device_id_type=MESH)
                cp.start()
                sends.append(cp)
            for k, (dx, dy, dc) in enumerate(masks):
                peer = 4 * (x ^ dx) + 2 * (y ^ dy) + (cc ^ dc)
                pltpu.make_async_remote_copy(
                    src_ref=src_ref, dst_ref=dst_ref.at[peer], send_sem=send_sems.at[first + k],
                    recv_sem=recv_sems.at[first + k], device_id=(x ^ dx, y ^ dy, cc ^ dc), device_id_type=MESH).wait_recv()
            return sends

        c_all[me] = c_ref[...]
        sends = gather(c_ref, c_all, 0)
        w_bf = w_ref[...].astype(BF)
        for d in range(N_DEV):
            cv = c_all[d]
            a = cv * _sigmoid(cv)
            a_ref[d:d + 1, :] = a
            mod_ref[me, d:d + 1, :] = _dot(a.astype(BF), w_bf)
        sends += gather(mod_ref.at[me], mod_ref, n)
        for cp in sends:
            cp.wait_send()

    vmem = pl.BlockSpec(memory_space=pltpu.VMEM)
    return _call(
        body, name="ada_exchange",
        out_shape=(jax.ShapeDtypeStruct((N_DEV, D_MODEL), F32), jax.ShapeDtypeStruct((N_DEV, N_DEV, W_ADA_SHARD), F32)),
        in_specs=[vmem, vmem], out_specs=(vmem, vmem),
        scratch_shapes=[pltpu.VMEM((N_DEV, 1, D_MODEL), F32), pltpu.SemaphoreType.DMA((2 * n,)),
                        pltpu.SemaphoreType.DMA((2 * n,))],
        compiler_params=_params(has_side_effects=True),
    )(c, w_ada_shard)


def _grad_w_ada(a_t, dm_shard):
    def body(a_ref, dm_ref, out_ref):
        acc = jnp.zeros((D_MODEL, W_ADA_SHARD), F32)
        for b in range(N_DEV):
            acc = acc + a_ref[:, b:b + 1] * dm_ref[b:b + 1, :]
        out_ref[...] = acc

    return _call(body, name="grad_w_ada", out_shape=jax.ShapeDtypeStruct((D_MODEL, W_ADA_SHARD), F32),
                 compiler_params=_params())(a_t, dm_shard)


def _sum_devices(parts):
    n = parts.shape[-1]

    def body(p_ref, out_ref):
        acc = p_ref[0]
        for b in range(1, N_DEV):
            acc = acc + p_ref[b]
        out_ref[...] = acc

    return _call(body, name="sum_devices", out_shape=jax.ShapeDtypeStruct((1, n), F32), compiler_params=_params())(parts)


def _pair_sum(part_a, part_b, name):
    n, r2, ca = part_a.shape
    cb = part_b.shape[2]
    r = r2 // 2
    tr = 128

    def body(a_ref, b_ref, o_ref, land_a, land_b, send_sems, recv_sems):
        x, y, c = lax.axis_index("x"), lax.axis_index("y"), lax.axis_index("c")
        mine = pl.multiple_of(c * r, r)
        theirs = pl.multiple_of((1 - c) * r, r)
        copies = []
        for k in range(n):
            for j, (src, land) in enumerate(((a_ref, land_a), (b_ref, land_b))):
                cp = pltpu.make_async_remote_copy(
                    src_ref=src.at[k, pl.ds(theirs, r), :], dst_ref=land.at[k], send_sem=send_sems.at[2 * k + j],
                    recv_sem=recv_sems.at[2 * k + j], device_id=(x, y, 1 - c), device_id_type=MESH)
                cp.start()
                copies.append(cp)
        for k in range(n):
            copies[2 * k].wait_recv()
            copies[2 * k + 1].wait_recv()

            def rows(i, carry, k=k):
                sl = pl.ds(pl.multiple_of(i * tr, tr), tr)
                own = pl.ds(pl.multiple_of(mine + i * tr, tr), tr)
                o_ref[k, sl, :ca] = (a_ref[k, own, :].astype(F32) + land_a[k, sl, :].astype(F32)).astype(BF)
                o_ref[k, sl, ca:] = (b_ref[k, own, :].astype(F32) + land_b[k, sl, :].astype(F32)).astype(BF)
                return carry

            lax.fori_loop(0, r // tr, rows, 0)
        for cp in copies:
            cp.wait_send()

    vmem = pl.BlockSpec(memory_space=pltpu.VMEM)
    return _call(body, name=name, out_shape=jax.ShapeDtypeStruct((n, r, ca + cb), BF), in_specs=[vmem, vmem],
                 out_specs=vmem,
                 scratch_shapes=[pltpu.VMEM((n, r, ca), BF), pltpu.VMEM((n, r, cb), BF),
                                 pltpu.SemaphoreType.DMA((2 * n,)), pltpu.SemaphoreType.DMA((2 * n,))],
                 compiler_params=_params(has_side_effects=True))(part_a, part_b)


def _sum_chips_and_share(parts, name):
    _, r, cdim = parts.shape
    tr = 128

    def body(p_ref, o_ref, send_sem, recv_sem):
        x, y, c = lax.axis_index("x"), lax.axis_index("y"), lax.axis_index("c")
        mine = pl.multiple_of(c * r, r)

        def rows(n, carry):
            sl = pl.ds(pl.multiple_of(n * tr, tr), tr)
            p = [p_ref[k, sl, :].astype(F32) for k in range(N_CHIPS)]
            o_ref[pl.ds(pl.multiple_of(mine + n * tr, tr), tr), :] = ((p[0] + p[1]) + p[2]) + p[3]
            return carry

        lax.fori_loop(0, r // tr, rows, 0)
        half = o_ref.at[pl.ds(mine, r), :]
        cp = pltpu.make_async_remote_copy(src_ref=half, dst_ref=half, send_sem=send_sem, recv_sem=recv_sem,
                                          device_id=(x, y, 1 - c), device_id_type=MESH)
        cp.start()
        other = o_ref.at[pl.ds(pl.multiple_of((1 - c) * r, r), r), :]
        pltpu.make_async_remote_copy(src_ref=other, dst_ref=other, send_sem=send_sem, recv_sem=recv_sem,
                                     device_id=(x, y, 1 - c), device_id_type=MESH).wait_recv()
        cp.wait_send()

    vmem = pl.BlockSpec(memory_space=pltpu.VMEM)
    return _call(body, name=name, out_shape=jax.ShapeDtypeStruct((2 * r, cdim), F32), in_specs=[vmem], out_specs=vmem,
                 scratch_shapes=[pltpu.SemaphoreType.DMA, pltpu.SemaphoreType.DMA],
                 compiler_params=_params(has_side_effects=True))(parts)


def _adamw(w, g, m, v, name):
    r, cdim = w.shape[-2:]
    lead = w.ndim - 2
    tr = r if r <= 256 else max(t for t in range(8, ADAMW_BLOCK_ELEMS // cdim + 1, 8) if r % t == 0)
    c1 = 1.0 / (1.0 - ADAM_B1 ** ADAM_STEP)
    c2 = 1.0 / (1.0 - ADAM_B2 ** ADAM_STEP)

    def body(w_ref, g_ref, m_ref, v_ref, go_ref, d_ref, nm_ref, nv_ref):
        gv = g_ref[...].reshape(go_ref.shape)
        nm = ADAM_B1 * m_ref[...] + (1.0 - ADAM_B1) * gv
        nv = ADAM_B2 * v_ref[...] + (1.0 - ADAM_B2) * (gv * gv)
        m_hat = nm * c1
        v_hat = nv * c2
        go_ref[...] = gv
        d_ref[...] = -ADAM_LR * (m_hat / (jnp.sqrt(v_hat) + ADAM_EPS) + ADAM_WD * w_ref[...])
        nm_ref[...] = nm
        nv_ref[...] = nv

    spec = pl.BlockSpec((1,) * lead + (tr, cdim), lambda i: (0,) * lead + (i, 0))
    shp = jax.ShapeDtypeStruct(w.shape, F32)
    return _call(body, name=name, out_shape=(shp,) * 4, grid=(r // tr,),
                 in_specs=[spec, pl.BlockSpec((tr, cdim), lambda i: (i, 0)), spec, spec],
                 out_specs=(spec,) * 4, compiler_params=_params(("parallel",)))(w, g, m, v)


def _head_of_row(r, nc):
    assert nc & (nc - 1) == 0
    return lax.shift_right_logical(r, nc.bit_length() - 1)


def _chunk_mats(rows, nc, reverse):
    ri = lax.broadcasted_iota(jnp.int32, (rows, rows), 0)
    ci = lax.broadcasted_iota(jnp.int32, (rows, rows), 1)
    same = _head_of_row(ri, nc) == _head_of_row(ci, nc)
    between = jnp.where(same & ((ci > ri) if reverse else (ci < ri)), 1.0, 0.0).astype(F32)
    li = lax.broadcasted_iota(jnp.int32, (LANES, LANES), 0)
    lj = lax.broadcasted_iota(jnp.int32, (LANES, LANES), 1)
    within = jnp.where((li >= lj) if reverse else (li <= lj), 1.0, 0.0).astype(F32)
    return between, within


def _dot_hi(a, b):
    return jnp.dot(a, b, preferred_element_type=F32, precision=lax.Precision.HIGHEST)


def _scan_rows(t, nc, reverse):
    between, within = _chunk_mats(t.shape[0], nc, reverse)
    inner = _dot_hi(t, within)
    tot = jnp.sum(t, axis=1, keepdims=True)
    return inner + _dot_hi(between, jnp.broadcast_to(tot, t.shape))


def _log_forget_cumsum(f_rows, bias_rows, nc):
    def body(f_ref, b_ref, cum_ref):
        z = f_ref[...] + b_ref[...]
        lf = jnp.minimum(z, 0.0) - jnp.log(1.0 + jnp.exp(-jnp.abs(z)))
        cum_ref[...] = _scan_rows(lf, nc, False)

    return _call(body, name="forget_cumsum", out_shape=jax.ShapeDtypeStruct(f_rows.shape, F32),
                 compiler_params=_params())(f_rows, bias_rows)


def _log_forget_cumsum_bwd(dcum_rows, f_rows, bias_rows, nc):
    rows = f_rows.shape[0]

    def body(d_ref, f_ref, b_ref, df_ref, db_ref):
        dlf = _scan_rows(d_ref[...], nc, True)
        z = f_ref[...] + b_ref[...]
        df = dlf * _sigmoid(-z)
        df_ref[...] = df
        hi = lax.broadcasted_iota(jnp.int32, (FOX_HEADS, rows), 0)
        ri = lax.broadcasted_iota(jnp.int32, (FOX_HEADS, rows), 1)
        sel = jnp.where(_head_of_row(ri, nc) == hi, 1.0, 0.0).astype(F32)
        db_ref[...] = jnp.sum(_dot_hi(sel, df), axis=1, keepdims=True)

    return _call(body, name="forget_cumsum_bwd",
                 out_shape=(jax.ShapeDtypeStruct(f_rows.shape, F32), jax.ShapeDtypeStruct((FOX_HEADS, 1), F32)),
                 compiler_params=_params())(dcum_rows, f_rows, bias_rows)


def _rms_hat(xv):
    rstd = lax.rsqrt(jnp.mean(xv * xv, axis=-1, keepdims=True) + RMS_EPS)
    return xv * rstd, rstd


def _modulated(x_ref, g_ref, sc_ref, sh_ref):
    xhat, _ = _rms_hat(x_ref[...])
    return ((xhat * g_ref[...]) * sc_ref[...] + sh_ref[...]).astype(BF)


def _forget_logits(x, g_pre, scale1p, shift, w_f, tm):
    s = x.shape[0]

    def body(x_ref, g_ref, sc_ref, sh_ref, w_ref, f_ref):
        f_ref[...] = _dot(_modulated(x_ref, g_ref, sc_ref, sh_ref), w_ref[...])

    vec = _full((1, D_MODEL))
    return _call(
        body, name="forget_logits", out_shape=jax.ShapeDtypeStruct((s, LANES), F32), grid=(s // tm,),
        in_specs=[pl.BlockSpec((tm, D_MODEL), lambda i: (i, 0)), vec, vec, vec, _full((D_MODEL, LANES))],
        out_specs=pl.BlockSpec((tm, LANES), lambda i: (i, 0)), compiler_params=_params(("parallel",)),
    )(x, g_pre, scale1p, shift, w_f)


def _split3(v):
    hi = v.astype(BF).astype(F32)
    mid = (v - hi).astype(BF).astype(F32)
    lo = ((v - hi) - mid).astype(BF).astype(F32)
    return hi, mid, lo


def _in_proj(x, g_pre, scale1p, shift, w_rows, w_t_fox, cum, cos_t, sin_t, w_out_halves, tm):
    s = x.shape[0]
    r_va, r_za, r_qb, r_zb, r_kb, r_vb = 0, 512, 1024, 1536, 2048, 2176

    def body(x_ref, g_ref, sc_ref, sh_ref, w_ref, wt_ref, cum_ref, cos_ref, sin_ref, wo_ref,
             h_ref, qat_ref, ka_ref, kat_ref, v_ref, vt_ref, za_ref, zb_ref, qb_ref, kb_ref, vb_ref,
             qbt_ref, kbt_ref, vbt_ref, mo_ref, wo_all_ref, send_sems, recv_sems, local_sem):
        start_gather, finish_gather = _chip_gather(wo_ref, wo_all_ref, send_sems, recv_sems, local_sem)

        @pl.when(pl.program_id(0) == 0)
        def _():
            start_gather()

        @pl.when(pl.program_id(0) == s // tm - 1)
        def _():
            finish_gather()

        hb = _modulated(x_ref, g_ref, sc_ref, sh_ref)
        h_ref[...] = hb

        def sec(c0, width):
            return _dot(hb, w_ref[:, c0:c0 + width])

        def sec_t(r0):
            return _dot_nt(wt_ref[r0:r0 + FOX_W, :], hb)

        q_t = sec_t(0) * Q_SCALE
        k_t = sec_t(FOX_W)
        v_t = sec_t(2 * FOX_W)
        va = sec(r_va, FOX_W)
        zeros = jnp.zeros((AUG_DIM - HEAD_DIM - AUG_ROWS, tm), F32)
        ri = lax.broadcasted_iota(jnp.int32, (AUG_ROWS, tm), 0)
        const = jnp.where(ri == AUG_ROWS - 1, 0.0, 1.0)
        ri_v = lax.broadcasted_iota(jnp.int32, (VT_ROWS - HEAD_DIM, tm), 0)
        v_feat = jnp.where(ri_v == 0, 1.0, 0.0).astype(BF)
        for hd in range(FOX_HEADS):
            rows = slice(hd * HEAD_DIM, (hd + 1) * HEAD_DIM)
            cum2 = cum_ref[hd:hd + 1, :] * LOG2E
            hi, mid, lo = (jnp.broadcast_to(part, (AUG_ROWS, tm)) for part in _split3(cum2))
            q_feat = jnp.where(ri == 1, hi, jnp.where(ri == 2, mid, jnp.where(ri == 3, lo, const)))
            k_feat = jnp.where(ri == 4, -hi, jnp.where(ri == 5, -mid, jnp.where(ri == 6, -lo, const)))
            q_aug = jnp.concatenate([q_t[rows], q_feat, zeros], axis=0)
            k_aug = jnp.concatenate([k_t[rows], k_feat, zeros], axis=0)
            mo_ref[hd:hd + 1, :] = jnp.sum(q_t[rows] * k_t[rows], axis=0, keepdims=True) + 1.0
            qat_ref[hd] = q_aug.astype(BF)
            kat_ref[hd] = k_aug.astype(BF)
            ka_ref[hd] = k_aug.T.astype(BF)
            vt_ref[hd] = jnp.concatenate([v_t[rows].astype(BF), v_feat], axis=0)
            v_ref[hd] = va[:, rows].astype(BF)
        za_ref[...] = sec(r_za, FOX_W)
        zb_ref[...] = sec(r_zb, SWA_W)
        cos2, sin2 = cos_ref[...], sin_ref[...]
        cos8 = jnp.concatenate([cos2] * 4, axis=1)
        sin8 = jnp.concatenate([sin2] * 4, axis=1)
        qb = sec(r_qb, SWA_W)
        qb = (qb * cos8 + _rope_partner(qb) * sin8) * (HEAD_DIM ** -0.5)
        qb_ref[...] = qb.astype(BF)
        for a in range(SWA_W // LANES):
            qbt_ref[a * LANES:(a + 1) * LANES, :] = qb[:, a * LANES:(a + 1) * LANES].T.astype(BF)
        kb = sec(r_kb, SWA_KV_W)
        kb = kb * cos2 + _rope_partner(kb) * sin2
        vb = sec(r_vb, SWA_KV_W)
        kb_t, vb_t = kb.T, vb.T
        for hd in range(SWA_KV_HEADS):
            sl = slice(hd * HEAD_DIM, (hd + 1) * HEAD_DIM)
            kb_ref[hd] = kb[:, sl].astype(BF)
            vb_ref[hd] = vb[:, sl].astype(BF)
            kbt_ref[hd] = kb_t[sl].astype(BF)
            vbt_ref[hd] = jnp.concatenate([vb_t[sl].astype(BF), v_feat], axis=0)

    row = lambda w: pl.BlockSpec((tm, w), lambda i: (i, 0))
    heads = lambda n, w=HEAD_DIM: pl.BlockSpec((n, tm, w), lambda i: (0, i, 0))
    heads_t = lambda w: pl.BlockSpec((FOX_HEADS, w, tm), lambda i: (0, 0, i))
    vec = _full((1, D_MODEL))
    hs = lambda a, b: jax.ShapeDtypeStruct((FOX_HEADS, a, b), BF)
    out_shape = (
        jax.ShapeDtypeStruct((s, D_MODEL), BF),
        hs(AUG_DIM, s), hs(s, AUG_DIM), hs(AUG_DIM, s), hs(s, HEAD_DIM), hs(VT_ROWS, s),
        jax.ShapeDtypeStruct((s, FOX_W), F32), jax.ShapeDtypeStruct((s, SWA_W), F32),
        jax.ShapeDtypeStruct((s, SWA_W), BF),
        jax.ShapeDtypeStruct((SWA_KV_HEADS, s, HEAD_DIM), BF), jax.ShapeDtypeStruct((SWA_KV_HEADS, s, HEAD_DIM), BF),
        jax.ShapeDtypeStruct((SWA_W, s), BF),
        jax.ShapeDtypeStruct((SWA_KV_HEADS, HEAD_DIM, s), BF), jax.ShapeDtypeStruct((SWA_KV_HEADS, VT_ROWS, s), BF),
        jax.ShapeDtypeStruct((FOX_HEADS, s), F32),
        jax.ShapeDtypeStruct((N_CHIPS,) + w_out_halves.shape, w_out_halves.dtype),
    )
    kv_t = lambda w: pl.BlockSpec((SWA_KV_HEADS, w, tm), lambda i: (0, 0, i))
    hbm = pl.BlockSpec(memory_space=pl.ANY)
    return _call(
        body, name="in_proj", out_shape=out_shape, grid=(s // tm,),
        in_specs=[row(D_MODEL), vec, vec, vec, _full(w_rows.shape), _full(w_t_fox.shape),
                  pl.BlockSpec((FOX_HEADS, tm), lambda i: (0, i)), row(LANES), row(LANES), hbm],
        out_specs=(row(D_MODEL), heads_t(AUG_DIM), heads(FOX_HEADS, AUG_DIM), heads_t(AUG_DIM), heads(FOX_HEADS),
                   heads_t(VT_ROWS), row(FOX_W), row(SWA_W), row(SWA_W), heads(SWA_KV_HEADS), heads(SWA_KV_HEADS),
                   pl.BlockSpec((SWA_W, tm), lambda i: (0, i)), kv_t(HEAD_DIM), kv_t(VT_ROWS),
                   pl.BlockSpec((FOX_HEADS, tm), lambda i: (0, i)), hbm),
        scratch_shapes=list(CHIP_GATHER_SEMS),
        compiler_params=_params(("arbitrary",), has_side_effects=True),
    )(x, g_pre, scale1p, shift, w_rows, w_t_fox, cum, cos_t, sin_t, w_out_halves)


def _diag_chunks(d, bq, bk, chunk):
    out = []
    for c0 in range(0, bq, chunk):
        if d is None or d * bk + bk - 1 <= c0:
            out.append((c0, None, bk))
        elif d * bk <= c0 + chunk - 1:
            n_keys = min(bk, c0 + chunk - d * bk)
            kpos = d * bk + lax.broadcasted_iota(jnp.int32, (n_keys, chunk), 0)
            qpos = c0 + lax.broadcasted_iota(jnp.int32, (n_keys, chunk), 1)
            out.append((c0, kpos <= qpos, n_keys))
    return out


def _fox_fwd(qat, ka, vt, m_own, bq, bk, chunk, running_max):
    nh, _, s = qat.shape
    r = bq // bk

    pairs = [(i, j) for i in range(s // bq) for j in range(i * r + r)]

    def body(i_tab, j_tab, ka_ref, qat_ref, vt_ref, mo_ref, o_ref, lse_ref, bad_ref, *rest):
        pt_ref, m_scr, acc_scr = (None,) * running_max + rest
        i, j = i_tab[pl.program_id(1)], j_tab[pl.program_id(1)]

        @pl.when(j == 0)
        def _():
            m_scr[...] = jnp.full(m_scr.shape, NEG, F32) if running_max else mo_ref[0]
            acc_scr[...] = jnp.zeros(acc_scr.shape, F32)

        def careful(d):
            kv, vtv = ka_ref[0], vt_ref[0]

            def one_chunk(n, carry):
                c0 = pl.multiple_of(n * chunk, chunk)
                cs = pl.ds(c0, chunk)
                sc = _dot(kv, qat_ref[0, :, cs])
                if d is not None:
                    kpos = d * bk + lax.broadcasted_iota(jnp.int32, (bk, chunk), 0)
                    qpos = c0 + lax.broadcasted_iota(jnp.int32, (bk, chunk), 1)
                    sc = jnp.where(kpos <= qpos, sc, NEG)
                m_prev = m_scr[:, cs]
                m_new = jnp.maximum(m_prev, jnp.max(sc, axis=0, keepdims=True))
                p = jnp.exp2(sc - m_new).astype(BF)
                acc_scr[:, cs] = jnp.exp2(m_prev - m_new) * acc_scr[:, cs] + _dot(vtv, p)
                m_scr[:, cs] = m_new
                return carry

            lax.fori_loop(0, bq // chunk, one_chunk, 0)

        def fast(d):
            todo = _diag_chunks(d, bq, bk, chunk)
            scores = lambda t: _dot(ka_ref[0, :t[2], :], qat_ref[0, :, t[0]:t[0] + chunk])
            sc_next = scores(todo[0])
            for n, (c0, mask, n_keys) in enumerate(todo):
                cs = slice(c0, c0 + chunk)
                sc = sc_next
                if n + 1 < len(todo):
                    sc_next = scores(todo[n + 1])
                if mask is not None:
                    sc = jnp.where(mask, sc, NEG)
                p = jnp.exp2(sc - m_scr[:, cs]).astype(BF)
                pt_ref[0, :n_keys, cs] = p
                acc_scr[:, cs] += _dot(vt_ref[0, :, :n_keys], p)

        step = careful if running_max else fast

        @pl.when(j < i * r)
        def _():
            step(None)

        for d in range(r):
            @pl.when(j == i * r + d)
            def _(d=d):
                step(d)

        @pl.when(j == i * r + r - 1)
        def _():
            l = acc_scr[HEAD_DIM:HEAD_DIM + 1, :]
            o_ref[0] = acc_scr[:HEAD_DIM, :] / l
            lse_ref[0] = m_scr[...] + jnp.log2(l)
            bad_ref[0] = jnp.where(l < OVERFLOW_GUARD, 0.0, 1.0)

    qmap_t = lambda h, t, it, jt: (h, 0, it[t])
    qrow = pl.BlockSpec((1, 1, bq), qmap_t)
    row_shape = jax.ShapeDtypeStruct((nh, 1, s), F32)
    out_shape = (jax.ShapeDtypeStruct((nh, HEAD_DIM, s), F32), row_shape, row_shape)
    out_specs = (pl.BlockSpec((1, HEAD_DIM, bq), qmap_t), qrow, qrow)
    if not running_max:
        out_shape += (jax.ShapeDtypeStruct((nh, s, s), BF),)
        out_specs += (pl.BlockSpec((1, bk, bq), lambda h, t, it, jt: (h, jt[t], it[t])),)
    grid_spec = pltpu.PrefetchScalarGridSpec(
        num_scalar_prefetch=2, grid=(nh, len(pairs)),
        in_specs=[pl.BlockSpec((1, bk, AUG_DIM), lambda h, t, it, jt: (h, jt[t], 0)), pl.BlockSpec((1, AUG_DIM, bq), qmap_t),
                  pl.BlockSpec((1, VT_ROWS, bk), lambda h, t, it, jt: (h, 0, jt[t])), qrow],
        out_specs=out_specs,
        scratch_shapes=[pltpu.VMEM((1, bq), F32), pltpu.VMEM((VT_ROWS, bq), F32)])
    return _call(
        body, name="fox_fwd_running_max" if running_max else "fox_fwd", out_shape=out_shape, grid_spec=grid_spec,
        compiler_params=_params(("parallel", "arbitrary")),
    )(jnp.asarray([p[0] for p in pairs], jnp.int32), jnp.asarray([p[1] for p in pairs], jnp.int32), ka, qat, vt, m_own)


def _fox_bwd(qat, ka, kat, v, dot_, lse, delta, pt, bq, bk, chunk, dq_blk):
    nh, _, s = qat.shape
    r = bq // bk
    nq = s // bq
    stored = pt is not None

    pairs = [(j, i) for j in range(s // bk) for i in range(j // r, nq)]

    def body(j_tab, i_tab, a_ref, b_ref, kat_ref, v_ref, qat_ref, do_ref, dl_ref, dq_ref, dk_ref, dv_ref, dk_scr, dv_scr):
        ka_ref, lse_ref, pt_ref = (None, None, a_ref) if stored else (a_ref, b_ref, None)
        j, i = j_tab[pl.program_id(1)], i_tab[pl.program_id(1)]

        @pl.when(pl.program_id(1) == 0)
        def _():
            dq_ref[...] = jnp.zeros(dq_ref.shape, F32)

        @pl.when(i * r <= j)
        def _():
            dk_scr[...] = jnp.zeros(dk_scr.shape, F32)
            dv_scr[...] = jnp.zeros(dv_scr.shape, F32)

        def step(d):
            todo = _diag_chunks(d, bq, bk, chunk)

            def products(t):
                cs = slice(t[0], t[0] + chunk)
                return (None if stored else _dot(ka_ref[0, :t[2], :], qat_ref[0, :, cs]),
                        _dot(v_ref[0, :t[2], :], do_ref[0, :, cs]))

            nxt = products(todo[0])
            for n, (c0, mask, n_keys) in enumerate(todo):
                cs = slice(c0, c0 + chunk)
                sc, dp = nxt
                if n + 1 < len(todo):
                    nxt = products(todo[n + 1])
                if stored:
                    p_bf = pt_ref[0, :n_keys, cs]
                    p = p_bf.astype(F32)
                else:
                    p = jnp.exp2(sc - lse_ref[0, :, cs])
                    if mask is not None:
                        p = jnp.where(mask, p, 0.0)
                    p_bf = p.astype(BF)
                ds = (p * (dp - dl_ref[0, :, cs])).astype(BF)
                dv_scr[:, :n_keys] += _dot_nt(do_ref[0, :, cs], p_bf)
                dk_scr[:, :n_keys] += _dot_nt(qat_ref[0, :VT_ROWS, cs], ds)
                c1 = c0 % dq_blk
                dq_ref[0, i * (bq // dq_blk) + c0 // dq_blk, :, c1:c1 + chunk] += _dot(kat_ref[0, :VT_ROWS, :n_keys], ds)

        @pl.when(i * r > j)
        def _():
            step(None)

        for d in range(r):
            @pl.when(j == i * r + d)
            def _(d=d):
                step(d)

        @pl.when(i == nq - 1)
        def _():
            dk_ref[0] = dk_scr[...]
            dv_ref[0] = dv_scr[...]

    qmap = lambda h, t, jt, it: (h, 0, it[t])
    kmap = lambda h, t, jt, it: (h, jt[t], 0)
    kmap_t = lambda h, t, jt, it: (h, 0, jt[t])
    if stored:
        first = [(pt, pl.BlockSpec((1, bk, bq), lambda h, t, jt, it: (h, jt[t], it[t]))),
                 (delta, pl.BlockSpec((1, 1, bq), qmap))]
    else:
        first = [(ka, pl.BlockSpec((1, bk, AUG_DIM), kmap)), (lse, pl.BlockSpec((1, 1, bq), qmap))]
    grid_spec = pltpu.PrefetchScalarGridSpec(
        num_scalar_prefetch=2, grid=(nh, len(pairs)),
        in_specs=[first[0][1], first[1][1], pl.BlockSpec((1, AUG_DIM, bk), kmap_t), pl.BlockSpec((1, bk, HEAD_DIM), kmap),
                  pl.BlockSpec((1, AUG_DIM, bq), qmap), pl.BlockSpec((1, HEAD_DIM, bq), qmap),
                  pl.BlockSpec((1, 1, bq), qmap)],
        out_specs=(pl.BlockSpec((1, s // dq_blk, VT_ROWS, dq_blk), lambda h, t, jt, it: (h, 0, 0, 0)),
                   pl.BlockSpec((1, VT_ROWS, bk), kmap_t), pl.BlockSpec((1, HEAD_DIM, bk), kmap_t)),
        scratch_shapes=[pltpu.VMEM((VT_ROWS, bk), F32), pltpu.VMEM((HEAD_DIM, bk), F32)])
    return _call(
        body, name="fox_bwd" if stored else "fox_bwd_recompute",
        out_shape=(jax.ShapeDtypeStruct((nh, s // dq_blk, VT_ROWS, dq_blk), F32),
                   jax.ShapeDtypeStruct((nh, VT_ROWS, s), F32), jax.ShapeDtypeStruct((nh, HEAD_DIM, s), F32)),
        grid_spec=grid_spec, compiler_params=_params(("parallel", "arbitrary")),
    )(jnp.asarray([p[0] for p in pairs], jnp.int32), jnp.asarray([p[1] for p in pairs], jnp.int32),
      first[0][0], first[1][0], kat, v, qat, dot_, delta)


def _swa_mask(i, tq):
    kpos = i * tq - WINDOW + lax.broadcasted_iota(jnp.int32, (tq + WINDOW, tq), 0)
    qpos = i * tq + lax.broadcasted_iota(jnp.int32, (tq + WINDOW, tq), 1)
    rel = qpos - kpos
    return (rel >= 0) & (rel < WINDOW) & (kpos >= 0)


def _swa_rows(ref, g, i, tq):
    before = pl.multiple_of(jnp.maximum(i * tq - WINDOW, 0), WINDOW)
    return jnp.concatenate([ref[g, pl.ds(before, WINDOW), :], ref[g, pl.ds(pl.multiple_of(i * tq, tq), tq), :]], axis=0)


def _swa_before(n_rows, tq):
    return pl.BlockSpec((SWA_KV_HEADS, n_rows, WINDOW), lambda i: (0, 0, jnp.maximum(i * (tq // WINDOW) - 1, 0)))


def _swa_probs_t(sc, mask, sink):
    sc = jnp.where(mask, sc, NEG)
    m = jnp.maximum(jnp.max(sc, axis=0, keepdims=True), sink)
    p = jnp.exp(sc - m)
    e_sink = jnp.exp(sink - m)
    inv_l = 1.0 / (jnp.sum(p, axis=0, keepdims=True) + e_sink)
    return p * inv_l, e_sink * inv_l


def _swa_fwd(qbt, kb, vbt, sinks, tq):
    s = qbt.shape[1]
    n_heads = SWA_KV_HEADS * SWA_GROUP

    def body(q_ref, k_ref, vb_ref, vc_ref, s_ref, o_ref):
        i = pl.program_id(0)
        mask = _swa_mask(i, tq)
        kw = [_swa_rows(k_ref, g, i, tq) for g in range(SWA_KV_HEADS)]
        vtw = [jnp.concatenate([vb_ref[g], vc_ref[g]], axis=1) for g in range(SWA_KV_HEADS)]
        scores = lambda hd: _dot(kw[hd // SWA_GROUP], q_ref[hd * HEAD_DIM:(hd + 1) * HEAD_DIM, :])
        sc_next = scores(0)
        for hd in range(n_heads):
            g, hh = divmod(hd, SWA_GROUP)
            rows = slice(hd * HEAD_DIM, (hd + 1) * HEAD_DIM)
            sink = s_ref[g][:, hh:hh + 1]
            sc = jnp.where(mask, sc_next, NEG)
            if hd + 1 < n_heads:
                sc_next = scores(hd + 1)
            m = jnp.maximum(jnp.max(sc, axis=0, keepdims=True), sink)
            acc = _dot(vtw[g], jnp.exp(sc - m).astype(BF))
            o_ref[rows, :] = acc[:HEAD_DIM] / (acc[HEAD_DIM:HEAD_DIM + 1] + jnp.exp(sink - m))

    kvspec = _full((SWA_KV_HEADS, s, HEAD_DIM))
    qspec = pl.BlockSpec((SWA_W, tq), lambda i: (0, i))
    return _call(
        body, name="swa_fwd", out_shape=jax.ShapeDtypeStruct((SWA_W, s), F32), grid=(s // tq,),
        in_specs=[qspec, kvspec, _swa_before(VT_ROWS, tq), pl.BlockSpec((SWA_KV_HEADS, VT_ROWS, tq), lambda i: (0, 0, i)),
                  _full((SWA_KV_HEADS, 1, SWA_GROUP))],
        out_specs=qspec, compiler_params=_params(("parallel",)),
    )(qbt, kb, vbt, vbt, sinks)


def _swa_bwd(qb, qbt, kb, kbt, vb, sinks, dob, dobt, tq):
    s = qb.shape[0]
    n_heads = SWA_KV_HEADS * SWA_GROUP

    def body(q_ref, qt_ref, k_ref, ktb_ref, ktc_ref, v_ref, s_ref, do_ref, dot_ref, dq_ref, dk_ref, dv_ref, ds_ref):
        i = pl.program_id(0)

        @pl.when(i == 0)
        def _():
            dk_ref[...] = jnp.zeros(dk_ref.shape, F32)
            dv_ref[...] = jnp.zeros(dv_ref.shape, F32)
            ds_ref[...] = jnp.zeros(ds_ref.shape, F32)

        mask = _swa_mask(i, tq)
        kw = [_swa_rows(k_ref, g, i, tq) for g in range(SWA_KV_HEADS)]
        vw = [_swa_rows(v_ref, g, i, tq) for g in range(SWA_KV_HEADS)]
        ktw = [jnp.concatenate([ktb_ref[g], ktc_ref[g]], axis=1) for g in range(SWA_KV_HEADS)]
        before = pl.ds(pl.multiple_of(jnp.maximum(i * tq - WINDOW, 0), WINDOW), WINDOW)
        own = pl.ds(pl.multiple_of(i * tq, tq), tq)

        def products(hd):
            rows = slice(hd * HEAD_DIM, (hd + 1) * HEAD_DIM)
            return _dot(kw[hd // SWA_GROUP], qt_ref[rows, :]), _dot(vw[hd // SWA_GROUP], dot_ref[rows, :])

        nxt = products(0)
        for g in range(SWA_KV_HEADS):
            dsinks = []
            dk_acc = jnp.zeros((tq + WINDOW, HEAD_DIM), F32)
            dv_acc = jnp.zeros((tq + WINDOW, HEAD_DIM), F32)
            for hh in range(SWA_GROUP):
                hd = g * SWA_GROUP + hh
                rows = slice(hd * HEAD_DIM, (hd + 1) * HEAD_DIM)
                sc, dp = nxt
                if hd + 1 < n_heads:
                    nxt = products(hd + 1)
                p, p_sink = _swa_probs_t(sc, mask, s_ref[g][:, hh:hh + 1])
                delta = jnp.sum(p * dp, axis=0, keepdims=True)
                dsc = (p * (dp - delta)).astype(BF)
                dq_ref[rows, :] = _dot(ktw[g], dsc)
                dk_acc = dk_acc + _dot(dsc, q_ref[:, rows])
                dv_acc = dv_acc + _dot(p.astype(BF), do_ref[:, rows])
                dsinks.append(-jnp.sum(p_sink * delta, axis=1, keepdims=True))
            dk_ref[g, before, :] += dk_acc[:WINDOW]
            dk_ref[g, own, :] += dk_acc[WINDOW:]
            dv_ref[g, before, :] += dv_acc[:WINDOW]
            dv_ref[g, own, :] += dv_acc[WINDOW:]
            ds_ref[g] += jnp.concatenate(dsinks, axis=1)

    kvspec = _full((SWA_KV_HEADS, s, HEAD_DIM))
    qspec = pl.BlockSpec((tq, SWA_W), lambda i: (i, 0))
    qspec_t = pl.BlockSpec((SWA_W, tq), lambda i: (0, i))
    sspec = _full((SWA_KV_HEADS, 1, SWA_GROUP))
    kvshape = jax.ShapeDtypeStruct((SWA_KV_HEADS, s, HEAD_DIM), F32)
    return _call(
        body, name="swa_bwd",
        out_shape=(jax.ShapeDtypeStruct((SWA_W, s), F32), kvshape, kvshape,
                   jax.ShapeDtypeStruct((SWA_KV_HEADS, 1, SWA_GROUP), F32)),
        grid=(s // tq,),
        in_specs=[qspec, qspec_t, kvspec, _swa_before(HEAD_DIM, tq),
                  pl.BlockSpec((SWA_KV_HEADS, HEAD_DIM, tq), lambda i: (0, 0, i)), kvspec, sspec, qspec, qspec_t],
        out_specs=(qspec_t, kvspec, kvspec, sspec),
        compiler_params=_params(("arbitrary",)),
    )(qb, qbt, kb, kbt, kbt, vb, sinks, dob, dobt)


def _pairs_to_rows(ref, n_rows=HEAD_DIM):
    parts = []
    for a in range(0, FOX_HEADS, 2):
        parts.append(jnp.concatenate([ref[a][:n_rows], ref[a + 1][:n_rows]], axis=0).T)
    return jnp.concatenate(parts, axis=1)


def _blocks_to_rows(ref):
    return jnp.concatenate([ref[a:a + LANES, :].T for a in range(0, ref.shape[0], LANES)], axis=1)


def _out_proj(oat, za, obt, zb, x, tgt, w_out, w_out_t, gate, g_post, inv_l, tm):
    s = x.shape[0]

    def body(oat_ref, za_ref, obt_ref, zb_ref, x_ref, t_ref, w_ref, wt_ref, gate_ref, gp_ref, il_ref,
             dout_ref, doat_ref, dla_ref, dza_ref, dob_ref, dobt_ref, dzb_ref, gw_ref, dgate_ref, dgp_ref, loss_ref):
        i = pl.program_id(0)

        @pl.when(i == 0)
        def _():
            gw_ref[...] = jnp.zeros(gw_ref.shape, F32)
            dgate_ref[...] = jnp.zeros(dgate_ref.shape, F32)
            dgp_ref[...] = jnp.zeros(dgp_ref.shape, F32)
            loss_ref[...] = jnp.zeros(loss_ref.shape, F32)

        oa_v = _pairs_to_rows(oat_ref)
        ob_v = _blocks_to_rows(obt_ref)
        za_v, zb_v = za_ref[...], zb_ref[...]
        sga, sgb = _sigmoid(za_v), _sigmoid(zb_v)
        sila, silb = za_v * sga, zb_v * sgb
        u = jnp.concatenate([oa_v * sila, ob_v * silb], axis=1).astype(BF)
        yv = _dot(u, w_ref[...])
        yhat, rstd = _rms_hat(yv)
        gp, gate_v = gp_ref[...], gate_ref[...]
        nrm = yhat * gp
        diff = (x_ref[...] + gate_v * nrm) - t_ref[...]
        loss_ref[...] += 0.5 * jnp.sum(jnp.sum(diff * diff, axis=1, keepdims=True), axis=0, keepdims=True) / D_MODEL
        dout = diff * (1.0 / D_MODEL)
        dout_ref[...] = dout
        dgate_ref[...] += jnp.sum(dout * nrm, axis=0, keepdims=True)
        dn = dout * gate_v
        dgp_ref[...] += jnp.sum(dn * yhat, axis=0, keepdims=True)
        dyhat = dn * gp
        dy = (rstd * (dyhat - yhat * jnp.mean(dyhat * yhat, axis=1, keepdims=True))).astype(BF)
        gw_ref[...] += _dot_tn(u, dy)
        du = _dot(dy, wt_ref[...])
        dua, dub = du[:, :FOX_W], du[:, FOX_W:]
        doa = dua * sila
        for a in range(0, FOX_HEADS, 2):
            pair_t = doa[:, a * HEAD_DIM:(a + 2) * HEAD_DIM].T
            for hd, rows in ((a, slice(0, HEAD_DIM)), (a + 1, slice(HEAD_DIM, 2 * HEAD_DIM))):
                inv_l = il_ref[hd]
                doat_ref[hd] = (pair_t[rows] * inv_l).astype(BF)
                dla_ref[hd] = jnp.sum(pair_t[rows] * oat_ref[hd], axis=0, keepdims=True) * inv_l
        dob = dub * silb
        dob_ref[...] = dob.astype(BF)
        for a in range(0, SWA_W, LANES):
            dobt_ref[a:a + LANES, :] = dob[:, a:a + LANES].T.astype(BF)
        dza_ref[...] = (dua * oa_v * (sga * (1.0 + za_v * (1.0 - sga)))).astype(BF)
        dzb_ref[...] = (dub * ob_v * (sgb * (1.0 + zb_v * (1.0 - sgb)))).astype(BF)

    row = lambda w: pl.BlockSpec((tm, w), lambda i: (i, 0))
    heads_t = lambda w: pl.BlockSpec((FOX_HEADS, w, tm), lambda i: (0, 0, i))
    vec = _full((1, D_MODEL))
    mat = _full((D_MODEL, D_MODEL))
    out_shape = (
        jax.ShapeDtypeStruct((s, D_MODEL), F32),
        jax.ShapeDtypeStruct((FOX_HEADS, HEAD_DIM, s), BF), jax.ShapeDtypeStruct((FOX_HEADS, 1, s), F32),
        jax.ShapeDtypeStruct((s, FOX_W), BF), jax.ShapeDtypeStruct((s, SWA_W), BF), jax.ShapeDtypeStruct((SWA_W, s), BF),
        jax.ShapeDtypeStruct((s, SWA_W), BF),
        jax.ShapeDtypeStruct((D_MODEL, D_MODEL), F32),
        jax.ShapeDtypeStruct((1, D_MODEL), F32), jax.ShapeDtypeStruct((1, D_MODEL), F32),
        jax.ShapeDtypeStruct((1, 1), F32),
    )
    col = pl.BlockSpec((SWA_W, tm), lambda i: (0, i))
    return _call(
        body, name="out_proj", out_shape=out_shape, grid=(s // tm,),
        in_specs=[heads_t(HEAD_DIM), row(FOX_W), col, row(SWA_W), row(D_MODEL), row(D_MODEL), mat, mat, vec, vec,
                  heads_t(1)],
        out_specs=(row(D_MODEL), heads_t(HEAD_DIM), heads_t(1), row(FOX_W), row(SWA_W), col, row(SWA_W), mat, vec, vec,
                   _full((1, 1))),
        compiler_params=_params(("arbitrary",)),
    )(oat, za, obt, zb, x, tgt, w_out, w_out_t, gate, g_post, inv_l)


def _assemble_dproj(dqt, dkt, dvt, dza, dqb, dzb, dkb, dvb, df, cos_t, sin_t, tm):
    s = dza.shape[0]

    def body(dqt_ref, dkt_ref, dvt_ref, dza_ref, dqb_ref, dzb_ref, dkb_ref, dvb_ref, df_ref, cos_ref, sin_ref, o_ref):
        def cat(ref, n):
            return jnp.concatenate([ref[hd] for hd in range(n)], axis=1)

        cos2, sin2 = cos_ref[...], sin_ref[...]
        cos8 = jnp.concatenate([cos2] * 4, axis=1)
        sin8 = jnp.concatenate([sin2] * 4, axis=1)
        scale = HEAD_DIM ** -0.5
        o_ref[:, C_QA:C_QA + FOX_W] = (_pairs_to_rows(dqt_ref.at[:, 0]) * scale).astype(BF)
        o_ref[:, C_KA:C_KA + FOX_W] = (_pairs_to_rows(dkt_ref) * LN2).astype(BF)
        o_ref[:, C_VA:C_VA + FOX_W] = _pairs_to_rows(dvt_ref).astype(BF)
        o_ref[:, C_ZA:C_ZA + FOX_W] = dza_ref[...]
        dq = _blocks_to_rows(dqb_ref) * scale
        o_ref[:, C_QB:C_QB + SWA_W] = (dq * cos8 - _rope_partner(dq) * sin8).astype(BF)
        o_ref[:, C_ZB:C_ZB + SWA_W] = dzb_ref[...]
        dk = cat(dkb_ref, SWA_KV_HEADS)
        o_ref[:, C_KB:C_KB + SWA_KV_W] = (dk * cos2 - _rope_partner(dk) * sin2).astype(BF)
        o_ref[:, C_VB:C_VB + SWA_KV_W] = cat(dvb_ref, SWA_KV_HEADS).astype(BF)
        o_ref[:, C_F:C_F + LANES] = df_ref[...].astype(BF)

    row = lambda w: pl.BlockSpec((tm, w), lambda i: (i, 0))
    heads = lambda n: pl.BlockSpec((n, tm, HEAD_DIM), lambda i: (0, i, 0))
    heads_t = lambda w: pl.BlockSpec((FOX_HEADS, w, tm), lambda i: (0, 0, i))
    return _call(
        body, name="assemble_dproj", out_shape=jax.ShapeDtypeStruct((s, WP), BF), grid=(s // tm,),
        in_specs=[pl.BlockSpec((FOX_HEADS, 1, VT_ROWS, tm), lambda i: (0, i, 0, 0)), heads_t(VT_ROWS), heads_t(HEAD_DIM),
                  row(FOX_W), pl.BlockSpec((SWA_W, tm), lambda i: (0, i)), row(SWA_W), heads(SWA_KV_HEADS),
                  heads(SWA_KV_HEADS), row(LANES), row(LANES), row(LANES)],
        out_specs=row(WP), compiler_params=_params(("parallel",)),
    )(dqt, dkt, dvt, dza, dqb, dzb, dkb, dvb, df, cos_t, sin_t)


def _in_proj_bwd_x(dproj, w_al_t, x, dout, g_pre, scale1p, tm, parts):
    s = x.shape[0]
    n_steps = s // tm
    masks = [(1, 0), (0, 1), (1, 1)]

    def body(dp_ref, wt_ref, x_ref, dout_ref, g_ref, sc_ref, parts_ref, gx_ref, dsh_ref, dsc_ref, dg_ref, got_ref,
             send_sems, recv_sems, local_sem):
        i = pl.program_id(0)
        cx, cy, cc = lax.axis_index("x"), lax.axis_index("y"), lax.axis_index("c")
        me = 2 * cx + cy
        own = pltpu.make_async_copy(parts_ref.at[me], got_ref.at[me], local_sem)

        def copy(k, send):
            dx, dy = masks[k]
            peer = 2 * (cx ^ dx) + (cy ^ dy)
            return pltpu.make_async_remote_copy(
                src_ref=parts_ref.at[peer if send else me], dst_ref=got_ref.at[me if send else peer],
                send_sem=send_sems.at[k], recv_sem=recv_sems.at[k], device_id=(cx ^ dx, cy ^ dy, cc), device_id_type=MESH)

        @pl.when(i == 0)
        def _():
            dsh_ref[...] = jnp.zeros(dsh_ref.shape, F32)
            dsc_ref[...] = jnp.zeros(dsc_ref.shape, F32)
            dg_ref[...] = jnp.zeros(dg_ref.shape, F32)
            own.start()
            for k in range(len(masks)):
                copy(k, True).start()

        @pl.when(i == n_steps - 1)
        def _():
            for k in range(len(masks)):
                copy(k, False).wait_recv()
            for k in range(len(masks)):
                copy(k, True).wait_send()
            own.wait()

        dh = _dot(dp_ref[...], wt_ref[...])
        xhat, rstd = _rms_hat(x_ref[...])
        g, sc = g_ref[...], sc_ref[...]
        dsh_ref[...] += jnp.sum(dh, axis=0, keepdims=True)
        dhx = dh * xhat
        dsc_ref[...] += jnp.sum(dhx * g, axis=0, keepdims=True)
        dg_ref[...] += jnp.sum(dhx * sc, axis=0, keepdims=True)
        dxhat = dh * (g * sc)
        gx_ref[...] = dout_ref[...] + rstd * (dxhat - xhat * jnp.mean(dxhat * xhat, axis=1, keepdims=True))

    row = lambda w: pl.BlockSpec((tm, w), lambda i: (i, 0))
    vec = _full((1, D_MODEL))
    vshape = jax.ShapeDtypeStruct((1, D_MODEL), F32)
    hbm = pl.BlockSpec(memory_space=pl.ANY)
    return _call(
        body, name="in_proj_bwd_x",
        out_shape=(jax.ShapeDtypeStruct((s, D_MODEL), F32), vshape, vshape, vshape,
                   jax.ShapeDtypeStruct(parts.shape, parts.dtype)),
        grid=(n_steps,),
        in_specs=[row(WP), _full((WP, D_MODEL)), row(D_MODEL), row(D_MODEL), vec, vec, hbm],
        out_specs=(row(D_MODEL), vec, vec, vec, hbm),
        scratch_shapes=[pltpu.SemaphoreType.DMA((3,)), pltpu.SemaphoreType.DMA((3,)), pltpu.SemaphoreType.DMA],
        compiler_params=_params(("arbitrary",), has_side_effects=True),
    )(dproj, w_al_t, x, dout, g_pre, scale1p, parts)


def _in_proj_bwd_w(h, dproj, tk, tn):
    s = h.shape[0]
    n_k = s // tk

    def body(h_ref, dp_ref, gw_ref, acc_scr):
        k = pl.program_id(1)

        @pl.when(k == 0)
        def _():
            acc_scr[...] = jnp.zeros(acc_scr.shape, F32)

        acc_scr[...] += _dot_tn(h_ref[...], dp_ref[...])

        @pl.when(k == n_k - 1)
        def _():
            gw_ref[...] = acc_scr[...].astype(BF)

    return _call(
        body, name="in_proj_bwd_w", out_shape=jax.ShapeDtypeStruct((D_MODEL, WP), BF), grid=(WP // tn, n_k),
        in_specs=[pl.BlockSpec((tk, D_MODEL), lambda n, k: (k, 0)), pl.BlockSpec((tk, tn), lambda n, k: (k, n))],
        out_specs=pl.BlockSpec((D_MODEL, tn), lambda n, k: (0, n)),
        scratch_shapes=[pltpu.VMEM((D_MODEL, tn), F32)],
        compiler_params=_params(("parallel", "arbitrary")),
    )(h, dproj)


def _align_w_in(w_cols):
    def part(name, width):
        return w_cols[:, _SRC[name]:_SRC[name] + width]

    fpad = jnp.pad(part("fa", FOX_HEADS), ((0, 0), (0, LANES - FOX_HEADS)))
    return jnp.concatenate([part("qa", FOX_W), part("ka", FOX_W), part("va", FOX_W), part("za", FOX_W),
                            part("qb", SWA_W), part("zb", SWA_W), part("kb", SWA_KV_W), part("vb", SWA_KV_W), fpad], axis=1)


def _unalign_w_in(g_al):
    def part(c0, width):
        return g_al[:, c0:c0 + width]

    return jnp.concatenate([part(C_QA, FOX_W), part(C_KA, FOX_W), part(C_VA, FOX_W), part(C_F, FOX_HEADS),
                            part(C_ZA, FOX_W), part(C_QB, SWA_W), part(C_KB, SWA_KV_W), part(C_VB, SWA_KV_W),
                            part(C_ZB, SWA_W)], axis=1)


def _rope_tables(positions):
    inv_freq = ROPE_THETA ** (-jnp.arange(HALF, dtype=F32) / HALF)
    ang = positions.astype(F32)[:, None] * inv_freq
    cos, sin = jnp.cos(ang), jnp.sin(ang)
    return jnp.concatenate([cos, cos, cos, cos], axis=1), jnp.concatenate([-sin, sin, -sin, sin], axis=1)


def _tiles(s):
    if s >= 4096:
        return dict(tm=512, blk=512, bq=2048, bk=2048, bk_bwd=2048, chunk=256, tq=256, tm_out=512, tk=1024, tn=1152)
    return dict(tm=128, blk=128, bq=256, bk=256, bk_bwd=256, chunk=128, tq=128, tm_out=128, tk=128, tn=1152)


def kernel(x, c, positions, w_ada, b_ada, g_pre, w_in, b_fgate, sinks, w_out, g_post, loss_target, m_w_ada, m_b_ada, m_g_pre, m_w_in, m_b_fgate, m_sinks, m_w_out, m_g_post, v_w_ada, v_b_ada, v_g_pre, v_w_in, v_b_fgate, v_sinks, v_w_out, v_g_post):
    s = x.shape[1]
    t = _tiles(s)
    nc = s // LANES
    rows = FOX_HEADS * nc
    me = 4 * lax.axis_index("x") + 2 * lax.axis_index("y") + lax.axis_index("c")
    chip = 2 * lax.axis_index("x") + lax.axis_index("y")
    x2, tgt = x[0], loss_target[0]

    a_all, mod_all = _ada_exchange(c, w_ada[0])
    mod_rows = lax.dynamic_index_in_dim(mod_all, me, axis=1, keepdims=False)
    mod = mod_rows.reshape(N_CHIPS, 2, W_ADA_SHARD)[:, 0, :].reshape(1, 3 * D_MODEL) + b_ada
    shift, scale1p, gate = mod[:, :D_MODEL], 1.0 + mod[:, D_MODEL:2 * D_MODEL], mod[:, 2 * D_MODEL:]

    w_in_pad = jnp.pad(w_in[0].astype(BF), ((0, 0), (0, W_IN_SHARD_PAD - W_IN_SHARD)))
    w_all = _allgather_chips(w_in_pad.reshape(2, D_MODEL // 2, -1), "gather_w_in").reshape(N_CHIPS, D_MODEL, -1)
    w_cols = jnp.concatenate([w_all[k, :, :W_IN_SHARD] for k in range(N_CHIPS)], axis=1)
    w_al = _align_w_in(w_cols)
    w_al_t = w_al.T

    cos_t, sin_t = _rope_tables(positions[0])

    f_pad = _forget_logits(x2, g_pre, scale1p, shift, w_al[:, C_F:], t["tk"])
    f_rows = f_pad[:, :FOX_HEADS].T.reshape(rows, LANES)
    bias_rows = jnp.repeat(b_fgate[0], nc)[:, None]
    cum = _log_forget_cumsum(f_rows, bias_rows, nc).reshape(FOX_HEADS, s)
    h, qat, ka, kat, va, vat, za, zb, qb, kb, vb, qbt, kbt, vbt, m_own, w_out_all = _in_proj(
        x2, g_pre, scale1p, shift, w_al[:, C_VA:C_F], w_al_t[:C_ZA], cum, cos_t, sin_t,
        w_out[0].astype(BF).reshape(2, W_OUT_SHARD // 2, D_MODEL), t["tm"])
    w_out_all = w_out_all.reshape(D_MODEL, D_MODEL)
    w_out_t = w_out_all.T
    m_own = m_own[:, None, :]
    fox_args = (qat, ka, vat, m_own, t["bq"], t["bk"], t["chunk"])
    oat, lse, bad, pt = _fox_fwd(*fox_args, running_max=False)
    overflowed = jnp.max(bad) > 0.0
    oat, lse = lax.cond(overflowed, lambda: _fox_fwd(*fox_args, running_max=True)[:2], lambda: (oat, lse))
    inv_l = jnp.where(overflowed, 1.0, jnp.exp2(m_own - lse))
    sinks_g = sinks.reshape(SWA_KV_HEADS, 1, SWA_GROUP)
    obt = _swa_fwd(qbt, kb, vbt, sinks_g, t["tq"])

    dout, doat, delta_a, dza, dob, dobt, dzb, gw_out, dgate, dg_post, loss_part = _out_proj(
        oat, za, obt, zb, x2, tgt, w_out_all, w_out_t, gate, g_post, inv_l, t["tm_out"])

    bwd_args = (qat, ka, kat, va, doat, lse, delta_a)
    bwd_tiles = (t["bq"], t["bk_bwd"], t["chunk"], t["blk"])
    dqt, dkt, dvt = lax.cond(overflowed, lambda: _fox_bwd(*bwd_args, None, *bwd_tiles),
                             lambda: _fox_bwd(*bwd_args, pt, *bwd_tiles))
    dcum = dqt[:, :, HEAD_DIM, :].reshape(FOX_HEADS, s) - dkt[:, HEAD_DIM, :]
    df_rows, db_heads = _log_forget_cumsum_bwd(dcum.reshape(rows, LANES), f_rows, bias_rows, nc)
    df_pad = jnp.pad(df_rows.reshape(FOX_HEADS, s).T, ((0, 0), (0, LANES - FOX_HEADS)))
    dqb, dkb, dvb, dsinks = _swa_bwd(qb, qbt, kb, kbt, vb, sinks_g, dob, dobt, t["tq"])

    dproj = _assemble_dproj(dqt, dkt, dvt, dza, dqb, dzb, dkb, dvb, df_pad, cos_t, sin_t, t["blk"])
    gw_in = _unalign_w_in(_in_proj_bwd_w(h, dproj, 2 * t["tk"], t["tn"]))

    gin = jnp.stack([jnp.pad(gw_in[:, k * W_IN_SHARD:(k + 1) * W_IN_SHARD], ((0, 0), (0, W_IN_SHARD_PAD - W_IN_SHARD)))
                     for k in range(N_CHIPS)])
    gout = gw_out.astype(BF).reshape(N_CHIPS, D_MODEL, W_OUT_SHARD)
    pair = _pair_sum(gin, gout, "pair_sum")
    grad_x, dshift, dscale, dg_pre, from_chips = _in_proj_bwd_x(
        dproj, w_al_t, x2, dout, g_pre, scale1p, t["tm_out"], pair)

    pad_lane = lambda vrow: jnp.pad(vrow, ((0, 0), (0, LANES - vrow.shape[1])))
    packed = jnp.concatenate([dshift, dscale, dgate, dg_pre, dg_post,
                              pad_lane(db_heads.reshape(1, FOX_HEADS)), pad_lane(dsinks.reshape(1, FOX_HEADS)),
                              pad_lane(loss_part)], axis=1)
    parts = _allgather_devices(packed, "gather_partials")
    tot = _sum_devices(parts)
    loss = tot[0, P_LOSS]
    g_b_ada = tot[:, P_DMOD:P_DMOD + 3 * D_MODEL]
    g_g_pre = tot[:, P_GPRE:P_GPRE + D_MODEL]
    g_g_post = tot[:, P_GPOST:P_GPOST + D_MODEL]
    g_b_fgate = tot[:, P_BF:P_BF + FOX_HEADS]
    g_sinks = tot[:, P_SINK:P_SINK + FOX_HEADS]
    dm_shard = lax.dynamic_slice_in_dim(parts[:, 0, :3 * D_MODEL], chip * W_ADA_SHARD, W_ADA_SHARD, axis=1)
    g_w_ada = _grad_w_ada(a_all.T, dm_shard)

    gfull = _sum_chips_and_share(from_chips, "sum_chips_and_share")
    g_w_in = gfull[:, :W_IN_SHARD]
    g_w_out = gfull[:, W_IN_SHARD_PAD:].reshape(W_OUT_SHARD, D_MODEL)

    grads = dict(w_ada=g_w_ada, b_ada=g_b_ada, g_pre=g_g_pre, w_in=g_w_in, b_fgate=g_b_fgate, sinks=g_sinks,
                 w_out=g_w_out, g_post=g_g_post)
    weights = dict(w_ada=w_ada, b_ada=b_ada, g_pre=g_pre, w_in=w_in, b_fgate=b_fgate, sinks=sinks, w_out=w_out, g_post=g_post)
    moms = dict(w_ada=m_w_ada, b_ada=m_b_ada, g_pre=m_g_pre, w_in=m_w_in, b_fgate=m_b_fgate, sinks=m_sinks, w_out=m_w_out, g_post=m_g_post)
    vars_ = dict(w_ada=v_w_ada, b_ada=v_b_ada, g_pre=v_g_pre, w_in=v_w_in, b_fgate=v_b_fgate, sinks=v_sinks, w_out=v_w_out, g_post=v_g_post)
    names = ["w_ada", "b_ada", "g_pre", "w_in", "b_fgate", "sinks", "w_out", "g_post"]
    g_out, d_out, m_out, v_out = [], [], [], []
    for n in names:
        if n == "w_in":
            flat = lambda a: jnp.transpose(a, (2, 0, 1)).reshape(W_IN_SHARD * D_MODEL // LANES, LANES)
            unflat = lambda a: jnp.transpose(a.reshape(W_IN_SHARD, 1, D_MODEL), (1, 2, 0))
            outs = _adamw(flat(w_in), flat(grads[n][None]), flat(moms[n]), flat(vars_[n]), "adamw_" + n)
            go, d, nm, nv = (unflat(a) for a in outs)
        else:
            g2 = grads[n].reshape(weights[n].shape[-2:])
            go, d, nm, nv = _adamw(weights[n], g2, moms[n], vars_[n], "adamw_" + n)
        g_out.append(go)
        d_out.append(d)
        m_out.append(nm)
        v_out.append(nv)
    return (loss, grad_x.reshape(x.shape), *g_out, *d_out, *m_out, *v_out)
```

```python
import jax
import jax.numpy as jnp
from jax import lax
from jax.experimental import pallas as pl
from jax.experimental.pallas import tpu as pltpu

_INTERPRET = False

D_MODEL = 1024
HEAD_DIM = 64
HALF = HEAD_DIM // 2
AUG_DIM = 128
AUG_ROWS = 8
VT_ROWS = 80
LOG2E = 1.4426950408889634
LN2 = 0.6931471805599453
Q_SCALE = LOG2E * 64 ** -0.5
FOX_HEADS = 8
FOX_W = 512
SWA_W = 512
SWA_KV_HEADS = 2
SWA_GROUP = 4
SWA_KV_W = 128
WINDOW = 128
ROPE_THETA = 10000.0
RMS_EPS = 1e-6
IN_WIDTH = 3336
N_CHIPS = 4
N_DEV = 8
W_IN_SHARD = IN_WIDTH // N_CHIPS
W_IN_SHARD_PAD = 896
W_ADA_SHARD = 3 * D_MODEL // N_CHIPS
W_OUT_SHARD = D_MODEL // N_CHIPS
LANES = 128

_SRC = dict(qa=0, ka=512, va=1024, fa=1536, za=1544, qb=2056, kb=2568, vb=2696, zb=2824)
C_QA, C_KA, C_VA, C_ZA, C_QB, C_ZB, C_KB, C_VB, C_F = 0, 512, 1024, 1536, 2048, 2560, 3072, 3200, 3328
WP = 3456

ADAM_LR = 0.001
ADAM_B1 = 0.9
ADAM_B2 = 0.999
ADAM_EPS = 1e-08
ADAM_WD = 0.01
ADAM_STEP = 10
ADAMW_BLOCK_ELEMS = 300_000

VMEM_LIMIT = 56 * 1024 * 1024
NEG = -1e30
OVERFLOW_GUARD = 1e30
MESH = pl.DeviceIdType.MESH
BF = jnp.bfloat16
F32 = jnp.float32

P_DMOD, P_GPRE, P_GPOST, P_BF, P_SINK, P_LOSS, P_LEN = 0, 3072, 4096, 5120, 5248, 5376, 5504


def _call(body, **kw):
    return pl.pallas_call(body, interpret=_INTERPRET, **kw)


def _params(sem=None, **kw):
    return pltpu.CompilerParams(dimension_semantics=sem, vmem_limit_bytes=VMEM_LIMIT, **kw)


def _full(shape):
    zeros = (0,) * len(shape)
    return pl.BlockSpec(shape, lambda *_: zeros)


def _dot(a, b):
    return jnp.dot(a, b, preferred_element_type=F32)


def _dot_nt(a, b):
    return lax.dot_general(a, b, (((1,), (1,)), ((), ())), preferred_element_type=F32)


def _dot_tn(a, b):
    return lax.dot_general(a, b, (((0,), (0,)), ((), ())), preferred_element_type=F32)


def _sigmoid(z):
    return 1.0 / (1.0 + jnp.exp(-z))


def _rope_partner(t):
    w = t.shape[-1]
    lane = lax.broadcasted_iota(jnp.int32, t.shape, t.ndim - 1)
    return jnp.where((lane & (HEAD_DIM - 1)) < HALF, pltpu.roll(t, w - HALF, t.ndim - 1), pltpu.roll(t, HALF, t.ndim - 1))


def _allgather_devices(v, name):
    r, cdim = v.shape
    masks = [(dx, dy, dc) for dx in (0, 1) for dy in (0, 1) for dc in (0, 1)][1:]

    def body(v_ref, out_ref, tot_ref, send_sems, recv_sems):
        x, y, c = lax.axis_index("x"), lax.axis_index("y"), lax.axis_index("c")
        me = 4 * x + 2 * y + c
        out_ref[me] = v_ref[...]
        copies = []
        for k, (dx, dy, dc) in enumerate(masks):
            cp = pltpu.make_async_remote_copy(
                src_ref=v_ref, dst_ref=out_ref.at[me], send_sem=send_sems.at[k], recv_sem=recv_sems.at[k],
                device_id=(x ^ dx, y ^ dy, c ^ dc), device_id_type=MESH)
            cp.start()
            copies.append(cp)
        for k, (dx, dy, dc) in enumerate(masks):
            peer = 4 * (x ^ dx) + 2 * (y ^ dy) + (c ^ dc)
            pltpu.make_async_remote_copy(
                src_ref=v_ref, dst_ref=out_ref.at[peer], send_sem=send_sems.at[k], recv_sem=recv_sems.at[k],
                device_id=(x ^ dx, y ^ dy, c ^ dc), device_id_type=MESH).wait_recv()
        acc = out_ref[0]
        for b in range(1, N_DEV):
            acc = acc + out_ref[b]
        tot_ref[...] = acc
        for cp in copies:
            cp.wait_send()

    vmem = pl.BlockSpec(memory_space=pltpu.VMEM)
    return _call(
        body, name=name,
        out_shape=(jax.ShapeDtypeStruct((N_DEV, r, cdim), v.dtype), jax.ShapeDtypeStruct((r, cdim), v.dtype)),
        in_specs=[vmem], out_specs=(vmem, vmem),
        scratch_shapes=[pltpu.SemaphoreType.DMA((7,)), pltpu.SemaphoreType.DMA((7,))],
        compiler_params=pltpu.CompilerParams(has_side_effects=True),
    )(v)


CHIP_MASKS = [(1, 0), (0, 1), (1, 1)]
CHIP_GATHER_SEMS = [pltpu.SemaphoreType.DMA((2 * len(CHIP_MASKS),)), pltpu.SemaphoreType.DMA((2 * len(CHIP_MASKS),)),
                    pltpu.SemaphoreType.DMA]


def _chip_gather(v_ref, out_ref, send_sems, recv_sems, local_sem):
    n = len(CHIP_MASKS)
    x, y, c = lax.axis_index("x"), lax.axis_index("y"), lax.axis_index("c")
    me = 2 * x + y
    mine = pltpu.make_async_copy(v_ref, out_ref.at[me], local_sem)

    def copy(k, chip, half, to):
        return pltpu.make_async_remote_copy(
            src_ref=v_ref.at[half] if k < n else out_ref.at[chip, half], dst_ref=out_ref.at[chip, half],
            send_sem=send_sems.at[k], recv_sem=recv_sems.at[k], device_id=to, device_id_type=MESH)

    def start():
        mine.start()
        for k, (dx, dy) in enumerate(CHIP_MASKS):
            copy(k, me, c, (x ^ dx, y ^ dy, c)).start()

    def finish():
        passed = []
        for k, (dx, dy) in enumerate(CHIP_MASKS):
            peer = 2 * (x ^ dx) + (y ^ dy)
            copy(k, peer, c, (x, y, c)).wait_recv()
            cp = copy(n + k, peer, c, (x, y, 1 - c))
            cp.start()
            passed.append(cp)
        for k, (dx, dy) in enumerate(CHIP_MASKS):
            copy(n + k, 2 * (x ^ dx) + (y ^ dy), 1 - c, (x, y, c)).wait_recv()
        for k, (dx, dy) in enumerate(CHIP_MASKS):
            copy(k, me, c, (x ^ dx, y ^ dy, c)).wait_send()
        for cp in passed:
            cp.wait_send()
        mine.wait()

    return start, finish


def _allgather_chips(v, name):
    _, r, cdim = v.shape
    n = len(CHIP_MASKS)

    def body(v_ref, out_ref, send_sems, recv_sems, local_sem):
        start, finish = _chip_gather(v_ref, out_ref, send_sems, recv_sems, local_sem)
        start()
        finish()

    return _call(
        body, name=name, out_shape=jax.ShapeDtypeStruct((N_CHIPS, 2, r, cdim), v.dtype),
        in_specs=[pl.BlockSpec(memory_space=pl.ANY)], out_specs=pl.BlockSpec(memory_space=pl.ANY),
        scratch_shapes=[pltpu.SemaphoreType.DMA((2 * n,)), pltpu.SemaphoreType.DMA((2 * n,)), pltpu.SemaphoreType.DMA],
        compiler_params=pltpu.CompilerParams(has_side_effects=True),
    )(v)


def _ada_exchange(c, w_ada_shard):
    masks = [(dx, dy, dc) for dx in (0, 1) for dy in (0, 1) for dc in (0, 1)][1:]
    n = len(masks)

    def body(c_ref, w_ref, a_ref, mod_ref, c_all, send_sems, recv_sems):
        x, y, cc = lax.axis_index("x"), lax.axis_index("y"), lax.axis_index("c")
        me = 4 * x + 2 * y + cc

        def gather(src_ref, dst_ref, first):
            sends = []
            for k, (dx, dy, dc) in enumerate(masks):
                cp = pltpu.make_async_remote_copy(
                    src_ref=src_ref, dst_ref=dst_ref.at[me], send_sem=send_sems.at[first + k],
                    recv_sem=recv_sems.at[first + k], device_id=(x ^ dx, y ^ dy, cc ^ dc), device_id_type=MESH)
                cp.start()
                sends.append(cp)
            for k, (dx, dy, dc) in enumerate(masks):
                peer = 4 * (x ^ dx) + 2 * (y ^ dy) + (cc ^ dc)
                pltpu.make_async_remote_copy(
                    src_ref=src_ref, dst_ref=dst_ref.at[peer], send_sem=send_sems.at[first + k],
                    recv_sem=recv_sems.at[first + k], device_id=(x ^ dx, y ^ dy, cc ^ dc), device_id_type=MESH).wait_recv()
            return sends

        c_all[me] = c_ref[...]
        sends = gather(c_ref, c_all, 0)
        w_bf = w_ref[...].astype(BF)
        for d in range(N_DEV):
            cv = c_all[d]
            a = cv * _sigmoid(cv)
            a_ref[d:d + 1, :] = a
            mod_ref[me, d:d + 1, :] = _dot(a.astype(BF), w_bf)
        sends += gather(mod_ref.at[me], mod_ref, n)
        for cp in sends:
            cp.wait_send()

    vmem = pl.BlockSpec(memory_space=pltpu.VMEM)
    return _call(
        body, name="ada_exchange",
        out_shape=(jax.ShapeDtypeStruct((N_DEV, D_MODEL), F32), jax.ShapeDtypeStruct((N_DEV, N_DEV, W_ADA_SHARD), F32)),
        in_specs=[vmem, vmem], out_specs=(vmem, vmem),
        scratch_shapes=[pltpu.VMEM((N_DEV, 1, D_MODEL), F32), pltpu.SemaphoreType.DMA((2 * n,)),
                        pltpu.SemaphoreType.DMA((2 * n,))],
        compiler_params=_params(has_side_effects=True),
    )(c, w_ada_shard)


def _grad_w_ada(a_t, dm_shard):
    def body(a_ref, dm_ref, out_ref):
        acc = jnp.zeros((D_MODEL, W_ADA_SHARD), F32)
        for b in range(N_DEV):
            acc = acc + a_ref[:, b:b + 1] * dm_ref[b:b + 1, :]
        out_ref[...] = acc

    return _call(body, name="grad_w_ada", out_shape=jax.ShapeDtypeStruct((D_MODEL, W_ADA_SHARD), F32),
                 compiler_params=_params())(a_t, dm_shard)


def _pair_sum(part_a, part_b, name):
    n, r2, ca = part_a.shape
    cb = part_b.shape[2]
    r = r2 // 2
    tr = 128

    def body(a_ref, b_ref, o_ref, land_a, land_b, send_sems, recv_sems):
        x, y, c = lax.axis_index("x"), lax.axis_index("y"), lax.axis_index("c")
        mine = pl.multiple_of(c * r, r)
        theirs = pl.multiple_of((1 - c) * r, r)
        copies = []
        for k in range(n):
            for j, (src, land) in enumerate(((a_ref, land_a), (b_ref, land_b))):
                cp = pltpu.make_async_remote_copy(
                    src_ref=src.at[k, pl.ds(theirs, r), :], dst_ref=land.at[k], send_sem=send_sems.at[2 * k + j],
                    recv_sem=recv_sems.at[2 * k + j], device_id=(x, y, 1 - c), device_id_type=MESH)
                cp.start()
                copies.append(cp)
        for k in range(n):
            copies[2 * k].wait_recv()
            copies[2 * k + 1].wait_recv()

            def rows(i, carry, k=k):
                sl = pl.ds(pl.multiple_of(i * tr, tr), tr)
                own = pl.ds(pl.multiple_of(mine + i * tr, tr), tr)
                o_ref[k, sl, :ca] = (a_ref[k, own, :].astype(F32) + land_a[k, sl, :].astype(F32)).astype(BF)
                o_ref[k, sl, ca:] = (b_ref[k, own, :].astype(F32) + land_b[k, sl, :].astype(F32)).astype(BF)
                return carry

            lax.fori_loop(0, r // tr, rows, 0)
        for cp in copies:
            cp.wait_send()

    vmem = pl.BlockSpec(memory_space=pltpu.VMEM)
    return _call(body, name=name, out_shape=jax.ShapeDtypeStruct((n, r, ca + cb), BF), in_specs=[vmem, vmem],
                 out_specs=vmem,
                 scratch_shapes=[pltpu.VMEM((n, r, ca), BF), pltpu.VMEM((n, r, cb), BF),
                                 pltpu.SemaphoreType.DMA((2 * n,)), pltpu.SemaphoreType.DMA((2 * n,))],
                 compiler_params=_params(has_side_effects=True))(part_a, part_b)


def _sum_chips_and_share(parts, name):
    _, r, cdim = parts.shape
    tr = 128

    def body(p_ref, o_ref, send_sem, recv_sem):
        x, y, c = lax.axis_index("x"), lax.axis_index("y"), lax.axis_index("c")
        mine = pl.multiple_of(c * r, r)

        def rows(n, carry):
            sl = pl.ds(pl.multiple_of(n * tr, tr), tr)
            p = [p_ref[k, sl, :].astype(F32) for k in range(N_CHIPS)]
            o_ref[pl.ds(pl.multiple_of(mine + n * tr, tr), tr), :] = ((p[0] + p[1]) + p[2]) + p[3]
            return carry

        lax.fori_loop(0, r // tr, rows, 0)
        half = o_ref.at[pl.ds(mine, r), :]
        cp = pltpu.make_async_remote_copy(src_ref=half, dst_ref=half, send_sem=send_sem, recv_sem=recv_sem,
                                          device_id=(x, y, 1 - c), device_id_type=MESH)
        cp.start()
        other = o_ref.at[pl.ds(pl.multiple_of((1 - c) * r, r), r), :]
        pltpu.make_async_remote_copy(src_ref=other, dst_ref=other, send_sem=send_sem, recv_sem=recv_sem,
                                     device_id=(x, y, 1 - c), device_id_type=MESH).wait_recv()
        cp.wait_send()

    vmem = pl.BlockSpec(memory_space=pltpu.VMEM)
    return _call(body, name=name, out_shape=jax.ShapeDtypeStruct((2 * r, cdim), F32), in_specs=[vmem], out_specs=vmem,
                 scratch_shapes=[pltpu.SemaphoreType.DMA, pltpu.SemaphoreType.DMA],
                 compiler_params=_params(has_side_effects=True))(parts)


def _adamw(w, g, m, v, name):
    r, cdim = w.shape[-2:]
    lead = w.ndim - 2
    tr = r if r <= 256 else max(t for t in range(8, ADAMW_BLOCK_ELEMS // cdim + 1, 8) if r % t == 0)
    c1 = 1.0 / (1.0 - ADAM_B1 ** ADAM_STEP)
    c2 = 1.0 / (1.0 - ADAM_B2 ** ADAM_STEP)

    def body(w_ref, g_ref, m_ref, v_ref, go_ref, d_ref, nm_ref, nv_ref):
        gv = g_ref[...].reshape(go_ref.shape)
        nm = ADAM_B1 * m_ref[...] + (1.0 - ADAM_B1) * gv
        nv = ADAM_B2 * v_ref[...] + (1.0 - ADAM_B2) * (gv * gv)
        m_hat = nm * c1
        v_hat = nv * c2
        go_ref[...] = gv
        d_ref[...] = -ADAM_LR * (m_hat / (jnp.sqrt(v_hat) + ADAM_EPS) + ADAM_WD * w_ref[...])
        nm_ref[...] = nm
        nv_ref[...] = nv

    spec = pl.BlockSpec((1,) * lead + (tr, cdim), lambda i: (0,) * lead + (i, 0))
    shp = jax.ShapeDtypeStruct(w.shape, F32)
    return _call(body, name=name, out_shape=(shp,) * 4, grid=(r // tr,),
                 in_specs=[spec, pl.BlockSpec((tr, cdim), lambda i: (i, 0)), spec, spec],
                 out_specs=(spec,) * 4, compiler_params=_params(("parallel",)))(w, g, m, v)


def _head_of_row(r, nc):
    assert nc & (nc - 1) == 0
    return lax.shift_right_logical(r, nc.bit_length() - 1)


def _chunk_mats(rows, nc, reverse):
    ri = lax.broadcasted_iota(jnp.int32, (rows, rows), 0)
    ci = lax.broadcasted_iota(jnp.int32, (rows, rows), 1)
    same = _head_of_row(ri, nc) == _head_of_row(ci, nc)
    between = jnp.where(same & ((ci > ri) if reverse else (ci < ri)), 1.0, 0.0).astype(F32)
    li = lax.broadcasted_iota(jnp.int32, (LANES, LANES), 0)
    lj = lax.broadcasted_iota(jnp.int32, (LANES, LANES), 1)
    within = jnp.where((li >= lj) if reverse else (li <= lj), 1.0, 0.0).astype(F32)
    return between, within


def _dot_hi(a, b):
    return jnp.dot(a, b, preferred_element_type=F32, precision=lax.Precision.HIGHEST)


def _scan_rows(t, nc, reverse):
    between, within = _chunk_mats(t.shape[0], nc, reverse)
    inner = _dot_hi(t, within)
    tot = jnp.sum(t, axis=1, keepdims=True)
    return inner + _dot_hi(between, jnp.broadcast_to(tot, t.shape))


def _log_forget_cumsum(f_rows, bias_rows, nc):
    def body(f_ref, b_ref, cum_ref):
        z = f_ref[...] + b_ref[...]
        lf = jnp.minimum(z, 0.0) - jnp.log(1.0 + jnp.exp(-jnp.abs(z)))
        cum_ref[...] = _scan_rows(lf, nc, False)

    return _call(body, name="forget_cumsum", out_shape=jax.ShapeDtypeStruct(f_rows.shape, F32),
                 compiler_params=_params())(f_rows, bias_rows)


def _log_forget_cumsum_bwd(dcum_rows, f_rows, bias_rows, nc):
    rows = f_rows.shape[0]

    def body(d_ref, f_ref, b_ref, df_ref, db_ref):
        dlf = _scan_rows(d_ref[...], nc, True)
        z = f_ref[...] + b_ref[...]
        df = dlf * _sigmoid(-z)
        df_ref[...] = df
        hi = lax.broadcasted_iota(jnp.int32, (FOX_HEADS, rows), 0)
        ri = lax.broadcasted_iota(jnp.int32, (FOX_HEADS, rows), 1)
        sel = jnp.where(_head_of_row(ri, nc) == hi, 1.0, 0.0).astype(F32)
        db_ref[...] = jnp.sum(_dot_hi(sel, df), axis=1, keepdims=True)

    return _call(body, name="forget_cumsum_bwd",
                 out_shape=(jax.ShapeDtypeStruct(f_rows.shape, F32), jax.ShapeDtypeStruct((FOX_HEADS, 1), F32)),
                 compiler_params=_params())(dcum_rows, f_rows, bias_rows)


def _rms_hat(xv):
    rstd = lax.rsqrt(jnp.mean(xv * xv, axis=-1, keepdims=True) + RMS_EPS)
    return xv * rstd, rstd


def _modulated(x_ref, g_ref, sc_ref, sh_ref):
    xhat, _ = _rms_hat(x_ref[...])
    return ((xhat * g_ref[...]) * sc_ref[...] + sh_ref[...]).astype(BF)


def _forget_logits(x, g_pre, scale1p, shift, w_f, tm):
    s = x.shape[0]

    def body(x_ref, g_ref, sc_ref, sh_ref, w_ref, f_ref):
        f_ref[...] = _dot(_modulated(x_ref, g_ref, sc_ref, sh_ref), w_ref[...])

    vec = _full((1, D_MODEL))
    return _call(
        body, name="forget_logits", out_shape=jax.ShapeDtypeStruct((s, LANES), F32), grid=(s // tm,),
        in_specs=[pl.BlockSpec((tm, D_MODEL), lambda i: (i, 0)), vec, vec, vec, _full((D_MODEL, LANES))],
        out_specs=pl.BlockSpec((tm, LANES), lambda i: (i, 0)), compiler_params=_params(("parallel",)),
    )(x, g_pre, scale1p, shift, w_f)


def _split3(v):
    hi = v.astype(BF).astype(F32)
    mid = (v - hi).astype(BF).astype(F32)
    lo = ((v - hi) - mid).astype(BF).astype(F32)
    return hi, mid, lo


def _in_proj(x, g_pre, scale1p, shift, w_rows, w_t_fox, cum, cos_t, sin_t, w_out_halves, tm):
    s = x.shape[0]
    r_va, r_za, r_qb, r_zb, r_kb, r_vb = 0, 512, 1024, 1536, 2048, 2176

    def body(x_ref, g_ref, sc_ref, sh_ref, w_ref, wt_ref, cum_ref, cos_ref, sin_ref, wo_ref,
             h_ref, qat_ref, ka_ref, kat_ref, v_ref, vt_ref, za_ref, zb_ref, qb_ref, kb_ref, vb_ref,
             qbt_ref, kbt_ref, vbt_ref, mo_ref, wo_all_ref, send_sems, recv_sems, local_sem):
        start_gather, finish_gather = _chip_gather(wo_ref, wo_all_ref, send_sems, recv_sems, local_sem)

        @pl.when(pl.program_id(0) == 0)
        def _():
            start_gather()

        @pl.when(pl.program_id(0) == s // tm - 1)
        def _():
            finish_gather()

        hb = _modulated(x_ref, g_ref, sc_ref, sh_ref)
        h_ref[...] = hb

        def sec(c0, width):
            return _dot(hb, w_ref[:, c0:c0 + width])

        def sec_t(r0):
            return _dot_nt(wt_ref[r0:r0 + FOX_W, :], hb)

        q_t = sec_t(0) * Q_SCALE
        k_t = sec_t(FOX_W)
        v_t = sec_t(2 * FOX_W)
        va = sec(r_va, FOX_W)
        zeros = jnp.zeros((AUG_DIM - HEAD_DIM - AUG_ROWS, tm), F32)
        ri = lax.broadcasted_iota(jnp.int32, (AUG_ROWS, tm), 0)
        const = jnp.where(ri == AUG_ROWS - 1, 0.0, 1.0)
        ri_v = lax.broadcasted_iota(jnp.int32, (VT_ROWS - HEAD_DIM, tm), 0)
        v_feat = jnp.where(ri_v == 0, 1.0, 0.0).astype(BF)
        for hd in range(FOX_HEADS):
            rows = slice(hd * HEAD_DIM, (hd + 1) * HEAD_DIM)
            cum2 = cum_ref[hd:hd + 1, :] * LOG2E
            hi, mid, lo = (jnp.broadcast_to(part, (AUG_ROWS, tm)) for part in _split3(cum2))
            q_feat = jnp.where(ri == 1, hi, jnp.where(ri == 2, mid, jnp.where(ri == 3, lo, const)))
            k_feat = jnp.where(ri == 4, -hi, jnp.where(ri == 5, -mid, jnp.where(ri == 6, -lo, const)))
            q_aug = jnp.concatenate([q_t[rows], q_feat, zeros], axis=0)
            k_aug = jnp.concatenate([k_t[rows], k_feat, zeros], axis=0)
            mo_ref[hd:hd + 1, :] = jnp.sum(q_t[rows] * k_t[rows], axis=0, keepdims=True) + 1.0
            qat_ref[hd] = q_aug.astype(BF)
            kat_ref[hd] = k_aug.astype(BF)
            ka_ref[hd] = k_aug.T.astype(BF)
            vt_ref[hd] = jnp.concatenate([v_t[rows].astype(BF), v_feat], axis=0)
            v_ref[hd] = va[:, rows].astype(BF)
        za_ref[...] = sec(r_za, FOX_W)
        zb_ref[...] = sec(r_zb, SWA_W)
        cos2, sin2 = cos_ref[...], sin_ref[...]
        cos8 = jnp.concatenate([cos2] * 4, axis=1)
        sin8 = jnp.concatenate([sin2] * 4, axis=1)
        qb = sec(r_qb, SWA_W)
        qb = (qb * cos8 + _rope_partner(qb) * sin8) * (HEAD_DIM ** -0.5)
        qb_ref[...] = qb.astype(BF)
        for a in range(SWA_W // LANES):
            qbt_ref[a * LANES:(a + 1) * LANES, :] = qb[:, a * LANES:(a + 1) * LANES].T.astype(BF)
        kb = sec(r_kb, SWA_KV_W)
        kb = kb * cos2 + _rope_partner(kb) * sin2
        vb = sec(r_vb, SWA_KV_W)
        kb_t, vb_t = kb.T, vb.T
        for hd in range(SWA_KV_HEADS):
            sl = slice(hd * HEAD_DIM, (hd + 1) * HEAD_DIM)
            kb_ref[hd] = kb[:, sl].astype(BF)
            vb_ref[hd] = vb[:, sl].astype(BF)
            kbt_ref[hd] = kb_t[sl].astype(BF)
            vbt_ref[hd] = jnp.concatenate([vb_t[sl].astype(BF), v_feat], axis=0)

    row = lambda w: pl.BlockSpec((tm, w), lambda i: (i, 0))
    heads = lambda n, w=HEAD_DIM: pl.BlockSpec((n, tm, w), lambda i: (0, i, 0))
    heads_t = lambda w: pl.BlockSpec((FOX_HEADS, w, tm), lambda i: (0, 0, i))
    vec = _full((1, D_MODEL))
    hs = lambda a, b: jax.ShapeDtypeStruct((FOX_HEADS, a, b), BF)
    out_shape = (
        jax.ShapeDtypeStruct((s, D_MODEL), BF),
        hs(AUG_DIM, s), hs(s, AUG_DIM), hs(AUG_DIM, s), hs(s, HEAD_DIM), hs(VT_ROWS, s),
        jax.ShapeDtypeStruct((s, FOX_W), F32), jax.ShapeDtypeStruct((s, SWA_W), F32),
        jax.ShapeDtypeStruct((s, SWA_W), BF),
        jax.ShapeDtypeStruct((SWA_KV_HEADS, s, HEAD_DIM), BF), jax.ShapeDtypeStruct((SWA_KV_HEADS, s, HEAD_DIM), BF),
        jax.ShapeDtypeStruct((SWA_W, s), BF),
        jax.ShapeDtypeStruct((SWA_KV_HEADS, HEAD_DIM, s), BF), jax.ShapeDtypeStruct((SWA_KV_HEADS, VT_ROWS, s), BF),
        jax.ShapeDtypeStruct((FOX_HEADS, s), F32),
        jax.ShapeDtypeStruct((N_CHIPS,) + w_out_halves.shape, w_out_halves.dtype),
    )
    kv_t = lambda w: pl.BlockSpec((SWA_KV_HEADS, w, tm), lambda i: (0, 0, i))
    hbm = pl.BlockSpec(memory_space=pl.ANY)
    return _call(
        body, name="in_proj", out_shape=out_shape, grid=(s // tm,),
        in_specs=[row(D_MODEL), vec, vec, vec, _full(w_rows.shape), _full(w_t_fox.shape),
                  pl.BlockSpec((FOX_HEADS, tm), lambda i: (0, i)), row(LANES), row(LANES), hbm],
        out_specs=(row(D_MODEL), heads_t(AUG_DIM), heads(FOX_HEADS, AUG_DIM), heads_t(AUG_DIM), heads(FOX_HEADS),
                   heads_t(VT_ROWS), row(FOX_W), row(SWA_W), row(SWA_W), heads(SWA_KV_HEADS), heads(SWA_KV_HEADS),
                   pl.BlockSpec((SWA_W, tm), lambda i: (0, i)), kv_t(HEAD_DIM), kv_t(VT_ROWS),
                   pl.BlockSpec((FOX_HEADS, tm), lambda i: (0, i)), hbm),
        scratch_shapes=list(CHIP_GATHER_SEMS),
        compiler_params=_params(("arbitrary",), has_side_effects=True),
    )(x, g_pre, scale1p, shift, w_rows, w_t_fox, cum, cos_t, sin_t, w_out_halves)


def _diag_chunks(d, bq, bk, chunk):
    out = []
    for c0 in range(0, bq, chunk):
        if d is None or d * bk + bk - 1 <= c0:
            out.append((c0, None, bk))
        elif d * bk <= c0 + chunk - 1:
            n_keys = min(bk, c0 + chunk - d * bk)
            kpos = d * bk + lax.broadcasted_iota(jnp.int32, (n_keys, chunk), 0)
            qpos = c0 + lax.broadcasted_iota(jnp.int32, (n_keys, chunk), 1)
            out.append((c0, kpos <= qpos, n_keys))
    return out


def _fox_fwd(qat, ka, vt, m_own, bq, bk, chunk, running_max):
    nh, _, s = qat.shape
    r = bq // bk

    pairs = [(i, j) for i in range(s // bq) for j in range(i * r + r)]

    def body(i_tab, j_tab, ka_ref, qat_ref, vt_ref, mo_ref, o_ref, lse_ref, bad_ref, *rest):
        pt_ref, m_scr, acc_scr = (None,) * running_max + rest
        i, j = i_tab[pl.program_id(1)], j_tab[pl.program_id(1)]

        @pl.when(j == 0)
        def _():
            m_scr[...] = jnp.full(m_scr.shape, NEG, F32) if running_max else mo_ref[0]
            acc_scr[...] = jnp.zeros(acc_scr.shape, F32)

        def careful(d):
            kv, vtv = ka_ref[0], vt_ref[0]

            def one_chunk(n, carry):
                c0 = pl.multiple_of(n * chunk, chunk)
                cs = pl.ds(c0, chunk)
                sc = _dot(kv, qat_ref[0, :, cs])
                if d is not None:
                    kpos = d * bk + lax.broadcasted_iota(jnp.int32, (bk, chunk), 0)
                    qpos = c0 + lax.broadcasted_iota(jnp.int32, (bk, chunk), 1)
                    sc = jnp.where(kpos <= qpos, sc, NEG)
                m_prev = m_scr[:, cs]
                m_new = jnp.maximum(m_prev, jnp.max(sc, axis=0, keepdims=True))
                p = jnp.exp2(sc - m_new).astype(BF)
                acc_scr[:, cs] = jnp.exp2(m_prev - m_new) * acc_scr[:, cs] + _dot(vtv, p)
                m_scr[:, cs] = m_new
                return carry

            lax.fori_loop(0, bq // chunk, one_chunk, 0)

        def fast(d):
            todo = _diag_chunks(d, bq, bk, chunk)
            scores = lambda t: _dot(ka_ref[0, :t[2], :], qat_ref[0, :, t[0]:t[0] + chunk])
            sc_next = scores(todo[0])
            for n, (c0, mask, n_keys) in enumerate(todo):
                cs = slice(c0, c0 + chunk)
                sc = sc_next
                if n + 1 < len(todo):
                    sc_next = scores(todo[n + 1])
                if mask is not None:
                    sc = jnp.where(mask, sc, NEG)
                p = jnp.exp2(sc - m_scr[:, cs]).astype(BF)
                pt_ref[0, :n_keys, cs] = p
                acc_scr[:, cs] += _dot(vt_ref[0, :, :n_keys], p)

        step = careful if running_max else fast

        @pl.when(j < i * r)
        def _():
            step(None)

        for d in range(r):
            @pl.when(j == i * r + d)
            def _(d=d):
                step(d)

        @pl.when(j == i * r + r - 1)
        def _():
            l = acc_scr[HEAD_DIM:HEAD_DIM + 1, :]
            o_ref[0] = acc_scr[:HEAD_DIM, :] / l
            lse_ref[0] = m_scr[...] + jnp.log2(l)
            bad_ref[0] = jnp.where(l < OVERFLOW_GUARD, 0.0, 1.0)

    qmap_t = lambda h, t, it, jt: (h, 0, it[t])
    qrow = pl.BlockSpec((1, 1, bq), qmap_t)
    row_shape = jax.ShapeDtypeStruct((nh, 1, s), F32)
    out_shape = (jax.ShapeDtypeStruct((nh, HEAD_DIM, s), F32), row_shape, row_shape)
    out_specs = (pl.BlockSpec((1, HEAD_DIM, bq), qmap_t), qrow, qrow)
    if not running_max:
        out_shape += (jax.ShapeDtypeStruct((nh, s, s), BF),)
        out_specs += (pl.BlockSpec((1, bk, bq), lambda h, t, it, jt: (h, jt[t], it[t])),)
    grid_spec = pltpu.PrefetchScalarGridSpec(
        num_scalar_prefetch=2, grid=(nh, len(pairs)),
        in_specs=[pl.BlockSpec((1, bk, AUG_DIM), lambda h, t, it, jt: (h, jt[t], 0)), pl.BlockSpec((1, AUG_DIM, bq), qmap_t),
                  pl.BlockSpec((1, VT_ROWS, bk), lambda h, t, it, jt: (h, 0, jt[t])), qrow],
        out_specs=out_specs,
        scratch_shapes=[pltpu.VMEM((1, bq), F32), pltpu.VMEM((VT_ROWS, bq), F32)])
    return _call(
        body, name="fox_fwd_running_max" if running_max else "fox_fwd", out_shape=out_shape, grid_spec=grid_spec,
        compiler_params=_params(("parallel", "arbitrary")),
    )(jnp.asarray([p[0] for p in pairs], jnp.int32), jnp.asarray([p[1] for p in pairs], jnp.int32), ka, qat, vt, m_own)


def _fox_bwd(qat, ka, kat, v, dot_, lse, delta, pt, bq, bk, chunk, dq_blk):
    nh, _, s = qat.shape
    r = bq // bk
    nq = s // bq
    stored = pt is not None

    pairs = [(j, i) for j in range(s // bk) for i in range(j // r, nq)]

    def body(j_tab, i_tab, a_ref, b_ref, kat_ref, v_ref, qat_ref, do_ref, dl_ref, dq_ref, dk_ref, dv_ref, dk_scr, dv_scr):
        ka_ref, lse_ref, pt_ref = (None, None, a_ref) if stored else (a_ref, b_ref, None)
        j, i = j_tab[pl.program_id(1)], i_tab[pl.program_id(1)]

        @pl.when(pl.program_id(1) == 0)
        def _():
            dq_ref[...] = jnp.zeros(dq_ref.shape, F32)

        @pl.when(i * r <= j)
        def _():
            dk_scr[...] = jnp.zeros(dk_scr.shape, F32)
            dv_scr[...] = jnp.zeros(dv_scr.shape, F32)

        def step(d):
            todo = _diag_chunks(d, bq, bk, chunk)

            def products(t):
                cs = slice(t[0], t[0] + chunk)
                return (None if stored else _dot(ka_ref[0, :t[2], :], qat_ref[0, :, cs]),
                        _dot(v_ref[0, :t[2], :], do_ref[0, :, cs]))

            nxt = products(todo[0])
            for n, (c0, mask, n_keys) in enumerate(todo):
                cs = slice(c0, c0 + chunk)
                sc, dp = nxt
                if n + 1 < len(todo):
                    nxt = products(todo[n + 1])
                if stored:
                    p_bf = pt_ref[0, :n_keys, cs]
                    p = p_bf.astype(F32)
                else:
                    p = jnp.exp2(sc - lse_ref[0, :, cs])
                    if mask is not None:
                        p = jnp.where(mask, p, 0.0)
                    p_bf = p.astype(BF)
                ds = (p * (dp - dl_ref[0, :, cs])).astype(BF)
                dv_scr[:, :n_keys] += _dot_nt(do_ref[0, :, cs], p_bf)
                dk_scr[:, :n_keys] += _dot_nt(qat_ref[0, :VT_ROWS, cs], ds)
                c1 = c0 % dq_blk
                dq_ref[0, i * (bq // dq_blk) + c0 // dq_blk, :, c1:c1 + chunk] += _dot(kat_ref[0, :VT_ROWS, :n_keys], ds)

        @pl.when(i * r > j)
        def _():
            step(None)

        for d in range(r):
            @pl.when(j == i * r + d)
            def _(d=d):
                step(d)

        @pl.when(i == nq - 1)
        def _():
            dk_ref[0] = dk_scr[...]
            dv_ref[0] = dv_scr[...]

    qmap = lambda h, t, jt, it: (h, 0, it[t])
    kmap = lambda h, t, jt, it: (h, jt[t], 0)
    kmap_t = lambda h, t, jt, it: (h, 0, jt[t])
    if stored:
        first = [(pt, pl.BlockSpec((1, bk, bq), lambda h, t, jt, it: (h, jt[t], it[t]))),
                 (delta, pl.BlockSpec((1, 1, bq), qmap))]
    else:
        first = [(ka, pl.BlockSpec((1, bk, AUG_DIM), kmap)), (lse, pl.BlockSpec((1, 1, bq), qmap))]
    grid_spec = pltpu.PrefetchScalarGridSpec(
        num_scalar_prefetch=2, grid=(nh, len(pairs)),
        in_specs=[first[0][1], first[1][1], pl.BlockSpec((1, AUG_DIM, bk), kmap_t), pl.BlockSpec((1, bk, HEAD_DIM), kmap),
                  pl.BlockSpec((1, AUG_DIM, bq), qmap), pl.BlockSpec((1, HEAD_DIM, bq), qmap),
                  pl.BlockSpec((1, 1, bq), qmap)],
        out_specs=(pl.BlockSpec((1, s // dq_blk, VT_ROWS, dq_blk), lambda h, t, jt, it: (h, 0, 0, 0)),
                   pl.BlockSpec((1, VT_ROWS, bk), kmap_t), pl.BlockSpec((1, HEAD_DIM, bk), kmap_t)),
        scratch_shapes=[pltpu.VMEM((VT_ROWS, bk), F32), pltpu.VMEM((HEAD_DIM, bk), F32)])
    return _call(
        body, name="fox_bwd" if stored else "fox_bwd_recompute",
        out_shape=(jax.ShapeDtypeStruct((nh, s // dq_blk, VT_ROWS, dq_blk), F32),
                   jax.ShapeDtypeStruct((nh, VT_ROWS, s), F32), jax.ShapeDtypeStruct((nh, HEAD_DIM, s), F32)),
        grid_spec=grid_spec, compiler_params=_params(("parallel", "arbitrary")),
    )(jnp.asarray([p[0] for p in pairs], jnp.int32), jnp.asarray([p[1] for p in pairs], jnp.int32),
      first[0][0], first[1][0], kat, v, qat, dot_, delta)


def _swa_mask(i, tq):
    kpos = i * tq - WINDOW + lax.broadcasted_iota(jnp.int32, (tq + WINDOW, tq), 0)
    qpos = i * tq + lax.broadcasted_iota(jnp.int32, (tq + WINDOW, tq), 1)
    rel = qpos - kpos
    return (rel >= 0) & (rel < WINDOW) & (kpos >= 0)


def _swa_rows(ref, g, i, tq):
    before = pl.multiple_of(jnp.maximum(i * tq - WINDOW, 0), WINDOW)
    return jnp.concatenate([ref[g, pl.ds(before, WINDOW), :], ref[g, pl.ds(pl.multiple_of(i * tq, tq), tq), :]], axis=0)


def _swa_before(n_rows, tq):
    return pl.BlockSpec((SWA_KV_HEADS, n_rows, WINDOW), lambda i: (0, 0, jnp.maximum(i * (tq // WINDOW) - 1, 0)))


def _swa_probs_t(sc, mask, sink):
    sc = jnp.where(mask, sc, NEG)
    m = jnp.maximum(jnp.max(sc, axis=0, keepdims=True), sink)
    p = jnp.exp(sc - m)
    e_sink = jnp.exp(sink - m)
    inv_l = 1.0 / (jnp.sum(p, axis=0, keepdims=True) + e_sink)
    return p * inv_l, e_sink * inv_l


def _swa_fwd(qbt, kb, vbt, sinks, tq):
    s = qbt.shape[1]
    n_heads = SWA_KV_HEADS * SWA_GROUP

    def body(q_ref, k_ref, vb_ref, vc_ref, s_ref, o_ref):
        i = pl.program_id(0)
        mask = _swa_mask(i, tq)
        kw = [_swa_rows(k_ref, g, i, tq) for g in range(SWA_KV_HEADS)]
        vtw = [jnp.concatenate([vb_ref[g], vc_ref[g]], axis=1) for g in range(SWA_KV_HEADS)]
        scores = lambda hd: _dot(kw[hd // SWA_GROUP], q_ref[hd * HEAD_DIM:(hd + 1) * HEAD_DIM, :])
        sc_next = scores(0)
        for hd in range(n_heads):
            g, hh = divmod(hd, SWA_GROUP)
            rows = slice(hd * HEAD_DIM, (hd + 1) * HEAD_DIM)
            sink = s_ref[g][:, hh:hh + 1]
            sc = jnp.where(mask, sc_next, NEG)
            if hd + 1 < n_heads:
                sc_next = scores(hd + 1)
            m = jnp.maximum(jnp.max(sc, axis=0, keepdims=True), sink)
            acc = _dot(vtw[g], jnp.exp(sc - m).astype(BF))
            o_ref[rows, :] = acc[:HEAD_DIM] / (acc[HEAD_DIM:HEAD_DIM + 1] + jnp.exp(sink - m))

    kvspec = _full((SWA_KV_HEADS, s, HEAD_DIM))
    qspec = pl.BlockSpec((SWA_W, tq), lambda i: (0, i))
    return _call(
        body, name="swa_fwd", out_shape=jax.ShapeDtypeStruct((SWA_W, s), F32), grid=(s // tq,),
        in_specs=[qspec, kvspec, _swa_before(VT_ROWS, tq), pl.BlockSpec((SWA_KV_HEADS, VT_ROWS, tq), lambda i: (0, 0, i)),
                  _full((SWA_KV_HEADS, 1, SWA_GROUP))],
        out_specs=qspec, compiler_params=_params(("parallel",)),
    )(qbt, kb, vbt, vbt, sinks)


def _swa_bwd(qb, qbt, kb, kbt, vb, sinks, dob, dobt, tq):
    s = qb.shape[0]
    n_heads = SWA_KV_HEADS * SWA_GROUP

    def body(q_ref, qt_ref, k_ref, ktb_ref, ktc_ref, v_ref, s_ref, do_ref, dot_ref, dq_ref, dk_ref, dv_ref, ds_ref):
        i = pl.program_id(0)

        @pl.when(i == 0)
        def _():
            dk_ref[...] = jnp.zeros(dk_ref.shape, F32)
            dv_ref[...] = jnp.zeros(dv_ref.shape, F32)
            ds_ref[...] = jnp.zeros(ds_ref.shape, F32)

        mask = _swa_mask(i, tq)
        kw = [_swa_rows(k_ref, g, i, tq) for g in range(SWA_KV_HEADS)]
        vw = [_swa_rows(v_ref, g, i, tq) for g in range(SWA_KV_HEADS)]
        ktw = [jnp.concatenate([ktb_ref[g], ktc_ref[g]], axis=1) for g in range(SWA_KV_HEADS)]
        before = pl.ds(pl.multiple_of(jnp.maximum(i * tq - WINDOW, 0), WINDOW), WINDOW)
        own = pl.ds(pl.multiple_of(i * tq, tq), tq)

        def products(hd):
            rows = slice(hd * HEAD_DIM, (hd + 1) * HEAD_DIM)
            return _dot(kw[hd // SWA_GROUP], qt_ref[rows, :]), _dot(vw[hd // SWA_GROUP], dot_ref[rows, :])

        nxt = products(0)
        for g in range(SWA_KV_HEADS):
            dsinks = []
            dk_acc = jnp.zeros((tq + WINDOW, HEAD_DIM), F32)
            dv_acc = jnp.zeros((tq + WINDOW, HEAD_DIM), F32)
            for hh in range(SWA_GROUP):
                hd = g * SWA_GROUP + hh
                rows = slice(hd * HEAD_DIM, (hd + 1) * HEAD_DIM)
                sc, dp = nxt
                if hd + 1 < n_heads:
                    nxt = products(hd + 1)
                p, p_sink = _swa_probs_t(sc, mask, s_ref[g][:, hh:hh + 1])
                delta = jnp.sum(p * dp, axis=0, keepdims=True)
                dsc = (p * (dp - delta)).astype(BF)
                dq_ref[rows, :] = _dot(ktw[g], dsc)
                dk_acc = dk_acc + _dot(dsc, q_ref[:, rows])
                dv_acc = dv_acc + _dot(p.astype(BF), do_ref[:, rows])
                dsinks.append(-jnp.sum(p_sink * delta, axis=1, keepdims=True))
            dk_ref[g, before, :] += dk_acc[:WINDOW]
            dk_ref[g, own, :] += dk_acc[WINDOW:]
            dv_ref[g, before, :] += dv_acc[:WINDOW]
            dv_ref[g, own, :] += dv_acc[WINDOW:]
            ds_ref[g] += jnp.concatenate(dsinks, axis=1)

    kvspec = _full((SWA_KV_HEADS, s, HEAD_DIM))
    qspec = pl.BlockSpec((tq, SWA_W), lambda i: (i, 0))
    qspec_t = pl.BlockSpec((SWA_W, tq), lambda i: (0, i))
    sspec = _full((SWA_KV_HEADS, 1, SWA_GROUP))
    kvshape = jax.ShapeDtypeStruct((SWA_KV_HEADS, s, HEAD_DIM), F32)
    return _call(
        body, name="swa_bwd",
        out_shape=(jax.ShapeDtypeStruct((SWA_W, s), F32), kvshape, kvshape,
                   jax.ShapeDtypeStruct((SWA_KV_HEADS, 1, SWA_GROUP), F32)),
        grid=(s // tq,),
        in_specs=[qspec, qspec_t, kvspec, _swa_before(HEAD_DIM, tq),
                  pl.BlockSpec((SWA_KV_HEADS, HEAD_DIM, tq), lambda i: (0, 0, i)), kvspec, sspec, qspec, qspec_t],
        out_specs=(qspec_t, kvspec, kvspec, sspec),
        compiler_params=_params(("arbitrary",)),
    )(qb, qbt, kb, kbt, kbt, vb, sinks, dob, dobt)


def _pairs_to_rows(ref, n_rows=HEAD_DIM):
    parts = []
    for a in range(0, FOX_HEADS, 2):
        parts.append(jnp.concatenate([ref[a][:n_rows], ref[a + 1][:n_rows]], axis=0).T)
    return jnp.concatenate(parts, axis=1)


def _blocks_to_rows(ref):
    return jnp.concatenate([ref[a:a + LANES, :].T for a in range(0, ref.shape[0], LANES)], axis=1)


def _out_proj(oat, za, obt, zb, x, tgt, w_out, w_out_t, gate, g_post, inv_l, tm):
    s = x.shape[0]

    def body(oat_ref, za_ref, obt_ref, zb_ref, x_ref, t_ref, w_ref, wt_ref, gate_ref, gp_ref, il_ref,
             dout_ref, doat_ref, dla_ref, dza_ref, dob_ref, dobt_ref, dzb_ref, gw_ref, dgate_ref, dgp_ref, loss_ref):
        i = pl.program_id(0)

        @pl.when(i == 0)
        def _():
            gw_ref[...] = jnp.zeros(gw_ref.shape, F32)
            dgate_ref[...] = jnp.zeros(dgate_ref.shape, F32)
            dgp_ref[...] = jnp.zeros(dgp_ref.shape, F32)
            loss_ref[...] = jnp.zeros(loss_ref.shape, F32)

        oa_v = _pairs_to_rows(oat_ref)
        ob_v = _blocks_to_rows(obt_ref)
        za_v, zb_v = za_ref[...], zb_ref[...]
        sga, sgb = _sigmoid(za_v), _sigmoid(zb_v)
        sila, silb = za_v * sga, zb_v * sgb
        u = jnp.concatenate([oa_v * sila, ob_v * silb], axis=1).astype(BF)
        yv = _dot(u, w_ref[...])
        yhat, rstd = _rms_hat(yv)
        gp, gate_v = gp_ref[...], gate_ref[...]
        nrm = yhat * gp
        diff = (x_ref[...] + gate_v * nrm) - t_ref[...]
        loss_ref[...] += 0.5 * jnp.sum(jnp.sum(diff * diff, axis=1, keepdims=True), axis=0, keepdims=True) / D_MODEL
        dout = diff * (1.0 / D_MODEL)
        dout_ref[...] = dout
        dgate_ref[...] += jnp.sum(dout * nrm, axis=0, keepdims=True)
        dn = dout * gate_v
        dgp_ref[...] += jnp.sum(dn * yhat, axis=0, keepdims=True)
        dyhat = dn * gp
        dy = (rstd * (dyhat - yhat * jnp.mean(dyhat * yhat, axis=1, keepdims=True))).astype(BF)
        gw_ref[...] += _dot_tn(u, dy)
        du = _dot(dy, wt_ref[...])
        dua, dub = du[:, :FOX_W], du[:, FOX_W:]
        doa = dua * sila
        for a in range(0, FOX_HEADS, 2):
            pair_t = doa[:, a * HEAD_DIM:(a + 2) * HEAD_DIM].T
            for hd, rows in ((a, slice(0, HEAD_DIM)), (a + 1, slice(HEAD_DIM, 2 * HEAD_DIM))):
                inv_l = il_ref[hd]
                doat_ref[hd] = (pair_t[rows] * inv_l).astype(BF)
                dla_ref[hd] = jnp.sum(pair_t[rows] * oat_ref[hd], axis=0, keepdims=True) * inv_l
        dob = dub * silb
        dob_ref[...] = dob.astype(BF)
        for a in range(0, SWA_W, LANES):
            dobt_ref[a:a + LANES, :] = dob[:, a:a + LANES].T.astype(BF)
        dza_ref[...] = (dua * oa_v * (sga * (1.0 + za_v * (1.0 - sga)))).astype(BF)
        dzb_ref[...] = (dub * ob_v * (sgb * (1.0 + zb_v * (1.0 - sgb)))).astype(BF)

    row = lambda w: pl.BlockSpec((tm, w), lambda i: (i, 0))
    heads_t = lambda w: pl.BlockSpec((FOX_HEADS, w, tm), lambda i: (0, 0, i))
    vec = _full((1, D_MODEL))
    mat = _full((D_MODEL, D_MODEL))
    out_shape = (
        jax.ShapeDtypeStruct((s, D_MODEL), F32),
        jax.ShapeDtypeStruct((FOX_HEADS, HEAD_DIM, s), BF), jax.ShapeDtypeStruct((FOX_HEADS, 1, s), F32),
        jax.ShapeDtypeStruct((s, FOX_W), BF), jax.ShapeDtypeStruct((s, SWA_W), BF), jax.ShapeDtypeStruct((SWA_W, s), BF),
        jax.ShapeDtypeStruct((s, SWA_W), BF),
        jax.ShapeDtypeStruct((D_MODEL, D_MODEL), F32),
        jax.ShapeDtypeStruct((1, D_MODEL), F32), jax.ShapeDtypeStruct((1, D_MODEL), F32),
        jax.ShapeDtypeStruct((1, 1), F32),
    )
    col = pl.BlockSpec((SWA_W, tm), lambda i: (0, i))
    return _call(
        body, name="out_proj", out_shape=out_shape, grid=(s // tm,),
        in_specs=[heads_t(HEAD_DIM), row(FOX_W), col, row(SWA_W), row(D_MODEL), row(D_MODEL), mat, mat, vec, vec,
                  heads_t(1)],
        out_specs=(row(D_MODEL), heads_t(HEAD_DIM), heads_t(1), row(FOX_W), row(SWA_W), col, row(SWA_W), mat, vec, vec,
                   _full((1, 1))),
        compiler_params=_params(("arbitrary",)),
    )(oat, za, obt, zb, x, tgt, w_out, w_out_t, gate, g_post, inv_l)


def _assemble_dproj(dqt, dkt, dvt, dza, dqb, dzb, dkb, dvb, df, cos_t, sin_t, tm):
    s = dza.shape[0]

    def body(dqt_ref, dkt_ref, dvt_ref, dza_ref, dqb_ref, dzb_ref, dkb_ref, dvb_ref, df_ref, cos_ref, sin_ref, o_ref):
        def cat(ref, n):
            return jnp.concatenate([ref[hd] for hd in range(n)], axis=1)

        cos2, sin2 = cos_ref[...], sin_ref[...]
        cos8 = jnp.concatenate([cos2] * 4, axis=1)
        sin8 = jnp.concatenate([sin2] * 4, axis=1)
        scale = HEAD_DIM ** -0.5
        o_ref[:, C_QA:C_QA + FOX_W] = (_pairs_to_rows(dqt_ref.at[:, 0]) * scale).astype(BF)
        o_ref[:, C_KA:C_KA + FOX_W] = (_pairs_to_rows(dkt_ref) * LN2).astype(BF)
        o_ref[:, C_VA:C_VA + FOX_W] = _pairs_to_rows(dvt_ref).astype(BF)
        o_ref[:, C_ZA:C_ZA + FOX_W] = dza_ref[...]
        dq = _blocks_to_rows(dqb_ref) * scale
        o_ref[:, C_QB:C_QB + SWA_W] = (dq * cos8 - _rope_partner(dq) * sin8).astype(BF)
        o_ref[:, C_ZB:C_ZB + SWA_W] = dzb_ref[...]
        dk = cat(dkb_ref, SWA_KV_HEADS)
        o_ref[:, C_KB:C_KB + SWA_KV_W] = (dk * cos2 - _rope_partner(dk) * sin2).astype(BF)
        o_ref[:, C_VB:C_VB + SWA_KV_W] = cat(dvb_ref, SWA_KV_HEADS).astype(BF)
        o_ref[:, C_F:C_F + LANES] = df_ref[...].astype(BF)

    row = lambda w: pl.BlockSpec((tm, w), lambda i: (i, 0))
    heads = lambda n: pl.BlockSpec((n, tm, HEAD_DIM), lambda i: (0, i, 0))
    heads_t = lambda w: pl.BlockSpec((FOX_HEADS, w, tm), lambda i: (0, 0, i))
    return _call(
        body, name="assemble_dproj", out_shape=jax.ShapeDtypeStruct((s, WP), BF), grid=(s // tm,),
        in_specs=[pl.BlockSpec((FOX_HEADS, 1, VT_ROWS, tm), lambda i: (0, i, 0, 0)), heads_t(VT_ROWS), heads_t(HEAD_DIM),
                  row(FOX_W), pl.BlockSpec((SWA_W, tm), lambda i: (0, i)), row(SWA_W), heads(SWA_KV_HEADS),
                  heads(SWA_KV_HEADS), row(LANES), row(LANES), row(LANES)],
        out_specs=row(WP), compiler_params=_params(("parallel",)),
    )(dqt, dkt, dvt, dza, dqb, dzb, dkb, dvb, df, cos_t, sin_t)


def _in_proj_bwd_x(dproj, w_al_t, x, dout, g_pre, scale1p, tm, parts):
    s = x.shape[0]
    n_steps = s // tm
    masks = [(1, 0), (0, 1), (1, 1)]

    def body(dp_ref, wt_ref, x_ref, dout_ref, g_ref, sc_ref, parts_ref, gx_ref, dsh_ref, dsc_ref, dg_ref, got_ref,
             send_sems, recv_sems, local_sem):
        i = pl.program_id(0)
        cx, cy, cc = lax.axis_index("x"), lax.axis_index("y"), lax.axis_index("c")
        me = 2 * cx + cy
        own = pltpu.make_async_copy(parts_ref.at[me], got_ref.at[me], local_sem)

        def copy(k, send):
            dx, dy = masks[k]
            peer = 2 * (cx ^ dx) + (cy ^ dy)
            return pltpu.make_async_remote_copy(
                src_ref=parts_ref.at[peer if send else me], dst_ref=got_ref.at[me if send else peer],
                send_sem=send_sems.at[k], recv_sem=recv_sems.at[k], device_id=(cx ^ dx, cy ^ dy, cc), device_id_type=MESH)

        @pl.when(i == 0)
        def _():
            dsh_ref[...] = jnp.zeros(dsh_ref.shape, F32)
            dsc_ref[...] = jnp.zeros(dsc_ref.shape, F32)
            dg_ref[...] = jnp.zeros(dg_ref.shape, F32)
            own.start()
            for k in range(len(masks)):
                copy(k, True).start()

        @pl.when(i == n_steps - 1)
        def _():
            for k in range(len(masks)):
                copy(k, False).wait_recv()
            for k in range(len(masks)):
                copy(k, True).wait_send()
            own.wait()

        dh = _dot(dp_ref[...], wt_ref[...])
        xhat, rstd = _rms_hat(x_ref[...])
        g, sc = g_ref[...], sc_ref[...]
        dsh_ref[...] += jnp.sum(dh, axis=0, keepdims=True)
        dhx = dh * xhat
        dsc_ref[...] += jnp.sum(dhx * g, axis=0, keepdims=True)
        dg_ref[...] += jnp.sum(dhx * sc, axis=0, keepdims=True)
        dxhat = dh * (g * sc)
        gx_ref[...] = dout_ref[...] + rstd * (dxhat - xhat * jnp.mean(dxhat * xhat, axis=1, keepdims=True))

    row = lambda w: pl.BlockSpec((tm, w), lambda i: (i, 0))
    vec = _full((1, D_MODEL))
    vshape = jax.ShapeDtypeStruct((1, D_MODEL), F32)
    hbm = pl.BlockSpec(memory_space=pl.ANY)
    return _call(
        body, name="in_proj_bwd_x",
        out_shape=(jax.ShapeDtypeStruct((s, D_MODEL), F32), vshape, vshape, vshape,
                   jax.ShapeDtypeStruct(parts.shape, parts.dtype)),
        grid=(n_steps,),
        in_specs=[row(WP), _full((WP, D_MODEL)), row(D_MODEL), row(D_MODEL), vec, vec, hbm],
        out_specs=(row(D_MODEL), vec, vec, vec, hbm),
        scratch_shapes=[pltpu.SemaphoreType.DMA((3,)), pltpu.SemaphoreType.DMA((3,)), pltpu.SemaphoreType.DMA],
        compiler_params=_params(("arbitrary",), has_side_effects=True),
    )(dproj, w_al_t, x, dout, g_pre, scale1p, parts)


def _in_proj_bwd_w(h, dproj, tk, tn):
    s = h.shape[0]
    n_k = s // tk

    def body(h_ref, dp_ref, gw_ref, acc_scr):
        k = pl.program_id(1)

        @pl.when(k == 0)
        def _():
            acc_scr[...] = jnp.zeros(acc_scr.shape, F32)

        acc_scr[...] += _dot_tn(h_ref[...], dp_ref[...])

        @pl.when(k == n_k - 1)
        def _():
            gw_ref[...] = acc_scr[...].astype(BF)

    return _call(
        body, name="in_proj_bwd_w", out_shape=jax.ShapeDtypeStruct((D_MODEL, WP), BF), grid=(WP // tn, n_k),
        in_specs=[pl.BlockSpec((tk, D_MODEL), lambda n, k: (k, 0)), pl.BlockSpec((tk, tn), lambda n, k: (k, n))],
        out_specs=pl.BlockSpec((D_MODEL, tn), lambda n, k: (0, n)),
        scratch_shapes=[pltpu.VMEM((D_MODEL, tn), F32)],
        compiler_params=_params(("parallel", "arbitrary")),
    )(h, dproj)


def _align_w_in(w_cols):
    def part(name, width):
        return w_cols[:, _SRC[name]:_SRC[name] + width]

    fpad = jnp.pad(part("fa", FOX_HEADS), ((0, 0), (0, LANES - FOX_HEADS)))
    return jnp.concatenate([part("qa", FOX_W), part("ka", FOX_W), part("va", FOX_W), part("za", FOX_W),
                            part("qb", SWA_W), part("zb", SWA_W), part("kb", SWA_KV_W), part("vb", SWA_KV_W), fpad], axis=1)


def _unalign_w_in(g_al):
    def part(c0, width):
        return g_al[:, c0:c0 + width]

    return jnp.concatenate([part(C_QA, FOX_W), part(C_KA, FOX_W), part(C_VA, FOX_W), part(C_F, FOX_HEADS),
                            part(C_ZA, FOX_W), part(C_QB, SWA_W), part(C_KB, SWA_KV_W), part(C_VB, SWA_KV_W),
                            part(C_ZB, SWA_W)], axis=1)


def _rope_tables(positions):
    inv_freq = ROPE_THETA ** (-jnp.arange(HALF, dtype=F32) / HALF)
    ang = positions.astype(F32)[:, None] * inv_freq
    cos, sin = jnp.cos(ang), jnp.sin(ang)
    return jnp.concatenate([cos, cos, cos, cos], axis=1), jnp.concatenate([-sin, sin, -sin, sin], axis=1)


def _tiles(s):
    if s >= 4096:
        return dict(tm=512, blk=512, bq=2048, bk=2048, bk_bwd=2048, chunk=256, tq=256, tm_out=512, tk=1024, tn=1152)
    return dict(tm=128, blk=128, bq=256, bk=256, bk_bwd=256, chunk=128, tq=128, tm_out=128, tk=128, tn=1152)


def kernel(x, c, positions, w_ada, b_ada, g_pre, w_in, b_fgate, sinks, w_out, g_post, loss_target, m_w_ada, m_b_ada, m_g_pre, m_w_in, m_b_fgate, m_sinks, m_w_out, m_g_post, v_w_ada, v_b_ada, v_g_pre, v_w_in, v_b_fgate, v_sinks, v_w_out, v_g_post):
    s = x.shape[1]
    t = _tiles(s)
    nc = s // LANES
    rows = FOX_HEADS * nc
    me = 4 * lax.axis_index("x") + 2 * lax.axis_index("y") + lax.axis_index("c")
    chip = 2 * lax.axis_index("x") + lax.axis_index("y")
    x2, tgt = x[0], loss_target[0]

    a_all, mod_all = _ada_exchange(c, w_ada[0])
    mod_rows = lax.dynamic_index_in_dim(mod_all, me, axis=1, keepdims=False)
    mod = mod_rows.reshape(N_CHIPS, 2, W_ADA_SHARD)[:, 0, :].reshape(1, 3 * D_MODEL) + b_ada
    shift, scale1p, gate = mod[:, :D_MODEL], 1.0 + mod[:, D_MODEL:2 * D_MODEL], mod[:, 2 * D_MODEL:]

    w_in_pad = jnp.pad(w_in[0].astype(BF), ((0, 0), (0, W_IN_SHARD_PAD - W_IN_SHARD)))
    w_all = _allgather_chips(w_in_pad.reshape(2, D_MODEL // 2, -1), "gather_w_in").reshape(N_CHIPS, D_MODEL, -1)
    w_cols = jnp.concatenate([w_all[k, :, :W_IN_SHARD] for k in range(N_CHIPS)], axis=1)
    w_al = _align_w_in(w_cols)
    w_al_t = w_al.T

    cos_t, sin_t = _rope_tables(positions[0])

    f_pad = _forget_logits(x2, g_pre, scale1p, shift, w_al[:, C_F:], t["tk"])
    f_rows = f_pad[:, :FOX_HEADS].T.reshape(rows, LANES)
    bias_rows = jnp.repeat(b_fgate[0], nc)[:, None]
    cum = _log_forget_cumsum(f_rows, bias_rows, nc).reshape(FOX_HEADS, s)
    h, qat, ka, kat, va, vat, za, zb, qb, kb, vb, qbt, kbt, vbt, m_own, w_out_all = _in_proj(
        x2, g_pre, scale1p, shift, w_al[:, C_VA:C_F], w_al_t[:C_ZA], cum, cos_t, sin_t,
        w_out[0].astype(BF).reshape(2, W_OUT_SHARD // 2, D_MODEL), t["tm"])
    w_out_all = w_out_all.reshape(D_MODEL, D_MODEL)
    w_out_t = w_out_all.T
    m_own = m_own[:, None, :]
    fox_args = (qat, ka, vat, m_own, t["bq"], t["bk"], t["chunk"])
    oat, lse, bad, pt = _fox_fwd(*fox_args, running_max=False)
    overflowed = jnp.max(bad) > 0.0
    oat, lse = lax.cond(overflowed, lambda: _fox_fwd(*fox_args, running_max=True)[:2], lambda: (oat, lse))
    inv_l = jnp.where(overflowed, 1.0, jnp.exp2(m_own - lse))
    sinks_g = sinks.reshape(SWA_KV_HEADS, 1, SWA_GROUP)
    obt = _swa_fwd(qbt, kb, vbt, sinks_g, t["tq"])

    dout, doat, delta_a, dza, dob, dobt, dzb, gw_out, dgate, dg_post, loss_part = _out_proj(
        oat, za, obt, zb, x2, tgt, w_out_all, w_out_t, gate, g_post, inv_l, t["tm_out"])

    bwd_args = (qat, ka, kat, va, doat, lse, delta_a)
    bwd_tiles = (t["bq"], t["bk_bwd"], t["chunk"], t["blk"])
    dqt, dkt, dvt = lax.cond(overflowed, lambda: _fox_bwd(*bwd_args, None, *bwd_tiles),
                             lambda: _fox_bwd(*bwd_args, pt, *bwd_tiles))
    dcum = dqt[:, :, HEAD_DIM, :].reshape(FOX_HEADS, s) - dkt[:, HEAD_DIM, :]
    df_rows, db_heads = _log_forget_cumsum_bwd(dcum.reshape(rows, LANES), f_rows, bias_rows, nc)
    df_pad = jnp.pad(df_rows.reshape(FOX_HEADS, s).T, ((0, 0), (0, LANES - FOX_HEADS)))
    dqb, dkb, dvb, dsinks = _swa_bwd(qb, qbt, kb, kbt, vb, sinks_g, dob, dobt, t["tq"])

    dproj = _assemble_dproj(dqt, dkt, dvt, dza, dqb, dzb, dkb, dvb, df_pad, cos_t, sin_t, t["blk"])
    gw_in = _unalign_w_in(_in_proj_bwd_w(h, dproj, 2 * t["tk"], t["tn"]))

    gin = jnp.stack([jnp.pad(gw_in[:, k * W_IN_SHARD:(k + 1) * W_IN_SHARD], ((0, 0), (0, W_IN_SHARD_PAD - W_IN_SHARD)))
                     for k in range(N_CHIPS)])
    gout = gw_out.astype(BF).reshape(N_CHIPS, D_MODEL, W_OUT_SHARD)
    pair = _pair_sum(gin, gout, "pair_sum")
    grad_x, dshift, dscale, dg_pre, from_chips = _in_proj_bwd_x(
        dproj, w_al_t, x2, dout, g_pre, scale1p, t["tm_out"], pair)

    pad_lane = lambda vrow: jnp.pad(vrow, ((0, 0), (0, LANES - vrow.shape[1])))
    packed = jnp.concatenate([dshift, dscale, dgate, dg_pre, dg_post,
                              pad_lane(db_heads.reshape(1, FOX_HEADS)), pad_lane(dsinks.reshape(1, FOX_HEADS)),
                              pad_lane(loss_part)], axis=1)
    parts, tot = _allgather_devices(packed, "gather_partials")
    loss = tot[0, P_LOSS]
    g_b_ada = tot[:, P_DMOD:P_DMOD + 3 * D_MODEL]
    g_g_pre = tot[:, P_GPRE:P_GPRE + D_MODEL]
    g_g_post = tot[:, P_GPOST:P_GPOST + D_MODEL]
    g_b_fgate = tot[:, P_BF:P_BF + FOX_HEADS]
    g_sinks = tot[:, P_SINK:P_SINK + FOX_HEADS]
    dm_shard = lax.dynamic_slice_in_dim(parts[:, 0, :3 * D_MODEL], chip * W_ADA_SHARD, W_ADA_SHARD, axis=1)
    g_w_ada = _grad_w_ada(a_all.T, dm_shard)

    gfull = _sum_chips_and_share(from_chips, "sum_chips_and_share")
    g_w_in = gfull[:, :W_IN_SHARD]
    g_w_out = gfull[:, W_IN_SHARD_PAD:].reshape(W_OUT_SHARD, D_MODEL)

    grads = dict(w_ada=g_w_ada, b_ada=g_b_ada, g_pre=g_g_pre, w_in=g_w_in, b_fgate=g_b_fgate, sinks=g_sinks,
                 w_out=g_w_out, g_post=g_g_post)
    weights = dict(w_ada=w_ada, b_ada=b_ada, g_pre=g_pre, w_in=w_in, b_fgate=b_fgate, sinks=sinks, w_out=w_out, g_post=g_post)
    moms = dict(w_ada=m_w_ada, b_ada=m_b_ada, g_pre=m_g_pre, w_in=m_w_in, b_fgate=m_b_fgate, sinks=m_sinks, w_out=m_w_out, g_post=m_g_post)
    vars_ = dict(w_ada=v_w_ada, b_ada=v_b_ada, g_pre=v_g_pre, w_in=v_w_in, b_fgate=v_b_fgate, sinks=v_sinks, w_out=v_w_out, g_post=v_g_post)
    names = ["w_ada", "b_ada", "g_pre", "w_in", "b_fgate", "sinks", "w_out", "g_post"]
    g_out, d_out, m_out, v_out = [], [], [], []
    for n in names:
        if n == "w_in":
            flat = lambda a: jnp.transpose(a, (2, 0, 1)).reshape(W_IN_SHARD * D_MODEL // LANES, LANES)
            unflat = lambda a: jnp.transpose(a.reshape(W_IN_SHARD, 1, D_MODEL), (1, 2, 0))
            outs = _adamw(flat(w_in), flat(grads[n][None]), flat(moms[n]), flat(vars_[n]), "adamw_" + n)
            go, d, nm, nv = (unflat(a) for a in outs)
        else:
            g2 = grads[n].reshape(weights[n].shape[-2:])
            go, d, nm, nv = _adamw(weights[n], g2, moms[n], vars_[n], "adamw_" + n)
        g_out.append(go)
        d_out.append(d)
        m_out.append(nm)
        v_out.append(nv)
    return (loss, grad_x.reshape(x.shape), *g_out, *d_out, *m_out, *v_out)
```

```python
import jax
import jax.numpy as jnp
from jax import lax
from jax.experimental import pallas as pl
from jax.experimental.pallas import tpu as pltpu

_INTERPRET = False

D_MODEL = 1024
HEAD_DIM = 64
HALF = HEAD_DIM // 2
AUG_DIM = 128
AUG_ROWS = 8
VT_ROWS = 80
LOG2E = 1.4426950408889634
LN2 = 0.6931471805599453
Q_SCALE = LOG2E * 64 ** -0.5
FOX_HEADS = 8
FOX_W = 512
SWA_W = 512
SWA_KV_HEADS = 2
SWA_GROUP = 4
SWA_KV_W = 128
WINDOW = 128
ROPE_THETA = 10000.0
RMS_EPS = 1e-6
IN_WIDTH = 3336
N_CHIPS = 4
N_DEV = 8
W_IN_SHARD = IN_WIDTH // N_CHIPS
W_IN_SHARD_PAD = 896
W_ADA_SHARD = 3 * D_MODEL // N_CHIPS
W_OUT_SHARD = D_MODEL // N_CHIPS
LANES = 128

_SRC = dict(qa=0, ka=512, va=1024, fa=1536, za=1544, qb=2056, kb=2568, vb=2696, zb=2824)
C_QA, C_KA, C_VA, C_ZA, C_QB, C_ZB, C_KB, C_VB, C_F = 0, 512, 1024, 1536, 2048, 2560, 3072, 3200, 3328
WP = 3456

ADAM_LR = 0.001
ADAM_B1 = 0.9
ADAM_B2 = 0.999
ADAM_EPS = 1e-08
ADAM_WD = 0.01
ADAM_STEP = 10
ADAMW_BLOCK_ELEMS = 300_000

VMEM_LIMIT = 56 * 1024 * 1024
NEG = -1e30
OVERFLOW_GUARD = 1e30
MESH = pl.DeviceIdType.MESH
BF = jnp.bfloat16
F32 = jnp.float32

P_DMOD, P_GPRE, P_GPOST, P_BF, P_SINK, P_LOSS, P_LEN = 0, 3072, 4096, 5120, 5248, 5376, 5504


def _call(body, **kw):
    return pl.pallas_call(body, interpret=_INTERPRET, **kw)


def _params(sem=None, **kw):
    return pltpu.CompilerParams(dimension_semantics=sem, vmem_limit_bytes=VMEM_LIMIT, **kw)


def _full(shape):
    zeros = (0,) * len(shape)
    return pl.BlockSpec(shape, lambda *_: zeros)


def _dot(a, b):
    return jnp.dot(a, b, preferred_element_type=F32)


def _dot_nt(a, b):
    return lax.dot_general(a, b, (((1,), (1,)), ((), ())), preferred_element_type=F32)


def _dot_tn(a, b):
    return lax.dot_general(a, b, (((0,), (0,)), ((), ())), preferred_element_type=F32)


def _sigmoid(z):
    return 1.0 / (1.0 + jnp.exp(-z))


def _rope_partner(t):
    w = t.shape[-1]
    lane = lax.broadcasted_iota(jnp.int32, t.shape, t.ndim - 1)
    return jnp.where((lane & (HEAD_DIM - 1)) < HALF, pltpu.roll(t, w - HALF, t.ndim - 1), pltpu.roll(t, HALF, t.ndim - 1))


def _allgather_devices(v, name):
    r, cdim = v.shape
    masks = [(dx, dy, dc) for dx in (0, 1) for dy in (0, 1) for dc in (0, 1)][1:]

    def body(v_ref, out_ref, tot_ref, send_sems, recv_sems):
        x, y, c = lax.axis_index("x"), lax.axis_index("y"), lax.axis_index("c")
        me = 4 * x + 2 * y + c
        out_ref[me] = v_ref[...]
        copies = []
        for k, (dx, dy, dc) in enumerate(masks):
            cp = pltpu.make_async_remote_copy(
                src_ref=v_ref, dst_ref=out_ref.at[me], send_sem=send_sems.at[k], recv_sem=recv_sems.at[k],
                device_id=(x ^ dx, y ^ dy, c ^ dc), device_id_type=MESH)
            cp.start()
            copies.append(cp)
        for k, (dx, dy, dc) in enumerate(masks):
            peer = 4 * (x ^ dx) + 2 * (y ^ dy) + (c ^ dc)
            pltpu.make_async_remote_copy(
                src_ref=v_ref, dst_ref=out_ref.at[peer], send_sem=send_sems.at[k], recv_sem=recv_sems.at[k],
                device_id=(x ^ dx, y ^ dy, c ^ dc), device_id_type=MESH).wait_recv()
        acc = out_ref[0]
        for b in range(1, N_DEV):
            acc = acc + out_ref[b]
        tot_ref[...] = acc
        for cp in copies:
            cp.wait_send()

    vmem = pl.BlockSpec(memory_space=pltpu.VMEM)
    return _call(
        body, name=name,
        out_shape=(jax.ShapeDtypeStruct((N_DEV, r, cdim), v.dtype), jax.ShapeDtypeStruct((r, cdim), v.dtype)),
        in_specs=[vmem], out_specs=(vmem, vmem),
        scratch_shapes=[pltpu.SemaphoreType.DMA((7,)), pltpu.SemaphoreType.DMA((7,))],
        compiler_params=pltpu.CompilerParams(has_side_effects=True),
    )(v)


CHIP_MASKS = [(1, 0), (0, 1), (1, 1)]
CHIP_PIECES = 2
CHIP_GATHER_SEMS = [pltpu.SemaphoreType.DMA((2 * len(CHIP_MASKS) * CHIP_PIECES,)),
                    pltpu.SemaphoreType.DMA((2 * len(CHIP_MASKS) * CHIP_PIECES,)), pltpu.SemaphoreType.DMA]


def _chip_gather(v_ref, out_ref, send_sems, recv_sems, local_sem):
    n = len(CHIP_MASKS)
    rp = v_ref.shape[1] // CHIP_PIECES
    x, y, c = lax.axis_index("x"), lax.axis_index("y"), lax.axis_index("c")
    me = 2 * x + y
    mine = pltpu.make_async_copy(v_ref, out_ref.at[me], local_sem)
    order = [(p, k) for p in range(CHIP_PIECES) for k in range(n)]

    def peer_chip(k):
        return 2 * (x ^ CHIP_MASKS[k][0]) + (y ^ CHIP_MASKS[k][1])

    def peer_dev(k):
        return (x ^ CHIP_MASKS[k][0], y ^ CHIP_MASKS[k][1], c)

    def copy(k, p, chip, half, to, passing):
        rows = pl.ds(p * rp, rp)
        sem = ((n + k) if passing else k) * CHIP_PIECES + p
        return pltpu.make_async_remote_copy(
            src_ref=out_ref.at[chip, half, rows] if passing else v_ref.at[half, rows],
            dst_ref=out_ref.at[chip, half, rows], send_sem=send_sems.at[sem], recv_sem=recv_sems.at[sem],
            device_id=to, device_id_type=MESH)

    def start():
        mine.start()
        for p, k in order:
            copy(k, p, me, c, peer_dev(k), False).start()

    def finish():
        passed = []
        for p, k in order:
            copy(k, p, peer_chip(k), c, (x, y, c), False).wait_recv()
            cp = copy(k, p, peer_chip(k), c, (x, y, 1 - c), True)
            cp.start()
            passed.append(cp)
        for p, k in order:
            copy(k, p, peer_chip(k), 1 - c, (x, y, c), True).wait_recv()
        for p, k in order:
            copy(k, p, me, c, peer_dev(k), False).wait_send()
        for cp in passed:
            cp.wait_send()
        mine.wait()

    return start, finish


def _allgather_chips(v, name):
    _, r, cdim = v.shape
    n = len(CHIP_MASKS)

    def body(v_ref, out_ref, send_sems, recv_sems, local_sem):
        start, finish = _chip_gather(v_ref, out_ref, send_sems, recv_sems, local_sem)
        start()
        finish()

    return _call(
        body, name=name, out_shape=jax.ShapeDtypeStruct((N_CHIPS, 2, r, cdim), v.dtype),
        in_specs=[pl.BlockSpec(memory_space=pl.ANY)], out_specs=pl.BlockSpec(memory_space=pl.ANY),
        scratch_shapes=list(CHIP_GATHER_SEMS), compiler_params=pltpu.CompilerParams(has_side_effects=True),
    )(v)


def _ada_exchange(c, w_ada_shard):
    masks = [(dx, dy, dc) for dx in (0, 1) for dy in (0, 1) for dc in (0, 1)][1:]
    n = len(masks)

    def body(c_ref, w_ref, a_ref, mod_ref, c_all, send_sems, recv_sems):
        x, y, cc = lax.axis_index("x"), lax.axis_index("y"), lax.axis_index("c")
        me = 4 * x + 2 * y + cc

        def gather(src_ref, dst_ref, first):
            sends = []
            for k, (dx, dy, dc) in enumerate(masks):
                cp = pltpu.make_async_remote_copy(
                    src_ref=src_ref, dst_ref=dst_ref.at[me], send_sem=send_sems.at[first + k],
                    recv_sem=recv_sems.at[first + k], device_id=(x ^ dx, y ^ dy, cc ^ dc), device_id_type=MESH)
                cp.start()
                sends.append(cp)
            for k, (dx, dy, dc) in enumerate(masks):
                peer = 4 * (x ^ dx) + 2 * (y ^ dy) + (cc ^ dc)
                pltpu.make_async_remote_copy(
                    src_ref=src_ref, dst_ref=dst_ref.at[peer], send_sem=send_sems.at[first + k],
                    recv_sem=recv_sems.at[first + k], device_id=(x ^ dx, y ^ dy, cc ^ dc), device_id_type=MESH).wait_recv()
            return sends

        c_all[me] = c_ref[...]
        sends = gather(c_ref, c_all, 0)
        w_bf = w_ref[...].astype(BF)
        for d in range(N_DEV):
            cv = c_all[d]
            a = cv * _sigmoid(cv)
            a_ref[d:d + 1, :] = a
            mod_ref[me, d:d + 1, :] = _dot(a.astype(BF), w_bf)
        sends += gather(mod_ref.at[me], mod_ref, n)
        for cp in sends:
            cp.wait_send()

    vmem = pl.BlockSpec(memory_space=pltpu.VMEM)
    return _call(
        body, name="ada_exchange",
        out_shape=(jax.ShapeDtypeStruct((N_DEV, D_MODEL), F32), jax.ShapeDtypeStruct((N_DEV, N_DEV, W_ADA_SHARD), F32)),
        in_specs=[vmem, vmem], out_specs=(vmem, vmem),
        scratch_shapes=[pltpu.VMEM((N_DEV, 1, D_MODEL), F32), pltpu.SemaphoreType.DMA((2 * n,)),
                        pltpu.SemaphoreType.DMA((2 * n,))],
        compiler_params=_params(has_side_effects=True),
    )(c, w_ada_shard)


def _grad_w_ada(a_t, dm_shard):
    def body(a_ref, dm_ref, out_ref):
        acc = jnp.zeros((D_MODEL, W_ADA_SHARD), F32)
        for b in range(N_DEV):
            acc = acc + a_ref[:, b:b + 1] * dm_ref[b:b + 1, :]
        out_ref[...] = acc

    return _call(body, name="grad_w_ada", out_shape=jax.ShapeDtypeStruct((D_MODEL, W_ADA_SHARD), F32),
                 compiler_params=_params())(a_t, dm_shard)


def _pair_sum(part_a, part_b, name):
    n, r2, ca = part_a.shape
    cb = part_b.shape[2]
    r = r2 // 2
    tr = 128

    def body(a_ref, b_ref, o_ref, land_a, land_b, send_sems, recv_sems):
        x, y, c = lax.axis_index("x"), lax.axis_index("y"), lax.axis_index("c")
        mine = pl.multiple_of(c * r, r)
        theirs = pl.multiple_of((1 - c) * r, r)
        copies = []
        for k in range(n):
            for j, (src, land) in enumerate(((a_ref, land_a), (b_ref, land_b))):
                cp = pltpu.make_async_remote_copy(
                    src_ref=src.at[k, pl.ds(theirs, r), :], dst_ref=land.at[k], send_sem=send_sems.at[2 * k + j],
                    recv_sem=recv_sems.at[2 * k + j], device_id=(x, y, 1 - c), device_id_type=MESH)
                cp.start()
                copies.append(cp)
        for k in range(n):
            copies[2 * k].wait_recv()
            copies[2 * k + 1].wait_recv()

            def rows(i, carry, k=k):
                sl = pl.ds(pl.multiple_of(i * tr, tr), tr)
                own = pl.ds(pl.multiple_of(mine + i * tr, tr), tr)
                o_ref[k, sl, :ca] = (a_ref[k, own, :].astype(F32) + land_a[k, sl, :].astype(F32)).astype(BF)
                o_ref[k, sl, ca:] = (b_ref[k, own, :].astype(F32) + land_b[k, sl, :].astype(F32)).astype(BF)
                return carry

            lax.fori_loop(0, r // tr, rows, 0)
        for cp in copies:
            cp.wait_send()

    vmem = pl.BlockSpec(memory_space=pltpu.VMEM)
    return _call(body, name=name, out_shape=jax.ShapeDtypeStruct((n, r, ca + cb), BF), in_specs=[vmem, vmem],
                 out_specs=vmem,
                 scratch_shapes=[pltpu.VMEM((n, r, ca), BF), pltpu.VMEM((n, r, cb), BF),
                                 pltpu.SemaphoreType.DMA((2 * n,)), pltpu.SemaphoreType.DMA((2 * n,))],
                 compiler_params=_params(has_side_effects=True))(part_a, part_b)


def _sum_chips_and_share(parts, name):
    _, r, cdim = parts.shape
    tr = 128

    def body(p_ref, o_ref, send_sem, recv_sem):
        x, y, c = lax.axis_index("x"), lax.axis_index("y"), lax.axis_index("c")
        mine = pl.multiple_of(c * r, r)

        def rows(n, carry):
            sl = pl.ds(pl.multiple_of(n * tr, tr), tr)
            p = [p_ref[k, sl, :].astype(F32) for k in range(N_CHIPS)]
            o_ref[pl.ds(pl.multiple_of(mine + n * tr, tr), tr), :] = ((p[0] + p[1]) + p[2]) + p[3]
            return carry

        lax.fori_loop(0, r // tr, rows, 0)
        half = o_ref.at[pl.ds(mine, r), :]
        cp = pltpu.make_async_remote_copy(src_ref=half, dst_ref=half, send_sem=send_sem, recv_sem=recv_sem,
                                          device_id=(x, y, 1 - c), device_id_type=MESH)
        cp.start()
        other = o_ref.at[pl.ds(pl.multiple_of((1 - c) * r, r), r), :]
        pltpu.make_async_remote_copy(src_ref=other, dst_ref=other, send_sem=send_sem, recv_sem=recv_sem,
                                     device_id=(x, y, 1 - c), device_id_type=MESH).wait_recv()
        cp.wait_send()

    vmem = pl.BlockSpec(memory_space=pltpu.VMEM)
    return _call(body, name=name, out_shape=jax.ShapeDtypeStruct((2 * r, cdim), F32), in_specs=[vmem], out_specs=vmem,
                 scratch_shapes=[pltpu.SemaphoreType.DMA, pltpu.SemaphoreType.DMA],
                 compiler_params=_params(has_side_effects=True))(parts)


def _adamw(w, g, m, v, name):
    r, cdim = w.shape[-2:]
    lead = w.ndim - 2
    tr = r if r <= 256 else max(t for t in range(8, ADAMW_BLOCK_ELEMS // cdim + 1, 8) if r % t == 0)
    c1 = 1.0 / (1.0 - ADAM_B1 ** ADAM_STEP)
    c2 = 1.0 / (1.0 - ADAM_B2 ** ADAM_STEP)

    def body(w_ref, g_ref, m_ref, v_ref, go_ref, d_ref, nm_ref, nv_ref):
        gv = g_ref[...].reshape(go_ref.shape)
        nm = ADAM_B1 * m_ref[...] + (1.0 - ADAM_B1) * gv
        nv = ADAM_B2 * v_ref[...] + (1.0 - ADAM_B2) * (gv * gv)
        m_hat = nm * c1
        v_hat = nv * c2
        go_ref[...] = gv
        d_ref[...] = -ADAM_LR * (m_hat / (jnp.sqrt(v_hat) + ADAM_EPS) + ADAM_WD * w_ref[...])
        nm_ref[...] = nm
        nv_ref[...] = nv

    spec = pl.BlockSpec((1,) * lead + (tr, cdim), lambda i: (0,) * lead + (i, 0))
    shp = jax.ShapeDtypeStruct(w.shape, F32)
    return _call(body, name=name, out_shape=(shp,) * 4, grid=(r // tr,),
                 in_specs=[spec, pl.BlockSpec((tr, cdim), lambda i: (i, 0)), spec, spec],
                 out_specs=(spec,) * 4, compiler_params=_params(("parallel",)))(w, g, m, v)


def _head_of_row(r, nc):
    assert nc & (nc - 1) == 0
    return lax.shift_right_logical(r, nc.bit_length() - 1)


def _chunk_mats(rows, nc, reverse):
    ri = lax.broadcasted_iota(jnp.int32, (rows, rows), 0)
    ci = lax.broadcasted_iota(jnp.int32, (rows, rows), 1)
    same = _head_of_row(ri, nc) == _head_of_row(ci, nc)
    between = jnp.where(same & ((ci > ri) if reverse else (ci < ri)), 1.0, 0.0).astype(F32)
    li = lax.broadcasted_iota(jnp.int32, (LANES, LANES), 0)
    lj = lax.broadcasted_iota(jnp.int32, (LANES, LANES), 1)
    within = jnp.where((li >= lj) if reverse else (li <= lj), 1.0, 0.0).astype(F32)
    return between, within


def _dot_hi(a, b):
    return jnp.dot(a, b, preferred_element_type=F32, precision=lax.Precision.HIGHEST)


def _scan_rows(t, nc, reverse):
    between, within = _chunk_mats(t.shape[0], nc, reverse)
    inner = _dot_hi(t, within)
    tot = jnp.sum(t, axis=1, keepdims=True)
    return inner + _dot_hi(between, jnp.broadcast_to(tot, t.shape))


def _log_forget_cumsum(f_rows, bias_rows, nc):
    def body(f_ref, b_ref, cum_ref):
        z = f_ref[...] + b_ref[...]
        lf = jnp.minimum(z, 0.0) - jnp.log(1.0 + jnp.exp(-jnp.abs(z)))
        cum_ref[...] = _scan_rows(lf, nc, False)

    return _call(body, name="forget_cumsum", out_shape=jax.ShapeDtypeStruct(f_rows.shape, F32),
                 compiler_params=_params())(f_rows, bias_rows)


def _log_forget_cumsum_bwd(dcum_rows, f_rows, bias_rows, nc):
    rows = f_rows.shape[0]

    def body(d_ref, f_ref, b_ref, df_ref, db_ref):
        dlf = _scan_rows(d_ref[...], nc, True)
        z = f_ref[...] + b_ref[...]
        df = dlf * _sigmoid(-z)
        df_ref[...] = df
        hi = lax.broadcasted_iota(jnp.int32, (FOX_HEADS, rows), 0)
        ri = lax.broadcasted_iota(jnp.int32, (FOX_HEADS, rows), 1)
        sel = jnp.where(_head_of_row(ri, nc) == hi, 1.0, 0.0).astype(F32)
        db_ref[...] = jnp.sum(_dot_hi(sel, df), axis=1, keepdims=True)

    return _call(body, name="forget_cumsum_bwd",
                 out_shape=(jax.ShapeDtypeStruct(f_rows.shape, F32), jax.ShapeDtypeStruct((FOX_HEADS, 1), F32)),
                 compiler_params=_params())(dcum_rows, f_rows, bias_rows)


def _rms_hat(xv):
    rstd = lax.rsqrt(jnp.mean(xv * xv, axis=-1, keepdims=True) + RMS_EPS)
    return xv * rstd, rstd


def _modulated(x_ref, g_ref, sc_ref, sh_ref):
    xhat, _ = _rms_hat(x_ref[...])
    return ((xhat * g_ref[...]) * sc_ref[...] + sh_ref[...]).astype(BF)


def _forget_logits(x, g_pre, scale1p, shift, w_f, tm):
    s = x.shape[0]

    def body(x_ref, g_ref, sc_ref, sh_ref, w_ref, f_ref):
        f_ref[...] = _dot(_modulated(x_ref, g_ref, sc_ref, sh_ref), w_ref[...])

    vec = _full((1, D_MODEL))
    return _call(
        body, name="forget_logits", out_shape=jax.ShapeDtypeStruct((s, LANES), F32), grid=(s // tm,),
        in_specs=[pl.BlockSpec((tm, D_MODEL), lambda i: (i, 0)), vec, vec, vec, _full((D_MODEL, LANES))],
        out_specs=pl.BlockSpec((tm, LANES), lambda i: (i, 0)), compiler_params=_params(("parallel",)),
    )(x, g_pre, scale1p, shift, w_f)


def _split3(v):
    hi = v.astype(BF).astype(F32)
    mid = (v - hi).astype(BF).astype(F32)
    lo = ((v - hi) - mid).astype(BF).astype(F32)
    return hi, mid, lo


def _in_proj(x, g_pre, scale1p, shift, w_rows, w_t_fox, cum, cos_t, sin_t, w_out_halves, tm):
    s = x.shape[0]
    r_va, r_za, r_qb, r_zb, r_kb, r_vb = 0, 512, 1024, 1536, 2048, 2176

    def body(x_ref, g_ref, sc_ref, sh_ref, w_ref, wt_ref, cum_ref, cos_ref, sin_ref, wo_ref,
             h_ref, qat_ref, ka_ref, kat_ref, v_ref, vt_ref, za_ref, zb_ref, qb_ref, kb_ref, vb_ref,
             qbt_ref, kbt_ref, vbt_ref, mo_ref, wo_all_ref, send_sems, recv_sems, local_sem):
        start_gather, finish_gather = _chip_gather(wo_ref, wo_all_ref, send_sems, recv_sems, local_sem)

        @pl.when(pl.program_id(0) == 0)
        def _():
            start_gather()

        @pl.when(pl.program_id(0) == s // tm - 1)
        def _():
            finish_gather()

        hb = _modulated(x_ref, g_ref, sc_ref, sh_ref)
        h_ref[...] = hb

        def sec(c0, width):
            return _dot(hb, w_ref[:, c0:c0 + width])

        def sec_t(r0):
            return _dot_nt(wt_ref[r0:r0 + FOX_W, :], hb)

        q_t = sec_t(0) * Q_SCALE
        k_t = sec_t(FOX_W)
        v_t = sec_t(2 * FOX_W)
        va = sec(r_va, FOX_W)
        zeros = jnp.zeros((AUG_DIM - HEAD_DIM - AUG_ROWS, tm), F32)
        ri = lax.broadcasted_iota(jnp.int32, (AUG_ROWS, tm), 0)
        const = jnp.where(ri == AUG_ROWS - 1, 0.0, 1.0)
        ri_v = lax.broadcasted_iota(jnp.int32, (VT_ROWS - HEAD_DIM, tm), 0)
        v_feat = jnp.where(ri_v == 0, 1.0, 0.0).astype(BF)
        for hd in range(FOX_HEADS):
            rows = slice(hd * HEAD_DIM, (hd + 1) * HEAD_DIM)
            cum2 = cum_ref[hd:hd + 1, :] * LOG2E
            hi, mid, lo = (jnp.broadcast_to(part, (AUG_ROWS, tm)) for part in _split3(cum2))
            q_feat = jnp.where(ri == 1, hi, jnp.where(ri == 2, mid, jnp.where(ri == 3, lo, const)))
            k_feat = jnp.where(ri == 4, -hi, jnp.where(ri == 5, -mid, jnp.where(ri == 6, -lo, const)))
            q_aug = jnp.concatenate([q_t[rows], q_feat, zeros], axis=0)
            k_aug = jnp.concatenate([k_t[rows], k_feat, zeros], axis=0)
            mo_ref[hd:hd + 1, :] = jnp.sum(q_t[rows] * k_t[rows], axis=0, keepdims=True) + 1.0
            qat_ref[hd] = q_aug.astype(BF)
            kat_ref[hd] = k_aug.astype(BF)
            ka_ref[hd] = k_aug.T.astype(BF)
            vt_ref[hd] = jnp.concatenate([v_t[rows].astype(BF), v_feat], axis=0)
            v_ref[hd] = va[:, rows].astype(BF)
        za_ref[...] = sec(r_za, FOX_W)
        zb_ref[...] = sec(r_zb, SWA_W)
        cos2, sin2 = cos_ref[...], sin_ref[...]
        cos8 = jnp.concatenate([cos2] * 4, axis=1)
        sin8 = jnp.concatenate([sin2] * 4, axis=1)
        qb = sec(r_qb, SWA_W)
        qb = (qb * cos8 + _rope_partner(qb) * sin8) * (HEAD_DIM ** -0.5)
        qb_ref[...] = qb.astype(BF)
        for a in range(SWA_W // LANES):
            qbt_ref[a * LANES:(a + 1) * LANES, :] = qb[:, a * LANES:(a + 1) * LANES].T.astype(BF)
        kb = sec(r_kb, SWA_KV_W)
        kb = kb * cos2 + _rope_partner(kb) * sin2
        vb = sec(r_vb, SWA_KV_W)
        kb_t, vb_t = kb.T, vb.T
        for hd in range(SWA_KV_HEADS):
            sl = slice(hd * HEAD_DIM, (hd + 1) * HEAD_DIM)
            kb_ref[hd] = kb[:, sl].astype(BF)
            vb_ref[hd] = vb[:, sl].astype(BF)
            kbt_ref[hd] = kb_t[sl].astype(BF)
            vbt_ref[hd] = jnp.concatenate([vb_t[sl].astype(BF), v_feat], axis=0)

    row = lambda w: pl.BlockSpec((tm, w), lambda i: (i, 0))
    heads = lambda n, w=HEAD_DIM: pl.BlockSpec((n, tm, w), lambda i: (0, i, 0))
    heads_t = lambda w: pl.BlockSpec((FOX_HEADS, w, tm), lambda i: (0, 0, i))
    vec = _full((1, D_MODEL))
    hs = lambda a, b: jax.ShapeDtypeStruct((FOX_HEADS, a, b), BF)
    out_shape = (
        jax.ShapeDtypeStruct((s, D_MODEL), BF),
        hs(AUG_DIM, s), hs(s, AUG_DIM), hs(AUG_DIM, s), hs(s, HEAD_DIM), hs(VT_ROWS, s),
        jax.ShapeDtypeStruct((s, FOX_W), F32), jax.ShapeDtypeStruct((s, SWA_W), F32),
        jax.ShapeDtypeStruct((s, SWA_W), BF),
        jax.ShapeDtypeStruct((SWA_KV_HEADS, s, HEAD_DIM), BF), jax.ShapeDtypeStruct((SWA_KV_HEADS, s, HEAD_DIM), BF),
        jax.ShapeDtypeStruct((SWA_W, s), BF),
        jax.ShapeDtypeStruct((SWA_KV_HEADS, HEAD_DIM, s), BF), jax.ShapeDtypeStruct((SWA_KV_HEADS, VT_ROWS, s), BF),
        jax.ShapeDtypeStruct((FOX_HEADS, s), F32),
        jax.ShapeDtypeStruct((N_CHIPS,) + w_out_halves.shape, w_out_halves.dtype),
    )
    kv_t = lambda w: pl.BlockSpec((SWA_KV_HEADS, w, tm), lambda i: (0, 0, i))
    hbm = pl.BlockSpec(memory_space=pl.ANY)
    return _call(
        body, name="in_proj", out_shape=out_shape, grid=(s // tm,),
        in_specs=[row(D_MODEL), vec, vec, vec, _full(w_rows.shape), _full(w_t_fox.shape),
                  pl.BlockSpec((FOX_HEADS, tm), lambda i: (0, i)), row(LANES), row(LANES), hbm],
        out_specs=(row(D_MODEL), heads_t(AUG_DIM), heads(FOX_HEADS, AUG_DIM), heads_t(AUG_DIM), heads(FOX_HEADS),
                   heads_t(VT_ROWS), row(FOX_W), row(SWA_W), row(SWA_W), heads(SWA_KV_HEADS), heads(SWA_KV_HEADS),
                   pl.BlockSpec((SWA_W, tm), lambda i: (0, i)), kv_t(HEAD_DIM), kv_t(VT_ROWS),
                   pl.BlockSpec((FOX_HEADS, tm), lambda i: (0, i)), hbm),
        scratch_shapes=list(CHIP_GATHER_SEMS),
        compiler_params=_params(("arbitrary",), has_side_effects=True),
    )(x, g_pre, scale1p, shift, w_rows, w_t_fox, cum, cos_t, sin_t, w_out_halves)


def _diag_chunks(d, bq, bk, chunk):
    out = []
    for c0 in range(0, bq, chunk):
        if d is None or d * bk + bk - 1 <= c0:
            out.append((c0, None, bk))
        elif d * bk <= c0 + chunk - 1:
            n_keys = min(bk, c0 + chunk - d * bk)
            kpos = d * bk + lax.broadcasted_iota(jnp.int32, (n_keys, chunk), 0)
            qpos = c0 + lax.broadcasted_iota(jnp.int32, (n_keys, chunk), 1)
            out.append((c0, kpos <= qpos, n_keys))
    return out


def _fox_fwd(qat, ka, vt, m_own, bq, bk, chunk, running_max):
    nh, _, s = qat.shape
    r = bq // bk

    pairs = [(i, j) for i in range(s // bq) for j in range(i * r + r)]

    def body(i_tab, j_tab, ka_ref, qat_ref, vt_ref, mo_ref, o_ref, lse_ref, bad_ref, *rest):
        pt_ref, m_scr, acc_scr = (None,) * running_max + rest
        i, j = i_tab[pl.program_id(1)], j_tab[pl.program_id(1)]

        @pl.when(j == 0)
        def _():
            m_scr[...] = jnp.full(m_scr.shape, NEG, F32) if running_max else mo_ref[0]
            acc_scr[...] = jnp.zeros(acc_scr.shape, F32)

        def careful(d):
            kv, vtv = ka_ref[0], vt_ref[0]

            def one_chunk(n, carry):
                c0 = pl.multiple_of(n * chunk, chunk)
                cs = pl.ds(c0, chunk)
                sc = _dot(kv, qat_ref[0, :, cs])
                if d is not None:
                    kpos = d * bk + lax.broadcasted_iota(jnp.int32, (bk, chunk), 0)
                    qpos = c0 + lax.broadcasted_iota(jnp.int32, (bk, chunk), 1)
                    sc = jnp.where(kpos <= qpos, sc, NEG)
                m_prev = m_scr[:, cs]
                m_new = jnp.maximum(m_prev, jnp.max(sc, axis=0, keepdims=True))
                p = jnp.exp2(sc - m_new).astype(BF)
                acc_scr[:, cs] = jnp.exp2(m_prev - m_new) * acc_scr[:, cs] + _dot(vtv, p)
                m_scr[:, cs] = m_new
                return carry

            lax.fori_loop(0, bq // chunk, one_chunk, 0)

        def fast(d):
            todo = _diag_chunks(d, bq, bk, chunk)
            scores = lambda t: _dot(ka_ref[0, :t[2], :], qat_ref[0, :, t[0]:t[0] + chunk])
            sc_next = scores(todo[0])
            for n, (c0, mask, n_keys) in enumerate(todo):
                cs = slice(c0, c0 + chunk)
                sc = sc_next
                if n + 1 < len(todo):
                    sc_next = scores(todo[n + 1])
                if mask is not None:
                    sc = jnp.where(mask, sc, NEG)
                p = jnp.exp2(sc - m_scr[:, cs]).astype(BF)
                pt_ref[0, :n_keys, cs] = p
                acc_scr[:, cs] += _dot(vt_ref[0, :, :n_keys], p)

        step = careful if running_max else fast

        @pl.when(j < i * r)
        def _():
            step(None)

        for d in range(r):
            @pl.when(j == i * r + d)
            def _(d=d):
                step(d)

        @pl.when(j == i * r + r - 1)
        def _():
            l = acc_scr[HEAD_DIM:HEAD_DIM + 1, :]
            o_ref[0] = acc_scr[:HEAD_DIM, :] / l
            lse_ref[0] = m_scr[...] + jnp.log2(l)
            bad_ref[0] = jnp.where(l < OVERFLOW_GUARD, 0.0, 1.0)

    qmap_t = lambda h, t, it, jt: (h, 0, it[t])
    qrow = pl.BlockSpec((1, 1, bq), qmap_t)
    row_shape = jax.ShapeDtypeStruct((nh, 1, s), F32)
    out_shape = (jax.ShapeDtypeStruct((nh, HEAD_DIM, s), F32), row_shape, row_shape)
    out_specs = (pl.BlockSpec((1, HEAD_DIM, bq), qmap_t), qrow, qrow)
    if not running_max:
        out_shape += (jax.ShapeDtypeStruct((nh, s, s), BF),)
        out_specs += (pl.BlockSpec((1, bk, bq), lambda h, t, it, jt: (h, jt[t], it[t])),)
    grid_spec = pltpu.PrefetchScalarGridSpec(
        num_scalar_prefetch=2, grid=(nh, len(pairs)),
        in_specs=[pl.BlockSpec((1, bk, AUG_DIM), lambda h, t, it, jt: (h, jt[t], 0)), pl.BlockSpec((1, AUG_DIM, bq), qmap_t),
                  pl.BlockSpec((1, VT_ROWS, bk), lambda h, t, it, jt: (h, 0, jt[t])), qrow],
        out_specs=out_specs,
        scratch_shapes=[pltpu.VMEM((1, bq), F32), pltpu.VMEM((VT_ROWS, bq), F32)])
    return _call(
        body, name="fox_fwd_running_max" if running_max else "fox_fwd", out_shape=out_shape, grid_spec=grid_spec,
        compiler_params=_params(("parallel", "arbitrary")),
    )(jnp.asarray([p[0] for p in pairs], jnp.int32), jnp.asarray([p[1] for p in pairs], jnp.int32), ka, qat, vt, m_own)


def _fox_bwd(qat, ka, kat, v, dot_, lse, delta, pt, bq, bk, chunk, dq_blk):
    nh, _, s = qat.shape
    r = bq // bk
    nq = s // bq
    stored = pt is not None

    pairs = [(j, i) for j in range(s // bk) for i in range(j // r, nq)]

    def body(j_tab, i_tab, a_ref, b_ref, kat_ref, v_ref, qat_ref, do_ref, dl_ref, dq_ref, dk_ref, dv_ref, dk_scr, dv_scr):
        ka_ref, lse_ref, pt_ref = (None, None, a_ref) if stored else (a_ref, b_ref, None)
        j, i = j_tab[pl.program_id(1)], i_tab[pl.program_id(1)]

        @pl.when(pl.program_id(1) == 0)
        def _():
            dq_ref[...] = jnp.zeros(dq_ref.shape, F32)

        @pl.when(i * r <= j)
        def _():
            dk_scr[...] = jnp.zeros(dk_scr.shape, F32)
            dv_scr[...] = jnp.zeros(dv_scr.shape, F32)

        def step(d):
            todo = _diag_chunks(d, bq, bk, chunk)

            def products(t):
                cs = slice(t[0], t[0] + chunk)
                return (None if stored else _dot(ka_ref[0, :t[2], :], qat_ref[0, :, cs]),
                        _dot(v_ref[0, :t[2], :], do_ref[0, :, cs]))

            nxt = products(todo[0])
            for n, (c0, mask, n_keys) in enumerate(todo):
                cs = slice(c0, c0 + chunk)
                sc, dp = nxt
                if n + 1 < len(todo):
                    nxt = products(todo[n + 1])
                if stored:
                    p_bf = pt_ref[0, :n_keys, cs]
                    p = p_bf.astype(F32)
                else:
                    p = jnp.exp2(sc - lse_ref[0, :, cs])
                    if mask is not None:
                        p = jnp.where(mask, p, 0.0)
                    p_bf = p.astype(BF)
                ds = (p * (dp - dl_ref[0, :, cs])).astype(BF)
                dv_scr[:, :n_keys] += _dot_nt(do_ref[0, :, cs], p_bf)
                dk_scr[:, :n_keys] += _dot_nt(qat_ref[0, :VT_ROWS, cs], ds)
                c1 = c0 % dq_blk
                dq_ref[0, i * (bq // dq_blk) + c0 // dq_blk, :, c1:c1 + chunk] += _dot(kat_ref[0, :VT_ROWS, :n_keys], ds)

        @pl.when(i * r > j)
        def _():
            step(None)

        for d in range(r):
            @pl.when(j == i * r + d)
            def _(d=d):
                step(d)

        @pl.when(i == nq - 1)
        def _():
            dk_ref[0] = dk_scr[...]
            dv_ref[0] = dv_scr[...]

    qmap = lambda h, t, jt, it: (h, 0, it[t])
    kmap = lambda h, t, jt, it: (h, jt[t], 0)
    kmap_t = lambda h, t, jt, it: (h, 0, jt[t])
    if stored:
        first = [(pt, pl.BlockSpec((1, bk, bq), lambda h, t, jt, it: (h, jt[t], it[t]))),
                 (delta, pl.BlockSpec((1, 1, bq), qmap))]
    else:
        first = [(ka, pl.BlockSpec((1, bk, AUG_DIM), kmap)), (lse, pl.BlockSpec((1, 1, bq), qmap))]
    grid_spec = pltpu.PrefetchScalarGridSpec(
        num_scalar_prefetch=2, grid=(nh, len(pairs)),
        in_specs=[first[0][1], first[1][1], pl.BlockSpec((1, AUG_DIM, bk), kmap_t), pl.BlockSpec((1, bk, HEAD_DIM), kmap),
                  pl.BlockSpec((1, AUG_DIM, bq), qmap), pl.BlockSpec((1, HEAD_DIM, bq), qmap),
                  pl.BlockSpec((1, 1, bq), qmap)],
        out_specs=(pl.BlockSpec((1, s // dq_blk, VT_ROWS, dq_blk), lambda h, t, jt, it: (h, 0, 0, 0)),
                   pl.BlockSpec((1, VT_ROWS, bk), kmap_t), pl.BlockSpec((1, HEAD_DIM, bk), kmap_t)),
        scratch_shapes=[pltpu.VMEM((VT_ROWS, bk), F32), pltpu.VMEM((HEAD_DIM, bk), F32)])
    return _call(
        body, name="fox_bwd" if stored else "fox_bwd_recompute",
        out_shape=(jax.ShapeDtypeStruct((nh, s // dq_blk, VT_ROWS, dq_blk), F32),
                   jax.ShapeDtypeStruct((nh, VT_ROWS, s), F32), jax.ShapeDtypeStruct((nh, HEAD_DIM, s), F32)),
        grid_spec=grid_spec, compiler_params=_params(("parallel", "arbitrary")),
    )(jnp.asarray([p[0] for p in pairs], jnp.int32), jnp.asarray([p[1] for p in pairs], jnp.int32),
      first[0][0], first[1][0], kat, v, qat, dot_, delta)


def _swa_mask(i, tq):
    kpos = i * tq - WINDOW + lax.broadcasted_iota(jnp.int32, (tq + WINDOW, tq), 0)
    qpos = i * tq + lax.broadcasted_iota(jnp.int32, (tq + WINDOW, tq), 1)
    rel = qpos - kpos
    return (rel >= 0) & (rel < WINDOW) & (kpos >= 0)


def _swa_rows(ref, g, i, tq):
    before = pl.multiple_of(jnp.maximum(i * tq - WINDOW, 0), WINDOW)
    return jnp.concatenate([ref[g, pl.ds(before, WINDOW), :], ref[g, pl.ds(pl.multiple_of(i * tq, tq), tq), :]], axis=0)


def _swa_before(n_rows, tq):
    return pl.BlockSpec((SWA_KV_HEADS, n_rows, WINDOW), lambda i: (0, 0, jnp.maximum(i * (tq // WINDOW) - 1, 0)))


def _swa_probs_t(sc, mask, sink):
    sc = jnp.where(mask, sc, NEG)
    m = jnp.maximum(jnp.max(sc, axis=0, keepdims=True), sink)
    p = jnp.exp(sc - m)
    e_sink = jnp.exp(sink - m)
    inv_l = 1.0 / (jnp.sum(p, axis=0, keepdims=True) + e_sink)
    return p * inv_l, e_sink * inv_l


def _swa_fwd(qbt, kb, vbt, sinks, tq):
    s = qbt.shape[1]
    n_heads = SWA_KV_HEADS * SWA_GROUP

    def body(q_ref, k_ref, vb_ref, vc_ref, s_ref, o_ref):
        i = pl.program_id(0)
        mask = _swa_mask(i, tq)
        kw = [_swa_rows(k_ref, g, i, tq) for g in range(SWA_KV_HEADS)]
        vtw = [jnp.concatenate([vb_ref[g], vc_ref[g]], axis=1) for g in range(SWA_KV_HEADS)]
        scores = lambda hd: _dot(kw[hd // SWA_GROUP], q_ref[hd * HEAD_DIM:(hd + 1) * HEAD_DIM, :])
        sc_next = scores(0)
        for hd in range(n_heads):
            g, hh = divmod(hd, SWA_GROUP)
            rows = slice(hd * HEAD_DIM, (hd + 1) * HEAD_DIM)
            sink = s_ref[g][:, hh:hh + 1]
            sc = jnp.where(mask, sc_next, NEG)
            if hd + 1 < n_heads:
                sc_next = scores(hd + 1)
            m = jnp.maximum(jnp.max(sc, axis=0, keepdims=True), sink)
            acc = _dot(vtw[g], jnp.exp(sc - m).astype(BF))
            o_ref[rows, :] = acc[:HEAD_DIM] / (acc[HEAD_DIM:HEAD_DIM + 1] + jnp.exp(sink - m))

    kvspec = _full((SWA_KV_HEADS, s, HEAD_DIM))
    qspec = pl.BlockSpec((SWA_W, tq), lambda i: (0, i))
    return _call(
        body, name="swa_fwd", out_shape=jax.ShapeDtypeStruct((SWA_W, s), F32), grid=(s // tq,),
        in_specs=[qspec, kvspec, _swa_before(VT_ROWS, tq), pl.BlockSpec((SWA_KV_HEADS, VT_ROWS, tq), lambda i: (0, 0, i)),
                  _full((SWA_KV_HEADS, 1, SWA_GROUP))],
        out_specs=qspec, compiler_params=_params(("parallel",)),
    )(qbt, kb, vbt, vbt, sinks)


def _swa_bwd(qb, qbt, kb, kbt, vb, sinks, dob, dobt, tq):
    s = qb.shape[0]
    n_heads = SWA_KV_HEADS * SWA_GROUP

    def body(q_ref, qt_ref, k_ref, ktb_ref, ktc_ref, v_ref, s_ref, do_ref, dot_ref, dq_ref, dk_ref, dv_ref, ds_ref):
        i = pl.program_id(0)

        @pl.when(i == 0)
        def _():
            dk_ref[...] = jnp.zeros(dk_ref.shape, F32)
            dv_ref[...] = jnp.zeros(dv_ref.shape, F32)
            ds_ref[...] = jnp.zeros(ds_ref.shape, F32)

        mask = _swa_mask(i, tq)
        kw = [_swa_rows(k_ref, g, i, tq) for g in range(SWA_KV_HEADS)]
        vw = [_swa_rows(v_ref, g, i, tq) for g in range(SWA_KV_HEADS)]
        ktw = [jnp.concatenate([ktb_ref[g], ktc_ref[g]], axis=1) for g in range(SWA_KV_HEADS)]
        before = pl.ds(pl.multiple_of(jnp.maximum(i * tq - WINDOW, 0), WINDOW), WINDOW)
        own = pl.ds(pl.multiple_of(i * tq, tq), tq)

        def products(hd):
            rows = slice(hd * HEAD_DIM, (hd + 1) * HEAD_DIM)
            return _dot(kw[hd // SWA_GROUP], qt_ref[rows, :]), _dot(vw[hd // SWA_GROUP], dot_ref[rows, :])

        nxt = products(0)
        for g in range(SWA_KV_HEADS):
            dsinks = []
            dk_acc = jnp.zeros((tq + WINDOW, HEAD_DIM), F32)
            dv_acc = jnp.zeros((tq + WINDOW, HEAD_DIM), F32)
            for hh in range(SWA_GROUP):
                hd = g * SWA_GROUP + hh
                rows = slice(hd * HEAD_DIM, (hd + 1) * HEAD_DIM)
                sc, dp = nxt
                if hd + 1 < n_heads:
                    nxt = products(hd + 1)
                p, p_sink = _swa_probs_t(sc, mask, s_ref[g][:, hh:hh + 1])
                delta = jnp.sum(p * dp, axis=0, keepdims=True)
                dsc = (p * (dp - delta)).astype(BF)
                dq_ref[rows, :] = _dot(ktw[g], dsc)
                dk_acc = dk_acc + _dot(dsc, q_ref[:, rows])
                dv_acc = dv_acc + _dot(p.astype(BF), do_ref[:, rows])
                dsinks.append(-jnp.sum(p_sink * delta, axis=1, keepdims=True))
            dk_ref[g, before, :] += dk_acc[:WINDOW]
            dk_ref[g, own, :] += dk_acc[WINDOW:]
            dv_ref[g, before, :] += dv_acc[:WINDOW]
            dv_ref[g, own, :] += dv_acc[WINDOW:]
            ds_ref[g] += jnp.concatenate(dsinks, axis=1)

    kvspec = _full((SWA_KV_HEADS, s, HEAD_DIM))
    qspec = pl.BlockSpec((tq, SWA_W), lambda i: (i, 0))
    qspec_t = pl.BlockSpec((SWA_W, tq), lambda i: (0, i))
    sspec = _full((SWA_KV_HEADS, 1, SWA_GROUP))
    kvshape = jax.ShapeDtypeStruct((SWA_KV_HEADS, s, HEAD_DIM), F32)
    return _call(
        body, name="swa_bwd",
        out_shape=(jax.ShapeDtypeStruct((SWA_W, s), F32), kvshape, kvshape,
                   jax.ShapeDtypeStruct((SWA_KV_HEADS, 1, SWA_GROUP), F32)),
        grid=(s // tq,),
        in_specs=[qspec, qspec_t, kvspec, _swa_before(HEAD_DIM, tq),
                  pl.BlockSpec((SWA_KV_HEADS, HEAD_DIM, tq), lambda i: (0, 0, i)), kvspec, sspec, qspec, qspec_t],
        out_specs=(qspec_t, kvspec, kvspec, sspec),
        compiler_params=_params(("arbitrary",)),
    )(qb, qbt, kb, kbt, kbt, vb, sinks, dob, dobt)


def _pairs_to_rows(ref, n_rows=HEAD_DIM):
    parts = []
    for a in range(0, FOX_HEADS, 2):
        parts.append(jnp.concatenate([ref[a][:n_rows], ref[a + 1][:n_rows]], axis=0).T)
    return jnp.concatenate(parts, axis=1)


def _blocks_to_rows(ref):
    return jnp.concatenate([ref[a:a + LANES, :].T for a in range(0, ref.shape[0], LANES)], axis=1)


def _out_proj(oat, za, obt, zb, x, tgt, w_out, w_out_t, gate, g_post, inv_l, tm):
    s = x.shape[0]

    def body(oat_ref, za_ref, obt_ref, zb_ref, x_ref, t_ref, w_ref, wt_ref, gate_ref, gp_ref, il_ref,
             dout_ref, doat_ref, dla_ref, dza_ref, dob_ref, dobt_ref, dzb_ref, gw_ref, dgate_ref, dgp_ref, loss_ref):
        i = pl.program_id(0)

        @pl.when(i == 0)
        def _():
            gw_ref[...] = jnp.zeros(gw_ref.shape, F32)
            dgate_ref[...] = jnp.zeros(dgate_ref.shape, F32)
            dgp_ref[...] = jnp.zeros(dgp_ref.shape, F32)
            loss_ref[...] = jnp.zeros(loss_ref.shape, F32)

        oa_v = _pairs_to_rows(oat_ref)
        ob_v = _blocks_to_rows(obt_ref)
        za_v, zb_v = za_ref[...], zb_ref[...]
        sga, sgb = _sigmoid(za_v), _sigmoid(zb_v)
        sila, silb = za_v * sga, zb_v * sgb
        u = jnp.concatenate([oa_v * sila, ob_v * silb], axis=1).astype(BF)
        yv = _dot(u, w_ref[...])
        yhat, rstd = _rms_hat(yv)
        gp, gate_v = gp_ref[...], gate_ref[...]
        nrm = yhat * gp
        diff = (x_ref[...] + gate_v * nrm) - t_ref[...]
        loss_ref[...] += 0.5 * jnp.sum(jnp.sum(diff * diff, axis=1, keepdims=True), axis=0, keepdims=True) / D_MODEL
        dout = diff * (1.0 / D_MODEL)
        dout_ref[...] = dout
        dgate_ref[...] += jnp.sum(dout * nrm, axis=0, keepdims=True)
        dn = dout * gate_v
        dgp_ref[...] += jnp.sum(dn * yhat, axis=0, keepdims=True)
        dyhat = dn * gp
        dy = (rstd * (dyhat - yhat * jnp.mean(dyhat * yhat, axis=1, keepdims=True))).astype(BF)
        gw_ref[...] += _dot_tn(u, dy)
        du = _dot(dy, wt_ref[...])
        dua, dub = du[:, :FOX_W], du[:, FOX_W:]
        doa = dua * sila
        for a in range(0, FOX_HEADS, 2):
            pair_t = doa[:, a * HEAD_DIM:(a + 2) * HEAD_DIM].T
            for hd, rows in ((a, slice(0, HEAD_DIM)), (a + 1, slice(HEAD_DIM, 2 * HEAD_DIM))):
                inv_l = il_ref[hd]
                doat_ref[hd] = (pair_t[rows] * inv_l).astype(BF)
                dla_ref[hd] = jnp.sum(pair_t[rows] * oat_ref[hd], axis=0, keepdims=True) * inv_l
        dob = dub * silb
        dob_ref[...] = dob.astype(BF)
        for a in range(0, SWA_W, LANES):
            dobt_ref[a:a + LANES, :] = dob[:, a:a + LANES].T.astype(BF)
        dza_ref[...] = (dua * oa_v * (sga * (1.0 + za_v * (1.0 - sga)))).astype(BF)
        dzb_ref[...] = (dub * ob_v * (sgb * (1.0 + zb_v * (1.0 - sgb)))).astype(BF)

    row = lambda w: pl.BlockSpec((tm, w), lambda i: (i, 0))
    heads_t = lambda w: pl.BlockSpec((FOX_HEADS, w, tm), lambda i: (0, 0, i))
    vec = _full((1, D_MODEL))
    mat = _full((D_MODEL, D_MODEL))
    out_shape = (
        jax.ShapeDtypeStruct((s, D_MODEL), F32),
        jax.ShapeDtypeStruct((FOX_HEADS, HEAD_DIM, s), BF), jax.ShapeDtypeStruct((FOX_HEADS, 1, s), F32),
        jax.ShapeDtypeStruct((s, FOX_W), BF), jax.ShapeDtypeStruct((s, SWA_W), BF), jax.ShapeDtypeStruct((SWA_W, s), BF),
        jax.ShapeDtypeStruct((s, SWA_W), BF),
        jax.ShapeDtypeStruct((D_MODEL, D_MODEL), F32),
        jax.ShapeDtypeStruct((1, D_MODEL), F32), jax.ShapeDtypeStruct((1, D_MODEL), F32),
        jax.ShapeDtypeStruct((1, 1), F32),
    )
    col = pl.BlockSpec((SWA_W, tm), lambda i: (0, i))
    return _call(
        body, name="out_proj", out_shape=out_shape, grid=(s // tm,),
        in_specs=[heads_t(HEAD_DIM), row(FOX_W), col, row(SWA_W), row(D_MODEL), row(D_MODEL), mat, mat, vec, vec,
                  heads_t(1)],
        out_specs=(row(D_MODEL), heads_t(HEAD_DIM), heads_t(1), row(FOX_W), row(SWA_W), col, row(SWA_W), mat, vec, vec,
                   _full((1, 1))),
        compiler_params=_params(("arbitrary",)),
    )(oat, za, obt, zb, x, tgt, w_out, w_out_t, gate, g_post, inv_l)


def _assemble_dproj(dqt, dkt, dvt, dza, dqb, dzb, dkb, dvb, df, cos_t, sin_t, tm):
    s = dza.shape[0]

    def body(dqt_ref, dkt_ref, dvt_ref, dza_ref, dqb_ref, dzb_ref, dkb_ref, dvb_ref, df_ref, cos_ref, sin_ref, o_ref):
        def cat(ref, n):
            return jnp.concatenate([ref[hd] for hd in range(n)], axis=1)

        cos2, sin2 = cos_ref[...], sin_ref[...]
        cos8 = jnp.concatenate([cos2] * 4, axis=1)
        sin8 = jnp.concatenate([sin2] * 4, axis=1)
        scale = HEAD_DIM ** -0.5
        o_ref[:, C_QA:C_QA + FOX_W] = (_pairs_to_rows(dqt_ref.at[:, 0]) * scale).astype(BF)
        o_ref[:, C_KA:C_KA + FOX_W] = (_pairs_to_rows(dkt_ref) * LN2).astype(BF)
        o_ref[:, C_VA:C_VA + FOX_W] = _pairs_to_rows(dvt_ref).astype(BF)
        o_ref[:, C_ZA:C_ZA + FOX_W] = dza_ref[...]
        dq = _blocks_to_rows(dqb_ref) * scale
        o_ref[:, C_QB:C_QB + SWA_W] = (dq * cos8 - _rope_partner(dq) * sin8).astype(BF)
        o_ref[:, C_ZB:C_ZB + SWA_W] = dzb_ref[...]
        dk = cat(dkb_ref, SWA_KV_HEADS)
        o_ref[:, C_KB:C_KB + SWA_KV_W] = (dk * cos2 - _rope_partner(dk) * sin2).astype(BF)
        o_ref[:, C_VB:C_VB + SWA_KV_W] = cat(dvb_ref, SWA_KV_HEADS).astype(BF)
        o_ref[:, C_F:C_F + LANES] = df_ref[...].astype(BF)

    row = lambda w: pl.BlockSpec((tm, w), lambda i: (i, 0))
    heads = lambda n: pl.BlockSpec((n, tm, HEAD_DIM), lambda i: (0, i, 0))
    heads_t = lambda w: pl.BlockSpec((FOX_HEADS, w, tm), lambda i: (0, 0, i))
    return _call(
        body, name="assemble_dproj", out_shape=jax.ShapeDtypeStruct((s, WP), BF), grid=(s // tm,),
        in_specs=[pl.BlockSpec((FOX_HEADS, 1, VT_ROWS, tm), lambda i: (0, i, 0, 0)), heads_t(VT_ROWS), heads_t(HEAD_DIM),
                  row(FOX_W), pl.BlockSpec((SWA_W, tm), lambda i: (0, i)), row(SWA_W), heads(SWA_KV_HEADS),
                  heads(SWA_KV_HEADS), row(LANES), row(LANES), row(LANES)],
        out_specs=row(WP), compiler_params=_params(("parallel",)),
    )(dqt, dkt, dvt, dza, dqb, dzb, dkb, dvb, df, cos_t, sin_t)


def _in_proj_bwd_x(dproj, w_al_t, x, dout, g_pre, scale1p, tm, parts):
    s = x.shape[0]
    n_steps = s // tm
    masks = [(1, 0), (0, 1), (1, 1)]

    def body(dp_ref, wt_ref, x_ref, dout_ref, g_ref, sc_ref, parts_ref, gx_ref, dsh_ref, dsc_ref, dg_ref, got_ref,
             send_sems, recv_sems, local_sem):
        i = pl.program_id(0)
        cx, cy, cc = lax.axis_index("x"), lax.axis_index("y"), lax.axis_index("c")
        me = 2 * cx + cy
        own = pltpu.make_async_copy(parts_ref.at[me], got_ref.at[me], local_sem)

        def copy(k, send):
            dx, dy = masks[k]
            peer = 2 * (cx ^ dx) + (cy ^ dy)
            return pltpu.make_async_remote_copy(
                src_ref=parts_ref.at[peer if send else me], dst_ref=got_ref.at[me if send else peer],
                send_sem=send_sems.at[k], recv_sem=recv_sems.at[k], device_id=(cx ^ dx, cy ^ dy, cc), device_id_type=MESH)

        @pl.when(i == 0)
        def _():
            dsh_ref[...] = jnp.zeros(dsh_ref.shape, F32)
            dsc_ref[...] = jnp.zeros(dsc_ref.shape, F32)
            dg_ref[...] = jnp.zeros(dg_ref.shape, F32)
            own.start()
            for k in range(len(masks)):
                copy(k, True).start()

        @pl.when(i == n_steps - 1)
        def _():
            for k in range(len(masks)):
                copy(k, False).wait_recv()
            for k in range(len(masks)):
                copy(k, True).wait_send()
            own.wait()

        dh = _dot(dp_ref[...], wt_ref[...])
        xhat, rstd = _rms_hat(x_ref[...])
        g, sc = g_ref[...], sc_ref[...]
        dsh_ref[...] += jnp.sum(dh, axis=0, keepdims=True)
        dhx = dh * xhat
        dsc_ref[...] += jnp.sum(dhx * g, axis=0, keepdims=True)
        dg_ref[...] += jnp.sum(dhx * sc, axis=0, keepdims=True)
        dxhat = dh * (g * sc)
        gx_ref[...] = dout_ref[...] + rstd * (dxhat - xhat * jnp.mean(dxhat * xhat, axis=1, keepdims=True))

    row = lambda w: pl.BlockSpec((tm, w), lambda i: (i, 0))
    vec = _full((1, D_MODEL))
    vshape = jax.ShapeDtypeStruct((1, D_MODEL), F32)
    hbm = pl.BlockSpec(memory_space=pl.ANY)
    return _call(
        body, name="in_proj_bwd_x",
        out_shape=(jax.ShapeDtypeStruct((s, D_MODEL), F32), vshape, vshape, vshape,
                   jax.ShapeDtypeStruct(parts.shape, parts.dtype)),
        grid=(n_steps,),
        in_specs=[row(WP), _full((WP, D_MODEL)), row(D_MODEL), row(D_MODEL), vec, vec, hbm],
        out_specs=(row(D_MODEL), vec, vec, vec, hbm),
        scratch_shapes=[pltpu.SemaphoreType.DMA((3,)), pltpu.SemaphoreType.DMA((3,)), pltpu.SemaphoreType.DMA],
        compiler_params=_params(("arbitrary",), has_side_effects=True),
    )(dproj, w_al_t, x, dout, g_pre, scale1p, parts)


def _in_proj_bwd_w(h, dproj, tk, tn):
    s = h.shape[0]
    n_k = s // tk

    def body(h_ref, dp_ref, gw_ref, acc_scr):
        k = pl.program_id(1)

        @pl.when(k == 0)
        def _():
            acc_scr[...] = jnp.zeros(acc_scr.shape, F32)

        acc_scr[...] += _dot_tn(h_ref[...], dp_ref[...])

        @pl.when(k == n_k - 1)
        def _():
            gw_ref[...] = acc_scr[...].astype(BF)

    return _call(
        body, name="in_proj_bwd_w", out_shape=jax.ShapeDtypeStruct((D_MODEL, WP), BF), grid=(WP // tn, n_k),
        in_specs=[pl.BlockSpec((tk, D_MODEL), lambda n, k: (k, 0)), pl.BlockSpec((tk, tn), lambda n, k: (k, n))],
        out_specs=pl.BlockSpec((D_MODEL, tn), lambda n, k: (0, n)),
        scratch_shapes=[pltpu.VMEM((D_MODEL, tn), F32)],
        compiler_params=_params(("parallel", "arbitrary")),
    )(h, dproj)


def _align_w_in(w_cols):
    def part(name, width):
        return w_cols[:, _SRC[name]:_SRC[name] + width]

    fpad = jnp.pad(part("fa", FOX_HEADS), ((0, 0), (0, LANES - FOX_HEADS)))
    return jnp.concatenate([part("qa", FOX_W), part("ka", FOX_W), part("va", FOX_W), part("za", FOX_W),
                            part("qb", SWA_W), part("zb", SWA_W), part("kb", SWA_KV_W), part("vb", SWA_KV_W), fpad], axis=1)


def _unalign_w_in(g_al):
    def part(c0, width):
        return g_al[:, c0:c0 + width]

    return jnp.concatenate([part(C_QA, FOX_W), part(C_KA, FOX_W), part(C_VA, FOX_W), part(C_F, FOX_HEADS),
                            part(C_ZA, FOX_W), part(C_QB, SWA_W), part(C_KB, SWA_KV_W), part(C_VB, SWA_KV_W),
                            part(C_ZB, SWA_W)], axis=1)


def _rope_tables(positions):
    inv_freq = ROPE_THETA ** (-jnp.arange(HALF, dtype=F32) / HALF)
    ang = positions.astype(F32)[:, None] * inv_freq
    cos, sin = jnp.cos(ang), jnp.sin(ang)
    return jnp.concatenate([cos, cos, cos, cos], axis=1), jnp.concatenate([-sin, sin, -sin, sin], axis=1)


def _tiles(s):
    if s >= 4096:
        return dict(tm=512, blk=512, bq=2048, bk=2048, bk_bwd=2048, chunk=256, tq=256, tm_out=512, tk=1024, tn=1152)
    return dict(tm=128, blk=128, bq=256, bk=256, bk_bwd=256, chunk=128, tq=128, tm_out=128, tk=128, tn=1152)


def kernel(x, c, positions, w_ada, b_ada, g_pre, w_in, b_fgate, sinks, w_out, g_post, loss_target, m_w_ada, m_b_ada, m_g_pre, m_w_in, m_b_fgate, m_sinks, m_w_out, m_g_post, v_w_ada, v_b_ada, v_g_pre, v_w_in, v_b_fgate, v_sinks, v_w_out, v_g_post):
    s = x.shape[1]
    t = _tiles(s)
    nc = s // LANES
    rows = FOX_HEADS * nc
    me = 4 * lax.axis_index("x") + 2 * lax.axis_index("y") + lax.axis_index("c")
    chip = 2 * lax.axis_index("x") + lax.axis_index("y")
    x2, tgt = x[0], loss_target[0]

    a_all, mod_all = _ada_exchange(c, w_ada[0])
    mod_rows = lax.dynamic_index_in_dim(mod_all, me, axis=1, keepdims=False)
    mod = mod_rows.reshape(N_CHIPS, 2, W_ADA_SHARD)[:, 0, :].reshape(1, 3 * D_MODEL) + b_ada
    shift, scale1p, gate = mod[:, :D_MODEL], 1.0 + mod[:, D_MODEL:2 * D_MODEL], mod[:, 2 * D_MODEL:]

    w_in_pad = jnp.pad(w_in[0].astype(BF), ((0, 0), (0, W_IN_SHARD_PAD - W_IN_SHARD)))
    w_all = _allgather_chips(w_in_pad.reshape(2, D_MODEL // 2, -1), "gather_w_in").reshape(N_CHIPS, D_MODEL, -1)
    w_cols = jnp.concatenate([w_all[k, :, :W_IN_SHARD] for k in range(N_CHIPS)], axis=1)
    w_al = _align_w_in(w_cols)
    w_al_t = w_al.T

    cos_t, sin_t = _rope_tables(positions[0])

    f_pad = _forget_logits(x2, g_pre, scale1p, shift, w_al[:, C_F:], t["tk"])
    f_rows = f_pad[:, :FOX_HEADS].T.reshape(rows, LANES)
    bias_rows = jnp.repeat(b_fgate[0], nc)[:, None]
    cum = _log_forget_cumsum(f_rows, bias_rows, nc).reshape(FOX_HEADS, s)
    h, qat, ka, kat, va, vat, za, zb, qb, kb, vb, qbt, kbt, vbt, m_own, w_out_all = _in_proj(
        x2, g_pre, scale1p, shift, w_al[:, C_VA:C_F], w_al_t[:C_ZA], cum, cos_t, sin_t,
        w_out[0].astype(BF).reshape(2, W_OUT_SHARD // 2, D_MODEL), t["tm"])
    w_out_all = w_out_all.reshape(D_MODEL, D_MODEL)
    w_out_t = w_out_all.T
    m_own = m_own[:, None, :]
    fox_args = (qat, ka, vat, m_own, t["bq"], t["bk"], t["chunk"])
    oat, lse, bad, pt = _fox_fwd(*fox_args, running_max=False)
    overflowed = jnp.max(bad) > 0.0
    oat, lse = lax.cond(overflowed, lambda: _fox_fwd(*fox_args, running_max=True)[:2], lambda: (oat, lse))
    inv_l = jnp.where(overflowed, 1.0, jnp.exp2(m_own - lse))
    sinks_g = sinks.reshape(SWA_KV_HEADS, 1, SWA_GROUP)
    obt = _swa_fwd(qbt, kb, vbt, sinks_g, t["tq"])

    dout, doat, delta_a, dza, dob, dobt, dzb, gw_out, dgate, dg_post, loss_part = _out_proj(
        oat, za, obt, zb, x2, tgt, w_out_all, w_out_t, gate, g_post, inv_l, t["tm_out"])

    bwd_args = (qat, ka, kat, va, doat, lse, delta_a)
    bwd_tiles = (t["bq"], t["bk_bwd"], t["chunk"], t["blk"])
    dqt, dkt, dvt = lax.cond(overflowed, lambda: _fox_bwd(*bwd_args, None, *bwd_tiles),
                             lambda: _fox_bwd(*bwd_args, pt, *bwd_tiles))
    dcum = dqt[:, :, HEAD_DIM, :].reshape(FOX_HEADS, s) - dkt[:, HEAD_DIM, :]
    df_rows, db_heads = _log_forget_cumsum_bwd(dcum.reshape(rows, LANES), f_rows, bias_rows, nc)
    df_pad = jnp.pad(df_rows.reshape(FOX_HEADS, s).T, ((0, 0), (0, LANES - FOX_HEADS)))
    dqb, dkb, dvb, dsinks = _swa_bwd(qb, qbt, kb, kbt, vb, sinks_g, dob, dobt, t["tq"])

    dproj = _assemble_dproj(dqt, dkt, dvt, dza, dqb, dzb, dkb, dvb, df_pad, cos_t, sin_t, t["blk"])
    gw_in = _unalign_w_in(_in_proj_bwd_w(h, dproj, 2 * t["tk"], t["tn"]))

    gin = jnp.stack([jnp.pad(gw_in[:, k * W_IN_SHARD:(k + 1) * W_IN_SHARD], ((0, 0), (0, W_IN_SHARD_PAD - W_IN_SHARD)))
                     for k in range(N_CHIPS)])
    gout = gw_out.astype(BF).reshape(N_CHIPS, D_MODEL, W_OUT_SHARD)
    pair = _pair_sum(gin, gout, "pair_sum")
    grad_x, dshift, dscale, dg_pre, from_chips = _in_proj_bwd_x(
        dproj, w_al_t, x2, dout, g_pre, scale1p, t["tm_out"], pair)

    pad_lane = lambda vrow: jnp.pad(vrow, ((0, 0), (0, LANES - vrow.shape[1])))
    packed = jnp.concatenate([dshift, dscale, dgate, dg_pre, dg_post,
                              pad_lane(db_heads.reshape(1, FOX_HEADS)), pad_lane(dsinks.reshape(1, FOX_HEADS)),
                              pad_lane(loss_part)], axis=1)
    parts, tot = _allgather_devices(packed, "gather_partials")
    loss = tot[0, P_LOSS]
    g_b_ada = tot[:, P_DMOD:P_DMOD + 3 * D_MODEL]
    g_g_pre = tot[:, P_GPRE:P_GPRE + D_MODEL]
    g_g_post = tot[:, P_GPOST:P_GPOST + D_MODEL]
    g_b_fgate = tot[:, P_BF:P_BF + FOX_HEADS]
    g_sinks = tot[:, P_SINK:P_SINK + FOX_HEADS]
    dm_shard = lax.dynamic_slice_in_dim(parts[:, 0, :3 * D_MODEL], chip * W_ADA_SHARD, W_ADA_SHARD, axis=1)
    g_w_ada = _grad_w_ada(a_all.T, dm_shard)

    gfull = _sum_chips_and_share(from_chips, "sum_chips_and_share")
    g_w_in = gfull[:, :W_IN_SHARD]
    g_w_out = gfull[:, W_IN_SHARD_PAD:].reshape(W_OUT_SHARD, D_MODEL)

    grads = dict(w_ada=g_w_ada, b_ada=g_b_ada, g_pre=g_g_pre, w_in=g_w_in, b_fgate=g_b_fgate, sinks=g_sinks,
                 w_out=g_w_out, g_post=g_g_post)
    weights = dict(w_ada=w_ada, b_ada=b_ada, g_pre=g_pre, w_in=w_in, b_fgate=b_fgate, sinks=sinks, w_out=w_out, g_post=g_post)
    moms = dict(w_ada=m_w_ada, b_ada=m_b_ada, g_pre=m_g_pre, w_in=m_w_in, b_fgate=m_b_fgate, sinks=m_sinks, w_out=m_w_out, g_post=m_g_post)
    vars_ = dict(w_ada=v_w_ada, b_ada=v_b_ada, g_pre=v_g_pre, w_in=v_w_in, b_fgate=v_b_fgate, sinks=v_sinks, w_out=v_w_out, g_post=v_g_post)
    names = ["w_ada", "b_ada", "g_pre", "w_in", "b_fgate", "sinks", "w_out", "g_post"]
    g_out, d_out, m_out, v_out = [], [], [], []
    for n in names:
        if n == "w_in":
            flat = lambda a: jnp.transpose(a, (2, 0, 1)).reshape(W_IN_SHARD * D_MODEL // LANES, LANES)
            unflat = lambda a: jnp.transpose(a.reshape(W_IN_SHARD, 1, D_MODEL), (1, 2, 0))
            outs = _adamw(flat(w_in), flat(grads[n][None]), flat(moms[n]), flat(vars_[n]), "adamw_" + n)
            go, d, nm, nv = (unflat(a) for a in outs)
        else:
            g2 = grads[n].reshape(weights[n].shape[-2:])
            go, d, nm, nv = _adamw(weights[n], g2, moms[n], vars_[n], "adamw_" + n)
        g_out.append(go)
        d_out.append(d)
        m_out.append(nm)
        v_out.append(nv)
    return (loss, grad_x.reshape(x.shape), *g_out, *d_out, *m_out, *v_out)
```
